```python
import math
import jax, jax.numpy as jnp
from jax import lax
import numpy as np

D_MODEL = 1024
BATCH = 8
SEQ = 2048
DEPTH = 2

HEAD_DIM = 64
D_MIX = D_MODEL
SSM_WIDTH = D_MIX // 4
SSM_CH = 16
SSM_GROUPS = SSM_WIDTH // SSM_CH
SSM_STATE = 64
ATTN_HEADS = (D_MIX - SSM_WIDTH) // (2 * HEAD_DIM)
MLA_HEADS = ATTN_HEADS
MLA_NOPE = HEAD_DIM
MLA_ROPE = HEAD_DIM // 2
MLA_V = HEAD_DIM
MLA_Q_RANK = 192
MLA_KV_RANK = 128
NSA_HEADS = ATTN_HEADS
NSA_KV_HEADS = 2
NSA_REP = NSA_HEADS // NSA_KV_HEADS
NSA_DIM = HEAD_DIM
CMP_BLOCK = 32
CMP_STRIDE = 16
SEL_BLOCK = 64
SEL_TOPN = 8
WINDOW = 256
REL_BUCKETS = 32
REL_MAX_DIST = 128
XATTN_HEADS = 4
XATTN_DIM = D_MODEL // XATTN_HEADS
MEM_TOKENS = 256
FF_DENSE = 2816
N_EXPERTS = 8
TOP_K = 2
FF_EXPERT = 3584
Q_BLOCK = 128
ROPE_THETA = 10000.0
EPS = 1e-6
NEG_INF = -1e30
FORCE = 1e9
N_DENSE = (DEPTH + 1) // 2
N_MOE = DEPTH // 2
MLA_WIDTH = MLA_HEADS * MLA_V
NSA_WIDTH = NSA_HEADS * NSA_DIM
IN_SPLITS = (SSM_WIDTH, MLA_Q_RANK, MLA_KV_RANK, MLA_ROPE, NSA_HEADS * NSA_DIM,
             6 * NSA_KV_HEADS * NSA_DIM, 3 * NSA_HEADS)
IN_COLS = sum(IN_SPLITS)

kernel_name = "hymba_s5_mla_nsa_moe_trunk"


def rmsnorm(x, g):
    xf = x.astype(jnp.float32)
    y = xf * lax.rsqrt(jnp.mean(xf * xf, axis=-1, keepdims=True) + EPS)
    return (y * g.astype(jnp.float32)).astype(x.dtype)


def softmax32(s):
    return jax.nn.softmax(s.astype(jnp.float32), axis=-1)


def t5_bucket(dist):
    n = jnp.maximum(dist, 0)
    exact = REL_BUCKETS // 2
    nf = jnp.maximum(n, exact).astype(jnp.float32)
    large = exact + (jnp.log(nf / exact) / math.log(REL_MAX_DIST / exact)
                     * (REL_BUCKETS - exact)).astype(jnp.int32)
    return jnp.where(n < exact, n, jnp.minimum(large, REL_BUCKETS - 1))


def rope_tables(seq, dim):
    pos = jnp.arange(seq, dtype=jnp.float32)
    inv = 1.0 / (ROPE_THETA ** (jnp.arange(0, dim, 2, dtype=jnp.float32) / dim))
    ang = pos[:, None] * inv[None, :]
    return jnp.cos(ang), jnp.sin(ang)


def apply_rope(x, cos, sin):
    half = x.shape[-1] // 2
    xf = x.astype(jnp.float32)
    x1, x2 = xf[..., :half], xf[..., half:]
    return jnp.concatenate([x1 * cos - x2 * sin, x2 * cos + x1 * sin], axis=-1).astype(x.dtype)


def _cplx_combine(e1, e2):
    a1r, a1i, b1r, b1i = e1
    a2r, a2i, b2r, b2i = e2
    ar = a2r * a1r - a2i * a1i
    ai = a2r * a1i + a2i * a1r
    br = a2r * b1r - a2i * b1i + b2r
    bi = a2r * b1i + a2i * b1r + b2i
    return (ar, ai, br, bi)


def ssm_mixer(u, a_re, a_im, b_re, b_im, c_re, c_im, d, log_dt, w_glu):
    f32 = jnp.float32
    bsz, seq, _ = u.shape
    ug = u.astype(f32).reshape(bsz, seq, SSM_GROUPS, SSM_CH)
    dt = jnp.exp(log_dt.astype(f32))[:, None]
    lr, li = a_re.astype(f32), a_im.astype(f32)
    mag = jnp.exp(lr * dt)
    ab_r, ab_i = mag * jnp.cos(li * dt), mag * jnp.sin(li * dt)
    den = lr * lr + li * li
    nr = ab_r - 1.0
    f_r = (nr * lr + ab_i * li) / den
    f_i = (ab_i * lr - nr * li) / den
    br, bi = b_re.astype(f32), b_im.astype(f32)
    bb_r = f_r[..., None] * br - f_i[..., None] * bi
    bb_i = f_r[..., None] * bi + f_i[..., None] * br
    bu_r = jnp.einsum('bsgc,gpc->bsgp', ug, bb_r)
    bu_i = jnp.einsum('bsgc,gpc->bsgp', ug, bb_i)
    a_r = jnp.broadcast_to(ab_r, bu_r.shape)
    a_i = jnp.broadcast_to(ab_i, bu_r.shape)
    _, _, h_r, h_i = lax.associative_scan(_cplx_combine, (a_r, a_i, bu_r, bu_i), axis=1)
    y = (jnp.einsum('bsgp,gcp->bsgc', h_r, c_re.astype(f32))
         - jnp.einsum('bsgp,gcp->bsgc', h_i, c_im.astype(f32))
         + d.astype(f32).reshape(SSM_GROUPS, SSM_CH) * ug)
    y = jax.nn.gelu(y.reshape(bsz, seq, SSM_WIDTH)).astype(u.dtype)
    return y * jax.nn.sigmoid(y @ w_glu)


def mla_mixer(cq, ckv, kr, q_norm, w_uq, kv_norm, w_ukv, cos, sin):
    bsz, seq, _ = cq.shape
    q = (rmsnorm(cq, q_norm) @ w_uq).reshape(bsz, seq, MLA_HEADS, MLA_NOPE + MLA_ROPE)
    q_nope = q[..., :MLA_NOPE]
    q_rope = apply_rope(q[..., MLA_NOPE:], cos[:, None, :], sin[:, None, :])
    kv = (rmsnorm(ckv, kv_norm) @ w_ukv).reshape(bsz, seq, MLA_HEADS, MLA_NOPE + MLA_V)
    k_nope, v = kv[..., :MLA_NOPE], kv[..., MLA_NOPE:]
    k_rope = apply_rope(kr, cos, sin)
    scale = (MLA_NOPE + MLA_ROPE) ** -0.5
    kpos = jnp.arange(seq)

    def block(i):
        qs = i * Q_BLOCK
        qn = lax.dynamic_slice_in_dim(q_nope, qs, Q_BLOCK, axis=1)
        qr = lax.dynamic_slice_in_dim(q_rope, qs, Q_BLOCK, axis=1)
        s = (jnp.einsum('bqhd,bkhd->bhqk', qn, k_nope)
             + jnp.einsum('bqhr,bkr->bhqk', qr, k_rope)).astype(jnp.float32) * scale
        t = qs + jnp.arange(Q_BLOCK)
        s = jnp.where(kpos[None, :] <= t[:, None], s, NEG_INF)
        p = softmax32(s).astype(v.dtype)
        return jnp.einsum('bhqk,bkhd->bqhd', p, v)

    o = lax.map(block, jnp.arange(seq // Q_BLOCK))
    return o.transpose(1, 0, 2, 3, 4).reshape(bsz, seq, MLA_WIDTH)


def nsa_compress(k, pe, w1, w2):
    bsz, seq, g, dh = k.shape
    ratio = CMP_BLOCK // CMP_STRIDE
    n_chunk = seq // CMP_STRIDE
    nbc = n_chunk - ratio + 1
    ch = k.reshape(bsz, n_chunk, CMP_STRIDE, g, dh)
    blocks = jnp.concatenate([ch[:, j:j + nbc] for j in range(ratio)], axis=2)
    blocks = blocks + pe[None, None, :, None, :]
    flat = blocks.transpose(0, 1, 3, 2, 4).reshape(bsz, nbc, g, CMP_BLOCK * dh)
    return jax.nn.gelu(flat @ w1) @ w2


def nsa_mixer(q, kv, gate_logits, cmp_pe, cmp_w1, cmp_w2, rel_bias):
    f32 = jnp.float32
    bsz, seq, _ = q.shape
    G, R, dh = NSA_KV_HEADS, NSA_REP, NSA_DIM
    dt = q.dtype
    q = q.reshape(bsz, seq, G, R, dh)
    kv = kv.reshape(bsz, seq, 6, G, dh)
    k_cr, v_cr, k_sl, v_sl, k_wn, v_wn = [kv[:, :, j] for j in range(6)]
    gates = jax.nn.sigmoid(gate_logits.astype(f32)).reshape(bsz, seq, 3, G, R).astype(dt)
    k_c = nsa_compress(k_cr, cmp_pe[0], cmp_w1[0], cmp_w2[0])
    v_c = nsa_compress(v_cr, cmp_pe[1], cmp_w1[1], cmp_w2[1])
    nbc = k_c.shape[1]
    nbs = seq // SEL_BLOCK
    n_sel = min(SEL_TOPN, nbs)
    c_end = jnp.arange(nbc) * CMP_STRIDE + CMP_BLOCK - 1
    ci = jnp.arange(nbc)[:, None]
    sj = jnp.arange(nbs)[None, :]
    overlap = ((ci * CMP_STRIDE <= sj * SEL_BLOCK + SEL_BLOCK - 1)
               & (ci * CMP_STRIDE + CMP_BLOCK - 1 >= sj * SEL_BLOCK)).astype(f32)
    k_sb = k_sl.reshape(bsz, nbs, SEL_BLOCK, G, dh).transpose(0, 3, 1, 2, 4)
    v_sb = v_sl.reshape(bsz, nbs, SEL_BLOCK, G, dh).transpose(0, 3, 1, 2, 4)
    pad = ((0, 0), (WINDOW, 0), (0, 0), (0, 0))
    k_wp = jnp.pad(k_wn, pad)
    v_wp = jnp.pad(v_wn, pad)
    rb_h = rel_bias.astype(f32)
    rb_grn = rb_h.reshape(REL_BUCKETS, G, R).transpose(1, 2, 0)
    b_ix = jnp.arange(bsz)[:, None, None, None]
    g_ix = jnp.arange(G)[None, :, None, None]
    gi5 = jnp.arange(G)[None, :, None, None, None]
    ri5 = jnp.arange(R)[None, None, :, None, None]
    scale = dh ** -0.5
    span = Q_BLOCK + WINDOW

    def head_bias(dist):
        return rb_h[t5_bucket(dist)].reshape(*dist.shape, G, R).transpose(2, 3, 0, 1)

    def block(i):
        qs = i * Q_BLOCK
        t = qs + jnp.arange(Q_BLOCK)
        qb = lax.dynamic_slice_in_dim(q, qs, Q_BLOCK, axis=1)
        gb = lax.dynamic_slice_in_dim(gates, qs, Q_BLOCK, axis=1)
        dist_c = t[:, None] - c_end[None, :]
        ok_c = dist_c >= 0
        s_c = jnp.einsum('bqgrd,bngd->bgrqn', qb, k_c).astype(f32) * scale + head_bias(dist_c)
        p_c = softmax32(jnp.where(ok_c, s_c, NEG_INF)) * ok_c.any(-1)[:, None].astype(f32)
        o_c = jnp.einsum('bgrqn,bngd->bqgrd', p_c.astype(dt), v_c)
        imp = jnp.einsum('bgrqn,nm->bgqm', p_c, overlap)
        cur = t // SEL_BLOCK
        forced = (sj == 0) | (sj == cur[:, None]) | (sj == cur[:, None] - 1)
        future = sj * SEL_BLOCK > t[:, None]
        imp = jnp.where(forced, FORCE, jnp.where(future, -FORCE, imp))
        _, idx = lax.top_k(imp, n_sel)
        k_g = k_sb[b_ix, g_ix, idx].reshape(bsz, G, Q_BLOCK, n_sel * SEL_BLOCK, dh)
        v_g = v_sb[b_ix, g_ix, idx].reshape(bsz, G, Q_BLOCK, n_sel * SEL_BLOCK, dh)
        kpos = (idx[..., None] * SEL_BLOCK + jnp.arange(SEL_BLOCK)).reshape(bsz, G, Q_BLOCK, n_sel * SEL_BLOCK)
        dist_s = t[:, None] - kpos
        bias_s = rb_grn[gi5, ri5, t5_bucket(dist_s)[:, :, None]]
        s_s = jnp.einsum('bqgrd,bgqmd->bgrqm', qb, k_g).astype(f32) * scale + bias_s
        p_s = softmax32(jnp.where((dist_s >= 0)[:, :, None], s_s, NEG_INF))
        o_s = jnp.einsum('bgrqm,bgqmd->bqgrd', p_s.astype(dt), v_g)
        k_w = lax.dynamic_slice_in_dim(k_wp, qs, span, axis=1)
        v_w = lax.dynamic_slice_in_dim(v_wp, qs, span, axis=1)
        kpos_w = qs - WINDOW + jnp.arange(span)
        dist_w = t[:, None] - kpos_w[None, :]
        ok_w = (dist_w >= 0) & (dist_w < WINDOW) & (kpos_w >= 0)[None, :]
        s_w = jnp.einsum('bqgrd,blgd->bgrql', qb, k_w).astype(f32) * scale + head_bias(dist_w)
        p_w = softmax32(jnp.where(ok_w, s_w, NEG_INF))
        o_w = jnp.einsum('bgrql,blgd->bqgrd', p_w.astype(dt), v_w)
        o = (gb[:, :, 0][..., None] * o_c + gb[:, :, 1][..., None] * o_s
             + gb[:, :, 2][..., None] * o_w)
        return o.reshape(bsz, Q_BLOCK, G * R * dh)

    o = lax.map(block, jnp.arange(seq // Q_BLOCK))
    return o.transpose(1, 0, 2, 3).reshape(bsz, seq, NSA_WIDTH)


def cross_attn(h, mem, g_x, g_mem, wq, wkv, wo):
    bsz, seq, _ = h.shape
    m = mem.shape[1]
    q = (rmsnorm(h, g_x) @ wq).reshape(bsz, seq, XATTN_HEADS, XATTN_DIM)
    kv = (rmsnorm(mem, g_mem) @ wkv).reshape(bsz, m, 2, XATTN_HEADS, XATTN_DIM)
    k, v = kv[:, :, 0], kv[:, :, 1]
    s = jnp.einsum('bshd,bmhd->bhsm', q, k).astype(jnp.float32) * XATTN_DIM ** -0.5
    p = softmax32(s).astype(v.dtype)
    o = jnp.einsum('bhsm,bmhd->bshd', p, v).reshape(bsz, seq, XATTN_HEADS * XATTN_DIM)
    return o @ wo


def swiglu(x, wg, wu, wd):
    return (jax.nn.silu(x @ wg) * (x @ wu)) @ wd


def moe_swiglu(x, router, wg, wu, wd):
    bsz, seq, d = x.shape
    xt = x.reshape(-1, d)
    logits = (xt @ router).astype(jnp.float32)
    top_v, top_i = lax.top_k(logits, TOP_K)
    w = jax.nn.softmax(top_v, axis=-1)
    comb = jnp.sum(jax.nn.one_hot(top_i, N_EXPERTS, dtype=jnp.float32) * w[..., None], axis=1)
    comb = comb.astype(xt.dtype)
    out = jnp.zeros_like(xt)
    for e in range(N_EXPERTS):
        out = out + comb[:, e:e + 1] * swiglu(xt, wg[e], wu[e], wd[e])
    return out.reshape(bsz, seq, d)


def setup_inputs(seed: int = 0) -> dict:
    key = jax.random.key(seed)
    ks = iter(jax.random.split(key, 48))
    f32 = jnp.float32
    L, G, P, C = DEPTH, SSM_GROUPS, SSM_STATE, SSM_CH

    def nrm(shape, scale):
        return jax.random.normal(next(ks), shape, f32) * scale

    def gain(shape):
        return 1.0 + nrm(shape, 0.02)

    return {
        "x": nrm((BATCH, SEQ, D_MODEL), 1.0),
        "mem": nrm((BATCH, MEM_TOKENS, D_MODEL), 1.0),
        "w_in": nrm((L, D_MODEL, IN_COLS), D_MODEL ** -0.5),
        "w_out": nrm((L, D_MIX, D_MODEL), D_MIX ** -0.5),
        "mix_norm": gain((L, D_MODEL)),
        "out_norm": gain((L, D_MIX)),
        "ssm_a_re": -0.5 + nrm((L, G, P), 0.01),
        "ssm_a_im": jnp.pi * jnp.arange(P, dtype=f32) + nrm((L, G, P), 0.01),
        "ssm_b_re": nrm((L, G, P, C), C ** -0.5),
        "ssm_b_im": nrm((L, G, P, C), C ** -0.5),
        "ssm_c_re": nrm((L, G, C, P), P ** -0.5),
        "ssm_c_im": nrm((L, G, C, P), P ** -0.5),
        "ssm_d": nrm((L, SSM_WIDTH), 1.0),
        "ssm_log_dt": jax.random.uniform(next(ks), (L, G), f32, math.log(1e-3), math.log(1e-1)),
        "ssm_w_glu": nrm((L, SSM_WIDTH, SSM_WIDTH), SSM_WIDTH ** -0.5),
        "mla_q_norm": gain((L, MLA_Q_RANK)),
        "mla_w_uq": nrm((L, MLA_Q_RANK, MLA_HEADS * (MLA_NOPE + MLA_ROPE)), MLA_Q_RANK ** -0.5),
        "mla_kv_norm": gain((L, MLA_KV_RANK)),
        "mla_w_ukv": nrm((L, MLA_KV_RANK, MLA_HEADS * (MLA_NOPE + MLA_V)), MLA_KV_RANK ** -0.5),
        "nsa_cmp_pe": nrm((L, 2, CMP_BLOCK, NSA_DIM), 0.1),
        "nsa_cmp_w1": nrm((L, 2, CMP_BLOCK * NSA_DIM, NSA_DIM), (CMP_BLOCK * NSA_DIM) ** -0.5),
        "nsa_cmp_w2": nrm((L, 2, NSA_DIM, NSA_DIM), NSA_DIM ** -0.5),
        "rel_bias": nrm((REL_BUCKETS, NSA_HEADS), 0.5),
        "xattn_norm": gain((L, D_MODEL)),
        "mem_norm": gain((L, D_MODEL)),
        "xattn_wq": nrm((L, D_MODEL, XATTN_HEADS * XATTN_DIM), D_MODEL ** -0.5),
        "xattn_wkv": nrm((L, D_MODEL, 2 * XATTN_HEADS * XATTN_DIM), D_MODEL ** -0.5),
        "xattn_wo": nrm((L, XATTN_HEADS * XATTN_DIM, D_MODEL), D_MODEL ** -0.5),
        "ffn_norm": gain((L, D_MODEL)),
        "dense_w_gate": nrm((N_DENSE, D_MODEL, FF_DENSE), D_MODEL ** -0.5),
        "dense_w_up": nrm((N_DENSE, D_MODEL, FF_DENSE), D_MODEL ** -0.5),
        "dense_w_down": nrm((N_DENSE, FF_DENSE, D_MODEL), FF_DENSE ** -0.5),
        "moe_router": nrm((N_MOE, D_MODEL, N_EXPERTS), D_MODEL ** -0.5),
        "moe_w_gate": nrm((N_MOE, N_EXPERTS, D_MODEL, FF_EXPERT), D_MODEL ** -0.5),
        "moe_w_up": nrm((N_MOE, N_EXPERTS, D_MODEL, FF_EXPERT), D_MODEL ** -0.5),
        "moe_w_down": nrm((N_MOE, N_EXPERTS, FF_EXPERT, D_MODEL), FF_EXPERT ** -0.5),
        "final_norm": gain((D_MODEL,)),
    }


def reference(x, mem, w_in, w_out, mix_norm, out_norm, ssm_a_re, ssm_a_im, ssm_b_re, ssm_b_im,
              ssm_c_re, ssm_c_im, ssm_d, ssm_log_dt, ssm_w_glu, mla_q_norm, mla_w_uq, mla_kv_norm,
              mla_w_ukv, nsa_cmp_pe, nsa_cmp_w1, nsa_cmp_w2, rel_bias, xattn_norm, mem_norm, xattn_wq,
              xattn_wkv, xattn_wo, ffn_norm, dense_w_gate, dense_w_up, dense_w_down, moe_router,
              moe_w_gate, moe_w_up, moe_w_down, final_norm):
    seq = x.shape[1]
    cos, sin = rope_tables(seq, MLA_ROPE)
    offsets = np.cumsum(IN_SPLITS)[:-1].tolist()
    o1 = SSM_WIDTH
    o2 = SSM_WIDTH + MLA_WIDTH
    h = x
    for l in range(DEPTH):
        proj = rmsnorm(h, mix_norm[l]) @ w_in[l]
        u, cq, ckv, kr, nq, nkv, ngate = jnp.split(proj, offsets, axis=-1)
        y_ssm = ssm_mixer(u, ssm_a_re[l], ssm_a_im[l], ssm_b_re[l], ssm_b_im[l], ssm_c_re[l],
                          ssm_c_im[l], ssm_d[l], ssm_log_dt[l], ssm_w_glu[l])
        y_mla = mla_mixer(cq, ckv, kr, mla_q_norm[l], mla_w_uq[l], mla_kv_norm[l], mla_w_ukv[l], cos, sin)
        y_nsa = nsa_mixer(nq, nkv, ngate, nsa_cmp_pe[l], nsa_cmp_w1[l], nsa_cmp_w2[l], rel_bias)
        g = out_norm[l]
        y = jnp.concatenate([rmsnorm(y_ssm, g[:o1]), rmsnorm(y_mla, g[o1:o2]),
                             rmsnorm(y_nsa, g[o2:])], axis=-1)
        h = h + y @ w_out[l]
        h = h + cross_attn(h, mem, xattn_norm[l], mem_norm[l], xattn_wq[l], xattn_wkv[l], xattn_wo[l])
        hn = rmsnorm(h, ffn_norm[l])
        if l % 2 == 0:
            h = h + swiglu(hn, dense_w_gate[l // 2], dense_w_up[l // 2], dense_w_down[l // 2])
        else:
            h = h + moe_swiglu(hn, moe_router[l // 2], moe_w_gate[l // 2], moe_w_up[l // 2],
                               moe_w_down[l // 2])
    return rmsnorm(h, final_norm)
```

```python
import functools
import math

import jax
import jax.numpy as jnp
from jax import lax
from jax.experimental import pallas as pl
from jax.experimental.pallas import tpu as pltpu

F32 = jnp.float32
BF16 = jnp.bfloat16

HEAD_DIM = 64
SSM_WIDTH = 256
SSM_CH = 16
SSM_GROUPS = 16
SSM_STATE = 64
MLA_HEADS = 6
MLA_NOPE = 64
MLA_ROPE = 32
MLA_V = 64
MLA_Q_RANK = 192
MLA_KV_RANK = 128
NSA_HEADS = 6
NSA_KV_HEADS = 2
NSA_REP = 3
NSA_DIM = 64
CMP_BLOCK = 32
CMP_STRIDE = 16
SEL_BLOCK = 64
SEL_TOPN = 8
WINDOW = 256
REL_BUCKETS = 32
REL_MAX_DIST = 128
XATTN_HEADS = 4
N_EXPERTS = 8
ROPE_THETA = 10000.0
EPS = 1e-6
NEG_INF = -1e30
FORCE = 1e9

LANES = 128
VMEM_LIMIT = 56 * 1024 * 1024

C_U, C_CQ, C_KR, C_NQ, C_NKV, C_CKV, C_GATE = 0, 256, 512, 768, 1536, 2304, 2432
IN_COLS_PACKED = 2560


def _cparams(sem):
    return pltpu.CompilerParams(dimension_semantics=sem, vmem_limit_bytes=VMEM_LIMIT)


def _dot(a, b):
    return jnp.dot(a, b, preferred_element_type=F32)


def _dot_t(a, b):
    return lax.dot_general(a, b, (((1,), (1,)), ((), ())), preferred_element_type=F32)


def _rms(x, g, n):
    ms = jnp.sum(x * x, axis=-1, keepdims=True) * (1.0 / n)
    return x * lax.rsqrt(ms + EPS) * g


def _sigmoid(x):
    return 1.0 / (1.0 + jnp.exp(-x))


def _silu(x):
    return x * _sigmoid(x)


def _norm_mm_kernel(x_ref, g_ref, w_ref, o_ref, xn_ref):
    @pl.when(pl.program_id(1) == 0)
    def _():
        x = x_ref[...].astype(F32)
        xn_ref[...] = _rms(x, g_ref[...], x.shape[-1]).astype(BF16)

    o_ref[...] = _dot(xn_ref[...], w_ref[...]).astype(o_ref.dtype)


def norm_matmul(x, g, w, *, tm, tn, out_dtype):
    m, k = x.shape
    n = w.shape[1]
    return pl.pallas_call(
        _norm_mm_kernel,
        out_shape=jax.ShapeDtypeStruct((m, n), out_dtype),
        grid=(m // tm, n // tn),
        in_specs=[pl.BlockSpec((tm, k), lambda i, j: (i, 0)),
                  pl.BlockSpec((1, k), lambda i, j: (0, 0)),
                  pl.BlockSpec((k, tn), lambda i, j: (0, j))],
        out_specs=pl.BlockSpec((tm, tn), lambda i, j: (i, j)),
        scratch_shapes=[pltpu.VMEM((tm, k), BF16)],
        compiler_params=_cparams(("parallel", "arbitrary")),
        name="norm_matmul",
    )(x, g.reshape(1, k).astype(F32), w)


def _ssm_kernel(u_ref, bbr_ref, bbi_ref, ar_ref, ai_ref, ccr_ref, cci_ref, d_ref, wglu_ref,
                o_ref, hr_ref, hi_ref, cr_ref, ci_ref, *, tc, nb):
    @pl.when(pl.program_id(0) == 0)
    def _():
        cr_ref[...] = jnp.zeros_like(cr_ref)
        ci_ref[...] = jnp.zeros_like(ci_ref)

    u = u_ref[...]
    hr_ref[...] = _dot(u, bbr_ref[...])
    hi_ref[...] = _dot(u, bbi_ref[...])
    gp = ar_ref.shape[-1]
    ar = jnp.broadcast_to(ar_ref[...], (nb, gp))
    ai = jnp.broadcast_to(ai_ref[...], (nb, gp))

    def step(t, carry):
        hr, hi = carry
        rows = pl.ds(pl.multiple_of(t * nb, nb), nb)
        nr = ar * hr - ai * hi + hr_ref[rows, :]
        ni = ar * hi + ai * hr + hi_ref[rows, :]
        hr_ref[rows, :] = nr
        hi_ref[rows, :] = ni
        return nr, ni

    hr, hi = lax.fori_loop(0, tc, step, (cr_ref[...], ci_ref[...]))
    cr_ref[...] = hr
    ci_ref[...] = hi
    y = (_dot(hr_ref[...].astype(BF16), ccr_ref[...]) + _dot(hi_ref[...].astype(BF16), cci_ref[...])
         + d_ref[...] * u.astype(F32))
    y = jax.nn.gelu(y)
    z = _dot(y.astype(BF16), wglu_ref[...])
    o_ref[...] = (y * _sigmoid(z)).astype(o_ref.dtype)


def ssm_mixer(u_tm, a_re, a_im, b_re, b_im, c_re, c_im, d, log_dt, w_glu, *, nb, tc):
    rows = u_tm.shape[0]
    G, P, C = SSM_GROUPS, SSM_STATE, SSM_CH
    dt = jnp.exp(log_dt.astype(F32))[:, None]
    lr, li = a_re.astype(F32), a_im.astype(F32)
    mag = jnp.exp(lr * dt)
    ab_r, ab_i = mag * jnp.cos(li * dt), mag * jnp.sin(li * dt)
    den = lr * lr + li * li
    nr = ab_r - 1.0
    f_r = (nr * lr + ab_i * li) / den
    f_i = (ab_i * lr - nr * li) / den
    br, bi = b_re.astype(F32), b_im.astype(F32)
    bb_r = f_r[..., None] * br - f_i[..., None] * bi
    bb_i = f_r[..., None] * bi + f_i[..., None] * br
    eye = jnp.eye(G, dtype=F32)
    bbr = jnp.einsum('gpc,gh->gchp', bb_r, eye).reshape(G * C, G * P).astype(BF16)
    bbi = jnp.einsum('gpc,gh->gchp', bb_i, eye).reshape(G * C, G * P).astype(BF16)
    ccr = jnp.einsum('gcp,gh->gphc', c_re.astype(F32), eye).reshape(G * P, G * C).astype(BF16)
    cci = jnp.einsum('gcp,gh->gphc', -c_im.astype(F32), eye).reshape(G * P, G * C).astype(BF16)
    gp = G * P
    full = lambda shape: pl.BlockSpec(shape, lambda i: (0,) * len(shape))
    return pl.pallas_call(
        functools.partial(_ssm_kernel, tc=tc, nb=nb),
        out_shape=jax.ShapeDtypeStruct((rows, SSM_WIDTH), BF16),
        grid=(rows // (tc * nb),),
        in_specs=[pl.BlockSpec((tc * nb, SSM_WIDTH), lambda i: (i, 0)),
                  full((G * C, gp)), full((G * C, gp)), full((1, gp)), full((1, gp)),
                  full((gp, G * C)), full((gp, G * C)), full((1, SSM_WIDTH)),
                  full((SSM_WIDTH, SSM_WIDTH))],
        out_specs=pl.BlockSpec((tc * nb, SSM_WIDTH), lambda i: (i, 0)),
        scratch_shapes=[pltpu.VMEM((tc * nb, gp), F32), pltpu.VMEM((tc * nb, gp), F32),
                        pltpu.VMEM((nb, gp), F32), pltpu.VMEM((nb, gp), F32)],
        compiler_params=_cparams(("arbitrary",)),
        name="ssm_mixer",
    )(u_tm, bbr, bbi, ab_r.reshape(1, gp), ab_i.reshape(1, gp), ccr, cci,
      d.reshape(1, SSM_WIDTH).astype(F32), w_glu.astype(BF16))


def _mla_prep_kernel(cq_ref, kr_ref, ckv_ref, gq_ref, gkv_ref, wqa_ref, wqb_ref, wk_ref, wv_ref,
                     c1_ref, c0_ref, s0_ref, q_ref, k_ref, v_ref):
    qn = _rms(cq_ref[0].astype(F32), gq_ref[...], MLA_Q_RANK).astype(BF16)
    qa = _dot(qn, wqa_ref[...])
    qb = _dot(qn, wqb_ref[...])
    kn = _rms(ckv_ref[0].astype(F32), gkv_ref[...], MLA_KV_RANK).astype(BF16)
    ka = _dot(kn, wk_ref[...])
    va = _dot(kn, wv_ref[...])
    kr = kr_ref[0].astype(F32)
    c1, c0, s0 = c1_ref[...], c0_ref[...], s0_ref[...]
    krope = kr[:, :LANES] * c0 + kr[:, LANES:] * s0
    for h in range(MLA_HEADS):
        sl = slice(h * LANES, (h + 1) * LANES)
        q_ref[0, h] = (qa[:, sl] * c1 + qb[:, sl] * s0).astype(BF16)
        k_ref[0, h] = (ka[:, sl] + krope).astype(BF16)
        v_ref[0, h] = va[:, sl].astype(BF16)


def _mla_flash_kernel(q_ref, k_ref, v_ref, o_ref, m_ref, l_ref, acc_ref, *, tq):
    qi = pl.program_id(1)
    ki = pl.program_id(2)

    @pl.when(ki == 0)
    def _():
        m_ref[...] = jnp.full_like(m_ref, NEG_INF)
        l_ref[...] = jnp.zeros_like(l_ref)
        acc_ref[...] = jnp.zeros_like(acc_ref)

    lane = lax.broadcasted_iota(jnp.int32, (tq, LANES), 1)

    @pl.when(ki <= qi)
    def _():
        row = qi * tq + lax.broadcasted_iota(jnp.int32, (tq, tq), 0)
        col = ki * tq + lax.broadcasted_iota(jnp.int32, (tq, tq), 1)
        mask = col <= row
        for pr in range(MLA_HEADS // 2):
            alphas, pvs = [], []
            for hh in range(2):
                h = 2 * pr + hh
                s = jnp.where(mask, _dot_t(q_ref[0, h], k_ref[0, h]), NEG_INF)
                m_prev = m_ref[h]
                m_new = jnp.maximum(m_prev, jnp.max(s, axis=-1, keepdims=True))
                alpha = jnp.exp(m_prev - m_new)
                p = jnp.exp(s - m_new)
                l_ref[h] = alpha * l_ref[h] + jnp.sum(p, axis=-1, keepdims=True)
                m_ref[h] = m_new
                alphas.append(alpha)
                pvs.append(_dot(p.astype(BF16), v_ref[0, h]))
            a = jnp.where(lane < MLA_V, alphas[0], alphas[1])
            acc_ref[pr] = acc_ref[pr] * a + pvs[0] + pvs[1]

    @pl.when(ki == qi)
    def _():
        for pr in range(MLA_HEADS // 2):
            linv = jnp.where(lane < MLA_V, 1.0 / l_ref[2 * pr], 1.0 / l_ref[2 * pr + 1])
            o_ref[0, :, pr * LANES:(pr + 1) * LANES] = (acc_ref[pr] * linv).astype(o_ref.dtype)


def _rope_tables(seq):
    pos = jnp.arange(seq, dtype=F32)
    inv = 1.0 / (ROPE_THETA ** (jnp.arange(0, MLA_ROPE, 2, dtype=F32) / MLA_ROPE))
    ang = pos[:, None] * inv[None, :]
    cos, sin = jnp.cos(ang), jnp.sin(ang)
    cos2 = jnp.concatenate([cos, cos], axis=-1)
    sin2 = jnp.concatenate([sin, sin], axis=-1)
    z64 = jnp.zeros((seq, MLA_NOPE), F32)
    z32 = jnp.zeros((seq, LANES - MLA_NOPE - MLA_ROPE), F32)
    c1 = jnp.concatenate([jnp.ones((seq, MLA_NOPE), F32), cos2, z32], axis=-1)
    c0 = jnp.concatenate([z64, cos2, z32], axis=-1)
    s0 = jnp.concatenate([z64, sin2, z32], axis=-1)
    return c1, c0, s0


def _rot_half_cols(w):
    half = MLA_ROPE // 2
    return jnp.concatenate([-w[..., half:], w[..., :half]], axis=-1)


def mla_mixer(proj3, q_norm, w_uq, kv_norm, w_ukv, tabs, *, tm, tq):
    bsz, seq, _ = proj3.shape
    H = MLA_HEADS
    scale = (MLA_NOPE + MLA_ROPE) ** -0.5
    wq = (w_uq.astype(F32) * scale).reshape(MLA_Q_RANK, H, MLA_NOPE + MLA_ROPE)
    zq = jnp.zeros((MLA_Q_RANK, H, LANES - MLA_NOPE - MLA_ROPE), F32)
    z64 = jnp.zeros((MLA_Q_RANK, H, MLA_NOPE), F32)
    wqa = jnp.concatenate([wq, zq], axis=-1).reshape(MLA_Q_RANK, H * LANES)
    wqb = jnp.concatenate([z64, _rot_half_cols(wq[..., MLA_NOPE:]), zq], axis=-1).reshape(MLA_Q_RANK, H * LANES)
    padq = ((0, 256 - MLA_Q_RANK), (0, 0))
    wqa = jnp.pad(wqa, padq).astype(BF16)
    wqb = jnp.pad(wqb, padq).astype(BF16)
    gq = jnp.pad(q_norm.astype(F32), (0, 256 - MLA_Q_RANK)).reshape(1, 256)
    wkv = w_ukv.astype(F32).reshape(MLA_KV_RANK, H, MLA_NOPE + MLA_V)
    zk = jnp.zeros((MLA_KV_RANK, H, MLA_NOPE), F32)
    wk = jnp.concatenate([wkv[..., :MLA_NOPE], zk], axis=-1).reshape(MLA_KV_RANK, H * LANES).astype(BF16)
    wv_h = wkv[..., MLA_NOPE:]
    even = (jnp.arange(H) % 2 == 0)[None, :, None]
    wv = jnp.concatenate([jnp.where(even, wv_h, 0.0), jnp.where(even, 0.0, wv_h)], axis=-1)
    wv = wv.reshape(MLA_KV_RANK, H * LANES).astype(BF16)
    c1, c0, s0 = tabs
    full2 = lambda shape: pl.BlockSpec(shape, lambda b, i: (0,) * len(shape))
    tab_spec = pl.BlockSpec((tm, LANES), lambda b, i: (i, 0))
    hd_spec = pl.BlockSpec((1, H, tm, LANES), lambda b, i: (b, 0, i, 0))
    hd_shape = jax.ShapeDtypeStruct((bsz, H, seq, LANES), BF16)
    q, k, v = pl.pallas_call(
        _mla_prep_kernel,
        out_shape=(hd_shape, hd_shape, hd_shape),
        grid=(bsz, seq // tm),
        in_specs=[pl.BlockSpec((1, tm, 256), lambda b, i: (b, i, C_CQ // 256)),
                  pl.BlockSpec((1, tm, 256), lambda b, i: (b, i, C_KR // 256)),
                  pl.BlockSpec((1, tm, 128), lambda b, i: (b, i, C_CKV // 128)),
                  full2((1, 256)), full2((1, 128)),
                  full2((256, H * LANES)), full2((256, H * LANES)),
                  full2((128, H * LANES)), full2((128, H * LANES)),
                  tab_spec, tab_spec, tab_spec],
        out_specs=(hd_spec, hd_spec, hd_spec),
        compiler_params=_cparams(("parallel", "parallel")),
        name="mla_prep",
    )(proj3, proj3, proj3, gq, kv_norm.astype(F32).reshape(1, 128), wqa, wqb, wk, wv, c1, c0, s0)

    nq = seq // tq
    return pl.pallas_call(
        functools.partial(_mla_flash_kernel, tq=tq),
        out_shape=jax.ShapeDtypeStruct((bsz, seq, H * MLA_V), BF16),
        grid=(bsz, nq, nq),
        in_specs=[pl.BlockSpec((1, H, tq, LANES), lambda b, i, j: (b, 0, i, 0)),
                  pl.BlockSpec((1, H, tq, LANES), lambda b, i, j: (b, 0, jnp.minimum(i, j), 0)),
                  pl.BlockSpec((1, H, tq, LANES), lambda b, i, j: (b, 0, jnp.minimum(i, j), 0))],
        out_specs=pl.BlockSpec((1, tq, H * MLA_V), lambda b, i, j: (b, i, 0)),
        scratch_shapes=[pltpu.VMEM((H, tq, 1), F32), pltpu.VMEM((H, tq, 1), F32),
                        pltpu.VMEM((H // 2, tq, LANES), F32)],
        compiler_params=_cparams(("parallel", "parallel", "arbitrary")),
        name="mla_flash",
    )(q, k, v)


def _nsa_cmp_kernel(x_ref, pea_ref, peb_ref, w1a_ref, w1b_ref, w2_ref, o_ref):
    x = x_ref[0, 0]
    w1a, w1b = w1a_ref[0], w1b_ref[0]
    bias = _dot(pea_ref[0], w1a)[0:1] + _dot(peb_ref[0], w1b)[0:1]
    a = _dot(x, w1a)
    b = _dot(x, w1b)
    n = b.shape[0]
    pre = a + pltpu.roll(b, n - 1, 0) + bias
    o_ref[0, 0] = _dot(jax.nn.gelu(pre).astype(BF16), w2_ref[0]).astype(o_ref.dtype)


def nsa_compress(kv_cr, cmp_pe, cmp_w1, cmp_w2):
    bsz, _, nch, width = kv_cr.shape
    G, dh = NSA_KV_HEADS, NSA_DIM
    half = CMP_BLOCK // 2
    eye = jnp.eye(G, dtype=F32)
    w1r = cmp_w1.astype(F32).reshape(2, CMP_BLOCK, dh, dh)
    w1a = jnp.einsum('kpde,gh->kpgdhe', w1r[:, :half], eye).reshape(2, width, G * dh).astype(BF16)
    w1b = jnp.einsum('kpde,gh->kpgdhe', w1r[:, half:], eye).reshape(2, width, G * dh).astype(BF16)
    w2 = jnp.einsum('kde,gh->kgdhe', cmp_w2.astype(F32), eye).reshape(2, G * dh, G * dh).astype(BF16)
    pe = cmp_pe.astype(F32)
    pe_g = jnp.broadcast_to(pe[:, :, None, :], (2, CMP_BLOCK, G, dh))
    pea = jnp.broadcast_to(pe_g[:, :half].reshape(2, 1, width), (2, 8, width)).astype(BF16)
    peb = jnp.broadcast_to(pe_g[:, half:].reshape(2, 1, width), (2, 8, width)).astype(BF16)
    kvspec = lambda shape: pl.BlockSpec(shape, lambda b, k: (k,) + (0,) * (len(shape) - 1))
    return pl.pallas_call(
        _nsa_cmp_kernel,
        out_shape=jax.ShapeDtypeStruct((bsz, 2, nch, G * dh), BF16),
        grid=(bsz, 2),
        in_specs=[pl.BlockSpec((1, 1, nch, width), lambda b, k: (b, k, 0, 0)),
                  kvspec((1, 8, width)), kvspec((1, 8, width)),
                  kvspec((1, width, G * dh)), kvspec((1, width, G * dh)),
                  kvspec((1, G * dh, G * dh))],
        out_specs=pl.BlockSpec((1, 1, nch, G * dh), lambda b, k: (b, k, 0, 0)),
        compiler_params=_cparams(("parallel", "parallel")),
        name="nsa_compress",
    )(kv_cr, pea, peb, w1a, w1b, w2)


def _nsa_kernel(q_ref, gate_ref, ksl_ref, vsl_ref, kwn_ref, vwn_ref, kvc_ref, bc_ref, bw_ref, bs_ref,
                bfar_ref, ov_ref, o_ref, m_ref, l_ref, acc_ref, *, tq, nbs, n_sel):
    qi = pl.program_id(1)
    R = NSA_REP
    lane = lax.broadcasted_iota(jnp.int32, (tq, LANES), 1)
    t = qi * tq + lax.broadcasted_iota(jnp.int32, (tq, LANES), 0)
    t3 = qi * tq + lax.broadcasted_iota(jnp.int32, (R * tq, 1), 0) % tq
    gs = _sigmoid(gate_ref[0].astype(F32))
    kc = kvc_ref[0, 0]
    vc = kvc_ref[0, 1]
    ov = ov_ref[...]
    jj = lax.broadcasted_iota(jnp.int32, (LANES, LANES), 0)
    cblk = lax.broadcasted_iota(jnp.int32, (LANES, LANES), 1) // SEL_BLOCK

    def stack(fn):
        return jnp.concatenate([fn(r) for r in range(R)], axis=0)

    def reset():
        m_ref[...] = jnp.full_like(m_ref, NEG_INF)
        l_ref[...] = jnp.zeros_like(l_ref)
        acc_ref[...] = jnp.zeros_like(acc_ref)

    def online(s, v):
        m_prev = m_ref[...]
        m_new = jnp.maximum(m_prev, jnp.max(s, axis=-1, keepdims=True))
        alpha = jnp.exp(m_prev - m_new)
        p = jnp.exp(s - m_new)
        l_ref[...] = alpha * l_ref[...] + jnp.sum(p, axis=-1, keepdims=True)
        m_ref[...] = m_new
        acc_ref[...] = acc_ref[...] * alpha + _dot(p.astype(BF16), v)

    res = []
    for g in range(NSA_KV_HEADS):
        qg = stack(lambda r: q_ref[0, :, (g * R + r) * LANES:(g * R + r + 1) * LANES])

        s = _dot_t(qg, kc) + stack(lambda r: bc_ref[g * R + r])
        valid = t3 >= (CMP_BLOCK - 1)
        m = jnp.max(s, axis=-1, keepdims=True)
        p = jnp.where(valid, jnp.exp(s - m), 0.0)
        l = jnp.where(valid, jnp.sum(p, axis=-1, keepdims=True), 1.0)
        pc = p / l
        o_c = _dot(pc.astype(BF16), vc)

        psum = pc[0:tq] + pc[tq:2 * tq] + pc[2 * tq:3 * tq]
        p_hi = psum.astype(BF16)
        p_lo = (psum - p_hi.astype(F32)).astype(BF16)
        imp = _dot(p_hi, ov) + _dot(p_lo, ov)
        cur = t // SEL_BLOCK
        forced = (lane == 0) | (lane == cur) | (lane == cur - 1)
        future = lane * SEL_BLOCK > t
        imp = jnp.where(forced, FORCE, jnp.where(future, -FORCE, imp))
        imp = jnp.where(lane < nbs, imp, -2.0 * FORCE)
        rank = jnp.zeros((tq, LANES), F32)
        for i in range(nbs):
            ci = imp[:, i:i + 1]
            beats = (ci > imp) | ((ci == imp) & (lane > i))
            rank = rank + jnp.where(beats, 1.0, 0.0)
        selb = jnp.where(rank < n_sel, 1.0, 0.0).astype(BF16)

        def sel_tile(kt, bias, extra):
            rows = pl.ds(pl.multiple_of(kt * LANES, LANES), LANES)
            e = jnp.where(jj == 2 * kt + cblk, 1.0, 0.0).astype(BF16)
            madd = (_dot(selb, e) - 1.0) * (-NEG_INF) + extra
            s = _dot_t(qg, ksl_ref[0, rows, :]) + bias + jnp.concatenate([madd] * R, axis=0)
            online(s, vsl_ref[0, rows, :])

        reset()
        far = stack(lambda r: bfar_ref[g * R + r])

        def far_body(kt, c):
            sel_tile(kt, far, 0.0)
            return c

        lax.fori_loop(0, jnp.maximum(qi - 1, 0), far_body, 0)
        sel_tile(jnp.maximum(qi - 1, 0), stack(lambda r: bs_ref[g * R + r, 1]),
                 jnp.where(qi >= 1, 0.0, NEG_INF))
        sel_tile(qi, stack(lambda r: bs_ref[g * R + r, 0]), 0.0)
        o_s = acc_ref[...] / l_ref[...]

        reset()
        for d in (2, 1, 0):
            kt = jnp.maximum(qi - d, 0)
            rows = pl.ds(pl.multiple_of(kt * LANES, LANES), LANES)
            bias = stack(lambda r: bw_ref[g * R + r, d]) + jnp.where(qi >= d, 0.0, NEG_INF)
            online(_dot_t(qg, kwn_ref[0, rows, :]) + bias, vwn_ref[0, rows, :])
        o_w = acc_ref[...] / l_ref[...]

        slabs = []
        for r in range(R):
            h = g * R + r
            rs = slice(r * tq, (r + 1) * tq)
            slabs.append(gs[:, h:h + 1] * o_c[rs] + gs[:, NSA_HEADS + h:NSA_HEADS + h + 1] * o_s[rs]
                         + gs[:, 2 * NSA_HEADS + h:2 * NSA_HEADS + h + 1] * o_w[rs])
        res.append(slabs)

    for r in range(R):
        o_ref[0, :, r * LANES:(r + 1) * LANES] = jnp.where(lane < NSA_DIM, res[0][r], res[1][r]).astype(o_ref.dtype)


def _t5_bucket(dist):
    n = jnp.maximum(dist, 0)
    exact = REL_BUCKETS // 2
    nf = jnp.maximum(n, exact).astype(F32)
    large = exact + (jnp.log(nf / exact) / math.log(REL_MAX_DIST / exact)
                     * (REL_BUCKETS - exact)).astype(jnp.int32)
    return jnp.where(n < exact, n, jnp.minimum(large, REL_BUCKETS - 1))


def _nsa_bias_tables(rel_bias, seq, tq):
    rb = rel_bias.astype(F32).T
    i = jnp.arange(tq)[:, None]
    j = jnp.arange(tq)[None, :]

    def by_dist(dist, ok):
        return jnp.where(ok[None], rb[:, _t5_bucket(dist)], NEG_INF)

    t = jnp.arange(seq)[:, None]
    dist_c = t - (jnp.arange(LANES)[None, :] * CMP_STRIDE + CMP_BLOCK - 1)
    bc = by_dist(dist_c, dist_c >= 0)
    bw = jnp.stack([by_dist(tq * d + i - j, (tq * d + i - j >= 0) & (tq * d + i - j < WINDOW))
                    for d in range(3)], axis=1)
    bs = jnp.stack([by_dist(tq * d + i - j, tq * d + i - j >= 0) for d in range(2)], axis=1)
    bfar = jnp.broadcast_to(rb[:, REL_BUCKETS - 1][:, None, None], (NSA_HEADS, tq, LANES))
    ci = jnp.arange(LANES)[:, None]
    sj = jnp.arange(LANES)[None, :]
    nbs = seq // SEL_BLOCK
    ov = ((ci * CMP_STRIDE <= sj * SEL_BLOCK + SEL_BLOCK - 1)
          & (ci * CMP_STRIDE + CMP_BLOCK - 1 >= sj * SEL_BLOCK)
          & (ci < seq // CMP_STRIDE - 1) & (sj < nbs))
    return bc, bw, bs, bfar, ov.astype(BF16)


def nsa_mixer(proj3, kvc, tables, *, tq):
    bsz, seq, _ = proj3.shape
    assert tq == LANES and seq // CMP_STRIDE == LANES and tq * 2 == WINDOW
    bc, bw, bs, bfar, ov = tables
    nbs = seq // SEL_BLOCK
    H = NSA_HEADS
    slab = lambda j: pl.BlockSpec((1, seq, LANES), lambda b, i: (b, 0, C_NKV // LANES + j))
    const = lambda shape: pl.BlockSpec(shape, lambda b, i: (0,) * len(shape))
    return pl.pallas_call(
        functools.partial(_nsa_kernel, tq=tq, nbs=nbs, n_sel=min(SEL_TOPN, nbs)),
        out_shape=jax.ShapeDtypeStruct((bsz, seq, H * NSA_DIM), BF16),
        grid=(bsz, seq // tq),
        in_specs=[pl.BlockSpec((1, tq, H * LANES), lambda b, i: (b, i, C_NQ // (H * LANES))),
                  pl.BlockSpec((1, tq, LANES), lambda b, i: (b, i, C_GATE // LANES)),
                  slab(2), slab(3), slab(4), slab(5),
                  pl.BlockSpec((1, 2, LANES, LANES), lambda b, i: (b, 0, 0, 0)),
                  pl.BlockSpec((H, tq, LANES), lambda b, i: (0, i, 0)),
                  const((H, 3, tq, tq)), const((H, 2, tq, tq)), const((H, tq, LANES)),
                  const((LANES, LANES))],
        out_specs=pl.BlockSpec((1, tq, H * NSA_DIM), lambda b, i: (b, i, 0)),
        scratch_shapes=[pltpu.VMEM((NSA_REP * tq, 1), F32), pltpu.VMEM((NSA_REP * tq, 1), F32),
                        pltpu.VMEM((NSA_REP * tq, LANES), F32)],
        compiler_params=_cparams(("parallel", "arbitrary")),
        name="nsa_attention",
    )(proj3, proj3, proj3, proj3, proj3, proj3, kvc, bc, bw, bs, bfar, ov)


def _outproj_kernel(ys_ref, ym_ref, yn_ref, h_ref, g1_ref, g2_ref, g3_ref, w1_ref, w2_ref, w3_ref, o_ref):
    def part(y_ref, g_ref, w_ref):
        y = y_ref[...].astype(F32)
        return _dot(_rms(y, g_ref[...], y.shape[-1]).astype(BF16), w_ref[...])

    o_ref[...] = h_ref[...] + part(ys_ref, g1_ref, w1_ref) + part(ym_ref, g2_ref, w2_ref) + part(yn_ref, g3_ref, w3_ref)


def out_projection(y_ssm, y_mla, y_nsa, h, gains, weights, *, tm):
    m, d = h.shape
    row = lambda w: pl.BlockSpec((tm, w), lambda i: (i, 0))
    full = lambda a: pl.BlockSpec(a.shape, lambda i: (0, 0))
    gains = [g.reshape(1, -1).astype(F32) for g in gains]
    return pl.pallas_call(
        _outproj_kernel,
        out_shape=jax.ShapeDtypeStruct((m, d), F32),
        grid=(m // tm,),
        in_specs=[row(y_ssm.shape[1]), row(y_mla.shape[1]), row(y_nsa.shape[1]), row(d)]
                 + [full(g) for g in gains] + [full(w) for w in weights],
        out_specs=row(d),
        compiler_params=_cparams(("parallel",)),
        name="out_projection",
    )(y_ssm, y_mla, y_nsa, h, *gains, *weights)


def _xattn_kernel(h_ref, kv_ref, g_ref, wq_ref, wo_ref, o_ref, *, dh):
    h = h_ref[0]
    xn = _rms(h, g_ref[...], h.shape[-1]).astype(BF16)
    q = _dot(xn, wq_ref[...]).astype(BF16)
    hw = XATTN_HEADS * dh
    outs = []
    for hd in range(XATTN_HEADS):
        k = kv_ref[0, :, hd * dh:(hd + 1) * dh]
        v = kv_ref[0, :, hw + hd * dh:hw + (hd + 1) * dh]
        s = _dot_t(q[:, hd * dh:(hd + 1) * dh], k)
        p = jnp.exp(s - jnp.max(s, axis=-1, keepdims=True))
        p = p / jnp.sum(p, axis=-1, keepdims=True)
        outs.append(_dot(p.astype(BF16), v).astype(BF16))
    o = jnp.concatenate(outs, axis=-1)
    o_ref[0] = h + _dot(o, wo_ref[...])


def cross_attention(h3, kv3, g_x, wq, wo, *, tm):
    bsz, seq, d = h3.shape
    m = kv3.shape[1]
    dh = d // XATTN_HEADS
    wq_s = (wq.astype(F32) * dh ** -0.5).astype(BF16)
    const = lambda shape: pl.BlockSpec(shape, lambda b, i: (0,) * len(shape))
    return pl.pallas_call(
        functools.partial(_xattn_kernel, dh=dh),
        out_shape=jax.ShapeDtypeStruct((bsz, seq, d), F32),
        grid=(bsz, seq // tm),
        in_specs=[pl.BlockSpec((1, tm, d), lambda b, i: (b, i, 0)),
                  pl.BlockSpec((1, m, 2 * d), lambda b, i: (b, 0, 0)),
                  const((1, d)), const((d, d)), const((d, d))],
        out_specs=pl.BlockSpec((1, tm, d), lambda b, i: (b, i, 0)),
        compiler_params=_cparams(("parallel", "parallel")),
        name="cross_attention",
    )(h3, kv3, g_x.reshape(1, d).astype(F32), wq_s, wo.astype(BF16))


def _ffn_kernel(h_ref, g_ref, wg_ref, wu_ref, wd_ref, o_ref, xn_ref, acc_ref):
    j = pl.program_id(1)

    @pl.when(j == 0)
    def _():
        h = h_ref[...]
        xn_ref[...] = _rms(h, g_ref[...], h.shape[-1]).astype(BF16)
        acc_ref[...] = h

    xn = xn_ref[...]
    a = _silu(_dot(xn, wg_ref[...])) * _dot(xn, wu_ref[...])
    acc_ref[...] += _dot(a.astype(BF16), wd_ref[...])

    @pl.when(j == pl.num_programs(1) - 1)
    def _():
        o_ref[...] = acc_ref[...]


def dense_ffn(h, g, wg, wu, wd, *, tm, tf):
    m, d = h.shape
    ff = wg.shape[1]
    return pl.pallas_call(
        _ffn_kernel,
        out_shape=jax.ShapeDtypeStruct((m, d), F32),
        grid=(m // tm, ff // tf),
        in_specs=[pl.BlockSpec((tm, d), lambda i, j: (i, 0)),
                  pl.BlockSpec((1, d), lambda i, j: (0, 0)),
                  pl.BlockSpec((d, tf), lambda i, j: (0, j)),
                  pl.BlockSpec((d, tf), lambda i, j: (0, j)),
                  pl.BlockSpec((tf, d), lambda i, j: (j, 0))],
        out_specs=pl.BlockSpec((tm, d), lambda i, j: (i, 0)),
        scratch_shapes=[pltpu.VMEM((tm, d), BF16), pltpu.VMEM((tm, d), F32)],
        compiler_params=_cparams(("parallel", "arbitrary")),
        name="dense_ffn",
    )(h, g.reshape(1, d).astype(F32), wg.astype(BF16), wu.astype(BF16), wd.astype(BF16))


def _router_kernel(h_ref, g_ref, wr_hi_ref, wr_lo_ref, xn_ref, info_ref, cnt_ref, carry_ref, *, tm):
    i = pl.program_id(0)

    @pl.when(i == 0)
    def _():
        carry_ref[...] = jnp.zeros_like(carry_ref)

    h = h_ref[...]
    xn = _rms(h, g_ref[...], h.shape[-1])
    xn_ref[...] = xn
    x_hi = xn.astype(BF16)
    x_lo = (xn - x_hi.astype(F32)).astype(BF16)
    logits = _dot(x_hi, wr_hi_ref[...]) + _dot(x_lo, wr_hi_ref[...]) + _dot(x_hi, wr_lo_ref[...])
    lane = lax.broadcasted_iota(jnp.int32, (tm, LANES), 1)
    lanef = lane.astype(F32)
    logits = jnp.where(lane < N_EXPERTS, logits, NEG_INF)
    m1 = jnp.max(logits, axis=-1, keepdims=True)
    i1 = jnp.min(jnp.where(logits == m1, lanef, float(LANES)), axis=-1, keepdims=True)
    rest = jnp.where(lanef == i1, NEG_INF, logits)
    m2 = jnp.max(rest, axis=-1, keepdims=True)
    i2 = jnp.min(jnp.where(rest == m2, lanef, float(LANES)), axis=-1, keepdims=True)
    e2 = jnp.exp(m2 - m1)
    w1 = 1.0 / (1.0 + e2)
    w2 = e2 / (1.0 + e2)
    oh1 = lanef == i1
    oh2 = lanef == i2
    oh = jnp.where(oh1 | oh2, 1.0, 0.0)
    rr = lax.broadcasted_iota(jnp.int32, (tm, tm), 0)
    cc = lax.broadcasted_iota(jnp.int32, (tm, tm), 1)
    tri = jnp.where(cc < rr, 1.0, 0.0).astype(BF16)
    before = _dot(tri, oh.astype(BF16)) + carry_ref[0:1, :]
    r1 = jnp.sum(jnp.where(oh1, before, 0.0), axis=-1, keepdims=True)
    r2 = jnp.sum(jnp.where(oh2, before, 0.0), axis=-1, keepdims=True)
    carry_ref[...] = carry_ref[...] + jnp.sum(oh, axis=0, keepdims=True)
    info = jnp.where(lane == 0, i1, jnp.where(lane == 1, i2, jnp.where(lane == 2, w1, jnp.where(
        lane == 3, w2, jnp.where(lane == 4, r1, jnp.where(lane == 5, r2, 0.0))))))
    info_ref[...] = info
    cnt_ref[...] = carry_ref[...]


def moe_router(h, g, router, *, tm):
    m, d = h.shape
    wr = jnp.pad(router.astype(F32), ((0, 0), (0, LANES - N_EXPERTS)))
    wr_hi = wr.astype(BF16)
    wr_lo = (wr - wr_hi.astype(F32)).astype(BF16)
    return pl.pallas_call(
        functools.partial(_router_kernel, tm=tm),
        out_shape=(jax.ShapeDtypeStruct((m, d), F32), jax.ShapeDtypeStruct((m, LANES), F32),
                   jax.ShapeDtypeStruct((8, LANES), F32)),
        grid=(m // tm,),
        in_specs=[pl.BlockSpec((tm, d), lambda i: (i, 0)),
                  pl.BlockSpec((1, d), lambda i: (0, 0)),
                  pl.BlockSpec((d, LANES), lambda i: (0, 0)),
                  pl.BlockSpec((d, LANES), lambda i: (0, 0))],
        out_specs=(pl.BlockSpec((tm, d), lambda i: (i, 0)),
                   pl.BlockSpec((tm, LANES), lambda i: (i, 0)),
                   pl.BlockSpec((8, LANES), lambda i: (0, 0))),
        scratch_shapes=[pltpu.VMEM((8, LANES), F32)],
        compiler_params=_cparams(("arbitrary",)),
        name="moe_router",
    )(h, g.reshape(1, d).astype(F32), wr_hi, wr_lo)


def _moe_ffn_kernel(src_ref, texp_ref, nact_ref, x_hbm, ws_ref, wg_ref, wu_ref, wd_ref, o_ref,
                    xbuf, xbf, acc_ref, sem, *, tm):
    i = pl.program_id(0)
    j = pl.program_id(1)
    nj = pl.num_programs(1)
    active = i < nact_ref[0]

    @pl.when(active & (j == 0))
    def _():
        def issue(s, c):
            tok = src_ref[i * tm + s]
            pltpu.make_async_copy(x_hbm.at[pl.ds(tok, 1), :], xbuf.at[pl.ds(s, 1), :], sem).start()
            return c

        lax.fori_loop(0, tm, issue, 0)

        def wait(s, c):
            pltpu.make_async_copy(x_hbm.at[pl.ds(0, 1), :], xbuf.at[pl.ds(s, 1), :], sem).wait()
            return c

        lax.fori_loop(0, tm, wait, 0)
        xbf[...] = xbuf[...].astype(BF16)
        acc_ref[...] = jnp.zeros_like(acc_ref)

    @pl.when(active)
    def _():
        x = xbf[...]
        a = _silu(_dot(x, wg_ref[0])) * _dot(x, wu_ref[0])
        acc_ref[...] += _dot(a.astype(BF16), wd_ref[0])

    @pl.when(j == nj - 1)
    def _():
        o_ref[...] = jnp.where(active, acc_ref[...] * ws_ref[...], 0.0)


def moe_expert_ffn(xn, src, tile_expert, n_active, w_slot, wg, wu, wd, *, tm, tf):
    n_slots = src.shape[0]
    d = xn.shape[1]
    ff = wg.shape[2]
    nj = ff // tf

    def wmap_col(i, j, src, texp, nact):
        return (texp[i], 0, jnp.where(i < nact[0], j, nj - 1))

    def wmap_row(i, j, src, texp, nact):
        return (texp[i], jnp.where(i < nact[0], j, nj - 1), 0)

    return pl.pallas_call(
        functools.partial(_moe_ffn_kernel, tm=tm),
        out_shape=jax.ShapeDtypeStruct((n_slots, d), F32),
        grid_spec=pltpu.PrefetchScalarGridSpec(
            num_scalar_prefetch=3,
            grid=(n_slots // tm, nj),
            in_specs=[pl.BlockSpec(memory_space=pl.ANY),
                      pl.BlockSpec((tm, 1), lambda i, j, *_: (i, 0)),
                      pl.BlockSpec((1, d, tf), wmap_col),
                      pl.BlockSpec((1, d, tf), wmap_col),
                      pl.BlockSpec((1, tf, d), wmap_row)],
            out_specs=pl.BlockSpec((tm, d), lambda i, j, *_: (i, 0)),
            scratch_shapes=[pltpu.VMEM((tm, d), F32), pltpu.VMEM((tm, d), BF16),
                            pltpu.VMEM((tm, d), F32), pltpu.SemaphoreType.DMA(())]),
        compiler_params=_cparams(("arbitrary", "arbitrary")),
        name="moe_expert_ffn",
    )(src, tile_expert, n_active, xn, w_slot, wg, wu, wd)


def _moe_combine_kernel(pos_ref, h_ref, ys_hbm, g_ref, o_ref, buf, sem, *, tm, final_norm):
    i = pl.program_id(0)

    def issue(s, c):
        for k in range(2):
            p = pos_ref[2 * (i * tm + s) + k]
            pltpu.make_async_copy(ys_hbm.at[pl.ds(p, 1), :], buf.at[k, pl.ds(s, 1), :], sem).start()
        return c

    lax.fori_loop(0, tm, issue, 0)

    def wait(s, c):
        for k in range(2):
            pltpu.make_async_copy(ys_hbm.at[pl.ds(0, 1), :], buf.at[k, pl.ds(s, 1), :], sem).wait()
        return c

    lax.fori_loop(0, tm, wait, 0)
    y = h_ref[...] + buf[0] + buf[1]
    if final_norm:
        y = _rms(y, g_ref[...], y.shape[-1])
    o_ref[...] = y


def moe_combine(h, ys, pos_flat, g_final, *, tm, final_norm):
    m, d = h.shape
    return pl.pallas_call(
        functools.partial(_moe_combine_kernel, tm=tm, final_norm=final_norm),
        out_shape=jax.ShapeDtypeStruct((m, d), F32),
        grid_spec=pltpu.PrefetchScalarGridSpec(
            num_scalar_prefetch=1,
            grid=(m // tm,),
            in_specs=[pl.BlockSpec((tm, d), lambda i, *_: (i, 0)),
                      pl.BlockSpec(memory_space=pl.ANY),
                      pl.BlockSpec((1, d), lambda i, *_: (0, 0))],
            out_specs=pl.BlockSpec((tm, d), lambda i, *_: (i, 0)),
            scratch_shapes=[pltpu.VMEM((2, tm, d), F32), pltpu.SemaphoreType.DMA(())]),
        compiler_params=_cparams(("arbitrary",)),
        name="moe_combine",
    )(pos_flat, h, ys, g_final.reshape(1, d).astype(F32))


def moe_layer(h, g, router, wg, wu, wd, g_final, *, final_norm, tm_r=512, tm_g=512, tf=512, tm_c=256):
    m, d = h.shape
    xn, info, cnt = moe_router(h, g, router, tm=tm_r)
    e_idx = info[:, 0:2].astype(jnp.int32)
    gate_w = info[:, 2:4]
    rank = info[:, 4:6].astype(jnp.int32)
    counts = cnt[0, :N_EXPERTS].astype(jnp.int32)
    tiles_per = (counts + tm_g - 1) // tm_g
    tile_end = jnp.cumsum(tiles_per)
    seg_start = (tile_end - tiles_per) * tm_g
    pos = seg_start[e_idx] + rank
    n_tiles = (2 * m) // tm_g + N_EXPERTS
    n_slots = n_tiles * tm_g
    tok = jnp.broadcast_to(jnp.arange(m, dtype=jnp.int32)[:, None], (m, 2))
    src = jnp.zeros((n_slots,), jnp.int32).at[pos.reshape(-1)].set(tok.reshape(-1))
    w_slot = jnp.zeros((n_slots,), F32).at[pos.reshape(-1)].set(gate_w.reshape(-1)).reshape(n_slots, 1)
    n_active = tile_end[-1:].astype(jnp.int32)
    tile_ids = jnp.minimum(jnp.arange(n_tiles, dtype=jnp.int32), n_active[0] - 1)
    tile_expert = jnp.sum(tile_ids[:, None] >= tile_end[None, :], axis=1).astype(jnp.int32)
    ys = moe_expert_ffn(xn, src, tile_expert, n_active, w_slot,
                        wg.astype(BF16), wu.astype(BF16), wd.astype(BF16), tm=tm_g, tf=tf)
    return moe_combine(h, ys, pos.reshape(-1).astype(jnp.int32), g_final, tm=tm_c, final_norm=final_norm)


def _final_norm_kernel(h_ref, g_ref, o_ref):
    h = h_ref[...]
    o_ref[...] = _rms(h, g_ref[...], h.shape[-1])


def final_rmsnorm(h, g, *, tm):
    m, d = h.shape
    return pl.pallas_call(
        _final_norm_kernel,
        out_shape=jax.ShapeDtypeStruct((m, d), F32),
        grid=(m // tm,),
        in_specs=[pl.BlockSpec((tm, d), lambda i: (i, 0)), pl.BlockSpec((1, d), lambda i: (0, 0))],
        out_specs=pl.BlockSpec((tm, d), lambda i: (i, 0)),
        compiler_params=_cparams(("parallel",)),
        name="final_rmsnorm",
    )(h, g.reshape(1, d).astype(F32))


def _pack_w_in(w):
    d = w.shape[0]
    w = w.astype(F32)
    o = 0
    u = w[:, o:o + SSM_WIDTH]; o += SSM_WIDTH
    cq = w[:, o:o + MLA_Q_RANK]; o += MLA_Q_RANK
    ckv = w[:, o:o + MLA_KV_RANK]; o += MLA_KV_RANK
    kr = w[:, o:o + MLA_ROPE]; o += MLA_ROPE
    nq = w[:, o:o + NSA_HEADS * NSA_DIM]; o += NSA_HEADS * NSA_DIM
    nkv = w[:, o:o + 6 * NSA_KV_HEADS * NSA_DIM]; o += 6 * NSA_KV_HEADS * NSA_DIM
    gate = w[:, o:o + 3 * NSA_HEADS]
    z = lambda n: jnp.zeros((d, n), F32)
    kr_a = jnp.concatenate([z(MLA_NOPE), kr, z(LANES - MLA_NOPE - MLA_ROPE)], axis=1)
    kr_b = jnp.concatenate([z(MLA_NOPE), _rot_half_cols(kr), z(LANES - MLA_NOPE - MLA_ROPE)], axis=1)
    nq_h = (nq * NSA_DIM ** -0.5).reshape(d, NSA_KV_HEADS, NSA_REP, NSA_DIM)
    zq = jnp.zeros((d, NSA_REP, NSA_DIM), F32)
    nq_p = jnp.concatenate([
        jnp.concatenate([nq_h[:, 0], zq], axis=-1).reshape(d, NSA_REP * LANES),
        jnp.concatenate([zq, nq_h[:, 1]], axis=-1).reshape(d, NSA_REP * LANES)], axis=1)
    packed = jnp.concatenate([u, cq, z(256 - MLA_Q_RANK), kr_a, kr_b, nq_p, nkv, ckv,
                              gate, z(LANES - 3 * NSA_HEADS)], axis=1)
    assert packed.shape[1] == IN_COLS_PACKED
    return packed.astype(BF16)


def kernel(x, mem, w_in, w_out, mix_norm, out_norm, ssm_a_re, ssm_a_im, ssm_b_re, ssm_b_im, ssm_c_re, ssm_c_im, ssm_d, ssm_log_dt, ssm_w_glu, mla_q_norm, mla_w_uq, mla_kv_norm, mla_w_ukv, nsa_cmp_pe, nsa_cmp_w1, nsa_cmp_w2, rel_bias, xattn_norm, mem_norm, xattn_wq, xattn_wkv, xattn_wo, ffn_norm, dense_w_gate, dense_w_up, dense_w_down, moe_router, moe_w_gate, moe_w_up, moe_w_down, final_norm):
    bsz, seq, d = x.shape
    depth = w_in.shape[0]
    T = bsz * seq
    nmem = mem.shape[1]
    tq_nsa = LANES
    rope_tabs = _rope_tables(seq)
    nsa_tabs = _nsa_bias_tables(rel_bias, seq, tq_nsa)
    o1, o2 = SSM_WIDTH, SSM_WIDTH + MLA_HEADS * MLA_V
    perm = jnp.arange(NSA_HEADS * NSA_DIM).reshape(NSA_KV_HEADS, NSA_REP, NSA_DIM).transpose(1, 0, 2).reshape(-1)
    mem2 = mem.reshape(bsz * nmem, d)
    h = x.reshape(T, d)
    for l in range(depth):
        proj = norm_matmul(h, mix_norm[l], _pack_w_in(w_in[l]), tm=1024, tn=512, out_dtype=BF16)
        proj3 = proj.reshape(bsz, seq, IN_COLS_PACKED)
        u_tm = proj3[:, :, C_U:C_U + SSM_WIDTH].transpose(1, 0, 2).reshape(seq * bsz, SSM_WIDTH)
        y_ssm = ssm_mixer(u_tm, ssm_a_re[l], ssm_a_im[l], ssm_b_re[l], ssm_b_im[l], ssm_c_re[l], ssm_c_im[l],
                          ssm_d[l], ssm_log_dt[l], ssm_w_glu[l], nb=bsz, tc=64)
        y_ssm = y_ssm.reshape(seq, bsz, SSM_WIDTH).transpose(1, 0, 2).reshape(T, SSM_WIDTH)
        y_mla = mla_mixer(proj3, mla_q_norm[l], mla_w_uq[l], mla_kv_norm[l], mla_w_ukv[l], rope_tabs,
                          tm=512, tq=256).reshape(T, -1)
        nch = seq // CMP_STRIDE
        kv_cr = jnp.stack([proj3[:, :, C_NKV:C_NKV + LANES].reshape(bsz, nch, CMP_STRIDE * LANES),
                           proj3[:, :, C_NKV + LANES:C_NKV + 2 * LANES].reshape(bsz, nch, CMP_STRIDE * LANES)],
                          axis=1)
        kvc = nsa_compress(kv_cr, nsa_cmp_pe[l], nsa_cmp_w1[l], nsa_cmp_w2[l])
        y_nsa = nsa_mixer(proj3, kvc, nsa_tabs, tq=tq_nsa).reshape(T, -1)
        g_out = out_norm[l]
        wo_l = w_out[l]
        h = out_projection(y_ssm, y_mla, y_nsa, h,
                           [g_out[:o1], g_out[o1:o2], g_out[o2:][perm]],
                           [wo_l[:o1].astype(BF16), wo_l[o1:o2].astype(BF16), wo_l[o2:][perm].astype(BF16)],
                           tm=512)
        kv_mem = norm_matmul(mem2, mem_norm[l], xattn_wkv[l].astype(BF16), tm=256, tn=512, out_dtype=BF16)
        h = cross_attention(h.reshape(bsz, seq, d), kv_mem.reshape(bsz, nmem, 2 * d), xattn_norm[l],
                            xattn_wq[l], xattn_wo[l], tm=256).reshape(T, d)
        last = l == depth - 1
        if l % 2 == 0:
            h = dense_ffn(h, ffn_norm[l], dense_w_gate[l // 2], dense_w_up[l // 2], dense_w_down[l // 2],
                          tm=512, tf=256)
            if last:
                h = final_rmsnorm(h, final_norm, tm=512)
        else:
            h = moe_layer(h, ffn_norm[l], moe_router[l // 2], moe_w_gate[l // 2], moe_w_up[l // 2],
                          moe_w_down[l // 2], final_norm, final_norm=last)
    return h.reshape(bsz, seq, d)
```

```python
import functools
import math

import jax
import jax.numpy as jnp
from jax import lax
from jax.experimental import pallas as pl
from jax.experimental.pallas import tpu as pltpu

F32 = jnp.float32
BF16 = jnp.bfloat16

HEAD_DIM = 64
SSM_WIDTH = 256
SSM_CH = 16
SSM_GROUPS = 16
SSM_STATE = 64
MLA_HEADS = 6
MLA_NOPE = 64
MLA_ROPE = 32
MLA_V = 64
MLA_Q_RANK = 192
MLA_KV_RANK = 128
NSA_HEADS = 6
NSA_KV_HEADS = 2
NSA_REP = 3
NSA_DIM = 64
CMP_BLOCK = 32
CMP_STRIDE = 16
SEL_BLOCK = 64
SEL_TOPN = 8
WINDOW = 256
REL_BUCKETS = 32
REL_MAX_DIST = 128
XATTN_HEADS = 4
N_EXPERTS = 8
ROPE_THETA = 10000.0
EPS = 1e-6
NEG_INF = -1e30
FORCE = 1e9

LANES = 128
VMEM_LIMIT = 56 * 1024 * 1024

C_U, C_CQ, C_KR, C_NQ, C_NKV, C_CKV, C_GATE = 0, 256, 512, 768, 1536, 2304, 2432
IN_COLS_PACKED = 2560


def _cparams(sem):
    return pltpu.CompilerParams(dimension_semantics=sem, vmem_limit_bytes=VMEM_LIMIT)


def _dot(a, b):
    return jnp.dot(a, b, preferred_element_type=F32)


def _dot_t(a, b):
    return lax.dot_general(a, b, (((1,), (1,)), ((), ())), preferred_element_type=F32)


def _rms(x, g, n):
    ms = jnp.sum(x * x, axis=-1, keepdims=True) * (1.0 / n)
    return x * lax.rsqrt(ms + EPS) * g


def _sigmoid(x):
    return 1.0 / (1.0 + jnp.exp(-x))


def _silu(x):
    return x * _sigmoid(x)


def _norm_mm_kernel(x_ref, g_ref, w_ref, o_ref, xn_ref):
    @pl.when(pl.program_id(1) == 0)
    def _():
        x = x_ref[...].astype(F32)
        xn_ref[...] = _rms(x, g_ref[...], x.shape[-1]).astype(BF16)

    o_ref[...] = _dot(xn_ref[...], w_ref[...]).astype(o_ref.dtype)


def norm_matmul(x, g, w, *, tm, tn, out_dtype):
    m, k = x.shape
    n = w.shape[1]
    return pl.pallas_call(
        _norm_mm_kernel,
        out_shape=jax.ShapeDtypeStruct((m, n), out_dtype),
        grid=(m // tm, n // tn),
        in_specs=[pl.BlockSpec((tm, k), lambda i, j: (i, 0)),
                  pl.BlockSpec((1, k), lambda i, j: (0, 0)),
                  pl.BlockSpec((k, tn), lambda i, j: (0, j))],
        out_specs=pl.BlockSpec((tm, tn), lambda i, j: (i, j)),
        scratch_shapes=[pltpu.VMEM((tm, k), BF16)],
        compiler_params=_cparams(("parallel", "arbitrary")),
        name="norm_matmul",
    )(x, g.reshape(1, k).astype(F32), w)


def _ssm_kernel(u_ref, bbr_ref, bbi_ref, ar_ref, ai_ref, ccr_ref, cci_ref, d_ref, wglu_ref,
                o_ref, hr_ref, hi_ref, cr_ref, ci_ref, *, tc, nb):
    @pl.when(pl.program_id(0) == 0)
    def _():
        cr_ref[...] = jnp.zeros_like(cr_ref)
        ci_ref[...] = jnp.zeros_like(ci_ref)

    u = u_ref[...]
    hr_ref[...] = _dot(u, bbr_ref[...])
    hi_ref[...] = _dot(u, bbi_ref[...])
    gp = ar_ref.shape[-1]
    ar = jnp.broadcast_to(ar_ref[...], (nb, gp))
    ai = jnp.broadcast_to(ai_ref[...], (nb, gp))

    def step(t, carry):
        hr, hi = carry
        rows = pl.ds(pl.multiple_of(t * nb, nb), nb)
        nr = ar * hr - ai * hi + hr_ref[rows, :]
        ni = ar * hi + ai * hr + hi_ref[rows, :]
        hr_ref[rows, :] = nr
        hi_ref[rows, :] = ni
        return nr, ni

    hr, hi = lax.fori_loop(0, tc, step, (cr_ref[...], ci_ref[...]))
    cr_ref[...] = hr
    ci_ref[...] = hi
    y = (_dot(hr_ref[...].astype(BF16), ccr_ref[...]) + _dot(hi_ref[...].astype(BF16), cci_ref[...])
         + d_ref[...] * u.astype(F32))
    y = jax.nn.gelu(y)
    z = _dot(y.astype(BF16), wglu_ref[...])
    o_ref[...] = (y * _sigmoid(z)).astype(o_ref.dtype)


def ssm_mixer(u_tm, a_re, a_im, b_re, b_im, c_re, c_im, d, log_dt, w_glu, *, nb, tc):
    rows = u_tm.shape[0]
    G, P, C = SSM_GROUPS, SSM_STATE, SSM_CH
    dt = jnp.exp(log_dt.astype(F32))[:, None]
    lr, li = a_re.astype(F32), a_im.astype(F32)
    mag = jnp.exp(lr * dt)
    ab_r, ab_i = mag * jnp.cos(li * dt), mag * jnp.sin(li * dt)
    den = lr * lr + li * li
    nr = ab_r - 1.0
    f_r = (nr * lr + ab_i * li) / den
    f_i = (ab_i * lr - nr * li) / den
    br, bi = b_re.astype(F32), b_im.astype(F32)
    bb_r = f_r[..., None] * br - f_i[..., None] * bi
    bb_i = f_r[..., None] * bi + f_i[..., None] * br
    eye = jnp.eye(G, dtype=F32)
    bbr = jnp.einsum('gpc,gh->gchp', bb_r, eye).reshape(G * C, G * P).astype(BF16)
    bbi = jnp.einsum('gpc,gh->gchp', bb_i, eye).reshape(G * C, G * P).astype(BF16)
    ccr = jnp.einsum('gcp,gh->gphc', c_re.astype(F32), eye).reshape(G * P, G * C).astype(BF16)
    cci = jnp.einsum('gcp,gh->gphc', -c_im.astype(F32), eye).reshape(G * P, G * C).astype(BF16)
    gp = G * P
    full = lambda shape: pl.BlockSpec(shape, lambda i: (0,) * len(shape))
    return pl.pallas_call(
        functools.partial(_ssm_kernel, tc=tc, nb=nb),
        out_shape=jax.ShapeDtypeStruct((rows, SSM_WIDTH), BF16),
        grid=(rows // (tc * nb),),
        in_specs=[pl.BlockSpec((tc * nb, SSM_WIDTH), lambda i: (i, 0)),
                  full((G * C, gp)), full((G * C, gp)), full((1, gp)), full((1, gp)),
                  full((gp, G * C)), full((gp, G * C)), full((1, SSM_WIDTH)),
                  full((SSM_WIDTH, SSM_WIDTH))],
        out_specs=pl.BlockSpec((tc * nb, SSM_WIDTH), lambda i: (i, 0)),
        scratch_shapes=[pltpu.VMEM((tc * nb, gp), F32), pltpu.VMEM((tc * nb, gp), F32),
                        pltpu.VMEM((nb, gp), F32), pltpu.VMEM((nb, gp), F32)],
        compiler_params=_cparams(("arbitrary",)),
        name="ssm_mixer",
    )(u_tm, bbr, bbi, ab_r.reshape(1, gp), ab_i.reshape(1, gp), ccr, cci,
      d.reshape(1, SSM_WIDTH).astype(F32), w_glu.astype(BF16))


def _mla_prep_kernel(cq_ref, kr_ref, ckv_ref, gq_ref, gkv_ref, wqa_ref, wqb_ref, wk_ref, wv_ref,
                     c1_ref, c0_ref, s0_ref, q_ref, k_ref, v_ref):
    qn = _rms(cq_ref[0].astype(F32), gq_ref[...], MLA_Q_RANK).astype(BF16)
    qa = _dot(qn, wqa_ref[...])
    qb = _dot(qn, wqb_ref[...])
    kn = _rms(ckv_ref[0].astype(F32), gkv_ref[...], MLA_KV_RANK).astype(BF16)
    ka = _dot(kn, wk_ref[...])
    va = _dot(kn, wv_ref[...])
    kr = kr_ref[0].astype(F32)
    c1, c0, s0 = c1_ref[...], c0_ref[...], s0_ref[...]
    krope = kr[:, :LANES] * c0 + kr[:, LANES:] * s0
    for h in range(MLA_HEADS):
        sl = slice(h * LANES, (h + 1) * LANES)
        q_ref[0, h] = (qa[:, sl] * c1 + qb[:, sl] * s0).astype(BF16)
        k_ref[0, h] = (ka[:, sl] + krope).astype(BF16)
        v_ref[0, h] = va[:, sl].astype(BF16)


def _mla_flash_kernel(q_ref, k_ref, v_ref, o_ref, m_ref, l_ref, acc_ref, *, tq):
    qi = pl.program_id(1)
    m_ref[...] = jnp.full_like(m_ref, NEG_INF)
    l_ref[...] = jnp.zeros_like(l_ref)
    acc_ref[...] = jnp.zeros_like(acc_ref)
    lane = lax.broadcasted_iota(jnp.int32, (tq, LANES), 1)
    rep = tq // LANES

    def tile(kt, masked):
        ks = pl.ds(pl.multiple_of(kt * tq, tq), tq)
        if masked:
            mask = (lax.broadcasted_iota(jnp.int32, (tq, tq), 1)
                    <= lax.broadcasted_iota(jnp.int32, (tq, tq), 0))
        for pr in range(MLA_HEADS // 2):
            hs = (2 * pr, 2 * pr + 1)
            s = [_dot_t(q_ref[0, h], k_ref[0, h, ks, :]) for h in hs]
            if masked:
                s = [jnp.where(mask, x, NEG_INF) for x in s]
            m_prev = [m_ref[h] for h in hs]
            m_new = [jnp.maximum(mp, jnp.max(x, axis=-1, keepdims=True)) for mp, x in zip(m_prev, s)]
            alpha = [jnp.exp(mp - mn) for mp, mn in zip(m_prev, m_new)]
            p = [jnp.exp(x - jnp.tile(mn, (1, rep))) for x, mn in zip(s, m_new)]
            for h, a, pp, mn in zip(hs, alpha, p, m_new):
                l_ref[h] = a * l_ref[h] + jnp.sum(pp, axis=-1, keepdims=True)
                m_ref[h] = mn
            pv = _dot(p[0].astype(BF16), v_ref[0, hs[0], ks, :]) + _dot(p[1].astype(BF16), v_ref[0, hs[1], ks, :])
            acc_ref[pr] = acc_ref[pr] * jnp.where(lane < MLA_V, alpha[0], alpha[1]) + pv

    def body(kt, c):
        tile(kt, False)
        return c

    lax.fori_loop(0, qi, body, 0)
    tile(qi, True)
    for pr in range(MLA_HEADS // 2):
        linv = jnp.where(lane < MLA_V, 1.0 / l_ref[2 * pr], 1.0 / l_ref[2 * pr + 1])
        o_ref[0, :, pr * LANES:(pr + 1) * LANES] = (acc_ref[pr] * linv).astype(o_ref.dtype)


def _rope_tables(seq):
    pos = jnp.arange(seq, dtype=F32)
    inv = 1.0 / (ROPE_THETA ** (jnp.arange(0, MLA_ROPE, 2, dtype=F32) / MLA_ROPE))
    ang = pos[:, None] * inv[None, :]
    cos, sin = jnp.cos(ang), jnp.sin(ang)
    cos2 = jnp.concatenate([cos, cos], axis=-1)
    sin2 = jnp.concatenate([sin, sin], axis=-1)
    z64 = jnp.zeros((seq, MLA_NOPE), F32)
    z32 = jnp.zeros((seq, LANES - MLA_NOPE - MLA_ROPE), F32)
    c1 = jnp.concatenate([jnp.ones((seq, MLA_NOPE), F32), cos2, z32], axis=-1)
    c0 = jnp.concatenate([z64, cos2, z32], axis=-1)
    s0 = jnp.concatenate([z64, sin2, z32], axis=-1)
    return c1, c0, s0


def _rot_half_cols(w):
    half = MLA_ROPE // 2
    return jnp.concatenate([-w[..., half:], w[..., :half]], axis=-1)


def mla_mixer(proj3, q_norm, w_uq, kv_norm, w_ukv, tabs, *, tm, tq):
    bsz, seq, _ = proj3.shape
    H = MLA_HEADS
    scale = (MLA_NOPE + MLA_ROPE) ** -0.5
    wq = (w_uq.astype(F32) * scale).reshape(MLA_Q_RANK, H, MLA_NOPE + MLA_ROPE)
    zq = jnp.zeros((MLA_Q_RANK, H, LANES - MLA_NOPE - MLA_ROPE), F32)
    z64 = jnp.zeros((MLA_Q_RANK, H, MLA_NOPE), F32)
    wqa = jnp.concatenate([wq, zq], axis=-1).reshape(MLA_Q_RANK, H * LANES)
    wqb = jnp.concatenate([z64, _rot_half_cols(wq[..., MLA_NOPE:]), zq], axis=-1).reshape(MLA_Q_RANK, H * LANES)
    padq = ((0, 256 - MLA_Q_RANK), (0, 0))
    wqa = jnp.pad(wqa, padq).astype(BF16)
    wqb = jnp.pad(wqb, padq).astype(BF16)
    gq = jnp.pad(q_norm.astype(F32), (0, 256 - MLA_Q_RANK)).reshape(1, 256)
    wkv = w_ukv.astype(F32).reshape(MLA_KV_RANK, H, MLA_NOPE + MLA_V)
    zk = jnp.zeros((MLA_KV_RANK, H, MLA_NOPE), F32)
    wk = jnp.concatenate([wkv[..., :MLA_NOPE], zk], axis=-1).reshape(MLA_KV_RANK, H * LANES).astype(BF16)
    wv_h = wkv[..., MLA_NOPE:]
    even = (jnp.arange(H) % 2 == 0)[None, :, None]
    wv = jnp.concatenate([jnp.where(even, wv_h, 0.0), jnp.where(even, 0.0, wv_h)], axis=-1)
    wv = wv.reshape(MLA_KV_RANK, H * LANES).astype(BF16)
    c1, c0, s0 = tabs
    full2 = lambda shape: pl.BlockSpec(shape, lambda b, i: (0,) * len(shape))
    tab_spec = pl.BlockSpec((tm, LANES), lambda b, i: (i, 0))
    hd_spec = pl.BlockSpec((1, H, tm, LANES), lambda b, i: (b, 0, i, 0))
    hd_shape = jax.ShapeDtypeStruct((bsz, H, seq, LANES), BF16)
    q, k, v = pl.pallas_call(
        _mla_prep_kernel,
        out_shape=(hd_shape, hd_shape, hd_shape),
        grid=(bsz, seq // tm),
        in_specs=[pl.BlockSpec((1, tm, 256), lambda b, i: (b, i, C_CQ // 256)),
                  pl.BlockSpec((1, tm, 256), lambda b, i: (b, i, C_KR // 256)),
                  pl.BlockSpec((1, tm, 128), lambda b, i: (b, i, C_CKV // 128)),
                  full2((1, 256)), full2((1, 128)),
                  full2((256, H * LANES)), full2((256, H * LANES)),
                  full2((128, H * LANES)), full2((128, H * LANES)),
                  tab_spec, tab_spec, tab_spec],
        out_specs=(hd_spec, hd_spec, hd_spec),
        compiler_params=_cparams(("parallel", "parallel")),
        name="mla_prep",
    )(proj3, proj3, proj3, gq, kv_norm.astype(F32).reshape(1, 128), wqa, wqb, wk, wv, c1, c0, s0)

    return pl.pallas_call(
        functools.partial(_mla_flash_kernel, tq=tq),
        out_shape=jax.ShapeDtypeStruct((bsz, seq, H * MLA_V), BF16),
        grid=(bsz, seq // tq),
        in_specs=[pl.BlockSpec((1, H, tq, LANES), lambda b, i: (b, 0, i, 0)),
                  pl.BlockSpec((1, H, seq, LANES), lambda b, i: (b, 0, 0, 0)),
                  pl.BlockSpec((1, H, seq, LANES), lambda b, i: (b, 0, 0, 0))],
        out_specs=pl.BlockSpec((1, tq, H * MLA_V), lambda b, i: (b, i, 0)),
        scratch_shapes=[pltpu.VMEM((H, tq, LANES), F32), pltpu.VMEM((H, tq, LANES), F32),
                        pltpu.VMEM((H // 2, tq, LANES), F32)],
        compiler_params=_cparams(("parallel", "arbitrary")),
        name="mla_flash",
    )(q, k, v)


def _nsa_cmp_kernel(x_ref, pea_ref, peb_ref, w1a_ref, w1b_ref, w2_ref, o_ref):
    x = x_ref[0, 0]
    w1a, w1b = w1a_ref[0], w1b_ref[0]
    bias = _dot(pea_ref[0], w1a)[0:1] + _dot(peb_ref[0], w1b)[0:1]
    a = _dot(x, w1a)
    b = _dot(x, w1b)
    n = b.shape[0]
    pre = a + pltpu.roll(b, n - 1, 0) + bias
    o_ref[0, 0] = _dot(jax.nn.gelu(pre).astype(BF16), w2_ref[0]).astype(o_ref.dtype)


def nsa_compress(kv_cr, cmp_pe, cmp_w1, cmp_w2):
    bsz, _, nch, width = kv_cr.shape
    G, dh = NSA_KV_HEADS, NSA_DIM
    half = CMP_BLOCK // 2
    eye = jnp.eye(G, dtype=F32)
    w1r = cmp_w1.astype(F32).reshape(2, CMP_BLOCK, dh, dh)
    w1a = jnp.einsum('kpde,gh->kpgdhe', w1r[:, :half], eye).reshape(2, width, G * dh).astype(BF16)
    w1b = jnp.einsum('kpde,gh->kpgdhe', w1r[:, half:], eye).reshape(2, width, G * dh).astype(BF16)
    w2 = jnp.einsum('kde,gh->kgdhe', cmp_w2.astype(F32), eye).reshape(2, G * dh, G * dh).astype(BF16)
    pe = cmp_pe.astype(F32)
    pe_g = jnp.broadcast_to(pe[:, :, None, :], (2, CMP_BLOCK, G, dh))
    pea = jnp.broadcast_to(pe_g[:, :half].reshape(2, 1, width), (2, 8, width)).astype(BF16)
    peb = jnp.broadcast_to(pe_g[:, half:].reshape(2, 1, width), (2, 8, width)).astype(BF16)
    kvspec = lambda shape: pl.BlockSpec(shape, lambda b, k: (k,) + (0,) * (len(shape) - 1))
    return pl.pallas_call(
        _nsa_cmp_kernel,
        out_shape=jax.ShapeDtypeStruct((bsz, 2, nch, G * dh), BF16),
        grid=(bsz, 2),
        in_specs=[pl.BlockSpec((1, 1, nch, width), lambda b, k: (b, k, 0, 0)),
                  kvspec((1, 8, width)), kvspec((1, 8, width)),
                  kvspec((1, width, G * dh)), kvspec((1, width, G * dh)),
                  kvspec((1, G * dh, G * dh))],
        out_specs=pl.BlockSpec((1, 1, nch, G * dh), lambda b, k: (b, k, 0, 0)),
        compiler_params=_cparams(("parallel", "parallel")),
        name="nsa_compress",
    )(kv_cr, pea, peb, w1a, w1b, w2)


def _nsa_kernel(rb_far_ref, q_ref, gate_ref, ksl_ref, vsl_ref, kwn_ref, vwn_ref, kvc_ref, bc_ref, bw_ref,
                bs_ref, ov_ref, o_ref, m_ref, l_ref, acc_ref, *, tq, tk, nbs, n_sel):
    qi = pl.program_id(1)
    R, G = NSA_REP, NSA_KV_HEADS
    lane = lax.broadcasted_iota(jnp.int32, (tq, LANES), 1)
    t3 = qi * tq + lax.broadcasted_iota(jnp.int32, (R * tq, 1), 0) % tq
    gs = _sigmoid(gate_ref[0].astype(F32))
    kc = kvc_ref[0, 0]
    vc = kvc_ref[0, 1]
    ov = ov_ref[...]
    blocks_per_tile = tk // SEL_BLOCK
    jj = lax.broadcasted_iota(jnp.int32, (LANES, tk), 0)
    cblk = lax.broadcasted_iota(jnp.int32, (LANES, tk), 1) // SEL_BLOCK

    def stack(fn):
        return jnp.concatenate([fn(r) for r in range(R)], axis=0)

    qg = [stack(lambda r: q_ref[0, :, (g * R + r) * LANES:(g * R + r + 1) * LANES]) for g in range(G)]

    valid = t3 >= (CMP_BLOCK - 1)
    blk = lax.broadcasted_iota(jnp.int32, (nbs, tq), 0)
    tl = qi * tq + lax.broadcasted_iota(jnp.int32, (nbs, tq), 1)
    cur = tl // SEL_BLOCK
    forced = (blk == 0) | (blk == cur) | (blk == cur - 1)
    future = blk * SEL_BLOCK > tl
    o_c, selb = [], []
    for g in range(G):
        s = _dot_t(qg[g], kc) + stack(lambda r: bc_ref[g * R + r])
        m = jnp.max(s, axis=-1, keepdims=True)
        p = jnp.where(valid, jnp.exp(s - m), 0.0)
        l = jnp.where(valid, jnp.sum(p, axis=-1, keepdims=True), 1.0)
        pc = p / l
        o_c.append(_dot(pc.astype(BF16), vc))
        psum = pc[0:tq] + pc[tq:2 * tq] + pc[2 * tq:3 * tq]
        p_hi = psum.astype(BF16)
        p_lo = (psum - p_hi.astype(F32)).astype(BF16)
        imp = (_dot_t(ov, p_hi) + _dot_t(ov, p_lo))[:nbs]
        imp = jnp.where(forced, FORCE, jnp.where(future, -FORCE, imp))
        rank = jnp.zeros((nbs, tq), F32)
        for i in range(nbs):
            ri = imp[i:i + 1, :]
            beats = (ri > imp) | ((ri == imp) & (blk > i))
            rank = rank + jnp.where(beats, 1.0, 0.0)
        sel = jnp.where(rank < n_sel, 1.0, 0.0)
        sel = jnp.concatenate([sel, jnp.zeros((LANES - nbs, tq), F32)], axis=0).T
        selb.append(sel.astype(BF16))

    m_ref[...] = jnp.full_like(m_ref, NEG_INF)
    l_ref[...] = jnp.zeros_like(l_ref)
    acc_ref[...] = jnp.zeros_like(acc_ref)

    def sel_tile(kt, bias_fn):
        ks = pl.ds(pl.multiple_of(kt * tk, tk), tk)
        e = jnp.where(jj == blocks_per_tile * kt + cblk, 1.0, 0.0).astype(BF16)
        k_t = ksl_ref[0, ks, :]
        v_t = vsl_ref[0, ks, :]
        for g in range(G):
            madd = (_dot(selb[g], e) - 1.0) * (-NEG_INF)
            s = _dot_t(qg[g], k_t) + stack(lambda r: madd + bias_fn(g * R + r))
            m_prev = m_ref[g]
            m_new = jnp.maximum(m_prev, jnp.max(s, axis=-1, keepdims=True))
            alpha = jnp.exp(m_prev - m_new)
            p = jnp.exp(s - jnp.tile(m_new, (1, tk // LANES)))
            l_ref[g] = alpha * l_ref[g] + jnp.sum(p, axis=-1, keepdims=True)
            m_ref[g] = m_new
            acc_ref[g] = acc_ref[g] * alpha + _dot(p.astype(BF16), v_t)

    def far_body(kt, c):
        sel_tile(kt, lambda h: rb_far_ref[h])
        return c

    lax.fori_loop(0, jnp.maximum((qi - 1) // 2, 0), far_body, 0)
    odd = qi % 2

    @pl.when((odd == 0) & (qi >= 2))
    def _():
        sel_tile(qi // 2 - 1, lambda h: bs_ref[h, 2])

    sel_tile(qi // 2, lambda h: bs_ref[h, odd])
    o_s = [acc_ref[g] / l_ref[g] for g in range(G)]

    c = jnp.minimum(qi, 2)
    ws = pl.ds(pl.multiple_of(jnp.maximum(qi - 2, 0) * tq, tq), 3 * tq)
    k_w = kwn_ref[0, ws, :]
    v_w = vwn_ref[0, ws, :]
    o_w = []
    for g in range(G):
        s = _dot_t(qg[g], k_w) + stack(lambda r: bw_ref[g * R + r, c])
        p = jnp.exp(s - jnp.max(s, axis=-1, keepdims=True))
        o_w.append(_dot(p.astype(BF16), v_w) / jnp.sum(p, axis=-1, keepdims=True))

    for r in range(R):
        rs = slice(r * tq, (r + 1) * tq)
        res = []
        for g in range(G):
            h = g * R + r
            res.append(gs[:, h:h + 1] * o_c[g][rs] + gs[:, NSA_HEADS + h:NSA_HEADS + h + 1] * o_s[g][rs]
                       + gs[:, 2 * NSA_HEADS + h:2 * NSA_HEADS + h + 1] * o_w[g][rs])
        o_ref[0, :, r * LANES:(r + 1) * LANES] = jnp.where(lane < NSA_DIM, res[0], res[1]).astype(o_ref.dtype)


def _t5_bucket(dist):
    n = jnp.maximum(dist, 0)
    exact = REL_BUCKETS // 2
    nf = jnp.maximum(n, exact).astype(F32)
    large = exact + (jnp.log(nf / exact) / math.log(REL_MAX_DIST / exact)
                     * (REL_BUCKETS - exact)).astype(jnp.int32)
    return jnp.where(n < exact, n, jnp.minimum(large, REL_BUCKETS - 1))


def _nsa_bias_tables(rel_bias, seq, tq, tk):
    rb = rel_bias.astype(F32).T

    def by_dist(dist, ok):
        bucket = _t5_bucket(dist)[None]
        out = jnp.zeros((NSA_HEADS,) + dist.shape, F32)
        for k in range(REL_BUCKETS):
            out = jnp.where(bucket == k, rb[:, k].reshape((NSA_HEADS,) + (1,) * dist.ndim), out)
        return jnp.where(ok[None], out, NEG_INF)

    i = jnp.arange(tq)[:, None]
    t = jnp.arange(seq)[:, None]
    dist_c = t - (jnp.arange(LANES)[None, :] * CMP_STRIDE + CMP_BLOCK - 1)
    bc = by_dist(dist_c, dist_c >= 0)
    jw = jnp.arange(3 * tq)[None, :]
    bw = jnp.stack([by_dist(tq * c + i - jw, (tq * c + i - jw >= 0) & (tq * c + i - jw < WINDOW))
                    for c in range(3)], axis=1)
    js = jnp.arange(tk)[None, :]
    bs = jnp.stack([by_dist(tq * c + i - js, tq * c + i - js >= 0) for c in range(3)], axis=1)
    ci = jnp.arange(LANES)[:, None]
    sj = jnp.arange(LANES)[None, :]
    nbs = seq // SEL_BLOCK
    ov = ((ci * CMP_STRIDE <= sj * SEL_BLOCK + SEL_BLOCK - 1)
          & (ci * CMP_STRIDE + CMP_BLOCK - 1 >= sj * SEL_BLOCK)
          & (ci < seq // CMP_STRIDE - 1) & (sj < nbs))
    return rb[:, REL_BUCKETS - 1], bc, bw, bs, ov.T.astype(BF16)


def nsa_mixer(proj3, kvc, tables, *, tq, tk):
    bsz, seq, _ = proj3.shape
    assert tq == LANES and tk == 2 * tq and seq // CMP_STRIDE == LANES and tq * 2 == WINDOW and seq % tk == 0
    rb_far, bc, bw, bs, ov = tables
    nbs = seq // SEL_BLOCK
    H = NSA_HEADS
    slab = lambda j: pl.BlockSpec((1, seq, LANES), lambda b, i: (b, 0, C_NKV // LANES + j))
    const = lambda shape: pl.BlockSpec(shape, lambda b, i: (0,) * len(shape))
    return pl.pallas_call(
        functools.partial(_nsa_kernel, tq=tq, tk=tk, nbs=nbs, n_sel=min(SEL_TOPN, nbs)),
        out_shape=jax.ShapeDtypeStruct((bsz, seq, H * NSA_DIM), BF16),
        grid=(bsz, seq // tq),
        in_specs=[pl.BlockSpec(memory_space=pltpu.SMEM),
                  pl.BlockSpec((1, tq, H * LANES), lambda b, i: (b, i, C_NQ // (H * LANES))),
                  pl.BlockSpec((1, tq, LANES), lambda b, i: (b, i, C_GATE // LANES)),
                  slab(2), slab(3), slab(4), slab(5),
                  pl.BlockSpec((1, 2, LANES, LANES), lambda b, i: (b, 0, 0, 0)),
                  pl.BlockSpec((H, tq, LANES), lambda b, i: (0, i, 0)),
                  const((H, 3, tq, 3 * tq)), const((H, 3, tq, tk)),
                  const((LANES, LANES))],
        out_specs=pl.BlockSpec((1, tq, H * NSA_DIM), lambda b, i: (b, i, 0)),
        scratch_shapes=[pltpu.VMEM((NSA_KV_HEADS, NSA_REP * tq, LANES), F32),
                        pltpu.VMEM((NSA_KV_HEADS, NSA_REP * tq, LANES), F32),
                        pltpu.VMEM((NSA_KV_HEADS, NSA_REP * tq, LANES), F32)],
        compiler_params=_cparams(("parallel", "arbitrary")),
        name="nsa_attention",
    )(rb_far, proj3, proj3, proj3, proj3, proj3, proj3, kvc, bc, bw, bs, ov)


def _outproj_kernel(ys_ref, ym_ref, yn_ref, h_ref, g1_ref, g2_ref, g3_ref, w1_ref, w2_ref, w3_ref, o_ref):
    def part(y_ref, g_ref, w_ref):
        y = y_ref[...].astype(F32)
        return _dot(_rms(y, g_ref[...], y.shape[-1]).astype(BF16), w_ref[...])

    o_ref[...] = h_ref[...] + part(ys_ref, g1_ref, w1_ref) + part(ym_ref, g2_ref, w2_ref) + part(yn_ref, g3_ref, w3_ref)


def out_projection(y_ssm, y_mla, y_nsa, h, gains, weights, *, tm):
    m, d = h.shape
    row = lambda w: pl.BlockSpec((tm, w), lambda i: (i, 0))
    full = lambda a: pl.BlockSpec(a.shape, lambda i: (0, 0))
    gains = [g.reshape(1, -1).astype(F32) for g in gains]
    return pl.pallas_call(
        _outproj_kernel,
        out_shape=jax.ShapeDtypeStruct((m, d), F32),
        grid=(m // tm,),
        in_specs=[row(y_ssm.shape[1]), row(y_mla.shape[1]), row(y_nsa.shape[1]), row(d)]
                 + [full(g) for g in gains] + [full(w) for w in weights],
        out_specs=row(d),
        compiler_params=_cparams(("parallel",)),
        name="out_projection",
    )(y_ssm, y_mla, y_nsa, h, *gains, *weights)


def _xattn_kernel(h_ref, kv_ref, g_ref, wq_ref, wo_ref, o_ref, *, dh):
    h = h_ref[0]
    xn = _rms(h, g_ref[...], h.shape[-1]).astype(BF16)
    q = _dot(xn, wq_ref[...]).astype(BF16)
    hw = XATTN_HEADS * dh
    outs = []
    for hd in range(XATTN_HEADS):
        k = kv_ref[0, :, hd * dh:(hd + 1) * dh]
        v = kv_ref[0, :, hw + hd * dh:hw + (hd + 1) * dh]
        s = _dot_t(q[:, hd * dh:(hd + 1) * dh], k)
        p = jnp.exp(s - jnp.max(s, axis=-1, keepdims=True))
        p = p / jnp.sum(p, axis=-1, keepdims=True)
        outs.append(_dot(p.astype(BF16), v).astype(BF16))
    o = jnp.concatenate(outs, axis=-1)
    o_ref[0] = h + _dot(o, wo_ref[...])


def cross_attention(h3, kv3, g_x, wq, wo, *, tm):
    bsz, seq, d = h3.shape
    m = kv3.shape[1]
    dh = d // XATTN_HEADS
    wq_s = (wq.astype(F32) * dh ** -0.5).astype(BF16)
    const = lambda shape: pl.BlockSpec(shape, lambda b, i: (0,) * len(shape))
    return pl.pallas_call(
        functools.partial(_xattn_kernel, dh=dh),
        out_shape=jax.ShapeDtypeStruct((bsz, seq, d), F32),
        grid=(bsz, seq // tm),
        in_specs=[pl.BlockSpec((1, tm, d), lambda b, i: (b, i, 0)),
                  pl.BlockSpec((1, m, 2 * d), lambda b, i: (b, 0, 0)),
                  const((1, d)), const((d, d)), const((d, d))],
        out_specs=pl.BlockSpec((1, tm, d), lambda b, i: (b, i, 0)),
        compiler_params=_cparams(("parallel", "parallel")),
        name="cross_attention",
    )(h3, kv3, g_x.reshape(1, d).astype(F32), wq_s, wo.astype(BF16))


def _ffn_kernel(h_ref, g_ref, wg_ref, wu_ref, wd_ref, o_ref, xn_ref, acc_ref):
    j = pl.program_id(1)

    @pl.when(j == 0)
    def _():
        h = h_ref[...]
        xn_ref[...] = _rms(h, g_ref[...], h.shape[-1]).astype(BF16)
        acc_ref[...] = h

    xn = xn_ref[...]
    a = _silu(_dot(xn, wg_ref[...])) * _dot(xn, wu_ref[...])
    acc_ref[...] += _dot(a.astype(BF16), wd_ref[...])

    @pl.when(j == pl.num_programs(1) - 1)
    def _():
        o_ref[...] = acc_ref[...]


def dense_ffn(h, g, wg, wu, wd, *, tm, tf):
    m, d = h.shape
    ff = wg.shape[1]
    return pl.pallas_call(
        _ffn_kernel,
        out_shape=jax.ShapeDtypeStruct((m, d), F32),
        grid=(m // tm, ff // tf),
        in_specs=[pl.BlockSpec((tm, d), lambda i, j: (i, 0)),
                  pl.BlockSpec((1, d), lambda i, j: (0, 0)),
                  pl.BlockSpec((d, tf), lambda i, j: (0, j)),
                  pl.BlockSpec((d, tf), lambda i, j: (0, j)),
                  pl.BlockSpec((tf, d), lambda i, j: (j, 0))],
        out_specs=pl.BlockSpec((tm, d), lambda i, j: (i, 0)),
        scratch_shapes=[pltpu.VMEM((tm, d), BF16), pltpu.VMEM((tm, d), F32)],
        compiler_params=_cparams(("parallel", "arbitrary")),
        name="dense_ffn",
    )(h, g.reshape(1, d).astype(F32), wg.astype(BF16), wu.astype(BF16), wd.astype(BF16))


def _router_kernel(h_ref, g_ref, wr_hi_ref, wr_lo_ref, xn_ref, info_ref, cnt_ref, carry_ref, *, tm):
    i = pl.program_id(0)

    @pl.when(i == 0)
    def _():
        carry_ref[...] = jnp.zeros_like(carry_ref)

    h = h_ref[...]
    xn = _rms(h, g_ref[...], h.shape[-1])
    xn_ref[...] = xn
    x_hi = xn.astype(BF16)
    x_lo = (xn - x_hi.astype(F32)).astype(BF16)
    logits = _dot(x_hi, wr_hi_ref[...]) + _dot(x_lo, wr_hi_ref[...]) + _dot(x_hi, wr_lo_ref[...])
    lane = lax.broadcasted_iota(jnp.int32, (tm, LANES), 1)
    lanef = lane.astype(F32)
    logits = jnp.where(lane < N_EXPERTS, logits, NEG_INF)
    m1 = jnp.max(logits, axis=-1, keepdims=True)
    i1 = jnp.min(jnp.where(logits == m1, lanef, float(LANES)), axis=-1, keepdims=True)
    rest = jnp.where(lanef == i1, NEG_INF, logits)
    m2 = jnp.max(rest, axis=-1, keepdims=True)
    i2 = jnp.min(jnp.where(rest == m2, lanef, float(LANES)), axis=-1, keepdims=True)
    e2 = jnp.exp(m2 - m1)
    w1 = 1.0 / (1.0 + e2)
    w2 = e2 / (1.0 + e2)
    oh1 = lanef == i1
    oh2 = lanef == i2
    oh = jnp.where(oh1 | oh2, 1.0, 0.0)
    rr = lax.broadcasted_iota(jnp.int32, (tm, tm), 0)
    cc = lax.broadcasted_iota(jnp.int32, (tm, tm), 1)
    tri = jnp.where(cc < rr, 1.0, 0.0).astype(BF16)
    before = _dot(tri, oh.astype(BF16)) + carry_ref[0:1, :]
    r1 = jnp.sum(jnp.where(oh1, before, 0.0), axis=-1, keepdims=True)
    r2 = jnp.sum(jnp.where(oh2, before, 0.0), axis=-1, keepdims=True)
    carry_ref[...] = carry_ref[...] + jnp.sum(oh, axis=0, keepdims=True)
    info = jnp.where(lane == 0, i1, jnp.where(lane == 1, i2, jnp.where(lane == 2, w1, jnp.where(
        lane == 3, w2, jnp.where(lane == 4, r1, jnp.where(lane == 5, r2, 0.0))))))
    info_ref[...] = info
    cnt_ref[...] = carry_ref[...]


def moe_router(h, g, router, *, tm):
    m, d = h.shape
    wr = jnp.pad(router.astype(F32), ((0, 0), (0, LANES - N_EXPERTS)))
    wr_hi = wr.astype(BF16)
    wr_lo = (wr - wr_hi.astype(F32)).astype(BF16)
    return pl.pallas_call(
        functools.partial(_router_kernel, tm=tm),
        out_shape=(jax.ShapeDtypeStruct((m, d), F32), jax.ShapeDtypeStruct((m, LANES), F32),
                   jax.ShapeDtypeStruct((8, LANES), F32)),
        grid=(m // tm,),
        in_specs=[pl.BlockSpec((tm, d), lambda i: (i, 0)),
                  pl.BlockSpec((1, d), lambda i: (0, 0)),
                  pl.BlockSpec((d, LANES), lambda i: (0, 0)),
                  pl.BlockSpec((d, LANES), lambda i: (0, 0))],
        out_specs=(pl.BlockSpec((tm, d), lambda i: (i, 0)),
                   pl.BlockSpec((tm, LANES), lambda i: (i, 0)),
                   pl.BlockSpec((8, LANES), lambda i: (0, 0))),
        scratch_shapes=[pltpu.VMEM((8, LANES), F32)],
        compiler_params=_cparams(("arbitrary",)),
        name="moe_router",
    )(h, g.reshape(1, d).astype(F32), wr_hi, wr_lo)


def _row_gather_copy(src_hbm, row, dst, slot, sem):
    return pltpu.make_async_copy(src_hbm.at[pl.ds(row, 1), :], dst.at[pl.ds(slot, 1), :], sem)


def _moe_ffn_kernel(src_ref, texp_ref, nact_ref, x_hbm, wg_ref, wu_ref, wd_ref, o_ref,
                    xbuf, xbf, acc_ref, sem, *, tm):
    i = pl.program_id(0)
    j = pl.program_id(1)
    nj = pl.num_programs(1)
    nact = nact_ref[0]
    active = i < nact
    cur = i % 2

    def start_gather(tile, buf):
        def issue(s, c):
            _row_gather_copy(x_hbm, src_ref[tile * tm + s], xbuf.at[buf], s, sem.at[buf]).start()
            return c

        lax.fori_loop(0, tm, issue, 0)

    @pl.when((i == 0) & (j == 0))
    def _():
        start_gather(0, 0)

    @pl.when(active & (j == 0))
    def _():
        def wait(s, c):
            _row_gather_copy(x_hbm, 0, xbuf.at[cur], s, sem.at[cur]).wait()
            return c

        lax.fori_loop(0, tm, wait, 0)
        xbf[...] = xbuf[cur].astype(BF16)
        acc_ref[...] = jnp.zeros_like(acc_ref)

    @pl.when((j == 1) & (i + 1 < nact))
    def _():
        start_gather(i + 1, 1 - cur)

    @pl.when(active)
    def _():
        x = xbf[...]
        a = _silu(_dot(x, wg_ref[0])) * _dot(x, wu_ref[0])
        acc_ref[...] += _dot(a.astype(BF16), wd_ref[0])

    @pl.when(j == nj - 1)
    def _():
        o_ref[...] = jnp.where(active, acc_ref[...], 0.0)


def moe_expert_ffn(xn, src, tile_expert, n_active, wg, wu, wd, *, tm, tf):
    n_slots = src.shape[0]
    d = xn.shape[1]
    ff = wg.shape[2]
    nj = ff // tf
    assert nj >= 2

    def wmap_col(i, j, src, texp, nact):
        return (texp[i], 0, jnp.where(i < nact[0], j, nj - 1))

    def wmap_row(i, j, src, texp, nact):
        return (texp[i], jnp.where(i < nact[0], j, nj - 1), 0)

    return pl.pallas_call(
        functools.partial(_moe_ffn_kernel, tm=tm),
        out_shape=jax.ShapeDtypeStruct((n_slots, d), F32),
        grid_spec=pltpu.PrefetchScalarGridSpec(
            num_scalar_prefetch=3,
            grid=(n_slots // tm, nj),
            in_specs=[pl.BlockSpec(memory_space=pl.ANY),
                      pl.BlockSpec((1, d, tf), wmap_col),
                      pl.BlockSpec((1, d, tf), wmap_col),
                      pl.BlockSpec((1, tf, d), wmap_row)],
            out_specs=pl.BlockSpec((tm, d), lambda i, j, *_: (i, 0)),
            scratch_shapes=[pltpu.VMEM((2, tm, d), F32), pltpu.VMEM((tm, d), BF16),
                            pltpu.VMEM((tm, d), F32), pltpu.SemaphoreType.DMA((2,))]),
        compiler_params=_cparams(("arbitrary", "arbitrary")),
        name="moe_expert_ffn",
    )(src, tile_expert, n_active, xn, wg, wu, wd)


def _moe_combine_kernel(pos_ref, h_ref, info_ref, ys_hbm, g_ref, o_ref, buf, sem, *, tm, final_norm):
    i = pl.program_id(0)
    n = pl.num_programs(0)
    cur = i % 2

    def start_gather(tile, b):
        def issue(s, c):
            for k in range(2):
                _row_gather_copy(ys_hbm, pos_ref[2 * (tile * tm + s) + k], buf.at[b, k], s, sem.at[b]).start()
            return c

        lax.fori_loop(0, tm, issue, 0)

    @pl.when(i == 0)
    def _():
        start_gather(0, 0)

    @pl.when(i + 1 < n)
    def _():
        start_gather(i + 1, 1 - cur)

    def wait(s, c):
        for k in range(2):
            _row_gather_copy(ys_hbm, 0, buf.at[cur, k], s, sem.at[cur]).wait()
        return c

    lax.fori_loop(0, tm, wait, 0)
    info = info_ref[...]
    y = h_ref[...] + info[:, 2:3] * buf[cur, 0] + info[:, 3:4] * buf[cur, 1]
    if final_norm:
        y = _rms(y, g_ref[...], y.shape[-1])
    o_ref[...] = y


def moe_combine(h, info, ys, pos_flat, g_final, *, tm, final_norm):
    m, d = h.shape
    return pl.pallas_call(
        functools.partial(_moe_combine_kernel, tm=tm, final_norm=final_norm),
        out_shape=jax.ShapeDtypeStruct((m, d), F32),
        grid_spec=pltpu.PrefetchScalarGridSpec(
            num_scalar_prefetch=1,
            grid=(m // tm,),
            in_specs=[pl.BlockSpec((tm, d), lambda i, *_: (i, 0)),
                      pl.BlockSpec((tm, LANES), lambda i, *_: (i, 0)),
                      pl.BlockSpec(memory_space=pl.ANY),
                      pl.BlockSpec((1, d), lambda i, *_: (0, 0))],
            out_specs=pl.BlockSpec((tm, d), lambda i, *_: (i, 0)),
            scratch_shapes=[pltpu.VMEM((2, 2, tm, d), F32), pltpu.SemaphoreType.DMA((2,))]),
        compiler_params=_cparams(("arbitrary",)),
        name="moe_combine",
    )(pos_flat, h, info, ys, g_final.reshape(1, d).astype(F32))


def moe_layer(h, g, router, wg, wu, wd, g_final, *, final_norm, tm_r=512, tm_g=512, tf=512, tm_c=256):
    m, d = h.shape
    xn, info, cnt = moe_router(h, g, router, tm=tm_r)
    e_idx = info[:, 0:2].astype(jnp.int32)
    rank = info[:, 4:6].astype(jnp.int32)
    counts = cnt[0, :N_EXPERTS].astype(jnp.int32)
    tiles_per = (counts + tm_g - 1) // tm_g
    tile_end = jnp.cumsum(tiles_per)
    seg_start = (tile_end - tiles_per) * tm_g
    pos = rank
    for e in range(N_EXPERTS):
        pos = pos + jnp.where(e_idx == e, seg_start[e], 0)
    n_tiles = (2 * m) // tm_g + N_EXPERTS
    n_slots = n_tiles * tm_g
    tok = jnp.broadcast_to(jnp.arange(m, dtype=jnp.int32)[:, None], (m, 2))
    src = jnp.zeros((n_slots,), jnp.int32).at[pos.reshape(-1)].set(tok.reshape(-1))
    n_active = tile_end[-1:].astype(jnp.int32)
    tile_ids = jnp.minimum(jnp.arange(n_tiles, dtype=jnp.int32), n_active[0] - 1)
    tile_expert = jnp.sum(tile_ids[:, None] >= tile_end[None, :], axis=1).astype(jnp.int32)
    ys = moe_expert_ffn(xn, src, tile_expert, n_active,
                        wg.astype(BF16), wu.astype(BF16), wd.astype(BF16), tm=tm_g, tf=tf)
    return moe_combine(h, info, ys, pos.reshape(-1).astype(jnp.int32), g_final, tm=tm_c, final_norm=final_norm)


def _final_norm_kernel(h_ref, g_ref, o_ref):
    h = h_ref[...]
    o_ref[...] = _rms(h, g_ref[...], h.shape[-1])


def final_rmsnorm(h, g, *, tm):
    m, d = h.shape
    return pl.pallas_call(
        _final_norm_kernel,
        out_shape=jax.ShapeDtypeStruct((m, d), F32),
        grid=(m // tm,),
        in_specs=[pl.BlockSpec((tm, d), lambda i: (i, 0)), pl.BlockSpec((1, d), lambda i: (0, 0))],
        out_specs=pl.BlockSpec((tm, d), lambda i: (i, 0)),
        compiler_params=_cparams(("parallel",)),
        name="final_rmsnorm",
    )(h, g.reshape(1, d).astype(F32))


def _pack_w_in(w):
    d = w.shape[0]
    w = w.astype(F32)
    o = 0
    u = w[:, o:o + SSM_WIDTH]; o += SSM_WIDTH
    cq = w[:, o:o + MLA_Q_RANK]; o += MLA_Q_RANK
    ckv = w[:, o:o + MLA_KV_RANK]; o += MLA_KV_RANK
    kr = w[:, o:o + MLA_ROPE]; o += MLA_ROPE
    nq = w[:, o:o + NSA_HEADS * NSA_DIM]; o += NSA_HEADS * NSA_DIM
    nkv = w[:, o:o + 6 * NSA_KV_HEADS * NSA_DIM]; o += 6 * NSA_KV_HEADS * NSA_DIM
    gate = w[:, o:o + 3 * NSA_HEADS]
    z = lambda n: jnp.zeros((d, n), F32)
    kr_a = jnp.concatenate([z(MLA_NOPE), kr, z(LANES - MLA_NOPE - MLA_ROPE)], axis=1)
    kr_b = jnp.concatenate([z(MLA_NOPE), _rot_half_cols(kr), z(LANES - MLA_NOPE - MLA_ROPE)], axis=1)
    nq_h = (nq * NSA_DIM ** -0.5).reshape(d, NSA_KV_HEADS, NSA_REP, NSA_DIM)
    zq = jnp.zeros((d, NSA_REP, NSA_DIM), F32)
    nq_p = jnp.concatenate([
        jnp.concatenate([nq_h[:, 0], zq], axis=-1).reshape(d, NSA_REP * LANES),
        jnp.concatenate([zq, nq_h[:, 1]], axis=-1).reshape(d, NSA_REP * LANES)], axis=1)
    packed = jnp.concatenate([u, cq, z(256 - MLA_Q_RANK), kr_a, kr_b, nq_p, nkv, ckv,
                              gate, z(LANES - 3 * NSA_HEADS)], axis=1)
    assert packed.shape[1] == IN_COLS_PACKED
    return packed.astype(BF16)


def _rg_order(a):
    rest = a.shape[1:]
    return a.reshape((NSA_KV_HEADS, NSA_REP, NSA_DIM) + rest).swapaxes(0, 1).reshape((-1,) + rest)


def kernel(x, mem, w_in, w_out, mix_norm, out_norm, ssm_a_re, ssm_a_im, ssm_b_re, ssm_b_im, ssm_c_re, ssm_c_im, ssm_d, ssm_log_dt, ssm_w_glu, mla_q_norm, mla_w_uq, mla_kv_norm, mla_w_ukv, nsa_cmp_pe, nsa_cmp_w1, nsa_cmp_w2, rel_bias, xattn_norm, mem_norm, xattn_wq, xattn_wkv, xattn_wo, ffn_norm, dense_w_gate, dense_w_up, dense_w_down, moe_router, moe_w_gate, moe_w_up, moe_w_down, final_norm):
    bsz, seq, d = x.shape
    depth = w_in.shape[0]
    T = bsz * seq
    nmem = mem.shape[1]
    tq_nsa, tk_nsa = LANES, 2 * LANES
    rope_tabs = _rope_tables(seq)
    nsa_tabs = _nsa_bias_tables(rel_bias, seq, tq_nsa, tk_nsa)
    o1, o2 = SSM_WIDTH, SSM_WIDTH + MLA_HEADS * MLA_V
    mem2 = mem.reshape(bsz * nmem, d)
    h = x.reshape(T, d)
    for l in range(depth):
        proj = norm_matmul(h, mix_norm[l], _pack_w_in(w_in[l]), tm=1024, tn=512, out_dtype=BF16)
        proj3 = proj.reshape(bsz, seq, IN_COLS_PACKED)
        u_tm = proj3[:, :, C_U:C_U + SSM_WIDTH].transpose(1, 0, 2).reshape(seq * bsz, SSM_WIDTH)
        y_ssm = ssm_mixer(u_tm, ssm_a_re[l], ssm_a_im[l], ssm_b_re[l], ssm_b_im[l], ssm_c_re[l], ssm_c_im[l],
                          ssm_d[l], ssm_log_dt[l], ssm_w_glu[l], nb=bsz, tc=64)
        y_ssm = y_ssm.reshape(seq, bsz, SSM_WIDTH).transpose(1, 0, 2).reshape(T, SSM_WIDTH)
        y_mla = mla_mixer(proj3, mla_q_norm[l], mla_w_uq[l], mla_kv_norm[l], mla_w_ukv[l], rope_tabs,
                          tm=512, tq=512).reshape(T, -1)
        nch = seq // CMP_STRIDE
        kv_cr = jnp.stack([proj3[:, :, C_NKV:C_NKV + LANES].reshape(bsz, nch, CMP_STRIDE * LANES),
                           proj3[:, :, C_NKV + LANES:C_NKV + 2 * LANES].reshape(bsz, nch, CMP_STRIDE * LANES)],
                          axis=1)
        kvc = nsa_compress(kv_cr, nsa_cmp_pe[l], nsa_cmp_w1[l], nsa_cmp_w2[l])
        y_nsa = nsa_mixer(proj3, kvc, nsa_tabs, tq=tq_nsa, tk=tk_nsa).reshape(T, -1)
        g_out = out_norm[l]
        wo_l = w_out[l]
        h = out_projection(y_ssm, y_mla, y_nsa, h,
                           [g_out[:o1], g_out[o1:o2], _rg_order(g_out[o2:])],
                           [wo_l[:o1].astype(BF16), wo_l[o1:o2].astype(BF16), _rg_order(wo_l[o2:]).astype(BF16)],
                           tm=512)
        kv_mem = norm_matmul(mem2, mem_norm[l], xattn_wkv[l].astype(BF16), tm=256, tn=512, out_dtype=BF16)
        h = cross_attention(h.reshape(bsz, seq, d), kv_mem.reshape(bsz, nmem, 2 * d), xattn_norm[l],
                            xattn_wq[l], xattn_wo[l], tm=256).reshape(T, d)
        last = l == depth - 1
        if l % 2 == 0:
            h = dense_ffn(h, ffn_norm[l], dense_w_gate[l // 2], dense_w_up[l // 2], dense_w_down[l // 2],
                          tm=512, tf=256)
            if last:
                h = final_rmsnorm(h, final_norm, tm=512)
        else:
            h = moe_layer(h, ffn_norm[l], moe_router[l // 2], moe_w_gate[l // 2], moe_w_up[l // 2],
                          moe_w_down[l // 2], final_norm, final_norm=last)
    return h.reshape(bsz, seq, d)
```

```python
import functools
import math

import jax
import jax.numpy as jnp
from jax import lax
from jax.experimental import pallas as pl
from jax.experimental.pallas import tpu as pltpu

F32 = jnp.float32
BF16 = jnp.bfloat16

HEAD_DIM = 64
SSM_WIDTH = 256
SSM_CH = 16
SSM_GROUPS = 16
SSM_STATE = 64
MLA_HEADS = 6
MLA_NOPE = 64
MLA_ROPE = 32
MLA_V = 64
MLA_Q_RANK = 192
MLA_KV_RANK = 128
NSA_HEADS = 6
NSA_KV_HEADS = 2
NSA_REP = 3
NSA_DIM = 64
CMP_BLOCK = 32
CMP_STRIDE = 16
SEL_BLOCK = 64
SEL_TOPN = 8
WINDOW = 256
REL_BUCKETS = 32
REL_MAX_DIST = 128
XATTN_HEADS = 4
N_EXPERTS = 8
ROPE_THETA = 10000.0
EPS = 1e-6
NEG_INF = -1e30
FORCE = 1e9

LANES = 128
VMEM_LIMIT = 56 * 1024 * 1024

C_U, C_CQ, C_KR, C_NQ, C_NKV, C_CKV, C_GATE = 0, 256, 512, 768, 1536, 2304, 2432
IN_COLS_PACKED = 2560


def _cparams(sem):
    return pltpu.CompilerParams(dimension_semantics=sem, vmem_limit_bytes=VMEM_LIMIT)


def _dot(a, b):
    return jnp.dot(a, b, preferred_element_type=F32)


def _dot_t(a, b):
    return lax.dot_general(a, b, (((1,), (1,)), ((), ())), preferred_element_type=F32)


def _rms(x, g, n):
    ms = jnp.sum(x * x, axis=-1, keepdims=True) * (1.0 / n)
    return x * lax.rsqrt(ms + EPS) * g


def _sigmoid(x):
    return 1.0 / (1.0 + jnp.exp(-x))


def _silu(x):
    return x * _sigmoid(x)


def _norm_mm_kernel(x_ref, g_ref, w_ref, o_ref, xn_ref):
    @pl.when(pl.program_id(1) == 0)
    def _():
        x = x_ref[...].astype(F32)
        xn_ref[...] = _rms(x, g_ref[...], x.shape[-1]).astype(BF16)

    o_ref[...] = _dot(xn_ref[...], w_ref[...]).astype(o_ref.dtype)


def norm_matmul(x, g, w, *, tm, tn, out_dtype):
    m, k = x.shape
    n = w.shape[1]
    return pl.pallas_call(
        _norm_mm_kernel,
        out_shape=jax.ShapeDtypeStruct((m, n), out_dtype),
        grid=(m // tm, n // tn),
        in_specs=[pl.BlockSpec((tm, k), lambda i, j: (i, 0)),
                  pl.BlockSpec((1, k), lambda i, j: (0, 0)),
                  pl.BlockSpec((k, tn), lambda i, j: (0, j))],
        out_specs=pl.BlockSpec((tm, tn), lambda i, j: (i, j)),
        scratch_shapes=[pltpu.VMEM((tm, k), BF16)],
        compiler_params=_cparams(("parallel", "arbitrary")),
        name="norm_matmul",
    )(x, g.reshape(1, k).astype(F32), w)


def _ssm_kernel(u_ref, bbr_ref, bbi_ref, ar_ref, ai_ref, ccr_ref, cci_ref, d_ref, wglu_ref,
                o_ref, hr_ref, hi_ref, cr_ref, ci_ref, *, tc, nb):
    @pl.when(pl.program_id(0) == 0)
    def _():
        cr_ref[...] = jnp.zeros_like(cr_ref)
        ci_ref[...] = jnp.zeros_like(ci_ref)

    u = u_ref[...]
    hr_ref[...] = _dot(u, bbr_ref[...])
    hi_ref[...] = _dot(u, bbi_ref[...])
    gp = ar_ref.shape[-1]
    ar = jnp.broadcast_to(ar_ref[...], (nb, gp))
    ai = jnp.broadcast_to(ai_ref[...], (nb, gp))

    def step(t, carry):
        hr, hi = carry
        rows = pl.ds(pl.multiple_of(t * nb, nb), nb)
        nr = ar * hr - ai * hi + hr_ref[rows, :]
        ni = ar * hi + ai * hr + hi_ref[rows, :]
        hr_ref[rows, :] = nr
        hi_ref[rows, :] = ni
        return nr, ni

    hr, hi = lax.fori_loop(0, tc, step, (cr_ref[...], ci_ref[...]))
    cr_ref[...] = hr
    ci_ref[...] = hi
    y = (_dot(hr_ref[...].astype(BF16), ccr_ref[...]) + _dot(hi_ref[...].astype(BF16), cci_ref[...])
         + d_ref[...] * u.astype(F32))
    y = jax.nn.gelu(y)
    z = _dot(y.astype(BF16), wglu_ref[...])
    o_ref[...] = (y * _sigmoid(z)).astype(o_ref.dtype)


def ssm_mixer(u_tm, a_re, a_im, b_re, b_im, c_re, c_im, d, log_dt, w_glu, *, nb, tc):
    rows = u_tm.shape[0]
    G, P, C = SSM_GROUPS, SSM_STATE, SSM_CH
    dt = jnp.exp(log_dt.astype(F32))[:, None]
    lr, li = a_re.astype(F32), a_im.astype(F32)
    mag = jnp.exp(lr * dt)
    ab_r, ab_i = mag * jnp.cos(li * dt), mag * jnp.sin(li * dt)
    den = lr * lr + li * li
    nr = ab_r - 1.0
    f_r = (nr * lr + ab_i * li) / den
    f_i = (ab_i * lr - nr * li) / den
    br, bi = b_re.astype(F32), b_im.astype(F32)
    bb_r = f_r[..., None] * br - f_i[..., None] * bi
    bb_i = f_r[..., None] * bi + f_i[..., None] * br
    eye = jnp.eye(G, dtype=F32)
    bbr = jnp.einsum('gpc,gh->gchp', bb_r, eye).reshape(G * C, G * P).astype(BF16)
    bbi = jnp.einsum('gpc,gh->gchp', bb_i, eye).reshape(G * C, G * P).astype(BF16)
    ccr = jnp.einsum('gcp,gh->gphc', c_re.astype(F32), eye).reshape(G * P, G * C).astype(BF16)
    cci = jnp.einsum('gcp,gh->gphc', -c_im.astype(F32), eye).reshape(G * P, G * C).astype(BF16)
    gp = G * P
    full = lambda shape: pl.BlockSpec(shape, lambda i: (0,) * len(shape))
    return pl.pallas_call(
        functools.partial(_ssm_kernel, tc=tc, nb=nb),
        out_shape=jax.ShapeDtypeStruct((rows, SSM_WIDTH), BF16),
        grid=(rows // (tc * nb),),
        in_specs=[pl.BlockSpec((tc * nb, SSM_WIDTH), lambda i: (i, 0)),
                  full((G * C, gp)), full((G * C, gp)), full((1, gp)), full((1, gp)),
                  full((gp, G * C)), full((gp, G * C)), full((1, SSM_WIDTH)),
                  full((SSM_WIDTH, SSM_WIDTH))],
        out_specs=pl.BlockSpec((tc * nb, SSM_WIDTH), lambda i: (i, 0)),
        scratch_shapes=[pltpu.VMEM((tc * nb, gp), F32), pltpu.VMEM((tc * nb, gp), F32),
                        pltpu.VMEM((nb, gp), F32), pltpu.VMEM((nb, gp), F32)],
        compiler_params=_cparams(("arbitrary",)),
        name="ssm_mixer",
    )(u_tm, bbr, bbi, ab_r.reshape(1, gp), ab_i.reshape(1, gp), ccr, cci,
      d.reshape(1, SSM_WIDTH).astype(F32), w_glu.astype(BF16))


def _mla_prep_kernel(cq_ref, kr_ref, ckv_ref, gq_ref, gkv_ref, wqa_ref, wqb_ref, wk_ref, wv_ref,
                     c1_ref, c0_ref, s0_ref, q_ref, k_ref, v_ref):
    qn = _rms(cq_ref[0].astype(F32), gq_ref[...], MLA_Q_RANK).astype(BF16)
    qa = _dot(qn, wqa_ref[...])
    qb = _dot(qn, wqb_ref[...])
    kn = _rms(ckv_ref[0].astype(F32), gkv_ref[...], MLA_KV_RANK).astype(BF16)
    ka = _dot(kn, wk_ref[...])
    va = _dot(kn, wv_ref[...])
    kr = kr_ref[0].astype(F32)
    c1, c0, s0 = c1_ref[...], c0_ref[...], s0_ref[...]
    krope = kr[:, :LANES] * c0 + kr[:, LANES:] * s0
    for h in range(MLA_HEADS):
        sl = slice(h * LANES, (h + 1) * LANES)
        q_ref[0, h] = (qa[:, sl] * c1 + qb[:, sl] * s0).astype(BF16)
        k_ref[0, h] = (ka[:, sl] + krope).astype(BF16)
        v_ref[0, h] = va[:, sl].astype(BF16)


def _mla_flash_kernel(q_ref, k_ref, v_ref, o_ref, m_ref, l_ref, acc_ref, *, tq):
    qi = pl.program_id(1)
    m_ref[...] = jnp.full_like(m_ref, NEG_INF)
    l_ref[...] = jnp.zeros_like(l_ref)
    acc_ref[...] = jnp.zeros_like(acc_ref)
    lane = lax.broadcasted_iota(jnp.int32, (tq, LANES), 1)
    rep = tq // LANES

    def tile(kt, masked):
        ks = pl.ds(pl.multiple_of(kt * tq, tq), tq)
        if masked:
            mask = (lax.broadcasted_iota(jnp.int32, (tq, tq), 1)
                    <= lax.broadcasted_iota(jnp.int32, (tq, tq), 0))
        for pr in range(MLA_HEADS // 2):
            hs = (2 * pr, 2 * pr + 1)
            s = [_dot_t(q_ref[0, h], k_ref[0, h, ks, :]) for h in hs]
            if masked:
                s = [jnp.where(mask, x, NEG_INF) for x in s]
            m_prev = [m_ref[h] for h in hs]
            m_new = [jnp.maximum(mp, jnp.max(x, axis=-1, keepdims=True)) for mp, x in zip(m_prev, s)]
            alpha = [jnp.exp(mp - mn) for mp, mn in zip(m_prev, m_new)]
            p = [jnp.exp(x - jnp.tile(mn, (1, rep))) for x, mn in zip(s, m_new)]
            for h, a, pp, mn in zip(hs, alpha, p, m_new):
                l_ref[h] = a * l_ref[h] + jnp.sum(pp, axis=-1, keepdims=True)
                m_ref[h] = mn
            pv = _dot(p[0].astype(BF16), v_ref[0, hs[0], ks, :]) + _dot(p[1].astype(BF16), v_ref[0, hs[1], ks, :])
            acc_ref[pr] = acc_ref[pr] * jnp.where(lane < MLA_V, alpha[0], alpha[1]) + pv

    def body(kt, c):
        tile(kt, False)
        return c

    lax.fori_loop(0, qi, body, 0)
    tile(qi, True)
    for pr in range(MLA_HEADS // 2):
        linv = jnp.where(lane < MLA_V, 1.0 / l_ref[2 * pr], 1.0 / l_ref[2 * pr + 1])
        o_ref[0, :, pr * LANES:(pr + 1) * LANES] = (acc_ref[pr] * linv).astype(o_ref.dtype)


def _rope_tables(seq):
    pos = jnp.arange(seq, dtype=F32)
    inv = 1.0 / (ROPE_THETA ** (jnp.arange(0, MLA_ROPE, 2, dtype=F32) / MLA_ROPE))
    ang = pos[:, None] * inv[None, :]
    cos, sin = jnp.cos(ang), jnp.sin(ang)
    cos2 = jnp.concatenate([cos, cos], axis=-1)
    sin2 = jnp.concatenate([sin, sin], axis=-1)
    z64 = jnp.zeros((seq, MLA_NOPE), F32)
    z32 = jnp.zeros((seq, LANES - MLA_NOPE - MLA_ROPE), F32)
    c1 = jnp.concatenate([jnp.ones((seq, MLA_NOPE), F32), cos2, z32], axis=-1)
    c0 = jnp.concatenate([z64, cos2, z32], axis=-1)
    s0 = jnp.concatenate([z64, sin2, z32], axis=-1)
    return c1, c0, s0


def _rot_half_cols(w):
    half = MLA_ROPE // 2
    return jnp.concatenate([-w[..., half:], w[..., :half]], axis=-1)


def mla_mixer(proj3, q_norm, w_uq, kv_norm, w_ukv, tabs, *, tm, tq):
    bsz, seq, _ = proj3.shape
    H = MLA_HEADS
    scale = (MLA_NOPE + MLA_ROPE) ** -0.5
    wq = (w_uq.astype(F32) * scale).reshape(MLA_Q_RANK, H, MLA_NOPE + MLA_ROPE)
    zq = jnp.zeros((MLA_Q_RANK, H, LANES - MLA_NOPE - MLA_ROPE), F32)
    z64 = jnp.zeros((MLA_Q_RANK, H, MLA_NOPE), F32)
    wqa = jnp.concatenate([wq, zq], axis=-1).reshape(MLA_Q_RANK, H * LANES)
    wqb = jnp.concatenate([z64, _rot_half_cols(wq[..., MLA_NOPE:]), zq], axis=-1).reshape(MLA_Q_RANK, H * LANES)
    padq = ((0, 256 - MLA_Q_RANK), (0, 0))
    wqa = jnp.pad(wqa, padq).astype(BF16)
    wqb = jnp.pad(wqb, padq).astype(BF16)
    gq = jnp.pad(q_norm.astype(F32), (0, 256 - MLA_Q_RANK)).reshape(1, 256)
    wkv = w_ukv.astype(F32).reshape(MLA_KV_RANK, H, MLA_NOPE + MLA_V)
    zk = jnp.zeros((MLA_KV_RANK, H, MLA_NOPE), F32)
    wk = jnp.concatenate([wkv[..., :MLA_NOPE], zk], axis=-1).reshape(MLA_KV_RANK, H * LANES).astype(BF16)
    wv_h = wkv[..., MLA_NOPE:]
    even = (jnp.arange(H) % 2 == 0)[None, :, None]
    wv = jnp.concatenate([jnp.where(even, wv_h, 0.0), jnp.where(even, 0.0, wv_h)], axis=-1)
    wv = wv.reshape(MLA_KV_RANK, H * LANES).astype(BF16)
    c1, c0, s0 = tabs
    full2 = lambda shape: pl.BlockSpec(shape, lambda b, i: (0,) * len(shape))
    tab_spec = pl.BlockSpec((tm, LANES), lambda b, i: (i, 0))
    hd_spec = pl.BlockSpec((1, H, tm, LANES), lambda b, i: (b, 0, i, 0))
    hd_shape = jax.ShapeDtypeStruct((bsz, H, seq, LANES), BF16)
    q, k, v = pl.pallas_call(
        _mla_prep_kernel,
        out_shape=(hd_shape, hd_shape, hd_shape),
        grid=(bsz, seq // tm),
        in_specs=[pl.BlockSpec((1, tm, 256), lambda b, i: (b, i, C_CQ // 256)),
                  pl.BlockSpec((1, tm, 256), lambda b, i: (b, i, C_KR // 256)),
                  pl.BlockSpec((1, tm, 128), lambda b, i: (b, i, C_CKV // 128)),
                  full2((1, 256)), full2((1, 128)),
                  full2((256, H * LANES)), full2((256, H * LANES)),
                  full2((128, H * LANES)), full2((128, H * LANES)),
                  tab_spec, tab_spec, tab_spec],
        out_specs=(hd_spec, hd_spec, hd_spec),
        compiler_params=_cparams(("parallel", "parallel")),
        name="mla_prep",
    )(proj3, proj3, proj3, gq, kv_norm.astype(F32).reshape(1, 128), wqa, wqb, wk, wv, c1, c0, s0)

    return pl.pallas_call(
        functools.partial(_mla_flash_kernel, tq=tq),
        out_shape=jax.ShapeDtypeStruct((bsz, seq, H * MLA_V), BF16),
        grid=(bsz, seq // tq),
        in_specs=[pl.BlockSpec((1, H, tq, LANES), lambda b, i: (b, 0, i, 0)),
                  pl.BlockSpec((1, H, seq, LANES), lambda b, i: (b, 0, 0, 0)),
                  pl.BlockSpec((1, H, seq, LANES), lambda b, i: (b, 0, 0, 0))],
        out_specs=pl.BlockSpec((1, tq, H * MLA_V), lambda b, i: (b, i, 0)),
        scratch_shapes=[pltpu.VMEM((H, tq, LANES), F32), pltpu.VMEM((H, tq, LANES), F32),
                        pltpu.VMEM((H // 2, tq, LANES), F32)],
        compiler_params=_cparams(("parallel", "arbitrary")),
        name="mla_flash",
    )(q, k, v)


def _nsa_cmp_kernel(x_ref, pea_ref, peb_ref, w1a_ref, w1b_ref, w2_ref, o_ref):
    x = x_ref[0, 0]
    w1a, w1b = w1a_ref[0], w1b_ref[0]
    bias = _dot(pea_ref[0], w1a)[0:1] + _dot(peb_ref[0], w1b)[0:1]
    a = _dot(x, w1a)
    b = _dot(x, w1b)
    n = b.shape[0]
    pre = a + pltpu.roll(b, n - 1, 0) + bias
    o_ref[0, 0] = _dot(jax.nn.gelu(pre).astype(BF16), w2_ref[0]).astype(o_ref.dtype)


def nsa_compress(kv_cr, cmp_pe, cmp_w1, cmp_w2):
    bsz, _, nch, width = kv_cr.shape
    G, dh = NSA_KV_HEADS, NSA_DIM
    half = CMP_BLOCK // 2
    eye = jnp.eye(G, dtype=F32)
    w1r = cmp_w1.astype(F32).reshape(2, CMP_BLOCK, dh, dh)
    w1a = jnp.einsum('kpde,gh->kpgdhe', w1r[:, :half], eye).reshape(2, width, G * dh).astype(BF16)
    w1b = jnp.einsum('kpde,gh->kpgdhe', w1r[:, half:], eye).reshape(2, width, G * dh).astype(BF16)
    w2 = jnp.einsum('kde,gh->kgdhe', cmp_w2.astype(F32), eye).reshape(2, G * dh, G * dh).astype(BF16)
    pe = cmp_pe.astype(F32)
    pe_g = jnp.broadcast_to(pe[:, :, None, :], (2, CMP_BLOCK, G, dh))
    pea = jnp.broadcast_to(pe_g[:, :half].reshape(2, 1, width), (2, 8, width)).astype(BF16)
    peb = jnp.broadcast_to(pe_g[:, half:].reshape(2, 1, width), (2, 8, width)).astype(BF16)
    kvspec = lambda shape: pl.BlockSpec(shape, lambda b, k: (k,) + (0,) * (len(shape) - 1))
    return pl.pallas_call(
        _nsa_cmp_kernel,
        out_shape=jax.ShapeDtypeStruct((bsz, 2, nch, G * dh), BF16),
        grid=(bsz, 2),
        in_specs=[pl.BlockSpec((1, 1, nch, width), lambda b, k: (b, k, 0, 0)),
                  kvspec((1, 8, width)), kvspec((1, 8, width)),
                  kvspec((1, width, G * dh)), kvspec((1, width, G * dh)),
                  kvspec((1, G * dh, G * dh))],
        out_specs=pl.BlockSpec((1, 1, nch, G * dh), lambda b, k: (b, k, 0, 0)),
        compiler_params=_cparams(("parallel", "parallel")),
        name="nsa_compress",
    )(kv_cr, pea, peb, w1a, w1b, w2)


def _nsa_kernel(rb_far_ref, q_ref, gate_ref, ksl_ref, vsl_ref, kwn_ref, vwn_ref, kvc_ref, bc_ref, bw_ref,
                bs_ref, ov_ref, o_ref, m_ref, l_ref, acc_ref, *, tq, tk, nbs, n_sel):
    qi = pl.program_id(1)
    R, G = NSA_REP, NSA_KV_HEADS
    lane = lax.broadcasted_iota(jnp.int32, (tq, LANES), 1)
    t3 = qi * tq + lax.broadcasted_iota(jnp.int32, (R * tq, 1), 0) % tq
    gs = _sigmoid(gate_ref[0].astype(F32))
    kc = kvc_ref[0, 0]
    vc = kvc_ref[0, 1]
    ov = ov_ref[...]
    blocks_per_tile = tk // SEL_BLOCK
    jj = lax.broadcasted_iota(jnp.int32, (LANES, tk), 0)
    cblk = lax.broadcasted_iota(jnp.int32, (LANES, tk), 1) // SEL_BLOCK

    def stack(fn):
        return jnp.concatenate([fn(r) for r in range(R)], axis=0)

    qg = [stack(lambda r: q_ref[0, :, (g * R + r) * LANES:(g * R + r + 1) * LANES]) for g in range(G)]

    valid = t3 >= (CMP_BLOCK - 1)
    blk = lax.broadcasted_iota(jnp.int32, (nbs, tq), 0)
    tl = qi * tq + lax.broadcasted_iota(jnp.int32, (nbs, tq), 1)
    cur = tl // SEL_BLOCK
    forced = (blk == 0) | (blk == cur) | (blk == cur - 1)
    future = blk * SEL_BLOCK > tl
    o_c, selb = [], []
    for g in range(G):
        s = _dot_t(qg[g], kc) + stack(lambda r: bc_ref[g * R + r])
        m = jnp.max(s, axis=-1, keepdims=True)
        p = jnp.where(valid, jnp.exp(s - m), 0.0)
        l = jnp.where(valid, jnp.sum(p, axis=-1, keepdims=True), 1.0)
        pc = p / l
        o_c.append(_dot(pc.astype(BF16), vc))
        psum = pc[0:tq] + pc[tq:2 * tq] + pc[2 * tq:3 * tq]
        p_hi = psum.astype(BF16)
        p_lo = (psum - p_hi.astype(F32)).astype(BF16)
        imp = (_dot_t(ov, p_hi) + _dot_t(ov, p_lo))[:nbs]
        imp = jnp.where(forced, FORCE, jnp.where(future, -FORCE, imp))
        rank = jnp.zeros((nbs, tq), F32)
        for i in range(nbs):
            ri = imp[i:i + 1, :]
            beats = (ri > imp) | ((ri == imp) & (blk > i))
            rank = rank + jnp.where(beats, 1.0, 0.0)
        sel = jnp.where(rank < n_sel, 1.0, 0.0)
        sel = jnp.concatenate([sel, jnp.zeros((LANES - nbs, tq), F32)], axis=0).T
        selb.append(sel.astype(BF16))

    m_ref[...] = jnp.full_like(m_ref, NEG_INF)
    l_ref[...] = jnp.zeros_like(l_ref)
    acc_ref[...] = jnp.zeros_like(acc_ref)

    def sel_tile(kt, bias_fn):
        ks = pl.ds(pl.multiple_of(kt * tk, tk), tk)
        e = jnp.where(jj == blocks_per_tile * kt + cblk, 1.0, 0.0).astype(BF16)
        k_t = ksl_ref[0, ks, :]
        v_t = vsl_ref[0, ks, :]
        for g in range(G):
            madd = (_dot(selb[g], e) - 1.0) * (-NEG_INF)
            s = _dot_t(qg[g], k_t) + stack(lambda r: madd + bias_fn(g * R + r))
            m_prev = m_ref[g]
            m_new = jnp.maximum(m_prev, jnp.max(s, axis=-1, keepdims=True))
            alpha = jnp.exp(m_prev - m_new)
            p = jnp.exp(s - jnp.tile(m_new, (1, tk // LANES)))
            l_ref[g] = alpha * l_ref[g] + jnp.sum(p, axis=-1, keepdims=True)
            m_ref[g] = m_new
            acc_ref[g] = acc_ref[g] * alpha + _dot(p.astype(BF16), v_t)

    def far_body(kt, c):
        sel_tile(kt, lambda h: rb_far_ref[h])
        return c

    lax.fori_loop(0, jnp.maximum((qi - 1) // 2, 0), far_body, 0)
    odd = qi % 2

    @pl.when((odd == 0) & (qi >= 2))
    def _():
        sel_tile(qi // 2 - 1, lambda h: bs_ref[h, 2])

    sel_tile(qi // 2, lambda h: bs_ref[h, odd])
    o_s = [acc_ref[g] / l_ref[g] for g in range(G)]

    c = jnp.minimum(qi, 2)
    ws = pl.ds(pl.multiple_of(jnp.maximum(qi - 2, 0) * tq, tq), 3 * tq)
    k_w = kwn_ref[0, ws, :]
    v_w = vwn_ref[0, ws, :]
    o_w = []
    for g in range(G):
        s = _dot_t(qg[g], k_w) + stack(lambda r: bw_ref[g * R + r, c])
        p = jnp.exp(s - jnp.max(s, axis=-1, keepdims=True))
        o_w.append(_dot(p.astype(BF16), v_w) / jnp.sum(p, axis=-1, keepdims=True))

    for r in range(R):
        rs = slice(r * tq, (r + 1) * tq)
        res = []
        for g in range(G):
            h = g * R + r
            res.append(gs[:, h:h + 1] * o_c[g][rs] + gs[:, NSA_HEADS + h:NSA_HEADS + h + 1] * o_s[g][rs]
                       + gs[:, 2 * NSA_HEADS + h:2 * NSA_HEADS + h + 1] * o_w[g][rs])
        o_ref[0, :, r * LANES:(r + 1) * LANES] = jnp.where(lane < NSA_DIM, res[0], res[1]).astype(o_ref.dtype)


def _t5_bucket(dist):
    n = jnp.maximum(dist, 0)
    exact = REL_BUCKETS // 2
    nf = jnp.maximum(n, exact).astype(F32)
    large = exact + (jnp.log(nf / exact) / math.log(REL_MAX_DIST / exact)
                     * (REL_BUCKETS - exact)).astype(jnp.int32)
    return jnp.where(n < exact, n, jnp.minimum(large, REL_BUCKETS - 1))


def _nsa_bias_tables(rel_bias, seq, tq, tk):
    rb = rel_bias.astype(F32).T

    def by_dist(dist, ok):
        bucket = _t5_bucket(dist)[None]
        out = jnp.zeros((NSA_HEADS,) + dist.shape, F32)
        for k in range(REL_BUCKETS):
            out = jnp.where(bucket == k, rb[:, k].reshape((NSA_HEADS,) + (1,) * dist.ndim), out)
        return jnp.where(ok[None], out, NEG_INF)

    i = jnp.arange(tq)[:, None]
    t = jnp.arange(seq)[:, None]
    dist_c = t - (jnp.arange(LANES)[None, :] * CMP_STRIDE + CMP_BLOCK - 1)
    bc = by_dist(dist_c, dist_c >= 0)
    jw = jnp.arange(3 * tq)[None, :]
    bw = jnp.stack([by_dist(tq * c + i - jw, (tq * c + i - jw >= 0) & (tq * c + i - jw < WINDOW))
                    for c in range(3)], axis=1)
    js = jnp.arange(tk)[None, :]
    bs = jnp.stack([by_dist(tq * c + i - js, tq * c + i - js >= 0) for c in range(3)], axis=1)
    ci = jnp.arange(LANES)[:, None]
    sj = jnp.arange(LANES)[None, :]
    nbs = seq // SEL_BLOCK
    ov = ((ci * CMP_STRIDE <= sj * SEL_BLOCK + SEL_BLOCK - 1)
          & (ci * CMP_STRIDE + CMP_BLOCK - 1 >= sj * SEL_BLOCK)
          & (ci < seq // CMP_STRIDE - 1) & (sj < nbs))
    return rb[:, REL_BUCKETS - 1], bc, bw, bs, ov.T.astype(BF16)


def nsa_mixer(proj3, kvc, tables, *, tq, tk):
    bsz, seq, _ = proj3.shape
    assert tq == LANES and tk == 2 * tq and seq // CMP_STRIDE == LANES and tq * 2 == WINDOW and seq % tk == 0
    rb_far, bc, bw, bs, ov = tables
    nbs = seq // SEL_BLOCK
    H = NSA_HEADS
    slab = lambda j: pl.BlockSpec((1, seq, LANES), lambda b, i: (b, 0, C_NKV // LANES + j))
    const = lambda shape: pl.BlockSpec(shape, lambda b, i: (0,) * len(shape))
    return pl.pallas_call(
        functools.partial(_nsa_kernel, tq=tq, tk=tk, nbs=nbs, n_sel=min(SEL_TOPN, nbs)),
        out_shape=jax.ShapeDtypeStruct((bsz, seq, H * NSA_DIM), BF16),
        grid=(bsz, seq // tq),
        in_specs=[pl.BlockSpec(memory_space=pltpu.SMEM),
                  pl.BlockSpec((1, tq, H * LANES), lambda b, i: (b, i, C_NQ // (H * LANES))),
                  pl.BlockSpec((1, tq, LANES), lambda b, i: (b, i, C_GATE // LANES)),
                  slab(2), slab(3), slab(4), slab(5),
                  pl.BlockSpec((1, 2, LANES, LANES), lambda b, i: (b, 0, 0, 0)),
                  pl.BlockSpec((H, tq, LANES), lambda b, i: (0, i, 0)),
                  const((H, 3, tq, 3 * tq)), const((H, 3, tq, tk)),
                  const((LANES, LANES))],
        out_specs=pl.BlockSpec((1, tq, H * NSA_DIM), lambda b, i: (b, i, 0)),
        scratch_shapes=[pltpu.VMEM((NSA_KV_HEADS, NSA_REP * tq, LANES), F32),
                        pltpu.VMEM((NSA_KV_HEADS, NSA_REP * tq, LANES), F32),
                        pltpu.VMEM((NSA_KV_HEADS, NSA_REP * tq, LANES), F32)],
        compiler_params=_cparams(("parallel", "arbitrary")),
        name="nsa_attention",
    )(rb_far, proj3, proj3, proj3, proj3, proj3, proj3, kvc, bc, bw, bs, ov)


def _outproj_kernel(ys_ref, ym_ref, yn_ref, h_ref, g1_ref, g2_ref, g3_ref, w1_ref, w2_ref, w3_ref, o_ref):
    def part(y_ref, g_ref, w_ref):
        y = y_ref[...].astype(F32)
        return _dot(_rms(y, g_ref[...], y.shape[-1]).astype(BF16), w_ref[...])

    o_ref[...] = h_ref[...] + part(ys_ref, g1_ref, w1_ref) + part(ym_ref, g2_ref, w2_ref) + part(yn_ref, g3_ref, w3_ref)


def out_projection(y_ssm, y_mla, y_nsa, h, gains, weights, *, tm):
    m, d = h.shape
    row = lambda w: pl.BlockSpec((tm, w), lambda i: (i, 0))
    full = lambda a: pl.BlockSpec(a.shape, lambda i: (0, 0))
    gains = [g.reshape(1, -1).astype(F32) for g in gains]
    return pl.pallas_call(
        _outproj_kernel,
        out_shape=jax.ShapeDtypeStruct((m, d), F32),
        grid=(m // tm,),
        in_specs=[row(y_ssm.shape[1]), row(y_mla.shape[1]), row(y_nsa.shape[1]), row(d)]
                 + [full(g) for g in gains] + [full(w) for w in weights],
        out_specs=row(d),
        compiler_params=_cparams(("parallel",)),
        name="out_projection",
    )(y_ssm, y_mla, y_nsa, h, *gains, *weights)


def _xattn_kernel(h_ref, kv_ref, g_ref, wq_ref, wo_ref, o_ref, *, dh):
    h = h_ref[0]
    xn = _rms(h, g_ref[...], h.shape[-1]).astype(BF16)
    q = _dot(xn, wq_ref[...]).astype(BF16)
    hw = XATTN_HEADS * dh
    outs = []
    for hd in range(XATTN_HEADS):
        k = kv_ref[0, :, hd * dh:(hd + 1) * dh]
        v = kv_ref[0, :, hw + hd * dh:hw + (hd + 1) * dh]
        s = _dot_t(q[:, hd * dh:(hd + 1) * dh], k)
        p = jnp.exp(s - jnp.max(s, axis=-1, keepdims=True))
        p = p / jnp.sum(p, axis=-1, keepdims=True)
        outs.append(_dot(p.astype(BF16), v).astype(BF16))
    o = jnp.concatenate(outs, axis=-1)
    o_ref[0] = h + _dot(o, wo_ref[...])


def cross_attention(h3, kv3, g_x, wq, wo, *, tm):
    bsz, seq, d = h3.shape
    m = kv3.shape[1]
    dh = d // XATTN_HEADS
    wq_s = (wq.astype(F32) * dh ** -0.5).astype(BF16)
    const = lambda shape: pl.BlockSpec(shape, lambda b, i: (0,) * len(shape))
    return pl.pallas_call(
        functools.partial(_xattn_kernel, dh=dh),
        out_shape=jax.ShapeDtypeStruct((bsz, seq, d), F32),
        grid=(bsz, seq // tm),
        in_specs=[pl.BlockSpec((1, tm, d), lambda b, i: (b, i, 0)),
                  pl.BlockSpec((1, m, 2 * d), lambda b, i: (b, 0, 0)),
                  const((1, d)), const((d, d)), const((d, d))],
        out_specs=pl.BlockSpec((1, tm, d), lambda b, i: (b, i, 0)),
        compiler_params=_cparams(("parallel", "parallel")),
        name="cross_attention",
    )(h3, kv3, g_x.reshape(1, d).astype(F32), wq_s, wo.astype(BF16))


def _ffn_kernel(h_ref, g_ref, wg_ref, wu_ref, wd_ref, o_ref, xn_ref, acc_ref):
    j = pl.program_id(1)

    @pl.when(j == 0)
    def _():
        h = h_ref[...]
        xn_ref[...] = _rms(h, g_ref[...], h.shape[-1]).astype(BF16)
        acc_ref[...] = h

    xn = xn_ref[...]
    a = _silu(_dot(xn, wg_ref[...])) * _dot(xn, wu_ref[...])
    acc_ref[...] += _dot(a.astype(BF16), wd_ref[...])

    @pl.when(j == pl.num_programs(1) - 1)
    def _():
        o_ref[...] = acc_ref[...]


def dense_ffn(h, g, wg, wu, wd, *, tm, tf):
    m, d = h.shape
    ff = wg.shape[1]
    return pl.pallas_call(
        _ffn_kernel,
        out_shape=jax.ShapeDtypeStruct((m, d), F32),
        grid=(m // tm, ff // tf),
        in_specs=[pl.BlockSpec((tm, d), lambda i, j: (i, 0)),
                  pl.BlockSpec((1, d), lambda i, j: (0, 0)),
                  pl.BlockSpec((d, tf), lambda i, j: (0, j)),
                  pl.BlockSpec((d, tf), lambda i, j: (0, j)),
                  pl.BlockSpec((tf, d), lambda i, j: (j, 0))],
        out_specs=pl.BlockSpec((tm, d), lambda i, j: (i, 0)),
        scratch_shapes=[pltpu.VMEM((tm, d), BF16), pltpu.VMEM((tm, d), F32)],
        compiler_params=_cparams(("parallel", "arbitrary")),
        name="dense_ffn",
    )(h, g.reshape(1, d).astype(F32), wg.astype(BF16), wu.astype(BF16), wd.astype(BF16))


def _router_kernel(h_ref, g_ref, wr_hi_ref, wr_lo_ref, xn_ref, info_ref, cnt_ref, carry_ref, *, tm):
    i = pl.program_id(0)

    @pl.when(i == 0)
    def _():
        carry_ref[...] = jnp.zeros_like(carry_ref)

    h = h_ref[...]
    xn = _rms(h, g_ref[...], h.shape[-1])
    xn_ref[...] = xn
    x_hi = xn.astype(BF16)
    x_lo = (xn - x_hi.astype(F32)).astype(BF16)
    logits = _dot(x_hi, wr_hi_ref[...]) + _dot(x_lo, wr_hi_ref[...]) + _dot(x_hi, wr_lo_ref[...])
    lane = lax.broadcasted_iota(jnp.int32, (tm, LANES), 1)
    lanef = lane.astype(F32)
    logits = jnp.where(lane < N_EXPERTS, logits, NEG_INF)
    m1 = jnp.max(logits, axis=-1, keepdims=True)
    i1 = jnp.min(jnp.where(logits == m1, lanef, float(LANES)), axis=-1, keepdims=True)
    rest = jnp.where(lanef == i1, NEG_INF, logits)
    m2 = jnp.max(rest, axis=-1, keepdims=True)
    i2 = jnp.min(jnp.where(rest == m2, lanef, float(LANES)), axis=-1, keepdims=True)
    e2 = jnp.exp(m2 - m1)
    w1 = 1.0 / (1.0 + e2)
    w2 = e2 / (1.0 + e2)
    oh1 = lanef == i1
    oh2 = lanef == i2
    oh = jnp.where(oh1 | oh2, 1.0, 0.0)
    rr = lax.broadcasted_iota(jnp.int32, (tm, tm), 0)
    cc = lax.broadcasted_iota(jnp.int32, (tm, tm), 1)
    tri = jnp.where(cc < rr, 1.0, 0.0).astype(BF16)
    before = _dot(tri, oh.astype(BF16)) + carry_ref[0:1, :]
    r1 = jnp.sum(jnp.where(oh1, before, 0.0), axis=-1, keepdims=True)
    r2 = jnp.sum(jnp.where(oh2, before, 0.0), axis=-1, keepdims=True)
    carry_ref[...] = carry_ref[...] + jnp.sum(oh, axis=0, keepdims=True)
    info = jnp.where(lane == 0, i1, jnp.where(lane == 1, i2, jnp.where(lane == 2, w1, jnp.where(
        lane == 3, w2, jnp.where(lane == 4, r1, jnp.where(lane == 5, r2, 0.0))))))
    info_ref[...] = info
    cnt_ref[...] = carry_ref[...]


def moe_router(h, g, router, *, tm):
    m, d = h.shape
    wr = jnp.pad(router.astype(F32), ((0, 0), (0, LANES - N_EXPERTS)))
    wr_hi = wr.astype(BF16)
    wr_lo = (wr - wr_hi.astype(F32)).astype(BF16)
    return pl.pallas_call(
        functools.partial(_router_kernel, tm=tm),
        out_shape=(jax.ShapeDtypeStruct((m, d), F32), jax.ShapeDtypeStruct((m, LANES), F32),
                   jax.ShapeDtypeStruct((8, LANES), F32)),
        grid=(m // tm,),
        in_specs=[pl.BlockSpec((tm, d), lambda i: (i, 0)),
                  pl.BlockSpec((1, d), lambda i: (0, 0)),
                  pl.BlockSpec((d, LANES), lambda i: (0, 0)),
                  pl.BlockSpec((d, LANES), lambda i: (0, 0))],
        out_specs=(pl.BlockSpec((tm, d), lambda i: (i, 0)),
                   pl.BlockSpec((tm, LANES), lambda i: (i, 0)),
                   pl.BlockSpec((8, LANES), lambda i: (0, 0))),
        scratch_shapes=[pltpu.VMEM((8, LANES), F32)],
        compiler_params=_cparams(("arbitrary",)),
        name="moe_router",
    )(h, g.reshape(1, d).astype(F32), wr_hi, wr_lo)


def _row_gather_copy(src_hbm, row, dst, slot, sem):
    return pltpu.make_async_copy(src_hbm.at[pl.ds(row, 1), :], dst.at[pl.ds(slot, 1), :], sem)


def _moe_ffn_kernel(src_ref, texp_ref, nact_ref, x_hbm, wg_ref, wu_ref, wd_ref, o_ref,
                    xbuf, xbf, acc_ref, sem, *, tm):
    i = pl.program_id(0)
    j = pl.program_id(1)
    nj = pl.num_programs(1)
    nact = nact_ref[0]
    active = i < nact
    cur = i % 2

    def start_gather(tile, buf):
        def issue(s, c):
            _row_gather_copy(x_hbm, src_ref[tile * tm + s], xbuf.at[buf], s, sem.at[buf]).start()
            return c

        lax.fori_loop(0, tm, issue, 0, unroll=8)

    @pl.when((i == 0) & (j == 0))
    def _():
        start_gather(0, 0)

    @pl.when(active & (j == 0))
    def _():
        pltpu.make_async_copy(x_hbm.at[pl.ds(0, tm), :], xbuf.at[cur], sem.at[cur]).wait()
        xbf[...] = xbuf[cur].astype(BF16)
        acc_ref[...] = jnp.zeros_like(acc_ref)

    @pl.when((j == 1) & (i + 1 < nact))
    def _():
        start_gather(i + 1, 1 - cur)

    @pl.when(active)
    def _():
        x = xbf[...]
        a = _silu(_dot(x, wg_ref[0, 0])) * _dot(x, wu_ref[0, 0])
        acc_ref[...] += _dot(a.astype(BF16), wd_ref[0, 0])

    @pl.when(j == nj - 1)
    def _():
        o_ref[...] = jnp.where(active, acc_ref[...], 0.0)


def moe_expert_ffn(xn, src, tile_expert, n_active, wg, wu, wd, *, tm, tf):
    n_slots = src.shape[0]
    d = xn.shape[1]
    ne, _, ff = wg.shape
    nj = ff // tf
    assert nj >= 2 and nj * tf == ff
    wg = wg.reshape(ne, d, nj, tf).transpose(0, 2, 1, 3).astype(BF16)
    wu = wu.reshape(ne, d, nj, tf).transpose(0, 2, 1, 3).astype(BF16)
    wd = wd.reshape(ne, nj, tf, d).astype(BF16)

    def wmap(i, j, src, texp, nact):
        return (texp[i], jnp.where(i < nact[0], j, nj - 1), 0, 0)

    return pl.pallas_call(
        functools.partial(_moe_ffn_kernel, tm=tm),
        out_shape=jax.ShapeDtypeStruct((n_slots, d), F32),
        grid_spec=pltpu.PrefetchScalarGridSpec(
            num_scalar_prefetch=3,
            grid=(n_slots // tm, nj),
            in_specs=[pl.BlockSpec(memory_space=pl.ANY),
                      pl.BlockSpec((1, 1, d, tf), wmap),
                      pl.BlockSpec((1, 1, d, tf), wmap),
                      pl.BlockSpec((1, 1, tf, d), wmap)],
            out_specs=pl.BlockSpec((tm, d), lambda i, j, *_: (i, 0)),
            scratch_shapes=[pltpu.VMEM((2, tm, d), F32), pltpu.VMEM((tm, d), BF16),
                            pltpu.VMEM((tm, d), F32), pltpu.SemaphoreType.DMA((2,))]),
        compiler_params=_cparams(("arbitrary", "arbitrary")),
        name="moe_expert_ffn",
    )(src, tile_expert, n_active, xn, wg, wu, wd)


def _moe_combine_kernel(pos_ref, h_ref, info_ref, ys_hbm, g_ref, o_ref, buf, sem, *, tm, final_norm):
    i = pl.program_id(0)
    n = pl.num_programs(0)
    cur = i % 2

    def start_gather(tile, b):
        def issue(s, c):
            for k in range(2):
                _row_gather_copy(ys_hbm, pos_ref[2 * (tile * tm + s) + k], buf.at[b, k], s, sem.at[b]).start()
            return c

        lax.fori_loop(0, tm, issue, 0, unroll=8)

    @pl.when(i == 0)
    def _():
        start_gather(0, 0)

    @pl.when(i + 1 < n)
    def _():
        start_gather(i + 1, 1 - cur)

    for k in range(2):
        pltpu.make_async_copy(ys_hbm.at[pl.ds(0, tm), :], buf.at[cur, k], sem.at[cur]).wait()
    info = info_ref[...]
    y = h_ref[...] + info[:, 2:3] * buf[cur, 0] + info[:, 3:4] * buf[cur, 1]
    if final_norm:
        y = _rms(y, g_ref[...], y.shape[-1])
    o_ref[...] = y


def moe_combine(h, info, ys, pos_flat, g_final, *, tm, final_norm):
    m, d = h.shape
    return pl.pallas_call(
        functools.partial(_moe_combine_kernel, tm=tm, final_norm=final_norm),
        out_shape=jax.ShapeDtypeStruct((m, d), F32),
        grid_spec=pltpu.PrefetchScalarGridSpec(
            num_scalar_prefetch=1,
            grid=(m // tm,),
            in_specs=[pl.BlockSpec((tm, d), lambda i, *_: (i, 0)),
                      pl.BlockSpec((tm, LANES), lambda i, *_: (i, 0)),
                      pl.BlockSpec(memory_space=pl.ANY),
                      pl.BlockSpec((1, d), lambda i, *_: (0, 0))],
            out_specs=pl.BlockSpec((tm, d), lambda i, *_: (i, 0)),
            scratch_shapes=[pltpu.VMEM((2, 2, tm, d), F32), pltpu.SemaphoreType.DMA((2,))]),
        compiler_params=_cparams(("arbitrary",)),
        name="moe_combine",
    )(pos_flat, h, info, ys, g_final.reshape(1, d).astype(F32))


def moe_layer(h, g, router, wg, wu, wd, g_final, *, final_norm, tm_r=512, tm_g=512, tf=896, tm_c=256):
    m, d = h.shape
    xn, info, cnt = moe_router(h, g, router, tm=tm_r)
    e_idx = info[:, 0:2].astype(jnp.int32)
    rank = info[:, 4:6].astype(jnp.int32)
    counts = cnt[0, :N_EXPERTS].astype(jnp.int32)
    tiles_per = (counts + tm_g - 1) // tm_g
    tile_end = jnp.cumsum(tiles_per)
    seg_start = (tile_end - tiles_per) * tm_g
    pos = rank
    for e in range(N_EXPERTS):
        pos = pos + jnp.where(e_idx == e, seg_start[e], 0)
    n_tiles = (2 * m) // tm_g + N_EXPERTS
    n_slots = n_tiles * tm_g
    tok = jnp.broadcast_to(jnp.arange(m, dtype=jnp.int32)[:, None], (m, 2))
    src = jnp.zeros((n_slots,), jnp.int32).at[pos.reshape(-1)].set(tok.reshape(-1))
    n_active = tile_end[-1:].astype(jnp.int32)
    tile_ids = jnp.minimum(jnp.arange(n_tiles, dtype=jnp.int32), n_active[0] - 1)
    tile_expert = jnp.sum(tile_ids[:, None] >= tile_end[None, :], axis=1).astype(jnp.int32)
    ys = moe_expert_ffn(xn, src, tile_expert, n_active,
                        wg, wu, wd, tm=tm_g, tf=tf)
    return moe_combine(h, info, ys, pos.reshape(-1).astype(jnp.int32), g_final, tm=tm_c, final_norm=final_norm)


def _final_norm_kernel(h_ref, g_ref, o_ref):
    h = h_ref[...]
    o_ref[...] = _rms(h, g_ref[...], h.shape[-1])


def final_rmsnorm(h, g, *, tm):
    m, d = h.shape
    return pl.pallas_call(
        _final_norm_kernel,
        out_shape=jax.ShapeDtypeStruct((m, d), F32),
        grid=(m // tm,),
        in_specs=[pl.BlockSpec((tm, d), lambda i: (i, 0)), pl.BlockSpec((1, d), lambda i: (0, 0))],
        out_specs=pl.BlockSpec((tm, d), lambda i: (i, 0)),
        compiler_params=_cparams(("parallel",)),
        name="final_rmsnorm",
    )(h, g.reshape(1, d).astype(F32))


def _pack_w_in(w):
    d = w.shape[0]
    w = w.astype(F32)
    o = 0
    u = w[:, o:o + SSM_WIDTH]; o += SSM_WIDTH
    cq = w[:, o:o + MLA_Q_RANK]; o += MLA_Q_RANK
    ckv = w[:, o:o + MLA_KV_RANK]; o += MLA_KV_RANK
    kr = w[:, o:o + MLA_ROPE]; o += MLA_ROPE
    nq = w[:, o:o + NSA_HEADS * NSA_DIM]; o += NSA_HEADS * NSA_DIM
    nkv = w[:, o:o + 6 * NSA_KV_HEADS * NSA_DIM]; o += 6 * NSA_KV_HEADS * NSA_DIM
    gate = w[:, o:o + 3 * NSA_HEADS]
    z = lambda n: jnp.zeros((d, n), F32)
    kr_a = jnp.concatenate([z(MLA_NOPE), kr, z(LANES - MLA_NOPE - MLA_ROPE)], axis=1)
    kr_b = jnp.concatenate([z(MLA_NOPE), _rot_half_cols(kr), z(LANES - MLA_NOPE - MLA_ROPE)], axis=1)
    nq_h = (nq * NSA_DIM ** -0.5).reshape(d, NSA_KV_HEADS, NSA_REP, NSA_DIM)
    zq = jnp.zeros((d, NSA_REP, NSA_DIM), F32)
    nq_p = jnp.concatenate([
        jnp.concatenate([nq_h[:, 0], zq], axis=-1).reshape(d, NSA_REP * LANES),
        jnp.concatenate([zq, nq_h[:, 1]], axis=-1).reshape(d, NSA_REP * LANES)], axis=1)
    packed = jnp.concatenate([u, cq, z(256 - MLA_Q_RANK), kr_a, kr_b, nq_p, nkv, ckv,
                              gate, z(LANES - 3 * NSA_HEADS)], axis=1)
    assert packed.shape[1] == IN_COLS_PACKED
    return packed.astype(BF16)


def _rg_order(a):
    rest = a.shape[1:]
    return a.reshape((NSA_KV_HEADS, NSA_REP, NSA_DIM) + rest).swapaxes(0, 1).reshape((-1,) + rest)


def kernel(x, mem, w_in, w_out, mix_norm, out_norm, ssm_a_re, ssm_a_im, ssm_b_re, ssm_b_im, ssm_c_re, ssm_c_im, ssm_d, ssm_log_dt, ssm_w_glu, mla_q_norm, mla_w_uq, mla_kv_norm, mla_w_ukv, nsa_cmp_pe, nsa_cmp_w1, nsa_cmp_w2, rel_bias, xattn_norm, mem_norm, xattn_wq, xattn_wkv, xattn_wo, ffn_norm, dense_w_gate, dense_w_up, dense_w_down, moe_router, moe_w_gate, moe_w_up, moe_w_down, final_norm):
    bsz, seq, d = x.shape
    depth = w_in.shape[0]
    T = bsz * seq
    nmem = mem.shape[1]
    tq_nsa, tk_nsa = LANES, 2 * LANES
    rope_tabs = _rope_tables(seq)
    nsa_tabs = _nsa_bias_tables(rel_bias, seq, tq_nsa, tk_nsa)
    o1, o2 = SSM_WIDTH, SSM_WIDTH + MLA_HEADS * MLA_V
    mem2 = mem.reshape(bsz * nmem, d)
    h = x.reshape(T, d)
    for l in range(depth):
        proj = norm_matmul(h, mix_norm[l], _pack_w_in(w_in[l]), tm=512, tn=IN_COLS_PACKED, out_dtype=BF16)
        proj3 = proj.reshape(bsz, seq, IN_COLS_PACKED)
        u_tm = proj3[:, :, C_U:C_U + SSM_WIDTH].transpose(1, 0, 2).reshape(seq * bsz, SSM_WIDTH)
        y_ssm = ssm_mixer(u_tm, ssm_a_re[l], ssm_a_im[l], ssm_b_re[l], ssm_b_im[l], ssm_c_re[l], ssm_c_im[l],
                          ssm_d[l], ssm_log_dt[l], ssm_w_glu[l], nb=bsz, tc=64)
        y_ssm = y_ssm.reshape(seq, bsz, SSM_WIDTH).transpose(1, 0, 2).reshape(T, SSM_WIDTH)
        y_mla = mla_mixer(proj3, mla_q_norm[l], mla_w_uq[l], mla_kv_norm[l], mla_w_ukv[l], rope_tabs,
                          tm=512, tq=512).reshape(T, -1)
        nch = seq // CMP_STRIDE
        kv_cr = jnp.stack([proj3[:, :, C_NKV:C_NKV + LANES].reshape(bsz, nch, CMP_STRIDE * LANES),
                           proj3[:, :, C_NKV + LANES:C_NKV + 2 * LANES].reshape(bsz, nch, CMP_STRIDE * LANES)],
                          axis=1)
        kvc = nsa_compress(kv_cr, nsa_cmp_pe[l], nsa_cmp_w1[l], nsa_cmp_w2[l])
        y_nsa = nsa_mixer(proj3, kvc, nsa_tabs, tq=tq_nsa, tk=tk_nsa).reshape(T, -1)
        g_out = out_norm[l]
        wo_l = w_out[l]
        h = out_projection(y_ssm, y_mla, y_nsa, h,
                           [g_out[:o1], g_out[o1:o2], _rg_order(g_out[o2:])],
                           [wo_l[:o1].astype(BF16), wo_l[o1:o2].astype(BF16), _rg_order(wo_l[o2:]).astype(BF16)],
                           tm=512)
        kv_mem = norm_matmul(mem2, mem_norm[l], xattn_wkv[l].astype(BF16), tm=256, tn=512, out_dtype=BF16)
        h = cross_attention(h.reshape(bsz, seq, d), kv_mem.reshape(bsz, nmem, 2 * d), xattn_norm[l],
                            xattn_wq[l], xattn_wo[l], tm=256).reshape(T, d)
        last = l == depth - 1
        if l % 2 == 0:
            h = dense_ffn(h, ffn_norm[l], dense_w_gate[l // 2], dense_w_up[l // 2], dense_w_down[l // 2],
                          tm=512, tf=1408)
            if last:
                h = final_rmsnorm(h, final_norm, tm=512)
        else:
            h = moe_layer(h, ffn_norm[l], moe_router[l // 2], moe_w_gate[l // 2], moe_w_up[l // 2],
                          moe_w_down[l // 2], final_norm, final_norm=last)
    return h.reshape(bsz, seq, d)
```

```python
import functools
import math

import jax
import jax.numpy as jnp
from jax import lax
from jax.experimental import pallas as pl
from jax.experimental.pallas import tpu as pltpu

F32 = jnp.float32
BF16 = jnp.bfloat16

HEAD_DIM = 64
SSM_WIDTH = 256
SSM_CH = 16
SSM_GROUPS = 16
SSM_STATE = 64
MLA_HEADS = 6
MLA_NOPE = 64
MLA_ROPE = 32
MLA_V = 64
MLA_Q_RANK = 192
MLA_KV_RANK = 128
NSA_HEADS = 6
NSA_KV_HEADS = 2
NSA_REP = 3
NSA_DIM = 64
CMP_BLOCK = 32
CMP_STRIDE = 16
SEL_BLOCK = 64
SEL_TOPN = 8
WINDOW = 256
REL_BUCKETS = 32
REL_MAX_DIST = 128
XATTN_HEADS = 4
N_EXPERTS = 8
ROPE_THETA = 10000.0
EPS = 1e-6
NEG_INF = -1e30
FORCE = 1e9

LANES = 128
VMEM_LIMIT = 56 * 1024 * 1024

C_U, C_CQ, C_KR, C_NQ, C_NKV, C_CKV, C_GATE = 0, 256, 512, 768, 1536, 2304, 2432
IN_COLS_PACKED = 2560


def _cparams(sem):
    return pltpu.CompilerParams(dimension_semantics=sem, vmem_limit_bytes=VMEM_LIMIT)


def _dot(a, b):
    return jnp.dot(a, b, preferred_element_type=F32)


def _dot_t(a, b):
    return lax.dot_general(a, b, (((1,), (1,)), ((), ())), preferred_element_type=F32)


def _rms(x, g, n):
    ms = jnp.sum(x * x, axis=-1, keepdims=True) * (1.0 / n)
    return x * lax.rsqrt(ms + EPS) * g


def _sigmoid(x):
    return 1.0 / (1.0 + jnp.exp(-x))


def _silu(x):
    return x * _sigmoid(x)


def _norm_mm_kernel(x_ref, g_ref, w_ref, o_ref, xn_ref):
    @pl.when(pl.program_id(1) == 0)
    def _():
        x = x_ref[...].astype(F32)
        xn_ref[...] = _rms(x, g_ref[...], x.shape[-1]).astype(BF16)

    o_ref[...] = _dot(xn_ref[...], w_ref[...]).astype(o_ref.dtype)


def norm_matmul(x, g, w, *, tm, tn, out_dtype):
    m, k = x.shape
    n = w.shape[1]
    return pl.pallas_call(
        _norm_mm_kernel,
        out_shape=jax.ShapeDtypeStruct((m, n), out_dtype),
        grid=(m // tm, n // tn),
        in_specs=[pl.BlockSpec((tm, k), lambda i, j: (i, 0)),
                  pl.BlockSpec((1, k), lambda i, j: (0, 0)),
                  pl.BlockSpec((k, tn), lambda i, j: (0, j))],
        out_specs=pl.BlockSpec((tm, tn), lambda i, j: (i, j)),
        scratch_shapes=[pltpu.VMEM((tm, k), BF16)],
        compiler_params=_cparams(("parallel", "arbitrary")),
        name="norm_matmul",
    )(x, g.reshape(1, k).astype(F32), w)


def _ssm_kernel(u_ref, bbr_ref, bbi_ref, ar_ref, ai_ref, ccr_ref, cci_ref, d_ref, wglu_ref,
                o_ref, hr_ref, hi_ref, cr_ref, ci_ref, *, tc, nb):
    @pl.when(pl.program_id(0) == 0)
    def _():
        cr_ref[...] = jnp.zeros_like(cr_ref)
        ci_ref[...] = jnp.zeros_like(ci_ref)

    u = u_ref[...]
    hr_ref[...] = _dot(u, bbr_ref[...])
    hi_ref[...] = _dot(u, bbi_ref[...])
    gp = ar_ref.shape[-1]
    ar = jnp.broadcast_to(ar_ref[...], (nb, gp))
    ai = jnp.broadcast_to(ai_ref[...], (nb, gp))

    def step(t, carry):
        hr, hi = carry
        rows = pl.ds(pl.multiple_of(t * nb, nb), nb)
        nr = ar * hr - ai * hi + hr_ref[rows, :]
        ni = ar * hi + ai * hr + hi_ref[rows, :]
        hr_ref[rows, :] = nr
        hi_ref[rows, :] = ni
        return nr, ni

    hr, hi = lax.fori_loop(0, tc, step, (cr_ref[...], ci_ref[...]))
    cr_ref[...] = hr
    ci_ref[...] = hi
    y = (_dot(hr_ref[...].astype(BF16), ccr_ref[...]) + _dot(hi_ref[...].astype(BF16), cci_ref[...])
         + d_ref[...] * u.astype(F32))
    y = jax.nn.gelu(y)
    z = _dot(y.astype(BF16), wglu_ref[...])
    o_ref[...] = (y * _sigmoid(z)).astype(o_ref.dtype)


def ssm_mixer(u_tm, a_re, a_im, b_re, b_im, c_re, c_im, d, log_dt, w_glu, *, nb, tc):
    rows = u_tm.shape[0]
    G, P, C = SSM_GROUPS, SSM_STATE, SSM_CH
    dt = jnp.exp(log_dt.astype(F32))[:, None]
    lr, li = a_re.astype(F32), a_im.astype(F32)
    mag = jnp.exp(lr * dt)
    ab_r, ab_i = mag * jnp.cos(li * dt), mag * jnp.sin(li * dt)
    den = lr * lr + li * li
    nr = ab_r - 1.0
    f_r = (nr * lr + ab_i * li) / den
    f_i = (ab_i * lr - nr * li) / den
    br, bi = b_re.astype(F32), b_im.astype(F32)
    bb_r = f_r[..., None] * br - f_i[..., None] * bi
    bb_i = f_r[..., None] * bi + f_i[..., None] * br
    eye = jnp.eye(G, dtype=F32)
    bbr = jnp.einsum('gpc,gh->gchp', bb_r, eye).reshape(G * C, G * P).astype(BF16)
    bbi = jnp.einsum('gpc,gh->gchp', bb_i, eye).reshape(G * C, G * P).astype(BF16)
    ccr = jnp.einsum('gcp,gh->gphc', c_re.astype(F32), eye).reshape(G * P, G * C).astype(BF16)
    cci = jnp.einsum('gcp,gh->gphc', -c_im.astype(F32), eye).reshape(G * P, G * C).astype(BF16)
    gp = G * P
    full = lambda shape: pl.BlockSpec(shape, lambda i: (0,) * len(shape))
    return pl.pallas_call(
        functools.partial(_ssm_kernel, tc=tc, nb=nb),
        out_shape=jax.ShapeDtypeStruct((rows, SSM_WIDTH), BF16),
        grid=(rows // (tc * nb),),
        in_specs=[pl.BlockSpec((tc * nb, SSM_WIDTH), lambda i: (i, 0)),
                  full((G * C, gp)), full((G * C, gp)), full((1, gp)), full((1, gp)),
                  full((gp, G * C)), full((gp, G * C)), full((1, SSM_WIDTH)),
                  full((SSM_WIDTH, SSM_WIDTH))],
        out_specs=pl.BlockSpec((tc * nb, SSM_WIDTH), lambda i: (i, 0)),
        scratch_shapes=[pltpu.VMEM((tc * nb, gp), F32), pltpu.VMEM((tc * nb, gp), F32),
                        pltpu.VMEM((nb, gp), F32), pltpu.VMEM((nb, gp), F32)],
        compiler_params=_cparams(("arbitrary",)),
        name="ssm_mixer",
    )(u_tm, bbr, bbi, ab_r.reshape(1, gp), ab_i.reshape(1, gp), ccr, cci,
      d.reshape(1, SSM_WIDTH).astype(F32), w_glu.astype(BF16))


def _mla_prep_kernel(cq_ref, kr_ref, ckv_ref, gq_ref, gkv_ref, wqa_ref, wqb_ref, wk_ref, wv_ref,
                     c1_ref, c0_ref, s0_ref, q_ref, k_ref, v_ref):
    qn = _rms(cq_ref[0].astype(F32), gq_ref[...], MLA_Q_RANK).astype(BF16)
    qa = _dot(qn, wqa_ref[...])
    qb = _dot(qn, wqb_ref[...])
    kn = _rms(ckv_ref[0].astype(F32), gkv_ref[...], MLA_KV_RANK).astype(BF16)
    ka = _dot(kn, wk_ref[...])
    va = _dot(kn, wv_ref[...])
    kr = kr_ref[0].astype(F32)
    c1, c0, s0 = c1_ref[...], c0_ref[...], s0_ref[...]
    krope = kr[:, :LANES] * c0 + kr[:, LANES:] * s0
    for h in range(MLA_HEADS):
        sl = slice(h * LANES, (h + 1) * LANES)
        q_ref[0, h] = (qa[:, sl] * c1 + qb[:, sl] * s0).astype(BF16)
        k_ref[0, h] = (ka[:, sl] + krope).astype(BF16)
        v_ref[0, h] = va[:, sl].astype(BF16)


def _mla_flash_kernel(q_ref, k_ref, v_ref, o_ref, m_ref, l_ref, acc_ref, *, tq):
    qi = pl.program_id(1)
    m_ref[...] = jnp.full_like(m_ref, NEG_INF)
    l_ref[...] = jnp.zeros_like(l_ref)
    acc_ref[...] = jnp.zeros_like(acc_ref)
    lane = lax.broadcasted_iota(jnp.int32, (tq, LANES), 1)
    rep = tq // LANES

    def tile(kt, masked):
        ks = pl.ds(pl.multiple_of(kt * tq, tq), tq)
        if masked:
            mask = (lax.broadcasted_iota(jnp.int32, (tq, tq), 1)
                    <= lax.broadcasted_iota(jnp.int32, (tq, tq), 0))
        for pr in range(MLA_HEADS // 2):
            hs = (2 * pr, 2 * pr + 1)
            s = [_dot_t(q_ref[0, h], k_ref[0, h, ks, :]) for h in hs]
            if masked:
                s = [jnp.where(mask, x, NEG_INF) for x in s]
            m_prev = [m_ref[h] for h in hs]
            m_new = [jnp.maximum(mp, jnp.max(x, axis=-1, keepdims=True)) for mp, x in zip(m_prev, s)]
            alpha = [jnp.exp(mp - mn) for mp, mn in zip(m_prev, m_new)]
            p = [jnp.exp(x - jnp.tile(mn, (1, rep))) for x, mn in zip(s, m_new)]
            for h, a, pp, mn in zip(hs, alpha, p, m_new):
                l_ref[h] = a * l_ref[h] + jnp.sum(pp, axis=-1, keepdims=True)
                m_ref[h] = mn
            pv = _dot(p[0].astype(BF16), v_ref[0, hs[0], ks, :]) + _dot(p[1].astype(BF16), v_ref[0, hs[1], ks, :])
            acc_ref[pr] = acc_ref[pr] * jnp.where(lane < MLA_V, alpha[0], alpha[1]) + pv

    def body(kt, c):
        tile(kt, False)
        return c

    lax.fori_loop(0, qi, body, 0)
    tile(qi, True)
    for pr in range(MLA_HEADS // 2):
        linv = jnp.where(lane < MLA_V, 1.0 / l_ref[2 * pr], 1.0 / l_ref[2 * pr + 1])
        o_ref[0, :, pr * LANES:(pr + 1) * LANES] = (acc_ref[pr] * linv).astype(o_ref.dtype)


def _rope_tables(seq):
    pos = jnp.arange(seq, dtype=F32)
    inv = 1.0 / (ROPE_THETA ** (jnp.arange(0, MLA_ROPE, 2, dtype=F32) / MLA_ROPE))
    ang = pos[:, None] * inv[None, :]
    cos, sin = jnp.cos(ang), jnp.sin(ang)
    cos2 = jnp.concatenate([cos, cos], axis=-1)
    sin2 = jnp.concatenate([sin, sin], axis=-1)
    z64 = jnp.zeros((seq, MLA_NOPE), F32)
    z32 = jnp.zeros((seq, LANES - MLA_NOPE - MLA_ROPE), F32)
    c1 = jnp.concatenate([jnp.ones((seq, MLA_NOPE), F32), cos2, z32], axis=-1)
    c0 = jnp.concatenate([z64, cos2, z32], axis=-1)
    s0 = jnp.concatenate([z64, sin2, z32], axis=-1)
    return c1, c0, s0


def _rot_half_cols(w):
    half = MLA_ROPE // 2
    return jnp.concatenate([-w[..., half:], w[..., :half]], axis=-1)


def mla_mixer(proj3, q_norm, w_uq, kv_norm, w_ukv, tabs, *, tm, tq):
    bsz, seq, _ = proj3.shape
    H = MLA_HEADS
    scale = (MLA_NOPE + MLA_ROPE) ** -0.5
    wq = (w_uq.astype(F32) * scale).reshape(MLA_Q_RANK, H, MLA_NOPE + MLA_ROPE)
    zq = jnp.zeros((MLA_Q_RANK, H, LANES - MLA_NOPE - MLA_ROPE), F32)
    z64 = jnp.zeros((MLA_Q_RANK, H, MLA_NOPE), F32)
    wqa = jnp.concatenate([wq, zq], axis=-1).reshape(MLA_Q_RANK, H * LANES)
    wqb = jnp.concatenate([z64, _rot_half_cols(wq[..., MLA_NOPE:]), zq], axis=-1).reshape(MLA_Q_RANK, H * LANES)
    padq = ((0, 256 - MLA_Q_RANK), (0, 0))
    wqa = jnp.pad(wqa, padq).astype(BF16)
    wqb = jnp.pad(wqb, padq).astype(BF16)
    gq = jnp.pad(q_norm.astype(F32), (0, 256 - MLA_Q_RANK)).reshape(1, 256)
    wkv = w_ukv.astype(F32).reshape(MLA_KV_RANK, H, MLA_NOPE + MLA_V)
    zk = jnp.zeros((MLA_KV_RANK, H, MLA_NOPE), F32)
    wk = jnp.concatenate([wkv[..., :MLA_NOPE], zk], axis=-1).reshape(MLA_KV_RANK, H * LANES).astype(BF16)
    wv_h = wkv[..., MLA_NOPE:]
    even = (jnp.arange(H) % 2 == 0)[None, :, None]
    wv = jnp.concatenate([jnp.where(even, wv_h, 0.0), jnp.where(even, 0.0, wv_h)], axis=-1)
    wv = wv.reshape(MLA_KV_RANK, H * LANES).astype(BF16)
    c1, c0, s0 = tabs
    full2 = lambda shape: pl.BlockSpec(shape, lambda b, i: (0,) * len(shape))
    tab_spec = pl.BlockSpec((tm, LANES), lambda b, i: (i, 0))
    hd_spec = pl.BlockSpec((1, H, tm, LANES), lambda b, i: (b, 0, i, 0))
    hd_shape = jax.ShapeDtypeStruct((bsz, H, seq, LANES), BF16)
    q, k, v = pl.pallas_call(
        _mla_prep_kernel,
        out_shape=(hd_shape, hd_shape, hd_shape),
        grid=(bsz, seq // tm),
        in_specs=[pl.BlockSpec((1, tm, 256), lambda b, i: (b, i, C_CQ // 256)),
                  pl.BlockSpec((1, tm, 256), lambda b, i: (b, i, C_KR // 256)),
                  pl.BlockSpec((1, tm, 128), lambda b, i: (b, i, C_CKV // 128)),
                  full2((1, 256)), full2((1, 128)),
                  full2((256, H * LANES)), full2((256, H * LANES)),
                  full2((128, H * LANES)), full2((128, H * LANES)),
                  tab_spec, tab_spec, tab_spec],
        out_specs=(hd_spec, hd_spec, hd_spec),
        compiler_params=_cparams(("parallel", "parallel")),
        name="mla_prep",
    )(proj3, proj3, proj3, gq, kv_norm.astype(F32).reshape(1, 128), wqa, wqb, wk, wv, c1, c0, s0)

    return pl.pallas_call(
        functools.partial(_mla_flash_kernel, tq=tq),
        out_shape=jax.ShapeDtypeStruct((bsz, seq, H * MLA_V), BF16),
        grid=(bsz, seq // tq),
        in_specs=[pl.BlockSpec((1, H, tq, LANES), lambda b, i: (b, 0, i, 0)),
                  pl.BlockSpec((1, H, seq, LANES), lambda b, i: (b, 0, 0, 0)),
                  pl.BlockSpec((1, H, seq, LANES), lambda b, i: (b, 0, 0, 0))],
        out_specs=pl.BlockSpec((1, tq, H * MLA_V), lambda b, i: (b, i, 0)),
        scratch_shapes=[pltpu.VMEM((H, tq, LANES), F32), pltpu.VMEM((H, tq, LANES), F32),
                        pltpu.VMEM((H // 2, tq, LANES), F32)],
        compiler_params=_cparams(("parallel", "arbitrary")),
        name="mla_flash",
    )(q, k, v)


def _nsa_cmp_kernel(x_ref, pea_ref, peb_ref, w1a_ref, w1b_ref, w2_ref, o_ref):
    x = x_ref[0, 0]
    w1a, w1b = w1a_ref[0], w1b_ref[0]
    bias = _dot(pea_ref[0], w1a)[0:1] + _dot(peb_ref[0], w1b)[0:1]
    a = _dot(x, w1a)
    b = _dot(x, w1b)
    n = b.shape[0]
    pre = a + pltpu.roll(b, n - 1, 0) + bias
    o_ref[0, 0] = _dot(jax.nn.gelu(pre).astype(BF16), w2_ref[0]).astype(o_ref.dtype)


def nsa_compress(kv_cr, cmp_pe, cmp_w1, cmp_w2):
    bsz, _, nch, width = kv_cr.shape
    G, dh = NSA_KV_HEADS, NSA_DIM
    half = CMP_BLOCK // 2
    eye = jnp.eye(G, dtype=F32)
    w1r = cmp_w1.astype(F32).reshape(2, CMP_BLOCK, dh, dh)
    w1a = jnp.einsum('kpde,gh->kpgdhe', w1r[:, :half], eye).reshape(2, width, G * dh).astype(BF16)
    w1b = jnp.einsum('kpde,gh->kpgdhe', w1r[:, half:], eye).reshape(2, width, G * dh).astype(BF16)
    w2 = jnp.einsum('kde,gh->kgdhe', cmp_w2.astype(F32), eye).reshape(2, G * dh, G * dh).astype(BF16)
    pe = cmp_pe.astype(F32)
    pe_g = jnp.broadcast_to(pe[:, :, None, :], (2, CMP_BLOCK, G, dh))
    pea = jnp.broadcast_to(pe_g[:, :half].reshape(2, 1, width), (2, 8, width)).astype(BF16)
    peb = jnp.broadcast_to(pe_g[:, half:].reshape(2, 1, width), (2, 8, width)).astype(BF16)
    kvspec = lambda shape: pl.BlockSpec(shape, lambda b, k: (k,) + (0,) * (len(shape) - 1))
    return pl.pallas_call(
        _nsa_cmp_kernel,
        out_shape=jax.ShapeDtypeStruct((bsz, 2, nch, G * dh), BF16),
        grid=(bsz, 2),
        in_specs=[pl.BlockSpec((1, 1, nch, width), lambda b, k: (b, k, 0, 0)),
                  kvspec((1, 8, width)), kvspec((1, 8, width)),
                  kvspec((1, width, G * dh)), kvspec((1, width, G * dh)),
                  kvspec((1, G * dh, G * dh))],
        out_specs=pl.BlockSpec((1, 1, nch, G * dh), lambda b, k: (b, k, 0, 0)),
        compiler_params=_cparams(("parallel", "parallel")),
        name="nsa_compress",
    )(kv_cr, pea, peb, w1a, w1b, w2)


def _nsa_kernel(q_ref, gate_ref, ksl_ref, vsl_ref, kwn_ref, vwn_ref, kvc_ref, bc_ref, bw_ref,
                bs_ref, ov_ref, blk1h_ref, o_ref, kaug_ref, vaug_ref, m_ref, acc_ref, *, tq, tk, nbs, n_sel):
    qi = pl.program_id(1)
    R, G = NSA_REP, NSA_KV_HEADS
    H = R * G
    nsub = tk // LANES

    @pl.when(qi == 0)
    def _():
        kaug_ref[:, :LANES] = ksl_ref[0]
        kaug_ref[:, LANES:] = blk1h_ref[...]
        vaug_ref[:, :LANES] = vsl_ref[0]
        vaug_ref[:, LANES:] = jnp.ones((vaug_ref.shape[0], LANES), BF16)

    lane = lax.broadcasted_iota(jnp.int32, (tq, LANES), 1)
    t_row = qi * tq + lax.broadcasted_iota(jnp.int32, (H * tq, 1), 0) % tq
    gs = _sigmoid(gate_ref[0].astype(F32))
    kc = kvc_ref[0, 0]
    vc = kvc_ref[0, 1]
    ov = ov_ref[...]

    def stack(fn):
        return jnp.concatenate([fn(h) for h in range(H)], axis=0)

    q_all = stack(lambda h: q_ref[0, :, h * LANES:(h + 1) * LANES])

    valid = t_row >= (CMP_BLOCK - 1)
    s = _dot_t(q_all, kc) + stack(lambda h: bc_ref[h])
    m = jnp.max(s, axis=-1, keepdims=True)
    p = jnp.where(valid, jnp.exp(s - m), 0.0)
    l = jnp.where(valid, jnp.sum(p, axis=-1, keepdims=True), 1.0)
    pc = p / l
    o_c = _dot(pc.astype(BF16), vc)
    blk = lax.broadcasted_iota(jnp.int32, (nbs, tq), 0)
    tl = qi * tq + lax.broadcasted_iota(jnp.int32, (nbs, tq), 1)
    cur = tl // SEL_BLOCK
    forced = (blk == 0) | (blk == cur) | (blk == cur - 1)
    future = blk * SEL_BLOCK > tl
    qmask = []
    for g in range(G):
        b0 = g * R * tq
        psum = pc[b0:b0 + tq] + pc[b0 + tq:b0 + 2 * tq] + pc[b0 + 2 * tq:b0 + 3 * tq]
        p_hi = psum.astype(BF16)
        p_lo = (psum - p_hi.astype(F32)).astype(BF16)
        imp = (_dot_t(ov, p_hi) + _dot_t(ov, p_lo))[:nbs]
        imp = jnp.where(forced, FORCE, jnp.where(future, -FORCE, imp))
        rank = jnp.zeros((nbs, tq), F32)
        for i in range(nbs):
            ri = imp[i:i + 1, :]
            beats = (ri > imp) | ((ri == imp) & (blk > i))
            rank = rank + jnp.where(beats, 1.0, 0.0)
        sel = jnp.where(rank < n_sel, 0.0, NEG_INF)
        sel = jnp.concatenate([sel, jnp.zeros((LANES - nbs, tq), F32)], axis=0).T.astype(BF16)
        qmask += [sel] * R
    q_aug = jnp.concatenate([q_all, jnp.concatenate(qmask, axis=0)], axis=1)

    m_ref[...] = jnp.full_like(m_ref, NEG_INF)
    acc_ref[...] = jnp.zeros_like(acc_ref)

    def sel_tile(kt, bias):
        ks = pl.ds(pl.multiple_of(kt * tk, tk), tk)
        s = _dot_t(q_aug, kaug_ref[ks, :])
        if bias is not None:
            s = s + bias
        m_prev = m_ref[...]
        m_new = jnp.maximum(m_prev, jnp.max(s, axis=-1, keepdims=True))
        alpha = jnp.exp(m_prev - m_new)
        p = jnp.exp(s - jnp.tile(m_new, (1, nsub)))
        m_ref[...] = m_new
        acc_ref[...] = acc_ref[...] * jnp.tile(alpha, (1, 2)) + _dot(p.astype(BF16), vaug_ref[ks, :])

    def near_bias(kt):
        cols = []
        for sub in range(nsub):
            d = qi - (kt * nsub + sub)
            cols.append(stack(lambda h: jnp.where(d == 0, bs_ref[h, 0], jnp.where(
                d == 1, bs_ref[h, 1], jnp.where(d < 0, NEG_INF, 0.0)))))
        return jnp.concatenate(cols, axis=1)

    def far_body(kt, c):
        sel_tile(kt, None)
        return c

    kd = (qi * tq) // tk
    lax.fori_loop(0, jnp.maximum(kd - 1, 0), far_body, 0)

    @pl.when(kd >= 1)
    def _():
        sel_tile(kd - 1, near_bias(kd - 1))

    sel_tile(kd, near_bias(kd))
    o_s = acc_ref[:, :LANES] / acc_ref[:, LANES:]

    c = jnp.minimum(qi, 2)
    ws = pl.ds(pl.multiple_of(jnp.maximum(qi - 2, 0) * tq, tq), 3 * tq)
    s = _dot_t(q_all, kwn_ref[0, ws, :]) + stack(lambda h: bw_ref[h, c])
    p = jnp.exp(s - jnp.max(s, axis=-1, keepdims=True))
    o_w = _dot(p.astype(BF16), vwn_ref[0, ws, :]) / jnp.sum(p, axis=-1, keepdims=True)

    for r in range(R):
        res = []
        for g in range(G):
            h = g * R + r
            rs = slice(h * tq, (h + 1) * tq)
            res.append(gs[:, h:h + 1] * o_c[rs] + gs[:, H + h:H + h + 1] * o_s[rs]
                       + gs[:, 2 * H + h:2 * H + h + 1] * o_w[rs])
        o_ref[0, :, r * LANES:(r + 1) * LANES] = jnp.where(lane < NSA_DIM, res[0], res[1]).astype(o_ref.dtype)


def _t5_bucket(dist):
    n = jnp.maximum(dist, 0)
    exact = REL_BUCKETS // 2
    nf = jnp.maximum(n, exact).astype(F32)
    large = exact + (jnp.log(nf / exact) / math.log(REL_MAX_DIST / exact)
                     * (REL_BUCKETS - exact)).astype(jnp.int32)
    return jnp.where(n < exact, n, jnp.minimum(large, REL_BUCKETS - 1))


def _nsa_bias_tables(rel_bias, seq, tq):
    rb = rel_bias.astype(F32).T
    far = rb[:, REL_BUCKETS - 1].reshape(NSA_HEADS, 1, 1)

    def by_dist(dist, ok, shift=0.0):
        bucket = _t5_bucket(dist)[None]
        out = jnp.zeros((NSA_HEADS,) + dist.shape, F32)
        for k in range(REL_BUCKETS):
            out = jnp.where(bucket == k, rb[:, k].reshape((NSA_HEADS,) + (1,) * dist.ndim), out)
        return jnp.where(ok[None], out - shift, NEG_INF)

    i = jnp.arange(tq)[:, None]
    t = jnp.arange(seq)[:, None]
    dist_c = t - (jnp.arange(LANES)[None, :] * CMP_STRIDE + CMP_BLOCK - 1)
    bc = by_dist(dist_c, dist_c >= 0)
    jw = jnp.arange(3 * tq)[None, :]
    bw = jnp.stack([by_dist(tq * c + i - jw, (tq * c + i - jw >= 0) & (tq * c + i - jw < WINDOW))
                    for c in range(3)], axis=1)
    js = jnp.arange(tq)[None, :]
    bs = jnp.stack([by_dist(tq * c + i - js, tq * c + i - js >= 0, far) for c in range(2)], axis=1)
    ci = jnp.arange(LANES)[:, None]
    sj = jnp.arange(LANES)[None, :]
    nbs = seq // SEL_BLOCK
    ov = ((ci * CMP_STRIDE <= sj * SEL_BLOCK + SEL_BLOCK - 1)
          & (ci * CMP_STRIDE + CMP_BLOCK - 1 >= sj * SEL_BLOCK)
          & (ci < seq // CMP_STRIDE - 1) & (sj < nbs))
    blk1h = (jnp.arange(seq)[:, None] // SEL_BLOCK == sj).astype(BF16)
    return bc, bw, bs, ov.T.astype(BF16), blk1h


def nsa_mixer(proj3, kvc, tables, *, tq, tk):
    bsz, seq, _ = proj3.shape
    assert tq == LANES and tq >= REL_MAX_DIST and tk % tq == 0 and seq % tk == 0
    assert seq // CMP_STRIDE == LANES and tq * 2 == WINDOW
    bc, bw, bs, ov, blk1h = tables
    nbs = seq // SEL_BLOCK
    H = NSA_HEADS
    slab = lambda j: pl.BlockSpec((1, seq, LANES), lambda b, i: (b, 0, C_NKV // LANES + j))
    const = lambda shape: pl.BlockSpec(shape, lambda b, i: (0,) * len(shape))
    return pl.pallas_call(
        functools.partial(_nsa_kernel, tq=tq, tk=tk, nbs=nbs, n_sel=min(SEL_TOPN, nbs)),
        out_shape=jax.ShapeDtypeStruct((bsz, seq, H * NSA_DIM), BF16),
        grid=(bsz, seq // tq),
        in_specs=[pl.BlockSpec((1, tq, H * LANES), lambda b, i: (b, i, C_NQ // (H * LANES))),
                  pl.BlockSpec((1, tq, LANES), lambda b, i: (b, i, C_GATE // LANES)),
                  slab(2), slab(3), slab(4), slab(5),
                  pl.BlockSpec((1, 2, LANES, LANES), lambda b, i: (b, 0, 0, 0)),
                  pl.BlockSpec((H, tq, LANES), lambda b, i: (0, i, 0)),
                  const((H, 3, tq, 3 * tq)), const((H, 2, tq, tq)),
                  const((LANES, LANES)), const((seq, LANES))],
        out_specs=pl.BlockSpec((1, tq, H * NSA_DIM), lambda b, i: (b, i, 0)),
        scratch_shapes=[pltpu.VMEM((seq, 2 * LANES), BF16), pltpu.VMEM((seq, 2 * LANES), BF16),
                        pltpu.VMEM((H * tq, LANES), F32), pltpu.VMEM((H * tq, 2 * LANES), F32)],
        compiler_params=_cparams(("parallel", "arbitrary")),
        name="nsa_attention",
    )(proj3, proj3, proj3, proj3, proj3, proj3, kvc, bc, bw, bs, ov, blk1h)


def _outproj_kernel(ys_ref, ym_ref, yn_ref, h_ref, g1_ref, g2_ref, g3_ref, w1_ref, w2_ref, w3_ref, o_ref):
    def part(y_ref, g_ref, w_ref):
        y = y_ref[...].astype(F32)
        return _dot(_rms(y, g_ref[...], y.shape[-1]).astype(BF16), w_ref[...])

    o_ref[...] = h_ref[...] + part(ys_ref, g1_ref, w1_ref) + part(ym_ref, g2_ref, w2_ref) + part(yn_ref, g3_ref, w3_ref)


def out_projection(y_ssm, y_mla, y_nsa, h, gains, weights, *, tm):
    m, d = h.shape
    row = lambda w: pl.BlockSpec((tm, w), lambda i: (i, 0))
    full = lambda a: pl.BlockSpec(a.shape, lambda i: (0, 0))
    gains = [g.reshape(1, -1).astype(F32) for g in gains]
    return pl.pallas_call(
        _outproj_kernel,
        out_shape=jax.ShapeDtypeStruct((m, d), F32),
        grid=(m // tm,),
        in_specs=[row(y_ssm.shape[1]), row(y_mla.shape[1]), row(y_nsa.shape[1]), row(d)]
                 + [full(g) for g in gains] + [full(w) for w in weights],
        out_specs=row(d),
        compiler_params=_cparams(("parallel",)),
        name="out_projection",
    )(y_ssm, y_mla, y_nsa, h, *gains, *weights)


def _xattn_kernel(h_ref, kv_ref, g_ref, wq_ref, wo_ref, o_ref, *, dh):
    h = h_ref[0]
    xn = _rms(h, g_ref[...], h.shape[-1]).astype(BF16)
    q = _dot(xn, wq_ref[...]).astype(BF16)
    hw = XATTN_HEADS * dh
    outs = []
    for hd in range(XATTN_HEADS):
        k = kv_ref[0, :, hd * dh:(hd + 1) * dh]
        v = kv_ref[0, :, hw + hd * dh:hw + (hd + 1) * dh]
        s = _dot_t(q[:, hd * dh:(hd + 1) * dh], k)
        p = jnp.exp(s - jnp.max(s, axis=-1, keepdims=True))
        p = p / jnp.sum(p, axis=-1, keepdims=True)
        outs.append(_dot(p.astype(BF16), v).astype(BF16))
    o = jnp.concatenate(outs, axis=-1)
    o_ref[0] = h + _dot(o, wo_ref[...])


def cross_attention(h3, kv3, g_x, wq, wo, *, tm):
    bsz, seq, d = h3.shape
    m = kv3.shape[1]
    dh = d // XATTN_HEADS
    wq_s = (wq.astype(F32) * dh ** -0.5).astype(BF16)
    const = lambda shape: pl.BlockSpec(shape, lambda b, i: (0,) * len(shape))
    return pl.pallas_call(
        functools.partial(_xattn_kernel, dh=dh),
        out_shape=jax.ShapeDtypeStruct((bsz, seq, d), F32),
        grid=(bsz, seq // tm),
        in_specs=[pl.BlockSpec((1, tm, d), lambda b, i: (b, i, 0)),
                  pl.BlockSpec((1, m, 2 * d), lambda b, i: (b, 0, 0)),
                  const((1, d)), const((d, d)), const((d, d))],
        out_specs=pl.BlockSpec((1, tm, d), lambda b, i: (b, i, 0)),
        compiler_params=_cparams(("parallel", "parallel")),
        name="cross_attention",
    )(h3, kv3, g_x.reshape(1, d).astype(F32), wq_s, wo.astype(BF16))


def _ffn_kernel(h_ref, g_ref, wg_ref, wu_ref, wd_ref, o_ref, xn_ref, acc_ref):
    j = pl.program_id(1)

    @pl.when(j == 0)
    def _():
        h = h_ref[...]
        xn_ref[...] = _rms(h, g_ref[...], h.shape[-1]).astype(BF16)
        acc_ref[...] = h

    xn = xn_ref[...]
    a = _silu(_dot(xn, wg_ref[...])) * _dot(xn, wu_ref[...])
    acc_ref[...] += _dot(a.astype(BF16), wd_ref[...])

    @pl.when(j == pl.num_programs(1) - 1)
    def _():
        o_ref[...] = acc_ref[...]


def dense_ffn(h, g, wg, wu, wd, *, tm, tf):
    m, d = h.shape
    ff = wg.shape[1]
    return pl.pallas_call(
        _ffn_kernel,
        out_shape=jax.ShapeDtypeStruct((m, d), F32),
        grid=(m // tm, ff // tf),
        in_specs=[pl.BlockSpec((tm, d), lambda i, j: (i, 0)),
                  pl.BlockSpec((1, d), lambda i, j: (0, 0)),
                  pl.BlockSpec((d, tf), lambda i, j: (0, j)),
                  pl.BlockSpec((d, tf), lambda i, j: (0, j)),
                  pl.BlockSpec((tf, d), lambda i, j: (j, 0))],
        out_specs=pl.BlockSpec((tm, d), lambda i, j: (i, 0)),
        scratch_shapes=[pltpu.VMEM((tm, d), BF16), pltpu.VMEM((tm, d), F32)],
        compiler_params=_cparams(("parallel", "arbitrary")),
        name="dense_ffn",
    )(h, g.reshape(1, d).astype(F32), wg.astype(BF16), wu.astype(BF16), wd.astype(BF16))


def _router_kernel(h_ref, g_ref, wr_hi_ref, wr_lo_ref, xn_ref, info_ref, cnt_ref, carry_ref, *, tm):
    i = pl.program_id(0)

    @pl.when(i == 0)
    def _():
        carry_ref[...] = jnp.zeros_like(carry_ref)

    h = h_ref[...]
    xn = _rms(h, g_ref[...], h.shape[-1])
    xn_ref[...] = xn
    x_hi = xn.astype(BF16)
    x_lo = (xn - x_hi.astype(F32)).astype(BF16)
    logits = _dot(x_hi, wr_hi_ref[...]) + _dot(x_lo, wr_hi_ref[...]) + _dot(x_hi, wr_lo_ref[...])
    lane = lax.broadcasted_iota(jnp.int32, (tm, LANES), 1)
    lanef = lane.astype(F32)
    logits = jnp.where(lane < N_EXPERTS, logits, NEG_INF)
    m1 = jnp.max(logits, axis=-1, keepdims=True)
    i1 = jnp.min(jnp.where(logits == m1, lanef, float(LANES)), axis=-1, keepdims=True)
    rest = jnp.where(lanef == i1, NEG_INF, logits)
    m2 = jnp.max(rest, axis=-1, keepdims=True)
    i2 = jnp.min(jnp.where(rest == m2, lanef, float(LANES)), axis=-1, keepdims=True)
    e2 = jnp.exp(m2 - m1)
    w1 = 1.0 / (1.0 + e2)
    w2 = e2 / (1.0 + e2)
    oh1 = lanef == i1
    oh2 = lanef == i2
    oh = jnp.where(oh1 | oh2, 1.0, 0.0)
    rr = lax.broadcasted_iota(jnp.int32, (tm, tm), 0)
    cc = lax.broadcasted_iota(jnp.int32, (tm, tm), 1)
    tri = jnp.where(cc < rr, 1.0, 0.0).astype(BF16)
    before = _dot(tri, oh.astype(BF16)) + carry_ref[0:1, :]
    r1 = jnp.sum(jnp.where(oh1, before, 0.0), axis=-1, keepdims=True)
    r2 = jnp.sum(jnp.where(oh2, before, 0.0), axis=-1, keepdims=True)
    carry_ref[...] = carry_ref[...] + jnp.sum(oh, axis=0, keepdims=True)
    info = jnp.where(lane == 0, i1, jnp.where(lane == 1, i2, jnp.where(lane == 2, w1, jnp.where(
        lane == 3, w2, jnp.where(lane == 4, r1, jnp.where(lane == 5, r2, 0.0))))))
    info_ref[...] = info
    cnt_ref[...] = carry_ref[...]


def moe_router(h, g, router, *, tm):
    m, d = h.shape
    wr = jnp.pad(router.astype(F32), ((0, 0), (0, LANES - N_EXPERTS)))
    wr_hi = wr.astype(BF16)
    wr_lo = (wr - wr_hi.astype(F32)).astype(BF16)
    return pl.pallas_call(
        functools.partial(_router_kernel, tm=tm),
        out_shape=(jax.ShapeDtypeStruct((m, d), F32), jax.ShapeDtypeStruct((m, LANES), F32),
                   jax.ShapeDtypeStruct((8, LANES), F32)),
        grid=(m // tm,),
        in_specs=[pl.BlockSpec((tm, d), lambda i: (i, 0)),
                  pl.BlockSpec((1, d), lambda i: (0, 0)),
                  pl.BlockSpec((d, LANES), lambda i: (0, 0)),
                  pl.BlockSpec((d, LANES), lambda i: (0, 0))],
        out_specs=(pl.BlockSpec((tm, d), lambda i: (i, 0)),
                   pl.BlockSpec((tm, LANES), lambda i: (i, 0)),
                   pl.BlockSpec((8, LANES), lambda i: (0, 0))),
        scratch_shapes=[pltpu.VMEM((8, LANES), F32)],
        compiler_params=_cparams(("arbitrary",)),
        name="moe_router",
    )(h, g.reshape(1, d).astype(F32), wr_hi, wr_lo)


def _row_gather_copy(src_hbm, row, dst, slot, sem):
    return pltpu.make_async_copy(src_hbm.at[pl.ds(row, 1), :], dst.at[pl.ds(slot, 1), :], sem)


def _moe_ffn_kernel(src_ref, texp_ref, nact_ref, x_hbm, wg_ref, wu_ref, wd_ref, o_ref,
                    xbuf, xbf, acc_ref, sem, *, tm):
    i = pl.program_id(0)
    j = pl.program_id(1)
    nj = pl.num_programs(1)
    nact = nact_ref[0]
    active = i < nact
    cur = i % 2

    def start_gather(tile, buf):
        def issue(s, c):
            _row_gather_copy(x_hbm, src_ref[tile * tm + s], xbuf.at[buf], s, sem.at[buf]).start()
            return c

        lax.fori_loop(0, tm, issue, 0, unroll=8)

    @pl.when((i == 0) & (j == 0))
    def _():
        start_gather(0, 0)

    @pl.when(active & (j == 0))
    def _():
        pltpu.make_async_copy(x_hbm.at[pl.ds(0, tm), :], xbuf.at[cur], sem.at[cur]).wait()
        xbf[...] = xbuf[cur].astype(BF16)
        acc_ref[...] = jnp.zeros_like(acc_ref)

    @pl.when((j == 1) & (i + 1 < nact))
    def _():
        start_gather(i + 1, 1 - cur)

    @pl.when(active)
    def _():
        x = xbf[...]
        a = _silu(_dot(x, wg_ref[0, 0])) * _dot(x, wu_ref[0, 0])
        acc_ref[...] += _dot(a.astype(BF16), wd_ref[0, 0])

    @pl.when(j == nj - 1)
    def _():
        o_ref[...] = jnp.where(active, acc_ref[...], 0.0)


def moe_expert_ffn(xn, src, tile_expert, n_active, wg, wu, wd, *, tm, tf):
    n_slots = src.shape[0]
    d = xn.shape[1]
    ne, _, ff = wg.shape
    nj = ff // tf
    assert nj >= 2 and nj * tf == ff
    wg = wg.reshape(ne, d, nj, tf).transpose(0, 2, 1, 3).astype(BF16)
    wu = wu.reshape(ne, d, nj, tf).transpose(0, 2, 1, 3).astype(BF16)
    wd = wd.reshape(ne, nj, tf, d).astype(BF16)

    def wmap(i, j, src, texp, nact):
        return (texp[i], jnp.where(i < nact[0], j, nj - 1), 0, 0)

    return pl.pallas_call(
        functools.partial(_moe_ffn_kernel, tm=tm),
        out_shape=jax.ShapeDtypeStruct((n_slots, d), F32),
        grid_spec=pltpu.PrefetchScalarGridSpec(
            num_scalar_prefetch=3,
            grid=(n_slots // tm, nj),
            in_specs=[pl.BlockSpec(memory_space=pl.ANY),
                      pl.BlockSpec((1, 1, d, tf), wmap),
                      pl.BlockSpec((1, 1, d, tf), wmap),
                      pl.BlockSpec((1, 1, tf, d), wmap)],
            out_specs=pl.BlockSpec((tm, d), lambda i, j, *_: (i, 0)),
            scratch_shapes=[pltpu.VMEM((2, tm, d), F32), pltpu.VMEM((tm, d), BF16),
                            pltpu.VMEM((tm, d), F32), pltpu.SemaphoreType.DMA((2,))]),
        compiler_params=_cparams(("arbitrary", "arbitrary")),
        name="moe_expert_ffn",
    )(src, tile_expert, n_active, xn, wg, wu, wd)


def _moe_combine_kernel(pos_ref, h_ref, info_ref, ys_hbm, g_ref, o_ref, buf, sem, *, tm, final_norm):
    i = pl.program_id(0)
    n = pl.num_programs(0)
    cur = i % 2

    def start_gather(tile, b):
        def issue(s, c):
            for k in range(2):
                _row_gather_copy(ys_hbm, pos_ref[2 * (tile * tm + s) + k], buf.at[b, k], s, sem.at[b]).start()
            return c

        lax.fori_loop(0, tm, issue, 0, unroll=8)

    @pl.when(i == 0)
    def _():
        start_gather(0, 0)

    @pl.when(i + 1 < n)
    def _():
        start_gather(i + 1, 1 - cur)

    for k in range(2):
        pltpu.make_async_copy(ys_hbm.at[pl.ds(0, tm), :], buf.at[cur, k], sem.at[cur]).wait()
    info = info_ref[...]
    y = h_ref[...] + info[:, 2:3] * buf[cur, 0] + info[:, 3:4] * buf[cur, 1]
    if final_norm:
        y = _rms(y, g_ref[...], y.shape[-1])
    o_ref[...] = y


def moe_combine(h, info, ys, pos_flat, g_final, *, tm, final_norm):
    m, d = h.shape
    return pl.pallas_call(
        functools.partial(_moe_combine_kernel, tm=tm, final_norm=final_norm),
        out_shape=jax.ShapeDtypeStruct((m, d), F32),
        grid_spec=pltpu.PrefetchScalarGridSpec(
            num_scalar_prefetch=1,
            grid=(m // tm,),
            in_specs=[pl.BlockSpec((tm, d), lambda i, *_: (i, 0)),
                      pl.BlockSpec((tm, LANES), lambda i, *_: (i, 0)),
                      pl.BlockSpec(memory_space=pl.ANY),
                      pl.BlockSpec((1, d), lambda i, *_: (0, 0))],
            out_specs=pl.BlockSpec((tm, d), lambda i, *_: (i, 0)),
            scratch_shapes=[pltpu.VMEM((2, 2, tm, d), F32), pltpu.SemaphoreType.DMA((2,))]),
        compiler_params=_cparams(("arbitrary",)),
        name="moe_combine",
    )(pos_flat, h, info, ys, g_final.reshape(1, d).astype(F32))


def moe_layer(h, g, router, wg, wu, wd, g_final, *, final_norm, tm_r=512, tm_g=512, tf=896, tm_c=256):
    m, d = h.shape
    xn, info, cnt = moe_router(h, g, router, tm=tm_r)
    e_idx = info[:, 0:2].astype(jnp.int32)
    rank = info[:, 4:6].astype(jnp.int32)
    counts = cnt[0, :N_EXPERTS].astype(jnp.int32)
    tiles_per = (counts + tm_g - 1) // tm_g
    tile_end = jnp.cumsum(tiles_per)
    seg_start = (tile_end - tiles_per) * tm_g
    pos = rank
    for e in range(N_EXPERTS):
        pos = pos + jnp.where(e_idx == e, seg_start[e], 0)
    n_tiles = (2 * m) // tm_g + N_EXPERTS
    n_slots = n_tiles * tm_g
    tok = jnp.broadcast_to(jnp.arange(m, dtype=jnp.int32)[:, None], (m, 2))
    src = jnp.zeros((n_slots,), jnp.int32).at[pos.reshape(-1)].set(tok.reshape(-1))
    n_active = tile_end[-1:].astype(jnp.int32)
    tile_ids = jnp.minimum(jnp.arange(n_tiles, dtype=jnp.int32), n_active[0] - 1)
    tile_expert = jnp.sum(tile_ids[:, None] >= tile_end[None, :], axis=1).astype(jnp.int32)
    ys = moe_expert_ffn(xn, src, tile_expert, n_active,
                        wg, wu, wd, tm=tm_g, tf=tf)
    return moe_combine(h, info, ys, pos.reshape(-1).astype(jnp.int32), g_final, tm=tm_c, final_norm=final_norm)


def _final_norm_kernel(h_ref, g_ref, o_ref):
    h = h_ref[...]
    o_ref[...] = _rms(h, g_ref[...], h.shape[-1])


def final_rmsnorm(h, g, *, tm):
    m, d = h.shape
    return pl.pallas_call(
        _final_norm_kernel,
        out_shape=jax.ShapeDtypeStruct((m, d), F32),
        grid=(m // tm,),
        in_specs=[pl.BlockSpec((tm, d), lambda i: (i, 0)), pl.BlockSpec((1, d), lambda i: (0, 0))],
        out_specs=pl.BlockSpec((tm, d), lambda i: (i, 0)),
        compiler_params=_cparams(("parallel",)),
        name="final_rmsnorm",
    )(h, g.reshape(1, d).astype(F32))


def _pack_w_in(w):
    d = w.shape[0]
    w = w.astype(F32)
    o = 0
    u = w[:, o:o + SSM_WIDTH]; o += SSM_WIDTH
    cq = w[:, o:o + MLA_Q_RANK]; o += MLA_Q_RANK
    ckv = w[:, o:o + MLA_KV_RANK]; o += MLA_KV_RANK
    kr = w[:, o:o + MLA_ROPE]; o += MLA_ROPE
    nq = w[:, o:o + NSA_HEADS * NSA_DIM]; o += NSA_HEADS * NSA_DIM
    nkv = w[:, o:o + 6 * NSA_KV_HEADS * NSA_DIM]; o += 6 * NSA_KV_HEADS * NSA_DIM
    gate = w[:, o:o + 3 * NSA_HEADS]
    z = lambda n: jnp.zeros((d, n), F32)
    kr_a = jnp.concatenate([z(MLA_NOPE), kr, z(LANES - MLA_NOPE - MLA_ROPE)], axis=1)
    kr_b = jnp.concatenate([z(MLA_NOPE), _rot_half_cols(kr), z(LANES - MLA_NOPE - MLA_ROPE)], axis=1)
    nq_h = (nq * NSA_DIM ** -0.5).reshape(d, NSA_KV_HEADS, NSA_REP, NSA_DIM)
    zq = jnp.zeros((d, NSA_REP, NSA_DIM), F32)
    nq_p = jnp.concatenate([
        jnp.concatenate([nq_h[:, 0], zq], axis=-1).reshape(d, NSA_REP * LANES),
        jnp.concatenate([zq, nq_h[:, 1]], axis=-1).reshape(d, NSA_REP * LANES)], axis=1)
    packed = jnp.concatenate([u, cq, z(256 - MLA_Q_RANK), kr_a, kr_b, nq_p, nkv, ckv,
                              gate, z(LANES - 3 * NSA_HEADS)], axis=1)
    assert packed.shape[1] == IN_COLS_PACKED
    return packed.astype(BF16)


def _rg_order(a):
    rest = a.shape[1:]
    return a.reshape((NSA_KV_HEADS, NSA_REP, NSA_DIM) + rest).swapaxes(0, 1).reshape((-1,) + rest)


def kernel(x, mem, w_in, w_out, mix_norm, out_norm, ssm_a_re, ssm_a_im, ssm_b_re, ssm_b_im, ssm_c_re, ssm_c_im, ssm_d, ssm_log_dt, ssm_w_glu, mla_q_norm, mla_w_uq, mla_kv_norm, mla_w_ukv, nsa_cmp_pe, nsa_cmp_w1, nsa_cmp_w2, rel_bias, xattn_norm, mem_norm, xattn_wq, xattn_wkv, xattn_wo, ffn_norm, dense_w_gate, dense_w_up, dense_w_down, moe_router, moe_w_gate, moe_w_up, moe_w_down, final_norm):
    bsz, seq, d = x.shape
    depth = w_in.shape[0]
    T = bsz * seq
    nmem = mem.shape[1]
    tq_nsa, tk_nsa = LANES, 4 * LANES
    rope_tabs = _rope_tables(seq)
    nsa_tabs = _nsa_bias_tables(rel_bias, seq, tq_nsa)
    o1, o2 = SSM_WIDTH, SSM_WIDTH + MLA_HEADS * MLA_V
    mem2 = mem.reshape(bsz * nmem, d)
    h = x.reshape(T, d)
    for l in range(depth):
        proj = norm_matmul(h, mix_norm[l], _pack_w_in(w_in[l]), tm=512, tn=IN_COLS_PACKED, out_dtype=BF16)
        proj3 = proj.reshape(bsz, seq, IN_COLS_PACKED)
        u_tm = proj3[:, :, C_U:C_U + SSM_WIDTH].transpose(1, 0, 2).reshape(seq * bsz, SSM_WIDTH)
        y_ssm = ssm_mixer(u_tm, ssm_a_re[l], ssm_a_im[l], ssm_b_re[l], ssm_b_im[l], ssm_c_re[l], ssm_c_im[l],
                          ssm_d[l], ssm_log_dt[l], ssm_w_glu[l], nb=bsz, tc=64)
        y_ssm = y_ssm.reshape(seq, bsz, SSM_WIDTH).transpose(1, 0, 2).reshape(T, SSM_WIDTH)
        y_mla = mla_mixer(proj3, mla_q_norm[l], mla_w_uq[l], mla_kv_norm[l], mla_w_ukv[l], rope_tabs,
                          tm=512, tq=512).reshape(T, -1)
        nch = seq // CMP_STRIDE
        kv_cr = jnp.stack([proj3[:, :, C_NKV:C_NKV + LANES].reshape(bsz, nch, CMP_STRIDE * LANES),
                           proj3[:, :, C_NKV + LANES:C_NKV + 2 * LANES].reshape(bsz, nch, CMP_STRIDE * LANES)],
                          axis=1)
        kvc = nsa_compress(kv_cr, nsa_cmp_pe[l], nsa_cmp_w1[l], nsa_cmp_w2[l])
        y_nsa = nsa_mixer(proj3, kvc, nsa_tabs, tq=tq_nsa, tk=tk_nsa).reshape(T, -1)
        g_out = out_norm[l]
        wo_l = w_out[l]
        h = out_projection(y_ssm, y_mla, y_nsa, h,
                           [g_out[:o1], g_out[o1:o2], _rg_order(g_out[o2:])],
                           [wo_l[:o1].astype(BF16), wo_l[o1:o2].astype(BF16), _rg_order(wo_l[o2:]).astype(BF16)],
                           tm=512)
        kv_mem = norm_matmul(mem2, mem_norm[l], xattn_wkv[l].astype(BF16), tm=256, tn=512, out_dtype=BF16)
        h = cross_attention(h.reshape(bsz, seq, d), kv_mem.reshape(bsz, nmem, 2 * d), xattn_norm[l],
                            xattn_wq[l], xattn_wo[l], tm=256).reshape(T, d)
        last = l == depth - 1
        if l % 2 == 0:
            h = dense_ffn(h, ffn_norm[l], dense_w_gate[l // 2], dense_w_up[l // 2], dense_w_down[l // 2],
                          tm=512, tf=1408)
            if last:
                h = final_rmsnorm(h, final_norm, tm=512)
        else:
            h = moe_layer(h, ffn_norm[l], moe_router[l // 2], moe_w_gate[l // 2], moe_w_up[l // 2],
                          moe_w_down[l // 2], final_norm, final_norm=last)
    return h.reshape(bsz, seq, d)
```

```python
import functools
import math

import jax
import jax.numpy as jnp
from jax import lax
from jax.experimental import pallas as pl
from jax.experimental.pallas import tpu as pltpu

F32 = jnp.float32
BF16 = jnp.bfloat16

HEAD_DIM = 64
SSM_WIDTH = 256
SSM_CH = 16
SSM_GROUPS = 16
SSM_STATE = 64
MLA_HEADS = 6
MLA_NOPE = 64
MLA_ROPE = 32
MLA_V = 64
MLA_Q_RANK = 192
MLA_KV_RANK = 128
NSA_HEADS = 6
NSA_KV_HEADS = 2
NSA_REP = 3
NSA_DIM = 64
CMP_BLOCK = 32
CMP_STRIDE = 16
SEL_BLOCK = 64
SEL_TOPN = 8
WINDOW = 256
REL_BUCKETS = 32
REL_MAX_DIST = 128
XATTN_HEADS = 4
N_EXPERTS = 8
ROPE_THETA = 10000.0
EPS = 1e-6
NEG_INF = -1e30
FORCE = 1e9

LANES = 128
VMEM_LIMIT = 56 * 1024 * 1024

C_U, C_CQ, C_KR, C_NQ, C_NKV, C_CKV, C_GATE = 0, 256, 512, 768, 1536, 2304, 2432
IN_COLS_PACKED = 2560


def _cparams(sem):
    return pltpu.CompilerParams(dimension_semantics=sem, vmem_limit_bytes=VMEM_LIMIT)


def _dot(a, b):
    return jnp.dot(a, b, preferred_element_type=F32)


def _dot_t(a, b):
    return lax.dot_general(a, b, (((1,), (1,)), ((), ())), preferred_element_type=F32)


def _rms(x, g, n):
    ms = jnp.sum(x * x, axis=-1, keepdims=True) * (1.0 / n)
    return x * lax.rsqrt(ms + EPS) * g


def _sigmoid(x):
    return 1.0 / (1.0 + jnp.exp(-x))


def _silu(x):
    return x * _sigmoid(x)


def _norm_mm_kernel(x_ref, g_ref, w_ref, o_ref, xn_ref):
    @pl.when(pl.program_id(1) == 0)
    def _():
        x = x_ref[...].astype(F32)
        xn_ref[...] = _rms(x, g_ref[...], x.shape[-1]).astype(BF16)

    o_ref[...] = _dot(xn_ref[...], w_ref[...]).astype(o_ref.dtype)


def norm_matmul(x, g, w, *, tm, tn, out_dtype):
    m, k = x.shape
    n = w.shape[1]
    return pl.pallas_call(
        _norm_mm_kernel,
        out_shape=jax.ShapeDtypeStruct((m, n), out_dtype),
        grid=(m // tm, n // tn),
        in_specs=[pl.BlockSpec((tm, k), lambda i, j: (i, 0)),
                  pl.BlockSpec((1, k), lambda i, j: (0, 0)),
                  pl.BlockSpec((k, tn), lambda i, j: (0, j))],
        out_specs=pl.BlockSpec((tm, tn), lambda i, j: (i, j)),
        scratch_shapes=[pltpu.VMEM((tm, k), BF16)],
        compiler_params=_cparams(("parallel", "arbitrary")),
        name="norm_matmul",
    )(x, g.reshape(1, k).astype(F32), w)


def _ssm_kernel(u_ref, bbr_ref, bbi_ref, ar_ref, ai_ref, ccr_ref, cci_ref, d_ref, wglu_ref,
                o_ref, hr_ref, hi_ref, cr_ref, ci_ref, *, tc, nb):
    @pl.when(pl.program_id(0) == 0)
    def _():
        cr_ref[...] = jnp.zeros_like(cr_ref)
        ci_ref[...] = jnp.zeros_like(ci_ref)

    u = u_ref[...]
    hr_ref[...] = _dot(u, bbr_ref[...])
    hi_ref[...] = _dot(u, bbi_ref[...])
    gp = ar_ref.shape[-1]
    ar = jnp.broadcast_to(ar_ref[...], (nb, gp))
    ai = jnp.broadcast_to(ai_ref[...], (nb, gp))

    def step(t, carry):
        hr, hi = carry
        rows = pl.ds(pl.multiple_of(t * nb, nb), nb)
        nr = ar * hr - ai * hi + hr_ref[rows, :]
        ni = ar * hi + ai * hr + hi_ref[rows, :]
        hr_ref[rows, :] = nr
        hi_ref[rows, :] = ni
        return nr, ni

    hr, hi = lax.fori_loop(0, tc, step, (cr_ref[...], ci_ref[...]))
    cr_ref[...] = hr
    ci_ref[...] = hi
    y = (_dot(hr_ref[...].astype(BF16), ccr_ref[...]) + _dot(hi_ref[...].astype(BF16), cci_ref[...])
         + d_ref[...] * u.astype(F32))
    y = jax.nn.gelu(y)
    z = _dot(y.astype(BF16), wglu_ref[...])
    o_ref[...] = (y * _sigmoid(z)).astype(o_ref.dtype)


def ssm_mixer(u_tm, a_re, a_im, b_re, b_im, c_re, c_im, d, log_dt, w_glu, *, nb, tc):
    rows = u_tm.shape[0]
    G, P, C = SSM_GROUPS, SSM_STATE, SSM_CH
    dt = jnp.exp(log_dt.astype(F32))[:, None]
    lr, li = a_re.astype(F32), a_im.astype(F32)
    mag = jnp.exp(lr * dt)
    ab_r, ab_i = mag * jnp.cos(li * dt), mag * jnp.sin(li * dt)
    den = lr * lr + li * li
    nr = ab_r - 1.0
    f_r = (nr * lr + ab_i * li) / den
    f_i = (ab_i * lr - nr * li) / den
    br, bi = b_re.astype(F32), b_im.astype(F32)
    bb_r = f_r[..., None] * br - f_i[..., None] * bi
    bb_i = f_r[..., None] * bi + f_i[..., None] * br
    eye = jnp.eye(G, dtype=F32)
    bbr = jnp.einsum('gpc,gh->gchp', bb_r, eye).reshape(G * C, G * P).astype(BF16)
    bbi = jnp.einsum('gpc,gh->gchp', bb_i, eye).reshape(G * C, G * P).astype(BF16)
    ccr = jnp.einsum('gcp,gh->gphc', c_re.astype(F32), eye).reshape(G * P, G * C).astype(BF16)
    cci = jnp.einsum('gcp,gh->gphc', -c_im.astype(F32), eye).reshape(G * P, G * C).astype(BF16)
    gp = G * P
    full = lambda shape: pl.BlockSpec(shape, lambda i: (0,) * len(shape))
    return pl.pallas_call(
        functools.partial(_ssm_kernel, tc=tc, nb=nb),
        out_shape=jax.ShapeDtypeStruct((rows, SSM_WIDTH), BF16),
        grid=(rows // (tc * nb),),
        in_specs=[pl.BlockSpec((tc * nb, SSM_WIDTH), lambda i: (i, 0)),
                  full((G * C, gp)), full((G * C, gp)), full((1, gp)), full((1, gp)),
                  full((gp, G * C)), full((gp, G * C)), full((1, SSM_WIDTH)),
                  full((SSM_WIDTH, SSM_WIDTH))],
        out_specs=pl.BlockSpec((tc * nb, SSM_WIDTH), lambda i: (i, 0)),
        scratch_shapes=[pltpu.VMEM((tc * nb, gp), F32), pltpu.VMEM((tc * nb, gp), F32),
                        pltpu.VMEM((nb, gp), F32), pltpu.VMEM((nb, gp), F32)],
        compiler_params=_cparams(("arbitrary",)),
        name="ssm_mixer",
    )(u_tm, bbr, bbi, ab_r.reshape(1, gp), ab_i.reshape(1, gp), ccr, cci,
      d.reshape(1, SSM_WIDTH).astype(F32), w_glu.astype(BF16))


def _mla_prep_kernel(cq_ref, kr_ref, ckv_ref, gq_ref, gkv_ref, wqa_ref, wqb_ref, wk_ref, wv_ref,
                     c1_ref, c0_ref, s0_ref, q_ref, k_ref, v_ref):
    qn = _rms(cq_ref[0].astype(F32), gq_ref[...], MLA_Q_RANK).astype(BF16)
    qa = _dot(qn, wqa_ref[...])
    qb = _dot(qn, wqb_ref[...])
    kn = _rms(ckv_ref[0].astype(F32), gkv_ref[...], MLA_KV_RANK).astype(BF16)
    ka = _dot(kn, wk_ref[...])
    va = _dot(kn, wv_ref[...])
    kr = kr_ref[0].astype(F32)
    c1, c0, s0 = c1_ref[...], c0_ref[...], s0_ref[...]
    krope = kr[:, :LANES] * c0 + kr[:, LANES:] * s0
    for h in range(MLA_HEADS):
        sl = slice(h * LANES, (h + 1) * LANES)
        q_ref[0, h] = (qa[:, sl] * c1 + qb[:, sl] * s0).astype(BF16)
        k_ref[0, h] = (ka[:, sl] + krope).astype(BF16)
        v_ref[0, h] = va[:, sl].astype(BF16)


def _mla_flash_kernel(q_ref, k_ref, v_ref, o_ref, m_ref, l_ref, acc_ref, *, tq):
    qi = pl.program_id(1)
    m_ref[...] = jnp.full_like(m_ref, NEG_INF)
    l_ref[...] = jnp.zeros_like(l_ref)
    acc_ref[...] = jnp.zeros_like(acc_ref)
    lane = lax.broadcasted_iota(jnp.int32, (tq, LANES), 1)
    rep = tq // LANES

    def tile(kt, masked):
        ks = pl.ds(pl.multiple_of(kt * tq, tq), tq)
        if masked:
            mask = (lax.broadcasted_iota(jnp.int32, (tq, tq), 1)
                    <= lax.broadcasted_iota(jnp.int32, (tq, tq), 0))
        for pr in range(MLA_HEADS // 2):
            hs = (2 * pr, 2 * pr + 1)
            s = [_dot_t(q_ref[0, h], k_ref[0, h, ks, :]) for h in hs]
            if masked:
                s = [jnp.where(mask, x, NEG_INF) for x in s]
            m_prev = [m_ref[h] for h in hs]
            m_new = [jnp.maximum(mp, jnp.max(x, axis=-1, keepdims=True)) for mp, x in zip(m_prev, s)]
            alpha = [jnp.exp(mp - mn) for mp, mn in zip(m_prev, m_new)]
            p = [jnp.exp(x - jnp.tile(mn, (1, rep))) for x, mn in zip(s, m_new)]
            for h, a, pp, mn in zip(hs, alpha, p, m_new):
                l_ref[h] = a * l_ref[h] + jnp.sum(pp, axis=-1, keepdims=True)
                m_ref[h] = mn
            pv = _dot(p[0].astype(BF16), v_ref[0, hs[0], ks, :]) + _dot(p[1].astype(BF16), v_ref[0, hs[1], ks, :])
            acc_ref[pr] = acc_ref[pr] * jnp.where(lane < MLA_V, alpha[0], alpha[1]) + pv

    def body(kt, c):
        tile(kt, False)
        return c

    lax.fori_loop(0, qi, body, 0)
    tile(qi, True)
    for pr in range(MLA_HEADS // 2):
        linv = jnp.where(lane < MLA_V, 1.0 / l_ref[2 * pr], 1.0 / l_ref[2 * pr + 1])
        o_ref[0, :, pr * LANES:(pr + 1) * LANES] = (acc_ref[pr] * linv).astype(o_ref.dtype)


def _rope_tables(seq):
    pos = jnp.arange(seq, dtype=F32)
    inv = 1.0 / (ROPE_THETA ** (jnp.arange(0, MLA_ROPE, 2, dtype=F32) / MLA_ROPE))
    ang = pos[:, None] * inv[None, :]
    cos, sin = jnp.cos(ang), jnp.sin(ang)
    cos2 = jnp.concatenate([cos, cos], axis=-1)
    sin2 = jnp.concatenate([sin, sin], axis=-1)
    z64 = jnp.zeros((seq, MLA_NOPE), F32)
    z32 = jnp.zeros((seq, LANES - MLA_NOPE - MLA_ROPE), F32)
    c1 = jnp.concatenate([jnp.ones((seq, MLA_NOPE), F32), cos2, z32], axis=-1)
    c0 = jnp.concatenate([z64, cos2, z32], axis=-1)
    s0 = jnp.concatenate([z64, sin2, z32], axis=-1)
    return c1, c0, s0


def _rot_half_cols(w):
    half = MLA_ROPE // 2
    return jnp.concatenate([-w[..., half:], w[..., :half]], axis=-1)


def mla_mixer(proj3, q_norm, w_uq, kv_norm, w_ukv, tabs, *, tm, tq):
    bsz, seq, _ = proj3.shape
    H = MLA_HEADS
    scale = (MLA_NOPE + MLA_ROPE) ** -0.5
    wq = (w_uq.astype(F32) * scale).reshape(MLA_Q_RANK, H, MLA_NOPE + MLA_ROPE)
    zq = jnp.zeros((MLA_Q_RANK, H, LANES - MLA_NOPE - MLA_ROPE), F32)
    z64 = jnp.zeros((MLA_Q_RANK, H, MLA_NOPE), F32)
    wqa = jnp.concatenate([wq, zq], axis=-1).reshape(MLA_Q_RANK, H * LANES)
    wqb = jnp.concatenate([z64, _rot_half_cols(wq[..., MLA_NOPE:]), zq], axis=-1).reshape(MLA_Q_RANK, H * LANES)
    padq = ((0, 256 - MLA_Q_RANK), (0, 0))
    wqa = jnp.pad(wqa, padq).astype(BF16)
    wqb = jnp.pad(wqb, padq).astype(BF16)
    gq = jnp.pad(q_norm.astype(F32), (0, 256 - MLA_Q_RANK)).reshape(1, 256)
    wkv = w_ukv.astype(F32).reshape(MLA_KV_RANK, H, MLA_NOPE + MLA_V)
    zk = jnp.zeros((MLA_KV_RANK, H, MLA_NOPE), F32)
    wk = jnp.concatenate([wkv[..., :MLA_NOPE], zk], axis=-1).reshape(MLA_KV_RANK, H * LANES).astype(BF16)
    wv_h = wkv[..., MLA_NOPE:]
    even = (jnp.arange(H) % 2 == 0)[None, :, None]
    wv = jnp.concatenate([jnp.where(even, wv_h, 0.0), jnp.where(even, 0.0, wv_h)], axis=-1)
    wv = wv.reshape(MLA_KV_RANK, H * LANES).astype(BF16)
    c1, c0, s0 = tabs
    full2 = lambda shape: pl.BlockSpec(shape, lambda b, i: (0,) * len(shape))
    tab_spec = pl.BlockSpec((tm, LANES), lambda b, i: (i, 0))
    hd_spec = pl.BlockSpec((1, H, tm, LANES), lambda b, i: (b, 0, i, 0))
    hd_shape = jax.ShapeDtypeStruct((bsz, H, seq, LANES), BF16)
    q, k, v = pl.pallas_call(
        _mla_prep_kernel,
        out_shape=(hd_shape, hd_shape, hd_shape),
        grid=(bsz, seq // tm),
        in_specs=[pl.BlockSpec((1, tm, 256), lambda b, i: (b, i, C_CQ // 256)),
                  pl.BlockSpec((1, tm, 256), lambda b, i: (b, i, C_KR // 256)),
                  pl.BlockSpec((1, tm, 128), lambda b, i: (b, i, C_CKV // 128)),
                  full2((1, 256)), full2((1, 128)),
                  full2((256, H * LANES)), full2((256, H * LANES)),
                  full2((128, H * LANES)), full2((128, H * LANES)),
                  tab_spec, tab_spec, tab_spec],
        out_specs=(hd_spec, hd_spec, hd_spec),
        compiler_params=_cparams(("parallel", "parallel")),
        name="mla_prep",
    )(proj3, proj3, proj3, gq, kv_norm.astype(F32).reshape(1, 128), wqa, wqb, wk, wv, c1, c0, s0)

    return pl.pallas_call(
        functools.partial(_mla_flash_kernel, tq=tq),
        out_shape=jax.ShapeDtypeStruct((bsz, seq, H * MLA_V), BF16),
        grid=(bsz, seq // tq),
        in_specs=[pl.BlockSpec((1, H, tq, LANES), lambda b, i: (b, 0, i, 0)),
                  pl.BlockSpec((1, H, seq, LANES), lambda b, i: (b, 0, 0, 0)),
                  pl.BlockSpec((1, H, seq, LANES), lambda b, i: (b, 0, 0, 0))],
        out_specs=pl.BlockSpec((1, tq, H * MLA_V), lambda b, i: (b, i, 0)),
        scratch_shapes=[pltpu.VMEM((H, tq, LANES), F32), pltpu.VMEM((H, tq, LANES), F32),
                        pltpu.VMEM((H // 2, tq, LANES), F32)],
        compiler_params=_cparams(("parallel", "arbitrary")),
        name="mla_flash",
    )(q, k, v)


def _nsa_cmp_kernel(x_ref, pea_ref, peb_ref, w1a_ref, w1b_ref, w2_ref, o_ref):
    x = x_ref[0, 0]
    w1a, w1b = w1a_ref[0], w1b_ref[0]
    bias = _dot(pea_ref[0], w1a)[0:1] + _dot(peb_ref[0], w1b)[0:1]
    a = _dot(x, w1a)
    b = _dot(x, w1b)
    n = b.shape[0]
    pre = a + pltpu.roll(b, n - 1, 0) + bias
    o_ref[0, 0] = _dot(jax.nn.gelu(pre).astype(BF16), w2_ref[0]).astype(o_ref.dtype)


def nsa_compress(kv_cr, cmp_pe, cmp_w1, cmp_w2):
    bsz, _, nch, width = kv_cr.shape
    G, dh = NSA_KV_HEADS, NSA_DIM
    half = CMP_BLOCK // 2
    eye = jnp.eye(G, dtype=F32)
    w1r = cmp_w1.astype(F32).reshape(2, CMP_BLOCK, dh, dh)
    w1a = jnp.einsum('kpde,gh->kpgdhe', w1r[:, :half], eye).reshape(2, width, G * dh).astype(BF16)
    w1b = jnp.einsum('kpde,gh->kpgdhe', w1r[:, half:], eye).reshape(2, width, G * dh).astype(BF16)
    w2 = jnp.einsum('kde,gh->kgdhe', cmp_w2.astype(F32), eye).reshape(2, G * dh, G * dh).astype(BF16)
    pe = cmp_pe.astype(F32)
    pe_g = jnp.broadcast_to(pe[:, :, None, :], (2, CMP_BLOCK, G, dh))
    pea = jnp.broadcast_to(pe_g[:, :half].reshape(2, 1, width), (2, 8, width)).astype(BF16)
    peb = jnp.broadcast_to(pe_g[:, half:].reshape(2, 1, width), (2, 8, width)).astype(BF16)
    kvspec = lambda shape: pl.BlockSpec(shape, lambda b, k: (k,) + (0,) * (len(shape) - 1))
    return pl.pallas_call(
        _nsa_cmp_kernel,
        out_shape=jax.ShapeDtypeStruct((bsz, 2, nch, G * dh), BF16),
        grid=(bsz, 2),
        in_specs=[pl.BlockSpec((1, 1, nch, width), lambda b, k: (b, k, 0, 0)),
                  kvspec((1, 8, width)), kvspec((1, 8, width)),
                  kvspec((1, width, G * dh)), kvspec((1, width, G * dh)),
                  kvspec((1, G * dh, G * dh))],
        out_specs=pl.BlockSpec((1, 1, nch, G * dh), lambda b, k: (b, k, 0, 0)),
        compiler_params=_cparams(("parallel", "parallel")),
        name="nsa_compress",
    )(kv_cr, pea, peb, w1a, w1b, w2)


def _nsa_kernel(q_ref, gate_ref, ksl_ref, vsl_ref, kwn_ref, vwn_ref, kvc_ref, bc_ref, bw_ref,
                bs_ref, ov_ref, blk1h_ref, o_ref, kaug_ref, vaug_ref, m_ref, acc_ref, *, tq, tk, nbs, n_sel):
    qi = pl.program_id(1)
    R, G = NSA_REP, NSA_KV_HEADS
    H = R * G
    nsub = tk // LANES

    @pl.when(qi == 0)
    def _():
        kaug_ref[:, :LANES] = ksl_ref[0]
        kaug_ref[:, LANES:] = blk1h_ref[...]
        vaug_ref[:, :LANES] = vsl_ref[0]
        vaug_ref[:, LANES:] = jnp.ones((vaug_ref.shape[0], LANES), BF16)

    lane = lax.broadcasted_iota(jnp.int32, (tq, LANES), 1)
    t_row = qi * tq + lax.broadcasted_iota(jnp.int32, (H * tq, 1), 0) % tq
    gs = _sigmoid(gate_ref[0].astype(F32))
    kc = kvc_ref[0, 0]
    vc = kvc_ref[0, 1]
    ov = ov_ref[...]

    def stack(fn):
        return jnp.concatenate([fn(h) for h in range(H)], axis=0)

    q_all = stack(lambda h: q_ref[0, :, h * LANES:(h + 1) * LANES])

    c = jnp.minimum(qi, 2)
    ws = pl.ds(pl.multiple_of(jnp.maximum(qi - 2, 0) * tq, tq), 3 * tq)
    s_w = _dot_t(q_all, kwn_ref[0, ws, :]) + stack(lambda h: bw_ref[h, c])
    p_w = jnp.exp(s_w - jnp.max(s_w, axis=-1, keepdims=True))
    o_w = _dot(p_w.astype(BF16), vwn_ref[0, ws, :]) / jnp.sum(p_w, axis=-1, keepdims=True)

    valid = t_row >= (CMP_BLOCK - 1)
    s = _dot_t(q_all, kc) + stack(lambda h: bc_ref[h])
    m = jnp.max(s, axis=-1, keepdims=True)
    p = jnp.where(valid, jnp.exp(s - m), 0.0)
    l = jnp.where(valid, jnp.sum(p, axis=-1, keepdims=True), 1.0)
    pc = p / l
    o_c = _dot(pc.astype(BF16), vc)
    blk = lax.broadcasted_iota(jnp.int32, (nbs, tq), 0)
    tl = qi * tq + lax.broadcasted_iota(jnp.int32, (nbs, tq), 1)
    cur = tl // SEL_BLOCK
    forced = (blk == 0) | (blk == cur) | (blk == cur - 1)
    future = blk * SEL_BLOCK > tl
    qmask = []
    for g in range(G):
        b0 = g * R * tq
        psum = pc[b0:b0 + tq] + pc[b0 + tq:b0 + 2 * tq] + pc[b0 + 2 * tq:b0 + 3 * tq]
        p_hi = psum.astype(BF16)
        p_lo = (psum - p_hi.astype(F32)).astype(BF16)
        imp = (_dot_t(ov, p_hi) + _dot_t(ov, p_lo))[:nbs]
        imp = jnp.where(forced, FORCE, jnp.where(future, -FORCE, imp))
        rank = jnp.zeros((nbs, tq), F32)
        for i in range(nbs):
            ri = imp[i:i + 1, :]
            beats = (ri > imp) | ((ri == imp) & (blk > i))
            rank = rank + jnp.where(beats, 1.0, 0.0)
        sel = jnp.where(rank < n_sel, 0.0, NEG_INF)
        sel = jnp.concatenate([sel, jnp.zeros((LANES - nbs, tq), F32)], axis=0).T.astype(BF16)
        qmask += [sel] * R
    q_aug = jnp.concatenate([q_all, jnp.concatenate(qmask, axis=0)], axis=1)

    m_ref[...] = jnp.full_like(m_ref, NEG_INF)
    acc_ref[...] = jnp.zeros_like(acc_ref)

    def sel_tile(kt, bias):
        ks = pl.ds(pl.multiple_of(kt * tk, tk), tk)
        s = _dot_t(q_aug, kaug_ref[ks, :])
        if bias is not None:
            s = s + bias
        m_prev = m_ref[...]
        m_new = jnp.maximum(m_prev, jnp.max(s, axis=-1, keepdims=True))
        alpha = jnp.exp(m_prev - m_new)
        p = jnp.exp(s - jnp.tile(m_new, (1, nsub)))
        m_ref[...] = m_new
        acc_ref[...] = acc_ref[...] * jnp.tile(alpha, (1, 2)) + _dot(p.astype(BF16), vaug_ref[ks, :])

    def near_bias(kt):
        cols = []
        for sub in range(nsub):
            d = qi - (kt * nsub + sub)
            cols.append(stack(lambda h: jnp.where(d == 0, bs_ref[h, 0], jnp.where(
                d == 1, bs_ref[h, 1], jnp.where(d < 0, NEG_INF, 0.0)))))
        return jnp.concatenate(cols, axis=1)

    def far_body(kt, c):
        sel_tile(kt, None)
        return c

    kd = (qi * tq) // tk
    lax.fori_loop(0, jnp.maximum(kd - 1, 0), far_body, 0)

    @pl.when(kd >= 1)
    def _():
        sel_tile(kd - 1, near_bias(kd - 1))

    sel_tile(kd, near_bias(kd))
    o_s = acc_ref[:, :LANES] / acc_ref[:, LANES:]

    for r in range(R):
        res = []
        for g in range(G):
            h = g * R + r
            rs = slice(h * tq, (h + 1) * tq)
            res.append(gs[:, h:h + 1] * o_c[rs] + gs[:, H + h:H + h + 1] * o_s[rs]
                       + gs[:, 2 * H + h:2 * H + h + 1] * o_w[rs])
        o_ref[0, :, r * LANES:(r + 1) * LANES] = jnp.where(lane < NSA_DIM, res[0], res[1]).astype(o_ref.dtype)


def _t5_bucket(dist):
    n = jnp.maximum(dist, 0)
    exact = REL_BUCKETS // 2
    nf = jnp.maximum(n, exact).astype(F32)
    large = exact + (jnp.log(nf / exact) / math.log(REL_MAX_DIST / exact)
                     * (REL_BUCKETS - exact)).astype(jnp.int32)
    return jnp.where(n < exact, n, jnp.minimum(large, REL_BUCKETS - 1))


def _nsa_bias_tables(rel_bias, seq, tq):
    rb = rel_bias.astype(F32).T
    far = rb[:, REL_BUCKETS - 1].reshape(NSA_HEADS, 1, 1)

    def by_dist(dist, ok, shift=0.0):
        bucket = _t5_bucket(dist)[None]
        out = jnp.zeros((NSA_HEADS,) + dist.shape, F32)
        for k in range(REL_BUCKETS):
            out = jnp.where(bucket == k, rb[:, k].reshape((NSA_HEADS,) + (1,) * dist.ndim), out)
        return jnp.where(ok[None], out - shift, NEG_INF)

    i = jnp.arange(tq)[:, None]
    t = jnp.arange(seq)[:, None]
    dist_c = t - (jnp.arange(LANES)[None, :] * CMP_STRIDE + CMP_BLOCK - 1)
    bc = by_dist(dist_c, dist_c >= 0)
    jw = jnp.arange(3 * tq)[None, :]
    bw = jnp.stack([by_dist(tq * c + i - jw, (tq * c + i - jw >= 0) & (tq * c + i - jw < WINDOW))
                    for c in range(3)], axis=1)
    js = jnp.arange(tq)[None, :]
    bs = jnp.stack([by_dist(tq * c + i - js, tq * c + i - js >= 0, far) for c in range(2)], axis=1)
    ci = jnp.arange(LANES)[:, None]
    sj = jnp.arange(LANES)[None, :]
    nbs = seq // SEL_BLOCK
    ov = ((ci * CMP_STRIDE <= sj * SEL_BLOCK + SEL_BLOCK - 1)
          & (ci * CMP_STRIDE + CMP_BLOCK - 1 >= sj * SEL_BLOCK)
          & (ci < seq // CMP_STRIDE - 1) & (sj < nbs))
    blk1h = (jnp.arange(seq)[:, None] // SEL_BLOCK == sj).astype(BF16)
    return bc, bw, bs, ov.T.astype(BF16), blk1h


def nsa_mixer(proj3, kvc, tables, *, tq, tk):
    bsz, seq, _ = proj3.shape
    assert tq == LANES and tq >= REL_MAX_DIST and tk % tq == 0 and seq % tk == 0
    assert seq // CMP_STRIDE == LANES and tq * 2 == WINDOW
    bc, bw, bs, ov, blk1h = tables
    nbs = seq // SEL_BLOCK
    H = NSA_HEADS
    slab = lambda j: pl.BlockSpec((1, seq, LANES), lambda b, i: (b, 0, C_NKV // LANES + j))
    const = lambda shape: pl.BlockSpec(shape, lambda b, i: (0,) * len(shape))
    return pl.pallas_call(
        functools.partial(_nsa_kernel, tq=tq, tk=tk, nbs=nbs, n_sel=min(SEL_TOPN, nbs)),
        out_shape=jax.ShapeDtypeStruct((bsz, seq, H * NSA_DIM), BF16),
        grid=(bsz, seq // tq),
        in_specs=[pl.BlockSpec((1, tq, H * LANES), lambda b, i: (b, i, C_NQ // (H * LANES))),
                  pl.BlockSpec((1, tq, LANES), lambda b, i: (b, i, C_GATE // LANES)),
                  slab(2), slab(3), slab(4), slab(5),
                  pl.BlockSpec((1, 2, LANES, LANES), lambda b, i: (b, 0, 0, 0)),
                  pl.BlockSpec((H, tq, LANES), lambda b, i: (0, i, 0)),
                  const((H, 3, tq, 3 * tq)), const((H, 2, tq, tq)),
                  const((LANES, LANES)), const((seq, LANES))],
        out_specs=pl.BlockSpec((1, tq, H * NSA_DIM), lambda b, i: (b, i, 0)),
        scratch_shapes=[pltpu.VMEM((seq, 2 * LANES), BF16), pltpu.VMEM((seq, 2 * LANES), BF16),
                        pltpu.VMEM((H * tq, LANES), F32), pltpu.VMEM((H * tq, 2 * LANES), F32)],
        compiler_params=_cparams(("parallel", "arbitrary")),
        name="nsa_attention",
    )(proj3, proj3, proj3, proj3, proj3, proj3, kvc, bc, bw, bs, ov, blk1h)


def _outproj_kernel(ys_ref, ym_ref, yn_ref, h_ref, g1_ref, g2_ref, g3_ref, w1_ref, w2_ref, w3_ref, o_ref):
    def part(y_ref, g_ref, w_ref):
        y = y_ref[...].astype(F32)
        return _dot(_rms(y, g_ref[...], y.shape[-1]).astype(BF16), w_ref[...])

    o_ref[...] = h_ref[...] + part(ys_ref, g1_ref, w1_ref) + part(ym_ref, g2_ref, w2_ref) + part(yn_ref, g3_ref, w3_ref)


def out_projection(y_ssm, y_mla, y_nsa, h, gains, weights, *, tm):
    m, d = h.shape
    row = lambda w: pl.BlockSpec((tm, w), lambda i: (i, 0))
    full = lambda a: pl.BlockSpec(a.shape, lambda i: (0, 0))
    gains = [g.reshape(1, -1).astype(F32) for g in gains]
    return pl.pallas_call(
        _outproj_kernel,
        out_shape=jax.ShapeDtypeStruct((m, d), F32),
        grid=(m // tm,),
        in_specs=[row(y_ssm.shape[1]), row(y_mla.shape[1]), row(y_nsa.shape[1]), row(d)]
                 + [full(g) for g in gains] + [full(w) for w in weights],
        out_specs=row(d),
        compiler_params=_cparams(("parallel",)),
        name="out_projection",
    )(y_ssm, y_mla, y_nsa, h, *gains, *weights)


def _xattn_kernel(h_ref, kv_ref, g_ref, wq_ref, wo_ref, o_ref, *, dh):
    h = h_ref[0]
    xn = _rms(h, g_ref[...], h.shape[-1]).astype(BF16)
    q = _dot(xn, wq_ref[...]).astype(BF16)
    hw = XATTN_HEADS * dh
    outs = []
    for hd in range(XATTN_HEADS):
        k = kv_ref[0, :, hd * dh:(hd + 1) * dh]
        v = kv_ref[0, :, hw + hd * dh:hw + (hd + 1) * dh]
        s = _dot_t(q[:, hd * dh:(hd + 1) * dh], k)
        p = jnp.exp(s - jnp.max(s, axis=-1, keepdims=True))
        p = p / jnp.sum(p, axis=-1, keepdims=True)
        outs.append(_dot(p.astype(BF16), v).astype(BF16))
    o = jnp.concatenate(outs, axis=-1)
    o_ref[0] = h + _dot(o, wo_ref[...])


def cross_attention(h3, kv3, g_x, wq, wo, *, tm):
    bsz, seq, d = h3.shape
    m = kv3.shape[1]
    dh = d // XATTN_HEADS
    wq_s = (wq.astype(F32) * dh ** -0.5).astype(BF16)
    const = lambda shape: pl.BlockSpec(shape, lambda b, i: (0,) * len(shape))
    return pl.pallas_call(
        functools.partial(_xattn_kernel, dh=dh),
        out_shape=jax.ShapeDtypeStruct((bsz, seq, d), F32),
        grid=(bsz, seq // tm),
        in_specs=[pl.BlockSpec((1, tm, d), lambda b, i: (b, i, 0)),
                  pl.BlockSpec((1, m, 2 * d), lambda b, i: (b, 0, 0)),
                  const((1, d)), const((d, d)), const((d, d))],
        out_specs=pl.BlockSpec((1, tm, d), lambda b, i: (b, i, 0)),
        compiler_params=_cparams(("parallel", "parallel")),
        name="cross_attention",
    )(h3, kv3, g_x.reshape(1, d).astype(F32), wq_s, wo.astype(BF16))


def _ffn_kernel(h_ref, g_ref, wg_ref, wu_ref, wd_ref, o_ref, xn_ref, acc_ref):
    j = pl.program_id(1)

    @pl.when(j == 0)
    def _():
        h = h_ref[...]
        xn_ref[...] = _rms(h, g_ref[...], h.shape[-1]).astype(BF16)
        acc_ref[...] = h

    xn = xn_ref[...]
    a = _silu(_dot(xn, wg_ref[...])) * _dot(xn, wu_ref[...])
    acc_ref[...] += _dot(a.astype(BF16), wd_ref[...])

    @pl.when(j == pl.num_programs(1) - 1)
    def _():
        o_ref[...] = acc_ref[...]


def dense_ffn(h, g, wg, wu, wd, *, tm, tf):
    m, d = h.shape
    ff = wg.shape[1]
    return pl.pallas_call(
        _ffn_kernel,
        out_shape=jax.ShapeDtypeStruct((m, d), F32),
        grid=(m // tm, ff // tf),
        in_specs=[pl.BlockSpec((tm, d), lambda i, j: (i, 0)),
                  pl.BlockSpec((1, d), lambda i, j: (0, 0)),
                  pl.BlockSpec((d, tf), lambda i, j: (0, j)),
                  pl.BlockSpec((d, tf), lambda i, j: (0, j)),
                  pl.BlockSpec((tf, d), lambda i, j: (j, 0))],
        out_specs=pl.BlockSpec((tm, d), lambda i, j: (i, 0)),
        scratch_shapes=[pltpu.VMEM((tm, d), BF16), pltpu.VMEM((tm, d), F32)],
        compiler_params=_cparams(("parallel", "arbitrary")),
        name="dense_ffn",
    )(h, g.reshape(1, d).astype(F32), wg.astype(BF16), wu.astype(BF16), wd.astype(BF16))


def _router_kernel(h_ref, g_ref, wr_hi_ref, wr_lo_ref, xn_ref, info_ref, cnt_ref, carry_ref, *, tm):
    i = pl.program_id(0)

    @pl.when(i == 0)
    def _():
        carry_ref[...] = jnp.zeros_like(carry_ref)

    h = h_ref[...]
    xn = _rms(h, g_ref[...], h.shape[-1])
    xn_ref[...] = xn
    x_hi = xn.astype(BF16)
    x_lo = (xn - x_hi.astype(F32)).astype(BF16)
    logits = _dot(x_hi, wr_hi_ref[...]) + _dot(x_lo, wr_hi_ref[...]) + _dot(x_hi, wr_lo_ref[...])
    lane = lax.broadcasted_iota(jnp.int32, (tm, LANES), 1)
    lanef = lane.astype(F32)
    logits = jnp.where(lane < N_EXPERTS, logits, NEG_INF)
    m1 = jnp.max(logits, axis=-1, keepdims=True)
    i1 = jnp.min(jnp.where(logits == m1, lanef, float(LANES)), axis=-1, keepdims=True)
    rest = jnp.where(lanef == i1, NEG_INF, logits)
    m2 = jnp.max(rest, axis=-1, keepdims=True)
    i2 = jnp.min(jnp.where(rest == m2, lanef, float(LANES)), axis=-1, keepdims=True)
    e2 = jnp.exp(m2 - m1)
    w1 = 1.0 / (1.0 + e2)
    w2 = e2 / (1.0 + e2)
    oh1 = lanef == i1
    oh2 = lanef == i2
    oh = jnp.where(oh1 | oh2, 1.0, 0.0)
    rr = lax.broadcasted_iota(jnp.int32, (tm, tm), 0)
    cc = lax.broadcasted_iota(jnp.int32, (tm, tm), 1)
    tri = jnp.where(cc < rr, 1.0, 0.0).astype(BF16)
    before = _dot(tri, oh.astype(BF16)) + carry_ref[0:1, :]
    r1 = jnp.sum(jnp.where(oh1, before, 0.0), axis=-1, keepdims=True)
    r2 = jnp.sum(jnp.where(oh2, before, 0.0), axis=-1, keepdims=True)
    carry_ref[...] = carry_ref[...] + jnp.sum(oh, axis=0, keepdims=True)
    info = jnp.where(lane == 0, i1, jnp.where(lane == 1, i2, jnp.where(lane == 2, w1, jnp.where(
        lane == 3, w2, jnp.where(lane == 4, r1, jnp.where(lane == 5, r2, 0.0))))))
    info_ref[...] = info
    cnt_ref[...] = carry_ref[...]


def moe_router(h, g, router, *, tm):
    m, d = h.shape
    wr = jnp.pad(router.astype(F32), ((0, 0), (0, LANES - N_EXPERTS)))
    wr_hi = wr.astype(BF16)
    wr_lo = (wr - wr_hi.astype(F32)).astype(BF16)
    return pl.pallas_call(
        functools.partial(_router_kernel, tm=tm),
        out_shape=(jax.ShapeDtypeStruct((m, d), F32), jax.ShapeDtypeStruct((m, LANES), F32),
                   jax.ShapeDtypeStruct((8, LANES), F32)),
        grid=(m // tm,),
        in_specs=[pl.BlockSpec((tm, d), lambda i: (i, 0)),
                  pl.BlockSpec((1, d), lambda i: (0, 0)),
                  pl.BlockSpec((d, LANES), lambda i: (0, 0)),
                  pl.BlockSpec((d, LANES), lambda i: (0, 0))],
        out_specs=(pl.BlockSpec((tm, d), lambda i: (i, 0)),
                   pl.BlockSpec((tm, LANES), lambda i: (i, 0)),
                   pl.BlockSpec((8, LANES), lambda i: (0, 0))),
        scratch_shapes=[pltpu.VMEM((8, LANES), F32)],
        compiler_params=_cparams(("arbitrary",)),
        name="moe_router",
    )(h, g.reshape(1, d).astype(F32), wr_hi, wr_lo)


def _row_copy(src_hbm, row, dst, slot, sem):
    return pltpu.make_async_copy(src_hbm.at[pl.ds(row, 1), :], dst.at[pl.ds(slot, 1), :], sem)


def _rows_wait(src_hbm, dst, sem):
    pltpu.make_async_copy(src_hbm.at[pl.ds(0, dst.shape[0]), :], dst, sem).wait()


def _moe_ffn_kernel(src_ref, texp_ref, nact_ref, x_hbm, wg_ref, wu_ref, wd_ref, o_ref,
                    xbuf, xbf, acc_ref, sem, *, tm, nj):
    i = pl.program_id(0)
    j = pl.program_id(1)
    nact = nact_ref[0]
    active = i < nact
    has_next = i + 1 < nact
    cur = i % 2
    rows_per_step = tm // nj

    @pl.when((i == 0) & (j == 0))
    def _():
        def issue(s, c):
            _row_copy(x_hbm, src_ref[s], xbuf.at[0], s, sem.at[0]).start()
            return c

        lax.fori_loop(0, tm, issue, 0, unroll=8)

    @pl.when(active & (j == 0))
    def _():
        _rows_wait(x_hbm, xbuf.at[cur], sem.at[cur])
        xbf[...] = xbuf[cur].astype(BF16)
        acc_ref[...] = jnp.zeros_like(acc_ref)

    def compute(prefetch):
        if prefetch:
            base = (i + 1) * tm + j * rows_per_step
            for k in range(rows_per_step):
                _row_copy(x_hbm, src_ref[base + k], xbuf.at[1 - cur], j * rows_per_step + k,
                          sem.at[1 - cur]).start()
        x = xbf[...]
        a = _silu(_dot(x, wg_ref[0, 0])) * _dot(x, wu_ref[0, 0])
        acc_ref[...] += _dot(a.astype(BF16), wd_ref[0, 0])

    @pl.when(active & has_next)
    def _():
        compute(True)

    @pl.when(active & jnp.logical_not(has_next))
    def _():
        compute(False)

    @pl.when(j == nj - 1)
    def _():
        o_ref[...] = jnp.where(active, acc_ref[...], 0.0)


def moe_expert_ffn(xn, src, tile_expert, n_active, wg, wu, wd, *, tm, tf):
    n_slots = src.shape[0]
    d = xn.shape[1]
    ne, _, ff = wg.shape
    nj = ff // tf
    assert nj * tf == ff and tm % nj == 0
    wg = wg.astype(BF16).reshape(ne, d, nj, tf).transpose(0, 2, 1, 3)
    wu = wu.astype(BF16).reshape(ne, d, nj, tf).transpose(0, 2, 1, 3)
    wd = wd.astype(BF16).reshape(ne, nj, tf, d)

    def wmap(i, j, src, texp, nact):
        return (texp[i], jnp.where(i < nact[0], j, nj - 1), 0, 0)

    return pl.pallas_call(
        functools.partial(_moe_ffn_kernel, tm=tm, nj=nj),
        out_shape=jax.ShapeDtypeStruct((n_slots, d), F32),
        grid_spec=pltpu.PrefetchScalarGridSpec(
            num_scalar_prefetch=3,
            grid=(n_slots // tm, nj),
            in_specs=[pl.BlockSpec(memory_space=pl.ANY),
                      pl.BlockSpec((1, 1, d, tf), wmap),
                      pl.BlockSpec((1, 1, d, tf), wmap),
                      pl.BlockSpec((1, 1, tf, d), wmap)],
            out_specs=pl.BlockSpec((tm, d), lambda i, j, *_: (i, 0)),
            scratch_shapes=[pltpu.VMEM((2, tm, d), F32), pltpu.VMEM((tm, d), BF16),
                            pltpu.VMEM((tm, d), F32), pltpu.SemaphoreType.DMA((2,))]),
        compiler_params=_cparams(("arbitrary", "arbitrary")),
        name="moe_expert_ffn",
    )(src, tile_expert, n_active, xn, wg, wu, wd)


def _moe_combine_kernel(pos_ref, h_ref, info_ref, ys_hbm, g_ref, o_ref, buf, sem, *, tm, final_norm):
    i = pl.program_id(0)
    n = pl.num_programs(0)
    cur = i % 2

    def start_gather(tile, b):
        def issue(s, c):
            for k in range(2):
                _row_copy(ys_hbm, pos_ref[2 * (tile * tm + s) + k], buf.at[b, k], s, sem.at[b]).start()
            return c

        lax.fori_loop(0, tm, issue, 0, unroll=8)

    @pl.when(i == 0)
    def _():
        start_gather(0, 0)

    @pl.when(i + 1 < n)
    def _():
        start_gather(i + 1, 1 - cur)

    for k in range(2):
        _rows_wait(ys_hbm, buf.at[cur, k], sem.at[cur])
    info = info_ref[...]
    y = h_ref[...] + info[:, 2:3] * buf[cur, 0] + info[:, 3:4] * buf[cur, 1]
    if final_norm:
        y = _rms(y, g_ref[...], y.shape[-1])
    o_ref[...] = y


def moe_combine(h, info, ys, pos_flat, g_final, *, tm, final_norm):
    m, d = h.shape
    return pl.pallas_call(
        functools.partial(_moe_combine_kernel, tm=tm, final_norm=final_norm),
        out_shape=jax.ShapeDtypeStruct((m, d), F32),
        grid_spec=pltpu.PrefetchScalarGridSpec(
            num_scalar_prefetch=1,
            grid=(m // tm,),
            in_specs=[pl.BlockSpec((tm, d), lambda i, *_: (i, 0)),
                      pl.BlockSpec((tm, LANES), lambda i, *_: (i, 0)),
                      pl.BlockSpec(memory_space=pl.ANY),
                      pl.BlockSpec((1, d), lambda i, *_: (0, 0))],
            out_specs=pl.BlockSpec((tm, d), lambda i, *_: (i, 0)),
            scratch_shapes=[pltpu.VMEM((2, 2, tm, d), F32), pltpu.SemaphoreType.DMA((2,))]),
        compiler_params=_cparams(("arbitrary",)),
        name="moe_combine",
    )(pos_flat, h, info, ys, g_final.reshape(1, d).astype(F32))


def moe_layer(h, g, router, wg, wu, wd, g_final, *, final_norm, tm_r=512, tm_g=512, tf=896, tm_c=256):
    m, d = h.shape
    xn, info, cnt = moe_router(h, g, router, tm=tm_r)
    e_idx = info[:, 0:2].astype(jnp.int32)
    rank = info[:, 4:6].astype(jnp.int32)
    counts = cnt[0, :N_EXPERTS].astype(jnp.int32)
    tiles_per = (counts + tm_g - 1) // tm_g
    tile_end = jnp.cumsum(tiles_per)
    seg_start = (tile_end - tiles_per) * tm_g
    pos = rank
    for e in range(N_EXPERTS):
        pos = pos + jnp.where(e_idx == e, seg_start[e], 0)
    n_tiles = (2 * m) // tm_g + N_EXPERTS
    n_slots = n_tiles * tm_g
    tok = jnp.broadcast_to(jnp.arange(m, dtype=jnp.int32)[:, None], (m, 2))
    src = jnp.zeros((n_slots,), jnp.int32).at[pos.reshape(-1)].set(tok.reshape(-1))
    n_active = tile_end[-1:].astype(jnp.int32)
    tile_ids = jnp.minimum(jnp.arange(n_tiles, dtype=jnp.int32), n_active[0] - 1)
    tile_expert = jnp.sum(tile_ids[:, None] >= tile_end[None, :], axis=1).astype(jnp.int32)
    ys = moe_expert_ffn(xn, src, tile_expert, n_active,
                        wg, wu, wd, tm=tm_g, tf=tf)
    return moe_combine(h, info, ys, pos.reshape(-1).astype(jnp.int32), g_final, tm=tm_c, final_norm=final_norm)


def _final_norm_kernel(h_ref, g_ref, o_ref):
    h = h_ref[...]
    o_ref[...] = _rms(h, g_ref[...], h.shape[-1])


def final_rmsnorm(h, g, *, tm):
    m, d = h.shape
    return pl.pallas_call(
        _final_norm_kernel,
        out_shape=jax.ShapeDtypeStruct((m, d), F32),
        grid=(m // tm,),
        in_specs=[pl.BlockSpec((tm, d), lambda i: (i, 0)), pl.BlockSpec((1, d), lambda i: (0, 0))],
        out_specs=pl.BlockSpec((tm, d), lambda i: (i, 0)),
        compiler_params=_cparams(("parallel",)),
        name="final_rmsnorm",
    )(h, g.reshape(1, d).astype(F32))


def _pack_w_in(w):
    d = w.shape[0]
    w = w.astype(F32)
    o = 0
    u = w[:, o:o + SSM_WIDTH]; o += SSM_WIDTH
    cq = w[:, o:o + MLA_Q_RANK]; o += MLA_Q_RANK
    ckv = w[:, o:o + MLA_KV_RANK]; o += MLA_KV_RANK
    kr = w[:, o:o + MLA_ROPE]; o += MLA_ROPE
    nq = w[:, o:o + NSA_HEADS * NSA_DIM]; o += NSA_HEADS * NSA_DIM
    nkv = w[:, o:o + 6 * NSA_KV_HEADS * NSA_DIM]; o += 6 * NSA_KV_HEADS * NSA_DIM
    gate = w[:, o:o + 3 * NSA_HEADS]
    z = lambda n: jnp.zeros((d, n), F32)
    kr_a = jnp.concatenate([z(MLA_NOPE), kr, z(LANES - MLA_NOPE - MLA_ROPE)], axis=1)
    kr_b = jnp.concatenate([z(MLA_NOPE), _rot_half_cols(kr), z(LANES - MLA_NOPE - MLA_ROPE)], axis=1)
    nq_h = (nq * NSA_DIM ** -0.5).reshape(d, NSA_KV_HEADS, NSA_REP, NSA_DIM)
    zq = jnp.zeros((d, NSA_REP, NSA_DIM), F32)
    nq_p = jnp.concatenate([
        jnp.concatenate([nq_h[:, 0], zq], axis=-1).reshape(d, NSA_REP * LANES),
        jnp.concatenate([zq, nq_h[:, 1]], axis=-1).reshape(d, NSA_REP * LANES)], axis=1)
    packed = jnp.concatenate([u, cq, z(256 - MLA_Q_RANK), kr_a, kr_b, nq_p, nkv, ckv,
                              gate, z(LANES - 3 * NSA_HEADS)], axis=1)
    assert packed.shape[1] == IN_COLS_PACKED
    return packed.astype(BF16)


def _rg_order(a):
    rest = a.shape[1:]
    return a.reshape((NSA_KV_HEADS, NSA_REP, NSA_DIM) + rest).swapaxes(0, 1).reshape((-1,) + rest)


def kernel(x, mem, w_in, w_out, mix_norm, out_norm, ssm_a_re, ssm_a_im, ssm_b_re, ssm_b_im, ssm_c_re, ssm_c_im, ssm_d, ssm_log_dt, ssm_w_glu, mla_q_norm, mla_w_uq, mla_kv_norm, mla_w_ukv, nsa_cmp_pe, nsa_cmp_w1, nsa_cmp_w2, rel_bias, xattn_norm, mem_norm, xattn_wq, xattn_wkv, xattn_wo, ffn_norm, dense_w_gate, dense_w_up, dense_w_down, moe_router, moe_w_gate, moe_w_up, moe_w_down, final_norm):
    bsz, seq, d = x.shape
    depth = w_in.shape[0]
    T = bsz * seq
    nmem = mem.shape[1]
    tq_nsa, tk_nsa = LANES, 4 * LANES
    rope_tabs = _rope_tables(seq)
    nsa_tabs = _nsa_bias_tables(rel_bias, seq, tq_nsa)
    o1, o2 = SSM_WIDTH, SSM_WIDTH + MLA_HEADS * MLA_V
    mem2 = mem.reshape(bsz * nmem, d)
    h = x.reshape(T, d)
    for l in range(depth):
        proj = norm_matmul(h, mix_norm[l], _pack_w_in(w_in[l]), tm=512, tn=IN_COLS_PACKED, out_dtype=BF16)
        proj3 = proj.reshape(bsz, seq, IN_COLS_PACKED)
        u_tm = proj3[:, :, C_U:C_U + SSM_WIDTH].transpose(1, 0, 2).reshape(seq * bsz, SSM_WIDTH)
        y_ssm = ssm_mixer(u_tm, ssm_a_re[l], ssm_a_im[l], ssm_b_re[l], ssm_b_im[l], ssm_c_re[l], ssm_c_im[l],
                          ssm_d[l], ssm_log_dt[l], ssm_w_glu[l], nb=bsz, tc=64)
        y_ssm = y_ssm.reshape(seq, bsz, SSM_WIDTH).transpose(1, 0, 2).reshape(T, SSM_WIDTH)
        y_mla = mla_mixer(proj3, mla_q_norm[l], mla_w_uq[l], mla_kv_norm[l], mla_w_ukv[l], rope_tabs,
                          tm=512, tq=512).reshape(T, -1)
        nch = seq // CMP_STRIDE
        kv_cr = jnp.stack([proj3[:, :, C_NKV:C_NKV + LANES].reshape(bsz, nch, CMP_STRIDE * LANES),
                           proj3[:, :, C_NKV + LANES:C_NKV + 2 * LANES].reshape(bsz, nch, CMP_STRIDE * LANES)],
                          axis=1)
        kvc = nsa_compress(kv_cr, nsa_cmp_pe[l], nsa_cmp_w1[l], nsa_cmp_w2[l])
        y_nsa = nsa_mixer(proj3, kvc, nsa_tabs, tq=tq_nsa, tk=tk_nsa).reshape(T, -1)
        g_out = out_norm[l]
        wo_l = w_out[l]
        h = out_projection(y_ssm, y_mla, y_nsa, h,
                           [g_out[:o1], g_out[o1:o2], _rg_order(g_out[o2:])],
                           [wo_l[:o1].astype(BF16), wo_l[o1:o2].astype(BF16), _rg_order(wo_l[o2:]).astype(BF16)],
                           tm=512)
        kv_mem = norm_matmul(mem2, mem_norm[l], xattn_wkv[l].astype(BF16), tm=256, tn=512, out_dtype=BF16)
        h = cross_attention(h.reshape(bsz, seq, d), kv_mem.reshape(bsz, nmem, 2 * d), xattn_norm[l],
                            xattn_wq[l], xattn_wo[l], tm=256).reshape(T, d)
        last = l == depth - 1
        if l % 2 == 0:
            h = dense_ffn(h, ffn_norm[l], dense_w_gate[l // 2], dense_w_up[l // 2], dense_w_down[l // 2],
                          tm=512, tf=1408)
            if last:
                h = final_rmsnorm(h, final_norm, tm=512)
        else:
            h = moe_layer(h, ffn_norm[l], moe_router[l // 2], moe_w_gate[l // 2], moe_w_up[l // 2],
                          moe_w_down[l // 2], final_norm, final_norm=last)
    return h.reshape(bsz, seq, d)
```

```python
import functools
import math

import jax
import jax.numpy as jnp
from jax import lax
from jax.experimental import pallas as pl
from jax.experimental.pallas import tpu as pltpu

F32 = jnp.float32
BF16 = jnp.bfloat16

HEAD_DIM = 64
SSM_WIDTH = 256
SSM_CH = 16
SSM_GROUPS = 16
SSM_STATE = 64
MLA_HEADS = 6
MLA_NOPE = 64
MLA_ROPE = 32
MLA_V = 64
MLA_Q_RANK = 192
MLA_KV_RANK = 128
NSA_HEADS = 6
NSA_KV_HEADS = 2
NSA_REP = 3
NSA_DIM = 64
CMP_BLOCK = 32
CMP_STRIDE = 16
SEL_BLOCK = 64
SEL_TOPN = 8
WINDOW = 256
REL_BUCKETS = 32
REL_MAX_DIST = 128
XATTN_HEADS = 4
N_EXPERTS = 8
ROPE_THETA = 10000.0
EPS = 1e-6
NEG_INF = -1e30
FORCE = 1e9

LANES = 128
VMEM_LIMIT = 56 * 1024 * 1024

C_U, C_CQ, C_KR, C_NQ, C_NKV, C_CKV, C_GATE = 0, 256, 512, 768, 1536, 2304, 2432
IN_COLS_PACKED = 2560


def _cparams(sem):
    return pltpu.CompilerParams(dimension_semantics=sem, vmem_limit_bytes=VMEM_LIMIT)


def _dot(a, b):
    return jnp.dot(a, b, preferred_element_type=F32)


def _dot_t(a, b):
    return lax.dot_general(a, b, (((1,), (1,)), ((), ())), preferred_element_type=F32)


def _rms(x, g, n):
    ms = jnp.sum(x * x, axis=-1, keepdims=True) * (1.0 / n)
    return x * lax.rsqrt(ms + EPS) * g


def _sigmoid(x):
    return 1.0 / (1.0 + jnp.exp(-x))


def _silu(x):
    return x * _sigmoid(x)


def _norm_mm_kernel(x_ref, g_ref, w_ref, o_ref, xn_ref):
    @pl.when(pl.program_id(1) == 0)
    def _():
        x = x_ref[...].astype(F32)
        xn_ref[...] = _rms(x, g_ref[...], x.shape[-1]).astype(BF16)

    o_ref[...] = _dot(xn_ref[...], w_ref[...]).astype(o_ref.dtype)


def norm_matmul(x, g, w, *, tm, tn, out_dtype):
    m, k = x.shape
    n = w.shape[1]
    return pl.pallas_call(
        _norm_mm_kernel,
        out_shape=jax.ShapeDtypeStruct((m, n), out_dtype),
        grid=(m // tm, n // tn),
        in_specs=[pl.BlockSpec((tm, k), lambda i, j: (i, 0)),
                  pl.BlockSpec((1, k), lambda i, j: (0, 0)),
                  pl.BlockSpec((k, tn), lambda i, j: (0, j))],
        out_specs=pl.BlockSpec((tm, tn), lambda i, j: (i, j)),
        scratch_shapes=[pltpu.VMEM((tm, k), BF16)],
        compiler_params=_cparams(("parallel", "arbitrary")),
        name="norm_matmul",
    )(x, g.reshape(1, k).astype(F32), w)


def _ssm_kernel(u_ref, bbr_ref, bbi_ref, ar_ref, ai_ref, ccr_ref, cci_ref, d_ref, wglu_ref,
                o_ref, hr_ref, hi_ref, cr_ref, ci_ref, *, tc, nb):
    @pl.when(pl.program_id(0) == 0)
    def _():
        cr_ref[...] = jnp.zeros_like(cr_ref)
        ci_ref[...] = jnp.zeros_like(ci_ref)

    u = u_ref[...]
    hr_ref[...] = _dot(u, bbr_ref[...])
    hi_ref[...] = _dot(u, bbi_ref[...])
    gp = ar_ref.shape[-1]
    ar = jnp.broadcast_to(ar_ref[...], (nb, gp))
    ai = jnp.broadcast_to(ai_ref[...], (nb, gp))

    def step(t, carry):
        hr, hi = carry
        rows = pl.ds(pl.multiple_of(t * nb, nb), nb)
        nr = ar * hr - ai * hi + hr_ref[rows, :]
        ni = ar * hi + ai * hr + hi_ref[rows, :]
        hr_ref[rows, :] = nr
        hi_ref[rows, :] = ni
        return nr, ni

    hr, hi = lax.fori_loop(0, tc, step, (cr_ref[...], ci_ref[...]))
    cr_ref[...] = hr
    ci_ref[...] = hi
    y = (_dot(hr_ref[...].astype(BF16), ccr_ref[...]) + _dot(hi_ref[...].astype(BF16), cci_ref[...])
         + d_ref[...] * u.astype(F32))
    y = jax.nn.gelu(y)
    z = _dot(y.astype(BF16), wglu_ref[...])
    o_ref[...] = (y * _sigmoid(z)).astype(o_ref.dtype)


def ssm_mixer(u_tm, a_re, a_im, b_re, b_im, c_re, c_im, d, log_dt, w_glu, *, nb, tc):
    rows = u_tm.shape[0]
    G, P, C = SSM_GROUPS, SSM_STATE, SSM_CH
    dt = jnp.exp(log_dt.astype(F32))[:, None]
    lr, li = a_re.astype(F32), a_im.astype(F32)
    mag = jnp.exp(lr * dt)
    ab_r, ab_i = mag * jnp.cos(li * dt), mag * jnp.sin(li * dt)
    den = lr * lr + li * li
    nr = ab_r - 1.0
    f_r = (nr * lr + ab_i * li) / den
    f_i = (ab_i * lr - nr * li) / den
    br, bi = b_re.astype(F32), b_im.astype(F32)
    bb_r = f_r[..., None] * br - f_i[..., None] * bi
    bb_i = f_r[..., None] * bi + f_i[..., None] * br
    eye = jnp.eye(G, dtype=F32)
    bbr = jnp.einsum('gpc,gh->gchp', bb_r, eye).reshape(G * C, G * P).astype(BF16)
    bbi = jnp.einsum('gpc,gh->gchp', bb_i, eye).reshape(G * C, G * P).astype(BF16)
    ccr = jnp.einsum('gcp,gh->gphc', c_re.astype(F32), eye).reshape(G * P, G * C).astype(BF16)
    cci = jnp.einsum('gcp,gh->gphc', -c_im.astype(F32), eye).reshape(G * P, G * C).astype(BF16)
    gp = G * P
    full = lambda shape: pl.BlockSpec(shape, lambda i: (0,) * len(shape))
    return pl.pallas_call(
        functools.partial(_ssm_kernel, tc=tc, nb=nb),
        out_shape=jax.ShapeDtypeStruct((rows, SSM_WIDTH), BF16),
        grid=(rows // (tc * nb),),
        in_specs=[pl.BlockSpec((tc * nb, SSM_WIDTH), lambda i: (i, 0)),
                  full((G * C, gp)), full((G * C, gp)), full((1, gp)), full((1, gp)),
                  full((gp, G * C)), full((gp, G * C)), full((1, SSM_WIDTH)),
                  full((SSM_WIDTH, SSM_WIDTH))],
        out_specs=pl.BlockSpec((tc * nb, SSM_WIDTH), lambda i: (i, 0)),
        scratch_shapes=[pltpu.VMEM((tc * nb, gp), F32), pltpu.VMEM((tc * nb, gp), F32),
                        pltpu.VMEM((nb, gp), F32), pltpu.VMEM((nb, gp), F32)],
        compiler_params=_cparams(("arbitrary",)),
        name="ssm_mixer",
    )(u_tm, bbr, bbi, ab_r.reshape(1, gp), ab_i.reshape(1, gp), ccr, cci,
      d.reshape(1, SSM_WIDTH).astype(F32), w_glu.astype(BF16))


def _mla_prep_kernel(cq_ref, kr_ref, ckv_ref, gq_ref, gkv_ref, wqa_ref, wqb_ref, wk_ref, wv_ref,
                     c1_ref, c0_ref, s0_ref, q_ref, k_ref, v_ref):
    qn = _rms(cq_ref[0].astype(F32), gq_ref[...], MLA_Q_RANK).astype(BF16)
    qa = _dot(qn, wqa_ref[...])
    qb = _dot(qn, wqb_ref[...])
    kn = _rms(ckv_ref[0].astype(F32), gkv_ref[...], MLA_KV_RANK).astype(BF16)
    ka = _dot(kn, wk_ref[...])
    va = _dot(kn, wv_ref[...])
    kr = kr_ref[0].astype(F32)
    c1, c0, s0 = c1_ref[...], c0_ref[...], s0_ref[...]
    krope = kr[:, :LANES] * c0 + kr[:, LANES:] * s0
    for h in range(MLA_HEADS):
        sl = slice(h * LANES, (h + 1) * LANES)
        q_ref[0, h] = (qa[:, sl] * c1 + qb[:, sl] * s0).astype(BF16)
        k_ref[0, h] = (ka[:, sl] + krope).astype(BF16)
        v_ref[0, h] = va[:, sl].astype(BF16)


def _mla_flash_kernel(q_ref, k_ref, v_ref, o_ref, m_ref, l_ref, acc_ref, *, tq):
    qi = pl.program_id(1)
    m_ref[...] = jnp.full_like(m_ref, NEG_INF)
    l_ref[...] = jnp.zeros_like(l_ref)
    acc_ref[...] = jnp.zeros_like(acc_ref)
    lane = lax.broadcasted_iota(jnp.int32, (tq, LANES), 1)
    rep = tq // LANES

    def tile(kt, masked):
        ks = pl.ds(pl.multiple_of(kt * tq, tq), tq)
        if masked:
            mask = (lax.broadcasted_iota(jnp.int32, (tq, tq), 1)
                    <= lax.broadcasted_iota(jnp.int32, (tq, tq), 0))
        for pr in range(MLA_HEADS // 2):
            hs = (2 * pr, 2 * pr + 1)
            s = [_dot_t(q_ref[0, h], k_ref[0, h, ks, :]) for h in hs]
            if masked:
                s = [jnp.where(mask, x, NEG_INF) for x in s]
            m_prev = [m_ref[h] for h in hs]
            m_new = [jnp.maximum(mp, jnp.max(x, axis=-1, keepdims=True)) for mp, x in zip(m_prev, s)]
            alpha = [jnp.exp(mp - mn) for mp, mn in zip(m_prev, m_new)]
            p = [jnp.exp(x - jnp.tile(mn, (1, rep))) for x, mn in zip(s, m_new)]
            for h, a, pp, mn in zip(hs, alpha, p, m_new):
                l_ref[h] = a * l_ref[h] + jnp.sum(pp, axis=-1, keepdims=True)
                m_ref[h] = mn
            pv = _dot(p[0].astype(BF16), v_ref[0, hs[0], ks, :]) + _dot(p[1].astype(BF16), v_ref[0, hs[1], ks, :])
            acc_ref[pr] = acc_ref[pr] * jnp.where(lane < MLA_V, alpha[0], alpha[1]) + pv

    def body(kt, c):
        tile(kt, False)
        return c

    lax.fori_loop(0, qi, body, 0)
    tile(qi, True)
    for pr in range(MLA_HEADS // 2):
        linv = jnp.where(lane < MLA_V, 1.0 / l_ref[2 * pr], 1.0 / l_ref[2 * pr + 1])
        o_ref[0, :, pr * LANES:(pr + 1) * LANES] = (acc_ref[pr] * linv).astype(o_ref.dtype)


def _rope_tables(seq):
    pos = jnp.arange(seq, dtype=F32)
    inv = 1.0 / (ROPE_THETA ** (jnp.arange(0, MLA_ROPE, 2, dtype=F32) / MLA_ROPE))
    ang = pos[:, None] * inv[None, :]
    cos, sin = jnp.cos(ang), jnp.sin(ang)
    cos2 = jnp.concatenate([cos, cos], axis=-1)
    sin2 = jnp.concatenate([sin, sin], axis=-1)
    z64 = jnp.zeros((seq, MLA_NOPE), F32)
    z32 = jnp.zeros((seq, LANES - MLA_NOPE - MLA_ROPE), F32)
    c1 = jnp.concatenate([jnp.ones((seq, MLA_NOPE), F32), cos2, z32], axis=-1)
    c0 = jnp.concatenate([z64, cos2, z32], axis=-1)
    s0 = jnp.concatenate([z64, sin2, z32], axis=-1)
    return c1, c0, s0


def _rot_half_cols(w):
    half = MLA_ROPE // 2
    return jnp.concatenate([-w[..., half:], w[..., :half]], axis=-1)


def mla_mixer(proj3, q_norm, w_uq, kv_norm, w_ukv, tabs, *, tm, tq):
    bsz, seq, _ = proj3.shape
    H = MLA_HEADS
    scale = (MLA_NOPE + MLA_ROPE) ** -0.5
    wq = (w_uq.astype(F32) * scale).reshape(MLA_Q_RANK, H, MLA_NOPE + MLA_ROPE)
    zq = jnp.zeros((MLA_Q_RANK, H, LANES - MLA_NOPE - MLA_ROPE), F32)
    z64 = jnp.zeros((MLA_Q_RANK, H, MLA_NOPE), F32)
    wqa = jnp.concatenate([wq, zq], axis=-1).reshape(MLA_Q_RANK, H * LANES)
    wqb = jnp.concatenate([z64, _rot_half_cols(wq[..., MLA_NOPE:]), zq], axis=-1).reshape(MLA_Q_RANK, H * LANES)
    padq = ((0, 256 - MLA_Q_RANK), (0, 0))
    wqa = jnp.pad(wqa, padq).astype(BF16)
    wqb = jnp.pad(wqb, padq).astype(BF16)
    gq = jnp.pad(q_norm.astype(F32), (0, 256 - MLA_Q_RANK)).reshape(1, 256)
    wkv = w_ukv.astype(F32).reshape(MLA_KV_RANK, H, MLA_NOPE + MLA_V)
    zk = jnp.zeros((MLA_KV_RANK, H, MLA_NOPE), F32)
    wk = jnp.concatenate([wkv[..., :MLA_NOPE], zk], axis=-1).reshape(MLA_KV_RANK, H * LANES).astype(BF16)
    wv_h = wkv[..., MLA_NOPE:]
    even = (jnp.arange(H) % 2 == 0)[None, :, None]
    wv = jnp.concatenate([jnp.where(even, wv_h, 0.0), jnp.where(even, 0.0, wv_h)], axis=-1)
    wv = wv.reshape(MLA_KV_RANK, H * LANES).astype(BF16)
    c1, c0, s0 = tabs
    full2 = lambda shape: pl.BlockSpec(shape, lambda b, i: (0,) * len(shape))
    tab_spec = pl.BlockSpec((tm, LANES), lambda b, i: (i, 0))
    hd_spec = pl.BlockSpec((1, H, tm, LANES), lambda b, i: (b, 0, i, 0))
    hd_shape = jax.ShapeDtypeStruct((bsz, H, seq, LANES), BF16)
    q, k, v = pl.pallas_call(
        _mla_prep_kernel,
        out_shape=(hd_shape, hd_shape, hd_shape),
        grid=(bsz, seq // tm),
        in_specs=[pl.BlockSpec((1, tm, 256), lambda b, i: (b, i, C_CQ // 256)),
                  pl.BlockSpec((1, tm, 256), lambda b, i: (b, i, C_KR // 256)),
                  pl.BlockSpec((1, tm, 128), lambda b, i: (b, i, C_CKV // 128)),
                  full2((1, 256)), full2((1, 128)),
                  full2((256, H * LANES)), full2((256, H * LANES)),
                  full2((128, H * LANES)), full2((128, H * LANES)),
                  tab_spec, tab_spec, tab_spec],
        out_specs=(hd_spec, hd_spec, hd_spec),
        compiler_params=_cparams(("parallel", "parallel")),
        name="mla_prep",
    )(proj3, proj3, proj3, gq, kv_norm.astype(F32).reshape(1, 128), wqa, wqb, wk, wv, c1, c0, s0)

    return pl.pallas_call(
        functools.partial(_mla_flash_kernel, tq=tq),
        out_shape=jax.ShapeDtypeStruct((bsz, seq, H * MLA_V), BF16),
        grid=(bsz, seq // tq),
        in_specs=[pl.BlockSpec((1, H, tq, LANES), lambda b, i: (b, 0, i, 0)),
                  pl.BlockSpec((1, H, seq, LANES), lambda b, i: (b, 0, 0, 0)),
                  pl.BlockSpec((1, H, seq, LANES), lambda b, i: (b, 0, 0, 0))],
        out_specs=pl.BlockSpec((1, tq, H * MLA_V), lambda b, i: (b, i, 0)),
        scratch_shapes=[pltpu.VMEM((H, tq, LANES), F32), pltpu.VMEM((H, tq, LANES), F32),
                        pltpu.VMEM((H // 2, tq, LANES), F32)],
        compiler_params=_cparams(("parallel", "arbitrary")),
        name="mla_flash",
    )(q, k, v)


def _nsa_cmp_kernel(x_ref, pea_ref, peb_ref, w1a_ref, w1b_ref, w2_ref, o_ref):
    x = x_ref[0, 0]
    w1a, w1b = w1a_ref[0], w1b_ref[0]
    bias = _dot(pea_ref[0], w1a)[0:1] + _dot(peb_ref[0], w1b)[0:1]
    a = _dot(x, w1a)
    b = _dot(x, w1b)
    n = b.shape[0]
    pre = a + pltpu.roll(b, n - 1, 0) + bias
    o_ref[0, 0] = _dot(jax.nn.gelu(pre).astype(BF16), w2_ref[0]).astype(o_ref.dtype)


def nsa_compress(kv_cr, cmp_pe, cmp_w1, cmp_w2):
    bsz, _, nch, width = kv_cr.shape
    G, dh = NSA_KV_HEADS, NSA_DIM
    half = CMP_BLOCK // 2
    eye = jnp.eye(G, dtype=F32)
    w1r = cmp_w1.astype(F32).reshape(2, CMP_BLOCK, dh, dh)
    w1a = jnp.einsum('kpde,gh->kpgdhe', w1r[:, :half], eye).reshape(2, width, G * dh).astype(BF16)
    w1b = jnp.einsum('kpde,gh->kpgdhe', w1r[:, half:], eye).reshape(2, width, G * dh).astype(BF16)
    w2 = jnp.einsum('kde,gh->kgdhe', cmp_w2.astype(F32), eye).reshape(2, G * dh, G * dh).astype(BF16)
    pe = cmp_pe.astype(F32)
    pe_g = jnp.broadcast_to(pe[:, :, None, :], (2, CMP_BLOCK, G, dh))
    pea = jnp.broadcast_to(pe_g[:, :half].reshape(2, 1, width), (2, 8, width)).astype(BF16)
    peb = jnp.broadcast_to(pe_g[:, half:].reshape(2, 1, width), (2, 8, width)).astype(BF16)
    kvspec = lambda shape: pl.BlockSpec(shape, lambda b, k: (k,) + (0,) * (len(shape) - 1))
    return pl.pallas_call(
        _nsa_cmp_kernel,
        out_shape=jax.ShapeDtypeStruct((bsz, 2, nch, G * dh), BF16),
        grid=(bsz, 2),
        in_specs=[pl.BlockSpec((1, 1, nch, width), lambda b, k: (b, k, 0, 0)),
                  kvspec((1, 8, width)), kvspec((1, 8, width)),
                  kvspec((1, width, G * dh)), kvspec((1, width, G * dh)),
                  kvspec((1, G * dh, G * dh))],
        out_specs=pl.BlockSpec((1, 1, nch, G * dh), lambda b, k: (b, k, 0, 0)),
        compiler_params=_cparams(("parallel", "parallel")),
        name="nsa_compress",
    )(kv_cr, pea, peb, w1a, w1b, w2)


def _nsa_kernel(q_ref, gate_ref, ksl_ref, vsl_ref, kwn_ref, vwn_ref, kvc_ref, bc_ref, bw_ref,
                bs_ref, ov_ref, blk1h_ref, o_ref, kaug_ref, vaug_ref, m_ref, acc_ref, *, tq, tk, nbs, n_sel):
    qi = pl.program_id(1)
    R, G = NSA_REP, NSA_KV_HEADS
    H = R * G
    nsub = tk // LANES

    @pl.when(qi == 0)
    def _():
        kaug_ref[:, :LANES] = ksl_ref[0]
        kaug_ref[:, LANES:] = blk1h_ref[...]
        vaug_ref[:, :LANES] = vsl_ref[0]
        vaug_ref[:, LANES:] = jnp.ones((vaug_ref.shape[0], LANES), BF16)

    lane = lax.broadcasted_iota(jnp.int32, (tq, LANES), 1)
    t_row = qi * tq + lax.broadcasted_iota(jnp.int32, (H * tq, 1), 0) % tq
    gs = _sigmoid(gate_ref[0].astype(F32))
    kc = kvc_ref[0, 0]
    vc = kvc_ref[0, 1]
    ov = ov_ref[...]

    def stack(fn):
        return jnp.concatenate([fn(h) for h in range(H)], axis=0)

    q_all = stack(lambda h: q_ref[0, :, h * LANES:(h + 1) * LANES])

    c = jnp.minimum(qi, 2)
    ws = pl.ds(pl.multiple_of(jnp.maximum(qi - 2, 0) * tq, tq), 3 * tq)
    k_w = kwn_ref[0, ws, :]
    v_w = vwn_ref[0, ws, :]
    o_w = []
    for g in range(G):
        rows = slice(g * R * tq, (g + 1) * R * tq)
        s_w = _dot_t(q_all[rows], k_w) + jnp.concatenate([bw_ref[g * R + r, c] for r in range(R)], axis=0)
        p_w = jnp.exp(s_w - jnp.max(s_w, axis=-1, keepdims=True))
        o_w.append(_dot(p_w.astype(BF16), v_w) / jnp.sum(p_w, axis=-1, keepdims=True))
    o_w = jnp.concatenate(o_w, axis=0)

    valid = t_row >= (CMP_BLOCK - 1)
    s = _dot_t(q_all, kc) + stack(lambda h: bc_ref[h])
    m = jnp.max(s, axis=-1, keepdims=True)
    p = jnp.where(valid, jnp.exp(s - m), 0.0)
    l = jnp.where(valid, jnp.sum(p, axis=-1, keepdims=True), 1.0)
    pc = p / l
    o_c = _dot(pc.astype(BF16), vc)
    blk = lax.broadcasted_iota(jnp.int32, (nbs, tq), 0)
    tl = qi * tq + lax.broadcasted_iota(jnp.int32, (nbs, tq), 1)
    cur = tl // SEL_BLOCK
    forced = (blk == 0) | (blk == cur) | (blk == cur - 1)
    future = blk * SEL_BLOCK > tl
    qmask = []
    for g in range(G):
        b0 = g * R * tq
        psum = pc[b0:b0 + tq] + pc[b0 + tq:b0 + 2 * tq] + pc[b0 + 2 * tq:b0 + 3 * tq]
        p_hi = psum.astype(BF16)
        p_lo = (psum - p_hi.astype(F32)).astype(BF16)
        imp = (_dot_t(ov, p_hi) + _dot_t(ov, p_lo))[:nbs]
        imp = jnp.where(forced, FORCE, jnp.where(future, -FORCE, imp))
        rank = jnp.zeros((nbs, tq), F32)
        for i in range(nbs):
            ri = imp[i:i + 1, :]
            beats = (ri > imp) | ((ri == imp) & (blk > i))
            rank = rank + jnp.where(beats, 1.0, 0.0)
        sel = jnp.where(rank < n_sel, 0.0, NEG_INF)
        sel = jnp.concatenate([sel, jnp.zeros((LANES - nbs, tq), F32)], axis=0).T.astype(BF16)
        qmask += [sel] * R
    q_aug = jnp.concatenate([q_all, jnp.concatenate(qmask, axis=0)], axis=1)

    m_ref[...] = jnp.full_like(m_ref, NEG_INF)
    acc_ref[...] = jnp.zeros_like(acc_ref)

    halves = [slice(g * R * tq, (g + 1) * R * tq) for g in range(G)]

    def sel_tile(start, width, near):
        ks = pl.ds(pl.multiple_of(start, LANES), width * LANES)
        k_t = kaug_ref[ks, :]
        v_t = vaug_ref[ks, :]
        s = [_dot_t(q_aug[hs], k_t) for hs in halves]
        n_plain = width - len(near)
        for g in range(G if near else 0):
            cols = [s[g][:, :n_plain * LANES]] if n_plain else []
            for n_i, tab in enumerate(near):
                c0 = (n_plain + n_i) * LANES
                cols.append(s[g][:, c0:c0 + LANES]
                            + jnp.concatenate([bs_ref[g * R + r, tab] for r in range(R)], axis=0))
            s[g] = jnp.concatenate(cols, axis=1)
        m_prev = [m_ref[hs, :] for hs in halves]
        m_new = [jnp.maximum(mp, jnp.max(x, axis=-1, keepdims=True)) for mp, x in zip(m_prev, s)]
        alpha = [jnp.exp(mp - mn) for mp, mn in zip(m_prev, m_new)]
        p = [jnp.exp(x - jnp.tile(mn, (1, width))).astype(BF16) for x, mn in zip(s, m_new)]
        pv = [_dot(pp, v_t) for pp in p]
        for hs, mn, a, o in zip(halves, m_new, alpha, pv):
            m_ref[hs, :] = mn
            acc_ref[hs, :] = acc_ref[hs, :] * jnp.tile(a, (1, 2)) + o

    def far_body(kt, c):
        sel_tile(kt * tk, nsub, ())
        return c

    kd = qi // nsub
    delta = qi % nsub
    prev_near = (delta == 0) & (kd >= 1)
    lax.fori_loop(0, jnp.where(prev_near, kd - 1, kd), far_body, 0)

    @pl.when(prev_near)
    def _():
        sel_tile((kd - 1) * tk, nsub, (1,))

    for w in range(1, nsub + 1):
        @pl.when(delta == w - 1)
        def _():
            sel_tile(kd * tk, w, (1, 0) if w >= 2 else (0,))

    o_s = acc_ref[:, :LANES] / acc_ref[:, LANES:]

    for r in range(R):
        res = []
        for g in range(G):
            h = g * R + r
            rs = slice(h * tq, (h + 1) * tq)
            res.append(gs[:, h:h + 1] * o_c[rs] + gs[:, H + h:H + h + 1] * o_s[rs]
                       + gs[:, 2 * H + h:2 * H + h + 1] * o_w[rs])
        o_ref[0, :, r * LANES:(r + 1) * LANES] = jnp.where(lane < NSA_DIM, res[0], res[1]).astype(o_ref.dtype)


def _t5_bucket(dist):
    n = jnp.maximum(dist, 0)
    exact = REL_BUCKETS // 2
    nf = jnp.maximum(n, exact).astype(F32)
    large = exact + jnp.floor(jnp.log(nf / exact) / math.log(REL_MAX_DIST / exact)
                              * (REL_BUCKETS - exact)).astype(jnp.int32)
    return jnp.where(n < exact, n, jnp.minimum(large, REL_BUCKETS - 1))


def _nsa_bias_tables(rel_bias, seq, tq):
    rb = rel_bias.astype(F32).T
    far = rb[:, REL_BUCKETS - 1].reshape(NSA_HEADS, 1, 1)

    def by_dist(dist, ok, shift=0.0):
        bucket = _t5_bucket(dist)[None]
        out = jnp.zeros((NSA_HEADS,) + dist.shape, F32)
        for k in range(REL_BUCKETS):
            out = jnp.where(bucket == k, rb[:, k].reshape((NSA_HEADS,) + (1,) * dist.ndim), out)
        return jnp.where(ok[None], out - shift, NEG_INF)

    i = jnp.arange(tq)[:, None]
    t = jnp.arange(seq)[:, None]
    dist_c = t - (jnp.arange(LANES)[None, :] * CMP_STRIDE + CMP_BLOCK - 1)
    bc = by_dist(dist_c, dist_c >= 0)
    jw = jnp.arange(3 * tq)[None, :]
    bw = jnp.stack([by_dist(tq * c + i - jw, (tq * c + i - jw >= 0) & (tq * c + i - jw < WINDOW))
                    for c in range(3)], axis=1)
    js = jnp.arange(tq)[None, :]
    bs = jnp.stack([by_dist(tq * c + i - js, tq * c + i - js >= 0, far) for c in range(2)], axis=1)
    ci = jnp.arange(LANES)[:, None]
    sj = jnp.arange(LANES)[None, :]
    nbs = seq // SEL_BLOCK
    ov = ((ci * CMP_STRIDE <= sj * SEL_BLOCK + SEL_BLOCK - 1)
          & (ci * CMP_STRIDE + CMP_BLOCK - 1 >= sj * SEL_BLOCK)
          & (ci < seq // CMP_STRIDE - 1) & (sj < nbs))
    blk1h = (jnp.arange(seq)[:, None] // SEL_BLOCK == sj).astype(BF16)
    return bc, bw, bs, ov.T.astype(BF16), blk1h


def nsa_mixer(proj3, kvc, tables, *, tq, tk):
    bsz, seq, _ = proj3.shape
    assert tq == LANES and tq >= REL_MAX_DIST and tk % tq == 0 and seq % tk == 0
    assert seq // CMP_STRIDE == LANES and tq * 2 == WINDOW
    bc, bw, bs, ov, blk1h = tables
    nbs = seq // SEL_BLOCK
    H = NSA_HEADS
    slab = lambda j: pl.BlockSpec((1, seq, LANES), lambda b, i: (b, 0, C_NKV // LANES + j))
    const = lambda shape: pl.BlockSpec(shape, lambda b, i: (0,) * len(shape))
    return pl.pallas_call(
        functools.partial(_nsa_kernel, tq=tq, tk=tk, nbs=nbs, n_sel=min(SEL_TOPN, nbs)),
        out_shape=jax.ShapeDtypeStruct((bsz, seq, H * NSA_DIM), BF16),
        grid=(bsz, seq // tq),
        in_specs=[pl.BlockSpec((1, tq, H * LANES), lambda b, i: (b, i, C_NQ // (H * LANES))),
                  pl.BlockSpec((1, tq, LANES), lambda b, i: (b, i, C_GATE // LANES)),
                  slab(2), slab(3), slab(4), slab(5),
                  pl.BlockSpec((1, 2, LANES, LANES), lambda b, i: (b, 0, 0, 0)),
                  pl.BlockSpec((H, tq, LANES), lambda b, i: (0, i, 0)),
                  const((H, 3, tq, 3 * tq)), const((H, 2, tq, tq)),
                  const((LANES, LANES)), const((seq, LANES))],
        out_specs=pl.BlockSpec((1, tq, H * NSA_DIM), lambda b, i: (b, i, 0)),
        scratch_shapes=[pltpu.VMEM((seq, 2 * LANES), BF16), pltpu.VMEM((seq, 2 * LANES), BF16),
                        pltpu.VMEM((H * tq, LANES), F32), pltpu.VMEM((H * tq, 2 * LANES), F32)],
        compiler_params=_cparams(("parallel", "arbitrary")),
        name="nsa_attention",
    )(proj3, proj3, proj3, proj3, proj3, proj3, kvc, bc, bw, bs, ov, blk1h)


def _outproj_kernel(ys_ref, ym_ref, yn_ref, h_ref, g1_ref, g2_ref, g3_ref, w1_ref, w2_ref, w3_ref, o_ref):
    def part(y_ref, g_ref, w_ref):
        y = y_ref[...].astype(F32)
        return _dot(_rms(y, g_ref[...], y.shape[-1]).astype(BF16), w_ref[...])

    o_ref[...] = h_ref[...] + part(ys_ref, g1_ref, w1_ref) + part(ym_ref, g2_ref, w2_ref) + part(yn_ref, g3_ref, w3_ref)


def out_projection(y_ssm, y_mla, y_nsa, h, gains, weights, *, tm):
    m, d = h.shape
    row = lambda w: pl.BlockSpec((tm, w), lambda i: (i, 0))
    full = lambda a: pl.BlockSpec(a.shape, lambda i: (0, 0))
    gains = [g.reshape(1, -1).astype(F32) for g in gains]
    return pl.pallas_call(
        _outproj_kernel,
        out_shape=jax.ShapeDtypeStruct((m, d), F32),
        grid=(m // tm,),
        in_specs=[row(y_ssm.shape[1]), row(y_mla.shape[1]), row(y_nsa.shape[1]), row(d)]
                 + [full(g) for g in gains] + [full(w) for w in weights],
        out_specs=row(d),
        compiler_params=_cparams(("parallel",)),
        name="out_projection",
    )(y_ssm, y_mla, y_nsa, h, *gains, *weights)


def _xattn_kernel(h_ref, kv_ref, g_ref, wq_ref, wo_ref, o_ref, *, dh):
    h = h_ref[0]
    xn = _rms(h, g_ref[...], h.shape[-1]).astype(BF16)
    q = _dot(xn, wq_ref[...]).astype(BF16)
    hw = XATTN_HEADS * dh
    heads = range(XATTN_HEADS)
    s = [_dot_t(q[:, hd * dh:(hd + 1) * dh], kv_ref[0, :, hd * dh:(hd + 1) * dh]) for hd in heads]
    p = [jnp.exp(x - jnp.max(x, axis=-1, keepdims=True)) for x in s]
    p = [(x / jnp.sum(x, axis=-1, keepdims=True)).astype(BF16) for x in p]
    outs = [_dot(p[hd], kv_ref[0, :, hw + hd * dh:hw + (hd + 1) * dh]).astype(BF16) for hd in heads]
    o = jnp.concatenate(outs, axis=-1)
    o_ref[0] = h + _dot(o, wo_ref[...])


def cross_attention(h3, kv3, g_x, wq, wo, *, tm):
    bsz, seq, d = h3.shape
    m = kv3.shape[1]
    dh = d // XATTN_HEADS
    wq_s = (wq.astype(F32) * dh ** -0.5).astype(BF16)
    const = lambda shape: pl.BlockSpec(shape, lambda b, i: (0,) * len(shape))
    return pl.pallas_call(
        functools.partial(_xattn_kernel, dh=dh),
        out_shape=jax.ShapeDtypeStruct((bsz, seq, d), F32),
        grid=(bsz, seq // tm),
        in_specs=[pl.BlockSpec((1, tm, d), lambda b, i: (b, i, 0)),
                  pl.BlockSpec((1, m, 2 * d), lambda b, i: (b, 0, 0)),
                  const((1, d)), const((d, d)), const((d, d))],
        out_specs=pl.BlockSpec((1, tm, d), lambda b, i: (b, i, 0)),
        compiler_params=_cparams(("parallel", "parallel")),
        name="cross_attention",
    )(h3, kv3, g_x.reshape(1, d).astype(F32), wq_s, wo.astype(BF16))


def _ffn_kernel(h_ref, g_ref, wg_ref, wu_ref, wd_ref, o_ref, xn_ref, acc_ref):
    j = pl.program_id(1)

    @pl.when(j == 0)
    def _():
        h = h_ref[...]
        xn_ref[...] = _rms(h, g_ref[...], h.shape[-1]).astype(BF16)
        acc_ref[...] = h

    xn = xn_ref[...]
    a = _silu(_dot(xn, wg_ref[...])) * _dot(xn, wu_ref[...])
    acc_ref[...] += _dot(a.astype(BF16), wd_ref[...])

    @pl.when(j == pl.num_programs(1) - 1)
    def _():
        o_ref[...] = acc_ref[...]


def dense_ffn(h, g, wg, wu, wd, *, tm, tf):
    m, d = h.shape
    ff = wg.shape[1]
    return pl.pallas_call(
        _ffn_kernel,
        out_shape=jax.ShapeDtypeStruct((m, d), F32),
        grid=(m // tm, ff // tf),
        in_specs=[pl.BlockSpec((tm, d), lambda i, j: (i, 0)),
                  pl.BlockSpec((1, d), lambda i, j: (0, 0)),
                  pl.BlockSpec((d, tf), lambda i, j: (0, j)),
                  pl.BlockSpec((d, tf), lambda i, j: (0, j)),
                  pl.BlockSpec((tf, d), lambda i, j: (j, 0))],
        out_specs=pl.BlockSpec((tm, d), lambda i, j: (i, 0)),
        scratch_shapes=[pltpu.VMEM((tm, d), BF16), pltpu.VMEM((tm, d), F32)],
        compiler_params=_cparams(("parallel", "arbitrary")),
        name="dense_ffn",
    )(h, g.reshape(1, d).astype(F32), wg.astype(BF16), wu.astype(BF16), wd.astype(BF16))


def _router_kernel(h_ref, g_ref, wr_hi_ref, wr_lo_ref, xn_ref, info_ref, cnt_ref, carry_ref, *, tm):
    i = pl.program_id(0)

    @pl.when(i == 0)
    def _():
        carry_ref[...] = jnp.zeros_like(carry_ref)

    h = h_ref[...]
    xn = _rms(h, g_ref[...], h.shape[-1])
    xn_ref[...] = xn
    x_hi = xn.astype(BF16)
    x_lo = (xn - x_hi.astype(F32)).astype(BF16)
    logits = _dot(x_hi, wr_hi_ref[...]) + _dot(x_lo, wr_hi_ref[...]) + _dot(x_hi, wr_lo_ref[...])
    lane = lax.broadcasted_iota(jnp.int32, (tm, LANES), 1)
    lanef = lane.astype(F32)
    logits = jnp.where(lane < N_EXPERTS, logits, NEG_INF)
    m1 = jnp.max(logits, axis=-1, keepdims=True)
    i1 = jnp.min(jnp.where(logits == m1, lanef, float(LANES)), axis=-1, keepdims=True)
    rest = jnp.where(lanef == i1, NEG_INF, logits)
    m2 = jnp.max(rest, axis=-1, keepdims=True)
    i2 = jnp.min(jnp.where(rest == m2, lanef, float(LANES)), axis=-1, keepdims=True)
    e2 = jnp.exp(m2 - m1)
    w1 = 1.0 / (1.0 + e2)
    w2 = e2 / (1.0 + e2)
    oh1 = lanef == i1
    oh2 = lanef == i2
    oh = jnp.where(oh1 | oh2, 1.0, 0.0)
    rr = lax.broadcasted_iota(jnp.int32, (tm, tm), 0)
    cc = lax.broadcasted_iota(jnp.int32, (tm, tm), 1)
    tri = jnp.where(cc < rr, 1.0, 0.0).astype(BF16)
    before = _dot(tri, oh.astype(BF16)) + carry_ref[0:1, :]
    r1 = jnp.sum(jnp.where(oh1, before, 0.0), axis=-1, keepdims=True)
    r2 = jnp.sum(jnp.where(oh2, before, 0.0), axis=-1, keepdims=True)
    carry_ref[...] = carry_ref[...] + jnp.sum(oh, axis=0, keepdims=True)
    info = jnp.where(lane == 0, i1, jnp.where(lane == 1, i2, jnp.where(lane == 2, w1, jnp.where(
        lane == 3, w2, jnp.where(lane == 4, r1, jnp.where(lane == 5, r2, 0.0))))))
    info_ref[...] = info
    cnt_ref[...] = carry_ref[...]


def moe_router(h, g, router, *, tm):
    m, d = h.shape
    wr = jnp.pad(router.astype(F32), ((0, 0), (0, LANES - N_EXPERTS)))
    wr_hi = wr.astype(BF16)
    wr_lo = (wr - wr_hi.astype(F32)).astype(BF16)
    return pl.pallas_call(
        functools.partial(_router_kernel, tm=tm),
        out_shape=(jax.ShapeDtypeStruct((m, d), F32), jax.ShapeDtypeStruct((m, LANES), F32),
                   jax.ShapeDtypeStruct((8, LANES), F32)),
        grid=(m // tm,),
        in_specs=[pl.BlockSpec((tm, d), lambda i: (i, 0)),
                  pl.BlockSpec((1, d), lambda i: (0, 0)),
                  pl.BlockSpec((d, LANES), lambda i: (0, 0)),
                  pl.BlockSpec((d, LANES), lambda i: (0, 0))],
        out_specs=(pl.BlockSpec((tm, d), lambda i: (i, 0)),
                   pl.BlockSpec((tm, LANES), lambda i: (i, 0)),
                   pl.BlockSpec((8, LANES), lambda i: (0, 0))),
        scratch_shapes=[pltpu.VMEM((8, LANES), F32)],
        compiler_params=_cparams(("arbitrary",)),
        name="moe_router",
    )(h, g.reshape(1, d).astype(F32), wr_hi, wr_lo)


def _row_copy(src_hbm, row, dst, slot, sem):
    return pltpu.make_async_copy(src_hbm.at[pl.ds(row, 1), :], dst.at[pl.ds(slot, 1), :], sem)


def _rows_wait(src_hbm, dst, sem):
    pltpu.make_async_copy(src_hbm.at[pl.ds(0, dst.shape[0]), :], dst, sem).wait()


def _moe_ffn_kernel(src_ref, texp_ref, nact_ref, x_hbm, wg_ref, wu_ref, wd_ref, o_ref,
                    xbuf, xbf, acc_ref, sem, *, tm, nj):
    i = pl.program_id(0)
    j = pl.program_id(1)
    nact = nact_ref[0]
    active = i < nact
    has_next = i + 1 < nact
    cur = i % 2
    rows_per_step = tm // nj

    @pl.when((i == 0) & (j == 0))
    def _():
        def issue(s, c):
            _row_copy(x_hbm, src_ref[s], xbuf.at[0], s, sem.at[0]).start()
            return c

        lax.fori_loop(0, tm, issue, 0, unroll=8)

    @pl.when(active & (j == 0))
    def _():
        _rows_wait(x_hbm, xbuf.at[cur], sem.at[cur])
        xbf[...] = xbuf[cur].astype(BF16)
        acc_ref[...] = jnp.zeros_like(acc_ref)

    def compute(prefetch):
        if prefetch:
            base = (i + 1) * tm + j * rows_per_step
            for k in range(rows_per_step):
                _row_copy(x_hbm, src_ref[base + k], xbuf.at[1 - cur], j * rows_per_step + k,
                          sem.at[1 - cur]).start()
        x = xbf[...]
        a = _silu(_dot(x, wg_ref[0])) * _dot(x, wu_ref[0])
        acc_ref[...] += _dot(a.astype(BF16), wd_ref[0])

    @pl.when(active & has_next)
    def _():
        compute(True)

    @pl.when(active & jnp.logical_not(has_next))
    def _():
        compute(False)

    @pl.when(j == nj - 1)
    def _():
        o_ref[...] = jnp.where(active, acc_ref[...], 0.0)


def moe_expert_ffn(xn, src, tile_expert, n_active, wg, wu, wd, *, tm, tf):
    n_slots = src.shape[0]
    d = xn.shape[1]
    ne, _, ff = wg.shape
    nj = ff // tf
    assert nj * tf == ff and tm % nj == 0
    wg, wu, wd = wg.astype(BF16), wu.astype(BF16), wd.astype(BF16)

    def wmap_col(i, j, src, texp, nact):
        return (texp[i], 0, jnp.where(i < nact[0], j, nj - 1))

    def wmap_row(i, j, src, texp, nact):
        return (texp[i], jnp.where(i < nact[0], j, nj - 1), 0)

    return pl.pallas_call(
        functools.partial(_moe_ffn_kernel, tm=tm, nj=nj),
        out_shape=jax.ShapeDtypeStruct((n_slots, d), F32),
        grid_spec=pltpu.PrefetchScalarGridSpec(
            num_scalar_prefetch=3,
            grid=(n_slots // tm, nj),
            in_specs=[pl.BlockSpec(memory_space=pl.ANY),
                      pl.BlockSpec((1, d, tf), wmap_col),
                      pl.BlockSpec((1, d, tf), wmap_col),
                      pl.BlockSpec((1, tf, d), wmap_row)],
            out_specs=pl.BlockSpec((tm, d), lambda i, j, *_: (i, 0)),
            scratch_shapes=[pltpu.VMEM((2, tm, d), F32), pltpu.VMEM((tm, d), BF16),
                            pltpu.VMEM((tm, d), F32), pltpu.SemaphoreType.DMA((2,))]),
        compiler_params=_cparams(("arbitrary", "arbitrary")),
        name="moe_expert_ffn",
    )(src, tile_expert, n_active, xn, wg, wu, wd)


def _moe_combine_kernel(pos_ref, h_ref, info_ref, ys_hbm, g_ref, o_ref, buf, sem, *, tm, final_norm):
    i = pl.program_id(0)
    n = pl.num_programs(0)
    cur = i % 2

    def start_gather(tile, b):
        def issue(s, c):
            for k in range(2):
                _row_copy(ys_hbm, pos_ref[2 * (tile * tm + s) + k], buf.at[b, k], s, sem.at[b]).start()
            return c

        lax.fori_loop(0, tm, issue, 0, unroll=8)

    @pl.when(i == 0)
    def _():
        start_gather(0, 0)

    @pl.when(i + 1 < n)
    def _():
        start_gather(i + 1, 1 - cur)

    for k in range(2):
        _rows_wait(ys_hbm, buf.at[cur, k], sem.at[cur])
    info = info_ref[...]
    y = h_ref[...] + info[:, 2:3] * buf[cur, 0] + info[:, 3:4] * buf[cur, 1]
    if final_norm:
        y = _rms(y, g_ref[...], y.shape[-1])
    o_ref[...] = y


def moe_combine(h, info, ys, pos_flat, g_final, *, tm, final_norm):
    m, d = h.shape
    return pl.pallas_call(
        functools.partial(_moe_combine_kernel, tm=tm, final_norm=final_norm),
        out_shape=jax.ShapeDtypeStruct((m, d), F32),
        grid_spec=pltpu.PrefetchScalarGridSpec(
            num_scalar_prefetch=1,
            grid=(m // tm,),
            in_specs=[pl.BlockSpec((tm, d), lambda i, *_: (i, 0)),
                      pl.BlockSpec((tm, LANES), lambda i, *_: (i, 0)),
                      pl.BlockSpec(memory_space=pl.ANY),
                      pl.BlockSpec((1, d), lambda i, *_: (0, 0))],
            out_specs=pl.BlockSpec((tm, d), lambda i, *_: (i, 0)),
            scratch_shapes=[pltpu.VMEM((2, 2, tm, d), F32), pltpu.SemaphoreType.DMA((2,))]),
        compiler_params=_cparams(("arbitrary",)),
        name="moe_combine",
    )(pos_flat, h, info, ys, g_final.reshape(1, d).astype(F32))


def moe_layer(h, g, router, wg, wu, wd, g_final, *, final_norm, tm_r=512, tm_g=512, tf=1792, tm_c=256):
    m, d = h.shape
    xn, info, cnt = moe_router(h, g, router, tm=tm_r)
    e_idx = info[:, 0:2].astype(jnp.int32)
    rank = info[:, 4:6].astype(jnp.int32)
    counts = cnt[0, :N_EXPERTS].astype(jnp.int32)
    tiles_per = (counts + tm_g - 1) // tm_g
    tile_end = jnp.cumsum(tiles_per)
    seg_start = (tile_end - tiles_per) * tm_g
    pos = rank
    for e in range(N_EXPERTS):
        pos = pos + jnp.where(e_idx == e, seg_start[e], 0)
    n_tiles = (2 * m) // tm_g + N_EXPERTS
    n_slots = n_tiles * tm_g
    tok = jnp.broadcast_to(jnp.arange(m, dtype=jnp.int32)[:, None], (m, 2))
    src = jnp.zeros((n_slots,), jnp.int32).at[pos.reshape(-1)].set(tok.reshape(-1))
    n_active = tile_end[-1:].astype(jnp.int32)
    tile_ids = jnp.minimum(jnp.arange(n_tiles, dtype=jnp.int32), n_active[0] - 1)
    tile_expert = jnp.sum(tile_ids[:, None] >= tile_end[None, :], axis=1).astype(jnp.int32)
    ys = moe_expert_ffn(xn, src, tile_expert, n_active,
                        wg, wu, wd, tm=tm_g, tf=tf)
    return moe_combine(h, info, ys, pos.reshape(-1).astype(jnp.int32), g_final, tm=tm_c, final_norm=final_norm)


def _final_norm_kernel(h_ref, g_ref, o_ref):
    h = h_ref[...]
    o_ref[...] = _rms(h, g_ref[...], h.shape[-1])


def final_rmsnorm(h, g, *, tm):
    m, d = h.shape
    return pl.pallas_call(
        _final_norm_kernel,
        out_shape=jax.ShapeDtypeStruct((m, d), F32),
        grid=(m // tm,),
        in_specs=[pl.BlockSpec((tm, d), lambda i: (i, 0)), pl.BlockSpec((1, d), lambda i: (0, 0))],
        out_specs=pl.BlockSpec((tm, d), lambda i: (i, 0)),
        compiler_params=_cparams(("parallel",)),
        name="final_rmsnorm",
    )(h, g.reshape(1, d).astype(F32))


def _pack_w_in(w):
    d = w.shape[0]
    w = w.astype(F32)
    o = 0
    u = w[:, o:o + SSM_WIDTH]; o += SSM_WIDTH
    cq = w[:, o:o + MLA_Q_RANK]; o += MLA_Q_RANK
    ckv = w[:, o:o + MLA_KV_RANK]; o += MLA_KV_RANK
    kr = w[:, o:o + MLA_ROPE]; o += MLA_ROPE
    nq = w[:, o:o + NSA_HEADS * NSA_DIM]; o += NSA_HEADS * NSA_DIM
    nkv = w[:, o:o + 6 * NSA_KV_HEADS * NSA_DIM]; o += 6 * NSA_KV_HEADS * NSA_DIM
    gate = w[:, o:o + 3 * NSA_HEADS]
    z = lambda n: jnp.zeros((d, n), F32)
    kr_a = jnp.concatenate([z(MLA_NOPE), kr, z(LANES - MLA_NOPE - MLA_ROPE)], axis=1)
    kr_b = jnp.concatenate([z(MLA_NOPE), _rot_half_cols(kr), z(LANES - MLA_NOPE - MLA_ROPE)], axis=1)
    nq_h = (nq * NSA_DIM ** -0.5).reshape(d, NSA_KV_HEADS, NSA_REP, NSA_DIM)
    zq = jnp.zeros((d, NSA_REP, NSA_DIM), F32)
    nq_p = jnp.concatenate([
        jnp.concatenate([nq_h[:, 0], zq], axis=-1).reshape(d, NSA_REP * LANES),
        jnp.concatenate([zq, nq_h[:, 1]], axis=-1).reshape(d, NSA_REP * LANES)], axis=1)
    packed = jnp.concatenate([u, cq, z(256 - MLA_Q_RANK), kr_a, kr_b, nq_p, nkv, ckv,
                              gate, z(LANES - 3 * NSA_HEADS)], axis=1)
    assert packed.shape[1] == IN_COLS_PACKED
    return packed.astype(BF16)


def _rg_order(a):
    rest = a.shape[1:]
    return a.reshape((NSA_KV_HEADS, NSA_REP, NSA_DIM) + rest).swapaxes(0, 1).reshape((-1,) + rest)


def kernel(x, mem, w_in, w_out, mix_norm, out_norm, ssm_a_re, ssm_a_im, ssm_b_re, ssm_b_im, ssm_c_re, ssm_c_im, ssm_d, ssm_log_dt, ssm_w_glu, mla_q_norm, mla_w_uq, mla_kv_norm, mla_w_ukv, nsa_cmp_pe, nsa_cmp_w1, nsa_cmp_w2, rel_bias, xattn_norm, mem_norm, xattn_wq, xattn_wkv, xattn_wo, ffn_norm, dense_w_gate, dense_w_up, dense_w_down, moe_router, moe_w_gate, moe_w_up, moe_w_down, final_norm):
    bsz, seq, d = x.shape
    depth = w_in.shape[0]
    T = bsz * seq
    nmem = mem.shape[1]
    tq_nsa, tk_nsa = LANES, 4 * LANES
    rope_tabs = _rope_tables(seq)
    nsa_tabs = _nsa_bias_tables(rel_bias, seq, tq_nsa)
    o1, o2 = SSM_WIDTH, SSM_WIDTH + MLA_HEADS * MLA_V
    mem2 = mem.reshape(bsz * nmem, d)
    h = x.reshape(T, d)
    for l in range(depth):
        proj = norm_matmul(h, mix_norm[l], _pack_w_in(w_in[l]), tm=512, tn=IN_COLS_PACKED, out_dtype=BF16)
        proj3 = proj.reshape(bsz, seq, IN_COLS_PACKED)
        u_tm = proj3[:, :, C_U:C_U + SSM_WIDTH].transpose(1, 0, 2).reshape(seq * bsz, SSM_WIDTH)
        y_ssm = ssm_mixer(u_tm, ssm_a_re[l], ssm_a_im[l], ssm_b_re[l], ssm_b_im[l], ssm_c_re[l], ssm_c_im[l],
                          ssm_d[l], ssm_log_dt[l], ssm_w_glu[l], nb=bsz, tc=64)
        y_ssm = y_ssm.reshape(seq, bsz, SSM_WIDTH).transpose(1, 0, 2).reshape(T, SSM_WIDTH)
        y_mla = mla_mixer(proj3, mla_q_norm[l], mla_w_uq[l], mla_kv_norm[l], mla_w_ukv[l], rope_tabs,
                          tm=512, tq=512).reshape(T, -1)
        nch = seq // CMP_STRIDE
        kv_cr = jnp.stack([proj3[:, :, C_NKV:C_NKV + LANES].reshape(bsz, nch, CMP_STRIDE * LANES),
                           proj3[:, :, C_NKV + LANES:C_NKV + 2 * LANES].reshape(bsz, nch, CMP_STRIDE * LANES)],
                          axis=1)
        kvc = nsa_compress(kv_cr, nsa_cmp_pe[l], nsa_cmp_w1[l], nsa_cmp_w2[l])
        y_nsa = nsa_mixer(proj3, kvc, nsa_tabs, tq=tq_nsa, tk=tk_nsa).reshape(T, -1)
        g_out = out_norm[l]
        wo_l = w_out[l]
        h = out_projection(y_ssm, y_mla, y_nsa, h,
                           [g_out[:o1], g_out[o1:o2], _rg_order(g_out[o2:])],
                           [wo_l[:o1].astype(BF16), wo_l[o1:o2].astype(BF16), _rg_order(wo_l[o2:]).astype(BF16)],
                           tm=512)
        kv_mem = norm_matmul(mem2, mem_norm[l], xattn_wkv[l].astype(BF16), tm=256, tn=512, out_dtype=BF16)
        h = cross_attention(h.reshape(bsz, seq, d), kv_mem.reshape(bsz, nmem, 2 * d), xattn_norm[l],
                            xattn_wq[l], xattn_wo[l], tm=256).reshape(T, d)
        last = l == depth - 1
        if l % 2 == 0:
            h = dense_ffn(h, ffn_norm[l], dense_w_gate[l // 2], dense_w_up[l // 2], dense_w_down[l // 2],
                          tm=512, tf=1408)
            if last:
                h = final_rmsnorm(h, final_norm, tm=512)
        else:
            h = moe_layer(h, ffn_norm[l], moe_router[l // 2], moe_w_gate[l // 2], moe_w_up[l // 2],
                          moe_w_down[l // 2], final_norm, final_norm=last)
    return h.reshape(bsz, seq, d)
```

```python
import functools
import math

import jax
import jax.numpy as jnp
from jax import lax
from jax.experimental import pallas as pl
from jax.experimental.pallas import tpu as pltpu

F32 = jnp.float32
BF16 = jnp.bfloat16

HEAD_DIM = 64
SSM_WIDTH = 256
SSM_CH = 16
SSM_GROUPS = 16
SSM_STATE = 64
MLA_HEADS = 6
MLA_NOPE = 64
MLA_ROPE = 32
MLA_V = 64
MLA_Q_RANK = 192
MLA_KV_RANK = 128
NSA_HEADS = 6
NSA_KV_HEADS = 2
NSA_REP = 3
NSA_DIM = 64
CMP_BLOCK = 32
CMP_STRIDE = 16
SEL_BLOCK = 64
SEL_TOPN = 8
WINDOW = 256
REL_BUCKETS = 32
REL_MAX_DIST = 128
XATTN_HEADS = 4
N_EXPERTS = 8
ROPE_THETA = 10000.0
EPS = 1e-6
NEG_INF = -1e30
FORCE = 1e9

LANES = 128
VMEM_LIMIT = 56 * 1024 * 1024

C_U, C_CQ, C_KR, C_NQ, C_NKV, C_CKV, C_GATE = 0, 256, 512, 768, 1536, 2304, 2432
IN_COLS_PACKED = 2560


def _cparams(sem):
    return pltpu.CompilerParams(dimension_semantics=sem, vmem_limit_bytes=VMEM_LIMIT)


def _dot(a, b):
    return jnp.dot(a, b, preferred_element_type=F32)


def _dot_t(a, b):
    return lax.dot_general(a, b, (((1,), (1,)), ((), ())), preferred_element_type=F32)


def _rms(x, g, n):
    ms = jnp.sum(x * x, axis=-1, keepdims=True) * (1.0 / n)
    return x * lax.rsqrt(ms + EPS) * g


def _sigmoid(x):
    return 1.0 / (1.0 + jnp.exp(-x))


def _silu(x):
    return x * _sigmoid(x)


def _norm_mm_kernel(x_ref, g_ref, w_ref, o_ref, xn_ref):
    @pl.when(pl.program_id(1) == 0)
    def _():
        x = x_ref[...].astype(F32)
        xn_ref[...] = _rms(x, g_ref[...], x.shape[-1]).astype(BF16)

    o_ref[...] = _dot(xn_ref[...], w_ref[...]).astype(o_ref.dtype)


def norm_matmul(x, g, w, *, tm, tn, out_dtype):
    m, k = x.shape
    n = w.shape[1]
    return pl.pallas_call(
        _norm_mm_kernel,
        out_shape=jax.ShapeDtypeStruct((m, n), out_dtype),
        grid=(m // tm, n // tn),
        in_specs=[pl.BlockSpec((tm, k), lambda i, j: (i, 0)),
                  pl.BlockSpec((1, k), lambda i, j: (0, 0)),
                  pl.BlockSpec((k, tn), lambda i, j: (0, j))],
        out_specs=pl.BlockSpec((tm, tn), lambda i, j: (i, j)),
        scratch_shapes=[pltpu.VMEM((tm, k), BF16)],
        compiler_params=_cparams(("parallel", "arbitrary")),
        name="norm_matmul",
    )(x, g.reshape(1, k).astype(F32), w)


def _ssm_kernel(u_ref, bbr_ref, bbi_ref, ar_ref, ai_ref, ccr_ref, cci_ref, d_ref, wglu_ref,
                o_ref, hr_ref, hi_ref, cr_ref, ci_ref, *, tc, nb):
    @pl.when(pl.program_id(0) == 0)
    def _():
        cr_ref[...] = jnp.zeros_like(cr_ref)
        ci_ref[...] = jnp.zeros_like(ci_ref)

    u = u_ref[...]
    hr_ref[...] = _dot(u, bbr_ref[...])
    hi_ref[...] = _dot(u, bbi_ref[...])
    gp = ar_ref.shape[-1]
    ar = jnp.broadcast_to(ar_ref[...], (nb, gp))
    ai = jnp.broadcast_to(ai_ref[...], (nb, gp))

    def step(t, carry):
        hr, hi = carry
        rows = pl.ds(pl.multiple_of(t * nb, nb), nb)
        nr = ar * hr - ai * hi + hr_ref[rows, :]
        ni = ar * hi + ai * hr + hi_ref[rows, :]
        hr_ref[rows, :] = nr
        hi_ref[rows, :] = ni
        return nr, ni

    hr, hi = lax.fori_loop(0, tc, step, (cr_ref[...], ci_ref[...]))
    cr_ref[...] = hr
    ci_ref[...] = hi
    y = (_dot(hr_ref[...].astype(BF16), ccr_ref[...]) + _dot(hi_ref[...].astype(BF16), cci_ref[...])
         + d_ref[...] * u.astype(F32))
    y = jax.nn.gelu(y)
    z = _dot(y.astype(BF16), wglu_ref[...])
    o_ref[...] = (y * _sigmoid(z)).astype(o_ref.dtype)


def ssm_mixer(u_tm, a_re, a_im, b_re, b_im, c_re, c_im, d, log_dt, w_glu, *, nb, tc):
    rows = u_tm.shape[0]
    G, P, C = SSM_GROUPS, SSM_STATE, SSM_CH
    dt = jnp.exp(log_dt.astype(F32))[:, None]
    lr, li = a_re.astype(F32), a_im.astype(F32)
    mag = jnp.exp(lr * dt)
    ab_r, ab_i = mag * jnp.cos(li * dt), mag * jnp.sin(li * dt)
    den = lr * lr + li * li
    nr = ab_r - 1.0
    f_r = (nr * lr + ab_i * li) / den
    f_i = (ab_i * lr - nr * li) / den
    br, bi = b_re.astype(F32), b_im.astype(F32)
    bb_r = f_r[..., None] * br - f_i[..., None] * bi
    bb_i = f_r[..., None] * bi + f_i[..., None] * br
    eye = jnp.eye(G, dtype=F32)
    bbr = jnp.einsum('gpc,gh->gchp', bb_r, eye).reshape(G * C, G * P).astype(BF16)
    bbi = jnp.einsum('gpc,gh->gchp', bb_i, eye).reshape(G * C, G * P).astype(BF16)
    ccr = jnp.einsum('gcp,gh->gphc', c_re.astype(F32), eye).reshape(G * P, G * C).astype(BF16)
    cci = jnp.einsum('gcp,gh->gphc', -c_im.astype(F32), eye).reshape(G * P, G * C).astype(BF16)
    gp = G * P
    full = lambda shape: pl.BlockSpec(shape, lambda i: (0,) * len(shape))
    return pl.pallas_call(
        functools.partial(_ssm_kernel, tc=tc, nb=nb),
        out_shape=jax.ShapeDtypeStruct((rows, SSM_WIDTH), BF16),
        grid=(rows // (tc * nb),),
        in_specs=[pl.BlockSpec((tc * nb, SSM_WIDTH), lambda i: (i, 0)),
                  full((G * C, gp)), full((G * C, gp)), full((1, gp)), full((1, gp)),
                  full((gp, G * C)), full((gp, G * C)), full((1, SSM_WIDTH)),
                  full((SSM_WIDTH, SSM_WIDTH))],
        out_specs=pl.BlockSpec((tc * nb, SSM_WIDTH), lambda i: (i, 0)),
        scratch_shapes=[pltpu.VMEM((tc * nb, gp), F32), pltpu.VMEM((tc * nb, gp), F32),
                        pltpu.VMEM((nb, gp), F32), pltpu.VMEM((nb, gp), F32)],
        compiler_params=_cparams(("arbitrary",)),
        name="ssm_mixer",
    )(u_tm, bbr, bbi, ab_r.reshape(1, gp), ab_i.reshape(1, gp), ccr, cci,
      d.reshape(1, SSM_WIDTH).astype(F32), w_glu.astype(BF16))


def _mla_prep_kernel(cq_ref, kr_ref, ckv_ref, gq_ref, gkv_ref, wqa_ref, wqb_ref, wk_ref, wv_ref,
                     c1_ref, c0_ref, s0_ref, q_ref, k_ref, v_ref):
    qn = _rms(cq_ref[0].astype(F32), gq_ref[...], MLA_Q_RANK).astype(BF16)
    qa = _dot(qn, wqa_ref[...])
    qb = _dot(qn, wqb_ref[...])
    kn = _rms(ckv_ref[0].astype(F32), gkv_ref[...], MLA_KV_RANK).astype(BF16)
    ka = _dot(kn, wk_ref[...])
    va = _dot(kn, wv_ref[...])
    kr = kr_ref[0].astype(F32)
    c1, c0, s0 = c1_ref[...], c0_ref[...], s0_ref[...]
    krope = kr[:, :LANES] * c0 + kr[:, LANES:] * s0
    for h in range(MLA_HEADS):
        sl = slice(h * LANES, (h + 1) * LANES)
        q_ref[0, h] = (qa[:, sl] * c1 + qb[:, sl] * s0).astype(BF16)
        k_ref[0, h] = (ka[:, sl] + krope).astype(BF16)
        v_ref[0, h] = va[:, sl].astype(BF16)


def _mla_flash_kernel(q_ref, k_ref, v_ref, o_ref, m_ref, l_ref, acc_ref, *, tq):
    qi = pl.program_id(1)
    m_ref[...] = jnp.full_like(m_ref, NEG_INF)
    l_ref[...] = jnp.zeros_like(l_ref)
    acc_ref[...] = jnp.zeros_like(acc_ref)
    lane = lax.broadcasted_iota(jnp.int32, (tq, LANES), 1)
    rep = tq // LANES

    def tile(kt, masked):
        ks = pl.ds(pl.multiple_of(kt * tq, tq), tq)
        if masked:
            mask = (lax.broadcasted_iota(jnp.int32, (tq, tq), 1)
                    <= lax.broadcasted_iota(jnp.int32, (tq, tq), 0))
        for pr in range(MLA_HEADS // 2):
            hs = (2 * pr, 2 * pr + 1)
            s = [_dot_t(q_ref[0, h], k_ref[0, h, ks, :]) for h in hs]
            if masked:
                s = [jnp.where(mask, x, NEG_INF) for x in s]
            m_prev = [m_ref[h] for h in hs]
            m_new = [jnp.maximum(mp, jnp.max(x, axis=-1, keepdims=True)) for mp, x in zip(m_prev, s)]
            alpha = [jnp.exp(mp - mn) for mp, mn in zip(m_prev, m_new)]
            p = [jnp.exp(x - jnp.tile(mn, (1, rep))) for x, mn in zip(s, m_new)]
            for h, a, pp, mn in zip(hs, alpha, p, m_new):
                l_ref[h] = a * l_ref[h] + jnp.sum(pp, axis=-1, keepdims=True)
                m_ref[h] = mn
            pv = _dot(p[0].astype(BF16), v_ref[0, hs[0], ks, :]) + _dot(p[1].astype(BF16), v_ref[0, hs[1], ks, :])
            acc_ref[pr] = acc_ref[pr] * jnp.where(lane < MLA_V, alpha[0], alpha[1]) + pv

    def body(kt, c):
        tile(kt, False)
        return c

    lax.fori_loop(0, qi, body, 0)
    tile(qi, True)
    for pr in range(MLA_HEADS // 2):
        linv = jnp.where(lane < MLA_V, 1.0 / l_ref[2 * pr], 1.0 / l_ref[2 * pr + 1])
        o_ref[0, :, pr * LANES:(pr + 1) * LANES] = (acc_ref[pr] * linv).astype(o_ref.dtype)


def _rope_tables(seq):
    pos = jnp.arange(seq, dtype=F32)
    inv = 1.0 / (ROPE_THETA ** (jnp.arange(0, MLA_ROPE, 2, dtype=F32) / MLA_ROPE))
    ang = pos[:, None] * inv[None, :]
    cos, sin = jnp.cos(ang), jnp.sin(ang)
    cos2 = jnp.concatenate([cos, cos], axis=-1)
    sin2 = jnp.concatenate([sin, sin], axis=-1)
    z64 = jnp.zeros((seq, MLA_NOPE), F32)
    z32 = jnp.zeros((seq, LANES - MLA_NOPE - MLA_ROPE), F32)
    c1 = jnp.concatenate([jnp.ones((seq, MLA_NOPE), F32), cos2, z32], axis=-1)
    c0 = jnp.concatenate([z64, cos2, z32], axis=-1)
    s0 = jnp.concatenate([z64, sin2, z32], axis=-1)
    return c1, c0, s0


def _rot_half_cols(w):
    half = MLA_ROPE // 2
    return jnp.concatenate([-w[..., half:], w[..., :half]], axis=-1)


def mla_mixer(proj3, q_norm, w_uq, kv_norm, w_ukv, tabs, *, tm, tq):
    bsz, seq, _ = proj3.shape
    H = MLA_HEADS
    scale = (MLA_NOPE + MLA_ROPE) ** -0.5
    wq = (w_uq.astype(F32) * scale).reshape(MLA_Q_RANK, H, MLA_NOPE + MLA_ROPE)
    zq = jnp.zeros((MLA_Q_RANK, H, LANES - MLA_NOPE - MLA_ROPE), F32)
    z64 = jnp.zeros((MLA_Q_RANK, H, MLA_NOPE), F32)
    wqa = jnp.concatenate([wq, zq], axis=-1).reshape(MLA_Q_RANK, H * LANES)
    wqb = jnp.concatenate([z64, _rot_half_cols(wq[..., MLA_NOPE:]), zq], axis=-1).reshape(MLA_Q_RANK, H * LANES)
    padq = ((0, 256 - MLA_Q_RANK), (0, 0))
    wqa = jnp.pad(wqa, padq).astype(BF16)
    wqb = jnp.pad(wqb, padq).astype(BF16)
    gq = jnp.pad(q_norm.astype(F32), (0, 256 - MLA_Q_RANK)).reshape(1, 256)
    wkv = w_ukv.astype(F32).reshape(MLA_KV_RANK, H, MLA_NOPE + MLA_V)
    zk = jnp.zeros((MLA_KV_RANK, H, MLA_NOPE), F32)
    wk = jnp.concatenate([wkv[..., :MLA_NOPE], zk], axis=-1).reshape(MLA_KV_RANK, H * LANES).astype(BF16)
    wv_h = wkv[..., MLA_NOPE:]
    even = (jnp.arange(H) % 2 == 0)[None, :, None]
    wv = jnp.concatenate([jnp.where(even, wv_h, 0.0), jnp.where(even, 0.0, wv_h)], axis=-1)
    wv = wv.reshape(MLA_KV_RANK, H * LANES).astype(BF16)
    c1, c0, s0 = tabs
    full2 = lambda shape: pl.BlockSpec(shape, lambda b, i: (0,) * len(shape))
    tab_spec = pl.BlockSpec((tm, LANES), lambda b, i: (i, 0))
    hd_spec = pl.BlockSpec((1, H, tm, LANES), lambda b, i: (b, 0, i, 0))
    hd_shape = jax.ShapeDtypeStruct((bsz, H, seq, LANES), BF16)
    q, k, v = pl.pallas_call(
        _mla_prep_kernel,
        out_shape=(hd_shape, hd_shape, hd_shape),
        grid=(bsz, seq // tm),
        in_specs=[pl.BlockSpec((1, tm, 256), lambda b, i: (b, i, C_CQ // 256)),
                  pl.BlockSpec((1, tm, 256), lambda b, i: (b, i, C_KR // 256)),
                  pl.BlockSpec((1, tm, 128), lambda b, i: (b, i, C_CKV // 128)),
                  full2((1, 256)), full2((1, 128)),
                  full2((256, H * LANES)), full2((256, H * LANES)),
                  full2((128, H * LANES)), full2((128, H * LANES)),
                  tab_spec, tab_spec, tab_spec],
        out_specs=(hd_spec, hd_spec, hd_spec),
        compiler_params=_cparams(("parallel", "parallel")),
        name="mla_prep",
    )(proj3, proj3, proj3, gq, kv_norm.astype(F32).reshape(1, 128), wqa, wqb, wk, wv, c1, c0, s0)

    return pl.pallas_call(
        functools.partial(_mla_flash_kernel, tq=tq),
        out_shape=jax.ShapeDtypeStruct((bsz, seq, H * MLA_V), BF16),
        grid=(bsz, seq // tq),
        in_specs=[pl.BlockSpec((1, H, tq, LANES), lambda b, i: (b, 0, i, 0)),
                  pl.BlockSpec((1, H, seq, LANES), lambda b, i: (b, 0, 0, 0)),
                  pl.BlockSpec((1, H, seq, LANES), lambda b, i: (b, 0, 0, 0))],
        out_specs=pl.BlockSpec((1, tq, H * MLA_V), lambda b, i: (b, i, 0)),
        scratch_shapes=[pltpu.VMEM((H, tq, LANES), F32), pltpu.VMEM((H, tq, LANES), F32),
                        pltpu.VMEM((H // 2, tq, LANES), F32)],
        compiler_params=_cparams(("parallel", "arbitrary")),
        name="mla_flash",
    )(q, k, v)


def _nsa_cmp_kernel(x_ref, pea_ref, peb_ref, w1a_ref, w1b_ref, w2_ref, o_ref):
    x = x_ref[0, 0]
    w1a, w1b = w1a_ref[0], w1b_ref[0]
    bias = _dot(pea_ref[0], w1a)[0:1] + _dot(peb_ref[0], w1b)[0:1]
    a = _dot(x, w1a)
    b = _dot(x, w1b)
    n = b.shape[0]
    pre = a + pltpu.roll(b, n - 1, 0) + bias
    o_ref[0, 0] = _dot(jax.nn.gelu(pre).astype(BF16), w2_ref[0]).astype(o_ref.dtype)


def nsa_compress(kv_cr, cmp_pe, cmp_w1, cmp_w2):
    bsz, _, nch, width = kv_cr.shape
    G, dh = NSA_KV_HEADS, NSA_DIM
    half = CMP_BLOCK // 2
    eye = jnp.eye(G, dtype=F32)
    w1r = cmp_w1.astype(F32).reshape(2, CMP_BLOCK, dh, dh)
    w1a = jnp.einsum('kpde,gh->kpgdhe', w1r[:, :half], eye).reshape(2, width, G * dh).astype(BF16)
    w1b = jnp.einsum('kpde,gh->kpgdhe', w1r[:, half:], eye).reshape(2, width, G * dh).astype(BF16)
    w2 = jnp.einsum('kde,gh->kgdhe', cmp_w2.astype(F32), eye).reshape(2, G * dh, G * dh).astype(BF16)
    pe = cmp_pe.astype(F32)
    pe_g = jnp.broadcast_to(pe[:, :, None, :], (2, CMP_BLOCK, G, dh))
    pea = jnp.broadcast_to(pe_g[:, :half].reshape(2, 1, width), (2, 8, width)).astype(BF16)
    peb = jnp.broadcast_to(pe_g[:, half:].reshape(2, 1, width), (2, 8, width)).astype(BF16)
    kvspec = lambda shape: pl.BlockSpec(shape, lambda b, k: (k,) + (0,) * (len(shape) - 1))
    return pl.pallas_call(
        _nsa_cmp_kernel,
        out_shape=jax.ShapeDtypeStruct((bsz, 2, nch, G * dh), BF16),
        grid=(bsz, 2),
        in_specs=[pl.BlockSpec((1, 1, nch, width), lambda b, k: (b, k, 0, 0)),
                  kvspec((1, 8, width)), kvspec((1, 8, width)),
                  kvspec((1, width, G * dh)), kvspec((1, width, G * dh)),
                  kvspec((1, G * dh, G * dh))],
        out_specs=pl.BlockSpec((1, 1, nch, G * dh), lambda b, k: (b, k, 0, 0)),
        compiler_params=_cparams(("parallel", "parallel")),
        name="nsa_compress",
    )(kv_cr, pea, peb, w1a, w1b, w2)


def _nsa_kernel(q_ref, gate_ref, ksl_ref, vsl_ref, kwn_ref, vwn_ref, kvc_ref, bc_ref, bw_ref,
                bs_ref, ov_ref, blk1h_ref, o_ref, kaug_ref, vaug_ref, m_ref, acc_ref, *, tq, tk, nbs, n_sel):
    qi = pl.program_id(1)
    R, G = NSA_REP, NSA_KV_HEADS
    H = R * G
    nsub = tk // tq
    nwin = WINDOW // tq

    @pl.when(qi == 0)
    def _():
        kaug_ref[:, :LANES] = ksl_ref[0]
        kaug_ref[:, LANES:] = blk1h_ref[...]
        vaug_ref[:, :LANES] = vsl_ref[0]
        vaug_ref[:, LANES:] = jnp.ones((vaug_ref.shape[0], LANES), BF16)

    lane = lax.broadcasted_iota(jnp.int32, (tq, LANES), 1)
    t_row = qi * tq + lax.broadcasted_iota(jnp.int32, (H * tq, 1), 0) % tq
    gs = _sigmoid(gate_ref[0].astype(F32))
    kc = kvc_ref[0, 0]
    vc = kvc_ref[0, 1]
    ov = ov_ref[...]

    def stack(fn):
        return jnp.concatenate([fn(h) for h in range(H)], axis=0)

    q_all = stack(lambda h: q_ref[0, :, h * LANES:(h + 1) * LANES])

    c = jnp.minimum(qi, nwin)
    ws = pl.ds(pl.multiple_of(jnp.maximum(qi - nwin, 0) * tq, tq), (nwin + 1) * tq)
    k_w = kwn_ref[0, ws, :]
    v_w = vwn_ref[0, ws, :]
    o_w = []
    for g in range(G):
        rows = slice(g * R * tq, (g + 1) * R * tq)
        s_w = _dot_t(q_all[rows], k_w) + jnp.concatenate([bw_ref[g * R + r, c] for r in range(R)], axis=0)
        p_w = jnp.exp(s_w - jnp.max(s_w, axis=-1, keepdims=True))
        o_w.append(_dot(p_w.astype(BF16), v_w) / jnp.sum(p_w, axis=-1, keepdims=True))
    o_w = jnp.concatenate(o_w, axis=0)

    valid = t_row >= (CMP_BLOCK - 1)
    s = _dot_t(q_all, kc) + stack(lambda h: bc_ref[h])
    m = jnp.max(s, axis=-1, keepdims=True)
    p = jnp.where(valid, jnp.exp(s - m), 0.0)
    l = jnp.where(valid, jnp.sum(p, axis=-1, keepdims=True), 1.0)
    pc = p / l
    o_c = _dot(pc.astype(BF16), vc)
    blk = lax.broadcasted_iota(jnp.int32, (nbs, tq), 0)
    tl = qi * tq + lax.broadcasted_iota(jnp.int32, (nbs, tq), 1)
    cur = tl // SEL_BLOCK
    forced = (blk == 0) | (blk == cur) | (blk == cur - 1)
    future = blk * SEL_BLOCK > tl
    qmask = []
    for g in range(G):
        b0 = g * R * tq
        psum = pc[b0:b0 + tq] + pc[b0 + tq:b0 + 2 * tq] + pc[b0 + 2 * tq:b0 + 3 * tq]
        p_hi = psum.astype(BF16)
        p_lo = (psum - p_hi.astype(F32)).astype(BF16)
        imp = (_dot_t(ov, p_hi) + _dot_t(ov, p_lo))[:nbs]
        imp = jnp.where(forced, FORCE, jnp.where(future, -FORCE, imp))
        rank = jnp.zeros((nbs, tq), F32)
        for i in range(nbs):
            ri = imp[i:i + 1, :]
            beats = (ri > imp) | ((ri == imp) & (blk > i))
            rank = rank + jnp.where(beats, 1.0, 0.0)
        sel = jnp.where(rank < n_sel, 0.0, NEG_INF)
        sel = jnp.concatenate([sel, jnp.zeros((LANES - nbs, tq), F32)], axis=0).T.astype(BF16)
        qmask += [sel] * R
    q_aug = jnp.concatenate([q_all, jnp.concatenate(qmask, axis=0)], axis=1)

    m_ref[...] = jnp.full_like(m_ref, NEG_INF)
    acc_ref[...] = jnp.zeros_like(acc_ref)

    halves = [slice(g * R * tq, (g + 1) * R * tq) for g in range(G)]

    def sel_tile(kt, bias):
        ks = pl.ds(pl.multiple_of(kt * tk, tk), tk)
        k_t = kaug_ref[ks, :]
        v_t = vaug_ref[ks, :]
        s = [_dot_t(q_aug[hs], k_t) for hs in halves]
        if bias is not None:
            s = [x + bias[hs] for x, hs in zip(s, halves)]
        m_prev = [m_ref[hs, :] for hs in halves]
        m_new = [jnp.maximum(mp, jnp.max(x, axis=-1, keepdims=True)) for mp, x in zip(m_prev, s)]
        alpha = [jnp.exp(mp - mn) for mp, mn in zip(m_prev, m_new)]
        p = [jnp.exp(x - jnp.tile(mn, (1, tk // LANES))).astype(BF16) for x, mn in zip(s, m_new)]
        pv = [_dot(pp, v_t) for pp in p]
        for hs, mn, a, o in zip(halves, m_new, alpha, pv):
            m_ref[hs, :] = mn
            acc_ref[hs, :] = acc_ref[hs, :] * jnp.tile(a, (1, 2)) + o

    def near_bias(kt):
        cols = []
        for sub in range(nsub):
            d = qi - (kt * nsub + sub)
            cols.append(stack(lambda h: jnp.where(d == 0, bs_ref[h, 0], jnp.where(
                d == 1, bs_ref[h, 1], jnp.where(d < 0, NEG_INF, 0.0)))))
        return jnp.concatenate(cols, axis=1)

    def far_body(kt, c):
        sel_tile(kt, None)
        return c

    kd = (qi * tq) // tk
    lax.fori_loop(0, jnp.maximum(kd - 1, 0), far_body, 0)

    @pl.when(kd >= 1)
    def _():
        sel_tile(kd - 1, near_bias(kd - 1))

    sel_tile(kd, near_bias(kd))
    o_s = acc_ref[:, :LANES] / acc_ref[:, LANES:]

    for r in range(R):
        res = []
        for g in range(G):
            h = g * R + r
            rs = slice(h * tq, (h + 1) * tq)
            res.append(gs[:, h:h + 1] * o_c[rs] + gs[:, H + h:H + h + 1] * o_s[rs]
                       + gs[:, 2 * H + h:2 * H + h + 1] * o_w[rs])
        o_ref[0, :, r * LANES:(r + 1) * LANES] = jnp.where(lane < NSA_DIM, res[0], res[1]).astype(o_ref.dtype)


def _t5_bucket(dist):
    n = jnp.maximum(dist, 0)
    exact = REL_BUCKETS // 2
    nf = jnp.maximum(n, exact).astype(F32)
    large = exact + jnp.floor(jnp.log(nf / exact) / math.log(REL_MAX_DIST / exact)
                              * (REL_BUCKETS - exact)).astype(jnp.int32)
    return jnp.where(n < exact, n, jnp.minimum(large, REL_BUCKETS - 1))


def _nsa_bias_tables(rel_bias, seq, tq):
    rb = rel_bias.astype(F32).T
    far = rb[:, REL_BUCKETS - 1].reshape(NSA_HEADS, 1, 1)

    def by_dist(dist, ok, shift=0.0):
        bucket = _t5_bucket(dist)[None]
        out = jnp.zeros((NSA_HEADS,) + dist.shape, F32)
        for k in range(REL_BUCKETS):
            out = jnp.where(bucket == k, rb[:, k].reshape((NSA_HEADS,) + (1,) * dist.ndim), out)
        return jnp.where(ok[None], out - shift, NEG_INF)

    i = jnp.arange(tq)[:, None]
    t = jnp.arange(seq)[:, None]
    dist_c = t - (jnp.arange(LANES)[None, :] * CMP_STRIDE + CMP_BLOCK - 1)
    bc = by_dist(dist_c, dist_c >= 0)
    nwin = WINDOW // tq
    jw = jnp.arange((nwin + 1) * tq)[None, :]
    bw = jnp.stack([by_dist(tq * c + i - jw, (tq * c + i - jw >= 0) & (tq * c + i - jw < WINDOW))
                    for c in range(nwin + 1)], axis=1)
    js = jnp.arange(tq)[None, :]
    bs = jnp.stack([by_dist(tq * c + i - js, tq * c + i - js >= 0, far) for c in range(2)], axis=1)
    ci = jnp.arange(LANES)[:, None]
    sj = jnp.arange(LANES)[None, :]
    nbs = seq // SEL_BLOCK
    ov = ((ci * CMP_STRIDE <= sj * SEL_BLOCK + SEL_BLOCK - 1)
          & (ci * CMP_STRIDE + CMP_BLOCK - 1 >= sj * SEL_BLOCK)
          & (ci < seq // CMP_STRIDE - 1) & (sj < nbs))
    blk1h = (jnp.arange(seq)[:, None] // SEL_BLOCK == sj).astype(BF16)
    return bc, bw, bs, ov.T.astype(BF16), blk1h


def nsa_mixer(proj3, kvc, tables, *, tq, tk):
    bsz, seq, _ = proj3.shape
    assert tq % LANES == 0 and tq >= REL_MAX_DIST and tk % tq == 0 and seq % tk == 0
    assert seq // CMP_STRIDE == LANES and WINDOW % tq == 0
    nwin = WINDOW // tq
    bc, bw, bs, ov, blk1h = tables
    nbs = seq // SEL_BLOCK
    H = NSA_HEADS
    slab = lambda j: pl.BlockSpec((1, seq, LANES), lambda b, i: (b, 0, C_NKV // LANES + j))
    const = lambda shape: pl.BlockSpec(shape, lambda b, i: (0,) * len(shape))
    return pl.pallas_call(
        functools.partial(_nsa_kernel, tq=tq, tk=tk, nbs=nbs, n_sel=min(SEL_TOPN, nbs)),
        out_shape=jax.ShapeDtypeStruct((bsz, seq, H * NSA_DIM), BF16),
        grid=(bsz, seq // tq),
        in_specs=[pl.BlockSpec((1, tq, H * LANES), lambda b, i: (b, i, C_NQ // (H * LANES))),
                  pl.BlockSpec((1, tq, LANES), lambda b, i: (b, i, C_GATE // LANES)),
                  slab(2), slab(3), slab(4), slab(5),
                  pl.BlockSpec((1, 2, LANES, LANES), lambda b, i: (b, 0, 0, 0)),
                  pl.BlockSpec((H, tq, LANES), lambda b, i: (0, i, 0)),
                  const((H, nwin + 1, tq, (nwin + 1) * tq)), const((H, 2, tq, tq)),
                  const((LANES, LANES)), const((seq, LANES))],
        out_specs=pl.BlockSpec((1, tq, H * NSA_DIM), lambda b, i: (b, i, 0)),
        scratch_shapes=[pltpu.VMEM((seq, 2 * LANES), BF16), pltpu.VMEM((seq, 2 * LANES), BF16),
                        pltpu.VMEM((H * tq, LANES), F32), pltpu.VMEM((H * tq, 2 * LANES), F32)],
        compiler_params=_cparams(("parallel", "arbitrary")),
        name="nsa_attention",
    )(proj3, proj3, proj3, proj3, proj3, proj3, kvc, bc, bw, bs, ov, blk1h)


def _outproj_kernel(ys_ref, ym_ref, yn_ref, h_ref, g1_ref, g2_ref, g3_ref, w1_ref, w2_ref, w3_ref, o_ref):
    def part(y_ref, g_ref, w_ref):
        y = y_ref[...].astype(F32)
        return _dot(_rms(y, g_ref[...], y.shape[-1]).astype(BF16), w_ref[...])

    o_ref[...] = h_ref[...] + part(ys_ref, g1_ref, w1_ref) + part(ym_ref, g2_ref, w2_ref) + part(yn_ref, g3_ref, w3_ref)


def out_projection(y_ssm, y_mla, y_nsa, h, gains, weights, *, tm):
    m, d = h.shape
    row = lambda w: pl.BlockSpec((tm, w), lambda i: (i, 0))
    full = lambda a: pl.BlockSpec(a.shape, lambda i: (0, 0))
    gains = [g.reshape(1, -1).astype(F32) for g in gains]
    return pl.pallas_call(
        _outproj_kernel,
        out_shape=jax.ShapeDtypeStruct((m, d), F32),
        grid=(m // tm,),
        in_specs=[row(y_ssm.shape[1]), row(y_mla.shape[1]), row(y_nsa.shape[1]), row(d)]
                 + [full(g) for g in gains] + [full(w) for w in weights],
        out_specs=row(d),
        compiler_params=_cparams(("parallel",)),
        name="out_projection",
    )(y_ssm, y_mla, y_nsa, h, *gains, *weights)


def _xattn_kernel(h_ref, kv_ref, g_ref, wq_ref, wo_ref, o_ref, *, dh):
    h = h_ref[0]
    xn = _rms(h, g_ref[...], h.shape[-1]).astype(BF16)
    q = _dot(xn, wq_ref[...]).astype(BF16)
    hw = XATTN_HEADS * dh
    heads = range(XATTN_HEADS)
    s = [_dot_t(q[:, hd * dh:(hd + 1) * dh], kv_ref[0, :, hd * dh:(hd + 1) * dh]) for hd in heads]
    p = [jnp.exp(x - jnp.max(x, axis=-1, keepdims=True)) for x in s]
    p = [(x / jnp.sum(x, axis=-1, keepdims=True)).astype(BF16) for x in p]
    outs = [_dot(p[hd], kv_ref[0, :, hw + hd * dh:hw + (hd + 1) * dh]).astype(BF16) for hd in heads]
    o = jnp.concatenate(outs, axis=-1)
    o_ref[0] = h + _dot(o, wo_ref[...])


def cross_attention(h3, kv3, g_x, wq, wo, *, tm):
    bsz, seq, d = h3.shape
    m = kv3.shape[1]
    dh = d // XATTN_HEADS
    wq_s = (wq.astype(F32) * dh ** -0.5).astype(BF16)
    const = lambda shape: pl.BlockSpec(shape, lambda b, i: (0,) * len(shape))
    return pl.pallas_call(
        functools.partial(_xattn_kernel, dh=dh),
        out_shape=jax.ShapeDtypeStruct((bsz, seq, d), F32),
        grid=(bsz, seq // tm),
        in_specs=[pl.BlockSpec((1, tm, d), lambda b, i: (b, i, 0)),
                  pl.BlockSpec((1, m, 2 * d), lambda b, i: (b, 0, 0)),
                  const((1, d)), const((d, d)), const((d, d))],
        out_specs=pl.BlockSpec((1, tm, d), lambda b, i: (b, i, 0)),
        compiler_params=_cparams(("parallel", "parallel")),
        name="cross_attention",
    )(h3, kv3, g_x.reshape(1, d).astype(F32), wq_s, wo.astype(BF16))


def _ffn_kernel(h_ref, g_ref, wg_ref, wu_ref, wd_ref, o_ref, xn_ref, acc_ref):
    j = pl.program_id(1)

    @pl.when(j == 0)
    def _():
        h = h_ref[...]
        xn_ref[...] = _rms(h, g_ref[...], h.shape[-1]).astype(BF16)
        acc_ref[...] = h

    xn = xn_ref[...]
    a = _silu(_dot(xn, wg_ref[...])) * _dot(xn, wu_ref[...])
    acc_ref[...] += _dot(a.astype(BF16), wd_ref[...])

    @pl.when(j == pl.num_programs(1) - 1)
    def _():
        o_ref[...] = acc_ref[...]


def dense_ffn(h, g, wg, wu, wd, *, tm, tf):
    m, d = h.shape
    ff = wg.shape[1]
    return pl.pallas_call(
        _ffn_kernel,
        out_shape=jax.ShapeDtypeStruct((m, d), F32),
        grid=(m // tm, ff // tf),
        in_specs=[pl.BlockSpec((tm, d), lambda i, j: (i, 0)),
                  pl.BlockSpec((1, d), lambda i, j: (0, 0)),
                  pl.BlockSpec((d, tf), lambda i, j: (0, j)),
                  pl.BlockSpec((d, tf), lambda i, j: (0, j)),
                  pl.BlockSpec((tf, d), lambda i, j: (j, 0))],
        out_specs=pl.BlockSpec((tm, d), lambda i, j: (i, 0)),
        scratch_shapes=[pltpu.VMEM((tm, d), BF16), pltpu.VMEM((tm, d), F32)],
        compiler_params=_cparams(("parallel", "arbitrary")),
        name="dense_ffn",
    )(h, g.reshape(1, d).astype(F32), wg.astype(BF16), wu.astype(BF16), wd.astype(BF16))


def _router_kernel(h_ref, g_ref, wr_hi_ref, wr_lo_ref, xn_ref, info_ref, cnt_ref, carry_ref, *, tm):
    i = pl.program_id(0)

    @pl.when(i == 0)
    def _():
        carry_ref[...] = jnp.zeros_like(carry_ref)

    h = h_ref[...]
    xn = _rms(h, g_ref[...], h.shape[-1])
    xn_ref[...] = xn
    x_hi = xn.astype(BF16)
    x_lo = (xn - x_hi.astype(F32)).astype(BF16)
    logits = _dot(x_hi, wr_hi_ref[...]) + _dot(x_lo, wr_hi_ref[...]) + _dot(x_hi, wr_lo_ref[...])
    lane = lax.broadcasted_iota(jnp.int32, (tm, LANES), 1)
    lanef = lane.astype(F32)
    logits = jnp.where(lane < N_EXPERTS, logits, NEG_INF)
    m1 = jnp.max(logits, axis=-1, keepdims=True)
    i1 = jnp.min(jnp.where(logits == m1, lanef, float(LANES)), axis=-1, keepdims=True)
    rest = jnp.where(lanef == i1, NEG_INF, logits)
    m2 = jnp.max(rest, axis=-1, keepdims=True)
    i2 = jnp.min(jnp.where(rest == m2, lanef, float(LANES)), axis=-1, keepdims=True)
    e2 = jnp.exp(m2 - m1)
    w1 = 1.0 / (1.0 + e2)
    w2 = e2 / (1.0 + e2)
    oh1 = lanef == i1
    oh2 = lanef == i2
    oh = jnp.where(oh1 | oh2, 1.0, 0.0)
    rr = lax.broadcasted_iota(jnp.int32, (tm, tm), 0)
    cc = lax.broadcasted_iota(jnp.int32, (tm, tm), 1)
    tri = jnp.where(cc < rr, 1.0, 0.0).astype(BF16)
    before = _dot(tri, oh.astype(BF16)) + carry_ref[0:1, :]
    r1 = jnp.sum(jnp.where(oh1, before, 0.0), axis=-1, keepdims=True)
    r2 = jnp.sum(jnp.where(oh2, before, 0.0), axis=-1, keepdims=True)
    carry_ref[...] = carry_ref[...] + jnp.sum(oh, axis=0, keepdims=True)
    info = jnp.where(lane == 0, i1, jnp.where(lane == 1, i2, jnp.where(lane == 2, w1, jnp.where(
        lane == 3, w2, jnp.where(lane == 4, r1, jnp.where(lane == 5, r2, 0.0))))))
    info_ref[...] = info
    cnt_ref[...] = carry_ref[...]


def moe_router(h, g, router, *, tm):
    m, d = h.shape
    wr = jnp.pad(router.astype(F32), ((0, 0), (0, LANES - N_EXPERTS)))
    wr_hi = wr.astype(BF16)
    wr_lo = (wr - wr_hi.astype(F32)).astype(BF16)
    return pl.pallas_call(
        functools.partial(_router_kernel, tm=tm),
        out_shape=(jax.ShapeDtypeStruct((m, d), F32), jax.ShapeDtypeStruct((m, LANES), F32),
                   jax.ShapeDtypeStruct((8, LANES), F32)),
        grid=(m // tm,),
        in_specs=[pl.BlockSpec((tm, d), lambda i: (i, 0)),
                  pl.BlockSpec((1, d), lambda i: (0, 0)),
                  pl.BlockSpec((d, LANES), lambda i: (0, 0)),
                  pl.BlockSpec((d, LANES), lambda i: (0, 0))],
        out_specs=(pl.BlockSpec((tm, d), lambda i: (i, 0)),
                   pl.BlockSpec((tm, LANES), lambda i: (i, 0)),
                   pl.BlockSpec((8, LANES), lambda i: (0, 0))),
        scratch_shapes=[pltpu.VMEM((8, LANES), F32)],
        compiler_params=_cparams(("arbitrary",)),
        name="moe_router",
    )(h, g.reshape(1, d).astype(F32), wr_hi, wr_lo)


def _row_copy(src_hbm, row, dst, slot, sem):
    return pltpu.make_async_copy(src_hbm.at[pl.ds(row, 1), :], dst.at[pl.ds(slot, 1), :], sem)


def _rows_wait(src_hbm, dst, sem):
    pltpu.make_async_copy(src_hbm.at[pl.ds(0, dst.shape[0]), :], dst, sem).wait()


def _moe_ffn_kernel(src_ref, texp_ref, nact_ref, x_hbm, wg_ref, wu_ref, wd_ref, o_ref,
                    xbuf, xbf, acc_ref, sem, *, tm, nj):
    i = pl.program_id(0)
    j = pl.program_id(1)
    nact = nact_ref[0]
    active = i < nact
    has_next = i + 1 < nact
    cur = i % 2
    rows_per_step = tm // nj

    @pl.when((i == 0) & (j == 0))
    def _():
        def issue(s, c):
            _row_copy(x_hbm, src_ref[s], xbuf.at[0], s, sem.at[0]).start()
            return c

        lax.fori_loop(0, tm, issue, 0, unroll=8)

    @pl.when(active & (j == 0))
    def _():
        _rows_wait(x_hbm, xbuf.at[cur], sem.at[cur])
        xbf[...] = xbuf[cur].astype(BF16)
        acc_ref[...] = jnp.zeros_like(acc_ref)

    def compute(prefetch):
        if prefetch:
            base = (i + 1) * tm + j * rows_per_step
            for k in range(rows_per_step):
                _row_copy(x_hbm, src_ref[base + k], xbuf.at[1 - cur], j * rows_per_step + k,
                          sem.at[1 - cur]).start()
        x = xbf[...]
        a = _silu(_dot(x, wg_ref[0])) * _dot(x, wu_ref[0])
        acc_ref[...] += _dot(a.astype(BF16), wd_ref[0])

    @pl.when(active & has_next)
    def _():
        compute(True)

    @pl.when(active & jnp.logical_not(has_next))
    def _():
        compute(False)

    @pl.when(j == nj - 1)
    def _():
        o_ref[...] = jnp.where(active, acc_ref[...], 0.0)


def moe_expert_ffn(xn, src, tile_expert, n_active, wg, wu, wd, *, tm, tf):
    n_slots = src.shape[0]
    d = xn.shape[1]
    ne, _, ff = wg.shape
    nj = ff // tf
    assert nj * tf == ff and tm % nj == 0
    wg, wu, wd = wg.astype(BF16), wu.astype(BF16), wd.astype(BF16)

    def wmap_col(i, j, src, texp, nact):
        return (texp[i], 0, jnp.where(i < nact[0], j, nj - 1))

    def wmap_row(i, j, src, texp, nact):
        return (texp[i], jnp.where(i < nact[0], j, nj - 1), 0)

    return pl.pallas_call(
        functools.partial(_moe_ffn_kernel, tm=tm, nj=nj),
        out_shape=jax.ShapeDtypeStruct((n_slots, d), F32),
        grid_spec=pltpu.PrefetchScalarGridSpec(
            num_scalar_prefetch=3,
            grid=(n_slots // tm, nj),
            in_specs=[pl.BlockSpec(memory_space=pl.ANY),
                      pl.BlockSpec((1, d, tf), wmap_col),
                      pl.BlockSpec((1, d, tf), wmap_col),
                      pl.BlockSpec((1, tf, d), wmap_row)],
            out_specs=pl.BlockSpec((tm, d), lambda i, j, *_: (i, 0)),
            scratch_shapes=[pltpu.VMEM((2, tm, d), F32), pltpu.VMEM((tm, d), BF16),
                            pltpu.VMEM((tm, d), F32), pltpu.SemaphoreType.DMA((2,))]),
        compiler_params=_cparams(("arbitrary", "arbitrary")),
        name="moe_expert_ffn",
    )(src, tile_expert, n_active, xn, wg, wu, wd)


def _moe_combine_kernel(pos_ref, h_ref, info_ref, ys_hbm, g_ref, o_ref, buf, sem, *, tm, final_norm):
    i = pl.program_id(0)
    n = pl.num_programs(0)
    cur = i % 2

    def start_gather(tile, b):
        def issue(s, c):
            for k in range(2):
                _row_copy(ys_hbm, pos_ref[2 * (tile * tm + s) + k], buf.at[b, k], s, sem.at[b]).start()
            return c

        lax.fori_loop(0, tm, issue, 0, unroll=8)

    @pl.when(i == 0)
    def _():
        start_gather(0, 0)

    @pl.when(i + 1 < n)
    def _():
        start_gather(i + 1, 1 - cur)

    for k in range(2):
        _rows_wait(ys_hbm, buf.at[cur, k], sem.at[cur])
    info = info_ref[...]
    y = h_ref[...] + info[:, 2:3] * buf[cur, 0] + info[:, 3:4] * buf[cur, 1]
    if final_norm:
        y = _rms(y, g_ref[...], y.shape[-1])
    o_ref[...] = y


def moe_combine(h, info, ys, pos_flat, g_final, *, tm, final_norm):
    m, d = h.shape
    return pl.pallas_call(
        functools.partial(_moe_combine_kernel, tm=tm, final_norm=final_norm),
        out_shape=jax.ShapeDtypeStruct((m, d), F32),
        grid_spec=pltpu.PrefetchScalarGridSpec(
            num_scalar_prefetch=1,
            grid=(m // tm,),
            in_specs=[pl.BlockSpec((tm, d), lambda i, *_: (i, 0)),
                      pl.BlockSpec((tm, LANES), lambda i, *_: (i, 0)),
                      pl.BlockSpec(memory_space=pl.ANY),
                      pl.BlockSpec((1, d), lambda i, *_: (0, 0))],
            out_specs=pl.BlockSpec((tm, d), lambda i, *_: (i, 0)),
            scratch_shapes=[pltpu.VMEM((2, 2, tm, d), F32), pltpu.SemaphoreType.DMA((2,))]),
        compiler_params=_cparams(("arbitrary",)),
        name="moe_combine",
    )(pos_flat, h, info, ys, g_final.reshape(1, d).astype(F32))


def moe_layer(h, g, router, wg, wu, wd, g_final, *, final_norm, tm_r=512, tm_g=512, tf=1792, tm_c=256):
    m, d = h.shape
    xn, info, cnt = moe_router(h, g, router, tm=tm_r)
    e_idx = info[:, 0:2].astype(jnp.int32)
    rank = info[:, 4:6].astype(jnp.int32)
    counts = cnt[0, :N_EXPERTS].astype(jnp.int32)
    tiles_per = (counts + tm_g - 1) // tm_g
    tile_end = jnp.cumsum(tiles_per)
    seg_start = (tile_end - tiles_per) * tm_g
    pos = rank
    for e in range(N_EXPERTS):
        pos = pos + jnp.where(e_idx == e, seg_start[e], 0)
    n_tiles = (2 * m) // tm_g + N_EXPERTS
    n_slots = n_tiles * tm_g
    tok = jnp.broadcast_to(jnp.arange(m, dtype=jnp.int32)[:, None], (m, 2))
    src = jnp.zeros((n_slots,), jnp.int32).at[pos.reshape(-1)].set(tok.reshape(-1))
    n_active = tile_end[-1:].astype(jnp.int32)
    tile_ids = jnp.minimum(jnp.arange(n_tiles, dtype=jnp.int32), n_active[0] - 1)
    tile_expert = jnp.sum(tile_ids[:, None] >= tile_end[None, :], axis=1).astype(jnp.int32)
    ys = moe_expert_ffn(xn, src, tile_expert, n_active,
                        wg, wu, wd, tm=tm_g, tf=tf)
    return moe_combine(h, info, ys, pos.reshape(-1).astype(jnp.int32), g_final, tm=tm_c, final_norm=final_norm)


def _final_norm_kernel(h_ref, g_ref, o_ref):
    h = h_ref[...]
    o_ref[...] = _rms(h, g_ref[...], h.shape[-1])


def final_rmsnorm(h, g, *, tm):
    m, d = h.shape
    return pl.pallas_call(
        _final_norm_kernel,
        out_shape=jax.ShapeDtypeStruct((m, d), F32),
        grid=(m // tm,),
        in_specs=[pl.BlockSpec((tm, d), lambda i: (i, 0)), pl.BlockSpec((1, d), lambda i: (0, 0))],
        out_specs=pl.BlockSpec((tm, d), lambda i: (i, 0)),
        compiler_params=_cparams(("parallel",)),
        name="final_rmsnorm",
    )(h, g.reshape(1, d).astype(F32))


def _pack_w_in(w):
    d = w.shape[0]
    w = w.astype(F32)
    o = 0
    u = w[:, o:o + SSM_WIDTH]; o += SSM_WIDTH
    cq = w[:, o:o + MLA_Q_RANK]; o += MLA_Q_RANK
    ckv = w[:, o:o + MLA_KV_RANK]; o += MLA_KV_RANK
    kr = w[:, o:o + MLA_ROPE]; o += MLA_ROPE
    nq = w[:, o:o + NSA_HEADS * NSA_DIM]; o += NSA_HEADS * NSA_DIM
    nkv = w[:, o:o + 6 * NSA_KV_HEADS * NSA_DIM]; o += 6 * NSA_KV_HEADS * NSA_DIM
    gate = w[:, o:o + 3 * NSA_HEADS]
    z = lambda n: jnp.zeros((d, n), F32)
    kr_a = jnp.concatenate([z(MLA_NOPE), kr, z(LANES - MLA_NOPE - MLA_ROPE)], axis=1)
    kr_b = jnp.concatenate([z(MLA_NOPE), _rot_half_cols(kr), z(LANES - MLA_NOPE - MLA_ROPE)], axis=1)
    nq_h = (nq * NSA_DIM ** -0.5).reshape(d, NSA_KV_HEADS, NSA_REP, NSA_DIM)
    zq = jnp.zeros((d, NSA_REP, NSA_DIM), F32)
    nq_p = jnp.concatenate([
        jnp.concatenate([nq_h[:, 0], zq], axis=-1).reshape(d, NSA_REP * LANES),
        jnp.concatenate([zq, nq_h[:, 1]], axis=-1).reshape(d, NSA_REP * LANES)], axis=1)
    packed = jnp.concatenate([u, cq, z(256 - MLA_Q_RANK), kr_a, kr_b, nq_p, nkv, ckv,
                              gate, z(LANES - 3 * NSA_HEADS)], axis=1)
    assert packed.shape[1] == IN_COLS_PACKED
    return packed.astype(BF16)


def _rg_order(a):
    rest = a.shape[1:]
    return a.reshape((NSA_KV_HEADS, NSA_REP, NSA_DIM) + rest).swapaxes(0, 1).reshape((-1,) + rest)


def kernel(x, mem, w_in, w_out, mix_norm, out_norm, ssm_a_re, ssm_a_im, ssm_b_re, ssm_b_im, ssm_c_re, ssm_c_im, ssm_d, ssm_log_dt, ssm_w_glu, mla_q_norm, mla_w_uq, mla_kv_norm, mla_w_ukv, nsa_cmp_pe, nsa_cmp_w1, nsa_cmp_w2, rel_bias, xattn_norm, mem_norm, xattn_wq, xattn_wkv, xattn_wo, ffn_norm, dense_w_gate, dense_w_up, dense_w_down, moe_router, moe_w_gate, moe_w_up, moe_w_down, final_norm):
    bsz, seq, d = x.shape
    depth = w_in.shape[0]
    T = bsz * seq
    nmem = mem.shape[1]
    tq_nsa, tk_nsa = 2 * LANES, 4 * LANES
    rope_tabs = _rope_tables(seq)
    nsa_tabs = _nsa_bias_tables(rel_bias, seq, tq_nsa)
    o1, o2 = SSM_WIDTH, SSM_WIDTH + MLA_HEADS * MLA_V
    mem2 = mem.reshape(bsz * nmem, d)
    h = x.reshape(T, d)
    for l in range(depth):
        proj = norm_matmul(h, mix_norm[l], _pack_w_in(w_in[l]), tm=512, tn=IN_COLS_PACKED, out_dtype=BF16)
        proj3 = proj.reshape(bsz, seq, IN_COLS_PACKED)
        u_tm = proj3[:, :, C_U:C_U + SSM_WIDTH].transpose(1, 0, 2).reshape(seq * bsz, SSM_WIDTH)
        y_ssm = ssm_mixer(u_tm, ssm_a_re[l], ssm_a_im[l], ssm_b_re[l], ssm_b_im[l], ssm_c_re[l], ssm_c_im[l],
                          ssm_d[l], ssm_log_dt[l], ssm_w_glu[l], nb=bsz, tc=64)
        y_ssm = y_ssm.reshape(seq, bsz, SSM_WIDTH).transpose(1, 0, 2).reshape(T, SSM_WIDTH)
        y_mla = mla_mixer(proj3, mla_q_norm[l], mla_w_uq[l], mla_kv_norm[l], mla_w_ukv[l], rope_tabs,
                          tm=512, tq=512).reshape(T, -1)
        nch = seq // CMP_STRIDE
        kv_cr = jnp.stack([proj3[:, :, C_NKV:C_NKV + LANES].reshape(bsz, nch, CMP_STRIDE * LANES),
                           proj3[:, :, C_NKV + LANES:C_NKV + 2 * LANES].reshape(bsz, nch, CMP_STRIDE * LANES)],
                          axis=1)
        kvc = nsa_compress(kv_cr, nsa_cmp_pe[l], nsa_cmp_w1[l], nsa_cmp_w2[l])
        y_nsa = nsa_mixer(proj3, kvc, nsa_tabs, tq=tq_nsa, tk=tk_nsa).reshape(T, -1)
        g_out = out_norm[l]
        wo_l = w_out[l]
        h = out_projection(y_ssm, y_mla, y_nsa, h,
                           [g_out[:o1], g_out[o1:o2], _rg_order(g_out[o2:])],
                           [wo_l[:o1].astype(BF16), wo_l[o1:o2].astype(BF16), _rg_order(wo_l[o2:]).astype(BF16)],
                           tm=512)
        kv_mem = norm_matmul(mem2, mem_norm[l], xattn_wkv[l].astype(BF16), tm=256, tn=512, out_dtype=BF16)
        h = cross_attention(h.reshape(bsz, seq, d), kv_mem.reshape(bsz, nmem, 2 * d), xattn_norm[l],
                            xattn_wq[l], xattn_wo[l], tm=256).reshape(T, d)
        last = l == depth - 1
        if l % 2 == 0:
            h = dense_ffn(h, ffn_norm[l], dense_w_gate[l // 2], dense_w_up[l // 2], dense_w_down[l // 2],
                          tm=512, tf=1408)
            if last:
                h = final_rmsnorm(h, final_norm, tm=512)
        else:
            h = moe_layer(h, ffn_norm[l], moe_router[l // 2], moe_w_gate[l // 2], moe_w_up[l // 2],
                          moe_w_down[l // 2], final_norm, final_norm=last)
    return h.reshape(bsz, seq, d)
```

```python
import functools
import math

import jax
import jax.numpy as jnp
from jax import lax
from jax.experimental import pallas as pl
from jax.experimental.pallas import tpu as pltpu

F32 = jnp.float32
BF16 = jnp.bfloat16

HEAD_DIM = 64
SSM_WIDTH = 256
SSM_CH = 16
SSM_GROUPS = 16
SSM_STATE = 64
MLA_HEADS = 6
MLA_NOPE = 64
MLA_ROPE = 32
MLA_V = 64
MLA_Q_RANK = 192
MLA_KV_RANK = 128
NSA_HEADS = 6
NSA_KV_HEADS = 2
NSA_REP = 3
NSA_DIM = 64
CMP_BLOCK = 32
CMP_STRIDE = 16
SEL_BLOCK = 64
SEL_TOPN = 8
WINDOW = 256
REL_BUCKETS = 32
REL_MAX_DIST = 128
XATTN_HEADS = 4
N_EXPERTS = 8
ROPE_THETA = 10000.0
EPS = 1e-6
NEG_INF = -1e30
FORCE = 1e9

LANES = 128
VMEM_LIMIT = 56 * 1024 * 1024

C_U, C_CQ, C_KR, C_NQ, C_NKV, C_CKV, C_GATE = 0, 256, 512, 768, 1536, 2304, 2432
IN_COLS_PACKED = 2560


def _cparams(sem):
    return pltpu.CompilerParams(dimension_semantics=sem, vmem_limit_bytes=VMEM_LIMIT)


def _dot(a, b):
    return jnp.dot(a, b, preferred_element_type=F32)


def _dot_t(a, b):
    return lax.dot_general(a, b, (((1,), (1,)), ((), ())), preferred_element_type=F32)


def _rms(x, g, n):
    ms = jnp.sum(x * x, axis=-1, keepdims=True) * (1.0 / n)
    return x * lax.rsqrt(ms + EPS) * g


def _sigmoid(x):
    return 1.0 / (1.0 + jnp.exp(-x))


def _silu(x):
    return x * _sigmoid(x)


def _norm_mm_kernel(x_ref, g_ref, w_ref, o_ref, xn_ref):
    @pl.when(pl.program_id(1) == 0)
    def _():
        x = x_ref[...].astype(F32)
        xn_ref[...] = _rms(x, g_ref[...], x.shape[-1]).astype(BF16)

    o_ref[...] = _dot(xn_ref[...], w_ref[...]).astype(o_ref.dtype)


def norm_matmul(x, g, w, *, tm, tn, out_dtype):
    m, k = x.shape
    n = w.shape[1]
    return pl.pallas_call(
        _norm_mm_kernel,
        out_shape=jax.ShapeDtypeStruct((m, n), out_dtype),
        grid=(m // tm, n // tn),
        in_specs=[pl.BlockSpec((tm, k), lambda i, j: (i, 0)),
                  pl.BlockSpec((1, k), lambda i, j: (0, 0)),
                  pl.BlockSpec((k, tn), lambda i, j: (0, j))],
        out_specs=pl.BlockSpec((tm, tn), lambda i, j: (i, j)),
        scratch_shapes=[pltpu.VMEM((tm, k), BF16)],
        compiler_params=_cparams(("parallel", "arbitrary")),
        name="norm_matmul",
    )(x, g.reshape(1, k).astype(F32), w)


def _ssm_kernel(u_ref, bbr_ref, bbi_ref, ar_ref, ai_ref, ccr_ref, cci_ref, d_ref, wglu_ref,
                o_ref, hr_ref, hi_ref, cr_ref, ci_ref, *, tc, nb):
    @pl.when(pl.program_id(0) == 0)
    def _():
        cr_ref[...] = jnp.zeros_like(cr_ref)
        ci_ref[...] = jnp.zeros_like(ci_ref)

    u = u_ref[...]
    hr_ref[...] = _dot(u, bbr_ref[...])
    hi_ref[...] = _dot(u, bbi_ref[...])
    gp = ar_ref.shape[-1]
    ar = jnp.broadcast_to(ar_ref[...], (nb, gp))
    ai = jnp.broadcast_to(ai_ref[...], (nb, gp))

    def step(t, carry):
        hr, hi = carry
        rows = pl.ds(pl.multiple_of(t * nb, nb), nb)
        nr = ar * hr - ai * hi + hr_ref[rows, :]
        ni = ar * hi + ai * hr + hi_ref[rows, :]
        hr_ref[rows, :] = nr
        hi_ref[rows, :] = ni
        return nr, ni

    hr, hi = lax.fori_loop(0, tc, step, (cr_ref[...], ci_ref[...]))
    cr_ref[...] = hr
    ci_ref[...] = hi
    y = (_dot(hr_ref[...].astype(BF16), ccr_ref[...]) + _dot(hi_ref[...].astype(BF16), cci_ref[...])
         + d_ref[...] * u.astype(F32))
    y = jax.nn.gelu(y)
    z = _dot(y.astype(BF16), wglu_ref[...])
    o_ref[...] = (y * _sigmoid(z)).astype(o_ref.dtype)


def ssm_mixer(u_tm, a_re, a_im, b_re, b_im, c_re, c_im, d, log_dt, w_glu, *, nb, tc):
    rows = u_tm.shape[0]
    G, P, C = SSM_GROUPS, SSM_STATE, SSM_CH
    dt = jnp.exp(log_dt.astype(F32))[:, None]
    lr, li = a_re.astype(F32), a_im.astype(F32)
    mag = jnp.exp(lr * dt)
    ab_r, ab_i = mag * jnp.cos(li * dt), mag * jnp.sin(li * dt)
    den = lr * lr + li * li
    nr = ab_r - 1.0
    f_r = (nr * lr + ab_i * li) / den
    f_i = (ab_i * lr - nr * li) / den
    br, bi = b_re.astype(F32), b_im.astype(F32)
    bb_r = f_r[..., None] * br - f_i[..., None] * bi
    bb_i = f_r[..., None] * bi + f_i[..., None] * br
    eye = jnp.eye(G, dtype=F32)
    bbr = jnp.einsum('gpc,gh->gchp', bb_r, eye).reshape(G * C, G * P).astype(BF16)
    bbi = jnp.einsum('gpc,gh->gchp', bb_i, eye).reshape(G * C, G * P).astype(BF16)
    ccr = jnp.einsum('gcp,gh->gphc', c_re.astype(F32), eye).reshape(G * P, G * C).astype(BF16)
    cci = jnp.einsum('gcp,gh->gphc', -c_im.astype(F32), eye).reshape(G * P, G * C).astype(BF16)
    gp = G * P
    full = lambda shape: pl.BlockSpec(shape, lambda i: (0,) * len(shape))
    return pl.pallas_call(
        functools.partial(_ssm_kernel, tc=tc, nb=nb),
        out_shape=jax.ShapeDtypeStruct((rows, SSM_WIDTH), BF16),
        grid=(rows // (tc * nb),),
        in_specs=[pl.BlockSpec((tc * nb, SSM_WIDTH), lambda i: (i, 0)),
                  full((G * C, gp)), full((G * C, gp)), full((1, gp)), full((1, gp)),
                  full((gp, G * C)), full((gp, G * C)), full((1, SSM_WIDTH)),
                  full((SSM_WIDTH, SSM_WIDTH))],
        out_specs=pl.BlockSpec((tc * nb, SSM_WIDTH), lambda i: (i, 0)),
        scratch_shapes=[pltpu.VMEM((tc * nb, gp), F32), pltpu.VMEM((tc * nb, gp), F32),
                        pltpu.VMEM((nb, gp), F32), pltpu.VMEM((nb, gp), F32)],
        compiler_params=_cparams(("arbitrary",)),
        name="ssm_mixer",
    )(u_tm, bbr, bbi, ab_r.reshape(1, gp), ab_i.reshape(1, gp), ccr, cci,
      d.reshape(1, SSM_WIDTH).astype(F32), w_glu.astype(BF16))


def _mla_prep_kernel(cq_ref, kr_ref, ckv_ref, gq_ref, gkv_ref, wqa_ref, wqb_ref, wk_ref, wv_ref,
                     c1_ref, c0_ref, s0_ref, q_ref, k_ref, v_ref):
    qn = _rms(cq_ref[0].astype(F32), gq_ref[...], MLA_Q_RANK).astype(BF16)
    qa = _dot(qn, wqa_ref[...])
    qb = _dot(qn, wqb_ref[...])
    kn = _rms(ckv_ref[0].astype(F32), gkv_ref[...], MLA_KV_RANK).astype(BF16)
    ka = _dot(kn, wk_ref[...])
    va = _dot(kn, wv_ref[...])
    kr = kr_ref[0].astype(F32)
    c1, c0, s0 = c1_ref[...], c0_ref[...], s0_ref[...]
    krope = kr[:, :LANES] * c0 + kr[:, LANES:] * s0
    for h in range(MLA_HEADS):
        sl = slice(h * LANES, (h + 1) * LANES)
        q_ref[0, h] = (qa[:, sl] * c1 + qb[:, sl] * s0).astype(BF16)
        k_ref[0, h] = (ka[:, sl] + krope).astype(BF16)
        v_ref[0, h] = va[:, sl].astype(BF16)


def _mla_flash_kernel(q_ref, k_ref, v_ref, o_ref, m_ref, l_ref, acc_ref, *, tq):
    qi = pl.program_id(1)
    m_ref[...] = jnp.full_like(m_ref, NEG_INF)
    l_ref[...] = jnp.zeros_like(l_ref)
    acc_ref[...] = jnp.zeros_like(acc_ref)
    lane = lax.broadcasted_iota(jnp.int32, (tq, LANES), 1)
    rep = tq // LANES

    def tile(kt, masked):
        ks = pl.ds(pl.multiple_of(kt * tq, tq), tq)
        if masked:
            mask = (lax.broadcasted_iota(jnp.int32, (tq, tq), 1)
                    <= lax.broadcasted_iota(jnp.int32, (tq, tq), 0))
        for pr in range(MLA_HEADS // 2):
            hs = (2 * pr, 2 * pr + 1)
            s = [_dot_t(q_ref[0, h], k_ref[0, h, ks, :]) for h in hs]
            if masked:
                s = [jnp.where(mask, x, NEG_INF) for x in s]
            m_prev = [m_ref[h] for h in hs]
            m_new = [jnp.maximum(mp, jnp.max(x, axis=-1, keepdims=True)) for mp, x in zip(m_prev, s)]
            alpha = [jnp.exp(mp - mn) for mp, mn in zip(m_prev, m_new)]
            p = [jnp.exp(x - jnp.tile(mn, (1, rep))) for x, mn in zip(s, m_new)]
            for h, a, pp, mn in zip(hs, alpha, p, m_new):
                l_ref[h] = a * l_ref[h] + jnp.sum(pp, axis=-1, keepdims=True)
                m_ref[h] = mn
            pv = _dot(p[0].astype(BF16), v_ref[0, hs[0], ks, :]) + _dot(p[1].astype(BF16), v_ref[0, hs[1], ks, :])
            acc_ref[pr] = acc_ref[pr] * jnp.where(lane < MLA_V, alpha[0], alpha[1]) + pv

    def body(kt, c):
        tile(kt, False)
        return c

    lax.fori_loop(0, qi, body, 0)
    tile(qi, True)
    for pr in range(MLA_HEADS // 2):
        linv = jnp.where(lane < MLA_V, 1.0 / l_ref[2 * pr], 1.0 / l_ref[2 * pr + 1])
        o_ref[0, :, pr * LANES:(pr + 1) * LANES] = (acc_ref[pr] * linv).astype(o_ref.dtype)


def _rope_tables(seq):
    pos = jnp.arange(seq, dtype=F32)
    inv = 1.0 / (ROPE_THETA ** (jnp.arange(0, MLA_ROPE, 2, dtype=F32) / MLA_ROPE))
    ang = pos[:, None] * inv[None, :]
    cos, sin = jnp.cos(ang), jnp.sin(ang)
    cos2 = jnp.concatenate([cos, cos], axis=-1)
    sin2 = jnp.concatenate([sin, sin], axis=-1)
    z64 = jnp.zeros((seq, MLA_NOPE), F32)
    z32 = jnp.zeros((seq, LANES - MLA_NOPE - MLA_ROPE), F32)
    c1 = jnp.concatenate([jnp.ones((seq, MLA_NOPE), F32), cos2, z32], axis=-1)
    c0 = jnp.concatenate([z64, cos2, z32], axis=-1)
    s0 = jnp.concatenate([z64, sin2, z32], axis=-1)
    return c1, c0, s0


def _rot_half_cols(w):
    half = MLA_ROPE // 2
    return jnp.concatenate([-w[..., half:], w[..., :half]], axis=-1)


def mla_mixer(proj3, q_norm, w_uq, kv_norm, w_ukv, tabs, *, tm, tq):
    bsz, seq, _ = proj3.shape
    H = MLA_HEADS
    scale = (MLA_NOPE + MLA_ROPE) ** -0.5
    wq = (w_uq.astype(F32) * scale).reshape(MLA_Q_RANK, H, MLA_NOPE + MLA_ROPE)
    zq = jnp.zeros((MLA_Q_RANK, H, LANES - MLA_NOPE - MLA_ROPE), F32)
    z64 = jnp.zeros((MLA_Q_RANK, H, MLA_NOPE), F32)
    wqa = jnp.concatenate([wq, zq], axis=-1).reshape(MLA_Q_RANK, H * LANES)
    wqb = jnp.concatenate([z64, _rot_half_cols(wq[..., MLA_NOPE:]), zq], axis=-1).reshape(MLA_Q_RANK, H * LANES)
    padq = ((0, 256 - MLA_Q_RANK), (0, 0))
    wqa = jnp.pad(wqa, padq).astype(BF16)
    wqb = jnp.pad(wqb, padq).astype(BF16)
    gq = jnp.pad(q_norm.astype(F32), (0, 256 - MLA_Q_RANK)).reshape(1, 256)
    wkv = w_ukv.astype(F32).reshape(MLA_KV_RANK, H, MLA_NOPE + MLA_V)
    zk = jnp.zeros((MLA_KV_RANK, H, MLA_NOPE), F32)
    wk = jnp.concatenate([wkv[..., :MLA_NOPE], zk], axis=-1).reshape(MLA_KV_RANK, H * LANES).astype(BF16)
    wv_h = wkv[..., MLA_NOPE:]
    even = (jnp.arange(H) % 2 == 0)[None, :, None]
    wv = jnp.concatenate([jnp.where(even, wv_h, 0.0), jnp.where(even, 0.0, wv_h)], axis=-1)
    wv = wv.reshape(MLA_KV_RANK, H * LANES).astype(BF16)
    c1, c0, s0 = tabs
    full2 = lambda shape: pl.BlockSpec(shape, lambda b, i: (0,) * len(shape))
    tab_spec = pl.BlockSpec((tm, LANES), lambda b, i: (i, 0))
    hd_spec = pl.BlockSpec((1, H, tm, LANES), lambda b, i: (b, 0, i, 0))
    hd_shape = jax.ShapeDtypeStruct((bsz, H, seq, LANES), BF16)
    q, k, v = pl.pallas_call(
        _mla_prep_kernel,
        out_shape=(hd_shape, hd_shape, hd_shape),
        grid=(bsz, seq // tm),
        in_specs=[pl.BlockSpec((1, tm, 256), lambda b, i: (b, i, C_CQ // 256)),
                  pl.BlockSpec((1, tm, 256), lambda b, i: (b, i, C_KR // 256)),
                  pl.BlockSpec((1, tm, 128), lambda b, i: (b, i, C_CKV // 128)),
                  full2((1, 256)), full2((1, 128)),
                  full2((256, H * LANES)), full2((256, H * LANES)),
                  full2((128, H * LANES)), full2((128, H * LANES)),
                  tab_spec, tab_spec, tab_spec],
        out_specs=(hd_spec, hd_spec, hd_spec),
        compiler_params=_cparams(("parallel", "parallel")),
        name="mla_prep",
    )(proj3, proj3, proj3, gq, kv_norm.astype(F32).reshape(1, 128), wqa, wqb, wk, wv, c1, c0, s0)

    return pl.pallas_call(
        functools.partial(_mla_flash_kernel, tq=tq),
        out_shape=jax.ShapeDtypeStruct((bsz, seq, H * MLA_V), BF16),
        grid=(bsz, seq // tq),
        in_specs=[pl.BlockSpec((1, H, tq, LANES), lambda b, i: (b, 0, i, 0)),
                  pl.BlockSpec((1, H, seq, LANES), lambda b, i: (b, 0, 0, 0)),
                  pl.BlockSpec((1, H, seq, LANES), lambda b, i: (b, 0, 0, 0))],
        out_specs=pl.BlockSpec((1, tq, H * MLA_V), lambda b, i: (b, i, 0)),
        scratch_shapes=[pltpu.VMEM((H, tq, LANES), F32), pltpu.VMEM((H, tq, LANES), F32),
                        pltpu.VMEM((H // 2, tq, LANES), F32)],
        compiler_params=_cparams(("parallel", "arbitrary")),
        name="mla_flash",
    )(q, k, v)


def _nsa_cmp_kernel(x_ref, pea_ref, peb_ref, w1a_ref, w1b_ref, w2_ref, o_ref):
    x = x_ref[0, 0]
    w1a, w1b = w1a_ref[0], w1b_ref[0]
    bias = _dot(pea_ref[0], w1a)[0:1] + _dot(peb_ref[0], w1b)[0:1]
    a = _dot(x, w1a)
    b = _dot(x, w1b)
    n = b.shape[0]
    pre = a + pltpu.roll(b, n - 1, 0) + bias
    o_ref[0, 0] = _dot(jax.nn.gelu(pre).astype(BF16), w2_ref[0]).astype(o_ref.dtype)


def nsa_compress(kv_cr, cmp_pe, cmp_w1, cmp_w2):
    bsz, _, nch, width = kv_cr.shape
    G, dh = NSA_KV_HEADS, NSA_DIM
    half = CMP_BLOCK // 2
    eye = jnp.eye(G, dtype=F32)
    w1r = cmp_w1.astype(F32).reshape(2, CMP_BLOCK, dh, dh)
    w1a = jnp.einsum('kpde,gh->kpgdhe', w1r[:, :half], eye).reshape(2, width, G * dh).astype(BF16)
    w1b = jnp.einsum('kpde,gh->kpgdhe', w1r[:, half:], eye).reshape(2, width, G * dh).astype(BF16)
    w2 = jnp.einsum('kde,gh->kgdhe', cmp_w2.astype(F32), eye).reshape(2, G * dh, G * dh).astype(BF16)
    pe = cmp_pe.astype(F32)
    pe_g = jnp.broadcast_to(pe[:, :, None, :], (2, CMP_BLOCK, G, dh))
    pea = jnp.broadcast_to(pe_g[:, :half].reshape(2, 1, width), (2, 8, width)).astype(BF16)
    peb = jnp.broadcast_to(pe_g[:, half:].reshape(2, 1, width), (2, 8, width)).astype(BF16)
    kvspec = lambda shape: pl.BlockSpec(shape, lambda b, k: (k,) + (0,) * (len(shape) - 1))
    return pl.pallas_call(
        _nsa_cmp_kernel,
        out_shape=jax.ShapeDtypeStruct((bsz, 2, nch, G * dh), BF16),
        grid=(bsz, 2),
        in_specs=[pl.BlockSpec((1, 1, nch, width), lambda b, k: (b, k, 0, 0)),
                  kvspec((1, 8, width)), kvspec((1, 8, width)),
                  kvspec((1, width, G * dh)), kvspec((1, width, G * dh)),
                  kvspec((1, G * dh, G * dh))],
        out_specs=pl.BlockSpec((1, 1, nch, G * dh), lambda b, k: (b, k, 0, 0)),
        compiler_params=_cparams(("parallel", "parallel")),
        name="nsa_compress",
    )(kv_cr, pea, peb, w1a, w1b, w2)


def _nsa_kernel(q_ref, gate_ref, ksl_ref, vsl_ref, kwn_ref, vwn_ref, kvc_ref, bc_ref, bw_ref,
                bs_ref, ov_ref, blk1h_ref, o_ref, kaug_ref, vaug_ref, m_ref, acc_ref, *, tq, tk, nbs, n_sel):
    qi = pl.program_id(1)
    R, G = NSA_REP, NSA_KV_HEADS
    H = R * G
    nsub = tk // tq
    nwin = WINDOW // tq

    @pl.when(qi == 0)
    def _():
        kaug_ref[:, :LANES] = ksl_ref[0]
        kaug_ref[:, LANES:] = blk1h_ref[...]
        vaug_ref[:, :LANES] = vsl_ref[0]
        vaug_ref[:, LANES:] = jnp.ones((vaug_ref.shape[0], LANES), BF16)

    lane = lax.broadcasted_iota(jnp.int32, (tq, LANES), 1)
    t_row = qi * tq + lax.broadcasted_iota(jnp.int32, (H * tq, 1), 0) % tq
    gs = _sigmoid(gate_ref[0].astype(F32))
    kc = kvc_ref[0, 0]
    vc = kvc_ref[0, 1]
    ov = ov_ref[...]

    def stack(fn):
        return jnp.concatenate([fn(h) for h in range(H)], axis=0)

    q_all = stack(lambda h: q_ref[0, :, h * LANES:(h + 1) * LANES])

    c = jnp.minimum(qi, nwin)
    ws = pl.ds(pl.multiple_of(jnp.maximum(qi - nwin, 0) * tq, tq), (nwin + 1) * tq)
    k_w = kwn_ref[0, ws, :]
    v_w = vwn_ref[0, ws, :]
    o_w = []
    for g in range(G):
        rows = slice(g * R * tq, (g + 1) * R * tq)
        s_w = _dot_t(q_all[rows], k_w) + jnp.concatenate([bw_ref[g * R + r, c] for r in range(R)], axis=0)
        p_w = jnp.exp(s_w - jnp.max(s_w, axis=-1, keepdims=True))
        o_w.append(_dot(p_w.astype(BF16), v_w) / jnp.sum(p_w, axis=-1, keepdims=True))
    o_w = jnp.concatenate(o_w, axis=0)

    valid = t_row >= (CMP_BLOCK - 1)
    s = _dot_t(q_all, kc) + stack(lambda h: bc_ref[h])
    m = jnp.max(s, axis=-1, keepdims=True)
    p = jnp.where(valid, jnp.exp(s - m), 0.0)
    l = jnp.where(valid, jnp.sum(p, axis=-1, keepdims=True), 1.0)
    pc = p / l
    o_c = _dot(pc.astype(BF16), vc)
    blk = lax.broadcasted_iota(jnp.int32, (nbs, tq), 0)
    tl = qi * tq + lax.broadcasted_iota(jnp.int32, (nbs, tq), 1)
    cur = tl // SEL_BLOCK
    forced = (blk == 0) | (blk == cur) | (blk == cur - 1)
    future = blk * SEL_BLOCK > tl
    qmask = []
    for g in range(G):
        b0 = g * R * tq
        psum = pc[b0:b0 + tq] + pc[b0 + tq:b0 + 2 * tq] + pc[b0 + 2 * tq:b0 + 3 * tq]
        p_hi = psum.astype(BF16)
        p_lo = (psum - p_hi.astype(F32)).astype(BF16)
        imp = (_dot_t(ov, p_hi) + _dot_t(ov, p_lo))[:nbs]
        imp = jnp.where(forced, FORCE, jnp.where(future, -FORCE, imp))
        rank = jnp.zeros((nbs, tq), F32)
        for i in range(nbs):
            ri = imp[i:i + 1, :]
            beats = (ri > imp) | ((ri == imp) & (blk > i))
            rank = rank + jnp.where(beats, 1.0, 0.0)
        sel = jnp.where(rank < n_sel, 0.0, NEG_INF)
        sel = jnp.concatenate([sel, jnp.zeros((LANES - nbs, tq), F32)], axis=0).T.astype(BF16)
        qmask += [sel] * R
    q_aug = jnp.concatenate([q_all, jnp.concatenate(qmask, axis=0)], axis=1)

    m_ref[...] = jnp.full_like(m_ref, NEG_INF)
    acc_ref[...] = jnp.zeros_like(acc_ref)

    halves = [slice(g * R * tq, (g + 1) * R * tq) for g in range(G)]

    def sel_tile(kt, bias):
        ks = pl.ds(pl.multiple_of(kt * tk, tk), tk)
        k_t = kaug_ref[ks, :]
        v_t = vaug_ref[ks, :]
        s = [_dot_t(q_aug[hs], k_t) for hs in halves]
        if bias is not None:
            s = [x + bias[hs] for x, hs in zip(s, halves)]
        m_prev = [m_ref[hs, :] for hs in halves]
        m_new = [jnp.maximum(mp, jnp.max(x, axis=-1, keepdims=True)) for mp, x in zip(m_prev, s)]
        alpha = [jnp.exp(mp - mn) for mp, mn in zip(m_prev, m_new)]
        p = [jnp.exp(x - jnp.tile(mn, (1, tk // LANES))).astype(BF16) for x, mn in zip(s, m_new)]
        pv = [_dot(pp, v_t) for pp in p]
        for hs, mn, a, o in zip(halves, m_new, alpha, pv):
            m_ref[hs, :] = mn
            acc_ref[hs, :] = acc_ref[hs, :] * jnp.tile(a, (1, 2)) + o

    def near_bias(kt):
        cols = []
        for sub in range(nsub):
            d = qi - (kt * nsub + sub)
            cols.append(stack(lambda h: jnp.where(d == 0, bs_ref[h, 0], jnp.where(
                d == 1, bs_ref[h, 1], jnp.where(d < 0, NEG_INF, 0.0)))))
        return jnp.concatenate(cols, axis=1)

    def far_body(kt, c):
        sel_tile(kt, None)
        return c

    kd = (qi * tq) // tk
    lax.fori_loop(0, jnp.maximum(kd - 1, 0), far_body, 0)

    @pl.when(kd >= 1)
    def _():
        sel_tile(kd - 1, near_bias(kd - 1))

    sel_tile(kd, near_bias(kd))
    o_s = acc_ref[:, :LANES] / acc_ref[:, LANES:]

    for r in range(R):
        res = []
        for g in range(G):
            h = g * R + r
            rs = slice(h * tq, (h + 1) * tq)
            res.append(gs[:, h:h + 1] * o_c[rs] + gs[:, H + h:H + h + 1] * o_s[rs]
                       + gs[:, 2 * H + h:2 * H + h + 1] * o_w[rs])
        o_ref[0, :, r * LANES:(r + 1) * LANES] = jnp.where(lane < NSA_DIM, res[0], res[1]).astype(o_ref.dtype)


def _t5_bucket(dist):
    n = jnp.maximum(dist, 0)
    exact = REL_BUCKETS // 2
    nf = jnp.maximum(n, exact).astype(F32)
    large = exact + jnp.floor(jnp.log(nf / exact) / math.log(REL_MAX_DIST / exact)
                              * (REL_BUCKETS - exact)).astype(jnp.int32)
    return jnp.where(n < exact, n, jnp.minimum(large, REL_BUCKETS - 1))


def _nsa_bias_tables(rel_bias, seq, tq):
    rb = rel_bias.astype(F32).T
    far = rb[:, REL_BUCKETS - 1].reshape(NSA_HEADS, 1, 1)

    def by_dist(dist, ok, shift=0.0):
        bucket = _t5_bucket(dist)[None]
        out = jnp.zeros((NSA_HEADS,) + dist.shape, F32)
        for k in range(REL_BUCKETS):
            out = jnp.where(bucket == k, rb[:, k].reshape((NSA_HEADS,) + (1,) * dist.ndim), out)
        return jnp.where(ok[None], out - shift, NEG_INF)

    i = jnp.arange(tq)[:, None]
    t = jnp.arange(seq)[:, None]
    dist_c = t - (jnp.arange(LANES)[None, :] * CMP_STRIDE + CMP_BLOCK - 1)
    bc = by_dist(dist_c, dist_c >= 0)
    nwin = WINDOW // tq
    jw = jnp.arange((nwin + 1) * tq)[None, :]
    bw = jnp.stack([by_dist(tq * c + i - jw, (tq * c + i - jw >= 0) & (tq * c + i - jw < WINDOW))
                    for c in range(nwin + 1)], axis=1)
    js = jnp.arange(tq)[None, :]
    bs = jnp.stack([by_dist(tq * c + i - js, tq * c + i - js >= 0, far) for c in range(2)], axis=1)
    ci = jnp.arange(LANES)[:, None]
    sj = jnp.arange(LANES)[None, :]
    nbs = seq // SEL_BLOCK
    ov = ((ci * CMP_STRIDE <= sj * SEL_BLOCK + SEL_BLOCK - 1)
          & (ci * CMP_STRIDE + CMP_BLOCK - 1 >= sj * SEL_BLOCK)
          & (ci < seq // CMP_STRIDE - 1) & (sj < nbs))
    blk1h = (jnp.arange(seq)[:, None] // SEL_BLOCK == sj).astype(BF16)
    return bc, bw, bs, ov.T.astype(BF16), blk1h


def nsa_mixer(proj3, kvc, tables, *, tq, tk):
    bsz, seq, _ = proj3.shape
    assert tq % LANES == 0 and tq >= REL_MAX_DIST and tk % tq == 0 and seq % tk == 0
    assert seq // CMP_STRIDE == LANES and WINDOW % tq == 0
    nwin = WINDOW // tq
    bc, bw, bs, ov, blk1h = tables
    nbs = seq // SEL_BLOCK
    H = NSA_HEADS
    slab = lambda j: pl.BlockSpec((1, seq, LANES), lambda b, i: (b, 0, C_NKV // LANES + j))
    const = lambda shape: pl.BlockSpec(shape, lambda b, i: (0,) * len(shape))
    return pl.pallas_call(
        functools.partial(_nsa_kernel, tq=tq, tk=tk, nbs=nbs, n_sel=min(SEL_TOPN, nbs)),
        out_shape=jax.ShapeDtypeStruct((bsz, seq, H * NSA_DIM), BF16),
        grid=(bsz, seq // tq),
        in_specs=[pl.BlockSpec((1, tq, H * LANES), lambda b, i: (b, i, C_NQ // (H * LANES))),
                  pl.BlockSpec((1, tq, LANES), lambda b, i: (b, i, C_GATE // LANES)),
                  slab(2), slab(3), slab(4), slab(5),
                  pl.BlockSpec((1, 2, LANES, LANES), lambda b, i: (b, 0, 0, 0)),
                  pl.BlockSpec((H, tq, LANES), lambda b, i: (0, i, 0)),
                  const((H, nwin + 1, tq, (nwin + 1) * tq)), const((H, 2, tq, tq)),
                  const((LANES, LANES)), const((seq, LANES))],
        out_specs=pl.BlockSpec((1, tq, H * NSA_DIM), lambda b, i: (b, i, 0)),
        scratch_shapes=[pltpu.VMEM((seq, 2 * LANES), BF16), pltpu.VMEM((seq, 2 * LANES), BF16),
                        pltpu.VMEM((H * tq, LANES), F32), pltpu.VMEM((H * tq, 2 * LANES), F32)],
        compiler_params=_cparams(("parallel", "arbitrary")),
        name="nsa_attention",
    )(proj3, proj3, proj3, proj3, proj3, proj3, kvc, bc, bw, bs, ov, blk1h)


def _outproj_kernel(ys_ref, ym_ref, yn_ref, h_ref, g1_ref, g2_ref, g3_ref, w1_ref, w2_ref, w3_ref, o_ref):
    def part(y_ref, g_ref, w_ref):
        y = y_ref[...].astype(F32)
        return _dot(_rms(y, g_ref[...], y.shape[-1]).astype(BF16), w_ref[...])

    o_ref[...] = h_ref[...] + part(ys_ref, g1_ref, w1_ref) + part(ym_ref, g2_ref, w2_ref) + part(yn_ref, g3_ref, w3_ref)


def out_projection(y_ssm, y_mla, y_nsa, h, gains, weights, *, tm):
    m, d = h.shape
    row = lambda w: pl.BlockSpec((tm, w), lambda i: (i, 0))
    full = lambda a: pl.BlockSpec(a.shape, lambda i: (0, 0))
    gains = [g.reshape(1, -1).astype(F32) for g in gains]
    return pl.pallas_call(
        _outproj_kernel,
        out_shape=jax.ShapeDtypeStruct((m, d), F32),
        grid=(m // tm,),
        in_specs=[row(y_ssm.shape[1]), row(y_mla.shape[1]), row(y_nsa.shape[1]), row(d)]
                 + [full(g) for g in gains] + [full(w) for w in weights],
        out_specs=row(d),
        compiler_params=_cparams(("parallel",)),
        name="out_projection",
    )(y_ssm, y_mla, y_nsa, h, *gains, *weights)


def _xattn_kernel(h_ref, kv_ref, g_ref, wq_ref, wo_ref, o_ref, *, dh):
    h = h_ref[0]
    xn = _rms(h, g_ref[...], h.shape[-1]).astype(BF16)
    q = _dot(xn, wq_ref[...]).astype(BF16)
    hw = XATTN_HEADS * dh
    heads = range(XATTN_HEADS)
    s = [_dot_t(q[:, hd * dh:(hd + 1) * dh], kv_ref[0, :, hd * dh:(hd + 1) * dh]) for hd in heads]
    p = [jnp.exp(x - jnp.max(x, axis=-1, keepdims=True)) for x in s]
    p = [(x / jnp.sum(x, axis=-1, keepdims=True)).astype(BF16) for x in p]
    outs = [_dot(p[hd], kv_ref[0, :, hw + hd * dh:hw + (hd + 1) * dh]).astype(BF16) for hd in heads]
    o = jnp.concatenate(outs, axis=-1)
    o_ref[0] = h + _dot(o, wo_ref[...])


def cross_attention(h3, kv3, g_x, wq, wo, *, tm):
    bsz, seq, d = h3.shape
    m = kv3.shape[1]
    dh = d // XATTN_HEADS
    wq_s = (wq.astype(F32) * dh ** -0.5).astype(BF16)
    const = lambda shape: pl.BlockSpec(shape, lambda b, i: (0,) * len(shape))
    return pl.pallas_call(
        functools.partial(_xattn_kernel, dh=dh),
        out_shape=jax.ShapeDtypeStruct((bsz, seq, d), F32),
        grid=(bsz, seq // tm),
        in_specs=[pl.BlockSpec((1, tm, d), lambda b, i: (b, i, 0)),
                  pl.BlockSpec((1, m, 2 * d), lambda b, i: (b, 0, 0)),
                  const((1, d)), const((d, d)), const((d, d))],
        out_specs=pl.BlockSpec((1, tm, d), lambda b, i: (b, i, 0)),
        compiler_params=_cparams(("parallel", "parallel")),
        name="cross_attention",
    )(h3, kv3, g_x.reshape(1, d).astype(F32), wq_s, wo.astype(BF16))


def _ffn_kernel(h_ref, g_ref, wg_ref, wu_ref, wd_ref, o_ref, xn_ref, acc_ref):
    j = pl.program_id(1)

    @pl.when(j == 0)
    def _():
        h = h_ref[...]
        xn_ref[...] = _rms(h, g_ref[...], h.shape[-1]).astype(BF16)
        acc_ref[...] = h

    xn = xn_ref[...]
    a = _silu(_dot(xn, wg_ref[...])) * _dot(xn, wu_ref[...])
    acc_ref[...] += _dot(a.astype(BF16), wd_ref[...])

    @pl.when(j == pl.num_programs(1) - 1)
    def _():
        o_ref[...] = acc_ref[...]


def dense_ffn(h, g, wg, wu, wd, *, tm, tf):
    m, d = h.shape
    ff = wg.shape[1]
    return pl.pallas_call(
        _ffn_kernel,
        out_shape=jax.ShapeDtypeStruct((m, d), F32),
        grid=(m // tm, ff // tf),
        in_specs=[pl.BlockSpec((tm, d), lambda i, j: (i, 0)),
                  pl.BlockSpec((1, d), lambda i, j: (0, 0)),
                  pl.BlockSpec((d, tf), lambda i, j: (0, j)),
                  pl.BlockSpec((d, tf), lambda i, j: (0, j)),
                  pl.BlockSpec((tf, d), lambda i, j: (j, 0))],
        out_specs=pl.BlockSpec((tm, d), lambda i, j: (i, 0)),
        scratch_shapes=[pltpu.VMEM((tm, d), BF16), pltpu.VMEM((tm, d), F32)],
        compiler_params=_cparams(("parallel", "arbitrary")),
        name="dense_ffn",
    )(h, g.reshape(1, d).astype(F32), wg.astype(BF16), wu.astype(BF16), wd.astype(BF16))


def _router_kernel(h_ref, g_ref, wr_hi_ref, wr_lo_ref, xn_ref, info_ref, cnt_ref, carry_ref, *, tm):
    i = pl.program_id(0)

    @pl.when(i == 0)
    def _():
        carry_ref[...] = jnp.zeros_like(carry_ref)

    h = h_ref[...]
    xn = _rms(h, g_ref[...], h.shape[-1])
    xn_ref[...] = xn
    x_hi = xn.astype(BF16)
    x_lo = (xn - x_hi.astype(F32)).astype(BF16)
    logits = _dot(x_hi, wr_hi_ref[...]) + _dot(x_lo, wr_hi_ref[...]) + _dot(x_hi, wr_lo_ref[...])
    lane = lax.broadcasted_iota(jnp.int32, (tm, LANES), 1)
    lanef = lane.astype(F32)
    logits = jnp.where(lane < N_EXPERTS, logits, NEG_INF)
    m1 = jnp.max(logits, axis=-1, keepdims=True)
    i1 = jnp.min(jnp.where(logits == m1, lanef, float(LANES)), axis=-1, keepdims=True)
    rest = jnp.where(lanef == i1, NEG_INF, logits)
    m2 = jnp.max(rest, axis=-1, keepdims=True)
    i2 = jnp.min(jnp.where(rest == m2, lanef, float(LANES)), axis=-1, keepdims=True)
    e2 = jnp.exp(m2 - m1)
    w1 = 1.0 / (1.0 + e2)
    w2 = e2 / (1.0 + e2)
    oh1 = lanef == i1
    oh2 = lanef == i2
    oh = jnp.where(oh1 | oh2, 1.0, 0.0)
    rr = lax.broadcasted_iota(jnp.int32, (tm, tm), 0)
    cc = lax.broadcasted_iota(jnp.int32, (tm, tm), 1)
    tri = jnp.where(cc < rr, 1.0, 0.0).astype(BF16)
    before = _dot(tri, oh.astype(BF16)) + carry_ref[0:1, :]
    r1 = jnp.sum(jnp.where(oh1, before, 0.0), axis=-1, keepdims=True)
    r2 = jnp.sum(jnp.where(oh2, before, 0.0), axis=-1, keepdims=True)
    carry_ref[...] = carry_ref[...] + jnp.sum(oh, axis=0, keepdims=True)
    info = jnp.where(lane == 0, i1, jnp.where(lane == 1, i2, jnp.where(lane == 2, w1, jnp.where(
        lane == 3, w2, jnp.where(lane == 4, r1, jnp.where(lane == 5, r2, 0.0))))))
    info_ref[...] = info
    cnt_ref[...] = carry_ref[...]


def moe_router(h, g, router, *, tm):
    m, d = h.shape
    wr = jnp.pad(router.astype(F32), ((0, 0), (0, LANES - N_EXPERTS)))
    wr_hi = wr.astype(BF16)
    wr_lo = (wr - wr_hi.astype(F32)).astype(BF16)
    return pl.pallas_call(
        functools.partial(_router_kernel, tm=tm),
        out_shape=(jax.ShapeDtypeStruct((m, d), F32), jax.ShapeDtypeStruct((m, LANES), F32),
                   jax.ShapeDtypeStruct((8, LANES), F32)),
        grid=(m // tm,),
        in_specs=[pl.BlockSpec((tm, d), lambda i: (i, 0)),
                  pl.BlockSpec((1, d), lambda i: (0, 0)),
                  pl.BlockSpec((d, LANES), lambda i: (0, 0)),
                  pl.BlockSpec((d, LANES), lambda i: (0, 0))],
        out_specs=(pl.BlockSpec((tm, d), lambda i: (i, 0)),
                   pl.BlockSpec((tm, LANES), lambda i: (i, 0)),
                   pl.BlockSpec((8, LANES), lambda i: (0, 0))),
        scratch_shapes=[pltpu.VMEM((8, LANES), F32)],
        compiler_params=_cparams(("arbitrary",)),
        name="moe_router",
    )(h, g.reshape(1, d).astype(F32), wr_hi, wr_lo)


def _row_copy(src_hbm, row, dst, slot, sem):
    return pltpu.make_async_copy(src_hbm.at[pl.ds(row, 1), :], dst.at[pl.ds(slot, 1), :], sem)


def _rows_wait(src_hbm, dst, sem):
    pltpu.make_async_copy(src_hbm.at[pl.ds(0, dst.shape[0]), :], dst, sem).wait()


def _moe_ffn_kernel(src_ref, texp_ref, nact_ref, x_hbm, wg_ref, wu_ref, wd_ref, o_ref,
                    xbuf, xbf, acc_ref, sem, *, tm, nj):
    i = pl.program_id(0)
    j = pl.program_id(1)
    nact = nact_ref[0]
    active = i < nact
    nbuf = xbuf.shape[0]
    ahead = nbuf - 1
    cur = i % nbuf
    rows_per_step = tm // nj

    for t0 in range(ahead):
        @pl.when((i == 0) & (j == 0) & (t0 < nact))
        def _():
            def issue(s, c):
                _row_copy(x_hbm, src_ref[t0 * tm + s], xbuf.at[t0], s, sem.at[t0]).start()
                return c

            lax.fori_loop(0, tm, issue, 0, unroll=8)

    @pl.when(active & (j == 0))
    def _():
        _rows_wait(x_hbm, xbuf.at[cur], sem.at[cur])
        xbf[...] = xbuf[cur].astype(BF16)
        acc_ref[...] = jnp.zeros_like(acc_ref)

    @pl.when(active)
    def _():
        x = xbf[...]
        a = _silu(_dot(x, wg_ref[0])) * _dot(x, wu_ref[0])
        acc_ref[...] += _dot(a.astype(BF16), wd_ref[0])

    @pl.when(i + ahead < nact)
    def _():
        nxt = (i + ahead) % nbuf
        base = (i + ahead) * tm + j * rows_per_step
        for k in range(rows_per_step):
            _row_copy(x_hbm, src_ref[base + k], xbuf.at[nxt], j * rows_per_step + k, sem.at[nxt]).start()

    @pl.when(j == nj - 1)
    def _():
        o_ref[...] = jnp.where(active, acc_ref[...], 0.0)


def moe_expert_ffn(xn, src, tile_expert, n_active, wg, wu, wd, *, tm, tf):
    n_slots = src.shape[0]
    d = xn.shape[1]
    ne, _, ff = wg.shape
    nj = ff // tf
    assert nj * tf == ff and tm % nj == 0
    wg, wu, wd = wg.astype(BF16), wu.astype(BF16), wd.astype(BF16)

    def wmap_col(i, j, src, texp, nact):
        return (texp[i], 0, jnp.where(i < nact[0], j, nj - 1))

    def wmap_row(i, j, src, texp, nact):
        return (texp[i], jnp.where(i < nact[0], j, nj - 1), 0)

    return pl.pallas_call(
        functools.partial(_moe_ffn_kernel, tm=tm, nj=nj),
        out_shape=jax.ShapeDtypeStruct((n_slots, d), F32),
        grid_spec=pltpu.PrefetchScalarGridSpec(
            num_scalar_prefetch=3,
            grid=(n_slots // tm, nj),
            in_specs=[pl.BlockSpec(memory_space=pl.ANY),
                      pl.BlockSpec((1, d, tf), wmap_col),
                      pl.BlockSpec((1, d, tf), wmap_col),
                      pl.BlockSpec((1, tf, d), wmap_row)],
            out_specs=pl.BlockSpec((tm, d), lambda i, j, *_: (i, 0)),
            scratch_shapes=[pltpu.VMEM((3, tm, d), F32), pltpu.VMEM((tm, d), BF16),
                            pltpu.VMEM((tm, d), F32), pltpu.SemaphoreType.DMA((3,))]),
        compiler_params=_cparams(("arbitrary", "arbitrary")),
        name="moe_expert_ffn",
    )(src, tile_expert, n_active, xn, wg, wu, wd)


def _moe_combine_kernel(pos_ref, h_ref, info_ref, ys_hbm, g_ref, o_ref, buf, sem, *, tm, final_norm):
    i = pl.program_id(0)
    n = pl.num_programs(0)
    cur = i % 2

    def start_gather(tile, b):
        for s in range(tm):
            for k in range(2):
                _row_copy(ys_hbm, pos_ref[2 * (tile * tm + s) + k], buf.at[b, k], s, sem.at[b]).start()

    @pl.when(i == 0)
    def _():
        start_gather(0, 0)

    @pl.when(i + 1 < n)
    def _():
        start_gather(i + 1, 1 - cur)

    for k in range(2):
        _rows_wait(ys_hbm, buf.at[cur, k], sem.at[cur])
    info = info_ref[...]
    y = h_ref[...] + info[:, 2:3] * buf[cur, 0] + info[:, 3:4] * buf[cur, 1]
    if final_norm:
        y = _rms(y, g_ref[...], y.shape[-1])
    o_ref[...] = y


def moe_combine(h, info, ys, pos_flat, g_final, *, tm, final_norm):
    m, d = h.shape
    return pl.pallas_call(
        functools.partial(_moe_combine_kernel, tm=tm, final_norm=final_norm),
        out_shape=jax.ShapeDtypeStruct((m, d), F32),
        grid_spec=pltpu.PrefetchScalarGridSpec(
            num_scalar_prefetch=1,
            grid=(m // tm,),
            in_specs=[pl.BlockSpec((tm, d), lambda i, *_: (i, 0)),
                      pl.BlockSpec((tm, LANES), lambda i, *_: (i, 0)),
                      pl.BlockSpec(memory_space=pl.ANY),
                      pl.BlockSpec((1, d), lambda i, *_: (0, 0))],
            out_specs=pl.BlockSpec((tm, d), lambda i, *_: (i, 0)),
            scratch_shapes=[pltpu.VMEM((2, 2, tm, d), F32), pltpu.SemaphoreType.DMA((2,))]),
        compiler_params=_cparams(("arbitrary",)),
        name="moe_combine",
    )(pos_flat, h, info, ys, g_final.reshape(1, d).astype(F32))


def moe_layer(h, g, router, wg, wu, wd, g_final, *, final_norm, tm_r=512, tm_g=512, tf=1792, tm_c=256):
    m, d = h.shape
    xn, info, cnt = moe_router(h, g, router, tm=tm_r)
    e_idx = info[:, 0:2].astype(jnp.int32)
    rank = info[:, 4:6].astype(jnp.int32)
    counts = cnt[0, :N_EXPERTS].astype(jnp.int32)
    tiles_per = (counts + tm_g - 1) // tm_g
    tile_end = jnp.cumsum(tiles_per)
    seg_start = (tile_end - tiles_per) * tm_g
    pos = rank
    for e in range(N_EXPERTS):
        pos = pos + jnp.where(e_idx == e, seg_start[e], 0)
    n_tiles = (2 * m) // tm_g + N_EXPERTS
    n_slots = n_tiles * tm_g
    tok = jnp.broadcast_to(jnp.arange(m, dtype=jnp.int32)[:, None], (m, 2))
    src = jnp.zeros((n_slots,), jnp.int32).at[pos.reshape(-1)].set(tok.reshape(-1))
    n_active = tile_end[-1:].astype(jnp.int32)
    tile_ids = jnp.minimum(jnp.arange(n_tiles, dtype=jnp.int32), n_active[0] - 1)
    tile_expert = jnp.sum(tile_ids[:, None] >= tile_end[None, :], axis=1).astype(jnp.int32)
    ys = moe_expert_ffn(xn, src, tile_expert, n_active,
                        wg, wu, wd, tm=tm_g, tf=tf)
    return moe_combine(h, info, ys, pos.reshape(-1).astype(jnp.int32), g_final, tm=tm_c, final_norm=final_norm)


def _final_norm_kernel(h_ref, g_ref, o_ref):
    h = h_ref[...]
    o_ref[...] = _rms(h, g_ref[...], h.shape[-1])


def final_rmsnorm(h, g, *, tm):
    m, d = h.shape
    return pl.pallas_call(
        _final_norm_kernel,
        out_shape=jax.ShapeDtypeStruct((m, d), F32),
        grid=(m // tm,),
        in_specs=[pl.BlockSpec((tm, d), lambda i: (i, 0)), pl.BlockSpec((1, d), lambda i: (0, 0))],
        out_specs=pl.BlockSpec((tm, d), lambda i: (i, 0)),
        compiler_params=_cparams(("parallel",)),
        name="final_rmsnorm",
    )(h, g.reshape(1, d).astype(F32))


def _pack_w_in(w):
    d = w.shape[0]
    w = w.astype(F32)
    o = 0
    u = w[:, o:o + SSM_WIDTH]; o += SSM_WIDTH
    cq = w[:, o:o + MLA_Q_RANK]; o += MLA_Q_RANK
    ckv = w[:, o:o + MLA_KV_RANK]; o += MLA_KV_RANK
    kr = w[:, o:o + MLA_ROPE]; o += MLA_ROPE
    nq = w[:, o:o + NSA_HEADS * NSA_DIM]; o += NSA_HEADS * NSA_DIM
    nkv = w[:, o:o + 6 * NSA_KV_HEADS * NSA_DIM]; o += 6 * NSA_KV_HEADS * NSA_DIM
    gate = w[:, o:o + 3 * NSA_HEADS]
    z = lambda n: jnp.zeros((d, n), F32)
    kr_a = jnp.concatenate([z(MLA_NOPE), kr, z(LANES - MLA_NOPE - MLA_ROPE)], axis=1)
    kr_b = jnp.concatenate([z(MLA_NOPE), _rot_half_cols(kr), z(LANES - MLA_NOPE - MLA_ROPE)], axis=1)
    nq_h = (nq * NSA_DIM ** -0.5).reshape(d, NSA_KV_HEADS, NSA_REP, NSA_DIM)
    zq = jnp.zeros((d, NSA_REP, NSA_DIM), F32)
    nq_p = jnp.concatenate([
        jnp.concatenate([nq_h[:, 0], zq], axis=-1).reshape(d, NSA_REP * LANES),
        jnp.concatenate([zq, nq_h[:, 1]], axis=-1).reshape(d, NSA_REP * LANES)], axis=1)
    packed = jnp.concatenate([u, cq, z(256 - MLA_Q_RANK), kr_a, kr_b, nq_p, nkv, ckv,
                              gate, z(LANES - 3 * NSA_HEADS)], axis=1)
    assert packed.shape[1] == IN_COLS_PACKED
    return packed.astype(BF16)


def _rg_order(a):
    rest = a.shape[1:]
    return a.reshape((NSA_KV_HEADS, NSA_REP, NSA_DIM) + rest).swapaxes(0, 1).reshape((-1,) + rest)


def kernel(x, mem, w_in, w_out, mix_norm, out_norm, ssm_a_re, ssm_a_im, ssm_b_re, ssm_b_im, ssm_c_re, ssm_c_im, ssm_d, ssm_log_dt, ssm_w_glu, mla_q_norm, mla_w_uq, mla_kv_norm, mla_w_ukv, nsa_cmp_pe, nsa_cmp_w1, nsa_cmp_w2, rel_bias, xattn_norm, mem_norm, xattn_wq, xattn_wkv, xattn_wo, ffn_norm, dense_w_gate, dense_w_up, dense_w_down, moe_router, moe_w_gate, moe_w_up, moe_w_down, final_norm):
    bsz, seq, d = x.shape
    depth = w_in.shape[0]
    T = bsz * seq
    nmem = mem.shape[1]
    tq_nsa, tk_nsa = 2 * LANES, 4 * LANES
    rope_tabs = _rope_tables(seq)
    nsa_tabs = _nsa_bias_tables(rel_bias, seq, tq_nsa)
    o1, o2 = SSM_WIDTH, SSM_WIDTH + MLA_HEADS * MLA_V
    mem2 = mem.reshape(bsz * nmem, d)
    h = x.reshape(T, d)
    for l in range(depth):
        proj = norm_matmul(h, mix_norm[l], _pack_w_in(w_in[l]), tm=512, tn=IN_COLS_PACKED, out_dtype=BF16)
        proj3 = proj.reshape(bsz, seq, IN_COLS_PACKED)
        u_tm = proj3[:, :, C_U:C_U + SSM_WIDTH].transpose(1, 0, 2).reshape(seq * bsz, SSM_WIDTH)
        y_ssm = ssm_mixer(u_tm, ssm_a_re[l], ssm_a_im[l], ssm_b_re[l], ssm_b_im[l], ssm_c_re[l], ssm_c_im[l],
                          ssm_d[l], ssm_log_dt[l], ssm_w_glu[l], nb=bsz, tc=64)
        y_ssm = y_ssm.reshape(seq, bsz, SSM_WIDTH).transpose(1, 0, 2).reshape(T, SSM_WIDTH)
        y_mla = mla_mixer(proj3, mla_q_norm[l], mla_w_uq[l], mla_kv_norm[l], mla_w_ukv[l], rope_tabs,
                          tm=512, tq=512).reshape(T, -1)
        nch = seq // CMP_STRIDE
        kv_cr = jnp.stack([proj3[:, :, C_NKV:C_NKV + LANES].reshape(bsz, nch, CMP_STRIDE * LANES),
                           proj3[:, :, C_NKV + LANES:C_NKV + 2 * LANES].reshape(bsz, nch, CMP_STRIDE * LANES)],
                          axis=1)
        kvc = nsa_compress(kv_cr, nsa_cmp_pe[l], nsa_cmp_w1[l], nsa_cmp_w2[l])
        y_nsa = nsa_mixer(proj3, kvc, nsa_tabs, tq=tq_nsa, tk=tk_nsa).reshape(T, -1)
        g_out = out_norm[l]
        wo_l = w_out[l]
        h = out_projection(y_ssm, y_mla, y_nsa, h,
                           [g_out[:o1], g_out[o1:o2], _rg_order(g_out[o2:])],
                           [wo_l[:o1].astype(BF16), wo_l[o1:o2].astype(BF16), _rg_order(wo_l[o2:]).astype(BF16)],
                           tm=512)
        kv_mem = norm_matmul(mem2, mem_norm[l], xattn_wkv[l].astype(BF16), tm=256, tn=512, out_dtype=BF16)
        h = cross_attention(h.reshape(bsz, seq, d), kv_mem.reshape(bsz, nmem, 2 * d), xattn_norm[l],
                            xattn_wq[l], xattn_wo[l], tm=256).reshape(T, d)
        last = l == depth - 1
        if l % 2 == 0:
            h = dense_ffn(h, ffn_norm[l], dense_w_gate[l // 2], dense_w_up[l // 2], dense_w_down[l // 2],
                          tm=512, tf=1408)
            if last:
                h = final_rmsnorm(h, final_norm, tm=512)
        else:
            h = moe_layer(h, ffn_norm[l], moe_router[l // 2], moe_w_gate[l // 2], moe_w_up[l // 2],
                          moe_w_down[l // 2], final_norm, final_norm=last)
    return h.reshape(bsz, seq, d)
```

```python
import functools
import math

import jax
import jax.numpy as jnp
from jax import lax
from jax.experimental import pallas as pl
from jax.experimental.pallas import tpu as pltpu

F32 = jnp.float32
BF16 = jnp.bfloat16

HEAD_DIM = 64
SSM_WIDTH = 256
SSM_CH = 16
SSM_GROUPS = 16
SSM_STATE = 64
MLA_HEADS = 6
MLA_NOPE = 64
MLA_ROPE = 32
MLA_V = 64
MLA_Q_RANK = 192
MLA_KV_RANK = 128
NSA_HEADS = 6
NSA_KV_HEADS = 2
NSA_REP = 3
NSA_DIM = 64
CMP_BLOCK = 32
CMP_STRIDE = 16
SEL_BLOCK = 64
SEL_TOPN = 8
WINDOW = 256
REL_BUCKETS = 32
REL_MAX_DIST = 128
XATTN_HEADS = 4
N_EXPERTS = 8
ROPE_THETA = 10000.0
EPS = 1e-6
NEG_INF = -1e30
FORCE = 1e9

LANES = 128
VMEM_LIMIT = 56 * 1024 * 1024

C_U, C_CQ, C_KR, C_NQ, C_NKV, C_CKV, C_GATE = 0, 256, 512, 768, 1536, 2304, 2432
IN_COLS_PACKED = 2560


def _cparams(sem):
    return pltpu.CompilerParams(dimension_semantics=sem, vmem_limit_bytes=VMEM_LIMIT)


def _dot(a, b):
    return jnp.dot(a, b, preferred_element_type=F32)


def _dot_t(a, b):
    return lax.dot_general(a, b, (((1,), (1,)), ((), ())), preferred_element_type=F32)


def _rms(x, g, n):
    ms = jnp.sum(x * x, axis=-1, keepdims=True) * (1.0 / n)
    return x * lax.rsqrt(ms + EPS) * g


def _sigmoid(x):
    return 1.0 / (1.0 + jnp.exp(-x))


def _silu(x):
    return x * _sigmoid(x)


def _norm_mm_kernel(x_ref, g_ref, w_ref, o_ref, xn_ref):
    @pl.when(pl.program_id(1) == 0)
    def _():
        x = x_ref[...].astype(F32)
        xn_ref[...] = _rms(x, g_ref[...], x.shape[-1]).astype(BF16)

    o_ref[...] = _dot(xn_ref[...], w_ref[...]).astype(o_ref.dtype)


def norm_matmul(x, g, w, *, tm, tn, out_dtype):
    m, k = x.shape
    n = w.shape[1]
    return pl.pallas_call(
        _norm_mm_kernel,
        out_shape=jax.ShapeDtypeStruct((m, n), out_dtype),
        grid=(m // tm, n // tn),
        in_specs=[pl.BlockSpec((tm, k), lambda i, j: (i, 0)),
                  pl.BlockSpec((1, k), lambda i, j: (0, 0)),
                  pl.BlockSpec((k, tn), lambda i, j: (0, j))],
        out_specs=pl.BlockSpec((tm, tn), lambda i, j: (i, j)),
        scratch_shapes=[pltpu.VMEM((tm, k), BF16)],
        compiler_params=_cparams(("parallel", "arbitrary")),
        name="norm_matmul",
    )(x, g.reshape(1, k).astype(F32), w)


def _ssm_kernel(u_ref, bbr_ref, bbi_ref, ar_ref, ai_ref, ccr_ref, cci_ref, d_ref, wglu_ref,
                o_ref, hr_ref, hi_ref, cr_ref, ci_ref, *, tc, nb):
    @pl.when(pl.program_id(0) == 0)
    def _():
        cr_ref[...] = jnp.zeros_like(cr_ref)
        ci_ref[...] = jnp.zeros_like(ci_ref)

    u = u_ref[...]
    hr_ref[...] = _dot(u, bbr_ref[...])
    hi_ref[...] = _dot(u, bbi_ref[...])
    gp = ar_ref.shape[-1]
    ar = jnp.broadcast_to(ar_ref[...], (nb, gp))
    ai = jnp.broadcast_to(ai_ref[...], (nb, gp))

    def step(t, carry):
        hr, hi = carry
        rows = pl.ds(pl.multiple_of(t * nb, nb), nb)
        nr = ar * hr - ai * hi + hr_ref[rows, :]
        ni = ar * hi + ai * hr + hi_ref[rows, :]
        hr_ref[rows, :] = nr
        hi_ref[rows, :] = ni
        return nr, ni

    hr, hi = lax.fori_loop(0, tc, step, (cr_ref[...], ci_ref[...]))
    cr_ref[...] = hr
    ci_ref[...] = hi
    y = (_dot(hr_ref[...].astype(BF16), ccr_ref[...]) + _dot(hi_ref[...].astype(BF16), cci_ref[...])
         + d_ref[...] * u.astype(F32))
    y = jax.nn.gelu(y)
    z = _dot(y.astype(BF16), wglu_ref[...])
    o_ref[...] = (y * _sigmoid(z)).astype(o_ref.dtype)


def ssm_mixer(u_tm, a_re, a_im, b_re, b_im, c_re, c_im, d, log_dt, w_glu, *, nb, tc):
    rows = u_tm.shape[0]
    G, P, C = SSM_GROUPS, SSM_STATE, SSM_CH
    dt = jnp.exp(log_dt.astype(F32))[:, None]
    lr, li = a_re.astype(F32), a_im.astype(F32)
    mag = jnp.exp(lr * dt)
    ab_r, ab_i = mag * jnp.cos(li * dt), mag * jnp.sin(li * dt)
    den = lr * lr + li * li
    nr = ab_r - 1.0
    f_r = (nr * lr + ab_i * li) / den
    f_i = (ab_i * lr - nr * li) / den
    br, bi = b_re.astype(F32), b_im.astype(F32)
    bb_r = f_r[..., None] * br - f_i[..., None] * bi
    bb_i = f_r[..., None] * bi + f_i[..., None] * br
    eye = jnp.eye(G, dtype=F32)
    bbr = jnp.einsum('gpc,gh->gchp', bb_r, eye).reshape(G * C, G * P).astype(BF16)
    bbi = jnp.einsum('gpc,gh->gchp', bb_i, eye).reshape(G * C, G * P).astype(BF16)
    ccr = jnp.einsum('gcp,gh->gphc', c_re.astype(F32), eye).reshape(G * P, G * C).astype(BF16)
    cci = jnp.einsum('gcp,gh->gphc', -c_im.astype(F32), eye).reshape(G * P, G * C).astype(BF16)
    gp = G * P
    full = lambda shape: pl.BlockSpec(shape, lambda i: (0,) * len(shape))
    return pl.pallas_call(
        functools.partial(_ssm_kernel, tc=tc, nb=nb),
        out_shape=jax.ShapeDtypeStruct((rows, SSM_WIDTH), BF16),
        grid=(rows // (tc * nb),),
        in_specs=[pl.BlockSpec((tc * nb, SSM_WIDTH), lambda i: (i, 0)),
                  full((G * C, gp)), full((G * C, gp)), full((1, gp)), full((1, gp)),
                  full((gp, G * C)), full((gp, G * C)), full((1, SSM_WIDTH)),
                  full((SSM_WIDTH, SSM_WIDTH))],
        out_specs=pl.BlockSpec((tc * nb, SSM_WIDTH), lambda i: (i, 0)),
        scratch_shapes=[pltpu.VMEM((tc * nb, gp), F32), pltpu.VMEM((tc * nb, gp), F32),
                        pltpu.VMEM((nb, gp), F32), pltpu.VMEM((nb, gp), F32)],
        compiler_params=_cparams(("arbitrary",)),
        name="ssm_mixer",
    )(u_tm, bbr, bbi, ab_r.reshape(1, gp), ab_i.reshape(1, gp), ccr, cci,
      d.reshape(1, SSM_WIDTH).astype(F32), w_glu.astype(BF16))


def _mla_prep_kernel(cq_ref, kr_ref, ckv_ref, gq_ref, gkv_ref, wqa_ref, wqb_ref, wk_ref, wv_ref,
                     c1_ref, c0_ref, s0_ref, q_ref, k_ref, v_ref):
    qn = _rms(cq_ref[0].astype(F32), gq_ref[...], MLA_Q_RANK).astype(BF16)
    qa = _dot(qn, wqa_ref[...])
    qb = _dot(qn, wqb_ref[...])
    kn = _rms(ckv_ref[0].astype(F32), gkv_ref[...], MLA_KV_RANK).astype(BF16)
    ka = _dot(kn, wk_ref[...])
    va = _dot(kn, wv_ref[...])
    kr = kr_ref[0].astype(F32)
    c1, c0, s0 = c1_ref[...], c0_ref[...], s0_ref[...]
    krope = kr[:, :LANES] * c0 + kr[:, LANES:] * s0
    for h in range(MLA_HEADS):
        sl = slice(h * LANES, (h + 1) * LANES)
        q_ref[0, h] = (qa[:, sl] * c1 + qb[:, sl] * s0).astype(BF16)
        k_ref[0, h] = (ka[:, sl] + krope).astype(BF16)
        v_ref[0, h] = va[:, sl].astype(BF16)


def _mla_flash_kernel(q_ref, k_ref, v_ref, o_ref, m_ref, l_ref, acc_ref, *, tq):
    qi = pl.program_id(1)
    m_ref[...] = jnp.full_like(m_ref, NEG_INF)
    l_ref[...] = jnp.zeros_like(l_ref)
    acc_ref[...] = jnp.zeros_like(acc_ref)
    lane = lax.broadcasted_iota(jnp.int32, (tq, LANES), 1)
    rep = tq // LANES

    def tile(kt, masked):
        ks = pl.ds(pl.multiple_of(kt * tq, tq), tq)
        if masked:
            mask = (lax.broadcasted_iota(jnp.int32, (tq, tq), 1)
                    <= lax.broadcasted_iota(jnp.int32, (tq, tq), 0))
        for pr in range(MLA_HEADS // 2):
            hs = (2 * pr, 2 * pr + 1)
            s = [_dot_t(q_ref[0, h], k_ref[0, h, ks, :]) for h in hs]
            if masked:
                s = [jnp.where(mask, x, NEG_INF) for x in s]
            m_prev = [m_ref[h] for h in hs]
            m_new = [jnp.maximum(mp, jnp.max(x, axis=-1, keepdims=True)) for mp, x in zip(m_prev, s)]
            alpha = [jnp.exp(mp - mn) for mp, mn in zip(m_prev, m_new)]
            p = [jnp.exp(x - jnp.tile(mn, (1, rep))) for x, mn in zip(s, m_new)]
            for h, a, pp, mn in zip(hs, alpha, p, m_new):
                l_ref[h] = a * l_ref[h] + jnp.sum(pp, axis=-1, keepdims=True)
                m_ref[h] = mn
            pv = _dot(p[0].astype(BF16), v_ref[0, hs[0], ks, :]) + _dot(p[1].astype(BF16), v_ref[0, hs[1], ks, :])
            acc_ref[pr] = acc_ref[pr] * jnp.where(lane < MLA_V, alpha[0], alpha[1]) + pv

    def body(kt, c):
        tile(kt, False)
        return c

    lax.fori_loop(0, qi, body, 0)
    tile(qi, True)
    for pr in range(MLA_HEADS // 2):
        linv = jnp.where(lane < MLA_V, 1.0 / l_ref[2 * pr], 1.0 / l_ref[2 * pr + 1])
        o_ref[0, :, pr * LANES:(pr + 1) * LANES] = (acc_ref[pr] * linv).astype(o_ref.dtype)


def _rope_tables(seq):
    pos = jnp.arange(seq, dtype=F32)
    inv = 1.0 / (ROPE_THETA ** (jnp.arange(0, MLA_ROPE, 2, dtype=F32) / MLA_ROPE))
    ang = pos[:, None] * inv[None, :]
    cos, sin = jnp.cos(ang), jnp.sin(ang)
    cos2 = jnp.concatenate([cos, cos], axis=-1)
    sin2 = jnp.concatenate([sin, sin], axis=-1)
    z64 = jnp.zeros((seq, MLA_NOPE), F32)
    z32 = jnp.zeros((seq, LANES - MLA_NOPE - MLA_ROPE), F32)
    c1 = jnp.concatenate([jnp.ones((seq, MLA_NOPE), F32), cos2, z32], axis=-1)
    c0 = jnp.concatenate([z64, cos2, z32], axis=-1)
    s0 = jnp.concatenate([z64, sin2, z32], axis=-1)
    return c1, c0, s0


def _rot_half_cols(w):
    half = MLA_ROPE // 2
    return jnp.concatenate([-w[..., half:], w[..., :half]], axis=-1)


def mla_mixer(proj3, q_norm, w_uq, kv_norm, w_ukv, tabs, *, tm, tq):
    bsz, seq, _ = proj3.shape
    H = MLA_HEADS
    scale = (MLA_NOPE + MLA_ROPE) ** -0.5
    wq = (w_uq.astype(F32) * scale).reshape(MLA_Q_RANK, H, MLA_NOPE + MLA_ROPE)
    zq = jnp.zeros((MLA_Q_RANK, H, LANES - MLA_NOPE - MLA_ROPE), F32)
    z64 = jnp.zeros((MLA_Q_RANK, H, MLA_NOPE), F32)
    wqa = jnp.concatenate([wq, zq], axis=-1).reshape(MLA_Q_RANK, H * LANES)
    wqb = jnp.concatenate([z64, _rot_half_cols(wq[..., MLA_NOPE:]), zq], axis=-1).reshape(MLA_Q_RANK, H * LANES)
    padq = ((0, 256 - MLA_Q_RANK), (0, 0))
    wqa = jnp.pad(wqa, padq).astype(BF16)
    wqb = jnp.pad(wqb, padq).astype(BF16)
    gq = jnp.pad(q_norm.astype(F32), (0, 256 - MLA_Q_RANK)).reshape(1, 256)
    wkv = w_ukv.astype(F32).reshape(MLA_KV_RANK, H, MLA_NOPE + MLA_V)
    zk = jnp.zeros((MLA_KV_RANK, H, MLA_NOPE), F32)
    wk = jnp.concatenate([wkv[..., :MLA_NOPE], zk], axis=-1).reshape(MLA_KV_RANK, H * LANES).astype(BF16)
    wv_h = wkv[..., MLA_NOPE:]
    even = (jnp.arange(H) % 2 == 0)[None, :, None]
    wv = jnp.concatenate([jnp.where(even, wv_h, 0.0), jnp.where(even, 0.0, wv_h)], axis=-1)
    wv = wv.reshape(MLA_KV_RANK, H * LANES).astype(BF16)
    c1, c0, s0 = tabs
    full2 = lambda shape: pl.BlockSpec(shape, lambda b, i: (0,) * len(shape))
    tab_spec = pl.BlockSpec((tm, LANES), lambda b, i: (i, 0))
    hd_spec = pl.BlockSpec((1, H, tm, LANES), lambda b, i: (b, 0, i, 0))
    hd_shape = jax.ShapeDtypeStruct((bsz, H, seq, LANES), BF16)
    q, k, v = pl.pallas_call(
        _mla_prep_kernel,
        out_shape=(hd_shape, hd_shape, hd_shape),
        grid=(bsz, seq // tm),
        in_specs=[pl.BlockSpec((1, tm, 256), lambda b, i: (b, i, C_CQ // 256)),
                  pl.BlockSpec((1, tm, 256), lambda b, i: (b, i, C_KR // 256)),
                  pl.BlockSpec((1, tm, 128), lambda b, i: (b, i, C_CKV // 128)),
                  full2((1, 256)), full2((1, 128)),
                  full2((256, H * LANES)), full2((256, H * LANES)),
                  full2((128, H * LANES)), full2((128, H * LANES)),
                  tab_spec, tab_spec, tab_spec],
        out_specs=(hd_spec, hd_spec, hd_spec),
        compiler_params=_cparams(("parallel", "parallel")),
        name="mla_prep",
    )(proj3, proj3, proj3, gq, kv_norm.astype(F32).reshape(1, 128), wqa, wqb, wk, wv, c1, c0, s0)

    return pl.pallas_call(
        functools.partial(_mla_flash_kernel, tq=tq),
        out_shape=jax.ShapeDtypeStruct((bsz, seq, H * MLA_V), BF16),
        grid=(bsz, seq // tq),
        in_specs=[pl.BlockSpec((1, H, tq, LANES), lambda b, i: (b, 0, i, 0)),
                  pl.BlockSpec((1, H, seq, LANES), lambda b, i: (b, 0, 0, 0)),
                  pl.BlockSpec((1, H, seq, LANES), lambda b, i: (b, 0, 0, 0))],
        out_specs=pl.BlockSpec((1, tq, H * MLA_V), lambda b, i: (b, i, 0)),
        scratch_shapes=[pltpu.VMEM((H, tq, LANES), F32), pltpu.VMEM((H, tq, LANES), F32),
                        pltpu.VMEM((H // 2, tq, LANES), F32)],
        compiler_params=_cparams(("parallel", "arbitrary")),
        name="mla_flash",
    )(q, k, v)


def _nsa_cmp_kernel(x_ref, pea_ref, peb_ref, w1a_ref, w1b_ref, w2_ref, o_ref):
    x = x_ref[0, 0]
    w1a, w1b = w1a_ref[0], w1b_ref[0]
    bias = _dot(pea_ref[0], w1a)[0:1] + _dot(peb_ref[0], w1b)[0:1]
    a = _dot(x, w1a)
    b = _dot(x, w1b)
    n = b.shape[0]
    pre = a + pltpu.roll(b, n - 1, 0) + bias
    o_ref[0, 0] = _dot(jax.nn.gelu(pre).astype(BF16), w2_ref[0]).astype(o_ref.dtype)


def nsa_compress(kv_cr, cmp_pe, cmp_w1, cmp_w2):
    bsz, _, nch, width = kv_cr.shape
    G, dh = NSA_KV_HEADS, NSA_DIM
    half = CMP_BLOCK // 2
    eye = jnp.eye(G, dtype=F32)
    w1r = cmp_w1.astype(F32).reshape(2, CMP_BLOCK, dh, dh)
    w1a = jnp.einsum('kpde,gh->kpgdhe', w1r[:, :half], eye).reshape(2, width, G * dh).astype(BF16)
    w1b = jnp.einsum('kpde,gh->kpgdhe', w1r[:, half:], eye).reshape(2, width, G * dh).astype(BF16)
    w2 = jnp.einsum('kde,gh->kgdhe', cmp_w2.astype(F32), eye).reshape(2, G * dh, G * dh).astype(BF16)
    pe = cmp_pe.astype(F32)
    pe_g = jnp.broadcast_to(pe[:, :, None, :], (2, CMP_BLOCK, G, dh))
    pea = jnp.broadcast_to(pe_g[:, :half].reshape(2, 1, width), (2, 8, width)).astype(BF16)
    peb = jnp.broadcast_to(pe_g[:, half:].reshape(2, 1, width), (2, 8, width)).astype(BF16)
    kvspec = lambda shape: pl.BlockSpec(shape, lambda b, k: (k,) + (0,) * (len(shape) - 1))
    return pl.pallas_call(
        _nsa_cmp_kernel,
        out_shape=jax.ShapeDtypeStruct((bsz, 2, nch, G * dh), BF16),
        grid=(bsz, 2),
        in_specs=[pl.BlockSpec((1, 1, nch, width), lambda b, k: (b, k, 0, 0)),
                  kvspec((1, 8, width)), kvspec((1, 8, width)),
                  kvspec((1, width, G * dh)), kvspec((1, width, G * dh)),
                  kvspec((1, G * dh, G * dh))],
        out_specs=pl.BlockSpec((1, 1, nch, G * dh), lambda b, k: (b, k, 0, 0)),
        compiler_params=_cparams(("parallel", "parallel")),
        name="nsa_compress",
    )(kv_cr, pea, peb, w1a, w1b, w2)


def _nsa_kernel(q_ref, gate_ref, ksl_ref, vsl_ref, kwn_ref, vwn_ref, kvc_ref, bc_ref, bw_ref,
                bs_ref, ov_ref, blk1h_ref, gexp_ref, o_ref, kaug_ref, vaug_ref, m_ref, acc_ref,
                *, tq, tk, nbs, n_sel):
    qi = pl.program_id(1)
    R, G = NSA_REP, NSA_KV_HEADS
    H = R * G
    nsub = tk // tq
    nwin = WINDOW // tq

    @pl.when(qi == 0)
    def _():
        kaug_ref[:, :LANES] = ksl_ref[0]
        kaug_ref[:, LANES:] = blk1h_ref[...]
        vaug_ref[:, :LANES] = vsl_ref[0]
        vaug_ref[:, LANES:] = jnp.ones((vaug_ref.shape[0], LANES), BF16)

    lane = lax.broadcasted_iota(jnp.int32, (tq, LANES), 1)
    t_row = qi * tq + lax.broadcasted_iota(jnp.int32, (H * tq, 1), 0) % tq
    kc = kvc_ref[0, 0]
    vc = kvc_ref[0, 1]
    ov = ov_ref[...]

    def stack(fn):
        return jnp.concatenate([fn(h) for h in range(H)], axis=0)

    q_all = stack(lambda h: q_ref[0, :, h * LANES:(h + 1) * LANES])

    c = jnp.minimum(qi, nwin)
    ws = pl.ds(pl.multiple_of(jnp.maximum(qi - nwin, 0) * tq, tq), (nwin + 1) * tq)
    k_w = kwn_ref[0, ws, :]
    v_w = vwn_ref[0, ws, :]
    o_w = []
    for g in range(G):
        rows = slice(g * R * tq, (g + 1) * R * tq)
        s_w = _dot_t(q_all[rows], k_w) + jnp.concatenate([bw_ref[g * R + r, c] for r in range(R)], axis=0)
        p_w = jnp.exp(s_w - jnp.max(s_w, axis=-1, keepdims=True))
        o_w.append(_dot(p_w.astype(BF16), v_w) / jnp.sum(p_w, axis=-1, keepdims=True))
    o_w = jnp.concatenate(o_w, axis=0)

    valid = t_row >= (CMP_BLOCK - 1)
    s = _dot_t(q_all, kc) + stack(lambda h: bc_ref[h])
    m = jnp.max(s, axis=-1, keepdims=True)
    p = jnp.where(valid, jnp.exp(s - m), 0.0)
    l = jnp.where(valid, jnp.sum(p, axis=-1, keepdims=True), 1.0)
    pc = p / l
    o_c = _dot(pc.astype(BF16), vc)
    blk = lax.broadcasted_iota(jnp.int32, (nbs, tq), 0)
    tl = qi * tq + lax.broadcasted_iota(jnp.int32, (nbs, tq), 1)
    cur = tl // SEL_BLOCK
    forced = (blk == 0) | (blk == cur) | (blk == cur - 1)
    future = blk * SEL_BLOCK > tl
    qmask = []
    for g in range(G):
        b0 = g * R * tq
        psum = pc[b0:b0 + tq] + pc[b0 + tq:b0 + 2 * tq] + pc[b0 + 2 * tq:b0 + 3 * tq]
        p_hi = psum.astype(BF16)
        p_lo = (psum - p_hi.astype(F32)).astype(BF16)
        imp = (_dot_t(ov, p_hi) + _dot_t(ov, p_lo))[:nbs]
        imp = jnp.where(forced, FORCE, jnp.where(future, -FORCE, imp))
        rank = jnp.zeros((nbs, tq), F32)
        for i in range(nbs):
            ri = imp[i:i + 1, :]
            beats = (ri > imp) | ((ri == imp) & (blk > i))
            rank = rank + jnp.where(beats, 1.0, 0.0)
        sel = jnp.where(rank < n_sel, 0.0, NEG_INF)
        sel = jnp.concatenate([sel, jnp.zeros((LANES - nbs, tq), F32)], axis=0).T.astype(BF16)
        qmask += [sel] * R
    q_aug = jnp.concatenate([q_all, jnp.concatenate(qmask, axis=0)], axis=1)

    m_ref[...] = jnp.full_like(m_ref, NEG_INF)
    acc_ref[...] = jnp.zeros_like(acc_ref)

    halves = [slice(g * R * tq, (g + 1) * R * tq) for g in range(G)]

    def sel_tile(kt, bias):
        ks = pl.ds(pl.multiple_of(kt * tk, tk), tk)
        k_t = kaug_ref[ks, :]
        v_t = vaug_ref[ks, :]
        s = [_dot_t(q_aug[hs], k_t) for hs in halves]
        if bias is not None:
            s = [x + bias[hs] for x, hs in zip(s, halves)]
        m_prev = [m_ref[hs, :] for hs in halves]
        m_new = [jnp.maximum(mp, jnp.max(x, axis=-1, keepdims=True)) for mp, x in zip(m_prev, s)]
        alpha = [jnp.exp(mp - mn) for mp, mn in zip(m_prev, m_new)]
        p = [jnp.exp(x - jnp.tile(mn, (1, tk // LANES))).astype(BF16) for x, mn in zip(s, m_new)]
        pv = [_dot(pp, v_t) for pp in p]
        for hs, mn, a, o in zip(halves, m_new, alpha, pv):
            m_ref[hs, :] = mn
            acc_ref[hs, :] = acc_ref[hs, :] * jnp.tile(a, (1, 2)) + o

    def near_bias(kt):
        cols = []
        for sub in range(nsub):
            d = qi - (kt * nsub + sub)
            cols.append(stack(lambda h: jnp.where(d == 0, bs_ref[h, 0], jnp.where(
                d == 1, bs_ref[h, 1], jnp.where(d < 0, NEG_INF, 0.0)))))
        return jnp.concatenate(cols, axis=1)

    def far_body(kt, c):
        sel_tile(kt, None)
        return c

    kd = (qi * tq) // tk
    lax.fori_loop(0, jnp.maximum(kd - 1, 0), far_body, 0)

    @pl.when(kd >= 1)
    def _():
        sel_tile(kd - 1, near_bias(kd - 1))

    sel_tile(kd, near_bias(kd))
    o_s = acc_ref[:, :LANES] / acc_ref[:, LANES:]

    gates = _sigmoid(_dot(gate_ref[0], gexp_ref[...]))
    for r in range(R):
        res = None
        for b, o_b in enumerate((o_c, o_s, o_w)):
            o_br = jnp.where(lane < NSA_DIM, o_b[r * tq:(r + 1) * tq], o_b[(R + r) * tq:(R + r + 1) * tq])
            term = gates[:, (b * R + r) * LANES:(b * R + r + 1) * LANES] * o_br
            res = term if res is None else res + term
        o_ref[0, :, r * LANES:(r + 1) * LANES] = res.astype(o_ref.dtype)


def _t5_bucket(dist):
    n = jnp.maximum(dist, 0)
    exact = REL_BUCKETS // 2
    nf = jnp.maximum(n, exact).astype(F32)
    large = exact + jnp.floor(jnp.log(nf / exact) / math.log(REL_MAX_DIST / exact)
                              * (REL_BUCKETS - exact)).astype(jnp.int32)
    return jnp.where(n < exact, n, jnp.minimum(large, REL_BUCKETS - 1))


def _nsa_bias_tables(rel_bias, seq, tq):
    rb = rel_bias.astype(F32).T
    far = rb[:, REL_BUCKETS - 1].reshape(NSA_HEADS, 1, 1)

    def by_dist(dist, ok, shift=0.0):
        bucket = _t5_bucket(dist)[None]
        out = jnp.zeros((NSA_HEADS,) + dist.shape, F32)
        for k in range(REL_BUCKETS):
            out = jnp.where(bucket == k, rb[:, k].reshape((NSA_HEADS,) + (1,) * dist.ndim), out)
        return jnp.where(ok[None], out - shift, NEG_INF)

    i = jnp.arange(tq)[:, None]
    t = jnp.arange(seq)[:, None]
    dist_c = t - (jnp.arange(LANES)[None, :] * CMP_STRIDE + CMP_BLOCK - 1)
    bc = by_dist(dist_c, dist_c >= 0)
    nwin = WINDOW // tq
    jw = jnp.arange((nwin + 1) * tq)[None, :]
    bw = jnp.stack([by_dist(tq * c + i - jw, (tq * c + i - jw >= 0) & (tq * c + i - jw < WINDOW))
                    for c in range(nwin + 1)], axis=1)
    js = jnp.arange(tq)[None, :]
    bs = jnp.stack([by_dist(tq * c + i - js, tq * c + i - js >= 0, far) for c in range(2)], axis=1)
    ci = jnp.arange(LANES)[:, None]
    sj = jnp.arange(LANES)[None, :]
    nbs = seq // SEL_BLOCK
    ov = ((ci * CMP_STRIDE <= sj * SEL_BLOCK + SEL_BLOCK - 1)
          & (ci * CMP_STRIDE + CMP_BLOCK - 1 >= sj * SEL_BLOCK)
          & (ci < seq // CMP_STRIDE - 1) & (sj < nbs))
    blk1h = (jnp.arange(seq)[:, None] // SEL_BLOCK == sj).astype(BF16)
    col = jnp.arange(3 * NSA_REP * LANES)
    slab, lane_g = col // LANES, (col % LANES) // NSA_DIM
    gate_col = (slab // NSA_REP) * NSA_HEADS + lane_g * NSA_REP + slab % NSA_REP
    gexp = (ci == gate_col[None, :]).astype(BF16)
    return bc, bw, bs, ov.T.astype(BF16), blk1h, gexp


def nsa_mixer(proj3, kvc, tables, *, tq, tk):
    bsz, seq, _ = proj3.shape
    assert tq % LANES == 0 and tq >= REL_MAX_DIST and tk % tq == 0 and seq % tk == 0
    assert seq // CMP_STRIDE == LANES and WINDOW % tq == 0
    nwin = WINDOW // tq
    bc, bw, bs, ov, blk1h, gexp = tables
    nbs = seq // SEL_BLOCK
    H = NSA_HEADS
    slab = lambda j: pl.BlockSpec((1, seq, LANES), lambda b, i: (b, 0, C_NKV // LANES + j))
    const = lambda shape: pl.BlockSpec(shape, lambda b, i: (0,) * len(shape))
    return pl.pallas_call(
        functools.partial(_nsa_kernel, tq=tq, tk=tk, nbs=nbs, n_sel=min(SEL_TOPN, nbs)),
        out_shape=jax.ShapeDtypeStruct((bsz, seq, H * NSA_DIM), BF16),
        grid=(bsz, seq // tq),
        in_specs=[pl.BlockSpec((1, tq, H * LANES), lambda b, i: (b, i, C_NQ // (H * LANES))),
                  pl.BlockSpec((1, tq, LANES), lambda b, i: (b, i, C_GATE // LANES)),
                  slab(2), slab(3), slab(4), slab(5),
                  pl.BlockSpec((1, 2, LANES, LANES), lambda b, i: (b, 0, 0, 0)),
                  pl.BlockSpec((H, tq, LANES), lambda b, i: (0, i, 0)),
                  const((H, nwin + 1, tq, (nwin + 1) * tq)), const((H, 2, tq, tq)),
                  const((LANES, LANES)), const((seq, LANES)), const(gexp.shape)],
        out_specs=pl.BlockSpec((1, tq, H * NSA_DIM), lambda b, i: (b, i, 0)),
        scratch_shapes=[pltpu.VMEM((seq, 2 * LANES), BF16), pltpu.VMEM((seq, 2 * LANES), BF16),
                        pltpu.VMEM((H * tq, LANES), F32), pltpu.VMEM((H * tq, 2 * LANES), F32)],
        compiler_params=_cparams(("parallel", "arbitrary")),
        name="nsa_attention",
    )(proj3, proj3, proj3, proj3, proj3, proj3, kvc, bc, bw, bs, ov, blk1h, gexp)


def _outproj_kernel(ys_ref, ym_ref, yn_ref, h_ref, g1_ref, g2_ref, g3_ref, w1_ref, w2_ref, w3_ref, o_ref):
    def part(y_ref, g_ref, w_ref):
        y = y_ref[...].astype(F32)
        return _dot(_rms(y, g_ref[...], y.shape[-1]).astype(BF16), w_ref[...])

    o_ref[...] = h_ref[...] + part(ys_ref, g1_ref, w1_ref) + part(ym_ref, g2_ref, w2_ref) + part(yn_ref, g3_ref, w3_ref)


def out_projection(y_ssm, y_mla, y_nsa, h, gains, weights, *, tm):
    m, d = h.shape
    row = lambda w: pl.BlockSpec((tm, w), lambda i: (i, 0))
    full = lambda a: pl.BlockSpec(a.shape, lambda i: (0, 0))
    gains = [g.reshape(1, -1).astype(F32) for g in gains]
    return pl.pallas_call(
        _outproj_kernel,
        out_shape=jax.ShapeDtypeStruct((m, d), F32),
        grid=(m // tm,),
        in_specs=[row(y_ssm.shape[1]), row(y_mla.shape[1]), row(y_nsa.shape[1]), row(d)]
                 + [full(g) for g in gains] + [full(w) for w in weights],
        out_specs=row(d),
        compiler_params=_cparams(("parallel",)),
        name="out_projection",
    )(y_ssm, y_mla, y_nsa, h, *gains, *weights)


def _xattn_kernel(h_ref, kv_ref, g_ref, wq_ref, wo_ref, o_ref, *, dh):
    h = h_ref[0]
    xn = _rms(h, g_ref[...], h.shape[-1]).astype(BF16)
    q = _dot(xn, wq_ref[...]).astype(BF16)
    hw = XATTN_HEADS * dh
    heads = range(XATTN_HEADS)
    s = [_dot_t(q[:, hd * dh:(hd + 1) * dh], kv_ref[0, :, hd * dh:(hd + 1) * dh]) for hd in heads]
    p = [jnp.exp(x - jnp.max(x, axis=-1, keepdims=True)) for x in s]
    p = [(x / jnp.sum(x, axis=-1, keepdims=True)).astype(BF16) for x in p]
    outs = [_dot(p[hd], kv_ref[0, :, hw + hd * dh:hw + (hd + 1) * dh]).astype(BF16) for hd in heads]
    o = jnp.concatenate(outs, axis=-1)
    o_ref[0] = h + _dot(o, wo_ref[...])


def cross_attention(h3, kv3, g_x, wq, wo, *, tm):
    bsz, seq, d = h3.shape
    m = kv3.shape[1]
    dh = d // XATTN_HEADS
    wq_s = (wq.astype(F32) * dh ** -0.5).astype(BF16)
    const = lambda shape: pl.BlockSpec(shape, lambda b, i: (0,) * len(shape))
    return pl.pallas_call(
        functools.partial(_xattn_kernel, dh=dh),
        out_shape=jax.ShapeDtypeStruct((bsz, seq, d), F32),
        grid=(bsz, seq // tm),
        in_specs=[pl.BlockSpec((1, tm, d), lambda b, i: (b, i, 0)),
                  pl.BlockSpec((1, m, 2 * d), lambda b, i: (b, 0, 0)),
                  const((1, d)), const((d, d)), const((d, d))],
        out_specs=pl.BlockSpec((1, tm, d), lambda b, i: (b, i, 0)),
        compiler_params=_cparams(("parallel", "parallel")),
        name="cross_attention",
    )(h3, kv3, g_x.reshape(1, d).astype(F32), wq_s, wo.astype(BF16))


def _ffn_kernel(h_ref, g_ref, wg_ref, wu_ref, wd_ref, o_ref, xn_ref, acc_ref):
    j = pl.program_id(1)

    @pl.when(j == 0)
    def _():
        h = h_ref[...]
        xn_ref[...] = _rms(h, g_ref[...], h.shape[-1]).astype(BF16)
        acc_ref[...] = h

    xn = xn_ref[...]
    a = _silu(_dot(xn, wg_ref[...])) * _dot(xn, wu_ref[...])
    acc_ref[...] += _dot(a.astype(BF16), wd_ref[...])

    @pl.when(j == pl.num_programs(1) - 1)
    def _():
        o_ref[...] = acc_ref[...]


def dense_ffn(h, g, wg, wu, wd, *, tm, tf):
    m, d = h.shape
    ff = wg.shape[1]
    return pl.pallas_call(
        _ffn_kernel,
        out_shape=jax.ShapeDtypeStruct((m, d), F32),
        grid=(m // tm, ff // tf),
        in_specs=[pl.BlockSpec((tm, d), lambda i, j: (i, 0)),
                  pl.BlockSpec((1, d), lambda i, j: (0, 0)),
                  pl.BlockSpec((d, tf), lambda i, j: (0, j)),
                  pl.BlockSpec((d, tf), lambda i, j: (0, j)),
                  pl.BlockSpec((tf, d), lambda i, j: (j, 0))],
        out_specs=pl.BlockSpec((tm, d), lambda i, j: (i, 0)),
        scratch_shapes=[pltpu.VMEM((tm, d), BF16), pltpu.VMEM((tm, d), F32)],
        compiler_params=_cparams(("parallel", "arbitrary")),
        name="dense_ffn",
    )(h, g.reshape(1, d).astype(F32), wg.astype(BF16), wu.astype(BF16), wd.astype(BF16))


def _router_kernel(h_ref, g_ref, wr_hi_ref, wr_lo_ref, xn_ref, info_ref, cnt_ref, carry_ref, *, tm):
    i = pl.program_id(0)

    @pl.when(i == 0)
    def _():
        carry_ref[...] = jnp.zeros_like(carry_ref)

    h = h_ref[...]
    xn = _rms(h, g_ref[...], h.shape[-1])
    xn_ref[...] = xn
    x_hi = xn.astype(BF16)
    x_lo = (xn - x_hi.astype(F32)).astype(BF16)
    logits = _dot(x_hi, wr_hi_ref[...]) + _dot(x_lo, wr_hi_ref[...]) + _dot(x_hi, wr_lo_ref[...])
    lane = lax.broadcasted_iota(jnp.int32, (tm, LANES), 1)
    lanef = lane.astype(F32)
    logits = jnp.where(lane < N_EXPERTS, logits, NEG_INF)
    m1 = jnp.max(logits, axis=-1, keepdims=True)
    i1 = jnp.min(jnp.where(logits == m1, lanef, float(LANES)), axis=-1, keepdims=True)
    rest = jnp.where(lanef == i1, NEG_INF, logits)
    m2 = jnp.max(rest, axis=-1, keepdims=True)
    i2 = jnp.min(jnp.where(rest == m2, lanef, float(LANES)), axis=-1, keepdims=True)
    e2 = jnp.exp(m2 - m1)
    w1 = 1.0 / (1.0 + e2)
    w2 = e2 / (1.0 + e2)
    oh1 = lanef == i1
    oh2 = lanef == i2
    oh = jnp.where(oh1 | oh2, 1.0, 0.0)
    rr = lax.broadcasted_iota(jnp.int32, (tm, tm), 0)
    cc = lax.broadcasted_iota(jnp.int32, (tm, tm), 1)
    tri = jnp.where(cc < rr, 1.0, 0.0).astype(BF16)
    before = _dot(tri, oh.astype(BF16)) + carry_ref[0:1, :]
    r1 = jnp.sum(jnp.where(oh1, before, 0.0), axis=-1, keepdims=True)
    r2 = jnp.sum(jnp.where(oh2, before, 0.0), axis=-1, keepdims=True)
    carry_ref[...] = carry_ref[...] + jnp.sum(oh, axis=0, keepdims=True)
    info = jnp.where(lane == 0, i1, jnp.where(lane == 1, i2, jnp.where(lane == 2, w1, jnp.where(
        lane == 3, w2, jnp.where(lane == 4, r1, jnp.where(lane == 5, r2, 0.0))))))
    info_ref[...] = info
    cnt_ref[...] = carry_ref[...]


def moe_router(h, g, router, *, tm):
    m, d = h.shape
    wr = jnp.pad(router.astype(F32), ((0, 0), (0, LANES - N_EXPERTS)))
    wr_hi = wr.astype(BF16)
    wr_lo = (wr - wr_hi.astype(F32)).astype(BF16)
    return pl.pallas_call(
        functools.partial(_router_kernel, tm=tm),
        out_shape=(jax.ShapeDtypeStruct((m, d), F32), jax.ShapeDtypeStruct((m, LANES), F32),
                   jax.ShapeDtypeStruct((8, LANES), F32)),
        grid=(m // tm,),
        in_specs=[pl.BlockSpec((tm, d), lambda i: (i, 0)),
                  pl.BlockSpec((1, d), lambda i: (0, 0)),
                  pl.BlockSpec((d, LANES), lambda i: (0, 0)),
                  pl.BlockSpec((d, LANES), lambda i: (0, 0))],
        out_specs=(pl.BlockSpec((tm, d), lambda i: (i, 0)),
                   pl.BlockSpec((tm, LANES), lambda i: (i, 0)),
                   pl.BlockSpec((8, LANES), lambda i: (0, 0))),
        scratch_shapes=[pltpu.VMEM((8, LANES), F32)],
        compiler_params=_cparams(("arbitrary",)),
        name="moe_router",
    )(h, g.reshape(1, d).astype(F32), wr_hi, wr_lo)


def _row_copy(src_hbm, row, dst, slot, sem):
    return pltpu.make_async_copy(src_hbm.at[pl.ds(row, 1), :], dst.at[pl.ds(slot, 1), :], sem)


def _rows_wait(src_hbm, dst, sem):
    pltpu.make_async_copy(src_hbm.at[pl.ds(0, dst.shape[0]), :], dst, sem).wait()


def _moe_ffn_kernel(src_ref, texp_ref, nact_ref, x_hbm, wg_ref, wu_ref, wd_ref, o_ref,
                    xbuf, xbf, acc_ref, sem, *, tm, nj):
    i = pl.program_id(0)
    j = pl.program_id(1)
    nact = nact_ref[0]
    active = i < nact
    nbuf = xbuf.shape[0]
    ahead = nbuf - 1
    cur = i % nbuf
    rows_per_step = tm // nj

    for t0 in range(ahead):
        @pl.when((i == 0) & (j == 0) & (t0 < nact))
        def _():
            def issue(s, c):
                _row_copy(x_hbm, src_ref[t0 * tm + s], xbuf.at[t0], s, sem.at[t0]).start()
                return c

            lax.fori_loop(0, tm, issue, 0, unroll=8)

    @pl.when(active & (j == 0))
    def _():
        _rows_wait(x_hbm, xbuf.at[cur], sem.at[cur])
        xbf[...] = xbuf[cur].astype(BF16)
        acc_ref[...] = jnp.zeros_like(acc_ref)

    def compute(prefetch):
        if prefetch:
            nxt = (i + ahead) % nbuf
            base = (i + ahead) * tm + j * rows_per_step
            for k in range(rows_per_step):
                _row_copy(x_hbm, src_ref[base + k], xbuf.at[nxt], j * rows_per_step + k, sem.at[nxt]).start()
        x = xbf[...]
        a = _silu(_dot(x, wg_ref[0])) * _dot(x, wu_ref[0])
        acc_ref[...] += _dot(a.astype(BF16), wd_ref[0])

    @pl.when(i + ahead < nact)
    def _():
        compute(True)

    @pl.when(active & (i + ahead >= nact))
    def _():
        compute(False)

    @pl.when(j == nj - 1)
    def _():
        o_ref[...] = jnp.where(active, acc_ref[...], 0.0)


def moe_expert_ffn(xn, src, tile_expert, n_active, wg, wu, wd, *, tm, tf):
    n_slots = src.shape[0]
    d = xn.shape[1]
    ne, _, ff = wg.shape
    nj = ff // tf
    assert nj * tf == ff and tm % nj == 0
    wg, wu, wd = wg.astype(BF16), wu.astype(BF16), wd.astype(BF16)

    def wmap_col(i, j, src, texp, nact):
        return (texp[i], 0, jnp.where(i < nact[0], j, nj - 1))

    def wmap_row(i, j, src, texp, nact):
        return (texp[i], jnp.where(i < nact[0], j, nj - 1), 0)

    return pl.pallas_call(
        functools.partial(_moe_ffn_kernel, tm=tm, nj=nj),
        out_shape=jax.ShapeDtypeStruct((n_slots, d), F32),
        grid_spec=pltpu.PrefetchScalarGridSpec(
            num_scalar_prefetch=3,
            grid=(n_slots // tm, nj),
            in_specs=[pl.BlockSpec(memory_space=pl.ANY),
                      pl.BlockSpec((1, d, tf), wmap_col),
                      pl.BlockSpec((1, d, tf), wmap_col),
                      pl.BlockSpec((1, tf, d), wmap_row)],
            out_specs=pl.BlockSpec((tm, d), lambda i, j, *_: (i, 0)),
            scratch_shapes=[pltpu.VMEM((3, tm, d), F32), pltpu.VMEM((tm, d), BF16),
                            pltpu.VMEM((tm, d), F32), pltpu.SemaphoreType.DMA((3,))]),
        compiler_params=_cparams(("arbitrary", "arbitrary")),
        name="moe_expert_ffn",
    )(src, tile_expert, n_active, xn, wg, wu, wd)


def _moe_combine_kernel(pos_ref, h_ref, info_ref, ys_hbm, g_ref, o_ref, buf, sem, *, tm, final_norm):
    i = pl.program_id(0)
    n = pl.num_programs(0)
    cur = i % 2

    def start_gather(tile, b):
        for s in range(tm):
            for k in range(2):
                _row_copy(ys_hbm, pos_ref[2 * (tile * tm + s) + k], buf.at[b, k], s, sem.at[b]).start()

    @pl.when(i == 0)
    def _():
        start_gather(0, 0)

    @pl.when(i + 1 < n)
    def _():
        start_gather(i + 1, 1 - cur)

    for k in range(2):
        _rows_wait(ys_hbm, buf.at[cur, k], sem.at[cur])
    info = info_ref[...]
    y = h_ref[...] + info[:, 2:3] * buf[cur, 0] + info[:, 3:4] * buf[cur, 1]
    if final_norm:
        y = _rms(y, g_ref[...], y.shape[-1])
    o_ref[...] = y


def moe_combine(h, info, ys, pos_flat, g_final, *, tm, final_norm):
    m, d = h.shape
    return pl.pallas_call(
        functools.partial(_moe_combine_kernel, tm=tm, final_norm=final_norm),
        out_shape=jax.ShapeDtypeStruct((m, d), F32),
        grid_spec=pltpu.PrefetchScalarGridSpec(
            num_scalar_prefetch=1,
            grid=(m // tm,),
            in_specs=[pl.BlockSpec((tm, d), lambda i, *_: (i, 0)),
                      pl.BlockSpec((tm, LANES), lambda i, *_: (i, 0)),
                      pl.BlockSpec(memory_space=pl.ANY),
                      pl.BlockSpec((1, d), lambda i, *_: (0, 0))],
            out_specs=pl.BlockSpec((tm, d), lambda i, *_: (i, 0)),
            scratch_shapes=[pltpu.VMEM((2, 2, tm, d), F32), pltpu.SemaphoreType.DMA((2,))]),
        compiler_params=_cparams(("arbitrary",)),
        name="moe_combine",
    )(pos_flat, h, info, ys, g_final.reshape(1, d).astype(F32))


def moe_layer(h, g, router, wg, wu, wd, g_final, *, final_norm, tm_r=512, tm_g=512, tf=1792, tm_c=256):
    m, d = h.shape
    xn, info, cnt = moe_router(h, g, router, tm=tm_r)
    e_idx = info[:, 0:2].astype(jnp.int32)
    rank = info[:, 4:6].astype(jnp.int32)
    counts = cnt[0, :N_EXPERTS].astype(jnp.int32)
    tiles_per = (counts + tm_g - 1) // tm_g
    tile_end = jnp.cumsum(tiles_per)
    seg_start = (tile_end - tiles_per) * tm_g
    pos = rank
    for e in range(N_EXPERTS):
        pos = pos + jnp.where(e_idx == e, seg_start[e], 0)
    n_tiles = (2 * m) // tm_g + N_EXPERTS
    n_slots = n_tiles * tm_g
    tok = jnp.broadcast_to(jnp.arange(m, dtype=jnp.int32)[:, None], (m, 2))
    src = jnp.zeros((n_slots,), jnp.int32).at[pos.reshape(-1)].set(tok.reshape(-1))
    n_active = tile_end[-1:].astype(jnp.int32)
    tile_ids = jnp.minimum(jnp.arange(n_tiles, dtype=jnp.int32), n_active[0] - 1)
    tile_expert = jnp.sum(tile_ids[:, None] >= tile_end[None, :], axis=1).astype(jnp.int32)
    ys = moe_expert_ffn(xn, src, tile_expert, n_active,
                        wg, wu, wd, tm=tm_g, tf=tf)
    return moe_combine(h, info, ys, pos.reshape(-1).astype(jnp.int32), g_final, tm=tm_c, final_norm=final_norm)


def _final_norm_kernel(h_ref, g_ref, o_ref):
    h = h_ref[...]
    o_ref[...] = _rms(h, g_ref[...], h.shape[-1])


def final_rmsnorm(h, g, *, tm):
    m, d = h.shape
    return pl.pallas_call(
        _final_norm_kernel,
        out_shape=jax.ShapeDtypeStruct((m, d), F32),
        grid=(m // tm,),
        in_specs=[pl.BlockSpec((tm, d), lambda i: (i, 0)), pl.BlockSpec((1, d), lambda i: (0, 0))],
        out_specs=pl.BlockSpec((tm, d), lambda i: (i, 0)),
        compiler_params=_cparams(("parallel",)),
        name="final_rmsnorm",
    )(h, g.reshape(1, d).astype(F32))


def _pack_w_in(w):
    d = w.shape[0]
    w = w.astype(F32)
    o = 0
    u = w[:, o:o + SSM_WIDTH]; o += SSM_WIDTH
    cq = w[:, o:o + MLA_Q_RANK]; o += MLA_Q_RANK
    ckv = w[:, o:o + MLA_KV_RANK]; o += MLA_KV_RANK
    kr = w[:, o:o + MLA_ROPE]; o += MLA_ROPE
    nq = w[:, o:o + NSA_HEADS * NSA_DIM]; o += NSA_HEADS * NSA_DIM
    nkv = w[:, o:o + 6 * NSA_KV_HEADS * NSA_DIM]; o += 6 * NSA_KV_HEADS * NSA_DIM
    gate = w[:, o:o + 3 * NSA_HEADS]
    z = lambda n: jnp.zeros((d, n), F32)
    kr_a = jnp.concatenate([z(MLA_NOPE), kr, z(LANES - MLA_NOPE - MLA_ROPE)], axis=1)
    kr_b = jnp.concatenate([z(MLA_NOPE), _rot_half_cols(kr), z(LANES - MLA_NOPE - MLA_ROPE)], axis=1)
    nq_h = (nq * NSA_DIM ** -0.5).reshape(d, NSA_KV_HEADS, NSA_REP, NSA_DIM)
    zq = jnp.zeros((d, NSA_REP, NSA_DIM), F32)
    nq_p = jnp.concatenate([
        jnp.concatenate([nq_h[:, 0], zq], axis=-1).reshape(d, NSA_REP * LANES),
        jnp.concatenate([zq, nq_h[:, 1]], axis=-1).reshape(d, NSA_REP * LANES)], axis=1)
    packed = jnp.concatenate([u, cq, z(256 - MLA_Q_RANK), kr_a, kr_b, nq_p, nkv, ckv,
                              gate, z(LANES - 3 * NSA_HEADS)], axis=1)
    assert packed.shape[1] == IN_COLS_PACKED
    return packed.astype(BF16)


def _rg_order(a):
    rest = a.shape[1:]
    return a.reshape((NSA_KV_HEADS, NSA_REP, NSA_DIM) + rest).swapaxes(0, 1).reshape((-1,) + rest)


def kernel(x, mem, w_in, w_out, mix_norm, out_norm, ssm_a_re, ssm_a_im, ssm_b_re, ssm_b_im, ssm_c_re, ssm_c_im, ssm_d, ssm_log_dt, ssm_w_glu, mla_q_norm, mla_w_uq, mla_kv_norm, mla_w_ukv, nsa_cmp_pe, nsa_cmp_w1, nsa_cmp_w2, rel_bias, xattn_norm, mem_norm, xattn_wq, xattn_wkv, xattn_wo, ffn_norm, dense_w_gate, dense_w_up, dense_w_down, moe_router, moe_w_gate, moe_w_up, moe_w_down, final_norm):
    bsz, seq, d = x.shape
    depth = w_in.shape[0]
    T = bsz * seq
    nmem = mem.shape[1]
    tq_nsa, tk_nsa = 2 * LANES, 4 * LANES
    rope_tabs = _rope_tables(seq)
    nsa_tabs = _nsa_bias_tables(rel_bias, seq, tq_nsa)
    o1, o2 = SSM_WIDTH, SSM_WIDTH + MLA_HEADS * MLA_V
    mem2 = mem.reshape(bsz * nmem, d)
    h = x.reshape(T, d)
    for l in range(depth):
        proj = norm_matmul(h, mix_norm[l], _pack_w_in(w_in[l]), tm=512, tn=IN_COLS_PACKED, out_dtype=BF16)
        proj3 = proj.reshape(bsz, seq, IN_COLS_PACKED)
        u_tm = proj3[:, :, C_U:C_U + SSM_WIDTH].transpose(1, 0, 2).reshape(seq * bsz, SSM_WIDTH)
        y_ssm = ssm_mixer(u_tm, ssm_a_re[l], ssm_a_im[l], ssm_b_re[l], ssm_b_im[l], ssm_c_re[l], ssm_c_im[l],
                          ssm_d[l], ssm_log_dt[l], ssm_w_glu[l], nb=bsz, tc=64)
        y_ssm = y_ssm.reshape(seq, bsz, SSM_WIDTH).transpose(1, 0, 2).reshape(T, SSM_WIDTH)
        y_mla = mla_mixer(proj3, mla_q_norm[l], mla_w_uq[l], mla_kv_norm[l], mla_w_ukv[l], rope_tabs,
                          tm=512, tq=512).reshape(T, -1)
        nch = seq // CMP_STRIDE
        kv_cr = jnp.stack([proj3[:, :, C_NKV:C_NKV + LANES].reshape(bsz, nch, CMP_STRIDE * LANES),
                           proj3[:, :, C_NKV + LANES:C_NKV + 2 * LANES].reshape(bsz, nch, CMP_STRIDE * LANES)],
                          axis=1)
        kvc = nsa_compress(kv_cr, nsa_cmp_pe[l], nsa_cmp_w1[l], nsa_cmp_w2[l])
        y_nsa = nsa_mixer(proj3, kvc, nsa_tabs, tq=tq_nsa, tk=tk_nsa).reshape(T, -1)
        g_out = out_norm[l]
        wo_l = w_out[l]
        h = out_projection(y_ssm, y_mla, y_nsa, h,
                           [g_out[:o1], g_out[o1:o2], _rg_order(g_out[o2:])],
                           [wo_l[:o1].astype(BF16), wo_l[o1:o2].astype(BF16), _rg_order(wo_l[o2:]).astype(BF16)],
                           tm=512)
        kv_mem = norm_matmul(mem2, mem_norm[l], xattn_wkv[l].astype(BF16), tm=256, tn=512, out_dtype=BF16)
        h = cross_attention(h.reshape(bsz, seq, d), kv_mem.reshape(bsz, nmem, 2 * d), xattn_norm[l],
                            xattn_wq[l], xattn_wo[l], tm=256).reshape(T, d)
        last = l == depth - 1
        if l % 2 == 0:
            h = dense_ffn(h, ffn_norm[l], dense_w_gate[l // 2], dense_w_up[l // 2], dense_w_down[l // 2],
                          tm=512, tf=1408)
            if last:
                h = final_rmsnorm(h, final_norm, tm=512)
        else:
            h = moe_layer(h, ffn_norm[l], moe_router[l // 2], moe_w_gate[l // 2], moe_w_up[l // 2],
                          moe_w_down[l // 2], final_norm, final_norm=last)
    return h.reshape(bsz, seq, d)
```

```python
import functools
import math

import jax
import jax.numpy as jnp
from jax import lax
from jax.experimental import pallas as pl
from jax.experimental.pallas import tpu as pltpu

F32 = jnp.float32
BF16 = jnp.bfloat16

HEAD_DIM = 64
SSM_WIDTH = 256
SSM_CH = 16
SSM_GROUPS = 16
SSM_STATE = 64
MLA_HEADS = 6
MLA_NOPE = 64
MLA_ROPE = 32
MLA_V = 64
MLA_Q_RANK = 192
MLA_KV_RANK = 128
NSA_HEADS = 6
NSA_KV_HEADS = 2
NSA_REP = 3
NSA_DIM = 64
CMP_BLOCK = 32
CMP_STRIDE = 16
SEL_BLOCK = 64
SEL_TOPN = 8
WINDOW = 256
REL_BUCKETS = 32
REL_MAX_DIST = 128
XATTN_HEADS = 4
N_EXPERTS = 8
ROPE_THETA = 10000.0
EPS = 1e-6
NEG_INF = -1e30
FORCE = 1e9
LOG2E = math.log2(math.e)

LANES = 128
VMEM_LIMIT = 56 * 1024 * 1024

C_U, C_CQ, C_KR, C_NQ, C_NKV, C_CKV, C_GATE = 0, 256, 512, 768, 1536, 2304, 2432
IN_COLS_PACKED = 2560


def _cparams(sem):
    return pltpu.CompilerParams(dimension_semantics=sem, vmem_limit_bytes=VMEM_LIMIT)


def _dot(a, b):
    return jnp.dot(a, b, preferred_element_type=F32)


def _dot_t(a, b):
    return lax.dot_general(a, b, (((1,), (1,)), ((), ())), preferred_element_type=F32)


def _rms(x, g, n):
    ms = jnp.sum(x * x, axis=-1, keepdims=True) * (1.0 / n)
    return x * lax.rsqrt(ms + EPS) * g


def _sigmoid(x):
    return 1.0 / (1.0 + jnp.exp(-x))


def _silu(x):
    return x * _sigmoid(x)


def _norm_mm_kernel(x_ref, g_ref, w_ref, o_ref, xn_ref):
    @pl.when(pl.program_id(1) == 0)
    def _():
        x = x_ref[...].astype(F32)
        xn_ref[...] = _rms(x, g_ref[...], x.shape[-1]).astype(BF16)

    o_ref[...] = _dot(xn_ref[...], w_ref[...]).astype(o_ref.dtype)


def norm_matmul(x, g, w, *, tm, tn, out_dtype):
    m, k = x.shape
    n = w.shape[1]
    return pl.pallas_call(
        _norm_mm_kernel,
        out_shape=jax.ShapeDtypeStruct((m, n), out_dtype),
        grid=(m // tm, n // tn),
        in_specs=[pl.BlockSpec((tm, k), lambda i, j: (i, 0)),
                  pl.BlockSpec((1, k), lambda i, j: (0, 0)),
                  pl.BlockSpec((k, tn), lambda i, j: (0, j))],
        out_specs=pl.BlockSpec((tm, tn), lambda i, j: (i, j)),
        scratch_shapes=[pltpu.VMEM((tm, k), BF16)],
        compiler_params=_cparams(("parallel", "arbitrary")),
        name="norm_matmul",
    )(x, g.reshape(1, k).astype(F32), w)


def _ssm_kernel(u_ref, bbr_ref, bbi_ref, ar_ref, ai_ref, ccr_ref, cci_ref, d_ref, wglu_ref,
                o_ref, hr_ref, hi_ref, cr_ref, ci_ref, *, tc, nb):
    @pl.when(pl.program_id(0) == 0)
    def _():
        cr_ref[...] = jnp.zeros_like(cr_ref)
        ci_ref[...] = jnp.zeros_like(ci_ref)

    u = u_ref[...]
    hr_ref[...] = _dot(u, bbr_ref[...])
    hi_ref[...] = _dot(u, bbi_ref[...])
    gp = ar_ref.shape[-1]
    ar = jnp.broadcast_to(ar_ref[...], (nb, gp))
    ai = jnp.broadcast_to(ai_ref[...], (nb, gp))

    def step(t, carry):
        hr, hi = carry
        rows = pl.ds(pl.multiple_of(t * nb, nb), nb)
        nr = ar * hr - ai * hi + hr_ref[rows, :]
        ni = ar * hi + ai * hr + hi_ref[rows, :]
        hr_ref[rows, :] = nr
        hi_ref[rows, :] = ni
        return nr, ni

    hr, hi = lax.fori_loop(0, tc, step, (cr_ref[...], ci_ref[...]))
    cr_ref[...] = hr
    ci_ref[...] = hi
    y = (_dot(hr_ref[...].astype(BF16), ccr_ref[...]) + _dot(hi_ref[...].astype(BF16), cci_ref[...])
         + d_ref[...] * u.astype(F32))
    y = jax.nn.gelu(y)
    z = _dot(y.astype(BF16), wglu_ref[...])
    o_ref[...] = (y * _sigmoid(z)).astype(o_ref.dtype)


def ssm_mixer(u_tm, a_re, a_im, b_re, b_im, c_re, c_im, d, log_dt, w_glu, *, nb, tc):
    rows = u_tm.shape[0]
    G, P, C = SSM_GROUPS, SSM_STATE, SSM_CH
    dt = jnp.exp(log_dt.astype(F32))[:, None]
    lr, li = a_re.astype(F32), a_im.astype(F32)
    mag = jnp.exp(lr * dt)
    ab_r, ab_i = mag * jnp.cos(li * dt), mag * jnp.sin(li * dt)
    den = lr * lr + li * li
    nr = ab_r - 1.0
    f_r = (nr * lr + ab_i * li) / den
    f_i = (ab_i * lr - nr * li) / den
    br, bi = b_re.astype(F32), b_im.astype(F32)
    bb_r = f_r[..., None] * br - f_i[..., None] * bi
    bb_i = f_r[..., None] * bi + f_i[..., None] * br
    eye = jnp.eye(G, dtype=F32)
    bbr = jnp.einsum('gpc,gh->gchp', bb_r, eye).reshape(G * C, G * P).astype(BF16)
    bbi = jnp.einsum('gpc,gh->gchp', bb_i, eye).reshape(G * C, G * P).astype(BF16)
    ccr = jnp.einsum('gcp,gh->gphc', c_re.astype(F32), eye).reshape(G * P, G * C).astype(BF16)
    cci = jnp.einsum('gcp,gh->gphc', -c_im.astype(F32), eye).reshape(G * P, G * C).astype(BF16)
    gp = G * P
    full = lambda shape: pl.BlockSpec(shape, lambda i: (0,) * len(shape))
    return pl.pallas_call(
        functools.partial(_ssm_kernel, tc=tc, nb=nb),
        out_shape=jax.ShapeDtypeStruct((rows, SSM_WIDTH), BF16),
        grid=(rows // (tc * nb),),
        in_specs=[pl.BlockSpec((tc * nb, SSM_WIDTH), lambda i: (i, 0)),
                  full((G * C, gp)), full((G * C, gp)), full((1, gp)), full((1, gp)),
                  full((gp, G * C)), full((gp, G * C)), full((1, SSM_WIDTH)),
                  full((SSM_WIDTH, SSM_WIDTH))],
        out_specs=pl.BlockSpec((tc * nb, SSM_WIDTH), lambda i: (i, 0)),
        scratch_shapes=[pltpu.VMEM((tc * nb, gp), F32), pltpu.VMEM((tc * nb, gp), F32),
                        pltpu.VMEM((nb, gp), F32), pltpu.VMEM((nb, gp), F32)],
        compiler_params=_cparams(("arbitrary",)),
        name="ssm_mixer",
    )(u_tm, bbr, bbi, ab_r.reshape(1, gp), ab_i.reshape(1, gp), ccr, cci,
      d.reshape(1, SSM_WIDTH).astype(F32), w_glu.astype(BF16))


def _mla_prep_kernel(cq_ref, kr_ref, ckv_ref, gq_ref, gkv_ref, wqa_ref, wqb_ref, wk_ref, wv_ref,
                     c1_ref, c0_ref, s0_ref, q_ref, k_ref, v_ref):
    qn = _rms(cq_ref[0].astype(F32), gq_ref[...], MLA_Q_RANK).astype(BF16)
    qa = _dot(qn, wqa_ref[...])
    qb = _dot(qn, wqb_ref[...])
    kn = _rms(ckv_ref[0].astype(F32), gkv_ref[...], MLA_KV_RANK).astype(BF16)
    ka = _dot(kn, wk_ref[...])
    va = _dot(kn, wv_ref[...])
    kr = kr_ref[0].astype(F32)
    c1, c0, s0 = c1_ref[...], c0_ref[...], s0_ref[...]
    krope = kr[:, :LANES] * c0 + kr[:, LANES:] * s0
    low_half = lax.broadcasted_iota(jnp.int32, (kr.shape[0], LANES), 1) < MLA_V
    for h in range(MLA_HEADS):
        sl = slice(h * LANES, (h + 1) * LANES)
        q_ref[0, h] = (qa[:, sl] * c1 + qb[:, sl] * s0).astype(BF16)
        k_ref[0, h] = (ka[:, sl] + krope).astype(BF16)
        ones = jnp.where(low_half if h % 2 == 0 else jnp.logical_not(low_half), 1.0, 0.0)
        v_ref[0, h] = jnp.concatenate([va[:, sl], ones], axis=1).astype(BF16)


def _mla_flash_kernel(q_ref, k_ref, v_ref, o_ref, m_ref, acc_ref, *, tq):
    qi = pl.program_id(1)
    m_ref[...] = jnp.full_like(m_ref, NEG_INF)
    acc_ref[...] = jnp.zeros_like(acc_ref)
    lane = lax.broadcasted_iota(jnp.int32, (tq, LANES), 1)
    rep = tq // LANES

    def tile(kt, masked):
        ks = pl.ds(pl.multiple_of(kt * tq, tq), tq)
        if masked:
            mask = (lax.broadcasted_iota(jnp.int32, (tq, tq), 1)
                    <= lax.broadcasted_iota(jnp.int32, (tq, tq), 0))
        for pr in range(MLA_HEADS // 2):
            hs = (2 * pr, 2 * pr + 1)
            s = [_dot_t(q_ref[0, h], k_ref[0, h, ks, :]) for h in hs]
            if masked:
                s = [jnp.where(mask, x, NEG_INF) for x in s]
            m_prev = [m_ref[h] for h in hs]
            m_new = [jnp.maximum(mp, jnp.max(x, axis=-1, keepdims=True)) for mp, x in zip(m_prev, s)]
            alpha = [jnp.exp2(mp - mn) for mp, mn in zip(m_prev, m_new)]
            p = [jnp.exp2(x - jnp.tile(mn, (1, rep))) for x, mn in zip(s, m_new)]
            for h, mn in zip(hs, m_new):
                m_ref[h] = mn
            pv = _dot(p[0].astype(BF16), v_ref[0, hs[0], ks, :]) + _dot(p[1].astype(BF16), v_ref[0, hs[1], ks, :])
            acc_ref[pr] = acc_ref[pr] * jnp.tile(jnp.where(lane < MLA_V, alpha[0], alpha[1]), (1, 2)) + pv

    def body(kt, c):
        tile(kt, False)
        return c

    lax.fori_loop(0, qi, body, 0)
    tile(qi, True)
    for pr in range(MLA_HEADS // 2):
        o_ref[0, :, pr * LANES:(pr + 1) * LANES] = (acc_ref[pr, :, :LANES] / acc_ref[pr, :, LANES:]).astype(o_ref.dtype)


def _rope_tables(seq):
    pos = jnp.arange(seq, dtype=F32)
    inv = 1.0 / (ROPE_THETA ** (jnp.arange(0, MLA_ROPE, 2, dtype=F32) / MLA_ROPE))
    ang = pos[:, None] * inv[None, :]
    cos, sin = jnp.cos(ang), jnp.sin(ang)
    cos2 = jnp.concatenate([cos, cos], axis=-1)
    sin2 = jnp.concatenate([sin, sin], axis=-1)
    z64 = jnp.zeros((seq, MLA_NOPE), F32)
    z32 = jnp.zeros((seq, LANES - MLA_NOPE - MLA_ROPE), F32)
    c1 = jnp.concatenate([jnp.ones((seq, MLA_NOPE), F32), cos2, z32], axis=-1)
    c0 = jnp.concatenate([z64, cos2, z32], axis=-1)
    s0 = jnp.concatenate([z64, sin2, z32], axis=-1)
    return c1, c0, s0


def _rot_half_cols(w):
    half = MLA_ROPE // 2
    return jnp.concatenate([-w[..., half:], w[..., :half]], axis=-1)


def mla_mixer(proj3, q_norm, w_uq, kv_norm, w_ukv, tabs, *, tm, tq):
    bsz, seq, _ = proj3.shape
    H = MLA_HEADS
    scale = (MLA_NOPE + MLA_ROPE) ** -0.5 * LOG2E
    wq =(w_uq.astype(F32) * scale).reshape(MLA_Q_RANK, H, MLA_NOPE + MLA_ROPE)
    zq = jnp.zeros((MLA_Q_RANK, H, LANES - MLA_NOPE - MLA_ROPE), F32)
    z64 = jnp.zeros((MLA_Q_RANK, H, MLA_NOPE), F32)
    wqa = jnp.concatenate([wq, zq], axis=-1).reshape(MLA_Q_RANK, H * LANES)
    wqb = jnp.concatenate([z64, _rot_half_cols(wq[..., MLA_NOPE:]), zq], axis=-1).reshape(MLA_Q_RANK, H * LANES)
    padq = ((0, 256 - MLA_Q_RANK), (0, 0))
    wqa = jnp.pad(wqa, padq).astype(BF16)
    wqb = jnp.pad(wqb, padq).astype(BF16)
    gq = jnp.pad(q_norm.astype(F32), (0, 256 - MLA_Q_RANK)).reshape(1, 256)
    wkv = w_ukv.astype(F32).reshape(MLA_KV_RANK, H, MLA_NOPE + MLA_V)
    zk = jnp.zeros((MLA_KV_RANK, H, MLA_NOPE), F32)
    wk = jnp.concatenate([wkv[..., :MLA_NOPE], zk], axis=-1).reshape(MLA_KV_RANK, H * LANES).astype(BF16)
    wv_h = wkv[..., MLA_NOPE:]
    even = (jnp.arange(H) % 2 == 0)[None, :, None]
    wv = jnp.concatenate([jnp.where(even, wv_h, 0.0), jnp.where(even, 0.0, wv_h)], axis=-1)
    wv = wv.reshape(MLA_KV_RANK, H * LANES).astype(BF16)
    c1, c0, s0 = tabs
    full2 = lambda shape: pl.BlockSpec(shape, lambda b, i: (0,) * len(shape))
    tab_spec = pl.BlockSpec((tm, LANES), lambda b, i: (i, 0))
    hd_spec = pl.BlockSpec((1, H, tm, LANES), lambda b, i: (b, 0, i, 0))
    hd_shape = jax.ShapeDtypeStruct((bsz, H, seq, LANES), BF16)
    v_spec = pl.BlockSpec((1, H, tm, 2 * LANES), lambda b, i: (b, 0, i, 0))
    v_shape = jax.ShapeDtypeStruct((bsz, H, seq, 2 * LANES), BF16)
    q, k, v = pl.pallas_call(
        _mla_prep_kernel,
        out_shape=(hd_shape, hd_shape, v_shape),
        grid=(bsz, seq // tm),
        in_specs=[pl.BlockSpec((1, tm, 256), lambda b, i: (b, i, C_CQ // 256)),
                  pl.BlockSpec((1, tm, 256), lambda b, i: (b, i, C_KR // 256)),
                  pl.BlockSpec((1, tm, 128), lambda b, i: (b, i, C_CKV // 128)),
                  full2((1, 256)), full2((1, 128)),
                  full2((256, H * LANES)), full2((256, H * LANES)),
                  full2((128, H * LANES)), full2((128, H * LANES)),
                  tab_spec, tab_spec, tab_spec],
        out_specs=(hd_spec, hd_spec, v_spec),
        compiler_params=_cparams(("parallel", "parallel")),
        name="mla_prep",
    )(proj3, proj3, proj3, gq, kv_norm.astype(F32).reshape(1, 128), wqa, wqb, wk, wv, c1, c0, s0)

    return pl.pallas_call(
        functools.partial(_mla_flash_kernel, tq=tq),
        out_shape=jax.ShapeDtypeStruct((bsz, seq, H * MLA_V), BF16),
        grid=(bsz, seq // tq),
        in_specs=[pl.BlockSpec((1, H, tq, LANES), lambda b, i: (b, 0, i, 0)),
                  pl.BlockSpec((1, H, seq, LANES), lambda b, i: (b, 0, 0, 0)),
                  pl.BlockSpec((1, H, seq, 2 * LANES), lambda b, i: (b, 0, 0, 0))],
        out_specs=pl.BlockSpec((1, tq, H * MLA_V), lambda b, i: (b, i, 0)),
        scratch_shapes=[pltpu.VMEM((H, tq, LANES), F32), pltpu.VMEM((H // 2, tq, 2 * LANES), F32)],
        compiler_params=_cparams(("parallel", "arbitrary")),
        name="mla_flash",
    )(q, k, v)


def _nsa_cmp_kernel(x_ref, pea_ref, peb_ref, w1a_ref, w1b_ref, w2_ref, o_ref):
    x = x_ref[0, 0]
    w1a, w1b = w1a_ref[0], w1b_ref[0]
    bias = _dot(pea_ref[0], w1a)[0:1] + _dot(peb_ref[0], w1b)[0:1]
    a = _dot(x, w1a)
    b = _dot(x, w1b)
    n = b.shape[0]
    pre = a + pltpu.roll(b, n - 1, 0) + bias
    o_ref[0, 0] = _dot(jax.nn.gelu(pre).astype(BF16), w2_ref[0]).astype(o_ref.dtype)


def nsa_compress(kv_cr, cmp_pe, cmp_w1, cmp_w2):
    bsz, _, nch, width = kv_cr.shape
    G, dh = NSA_KV_HEADS, NSA_DIM
    half = CMP_BLOCK // 2
    eye = jnp.eye(G, dtype=F32)
    w1r = cmp_w1.astype(F32).reshape(2, CMP_BLOCK, dh, dh)
    w1a = jnp.einsum('kpde,gh->kpgdhe', w1r[:, :half], eye).reshape(2, width, G * dh).astype(BF16)
    w1b = jnp.einsum('kpde,gh->kpgdhe', w1r[:, half:], eye).reshape(2, width, G * dh).astype(BF16)
    w2 = jnp.einsum('kde,gh->kgdhe', cmp_w2.astype(F32), eye).reshape(2, G * dh, G * dh).astype(BF16)
    pe = cmp_pe.astype(F32)
    pe_g = jnp.broadcast_to(pe[:, :, None, :], (2, CMP_BLOCK, G, dh))
    pea = jnp.broadcast_to(pe_g[:, :half].reshape(2, 1, width), (2, 8, width)).astype(BF16)
    peb = jnp.broadcast_to(pe_g[:, half:].reshape(2, 1, width), (2, 8, width)).astype(BF16)
    kvspec = lambda shape: pl.BlockSpec(shape, lambda b, k: (k,) + (0,) * (len(shape) - 1))
    return pl.pallas_call(
        _nsa_cmp_kernel,
        out_shape=jax.ShapeDtypeStruct((bsz, 2, nch, G * dh), BF16),
        grid=(bsz, 2),
        in_specs=[pl.BlockSpec((1, 1, nch, width), lambda b, k: (b, k, 0, 0)),
                  kvspec((1, 8, width)), kvspec((1, 8, width)),
                  kvspec((1, width, G * dh)), kvspec((1, width, G * dh)),
                  kvspec((1, G * dh, G * dh))],
        out_specs=pl.BlockSpec((1, 1, nch, G * dh), lambda b, k: (b, k, 0, 0)),
        compiler_params=_cparams(("parallel", "parallel")),
        name="nsa_compress",
    )(kv_cr, pea, peb, w1a, w1b, w2)


def _nsa_kernel(q_ref, gate_ref, ksl_ref, vsl_ref, kwn_ref, vwn_ref, kvc_ref, bc_ref, bw_ref,
                bs_ref, ov_ref, blk1h_ref, gexp_ref, o_ref, kaug_ref, vaug_ref, m_ref, acc_ref,
                *, tq, tk, nbs, n_sel):
    qi = pl.program_id(1)
    R, G = NSA_REP, NSA_KV_HEADS
    H = R * G
    nsub = tk // tq
    nwin = WINDOW // tq

    @pl.when(qi == 0)
    def _():
        kaug_ref[:, :LANES] = ksl_ref[0]
        kaug_ref[:, LANES:] = blk1h_ref[...]
        vaug_ref[:, :LANES] = vsl_ref[0]
        vaug_ref[:, LANES:] = jnp.ones((vaug_ref.shape[0], LANES), BF16)

    lane = lax.broadcasted_iota(jnp.int32, (tq, LANES), 1)
    t_row = qi * tq + lax.broadcasted_iota(jnp.int32, (H * tq, 1), 0) % tq
    kc = kvc_ref[0, 0]
    vc = kvc_ref[0, 1]
    ov = ov_ref[...]

    def stack(fn):
        return jnp.concatenate([fn(h) for h in range(H)], axis=0)

    q_all = stack(lambda h: q_ref[0, :, h * LANES:(h + 1) * LANES])

    c = jnp.minimum(qi, nwin)
    ws = pl.ds(pl.multiple_of(jnp.maximum(qi - nwin, 0) * tq, tq), (nwin + 1) * tq)
    k_w = kwn_ref[0, ws, :]
    v_w = vwn_ref[0, ws, :]
    o_w = []
    for g in range(G):
        rows = slice(g * R * tq, (g + 1) * R * tq)
        s_w = _dot_t(q_all[rows], k_w) + jnp.concatenate([bw_ref[g * R + r, c] for r in range(R)], axis=0)
        p_w = jnp.exp2(s_w - jnp.max(s_w, axis=-1, keepdims=True))
        o_w.append(_dot(p_w.astype(BF16), v_w) / jnp.sum(p_w, axis=-1, keepdims=True))
    o_w = jnp.concatenate(o_w, axis=0)

    valid = t_row >= (CMP_BLOCK - 1)
    s = _dot_t(q_all, kc) + stack(lambda h: bc_ref[h])
    m = jnp.max(s, axis=-1, keepdims=True)
    p = jnp.where(valid, jnp.exp2(s - m), 0.0)
    l = jnp.where(valid, jnp.sum(p, axis=-1, keepdims=True), 1.0)
    pc = p / l
    o_c = _dot(pc.astype(BF16), vc)
    blk = lax.broadcasted_iota(jnp.int32, (nbs, tq), 0)
    tl = qi * tq + lax.broadcasted_iota(jnp.int32, (nbs, tq), 1)
    cur = tl // SEL_BLOCK
    forced = (blk == 0) | (blk == cur) | (blk == cur - 1)
    future = blk * SEL_BLOCK > tl
    qmask = []
    for g in range(G):
        b0 = g * R * tq
        psum = pc[b0:b0 + tq] + pc[b0 + tq:b0 + 2 * tq] + pc[b0 + 2 * tq:b0 + 3 * tq]
        p_hi = psum.astype(BF16)
        p_lo = (psum - p_hi.astype(F32)).astype(BF16)
        imp = (_dot_t(ov, p_hi) + _dot_t(ov, p_lo))[:nbs]
        imp = jnp.where(forced, FORCE, jnp.where(future, -FORCE, imp))
        rank = jnp.zeros((nbs, tq), F32)
        for i in range(nbs):
            ri = imp[i:i + 1, :]
            beats = (ri > imp) | ((ri == imp) & (blk > i))
            rank = rank + jnp.where(beats, 1.0, 0.0)
        sel = jnp.where(rank < n_sel, 0.0, NEG_INF)
        sel = jnp.concatenate([sel, jnp.zeros((LANES - nbs, tq), F32)], axis=0).T.astype(BF16)
        qmask += [sel] * R
    q_aug = jnp.concatenate([q_all, jnp.concatenate(qmask, axis=0)], axis=1)

    m_ref[...] = jnp.full_like(m_ref, NEG_INF)
    acc_ref[...] = jnp.zeros_like(acc_ref)

    halves = [slice(g * R * tq, (g + 1) * R * tq) for g in range(G)]

    def sel_tile(kt, bias):
        ks = pl.ds(pl.multiple_of(kt * tk, tk), tk)
        k_t = kaug_ref[ks, :]
        v_t = vaug_ref[ks, :]
        s = [_dot_t(q_aug[hs], k_t) for hs in halves]
        if bias is not None:
            s = [x + bias[hs] for x, hs in zip(s, halves)]
        m_prev = [m_ref[hs, :] for hs in halves]
        m_new = [jnp.maximum(mp, jnp.max(x, axis=-1, keepdims=True)) for mp, x in zip(m_prev, s)]
        alpha = [jnp.exp2(mp - mn) for mp, mn in zip(m_prev, m_new)]
        p = [jnp.exp2(x - jnp.tile(mn, (1, tk // LANES))).astype(BF16) for x, mn in zip(s, m_new)]
        pv = [_dot(pp, v_t) for pp in p]
        for hs, mn, a, o in zip(halves, m_new, alpha, pv):
            m_ref[hs, :] = mn
            acc_ref[hs, :] = acc_ref[hs, :] * jnp.tile(a, (1, 2)) + o

    def near_bias(kt):
        cols = []
        for sub in range(nsub):
            d = qi - (kt * nsub + sub)
            cols.append(stack(lambda h: jnp.where(d == 0, bs_ref[h, 0], jnp.where(
                d == 1, bs_ref[h, 1], jnp.where(d < 0, NEG_INF, 0.0)))))
        return jnp.concatenate(cols, axis=1)

    def far_body(kt, c):
        sel_tile(kt, None)
        return c

    kd = (qi * tq) // tk
    lax.fori_loop(0, jnp.maximum(kd - 1, 0), far_body, 0)

    @pl.when(kd >= 1)
    def _():
        sel_tile(kd - 1, near_bias(kd - 1))

    sel_tile(kd, near_bias(kd))
    o_s = acc_ref[:, :LANES] / acc_ref[:, LANES:]

    gates = _sigmoid(_dot(gate_ref[0], gexp_ref[...]))
    for r in range(R):
        res = None
        for b, o_b in enumerate((o_c, o_s, o_w)):
            o_br = jnp.where(lane < NSA_DIM, o_b[r * tq:(r + 1) * tq], o_b[(R + r) * tq:(R + r + 1) * tq])
            term = gates[:, (b * R + r) * LANES:(b * R + r + 1) * LANES] * o_br
            res = term if res is None else res + term
        o_ref[0, :, r * LANES:(r + 1) * LANES] = res.astype(o_ref.dtype)


def _t5_bucket(dist):
    n = jnp.maximum(dist, 0)
    exact = REL_BUCKETS // 2
    nf = jnp.maximum(n, exact).astype(F32)
    large = exact + jnp.floor(jnp.log(nf / exact) / math.log(REL_MAX_DIST / exact)
                              * (REL_BUCKETS - exact)).astype(jnp.int32)
    return jnp.where(n < exact, n, jnp.minimum(large, REL_BUCKETS - 1))


def _nsa_bias_tables(rel_bias, seq, tq):
    rb = rel_bias.astype(F32).T
    far = rb[:, REL_BUCKETS - 1].reshape(NSA_HEADS, 1, 1)

    def by_dist(dist, ok, shift=0.0):
        bucket = _t5_bucket(dist)[None]
        out = jnp.zeros((NSA_HEADS,) + dist.shape, F32)
        for k in range(REL_BUCKETS):
            out = jnp.where(bucket == k, rb[:, k].reshape((NSA_HEADS,) + (1,) * dist.ndim), out)
        return jnp.where(ok[None], (out - shift) * LOG2E, NEG_INF)

    i = jnp.arange(tq)[:, None]
    t = jnp.arange(seq)[:, None]
    dist_c = t - (jnp.arange(LANES)[None, :] * CMP_STRIDE + CMP_BLOCK - 1)
    bc = by_dist(dist_c, dist_c >= 0)
    nwin = WINDOW // tq
    jw = jnp.arange((nwin + 1) * tq)[None, :]
    bw = jnp.stack([by_dist(tq * c + i - jw, (tq * c + i - jw >= 0) & (tq * c + i - jw < WINDOW))
                    for c in range(nwin + 1)], axis=1)
    js = jnp.arange(tq)[None, :]
    bs = jnp.stack([by_dist(tq * c + i - js, tq * c + i - js >= 0, far) for c in range(2)], axis=1)
    ci = jnp.arange(LANES)[:, None]
    sj = jnp.arange(LANES)[None, :]
    nbs = seq // SEL_BLOCK
    ov = ((ci * CMP_STRIDE <= sj * SEL_BLOCK + SEL_BLOCK - 1)
          & (ci * CMP_STRIDE + CMP_BLOCK - 1 >= sj * SEL_BLOCK)
          & (ci < seq // CMP_STRIDE - 1) & (sj < nbs))
    blk1h = (jnp.arange(seq)[:, None] // SEL_BLOCK == sj).astype(BF16)
    col = jnp.arange(3 * NSA_REP * LANES)
    slab, lane_g = col // LANES, (col % LANES) // NSA_DIM
    gate_col = (slab // NSA_REP) * NSA_HEADS + lane_g * NSA_REP + slab % NSA_REP
    gexp = (ci == gate_col[None, :]).astype(BF16)
    return bc, bw, bs, ov.T.astype(BF16), blk1h, gexp


def nsa_mixer(proj3, kvc, tables, *, tq, tk):
    bsz, seq, _ = proj3.shape
    assert tq % LANES == 0 and tq >= REL_MAX_DIST and tk % tq == 0 and seq % tk == 0
    assert seq // CMP_STRIDE == LANES and WINDOW % tq == 0
    nwin = WINDOW // tq
    bc, bw, bs, ov, blk1h, gexp = tables
    nbs = seq // SEL_BLOCK
    H = NSA_HEADS
    slab = lambda j: pl.BlockSpec((1, seq, LANES), lambda b, i: (b, 0, C_NKV // LANES + j))
    const = lambda shape: pl.BlockSpec(shape, lambda b, i: (0,) * len(shape))
    return pl.pallas_call(
        functools.partial(_nsa_kernel, tq=tq, tk=tk, nbs=nbs, n_sel=min(SEL_TOPN, nbs)),
        out_shape=jax.ShapeDtypeStruct((bsz, seq, H * NSA_DIM), BF16),
        grid=(bsz, seq // tq),
        in_specs=[pl.BlockSpec((1, tq, H * LANES), lambda b, i: (b, i, C_NQ // (H * LANES))),
                  pl.BlockSpec((1, tq, LANES), lambda b, i: (b, i, C_GATE // LANES)),
                  slab(2), slab(3), slab(4), slab(5),
                  pl.BlockSpec((1, 2, LANES, LANES), lambda b, i: (b, 0, 0, 0)),
                  pl.BlockSpec((H, tq, LANES), lambda b, i: (0, i, 0)),
                  const((H, nwin + 1, tq, (nwin + 1) * tq)), const((H, 2, tq, tq)),
                  const((LANES, LANES)), const((seq, LANES)), const(gexp.shape)],
        out_specs=pl.BlockSpec((1, tq, H * NSA_DIM), lambda b, i: (b, i, 0)),
        scratch_shapes=[pltpu.VMEM((seq, 2 * LANES), BF16), pltpu.VMEM((seq, 2 * LANES), BF16),
                        pltpu.VMEM((H * tq, LANES), F32), pltpu.VMEM((H * tq, 2 * LANES), F32)],
        compiler_params=_cparams(("parallel", "arbitrary")),
        name="nsa_attention",
    )(proj3, proj3, proj3, proj3, proj3, proj3, kvc, bc, bw, bs, ov, blk1h, gexp)


def _mix_xattn_kernel(ys_ref, ym_ref, yn_ref, h_ref, g1_ref, g2_ref, g3_ref, w1_ref, w2_ref, w3_ref,
                      kv_ref, g_ref, wq_ref, wo_ref, o_ref, *, dh):
    def part(y_ref, gy_ref, w_ref):
        y = y_ref[0].astype(F32)
        return _dot(_rms(y, gy_ref[...], y.shape[-1]).astype(BF16), w_ref[...])

    h = h_ref[0] + part(ys_ref, g1_ref, w1_ref) + part(ym_ref, g2_ref, w2_ref) + part(yn_ref, g3_ref, w3_ref)
    xn = _rms(h, g_ref[...], h.shape[-1]).astype(BF16)
    q = _dot(xn, wq_ref[...]).astype(BF16)
    hw = XATTN_HEADS * dh
    heads = range(XATTN_HEADS)
    s = [_dot_t(q[:, hd * dh:(hd + 1) * dh], kv_ref[0, :, hd * dh:(hd + 1) * dh]) for hd in heads]
    p = [jnp.exp2(x - jnp.max(x, axis=-1, keepdims=True)) for x in s]
    p = [(x / jnp.sum(x, axis=-1, keepdims=True)).astype(BF16) for x in p]
    outs = [_dot(p[hd], kv_ref[0, :, hw + hd * dh:hw + (hd + 1) * dh]).astype(BF16) for hd in heads]
    o = jnp.concatenate(outs, axis=-1)
    o_ref[0] = h + _dot(o, wo_ref[...])


def mix_out_cross_attention(ys3, h3, gains, weights, kv3, g_x, wq, wo, *, tm):
    bsz, seq, d = h3.shape
    m = kv3.shape[1]
    dh = d // XATTN_HEADS
    wq_s = (wq.astype(F32) * (dh ** -0.5 * LOG2E)).astype(BF16)
    gains = [g.reshape(1, -1).astype(F32) for g in gains]
    const = lambda shape: pl.BlockSpec(shape, lambda b, i: (0,) * len(shape))
    row = lambda w: pl.BlockSpec((1, tm, w), lambda b, i: (b, i, 0))
    return pl.pallas_call(
        functools.partial(_mix_xattn_kernel, dh=dh),
        out_shape=jax.ShapeDtypeStruct((bsz, seq, d), F32),
        grid=(bsz, seq // tm),
        in_specs=[row(y.shape[2]) for y in ys3] + [row(d)]
                 + [const(g.shape) for g in gains] + [const(w.shape) for w in weights]
                 + [pl.BlockSpec((1, m, 2 * d), lambda b, i: (b, 0, 0)),
                    const((1, d)), const((d, d)), const((d, d))],
        out_specs=row(d),
        compiler_params=_cparams(("parallel", "parallel")),
        name="mix_out_cross_attention",
    )(*ys3, h3, *gains, *weights, kv3, g_x.reshape(1, d).astype(F32), wq_s, wo.astype(BF16))


def _ffn_kernel(h_ref, g_ref, wg_ref, wu_ref, wd_ref, o_ref, xn_ref, acc_ref):
    j = pl.program_id(1)

    @pl.when(j == 0)
    def _():
        h = h_ref[...]
        xn_ref[...] = _rms(h, g_ref[...], h.shape[-1]).astype(BF16)
        acc_ref[...] = h

    xn = xn_ref[...]
    a = _silu(_dot(xn, wg_ref[...])) * _dot(xn, wu_ref[...])
    acc_ref[...] += _dot(a.astype(BF16), wd_ref[...])

    @pl.when(j == pl.num_programs(1) - 1)
    def _():
        o_ref[...] = acc_ref[...]


def dense_ffn(h, g, wg, wu, wd, *, tm, tf):
    m, d = h.shape
    ff = wg.shape[1]
    return pl.pallas_call(
        _ffn_kernel,
        out_shape=jax.ShapeDtypeStruct((m, d), F32),
        grid=(m // tm, ff // tf),
        in_specs=[pl.BlockSpec((tm, d), lambda i, j: (i, 0)),
                  pl.BlockSpec((1, d), lambda i, j: (0, 0)),
                  pl.BlockSpec((d, tf), lambda i, j: (0, j)),
                  pl.BlockSpec((d, tf), lambda i, j: (0, j)),
                  pl.BlockSpec((tf, d), lambda i, j: (j, 0))],
        out_specs=pl.BlockSpec((tm, d), lambda i, j: (i, 0)),
        scratch_shapes=[pltpu.VMEM((tm, d), BF16), pltpu.VMEM((tm, d), F32)],
        compiler_params=_cparams(("parallel", "arbitrary")),
        name="dense_ffn",
    )(h, g.reshape(1, d).astype(F32), wg.astype(BF16), wu.astype(BF16), wd.astype(BF16))


def _router_kernel(h_ref, g_ref, wr_hi_ref, wr_lo_ref, xn_ref, info_ref, cnt_ref, carry_ref, *, tm):
    i = pl.program_id(0)

    @pl.when(i == 0)
    def _():
        carry_ref[...] = jnp.zeros_like(carry_ref)

    h = h_ref[...]
    xn = _rms(h, g_ref[...], h.shape[-1])
    xn_ref[...] = xn
    x_hi = xn.astype(BF16)
    x_lo = (xn - x_hi.astype(F32)).astype(BF16)
    logits = _dot(x_hi, wr_hi_ref[...]) + _dot(x_lo, wr_hi_ref[...]) + _dot(x_hi, wr_lo_ref[...])
    lane = lax.broadcasted_iota(jnp.int32, (tm, LANES), 1)
    lanef = lane.astype(F32)
    logits = jnp.where(lane < N_EXPERTS, logits, NEG_INF)
    m1 = jnp.max(logits, axis=-1, keepdims=True)
    i1 = jnp.min(jnp.where(logits == m1, lanef, float(LANES)), axis=-1, keepdims=True)
    rest = jnp.where(lanef == i1, NEG_INF, logits)
    m2 = jnp.max(rest, axis=-1, keepdims=True)
    i2 = jnp.min(jnp.where(rest == m2, lanef, float(LANES)), axis=-1, keepdims=True)
    e2 = jnp.exp(m2 - m1)
    w1 = 1.0 / (1.0 + e2)
    w2 = e2 / (1.0 + e2)
    oh1 = lanef == i1
    oh2 = lanef == i2
    oh = jnp.where(oh1 | oh2, 1.0, 0.0)
    rr = lax.broadcasted_iota(jnp.int32, (tm, tm), 0)
    cc = lax.broadcasted_iota(jnp.int32, (tm, tm), 1)
    tri = jnp.where(cc < rr, 1.0, 0.0).astype(BF16)
    before = _dot(tri, oh.astype(BF16)) + carry_ref[0:1, :]
    r1 = jnp.sum(jnp.where(oh1, before, 0.0), axis=-1, keepdims=True)
    r2 = jnp.sum(jnp.where(oh2, before, 0.0), axis=-1, keepdims=True)
    carry_ref[...] = carry_ref[...] + jnp.sum(oh, axis=0, keepdims=True)
    info = jnp.where(lane == 0, i1, jnp.where(lane == 1, i2, jnp.where(lane == 2, w1, jnp.where(
        lane == 3, w2, jnp.where(lane == 4, r1, jnp.where(lane == 5, r2, 0.0))))))
    info_ref[...] = info
    cnt_ref[...] = carry_ref[...]


def moe_router(h, g, router, *, tm):
    m, d = h.shape
    wr = jnp.pad(router.astype(F32), ((0, 0), (0, LANES - N_EXPERTS)))
    wr_hi = wr.astype(BF16)
    wr_lo = (wr - wr_hi.astype(F32)).astype(BF16)
    return pl.pallas_call(
        functools.partial(_router_kernel, tm=tm),
        out_shape=(jax.ShapeDtypeStruct((m, d), F32), jax.ShapeDtypeStruct((m, LANES), F32),
                   jax.ShapeDtypeStruct((8, LANES), F32)),
        grid=(m // tm,),
        in_specs=[pl.BlockSpec((tm, d), lambda i: (i, 0)),
                  pl.BlockSpec((1, d), lambda i: (0, 0)),
                  pl.BlockSpec((d, LANES), lambda i: (0, 0)),
                  pl.BlockSpec((d, LANES), lambda i: (0, 0))],
        out_specs=(pl.BlockSpec((tm, d), lambda i: (i, 0)),
                   pl.BlockSpec((tm, LANES), lambda i: (i, 0)),
                   pl.BlockSpec((8, LANES), lambda i: (0, 0))),
        scratch_shapes=[pltpu.VMEM((8, LANES), F32)],
        compiler_params=_cparams(("arbitrary",)),
        name="moe_router",
    )(h, g.reshape(1, d).astype(F32), wr_hi, wr_lo)


def _row_copy(src_hbm, row, dst, slot, sem):
    return pltpu.make_async_copy(src_hbm.at[pl.ds(row, 1), :], dst.at[pl.ds(slot, 1), :], sem)


def _rows_wait(src_hbm, dst, sem):
    pltpu.make_async_copy(src_hbm.at[pl.ds(0, dst.shape[0]), :], dst, sem).wait()


def _moe_ffn_kernel(src_ref, texp_ref, nact_ref, x_hbm, wg_ref, wu_ref, wd_ref, o_ref,
                    xbuf, xbf, acc_ref, sem, *, tm, nj):
    i = pl.program_id(0)
    j = pl.program_id(1)
    nact = nact_ref[0]
    active = i < nact
    nbuf = xbuf.shape[0]
    ahead = nbuf - 1
    cur = i % nbuf
    rows_per_step = tm // nj

    for t0 in range(ahead):
        @pl.when((i == 0) & (j == 0) & (t0 < nact))
        def _():
            def issue(s, c):
                _row_copy(x_hbm, src_ref[t0 * tm + s], xbuf.at[t0], s, sem.at[t0]).start()
                return c

            lax.fori_loop(0, tm, issue, 0, unroll=8)

    @pl.when(active & (j == 0))
    def _():
        _rows_wait(x_hbm, xbuf.at[cur], sem.at[cur])
        xbf[...] = xbuf[cur].astype(BF16)
        acc_ref[...] = jnp.zeros_like(acc_ref)

    def compute(prefetch):
        if prefetch:
            nxt = (i + ahead) % nbuf
            base = (i + ahead) * tm + j * rows_per_step
            for k in range(rows_per_step):
                _row_copy(x_hbm, src_ref[base + k], xbuf.at[nxt], j * rows_per_step + k, sem.at[nxt]).start()
        x = xbf[...]
        a = _silu(_dot(x, wg_ref[0])) * _dot(x, wu_ref[0])
        acc_ref[...] += _dot(a.astype(BF16), wd_ref[0])

    @pl.when(i + ahead < nact)
    def _():
        compute(True)

    @pl.when(active & (i + ahead >= nact))
    def _():
        compute(False)

    @pl.when(j == nj - 1)
    def _():
        o_ref[...] = jnp.where(active, acc_ref[...], 0.0)


def moe_expert_ffn(xn, src, tile_expert, n_active, wg, wu, wd, *, tm, tf):
    n_slots = src.shape[0]
    d = xn.shape[1]
    ne, _, ff = wg.shape
    nj = ff // tf
    assert nj * tf == ff and tm % nj == 0
    wg, wu, wd = wg.astype(BF16), wu.astype(BF16), wd.astype(BF16)

    def wmap_col(i, j, src, texp, nact):
        return (texp[i], 0, jnp.where(i < nact[0], j, nj - 1))

    def wmap_row(i, j, src, texp, nact):
        return (texp[i], jnp.where(i < nact[0], j, nj - 1), 0)

    return pl.pallas_call(
        functools.partial(_moe_ffn_kernel, tm=tm, nj=nj),
        out_shape=jax.ShapeDtypeStruct((n_slots, d), F32),
        grid_spec=pltpu.PrefetchScalarGridSpec(
            num_scalar_prefetch=3,
            grid=(n_slots // tm, nj),
            in_specs=[pl.BlockSpec(memory_space=pl.ANY),
                      pl.BlockSpec((1, d, tf), wmap_col),
                      pl.BlockSpec((1, d, tf), wmap_col),
                      pl.BlockSpec((1, tf, d), wmap_row)],
            out_specs=pl.BlockSpec((tm, d), lambda i, j, *_: (i, 0)),
            scratch_shapes=[pltpu.VMEM((3, tm, d), F32), pltpu.VMEM((tm, d), BF16),
                            pltpu.VMEM((tm, d), F32), pltpu.SemaphoreType.DMA((3,))]),
        compiler_params=_cparams(("arbitrary", "arbitrary")),
        name="moe_expert_ffn",
    )(src, tile_expert, n_active, xn, wg, wu, wd)


def _moe_combine_kernel(pos_ref, h_ref, info_ref, ys_hbm, g_ref, o_ref, buf, sem, *, tm, final_norm):
    i = pl.program_id(0)
    n = pl.num_programs(0)
    cur = i % 2

    def start_gather(tile, b):
        for s in range(tm):
            for k in range(2):
                _row_copy(ys_hbm, pos_ref[2 * (tile * tm + s) + k], buf.at[b, k], s, sem.at[b]).start()

    @pl.when(i == 0)
    def _():
        start_gather(0, 0)

    @pl.when(i + 1 < n)
    def _():
        start_gather(i + 1, 1 - cur)

    for k in range(2):
        _rows_wait(ys_hbm, buf.at[cur, k], sem.at[cur])
    info = info_ref[...]
    y = h_ref[...] + info[:, 2:3] * buf[cur, 0] + info[:, 3:4] * buf[cur, 1]
    if final_norm:
        y = _rms(y, g_ref[...], y.shape[-1])
    o_ref[...] = y


def moe_combine(h, info, ys, pos_flat, g_final, *, tm, final_norm):
    m, d = h.shape
    return pl.pallas_call(
        functools.partial(_moe_combine_kernel, tm=tm, final_norm=final_norm),
        out_shape=jax.ShapeDtypeStruct((m, d), F32),
        grid_spec=pltpu.PrefetchScalarGridSpec(
            num_scalar_prefetch=1,
            grid=(m // tm,),
            in_specs=[pl.BlockSpec((tm, d), lambda i, *_: (i, 0)),
                      pl.BlockSpec((tm, LANES), lambda i, *_: (i, 0)),
                      pl.BlockSpec(memory_space=pl.ANY),
                      pl.BlockSpec((1, d), lambda i, *_: (0, 0))],
            out_specs=pl.BlockSpec((tm, d), lambda i, *_: (i, 0)),
            scratch_shapes=[pltpu.VMEM((2, 2, tm, d), F32), pltpu.SemaphoreType.DMA((2,))]),
        compiler_params=_cparams(("arbitrary",)),
        name="moe_combine",
    )(pos_flat, h, info, ys, g_final.reshape(1, d).astype(F32))


def moe_layer(h, g, router, wg, wu, wd, g_final, *, final_norm, tm_r=512, tm_g=512, tf=1792, tm_c=256):
    m, d = h.shape
    xn, info, cnt = moe_router(h, g, router, tm=tm_r)
    e_idx = info[:, 0:2].astype(jnp.int32)
    rank = info[:, 4:6].astype(jnp.int32)
    counts = cnt[0, :N_EXPERTS].astype(jnp.int32)
    tiles_per = (counts + tm_g - 1) // tm_g
    tile_end = jnp.cumsum(tiles_per)
    seg_start = (tile_end - tiles_per) * tm_g
    pos = rank
    for e in range(N_EXPERTS):
        pos = pos + jnp.where(e_idx == e, seg_start[e], 0)
    n_tiles = (2 * m) // tm_g + N_EXPERTS
    n_slots = n_tiles * tm_g
    tok = jnp.broadcast_to(jnp.arange(m, dtype=jnp.int32)[:, None], (m, 2))
    src = jnp.zeros((n_slots,), jnp.int32).at[pos.reshape(-1)].set(tok.reshape(-1))
    n_active = tile_end[-1:].astype(jnp.int32)
    tile_ids = jnp.minimum(jnp.arange(n_tiles, dtype=jnp.int32), n_active[0] - 1)
    tile_expert = jnp.sum(tile_ids[:, None] >= tile_end[None, :], axis=1).astype(jnp.int32)
    ys = moe_expert_ffn(xn, src, tile_expert, n_active,
                        wg, wu, wd, tm=tm_g, tf=tf)
    return moe_combine(h, info, ys, pos.reshape(-1).astype(jnp.int32), g_final, tm=tm_c, final_norm=final_norm)


def _final_norm_kernel(h_ref, g_ref, o_ref):
    h = h_ref[...]
    o_ref[...] = _rms(h, g_ref[...], h.shape[-1])


def final_rmsnorm(h, g, *, tm):
    m, d = h.shape
    return pl.pallas_call(
        _final_norm_kernel,
        out_shape=jax.ShapeDtypeStruct((m, d), F32),
        grid=(m // tm,),
        in_specs=[pl.BlockSpec((tm, d), lambda i: (i, 0)), pl.BlockSpec((1, d), lambda i: (0, 0))],
        out_specs=pl.BlockSpec((tm, d), lambda i: (i, 0)),
        compiler_params=_cparams(("parallel",)),
        name="final_rmsnorm",
    )(h, g.reshape(1, d).astype(F32))


def _pack_w_in(w):
    d = w.shape[0]
    w = w.astype(F32)
    o = 0
    u = w[:, o:o + SSM_WIDTH]; o += SSM_WIDTH
    cq = w[:, o:o + MLA_Q_RANK]; o += MLA_Q_RANK
    ckv = w[:, o:o + MLA_KV_RANK]; o += MLA_KV_RANK
    kr = w[:, o:o + MLA_ROPE]; o += MLA_ROPE
    nq = w[:, o:o + NSA_HEADS * NSA_DIM]; o += NSA_HEADS * NSA_DIM
    nkv = w[:, o:o + 6 * NSA_KV_HEADS * NSA_DIM]; o += 6 * NSA_KV_HEADS * NSA_DIM
    gate = w[:, o:o + 3 * NSA_HEADS]
    z = lambda n: jnp.zeros((d, n), F32)
    kr_a = jnp.concatenate([z(MLA_NOPE), kr, z(LANES - MLA_NOPE - MLA_ROPE)], axis=1)
    kr_b = jnp.concatenate([z(MLA_NOPE), _rot_half_cols(kr), z(LANES - MLA_NOPE - MLA_ROPE)], axis=1)
    nq_h = (nq * (NSA_DIM ** -0.5 * LOG2E)).reshape(d, NSA_KV_HEADS, NSA_REP, NSA_DIM)
    zq = jnp.zeros((d, NSA_REP, NSA_DIM), F32)
    nq_p = jnp.concatenate([
        jnp.concatenate([nq_h[:, 0], zq], axis=-1).reshape(d, NSA_REP * LANES),
        jnp.concatenate([zq, nq_h[:, 1]], axis=-1).reshape(d, NSA_REP * LANES)], axis=1)
    packed = jnp.concatenate([u, cq, z(256 - MLA_Q_RANK), kr_a, kr_b, nq_p, nkv, ckv,
                              gate, z(LANES - 3 * NSA_HEADS)], axis=1)
    assert packed.shape[1] == IN_COLS_PACKED
    return packed.astype(BF16)


def _rg_order(a):
    rest = a.shape[1:]
    return a.reshape((NSA_KV_HEADS, NSA_REP, NSA_DIM) + rest).swapaxes(0, 1).reshape((-1,) + rest)


def kernel(x, mem, w_in, w_out, mix_norm, out_norm, ssm_a_re, ssm_a_im, ssm_b_re, ssm_b_im, ssm_c_re, ssm_c_im, ssm_d, ssm_log_dt, ssm_w_glu, mla_q_norm, mla_w_uq, mla_kv_norm, mla_w_ukv, nsa_cmp_pe, nsa_cmp_w1, nsa_cmp_w2, rel_bias, xattn_norm, mem_norm, xattn_wq, xattn_wkv, xattn_wo, ffn_norm, dense_w_gate, dense_w_up, dense_w_down, moe_router, moe_w_gate, moe_w_up, moe_w_down, final_norm):
    bsz, seq, d = x.shape
    depth = w_in.shape[0]
    T = bsz * seq
    nmem = mem.shape[1]
    tq_nsa, tk_nsa = 2 * LANES, 4 * LANES
    rope_tabs = _rope_tables(seq)
    nsa_tabs = _nsa_bias_tables(rel_bias, seq, tq_nsa)
    o1, o2 = SSM_WIDTH, SSM_WIDTH + MLA_HEADS * MLA_V
    mem2 = mem.reshape(bsz * nmem, d)
    h = x.reshape(T, d)
    for l in range(depth):
        proj = norm_matmul(h, mix_norm[l], _pack_w_in(w_in[l]), tm=512, tn=IN_COLS_PACKED, out_dtype=BF16)
        proj3 = proj.reshape(bsz, seq, IN_COLS_PACKED)
        u_tm = proj3[:, :, C_U:C_U + SSM_WIDTH].transpose(1, 0, 2).reshape(seq * bsz, SSM_WIDTH)
        y_ssm = ssm_mixer(u_tm, ssm_a_re[l], ssm_a_im[l], ssm_b_re[l], ssm_b_im[l], ssm_c_re[l], ssm_c_im[l],
                          ssm_d[l], ssm_log_dt[l], ssm_w_glu[l], nb=bsz, tc=64)
        y_ssm = y_ssm.reshape(seq, bsz, SSM_WIDTH).transpose(1, 0, 2)
        y_mla = mla_mixer(proj3, mla_q_norm[l], mla_w_uq[l], mla_kv_norm[l], mla_w_ukv[l], rope_tabs,
                          tm=512, tq=512)
        nch = seq // CMP_STRIDE
        kv_cr = jnp.stack([proj3[:, :, C_NKV:C_NKV + LANES].reshape(bsz, nch, CMP_STRIDE * LANES),
                           proj3[:, :, C_NKV + LANES:C_NKV + 2 * LANES].reshape(bsz, nch, CMP_STRIDE * LANES)],
                          axis=1)
        kvc = nsa_compress(kv_cr, nsa_cmp_pe[l], nsa_cmp_w1[l], nsa_cmp_w2[l])
        y_nsa = nsa_mixer(proj3, kvc, nsa_tabs, tq=tq_nsa, tk=tk_nsa)
        g_out = out_norm[l]
        wo_l = w_out[l]
        kv_mem = norm_matmul(mem2, mem_norm[l], xattn_wkv[l].astype(BF16), tm=256, tn=512, out_dtype=BF16)
        h = mix_out_cross_attention(
            [y_ssm, y_mla, y_nsa], h.reshape(bsz, seq, d),
            [g_out[:o1], g_out[o1:o2], _rg_order(g_out[o2:])],
            [wo_l[:o1].astype(BF16), wo_l[o1:o2].astype(BF16), _rg_order(wo_l[o2:]).astype(BF16)],
            kv_mem.reshape(bsz, nmem, 2 * d), xattn_norm[l], xattn_wq[l], xattn_wo[l], tm=256).reshape(T, d)
        last = l == depth - 1
        if l % 2 == 0:
            h = dense_ffn(h, ffn_norm[l], dense_w_gate[l // 2], dense_w_up[l // 2], dense_w_down[l // 2],
                          tm=512, tf=1408)
            if last:
                h = final_rmsnorm(h, final_norm, tm=512)
        else:
            h = moe_layer(h, ffn_norm[l], moe_router[l // 2], moe_w_gate[l // 2], moe_w_up[l // 2],
                          moe_w_down[l // 2], final_norm, final_norm=last)
    return h.reshape(bsz, seq, d)
```

```python
import functools
import math

import jax
import jax.numpy as jnp
from jax import lax
from jax.experimental import pallas as pl
from jax.experimental.pallas import tpu as pltpu

F32 = jnp.float32
BF16 = jnp.bfloat16

HEAD_DIM = 64
SSM_WIDTH = 256
SSM_CH = 16
SSM_GROUPS = 16
SSM_STATE = 64
MLA_HEADS = 6
MLA_NOPE = 64
MLA_ROPE = 32
MLA_V = 64
MLA_Q_RANK = 192
MLA_KV_RANK = 128
NSA_HEADS = 6
NSA_KV_HEADS = 2
NSA_REP = 3
NSA_DIM = 64
CMP_BLOCK = 32
CMP_STRIDE = 16
SEL_BLOCK = 64
SEL_TOPN = 8
WINDOW = 256
REL_BUCKETS = 32
REL_MAX_DIST = 128
XATTN_HEADS = 4
N_EXPERTS = 8
ROPE_THETA = 10000.0
EPS = 1e-6
NEG_INF = -1e30
FORCE = 1e9
LOG2E = math.log2(math.e)

LANES = 128
VMEM_LIMIT = 56 * 1024 * 1024

C_U, C_CQ, C_KR, C_NQ, C_NKV, C_CKV, C_GATE = 0, 256, 512, 768, 1536, 2304, 2432
IN_COLS_PACKED = 2560


def _cparams(sem):
    return pltpu.CompilerParams(dimension_semantics=sem, vmem_limit_bytes=VMEM_LIMIT)


def _dot(a, b):
    return jnp.dot(a, b, preferred_element_type=F32)


def _dot_t(a, b):
    return lax.dot_general(a, b, (((1,), (1,)), ((), ())), preferred_element_type=F32)


def _rms(x, g, n):
    ms = jnp.sum(x * x, axis=-1, keepdims=True) * (1.0 / n)
    return x * lax.rsqrt(ms + EPS) * g


def _sigmoid(x):
    return 1.0 / (1.0 + jnp.exp(-x))


def _silu(x):
    return x * _sigmoid(x)


def _swiglu_accumulate(x_ref, wg, wu, wd, acc_ref, parts=2):
    rows = x_ref.shape[0] // parts
    sl = [slice(k * rows, (k + 1) * rows) for k in range(parts)]
    x = [x_ref[s, :] for s in sl]
    g = [_dot(xk, wg) for xk in x]
    u = [_dot(xk, wu) for xk in x]
    a = [(_silu(gk) * uk).astype(BF16) for gk, uk in zip(g, u)]
    d = [_dot(ak, wd) for ak in a]
    for s, dk in zip(sl, d):
        acc_ref[s, :] += dk


def _norm_mm_kernel(x_ref, g_ref, w_ref, o_ref, xn_ref):
    @pl.when(pl.program_id(1) == 0)
    def _():
        x = x_ref[...].astype(F32)
        xn_ref[...] = _rms(x, g_ref[...], x.shape[-1]).astype(BF16)

    o_ref[...] = _dot(xn_ref[...], w_ref[...]).astype(o_ref.dtype)


def norm_matmul(x, g, w, *, tm, tn, out_dtype):
    m, k = x.shape
    n = w.shape[1]
    return pl.pallas_call(
        _norm_mm_kernel,
        out_shape=jax.ShapeDtypeStruct((m, n), out_dtype),
        grid=(m // tm, n // tn),
        in_specs=[pl.BlockSpec((tm, k), lambda i, j: (i, 0)),
                  pl.BlockSpec((1, k), lambda i, j: (0, 0)),
                  pl.BlockSpec((k, tn), lambda i, j: (0, j))],
        out_specs=pl.BlockSpec((tm, tn), lambda i, j: (i, j)),
        scratch_shapes=[pltpu.VMEM((tm, k), BF16)],
        compiler_params=_cparams(("parallel", "arbitrary")),
        name="norm_matmul",
    )(x, g.reshape(1, k).astype(F32), w)


def _ssm_kernel(u_ref, bbr_ref, bbi_ref, ar_ref, ai_ref, ccr_ref, cci_ref, d_ref, wglu_ref,
                o_ref, hr_ref, hi_ref, cr_ref, ci_ref, *, tc, nb):
    @pl.when(pl.program_id(0) == 0)
    def _():
        cr_ref[...] = jnp.zeros_like(cr_ref)
        ci_ref[...] = jnp.zeros_like(ci_ref)

    u = u_ref[...]
    hr_ref[...] = _dot(u, bbr_ref[...])
    hi_ref[...] = _dot(u, bbi_ref[...])
    gp = ar_ref.shape[-1]
    ar = jnp.broadcast_to(ar_ref[...], (nb, gp))
    ai = jnp.broadcast_to(ai_ref[...], (nb, gp))

    def step(t, carry):
        hr, hi = carry
        rows = pl.ds(pl.multiple_of(t * nb, nb), nb)
        nr = ar * hr - ai * hi + hr_ref[rows, :]
        ni = ar * hi + ai * hr + hi_ref[rows, :]
        hr_ref[rows, :] = nr
        hi_ref[rows, :] = ni
        return nr, ni

    hr, hi = lax.fori_loop(0, tc, step, (cr_ref[...], ci_ref[...]))
    cr_ref[...] = hr
    ci_ref[...] = hi
    y = (_dot(hr_ref[...].astype(BF16), ccr_ref[...]) + _dot(hi_ref[...].astype(BF16), cci_ref[...])
         + d_ref[...] * u.astype(F32))
    y = jax.nn.gelu(y)
    z = _dot(y.astype(BF16), wglu_ref[...])
    o_ref[...] = (y * _sigmoid(z)).astype(o_ref.dtype)


def ssm_mixer(u_tm, a_re, a_im, b_re, b_im, c_re, c_im, d, log_dt, w_glu, *, nb, tc):
    rows = u_tm.shape[0]
    G, P, C = SSM_GROUPS, SSM_STATE, SSM_CH
    dt = jnp.exp(log_dt.astype(F32))[:, None]
    lr, li = a_re.astype(F32), a_im.astype(F32)
    mag = jnp.exp(lr * dt)
    ab_r, ab_i = mag * jnp.cos(li * dt), mag * jnp.sin(li * dt)
    den = lr * lr + li * li
    nr = ab_r - 1.0
    f_r = (nr * lr + ab_i * li) / den
    f_i = (ab_i * lr - nr * li) / den
    br, bi = b_re.astype(F32), b_im.astype(F32)
    bb_r = f_r[..., None] * br - f_i[..., None] * bi
    bb_i = f_r[..., None] * bi + f_i[..., None] * br
    eye = jnp.eye(G, dtype=F32)
    bbr = jnp.einsum('gpc,gh->gchp', bb_r, eye).reshape(G * C, G * P).astype(BF16)
    bbi = jnp.einsum('gpc,gh->gchp', bb_i, eye).reshape(G * C, G * P).astype(BF16)
    ccr = jnp.einsum('gcp,gh->gphc', c_re.astype(F32), eye).reshape(G * P, G * C).astype(BF16)
    cci = jnp.einsum('gcp,gh->gphc', -c_im.astype(F32), eye).reshape(G * P, G * C).astype(BF16)
    gp = G * P
    full = lambda shape: pl.BlockSpec(shape, lambda i: (0,) * len(shape))
    return pl.pallas_call(
        functools.partial(_ssm_kernel, tc=tc, nb=nb),
        out_shape=jax.ShapeDtypeStruct((rows, SSM_WIDTH), BF16),
        grid=(rows // (tc * nb),),
        in_specs=[pl.BlockSpec((tc * nb, SSM_WIDTH), lambda i: (i, 0)),
                  full((G * C, gp)), full((G * C, gp)), full((1, gp)), full((1, gp)),
                  full((gp, G * C)), full((gp, G * C)), full((1, SSM_WIDTH)),
                  full((SSM_WIDTH, SSM_WIDTH))],
        out_specs=pl.BlockSpec((tc * nb, SSM_WIDTH), lambda i: (i, 0)),
        scratch_shapes=[pltpu.VMEM((tc * nb, gp), F32), pltpu.VMEM((tc * nb, gp), F32),
                        pltpu.VMEM((nb, gp), F32), pltpu.VMEM((nb, gp), F32)],
        compiler_params=_cparams(("arbitrary",)),
        name="ssm_mixer",
    )(u_tm, bbr, bbi, ab_r.reshape(1, gp), ab_i.reshape(1, gp), ccr, cci,
      d.reshape(1, SSM_WIDTH).astype(F32), w_glu.astype(BF16))


def _mla_prep_kernel(cq_ref, kr_ref, ckv_ref, gq_ref, gkv_ref, wqa_ref, wqb_ref, wk_ref, wv_ref,
                     c1_ref, c0_ref, s0_ref, q_ref, k_ref, v_ref):
    qn = _rms(cq_ref[0].astype(F32), gq_ref[...], MLA_Q_RANK).astype(BF16)
    qa = _dot(qn, wqa_ref[...])
    qb = _dot(qn, wqb_ref[...])
    kn = _rms(ckv_ref[0].astype(F32), gkv_ref[...], MLA_KV_RANK).astype(BF16)
    ka = _dot(kn, wk_ref[...])
    va = _dot(kn, wv_ref[...])
    kr = kr_ref[0].astype(F32)
    c1, c0, s0 = c1_ref[...], c0_ref[...], s0_ref[...]
    krope = kr[:, :LANES] * c0 + kr[:, LANES:] * s0
    low_half = lax.broadcasted_iota(jnp.int32, (kr.shape[0], LANES), 1) < MLA_V
    for h in range(MLA_HEADS):
        sl = slice(h * LANES, (h + 1) * LANES)
        q_ref[0, h] = (qa[:, sl] * c1 + qb[:, sl] * s0).astype(BF16)
        k_ref[0, h] = (ka[:, sl] + krope).astype(BF16)
        ones = jnp.where(low_half if h % 2 == 0 else jnp.logical_not(low_half), 1.0, 0.0)
        v_ref[0, h] = jnp.concatenate([va[:, sl], ones], axis=1).astype(BF16)


def _mla_flash_kernel(q_ref, k_ref, v_ref, o_ref, m_ref, acc_ref, *, tq):
    qi = pl.program_id(1)
    m_ref[...] = jnp.full_like(m_ref, NEG_INF)
    acc_ref[...] = jnp.zeros_like(acc_ref)
    lane = lax.broadcasted_iota(jnp.int32, (tq, LANES), 1)
    rep = tq // LANES

    def tile(kt, masked):
        ks = pl.ds(pl.multiple_of(kt * tq, tq), tq)
        if masked:
            mask = (lax.broadcasted_iota(jnp.int32, (tq, tq), 1)
                    <= lax.broadcasted_iota(jnp.int32, (tq, tq), 0))
        hs = range(MLA_HEADS)
        s = [_dot_t(q_ref[0, h], k_ref[0, h, ks, :]) for h in hs]
        if masked:
            s = [jnp.where(mask, x, NEG_INF) for x in s]
        m_prev = [m_ref[h] for h in hs]
        m_new = [jnp.maximum(mp, jnp.max(x, axis=-1, keepdims=True)) for mp, x in zip(m_prev, s)]
        alpha = [jnp.exp2(mp - mn) for mp, mn in zip(m_prev, m_new)]
        p = [jnp.exp2(x - jnp.tile(mn, (1, rep))).astype(BF16) for x, mn in zip(s, m_new)]
        pv = [_dot(p[h], v_ref[0, h, ks, :]) for h in hs]
        for h in hs:
            m_ref[h] = m_new[h]
        for pr in range(MLA_HEADS // 2):
            a = jnp.tile(jnp.where(lane < MLA_V, alpha[2 * pr], alpha[2 * pr + 1]), (1, 2))
            acc_ref[pr] = acc_ref[pr] * a + pv[2 * pr] + pv[2 * pr + 1]

    def body(kt, c):
        tile(kt, False)
        return c

    lax.fori_loop(0, qi, body, 0)
    tile(qi, True)
    for pr in range(MLA_HEADS // 2):
        o_ref[0, :, pr * LANES:(pr + 1) * LANES] = (acc_ref[pr, :, :LANES] / acc_ref[pr, :, LANES:]).astype(o_ref.dtype)


def _rope_tables(seq):
    pos = jnp.arange(seq, dtype=F32)
    inv = 1.0 / (ROPE_THETA ** (jnp.arange(0, MLA_ROPE, 2, dtype=F32) / MLA_ROPE))
    ang = pos[:, None] * inv[None, :]
    cos, sin = jnp.cos(ang), jnp.sin(ang)
    cos2 = jnp.concatenate([cos, cos], axis=-1)
    sin2 = jnp.concatenate([sin, sin], axis=-1)
    z64 = jnp.zeros((seq, MLA_NOPE), F32)
    z32 = jnp.zeros((seq, LANES - MLA_NOPE - MLA_ROPE), F32)
    c1 = jnp.concatenate([jnp.ones((seq, MLA_NOPE), F32), cos2, z32], axis=-1)
    c0 = jnp.concatenate([z64, cos2, z32], axis=-1)
    s0 = jnp.concatenate([z64, sin2, z32], axis=-1)
    return c1, c0, s0


def _rot_half_cols(w):
    half = MLA_ROPE // 2
    return jnp.concatenate([-w[..., half:], w[..., :half]], axis=-1)


def mla_mixer(proj3, q_norm, w_uq, kv_norm, w_ukv, tabs, *, tm, tq):
    bsz, seq, _ = proj3.shape
    H = MLA_HEADS
    scale = (MLA_NOPE + MLA_ROPE) ** -0.5 * LOG2E
    wq =(w_uq.astype(F32) * scale).reshape(MLA_Q_RANK, H, MLA_NOPE + MLA_ROPE)
    zq = jnp.zeros((MLA_Q_RANK, H, LANES - MLA_NOPE - MLA_ROPE), F32)
    z64 = jnp.zeros((MLA_Q_RANK, H, MLA_NOPE), F32)
    wqa = jnp.concatenate([wq, zq], axis=-1).reshape(MLA_Q_RANK, H * LANES)
    wqb = jnp.concatenate([z64, _rot_half_cols(wq[..., MLA_NOPE:]), zq], axis=-1).reshape(MLA_Q_RANK, H * LANES)
    padq = ((0, 256 - MLA_Q_RANK), (0, 0))
    wqa = jnp.pad(wqa, padq).astype(BF16)
    wqb = jnp.pad(wqb, padq).astype(BF16)
    gq = jnp.pad(q_norm.astype(F32), (0, 256 - MLA_Q_RANK)).reshape(1, 256)
    wkv = w_ukv.astype(F32).reshape(MLA_KV_RANK, H, MLA_NOPE + MLA_V)
    zk = jnp.zeros((MLA_KV_RANK, H, MLA_NOPE), F32)
    wk = jnp.concatenate([wkv[..., :MLA_NOPE], zk], axis=-1).reshape(MLA_KV_RANK, H * LANES).astype(BF16)
    wv_h = wkv[..., MLA_NOPE:]
    even = (jnp.arange(H) % 2 == 0)[None, :, None]
    wv = jnp.concatenate([jnp.where(even, wv_h, 0.0), jnp.where(even, 0.0, wv_h)], axis=-1)
    wv = wv.reshape(MLA_KV_RANK, H * LANES).astype(BF16)
    c1, c0, s0 = tabs
    full2 = lambda shape: pl.BlockSpec(shape, lambda b, i: (0,) * len(shape))
    tab_spec = pl.BlockSpec((tm, LANES), lambda b, i: (i, 0))
    hd_spec = pl.BlockSpec((1, H, tm, LANES), lambda b, i: (b, 0, i, 0))
    hd_shape = jax.ShapeDtypeStruct((bsz, H, seq, LANES), BF16)
    v_spec = pl.BlockSpec((1, H, tm, 2 * LANES), lambda b, i: (b, 0, i, 0))
    v_shape = jax.ShapeDtypeStruct((bsz, H, seq, 2 * LANES), BF16)
    q, k, v = pl.pallas_call(
        _mla_prep_kernel,
        out_shape=(hd_shape, hd_shape, v_shape),
        grid=(bsz, seq // tm),
        in_specs=[pl.BlockSpec((1, tm, 256), lambda b, i: (b, i, C_CQ // 256)),
                  pl.BlockSpec((1, tm, 256), lambda b, i: (b, i, C_KR // 256)),
                  pl.BlockSpec((1, tm, 128), lambda b, i: (b, i, C_CKV // 128)),
                  full2((1, 256)), full2((1, 128)),
                  full2((256, H * LANES)), full2((256, H * LANES)),
                  full2((128, H * LANES)), full2((128, H * LANES)),
                  tab_spec, tab_spec, tab_spec],
        out_specs=(hd_spec, hd_spec, v_spec),
        compiler_params=_cparams(("parallel", "parallel")),
        name="mla_prep",
    )(proj3, proj3, proj3, gq, kv_norm.astype(F32).reshape(1, 128), wqa, wqb, wk, wv, c1, c0, s0)

    return pl.pallas_call(
        functools.partial(_mla_flash_kernel, tq=tq),
        out_shape=jax.ShapeDtypeStruct((bsz, seq, H * MLA_V), BF16),
        grid=(bsz, seq // tq),
        in_specs=[pl.BlockSpec((1, H, tq, LANES), lambda b, i: (b, 0, i, 0)),
                  pl.BlockSpec((1, H, seq, LANES), lambda b, i: (b, 0, 0, 0)),
                  pl.BlockSpec((1, H, seq, 2 * LANES), lambda b, i: (b, 0, 0, 0))],
        out_specs=pl.BlockSpec((1, tq, H * MLA_V), lambda b, i: (b, i, 0)),
        scratch_shapes=[pltpu.VMEM((H, tq, LANES), F32), pltpu.VMEM((H // 2, tq, 2 * LANES), F32)],
        compiler_params=_cparams(("parallel", "arbitrary")),
        name="mla_flash",
    )(q, k, v)


def _nsa_cmp_kernel(x_ref, pea_ref, peb_ref, w1a_ref, w1b_ref, w2_ref, o_ref):
    x = x_ref[0, 0]
    w1a, w1b = w1a_ref[0], w1b_ref[0]
    bias = _dot(pea_ref[0], w1a)[0:1] + _dot(peb_ref[0], w1b)[0:1]
    a = _dot(x, w1a)
    b = _dot(x, w1b)
    n = b.shape[0]
    pre = a + pltpu.roll(b, n - 1, 0) + bias
    o_ref[0, 0] = _dot(jax.nn.gelu(pre).astype(BF16), w2_ref[0]).astype(o_ref.dtype)


def nsa_compress(kv_cr, cmp_pe, cmp_w1, cmp_w2):
    bsz, _, nch, width = kv_cr.shape
    G, dh = NSA_KV_HEADS, NSA_DIM
    half = CMP_BLOCK // 2
    eye = jnp.eye(G, dtype=F32)
    w1r = cmp_w1.astype(F32).reshape(2, CMP_BLOCK, dh, dh)
    w1a = jnp.einsum('kpde,gh->kpgdhe', w1r[:, :half], eye).reshape(2, width, G * dh).astype(BF16)
    w1b = jnp.einsum('kpde,gh->kpgdhe', w1r[:, half:], eye).reshape(2, width, G * dh).astype(BF16)
    w2 = jnp.einsum('kde,gh->kgdhe', cmp_w2.astype(F32), eye).reshape(2, G * dh, G * dh).astype(BF16)
    pe = cmp_pe.astype(F32)
    pe_g = jnp.broadcast_to(pe[:, :, None, :], (2, CMP_BLOCK, G, dh))
    pea = jnp.broadcast_to(pe_g[:, :half].reshape(2, 1, width), (2, 8, width)).astype(BF16)
    peb = jnp.broadcast_to(pe_g[:, half:].reshape(2, 1, width), (2, 8, width)).astype(BF16)
    kvspec = lambda shape: pl.BlockSpec(shape, lambda b, k: (k,) + (0,) * (len(shape) - 1))
    return pl.pallas_call(
        _nsa_cmp_kernel,
        out_shape=jax.ShapeDtypeStruct((bsz, 2, nch, G * dh), BF16),
        grid=(bsz, 2),
        in_specs=[pl.BlockSpec((1, 1, nch, width), lambda b, k: (b, k, 0, 0)),
                  kvspec((1, 8, width)), kvspec((1, 8, width)),
                  kvspec((1, width, G * dh)), kvspec((1, width, G * dh)),
                  kvspec((1, G * dh, G * dh))],
        out_specs=pl.BlockSpec((1, 1, nch, G * dh), lambda b, k: (b, k, 0, 0)),
        compiler_params=_cparams(("parallel", "parallel")),
        name="nsa_compress",
    )(kv_cr, pea, peb, w1a, w1b, w2)


def _nsa_kernel(q_ref, gate_ref, ksl_ref, vsl_ref, kwn_ref, vwn_ref, kvc_ref, bc_ref, bw_ref,
                bs_ref, ov_ref, blk1h_ref, gexp_ref, o_ref, kaug_ref, vaug_ref, m_ref, acc_ref,
                *, tq, tk, nbs, n_sel):
    qi = pl.program_id(1)
    R, G = NSA_REP, NSA_KV_HEADS
    H = R * G
    nsub = tk // tq
    nwin = WINDOW // tq

    @pl.when(qi == 0)
    def _():
        kaug_ref[:, :LANES] = ksl_ref[0]
        kaug_ref[:, LANES:] = blk1h_ref[...]
        vaug_ref[:, :LANES] = vsl_ref[0]
        vaug_ref[:, LANES:] = jnp.ones((vaug_ref.shape[0], LANES), BF16)

    lane = lax.broadcasted_iota(jnp.int32, (tq, LANES), 1)
    t_row = qi * tq + lax.broadcasted_iota(jnp.int32, (H * tq, 1), 0) % tq
    kc = kvc_ref[0, 0]
    vc = kvc_ref[0, 1]
    ov = ov_ref[...]

    def stack(fn):
        return jnp.concatenate([fn(h) for h in range(H)], axis=0)

    q_all = stack(lambda h: q_ref[0, :, h * LANES:(h + 1) * LANES])

    c = jnp.minimum(qi, nwin)
    ws = pl.ds(pl.multiple_of(jnp.maximum(qi - nwin, 0) * tq, tq), (nwin + 1) * tq)
    k_w = kwn_ref[0, ws, :]
    v_w = vwn_ref[0, ws, :]
    o_w = []
    for g in range(G):
        rows = slice(g * R * tq, (g + 1) * R * tq)
        s_w = _dot_t(q_all[rows], k_w) + jnp.concatenate([bw_ref[g * R + r, c] for r in range(R)], axis=0)
        p_w = jnp.exp2(s_w - jnp.max(s_w, axis=-1, keepdims=True))
        o_w.append(_dot(p_w.astype(BF16), v_w) / jnp.sum(p_w, axis=-1, keepdims=True))
    o_w = jnp.concatenate(o_w, axis=0)

    valid = t_row >= (CMP_BLOCK - 1)
    s = _dot_t(q_all, kc) + stack(lambda h: bc_ref[h])
    m = jnp.max(s, axis=-1, keepdims=True)
    p = jnp.where(valid, jnp.exp2(s - m), 0.0)
    l = jnp.where(valid, jnp.sum(p, axis=-1, keepdims=True), 1.0)
    pc = p / l
    o_c = _dot(pc.astype(BF16), vc)
    blk = lax.broadcasted_iota(jnp.int32, (nbs, tq), 0)
    tl = qi * tq + lax.broadcasted_iota(jnp.int32, (nbs, tq), 1)
    cur = tl // SEL_BLOCK
    forced = (blk == 0) | (blk == cur) | (blk == cur - 1)
    future = blk * SEL_BLOCK > tl
    qmask = []
    for g in range(G):
        b0 = g * R * tq
        psum = pc[b0:b0 + tq] + pc[b0 + tq:b0 + 2 * tq] + pc[b0 + 2 * tq:b0 + 3 * tq]
        p_hi = psum.astype(BF16)
        p_lo = (psum - p_hi.astype(F32)).astype(BF16)
        imp = (_dot_t(ov, p_hi) + _dot_t(ov, p_lo))[:nbs]
        imp = jnp.where(forced, FORCE, jnp.where(future, -FORCE, imp))
        rank = jnp.zeros((nbs, tq), F32)
        for i in range(nbs):
            ri = imp[i:i + 1, :]
            beats = (ri > imp) | ((ri == imp) & (blk > i))
            rank = rank + jnp.where(beats, 1.0, 0.0)
        sel = jnp.where(rank < n_sel, 0.0, NEG_INF)
        sel = jnp.concatenate([sel, jnp.zeros((LANES - nbs, tq), F32)], axis=0).T.astype(BF16)
        qmask += [sel] * R
    q_aug = jnp.concatenate([q_all, jnp.concatenate(qmask, axis=0)], axis=1)

    m_ref[...] = jnp.full_like(m_ref, NEG_INF)
    acc_ref[...] = jnp.zeros_like(acc_ref)

    halves = [slice(g * R * tq, (g + 1) * R * tq) for g in range(G)]

    def sel_tile(kt, bias):
        ks = pl.ds(pl.multiple_of(kt * tk, tk), tk)
        k_t = kaug_ref[ks, :]
        v_t = vaug_ref[ks, :]
        s = [_dot_t(q_aug[hs], k_t) for hs in halves]
        if bias is not None:
            s = [x + bias[hs] for x, hs in zip(s, halves)]
        m_prev = [m_ref[hs, :] for hs in halves]
        m_new = [jnp.maximum(mp, jnp.max(x, axis=-1, keepdims=True)) for mp, x in zip(m_prev, s)]
        alpha = [jnp.exp2(mp - mn) for mp, mn in zip(m_prev, m_new)]
        p = [jnp.exp2(x - jnp.tile(mn, (1, tk // LANES))).astype(BF16) for x, mn in zip(s, m_new)]
        pv = [_dot(pp, v_t) for pp in p]
        for hs, mn, a, o in zip(halves, m_new, alpha, pv):
            m_ref[hs, :] = mn
            acc_ref[hs, :] = acc_ref[hs, :] * jnp.tile(a, (1, 2)) + o

    def near_bias(kt):
        cols = []
        for sub in range(nsub):
            d = qi - (kt * nsub + sub)
            cols.append(stack(lambda h: jnp.where(d == 0, bs_ref[h, 0], jnp.where(
                d == 1, bs_ref[h, 1], jnp.where(d < 0, NEG_INF, 0.0)))))
        return jnp.concatenate(cols, axis=1)

    def far_body(kt, c):
        sel_tile(kt, None)
        return c

    kd = (qi * tq) // tk
    lax.fori_loop(0, jnp.maximum(kd - 1, 0), far_body, 0)

    @pl.when(kd >= 1)
    def _():
        sel_tile(kd - 1, near_bias(kd - 1))

    sel_tile(kd, near_bias(kd))
    o_s = acc_ref[:, :LANES] / acc_ref[:, LANES:]

    gates = _sigmoid(_dot(gate_ref[0], gexp_ref[...]))
    for r in range(R):
        res = None
        for b, o_b in enumerate((o_c, o_s, o_w)):
            o_br = jnp.where(lane < NSA_DIM, o_b[r * tq:(r + 1) * tq], o_b[(R + r) * tq:(R + r + 1) * tq])
            term = gates[:, (b * R + r) * LANES:(b * R + r + 1) * LANES] * o_br
            res = term if res is None else res + term
        o_ref[0, :, r * LANES:(r + 1) * LANES] = res.astype(o_ref.dtype)


def _t5_bucket(dist):
    n = jnp.maximum(dist, 0)
    exact = REL_BUCKETS // 2
    nf = jnp.maximum(n, exact).astype(F32)
    large = exact + jnp.floor(jnp.log(nf / exact) / math.log(REL_MAX_DIST / exact)
                              * (REL_BUCKETS - exact)).astype(jnp.int32)
    return jnp.where(n < exact, n, jnp.minimum(large, REL_BUCKETS - 1))


def _nsa_bias_tables(rel_bias, seq, tq):
    rb = rel_bias.astype(F32).T
    far = rb[:, REL_BUCKETS - 1].reshape(NSA_HEADS, 1, 1)

    def by_dist(dist, ok, shift=0.0):
        bucket = _t5_bucket(dist)[None]
        out = jnp.zeros((NSA_HEADS,) + dist.shape, F32)
        for k in range(REL_BUCKETS):
            out = jnp.where(bucket == k, rb[:, k].reshape((NSA_HEADS,) + (1,) * dist.ndim), out)
        return jnp.where(ok[None], (out - shift) * LOG2E, NEG_INF)

    i = jnp.arange(tq)[:, None]
    t = jnp.arange(seq)[:, None]
    dist_c = t - (jnp.arange(LANES)[None, :] * CMP_STRIDE + CMP_BLOCK - 1)
    bc = by_dist(dist_c, dist_c >= 0)
    nwin = WINDOW // tq
    jw = jnp.arange((nwin + 1) * tq)[None, :]
    bw = jnp.stack([by_dist(tq * c + i - jw, (tq * c + i - jw >= 0) & (tq * c + i - jw < WINDOW))
                    for c in range(nwin + 1)], axis=1)
    js = jnp.arange(tq)[None, :]
    bs = jnp.stack([by_dist(tq * c + i - js, tq * c + i - js >= 0, far) for c in range(2)], axis=1)
    ci = jnp.arange(LANES)[:, None]
    sj = jnp.arange(LANES)[None, :]
    nbs = seq // SEL_BLOCK
    ov = ((ci * CMP_STRIDE <= sj * SEL_BLOCK + SEL_BLOCK - 1)
          & (ci * CMP_STRIDE + CMP_BLOCK - 1 >= sj * SEL_BLOCK)
          & (ci < seq // CMP_STRIDE - 1) & (sj < nbs))
    blk1h = (jnp.arange(seq)[:, None] // SEL_BLOCK == sj).astype(BF16)
    col = jnp.arange(3 * NSA_REP * LANES)
    slab, lane_g = col // LANES, (col % LANES) // NSA_DIM
    gate_col = (slab // NSA_REP) * NSA_HEADS + lane_g * NSA_REP + slab % NSA_REP
    gexp = (ci == gate_col[None, :]).astype(BF16)
    return bc, bw, bs, ov.T.astype(BF16), blk1h, gexp


def nsa_mixer(proj3, kvc, tables, *, tq, tk):
    bsz, seq, _ = proj3.shape
    assert tq % LANES == 0 and tq >= REL_MAX_DIST and tk % tq == 0 and seq % tk == 0
    assert seq // CMP_STRIDE == LANES and WINDOW % tq == 0
    nwin = WINDOW // tq
    bc, bw, bs, ov, blk1h, gexp = tables
    nbs = seq // SEL_BLOCK
    H = NSA_HEADS
    slab = lambda j: pl.BlockSpec((1, seq, LANES), lambda b, i: (b, 0, C_NKV // LANES + j))
    const = lambda shape: pl.BlockSpec(shape, lambda b, i: (0,) * len(shape))
    return pl.pallas_call(
        functools.partial(_nsa_kernel, tq=tq, tk=tk, nbs=nbs, n_sel=min(SEL_TOPN, nbs)),
        out_shape=jax.ShapeDtypeStruct((bsz, seq, H * NSA_DIM), BF16),
        grid=(bsz, seq // tq),
        in_specs=[pl.BlockSpec((1, tq, H * LANES), lambda b, i: (b, i, C_NQ // (H * LANES))),
                  pl.BlockSpec((1, tq, LANES), lambda b, i: (b, i, C_GATE // LANES)),
                  slab(2), slab(3), slab(4), slab(5),
                  pl.BlockSpec((1, 2, LANES, LANES), lambda b, i: (b, 0, 0, 0)),
                  pl.BlockSpec((H, tq, LANES), lambda b, i: (0, i, 0)),
                  const((H, nwin + 1, tq, (nwin + 1) * tq)), const((H, 2, tq, tq)),
                  const((LANES, LANES)), const((seq, LANES)), const(gexp.shape)],
        out_specs=pl.BlockSpec((1, tq, H * NSA_DIM), lambda b, i: (b, i, 0)),
        scratch_shapes=[pltpu.VMEM((seq, 2 * LANES), BF16), pltpu.VMEM((seq, 2 * LANES), BF16),
                        pltpu.VMEM((H * tq, LANES), F32), pltpu.VMEM((H * tq, 2 * LANES), F32)],
        compiler_params=_cparams(("parallel", "arbitrary")),
        name="nsa_attention",
    )(proj3, proj3, proj3, proj3, proj3, proj3, kvc, bc, bw, bs, ov, blk1h, gexp)


def _mix_xattn_kernel(ys_ref, ym_ref, yn_ref, h_ref, g1_ref, g2_ref, g3_ref, w1_ref, w2_ref, w3_ref,
                      kv_ref, g_ref, wq_ref, wo_ref, o_ref, *, dh):
    def part(y_ref, gy_ref, w_ref):
        y = y_ref[0].astype(F32)
        return _dot(_rms(y, gy_ref[...], y.shape[-1]).astype(BF16), w_ref[...])

    h = h_ref[0] + part(ys_ref, g1_ref, w1_ref) + part(ym_ref, g2_ref, w2_ref) + part(yn_ref, g3_ref, w3_ref)
    xn = _rms(h, g_ref[...], h.shape[-1]).astype(BF16)
    q = _dot(xn, wq_ref[...]).astype(BF16)
    hw = XATTN_HEADS * dh
    heads = range(XATTN_HEADS)
    s = [_dot_t(q[:, hd * dh:(hd + 1) * dh], kv_ref[0, :, hd * dh:(hd + 1) * dh]) for hd in heads]
    p = [jnp.exp2(x - jnp.max(x, axis=-1, keepdims=True)) for x in s]
    p = [(x / jnp.sum(x, axis=-1, keepdims=True)).astype(BF16) for x in p]
    outs = [_dot(p[hd], kv_ref[0, :, hw + hd * dh:hw + (hd + 1) * dh]).astype(BF16) for hd in heads]
    o = jnp.concatenate(outs, axis=-1)
    o_ref[0] = h + _dot(o, wo_ref[...])


def mix_out_cross_attention(ys3, h3, gains, weights, kv3, g_x, wq, wo, *, tm):
    bsz, seq, d = h3.shape
    m = kv3.shape[1]
    dh = d // XATTN_HEADS
    wq_s = (wq.astype(F32) * (dh ** -0.5 * LOG2E)).astype(BF16)
    gains = [g.reshape(1, -1).astype(F32) for g in gains]
    const = lambda shape: pl.BlockSpec(shape, lambda b, i: (0,) * len(shape))
    row = lambda w: pl.BlockSpec((1, tm, w), lambda b, i: (b, i, 0))
    return pl.pallas_call(
        functools.partial(_mix_xattn_kernel, dh=dh),
        out_shape=jax.ShapeDtypeStruct((bsz, seq, d), F32),
        grid=(bsz, seq // tm),
        in_specs=[row(y.shape[2]) for y in ys3] + [row(d)]
                 + [const(g.shape) for g in gains] + [const(w.shape) for w in weights]
                 + [pl.BlockSpec((1, m, 2 * d), lambda b, i: (b, 0, 0)),
                    const((1, d)), const((d, d)), const((d, d))],
        out_specs=row(d),
        compiler_params=_cparams(("parallel", "parallel")),
        name="mix_out_cross_attention",
    )(*ys3, h3, *gains, *weights, kv3, g_x.reshape(1, d).astype(F32), wq_s, wo.astype(BF16))


def _ffn_kernel(h_ref, g_ref, wg_ref, wu_ref, wd_ref, o_ref, xn_ref, acc_ref):
    j = pl.program_id(1)

    @pl.when(j == 0)
    def _():
        h = h_ref[...]
        xn_ref[...] = _rms(h, g_ref[...], h.shape[-1]).astype(BF16)
        acc_ref[...] = h

    _swiglu_accumulate(xn_ref, wg_ref[...], wu_ref[...], wd_ref[...], acc_ref)

    @pl.when(j == pl.num_programs(1) - 1)
    def _():
        o_ref[...] = acc_ref[...]


def dense_ffn(h, g, wg, wu, wd, *, tm, tf):
    m, d = h.shape
    ff = wg.shape[1]
    return pl.pallas_call(
        _ffn_kernel,
        out_shape=jax.ShapeDtypeStruct((m, d), F32),
        grid=(m // tm, ff // tf),
        in_specs=[pl.BlockSpec((tm, d), lambda i, j: (i, 0)),
                  pl.BlockSpec((1, d), lambda i, j: (0, 0)),
                  pl.BlockSpec((d, tf), lambda i, j: (0, j)),
                  pl.BlockSpec((d, tf), lambda i, j: (0, j)),
                  pl.BlockSpec((tf, d), lambda i, j: (j, 0))],
        out_specs=pl.BlockSpec((tm, d), lambda i, j: (i, 0)),
        scratch_shapes=[pltpu.VMEM((tm, d), BF16), pltpu.VMEM((tm, d), F32)],
        compiler_params=_cparams(("parallel", "arbitrary")),
        name="dense_ffn",
    )(h, g.reshape(1, d).astype(F32), wg.astype(BF16), wu.astype(BF16), wd.astype(BF16))


def _router_kernel(h_ref, g_ref, wr_hi_ref, wr_lo_ref, xn_ref, info_ref, cnt_ref, carry_ref, *, tm):
    i = pl.program_id(0)

    @pl.when(i == 0)
    def _():
        carry_ref[...] = jnp.zeros_like(carry_ref)

    h = h_ref[...]
    xn = _rms(h, g_ref[...], h.shape[-1])
    xn_ref[...] = xn
    x_hi = xn.astype(BF16)
    x_lo = (xn - x_hi.astype(F32)).astype(BF16)
    logits = _dot(x_hi, wr_hi_ref[...]) + _dot(x_lo, wr_hi_ref[...]) + _dot(x_hi, wr_lo_ref[...])
    lane = lax.broadcasted_iota(jnp.int32, (tm, LANES), 1)
    lanef = lane.astype(F32)
    logits = jnp.where(lane < N_EXPERTS, logits, NEG_INF)
    m1 = jnp.max(logits, axis=-1, keepdims=True)
    i1 = jnp.min(jnp.where(logits == m1, lanef, float(LANES)), axis=-1, keepdims=True)
    rest = jnp.where(lanef == i1, NEG_INF, logits)
    m2 = jnp.max(rest, axis=-1, keepdims=True)
    i2 = jnp.min(jnp.where(rest == m2, lanef, float(LANES)), axis=-1, keepdims=True)
    e2 = jnp.exp(m2 - m1)
    w1 = 1.0 / (1.0 + e2)
    w2 = e2 / (1.0 + e2)
    oh1 = lanef == i1
    oh2 = lanef == i2
    oh = jnp.where(oh1 | oh2, 1.0, 0.0)
    rr = lax.broadcasted_iota(jnp.int32, (tm, tm), 0)
    cc = lax.broadcasted_iota(jnp.int32, (tm, tm), 1)
    tri = jnp.where(cc < rr, 1.0, 0.0).astype(BF16)
    before = _dot(tri, oh.astype(BF16)) + carry_ref[0:1, :]
    r1 = jnp.sum(jnp.where(oh1, before, 0.0), axis=-1, keepdims=True)
    r2 = jnp.sum(jnp.where(oh2, before, 0.0), axis=-1, keepdims=True)
    carry_ref[...] = carry_ref[...] + jnp.sum(oh, axis=0, keepdims=True)
    info = jnp.where(lane == 0, i1, jnp.where(lane == 1, i2, jnp.where(lane == 2, w1, jnp.where(
        lane == 3, w2, jnp.where(lane == 4, r1, jnp.where(lane == 5, r2, 0.0))))))
    info_ref[...] = info
    cnt_ref[...] = carry_ref[...]


def moe_router(h, g, router, *, tm):
    m, d = h.shape
    wr = jnp.pad(router.astype(F32), ((0, 0), (0, LANES - N_EXPERTS)))
    wr_hi = wr.astype(BF16)
    wr_lo = (wr - wr_hi.astype(F32)).astype(BF16)
    return pl.pallas_call(
        functools.partial(_router_kernel, tm=tm),
        out_shape=(jax.ShapeDtypeStruct((m, d), F32), jax.ShapeDtypeStruct((m, LANES), F32),
                   jax.ShapeDtypeStruct((8, LANES), F32)),
        grid=(m // tm,),
        in_specs=[pl.BlockSpec((tm, d), lambda i: (i, 0)),
                  pl.BlockSpec((1, d), lambda i: (0, 0)),
                  pl.BlockSpec((d, LANES), lambda i: (0, 0)),
                  pl.BlockSpec((d, LANES), lambda i: (0, 0))],
        out_specs=(pl.BlockSpec((tm, d), lambda i: (i, 0)),
                   pl.BlockSpec((tm, LANES), lambda i: (i, 0)),
                   pl.BlockSpec((8, LANES), lambda i: (0, 0))),
        scratch_shapes=[pltpu.VMEM((8, LANES), F32)],
        compiler_params=_cparams(("arbitrary",)),
        name="moe_router",
    )(h, g.reshape(1, d).astype(F32), wr_hi, wr_lo)


def _row_copy(src_hbm, row, dst, slot, sem):
    return pltpu.make_async_copy(src_hbm.at[pl.ds(row, 1), :], dst.at[pl.ds(slot, 1), :], sem)


def _rows_wait(src_hbm, dst, sem):
    pltpu.make_async_copy(src_hbm.at[pl.ds(0, dst.shape[0]), :], dst, sem).wait()


def _moe_ffn_kernel(src_ref, texp_ref, nact_ref, x_hbm, wg_ref, wu_ref, wd_ref, o_ref,
                    xbuf, xbf, acc_ref, sem, *, tm, nj):
    i = pl.program_id(0)
    j = pl.program_id(1)
    nact = nact_ref[0]
    active = i < nact
    nbuf = xbuf.shape[0]
    ahead = nbuf - 1
    cur = i % nbuf
    rows_per_step = tm // nj

    for t0 in range(ahead):
        @pl.when((i == 0) & (j == 0) & (t0 < nact))
        def _():
            def issue(s, c):
                _row_copy(x_hbm, src_ref[t0 * tm + s], xbuf.at[t0], s, sem.at[t0]).start()
                return c

            lax.fori_loop(0, tm, issue, 0, unroll=8)

    @pl.when(active & (j == 0))
    def _():
        _rows_wait(x_hbm, xbuf.at[cur], sem.at[cur])
        xbf[...] = xbuf[cur].astype(BF16)
        acc_ref[...] = jnp.zeros_like(acc_ref)

    def compute(prefetch):
        if prefetch:
            nxt = (i + ahead) % nbuf
            base = (i + ahead) * tm + j * rows_per_step
            for k in range(rows_per_step):
                _row_copy(x_hbm, src_ref[base + k], xbuf.at[nxt], j * rows_per_step + k, sem.at[nxt]).start()
        _swiglu_accumulate(xbf, wg_ref[0], wu_ref[0], wd_ref[0], acc_ref)

    @pl.when(i + ahead < nact)
    def _():
        compute(True)

    @pl.when(active & (i + ahead >= nact))
    def _():
        compute(False)

    @pl.when(j == nj - 1)
    def _():
        o_ref[...] = jnp.where(active, acc_ref[...], 0.0)


def moe_expert_ffn(xn, src, tile_expert, n_active, wg, wu, wd, *, tm, tf):
    n_slots = src.shape[0]
    d = xn.shape[1]
    ne, _, ff = wg.shape
    nj = ff // tf
    assert nj * tf == ff and tm % nj == 0
    wg, wu, wd = wg.astype(BF16), wu.astype(BF16), wd.astype(BF16)

    def wmap_col(i, j, src, texp, nact):
        return (texp[i], 0, jnp.where(i < nact[0], j, nj - 1))

    def wmap_row(i, j, src, texp, nact):
        return (texp[i], jnp.where(i < nact[0], j, nj - 1), 0)

    return pl.pallas_call(
        functools.partial(_moe_ffn_kernel, tm=tm, nj=nj),
        out_shape=jax.ShapeDtypeStruct((n_slots, d), F32),
        grid_spec=pltpu.PrefetchScalarGridSpec(
            num_scalar_prefetch=3,
            grid=(n_slots // tm, nj),
            in_specs=[pl.BlockSpec(memory_space=pl.ANY),
                      pl.BlockSpec((1, d, tf), wmap_col),
                      pl.BlockSpec((1, d, tf), wmap_col),
                      pl.BlockSpec((1, tf, d), wmap_row)],
            out_specs=pl.BlockSpec((tm, d), lambda i, j, *_: (i, 0)),
            scratch_shapes=[pltpu.VMEM((3, tm, d), F32), pltpu.VMEM((tm, d), BF16),
                            pltpu.VMEM((tm, d), F32), pltpu.SemaphoreType.DMA((3,))]),
        compiler_params=_cparams(("arbitrary", "arbitrary")),
        name="moe_expert_ffn",
    )(src, tile_expert, n_active, xn, wg, wu, wd)


def _moe_combine_kernel(pos_ref, h_ref, info_ref, ys_hbm, g_ref, o_ref, buf, sem, *, tm, final_norm):
    i = pl.program_id(0)
    n = pl.num_programs(0)
    cur = i % 2

    def start_gather(tile, b):
        for s in range(tm):
            for k in range(2):
                _row_copy(ys_hbm, pos_ref[2 * (tile * tm + s) + k], buf.at[b, k], s, sem.at[b]).start()

    @pl.when(i == 0)
    def _():
        start_gather(0, 0)

    @pl.when(i + 1 < n)
    def _():
        start_gather(i + 1, 1 - cur)

    for k in range(2):
        _rows_wait(ys_hbm, buf.at[cur, k], sem.at[cur])
    info = info_ref[...]
    y = h_ref[...] + info[:, 2:3] * buf[cur, 0] + info[:, 3:4] * buf[cur, 1]
    if final_norm:
        y = _rms(y, g_ref[...], y.shape[-1])
    o_ref[...] = y


def moe_combine(h, info, ys, pos_flat, g_final, *, tm, final_norm):
    m, d = h.shape
    return pl.pallas_call(
        functools.partial(_moe_combine_kernel, tm=tm, final_norm=final_norm),
        out_shape=jax.ShapeDtypeStruct((m, d), F32),
        grid_spec=pltpu.PrefetchScalarGridSpec(
            num_scalar_prefetch=1,
            grid=(m // tm,),
            in_specs=[pl.BlockSpec((tm, d), lambda i, *_: (i, 0)),
                      pl.BlockSpec((tm, LANES), lambda i, *_: (i, 0)),
                      pl.BlockSpec(memory_space=pl.ANY),
                      pl.BlockSpec((1, d), lambda i, *_: (0, 0))],
            out_specs=pl.BlockSpec((tm, d), lambda i, *_: (i, 0)),
            scratch_shapes=[pltpu.VMEM((2, 2, tm, d), F32), pltpu.SemaphoreType.DMA((2,))]),
        compiler_params=_cparams(("arbitrary",)),
        name="moe_combine",
    )(pos_flat, h, info, ys, g_final.reshape(1, d).astype(F32))


def moe_layer(h, g, router, wg, wu, wd, g_final, *, final_norm, tm_r=512, tm_g=512, tf=1792, tm_c=256):
    m, d = h.shape
    xn, info, cnt = moe_router(h, g, router, tm=tm_r)
    e_idx = info[:, 0:2].astype(jnp.int32)
    rank = info[:, 4:6].astype(jnp.int32)
    counts = cnt[0, :N_EXPERTS].astype(jnp.int32)
    tiles_per = (counts + tm_g - 1) // tm_g
    tile_end = jnp.cumsum(tiles_per)
    seg_start = (tile_end - tiles_per) * tm_g
    pos = rank
    for e in range(N_EXPERTS):
        pos = pos + jnp.where(e_idx == e, seg_start[e], 0)
    n_tiles = (2 * m) // tm_g + N_EXPERTS
    n_slots = n_tiles * tm_g
    tok = jnp.broadcast_to(jnp.arange(m, dtype=jnp.int32)[:, None], (m, 2))
    src = jnp.zeros((n_slots,), jnp.int32).at[pos.reshape(-1)].set(tok.reshape(-1))
    n_active = tile_end[-1:].astype(jnp.int32)
    tile_ids = jnp.minimum(jnp.arange(n_tiles, dtype=jnp.int32), n_active[0] - 1)
    tile_expert = jnp.sum(tile_ids[:, None] >= tile_end[None, :], axis=1).astype(jnp.int32)
    ys = moe_expert_ffn(xn, src, tile_expert, n_active,
                        wg, wu, wd, tm=tm_g, tf=tf)
    return moe_combine(h, info, ys, pos.reshape(-1).astype(jnp.int32), g_final, tm=tm_c, final_norm=final_norm)


def _final_norm_kernel(h_ref, g_ref, o_ref):
    h = h_ref[...]
    o_ref[...] = _rms(h, g_ref[...], h.shape[-1])


def final_rmsnorm(h, g, *, tm):
    m, d = h.shape
    return pl.pallas_call(
        _final_norm_kernel,
        out_shape=jax.ShapeDtypeStruct((m, d), F32),
        grid=(m // tm,),
        in_specs=[pl.BlockSpec((tm, d), lambda i: (i, 0)), pl.BlockSpec((1, d), lambda i: (0, 0))],
        out_specs=pl.BlockSpec((tm, d), lambda i: (i, 0)),
        compiler_params=_cparams(("parallel",)),
        name="final_rmsnorm",
    )(h, g.reshape(1, d).astype(F32))


def _pack_w_in(w):
    d = w.shape[0]
    w = w.astype(F32)
    o = 0
    u = w[:, o:o + SSM_WIDTH]; o += SSM_WIDTH
    cq = w[:, o:o + MLA_Q_RANK]; o += MLA_Q_RANK
    ckv = w[:, o:o + MLA_KV_RANK]; o += MLA_KV_RANK
    kr = w[:, o:o + MLA_ROPE]; o += MLA_ROPE
    nq = w[:, o:o + NSA_HEADS * NSA_DIM]; o += NSA_HEADS * NSA_DIM
    nkv = w[:, o:o + 6 * NSA_KV_HEADS * NSA_DIM]; o += 6 * NSA_KV_HEADS * NSA_DIM
    gate = w[:, o:o + 3 * NSA_HEADS]
    z = lambda n: jnp.zeros((d, n), F32)
    kr_a = jnp.concatenate([z(MLA_NOPE), kr, z(LANES - MLA_NOPE - MLA_ROPE)], axis=1)
    kr_b = jnp.concatenate([z(MLA_NOPE), _rot_half_cols(kr), z(LANES - MLA_NOPE - MLA_ROPE)], axis=1)
    nq_h = (nq * (NSA_DIM ** -0.5 * LOG2E)).reshape(d, NSA_KV_HEADS, NSA_REP, NSA_DIM)
    zq = jnp.zeros((d, NSA_REP, NSA_DIM), F32)
    nq_p = jnp.concatenate([
        jnp.concatenate([nq_h[:, 0], zq], axis=-1).reshape(d, NSA_REP * LANES),
        jnp.concatenate([zq, nq_h[:, 1]], axis=-1).reshape(d, NSA_REP * LANES)], axis=1)
    packed = jnp.concatenate([u, cq, z(256 - MLA_Q_RANK), kr_a, kr_b, nq_p, nkv, ckv,
                              gate, z(LANES - 3 * NSA_HEADS)], axis=1)
    assert packed.shape[1] == IN_COLS_PACKED
    return packed.astype(BF16)


def _rg_order(a):
    rest = a.shape[1:]
    return a.reshape((NSA_KV_HEADS, NSA_REP, NSA_DIM) + rest).swapaxes(0, 1).reshape((-1,) + rest)


def kernel(x, mem, w_in, w_out, mix_norm, out_norm, ssm_a_re, ssm_a_im, ssm_b_re, ssm_b_im, ssm_c_re, ssm_c_im, ssm_d, ssm_log_dt, ssm_w_glu, mla_q_norm, mla_w_uq, mla_kv_norm, mla_w_ukv, nsa_cmp_pe, nsa_cmp_w1, nsa_cmp_w2, rel_bias, xattn_norm, mem_norm, xattn_wq, xattn_wkv, xattn_wo, ffn_norm, dense_w_gate, dense_w_up, dense_w_down, moe_router, moe_w_gate, moe_w_up, moe_w_down, final_norm):
    bsz, seq, d = x.shape
    depth = w_in.shape[0]
    T = bsz * seq
    nmem = mem.shape[1]
    tq_nsa, tk_nsa = 2 * LANES, 4 * LANES
    rope_tabs = _rope_tables(seq)
    nsa_tabs = _nsa_bias_tables(rel_bias, seq, tq_nsa)
    o1, o2 = SSM_WIDTH, SSM_WIDTH + MLA_HEADS * MLA_V
    mem2 = mem.reshape(bsz * nmem, d)
    h = x.reshape(T, d)
    for l in range(depth):
        proj = norm_matmul(h, mix_norm[l], _pack_w_in(w_in[l]), tm=512, tn=IN_COLS_PACKED, out_dtype=BF16)
        proj3 = proj.reshape(bsz, seq, IN_COLS_PACKED)
        u_tm = proj3[:, :, C_U:C_U + SSM_WIDTH].transpose(1, 0, 2).reshape(seq * bsz, SSM_WIDTH)
        y_ssm = ssm_mixer(u_tm, ssm_a_re[l], ssm_a_im[l], ssm_b_re[l], ssm_b_im[l], ssm_c_re[l], ssm_c_im[l],
                          ssm_d[l], ssm_log_dt[l], ssm_w_glu[l], nb=bsz, tc=64)
        y_ssm = y_ssm.reshape(seq, bsz, SSM_WIDTH).transpose(1, 0, 2)
        y_mla = mla_mixer(proj3, mla_q_norm[l], mla_w_uq[l], mla_kv_norm[l], mla_w_ukv[l], rope_tabs,
                          tm=512, tq=512)
        nch = seq // CMP_STRIDE
        kv_cr = jnp.stack([proj3[:, :, C_NKV:C_NKV + LANES].reshape(bsz, nch, CMP_STRIDE * LANES),
                           proj3[:, :, C_NKV + LANES:C_NKV + 2 * LANES].reshape(bsz, nch, CMP_STRIDE * LANES)],
                          axis=1)
        kvc = nsa_compress(kv_cr, nsa_cmp_pe[l], nsa_cmp_w1[l], nsa_cmp_w2[l])
        y_nsa = nsa_mixer(proj3, kvc, nsa_tabs, tq=tq_nsa, tk=tk_nsa)
        g_out = out_norm[l]
        wo_l = w_out[l]
        kv_mem = norm_matmul(mem2, mem_norm[l], xattn_wkv[l].astype(BF16), tm=256, tn=512, out_dtype=BF16)
        h = mix_out_cross_attention(
            [y_ssm, y_mla, y_nsa], h.reshape(bsz, seq, d),
            [g_out[:o1], g_out[o1:o2], _rg_order(g_out[o2:])],
            [wo_l[:o1].astype(BF16), wo_l[o1:o2].astype(BF16), _rg_order(wo_l[o2:]).astype(BF16)],
            kv_mem.reshape(bsz, nmem, 2 * d), xattn_norm[l], xattn_wq[l], xattn_wo[l], tm=256).reshape(T, d)
        last = l == depth - 1
        if l % 2 == 0:
            h = dense_ffn(h, ffn_norm[l], dense_w_gate[l // 2], dense_w_up[l // 2], dense_w_down[l // 2],
                          tm=512, tf=1408)
            if last:
                h = final_rmsnorm(h, final_norm, tm=512)
        else:
            h = moe_layer(h, ffn_norm[l], moe_router[l // 2], moe_w_gate[l // 2], moe_w_up[l // 2],
                          moe_w_down[l // 2], final_norm, final_norm=last)
    return h.reshape(bsz, seq, d)
```

```python
import functools
import math

import jax
import jax.numpy as jnp
from jax import lax
from jax.experimental import pallas as pl
from jax.experimental.pallas import tpu as pltpu

F32 = jnp.float32
BF16 = jnp.bfloat16

HEAD_DIM = 64
SSM_WIDTH = 256
SSM_CH = 16
SSM_GROUPS = 16
SSM_STATE = 64
MLA_HEADS = 6
MLA_NOPE = 64
MLA_ROPE = 32
MLA_V = 64
MLA_Q_RANK = 192
MLA_KV_RANK = 128
NSA_HEADS = 6
NSA_KV_HEADS = 2
NSA_REP = 3
NSA_DIM = 64
CMP_BLOCK = 32
CMP_STRIDE = 16
SEL_BLOCK = 64
SEL_TOPN = 8
WINDOW = 256
REL_BUCKETS = 32
REL_MAX_DIST = 128
XATTN_HEADS = 4
N_EXPERTS = 8
ROPE_THETA = 10000.0
EPS = 1e-6
NEG_INF = -1e30
FORCE = 1e9
LOG2E = math.log2(math.e)

LANES = 128
VMEM_LIMIT = 56 * 1024 * 1024

C_U, C_CQ, C_KR, C_NQ, C_NKV, C_CKV, C_GATE = 0, 256, 512, 768, 1536, 2304, 2432
IN_COLS_PACKED = 2560


def _cparams(sem):
    return pltpu.CompilerParams(dimension_semantics=sem, vmem_limit_bytes=VMEM_LIMIT)


def _dot(a, b):
    return jnp.dot(a, b, preferred_element_type=F32)


def _dot_t(a, b):
    return lax.dot_general(a, b, (((1,), (1,)), ((), ())), preferred_element_type=F32)


def _rms(x, g, n):
    ms = jnp.sum(x * x, axis=-1, keepdims=True) * (1.0 / n)
    return x * lax.rsqrt(ms + EPS) * g


def _sigmoid(x):
    return 1.0 / (1.0 + jnp.exp(-x))


def _silu(x):
    return x * _sigmoid(x)


def _swiglu_accumulate(x_ref, wg, wu, wd, acc_ref, parts=2):
    rows = x_ref.shape[0] // parts
    sl = [slice(k * rows, (k + 1) * rows) for k in range(parts)]
    x = [x_ref[s, :] for s in sl]
    g = [_dot(xk, wg) for xk in x]
    u = [_dot(xk, wu) for xk in x]
    a = [(_silu(gk) * uk).astype(BF16) for gk, uk in zip(g, u)]
    d = [_dot(ak, wd) for ak in a]
    for s, dk in zip(sl, d):
        acc_ref[s, :] += dk


def _norm_mm_kernel(x_ref, g_ref, w_ref, o_ref, xn_ref):
    @pl.when(pl.program_id(1) == 0)
    def _():
        x = x_ref[...].astype(F32)
        xn_ref[...] = _rms(x, g_ref[...], x.shape[-1]).astype(BF16)

    o_ref[...] = _dot(xn_ref[...], w_ref[...]).astype(o_ref.dtype)


def norm_matmul(x, g, w, *, tm, tn, out_dtype):
    m, k = x.shape
    n = w.shape[1]
    return pl.pallas_call(
        _norm_mm_kernel,
        out_shape=jax.ShapeDtypeStruct((m, n), out_dtype),
        grid=(m // tm, n // tn),
        in_specs=[pl.BlockSpec((tm, k), lambda i, j: (i, 0)),
                  pl.BlockSpec((1, k), lambda i, j: (0, 0)),
                  pl.BlockSpec((k, tn), lambda i, j: (0, j))],
        out_specs=pl.BlockSpec((tm, tn), lambda i, j: (i, j)),
        scratch_shapes=[pltpu.VMEM((tm, k), BF16)],
        compiler_params=_cparams(("parallel", "arbitrary")),
        name="norm_matmul",
    )(x, g.reshape(1, k).astype(F32), w)


def _ssm_kernel(u_ref, bbr_ref, bbi_ref, ar_ref, ai_ref, ccr_ref, cci_ref, d_ref, wglu_ref,
                o_ref, hr_ref, hi_ref, cr_ref, ci_ref, *, tc, nb):
    @pl.when(pl.program_id(0) == 0)
    def _():
        cr_ref[...] = jnp.zeros_like(cr_ref)
        ci_ref[...] = jnp.zeros_like(ci_ref)

    u = u_ref[...]
    hr_ref[...] = _dot(u, bbr_ref[...])
    hi_ref[...] = _dot(u, bbi_ref[...])
    gp = ar_ref.shape[-1]
    ar = jnp.broadcast_to(ar_ref[...], (nb, gp))
    ai = jnp.broadcast_to(ai_ref[...], (nb, gp))

    def step(t, carry):
        hr, hi = carry
        rows = pl.ds(pl.multiple_of(t * nb, nb), nb)
        nr = ar * hr - ai * hi + hr_ref[rows, :]
        ni = ar * hi + ai * hr + hi_ref[rows, :]
        hr_ref[rows, :] = nr
        hi_ref[rows, :] = ni
        return nr, ni

    hr, hi = lax.fori_loop(0, tc, step, (cr_ref[...], ci_ref[...]))
    cr_ref[...] = hr
    ci_ref[...] = hi
    y = (_dot(hr_ref[...].astype(BF16), ccr_ref[...]) + _dot(hi_ref[...].astype(BF16), cci_ref[...])
         + d_ref[...] * u.astype(F32))
    y = jax.nn.gelu(y)
    z = _dot(y.astype(BF16), wglu_ref[...])
    o_ref[...] = (y * _sigmoid(z)).astype(o_ref.dtype)


def ssm_mixer(u_tm, a_re, a_im, b_re, b_im, c_re, c_im, d, log_dt, w_glu, *, nb, tc):
    rows = u_tm.shape[0]
    G, P, C = SSM_GROUPS, SSM_STATE, SSM_CH
    dt = jnp.exp(log_dt.astype(F32))[:, None]
    lr, li = a_re.astype(F32), a_im.astype(F32)
    mag = jnp.exp(lr * dt)
    ab_r, ab_i = mag * jnp.cos(li * dt), mag * jnp.sin(li * dt)
    den = lr * lr + li * li
    nr = ab_r - 1.0
    f_r = (nr * lr + ab_i * li) / den
    f_i = (ab_i * lr - nr * li) / den
    br, bi = b_re.astype(F32), b_im.astype(F32)
    bb_r = f_r[..., None] * br - f_i[..., None] * bi
    bb_i = f_r[..., None] * bi + f_i[..., None] * br
    eye = jnp.eye(G, dtype=F32)
    bbr = jnp.einsum('gpc,gh->gchp', bb_r, eye).reshape(G * C, G * P).astype(BF16)
    bbi = jnp.einsum('gpc,gh->gchp', bb_i, eye).reshape(G * C, G * P).astype(BF16)
    ccr = jnp.einsum('gcp,gh->gphc', c_re.astype(F32), eye).reshape(G * P, G * C).astype(BF16)
    cci = jnp.einsum('gcp,gh->gphc', -c_im.astype(F32), eye).reshape(G * P, G * C).astype(BF16)
    gp = G * P
    full = lambda shape: pl.BlockSpec(shape, lambda i: (0,) * len(shape))
    return pl.pallas_call(
        functools.partial(_ssm_kernel, tc=tc, nb=nb),
        out_shape=jax.ShapeDtypeStruct((rows, SSM_WIDTH), BF16),
        grid=(rows // (tc * nb),),
        in_specs=[pl.BlockSpec((tc * nb, SSM_WIDTH), lambda i: (i, 0)),
                  full((G * C, gp)), full((G * C, gp)), full((1, gp)), full((1, gp)),
                  full((gp, G * C)), full((gp, G * C)), full((1, SSM_WIDTH)),
                  full((SSM_WIDTH, SSM_WIDTH))],
        out_specs=pl.BlockSpec((tc * nb, SSM_WIDTH), lambda i: (i, 0)),
        scratch_shapes=[pltpu.VMEM((tc * nb, gp), F32), pltpu.VMEM((tc * nb, gp), F32),
                        pltpu.VMEM((nb, gp), F32), pltpu.VMEM((nb, gp), F32)],
        compiler_params=_cparams(("arbitrary",)),
        name="ssm_mixer",
    )(u_tm, bbr, bbi, ab_r.reshape(1, gp), ab_i.reshape(1, gp), ccr, cci,
      d.reshape(1, SSM_WIDTH).astype(F32), w_glu.astype(BF16))


def _mla_prep_kernel(cq_ref, kr_ref, ckv_ref, gq_ref, gkv_ref, wqa_ref, wqb_ref, wk_ref, wv_ref,
                     c1_ref, c0_ref, s0_ref, q_ref, k_ref, v_ref):
    qn = _rms(cq_ref[0].astype(F32), gq_ref[...], MLA_Q_RANK).astype(BF16)
    qa = _dot(qn, wqa_ref[...])
    qb = _dot(qn, wqb_ref[...])
    kn = _rms(ckv_ref[0].astype(F32), gkv_ref[...], MLA_KV_RANK).astype(BF16)
    ka = _dot(kn, wk_ref[...])
    va = _dot(kn, wv_ref[...])
    kr = kr_ref[0].astype(F32)
    c1, c0, s0 = c1_ref[...], c0_ref[...], s0_ref[...]
    krope = kr[:, :LANES] * c0 + kr[:, LANES:] * s0
    low_half = lax.broadcasted_iota(jnp.int32, (kr.shape[0], LANES), 1) < MLA_V
    for h in range(MLA_HEADS):
        sl = slice(h * LANES, (h + 1) * LANES)
        q_ref[0, h] = (qa[:, sl] * c1 + qb[:, sl] * s0).astype(BF16)
        k_ref[0, h] = (ka[:, sl] + krope).astype(BF16)
        ones = jnp.where(low_half if h % 2 == 0 else jnp.logical_not(low_half), 1.0, 0.0)
        v_ref[0, h] = jnp.concatenate([va[:, sl], ones], axis=1).astype(BF16)


def _mla_flash_kernel(q_ref, k_ref, v_ref, o_ref, m_ref, acc_ref, *, tq):
    qi = pl.program_id(1)
    m_ref[...] = jnp.full_like(m_ref, NEG_INF)
    acc_ref[...] = jnp.zeros_like(acc_ref)
    lane = lax.broadcasted_iota(jnp.int32, (tq, LANES), 1)
    rep = tq // LANES

    def tile(kt, masked):
        ks = pl.ds(pl.multiple_of(kt * tq, tq), tq)
        if masked:
            mask = (lax.broadcasted_iota(jnp.int32, (tq, tq), 1)
                    <= lax.broadcasted_iota(jnp.int32, (tq, tq), 0))
        hs = range(MLA_HEADS)
        s = [_dot_t(q_ref[0, h], k_ref[0, h, ks, :]) for h in hs]
        if masked:
            s = [jnp.where(mask, x, NEG_INF) for x in s]
        m_prev = [m_ref[h] for h in hs]
        m_new = [jnp.maximum(mp, jnp.max(x, axis=-1, keepdims=True)) for mp, x in zip(m_prev, s)]
        alpha = [jnp.exp2(mp - mn) for mp, mn in zip(m_prev, m_new)]
        p = [jnp.exp2(x - jnp.tile(mn, (1, rep))).astype(BF16) for x, mn in zip(s, m_new)]
        pv = [_dot(p[h], v_ref[0, h, ks, :]) for h in hs]
        for h in hs:
            m_ref[h] = m_new[h]
        for pr in range(MLA_HEADS // 2):
            a = jnp.tile(jnp.where(lane < MLA_V, alpha[2 * pr], alpha[2 * pr + 1]), (1, 2))
            acc_ref[pr] = acc_ref[pr] * a + pv[2 * pr] + pv[2 * pr + 1]

    def body(kt, c):
        tile(kt, False)
        return c

    lax.fori_loop(0, qi, body, 0)
    tile(qi, True)
    for pr in range(MLA_HEADS // 2):
        o_ref[0, :, pr * LANES:(pr + 1) * LANES] = (acc_ref[pr, :, :LANES] / acc_ref[pr, :, LANES:]).astype(o_ref.dtype)


def _rope_tables(seq):
    pos = jnp.arange(seq, dtype=F32)
    inv = 1.0 / (ROPE_THETA ** (jnp.arange(0, MLA_ROPE, 2, dtype=F32) / MLA_ROPE))
    ang = pos[:, None] * inv[None, :]
    cos, sin = jnp.cos(ang), jnp.sin(ang)
    cos2 = jnp.concatenate([cos, cos], axis=-1)
    sin2 = jnp.concatenate([sin, sin], axis=-1)
    z64 = jnp.zeros((seq, MLA_NOPE), F32)
    z32 = jnp.zeros((seq, LANES - MLA_NOPE - MLA_ROPE), F32)
    c1 = jnp.concatenate([jnp.ones((seq, MLA_NOPE), F32), cos2, z32], axis=-1)
    c0 = jnp.concatenate([z64, cos2, z32], axis=-1)
    s0 = jnp.concatenate([z64, sin2, z32], axis=-1)
    return c1, c0, s0


def _rot_half_cols(w):
    half = MLA_ROPE // 2
    return jnp.concatenate([-w[..., half:], w[..., :half]], axis=-1)


def mla_mixer(proj3, q_norm, w_uq, kv_norm, w_ukv, tabs, *, tm, tq):
    bsz, seq, _ = proj3.shape
    H = MLA_HEADS
    scale = (MLA_NOPE + MLA_ROPE) ** -0.5 * LOG2E
    wq =(w_uq.astype(F32) * scale).reshape(MLA_Q_RANK, H, MLA_NOPE + MLA_ROPE)
    zq = jnp.zeros((MLA_Q_RANK, H, LANES - MLA_NOPE - MLA_ROPE), F32)
    z64 = jnp.zeros((MLA_Q_RANK, H, MLA_NOPE), F32)
    wqa = jnp.concatenate([wq, zq], axis=-1).reshape(MLA_Q_RANK, H * LANES)
    wqb = jnp.concatenate([z64, _rot_half_cols(wq[..., MLA_NOPE:]), zq], axis=-1).reshape(MLA_Q_RANK, H * LANES)
    padq = ((0, 256 - MLA_Q_RANK), (0, 0))
    wqa = jnp.pad(wqa, padq).astype(BF16)
    wqb = jnp.pad(wqb, padq).astype(BF16)
    gq = jnp.pad(q_norm.astype(F32), (0, 256 - MLA_Q_RANK)).reshape(1, 256)
    wkv = w_ukv.astype(F32).reshape(MLA_KV_RANK, H, MLA_NOPE + MLA_V)
    zk = jnp.zeros((MLA_KV_RANK, H, MLA_NOPE), F32)
    wk = jnp.concatenate([wkv[..., :MLA_NOPE], zk], axis=-1).reshape(MLA_KV_RANK, H * LANES).astype(BF16)
    wv_h = wkv[..., MLA_NOPE:]
    even = (jnp.arange(H) % 2 == 0)[None, :, None]
    wv = jnp.concatenate([jnp.where(even, wv_h, 0.0), jnp.where(even, 0.0, wv_h)], axis=-1)
    wv = wv.reshape(MLA_KV_RANK, H * LANES).astype(BF16)
    c1, c0, s0 = tabs
    full2 = lambda shape: pl.BlockSpec(shape, lambda b, i: (0,) * len(shape))
    tab_spec = pl.BlockSpec((tm, LANES), lambda b, i: (i, 0))
    hd_spec = pl.BlockSpec((1, H, tm, LANES), lambda b, i: (b, 0, i, 0))
    hd_shape = jax.ShapeDtypeStruct((bsz, H, seq, LANES), BF16)
    v_spec = pl.BlockSpec((1, H, tm, 2 * LANES), lambda b, i: (b, 0, i, 0))
    v_shape = jax.ShapeDtypeStruct((bsz, H, seq, 2 * LANES), BF16)
    q, k, v = pl.pallas_call(
        _mla_prep_kernel,
        out_shape=(hd_shape, hd_shape, v_shape),
        grid=(bsz, seq // tm),
        in_specs=[pl.BlockSpec((1, tm, 256), lambda b, i: (b, i, C_CQ // 256)),
                  pl.BlockSpec((1, tm, 256), lambda b, i: (b, i, C_KR // 256)),
                  pl.BlockSpec((1, tm, 128), lambda b, i: (b, i, C_CKV // 128)),
                  full2((1, 256)), full2((1, 128)),
                  full2((256, H * LANES)), full2((256, H * LANES)),
                  full2((128, H * LANES)), full2((128, H * LANES)),
                  tab_spec, tab_spec, tab_spec],
        out_specs=(hd_spec, hd_spec, v_spec),
        compiler_params=_cparams(("parallel", "parallel")),
        name="mla_prep",
    )(proj3, proj3, proj3, gq, kv_norm.astype(F32).reshape(1, 128), wqa, wqb, wk, wv, c1, c0, s0)

    return pl.pallas_call(
        functools.partial(_mla_flash_kernel, tq=tq),
        out_shape=jax.ShapeDtypeStruct((bsz, seq, H * MLA_V), BF16),
        grid=(bsz, seq // tq),
        in_specs=[pl.BlockSpec((1, H, tq, LANES), lambda b, i: (b, 0, i, 0)),
                  pl.BlockSpec((1, H, seq, LANES), lambda b, i: (b, 0, 0, 0)),
                  pl.BlockSpec((1, H, seq, 2 * LANES), lambda b, i: (b, 0, 0, 0))],
        out_specs=pl.BlockSpec((1, tq, H * MLA_V), lambda b, i: (b, i, 0)),
        scratch_shapes=[pltpu.VMEM((H, tq, LANES), F32), pltpu.VMEM((H // 2, tq, 2 * LANES), F32)],
        compiler_params=_cparams(("parallel", "arbitrary")),
        name="mla_flash",
    )(q, k, v)


def _nsa_cmp_kernel(x_ref, pea_ref, peb_ref, w1a_ref, w1b_ref, w2_ref, o_ref):
    x = x_ref[0, 0]
    w1a, w1b = w1a_ref[0], w1b_ref[0]
    bias = _dot(pea_ref[0], w1a)[0:1] + _dot(peb_ref[0], w1b)[0:1]
    a = _dot(x, w1a)
    b = _dot(x, w1b)
    n = b.shape[0]
    pre = a + pltpu.roll(b, n - 1, 0) + bias
    o_ref[0, 0] = _dot(jax.nn.gelu(pre).astype(BF16), w2_ref[0]).astype(o_ref.dtype)


def nsa_compress(kv_cr, cmp_pe, cmp_w1, cmp_w2):
    bsz, _, nch, width = kv_cr.shape
    G, dh = NSA_KV_HEADS, NSA_DIM
    half = CMP_BLOCK // 2
    eye = jnp.eye(G, dtype=F32)
    w1r = cmp_w1.astype(F32).reshape(2, CMP_BLOCK, dh, dh)
    w1a = jnp.einsum('kpde,gh->kpgdhe', w1r[:, :half], eye).reshape(2, width, G * dh).astype(BF16)
    w1b = jnp.einsum('kpde,gh->kpgdhe', w1r[:, half:], eye).reshape(2, width, G * dh).astype(BF16)
    w2 = jnp.einsum('kde,gh->kgdhe', cmp_w2.astype(F32), eye).reshape(2, G * dh, G * dh).astype(BF16)
    pe = cmp_pe.astype(F32)
    pe_g = jnp.broadcast_to(pe[:, :, None, :], (2, CMP_BLOCK, G, dh))
    pea = jnp.broadcast_to(pe_g[:, :half].reshape(2, 1, width), (2, 8, width)).astype(BF16)
    peb = jnp.broadcast_to(pe_g[:, half:].reshape(2, 1, width), (2, 8, width)).astype(BF16)
    kvspec = lambda shape: pl.BlockSpec(shape, lambda b, k: (k,) + (0,) * (len(shape) - 1))
    return pl.pallas_call(
        _nsa_cmp_kernel,
        out_shape=jax.ShapeDtypeStruct((bsz, 2, nch, G * dh), BF16),
        grid=(bsz, 2),
        in_specs=[pl.BlockSpec((1, 1, nch, width), lambda b, k: (b, k, 0, 0)),
                  kvspec((1, 8, width)), kvspec((1, 8, width)),
                  kvspec((1, width, G * dh)), kvspec((1, width, G * dh)),
                  kvspec((1, G * dh, G * dh))],
        out_specs=pl.BlockSpec((1, 1, nch, G * dh), lambda b, k: (b, k, 0, 0)),
        compiler_params=_cparams(("parallel", "parallel")),
        name="nsa_compress",
    )(kv_cr, pea, peb, w1a, w1b, w2)


def _nsa_kernel(q_ref, gate_ref, ksl_ref, vsl_ref, kwn_ref, vwn_ref, kvc_ref, bc_ref, bw_ref,
                bs_ref, ov_ref, blk1h_ref, gexp_ref, o_ref, kaug_ref, vaug_ref, m_ref, acc_ref,
                *, tq, tk, nbs, n_sel):
    qi = pl.program_id(1)
    R, G = NSA_REP, NSA_KV_HEADS
    H = R * G
    nsub = tk // tq
    nwin = WINDOW // tq

    @pl.when(qi == 0)
    def _():
        kaug_ref[:, :LANES] = ksl_ref[0]
        kaug_ref[:, LANES:] = blk1h_ref[...]
        vaug_ref[:, :LANES] = vsl_ref[0]
        vaug_ref[:, LANES:] = jnp.ones((vaug_ref.shape[0], LANES), BF16)

    lane = lax.broadcasted_iota(jnp.int32, (tq, LANES), 1)
    t_row = qi * tq + lax.broadcasted_iota(jnp.int32, (H * tq, 1), 0) % tq
    kc = kvc_ref[0, 0]
    vc = kvc_ref[0, 1]
    ov = ov_ref[...]

    def stack(fn):
        return jnp.concatenate([fn(h) for h in range(H)], axis=0)

    q_all = stack(lambda h: q_ref[0, :, h * LANES:(h + 1) * LANES])

    c = jnp.minimum(qi, nwin)
    ws = pl.ds(pl.multiple_of(jnp.maximum(qi - nwin, 0) * tq, tq), (nwin + 1) * tq)
    k_w = kwn_ref[0, ws, :]
    v_w = vwn_ref[0, ws, :]
    o_w = []
    for g in range(G):
        rows = slice(g * R * tq, (g + 1) * R * tq)
        s_w = _dot_t(q_all[rows], k_w) + jnp.concatenate([bw_ref[g * R + r, c] for r in range(R)], axis=0)
        p_w = jnp.exp2(s_w - jnp.max(s_w, axis=-1, keepdims=True))
        o_w.append(_dot(p_w.astype(BF16), v_w) / jnp.sum(p_w, axis=-1, keepdims=True))
    o_w = jnp.concatenate(o_w, axis=0)

    valid = t_row >= (CMP_BLOCK - 1)
    s = _dot_t(q_all, kc) + stack(lambda h: bc_ref[h])
    m = jnp.max(s, axis=-1, keepdims=True)
    p = jnp.where(valid, jnp.exp2(s - m), 0.0)
    l = jnp.where(valid, jnp.sum(p, axis=-1, keepdims=True), 1.0)
    pc = p / l
    o_c = _dot(pc.astype(BF16), vc)
    blk = lax.broadcasted_iota(jnp.int32, (nbs, tq), 0)
    tl = qi * tq + lax.broadcasted_iota(jnp.int32, (nbs, tq), 1)
    cur = tl // SEL_BLOCK
    forced = (blk == 0) | (blk == cur) | (blk == cur - 1)
    future = blk * SEL_BLOCK > tl
    qmask = []
    for g in range(G):
        b0 = g * R * tq
        psum = pc[b0:b0 + tq] + pc[b0 + tq:b0 + 2 * tq] + pc[b0 + 2 * tq:b0 + 3 * tq]
        p_hi = psum.astype(BF16)
        p_lo = (psum - p_hi.astype(F32)).astype(BF16)
        imp = (_dot_t(ov, p_hi) + _dot_t(ov, p_lo))[:nbs]
        imp = jnp.where(forced, FORCE, jnp.where(future, -FORCE, imp))
        rank = jnp.zeros((nbs, tq), F32)
        for i in range(nbs):
            ri = imp[i:i + 1, :]
            beats = (ri > imp) | ((ri == imp) & (blk > i))
            rank = rank + jnp.where(beats, 1.0, 0.0)
        sel = jnp.where(rank < n_sel, 0.0, NEG_INF)
        sel = jnp.concatenate([sel, jnp.zeros((LANES - nbs, tq), F32)], axis=0).T.astype(BF16)
        qmask += [sel] * R
    q_aug = jnp.concatenate([q_all, jnp.concatenate(qmask, axis=0)], axis=1)

    m_ref[...] = jnp.full_like(m_ref, NEG_INF)
    acc_ref[...] = jnp.zeros_like(acc_ref)

    halves = [slice(g * R * tq, (g + 1) * R * tq) for g in range(G)]

    def sel_tile(kt, bias):
        ks = pl.ds(pl.multiple_of(kt * tk, tk), tk)
        k_t = kaug_ref[ks, :]
        v_t = vaug_ref[ks, :]
        s = [_dot_t(q_aug[hs], k_t) for hs in halves]
        if bias is not None:
            s = [x + bias[hs] for x, hs in zip(s, halves)]
        m_prev = [m_ref[hs, :] for hs in halves]
        m_new = [jnp.maximum(mp, jnp.max(x, axis=-1, keepdims=True)) for mp, x in zip(m_prev, s)]
        alpha = [jnp.exp2(mp - mn) for mp, mn in zip(m_prev, m_new)]
        p = [jnp.exp2(x - jnp.tile(mn, (1, tk // LANES))).astype(BF16) for x, mn in zip(s, m_new)]
        pv = [_dot(pp, v_t) for pp in p]
        for hs, mn, a, o in zip(halves, m_new, alpha, pv):
            m_ref[hs, :] = mn
            acc_ref[hs, :] = acc_ref[hs, :] * jnp.tile(a, (1, 2)) + o

    def near_bias(kt):
        cols = []
        for sub in range(nsub):
            d = qi - (kt * nsub + sub)
            cols.append(stack(lambda h: jnp.where(d == 0, bs_ref[h, 0], jnp.where(
                d == 1, bs_ref[h, 1], jnp.where(d < 0, NEG_INF, 0.0)))))
        return jnp.concatenate(cols, axis=1)

    def far_body(kt, c):
        sel_tile(kt, None)
        return c

    kd = (qi * tq) // tk
    lax.fori_loop(0, jnp.maximum(kd - 1, 0), far_body, 0)

    @pl.when(kd >= 1)
    def _():
        sel_tile(kd - 1, near_bias(kd - 1))

    sel_tile(kd, near_bias(kd))
    o_s = acc_ref[:, :LANES] / acc_ref[:, LANES:]

    gates = _sigmoid(_dot(gate_ref[0], gexp_ref[...]))
    for r in range(R):
        res = None
        for b, o_b in enumerate((o_c, o_s, o_w)):
            o_br = jnp.where(lane < NSA_DIM, o_b[r * tq:(r + 1) * tq], o_b[(R + r) * tq:(R + r + 1) * tq])
            term = gates[:, (b * R + r) * LANES:(b * R + r + 1) * LANES] * o_br
            res = term if res is None else res + term
        o_ref[0, :, r * LANES:(r + 1) * LANES] = res.astype(o_ref.dtype)


def _t5_bucket(dist):
    n = jnp.maximum(dist, 0)
    exact = REL_BUCKETS // 2
    nf = jnp.maximum(n, exact).astype(F32)
    large = exact + jnp.floor(jnp.log(nf / exact) / math.log(REL_MAX_DIST / exact)
                              * (REL_BUCKETS - exact)).astype(jnp.int32)
    return jnp.where(n < exact, n, jnp.minimum(large, REL_BUCKETS - 1))


def _nsa_bias_tables(rel_bias, seq, tq):
    rb = rel_bias.astype(F32).T
    far = rb[:, REL_BUCKETS - 1].reshape(NSA_HEADS, 1, 1)

    def by_dist(dist, ok, shift=0.0):
        bucket = _t5_bucket(dist)[None]
        out = jnp.zeros((NSA_HEADS,) + dist.shape, F32)
        for k in range(REL_BUCKETS):
            out = jnp.where(bucket == k, rb[:, k].reshape((NSA_HEADS,) + (1,) * dist.ndim), out)
        return jnp.where(ok[None], (out - shift) * LOG2E, NEG_INF)

    i = jnp.arange(tq)[:, None]
    t = jnp.arange(seq)[:, None]
    dist_c = t - (jnp.arange(LANES)[None, :] * CMP_STRIDE + CMP_BLOCK - 1)
    bc = by_dist(dist_c, dist_c >= 0)
    nwin = WINDOW // tq
    jw = jnp.arange((nwin + 1) * tq)[None, :]
    bw = jnp.stack([by_dist(tq * c + i - jw, (tq * c + i - jw >= 0) & (tq * c + i - jw < WINDOW))
                    for c in range(nwin + 1)], axis=1)
    js = jnp.arange(tq)[None, :]
    bs = jnp.stack([by_dist(tq * c + i - js, tq * c + i - js >= 0, far) for c in range(2)], axis=1)
    ci = jnp.arange(LANES)[:, None]
    sj = jnp.arange(LANES)[None, :]
    nbs = seq // SEL_BLOCK
    ov = ((ci * CMP_STRIDE <= sj * SEL_BLOCK + SEL_BLOCK - 1)
          & (ci * CMP_STRIDE + CMP_BLOCK - 1 >= sj * SEL_BLOCK)
          & (ci < seq // CMP_STRIDE - 1) & (sj < nbs))
    blk1h = (jnp.arange(seq)[:, None] // SEL_BLOCK == sj).astype(BF16)
    col = jnp.arange(3 * NSA_REP * LANES)
    slab, lane_g = col // LANES, (col % LANES) // NSA_DIM
    gate_col = (slab // NSA_REP) * NSA_HEADS + lane_g * NSA_REP + slab % NSA_REP
    gexp = (ci == gate_col[None, :]).astype(BF16)
    return bc, bw, bs, ov.T.astype(BF16), blk1h, gexp


def nsa_mixer(proj3, kvc, tables, *, tq, tk):
    bsz, seq, _ = proj3.shape
    assert tq % LANES == 0 and tq >= REL_MAX_DIST and tk % tq == 0 and seq % tk == 0
    assert seq // CMP_STRIDE == LANES and WINDOW % tq == 0
    nwin = WINDOW // tq
    bc, bw, bs, ov, blk1h, gexp = tables
    nbs = seq // SEL_BLOCK
    H = NSA_HEADS
    slab = lambda j: pl.BlockSpec((1, seq, LANES), lambda b, i: (b, 0, C_NKV // LANES + j))
    const = lambda shape: pl.BlockSpec(shape, lambda b, i: (0,) * len(shape))
    return pl.pallas_call(
        functools.partial(_nsa_kernel, tq=tq, tk=tk, nbs=nbs, n_sel=min(SEL_TOPN, nbs)),
        out_shape=jax.ShapeDtypeStruct((bsz, seq, H * NSA_DIM), BF16),
        grid=(bsz, seq // tq),
        in_specs=[pl.BlockSpec((1, tq, H * LANES), lambda b, i: (b, i, C_NQ // (H * LANES))),
                  pl.BlockSpec((1, tq, LANES), lambda b, i: (b, i, C_GATE // LANES)),
                  slab(2), slab(3), slab(4), slab(5),
                  pl.BlockSpec((1, 2, LANES, LANES), lambda b, i: (b, 0, 0, 0)),
                  pl.BlockSpec((H, tq, LANES), lambda b, i: (0, i, 0)),
                  const((H, nwin + 1, tq, (nwin + 1) * tq)), const((H, 2, tq, tq)),
                  const((LANES, LANES)), const((seq, LANES)), const(gexp.shape)],
        out_specs=pl.BlockSpec((1, tq, H * NSA_DIM), lambda b, i: (b, i, 0)),
        scratch_shapes=[pltpu.VMEM((seq, 2 * LANES), BF16), pltpu.VMEM((seq, 2 * LANES), BF16),
                        pltpu.VMEM((H * tq, LANES), F32), pltpu.VMEM((H * tq, 2 * LANES), F32)],
        compiler_params=_cparams(("parallel", "arbitrary")),
        name="nsa_attention",
    )(proj3, proj3, proj3, proj3, proj3, proj3, kvc, bc, bw, bs, ov, blk1h, gexp)


def _mix_xattn_kernel(ys_ref, ym_ref, yn_ref, h_ref, g1_ref, g2_ref, g3_ref, w1_ref, w2_ref, w3_ref,
                      kv_ref, g_ref, wq_ref, wo_ref, o_ref, *, dh):
    rows = h_ref.shape[1] // 2
    sl = [slice(k * rows, (k + 1) * rows) for k in range(2)]
    hw = XATTN_HEADS * dh
    heads = range(XATTN_HEADS)

    def normed(y_ref, gy_ref):
        return [_rms(y_ref[0, s, :].astype(F32), gy_ref[...], y_ref.shape[-1]).astype(BF16) for s in sl]

    ns, nm, nn = normed(ys_ref, g1_ref), normed(ym_ref, g2_ref), normed(yn_ref, g3_ref)
    h = [h_ref[0, s, :] + _dot(a, w1_ref[...]) + _dot(b, w2_ref[...]) + _dot(c, w3_ref[...])
         for s, a, b, c in zip(sl, ns, nm, nn)]
    xn = [_rms(x, g_ref[...], x.shape[-1]).astype(BF16) for x in h]
    q = [_dot(x, wq_ref[...]).astype(BF16) for x in xn]
    s = [[_dot_t(qk[:, hd * dh:(hd + 1) * dh], kv_ref[0, :, hd * dh:(hd + 1) * dh]) for hd in heads] for qk in q]
    p = [[jnp.exp2(x - jnp.max(x, axis=-1, keepdims=True)) for x in sk] for sk in s]
    p = [[(x / jnp.sum(x, axis=-1, keepdims=True)).astype(BF16) for x in pk] for pk in p]
    o = [jnp.concatenate([_dot(pk[hd], kv_ref[0, :, hw + hd * dh:hw + (hd + 1) * dh]).astype(BF16)
                          for hd in heads], axis=-1) for pk in p]
    for sk, hk, ok in zip(sl, h, o):
        o_ref[0, sk, :] = hk + _dot(ok, wo_ref[...])


def mix_out_cross_attention(ys3, h3, gains, weights, kv3, g_x, wq, wo, *, tm):
    bsz, seq, d = h3.shape
    m = kv3.shape[1]
    dh = d // XATTN_HEADS
    wq_s = (wq.astype(F32) * (dh ** -0.5 * LOG2E)).astype(BF16)
    gains = [g.reshape(1, -1).astype(F32) for g in gains]
    const = lambda shape: pl.BlockSpec(shape, lambda b, i: (0,) * len(shape))
    row = lambda w: pl.BlockSpec((1, tm, w), lambda b, i: (b, i, 0))
    return pl.pallas_call(
        functools.partial(_mix_xattn_kernel, dh=dh),
        out_shape=jax.ShapeDtypeStruct((bsz, seq, d), F32),
        grid=(bsz, seq // tm),
        in_specs=[row(y.shape[2]) for y in ys3] + [row(d)]
                 + [const(g.shape) for g in gains] + [const(w.shape) for w in weights]
                 + [pl.BlockSpec((1, m, 2 * d), lambda b, i: (b, 0, 0)),
                    const((1, d)), const((d, d)), const((d, d))],
        out_specs=row(d),
        compiler_params=_cparams(("parallel", "parallel")),
        name="mix_out_cross_attention",
    )(*ys3, h3, *gains, *weights, kv3, g_x.reshape(1, d).astype(F32), wq_s, wo.astype(BF16))


def _ffn_kernel(h_ref, g_ref, wg_ref, wu_ref, wd_ref, o_ref, xn_ref, acc_ref):
    j = pl.program_id(1)

    @pl.when(j == 0)
    def _():
        h = h_ref[...]
        xn_ref[...] = _rms(h, g_ref[...], h.shape[-1]).astype(BF16)
        acc_ref[...] = h

    _swiglu_accumulate(xn_ref, wg_ref[...], wu_ref[...], wd_ref[...], acc_ref)

    @pl.when(j == pl.num_programs(1) - 1)
    def _():
        o_ref[...] = acc_ref[...]


def dense_ffn(h, g, wg, wu, wd, *, tm, tf):
    m, d = h.shape
    ff = wg.shape[1]
    return pl.pallas_call(
        _ffn_kernel,
        out_shape=jax.ShapeDtypeStruct((m, d), F32),
        grid=(m // tm, ff // tf),
        in_specs=[pl.BlockSpec((tm, d), lambda i, j: (i, 0)),
                  pl.BlockSpec((1, d), lambda i, j: (0, 0)),
                  pl.BlockSpec((d, tf), lambda i, j: (0, j)),
                  pl.BlockSpec((d, tf), lambda i, j: (0, j)),
                  pl.BlockSpec((tf, d), lambda i, j: (j, 0))],
        out_specs=pl.BlockSpec((tm, d), lambda i, j: (i, 0)),
        scratch_shapes=[pltpu.VMEM((tm, d), BF16), pltpu.VMEM((tm, d), F32)],
        compiler_params=_cparams(("parallel", "arbitrary")),
        name="dense_ffn",
    )(h, g.reshape(1, d).astype(F32), wg.astype(BF16), wu.astype(BF16), wd.astype(BF16))


def _router_kernel(h_ref, g_ref, wr_hi_ref, wr_lo_ref, xn_ref, info_ref, cnt_ref, carry_ref, *, tm):
    i = pl.program_id(0)

    @pl.when(i == 0)
    def _():
        carry_ref[...] = jnp.zeros_like(carry_ref)

    h = h_ref[...]
    xn = _rms(h, g_ref[...], h.shape[-1])
    xn_ref[...] = xn
    x_hi = xn.astype(BF16)
    x_lo = (xn - x_hi.astype(F32)).astype(BF16)
    logits = _dot(x_hi, wr_hi_ref[...]) + _dot(x_lo, wr_hi_ref[...]) + _dot(x_hi, wr_lo_ref[...])
    lane = lax.broadcasted_iota(jnp.int32, (tm, LANES), 1)
    lanef = lane.astype(F32)
    logits = jnp.where(lane < N_EXPERTS, logits, NEG_INF)
    m1 = jnp.max(logits, axis=-1, keepdims=True)
    i1 = jnp.min(jnp.where(logits == m1, lanef, float(LANES)), axis=-1, keepdims=True)
    rest = jnp.where(lanef == i1, NEG_INF, logits)
    m2 = jnp.max(rest, axis=-1, keepdims=True)
    i2 = jnp.min(jnp.where(rest == m2, lanef, float(LANES)), axis=-1, keepdims=True)
    e2 = jnp.exp(m2 - m1)
    w1 = 1.0 / (1.0 + e2)
    w2 = e2 / (1.0 + e2)
    oh1 = lanef == i1
    oh2 = lanef == i2
    oh = jnp.where(oh1 | oh2, 1.0, 0.0)
    rr = lax.broadcasted_iota(jnp.int32, (tm, tm), 0)
    cc = lax.broadcasted_iota(jnp.int32, (tm, tm), 1)
    tri = jnp.where(cc < rr, 1.0, 0.0).astype(BF16)
    before = _dot(tri, oh.astype(BF16)) + carry_ref[0:1, :]
    r1 = jnp.sum(jnp.where(oh1, before, 0.0), axis=-1, keepdims=True)
    r2 = jnp.sum(jnp.where(oh2, before, 0.0), axis=-1, keepdims=True)
    carry_ref[...] = carry_ref[...] + jnp.sum(oh, axis=0, keepdims=True)
    info = jnp.where(lane == 0, i1, jnp.where(lane == 1, i2, jnp.where(lane == 2, w1, jnp.where(
        lane == 3, w2, jnp.where(lane == 4, r1, jnp.where(lane == 5, r2, 0.0))))))
    info_ref[...] = info
    cnt_ref[...] = carry_ref[...]


def moe_router(h, g, router, *, tm):
    m, d = h.shape
    wr = jnp.pad(router.astype(F32), ((0, 0), (0, LANES - N_EXPERTS)))
    wr_hi = wr.astype(BF16)
    wr_lo = (wr - wr_hi.astype(F32)).astype(BF16)
    return pl.pallas_call(
        functools.partial(_router_kernel, tm=tm),
        out_shape=(jax.ShapeDtypeStruct((m, d), F32), jax.ShapeDtypeStruct((m, LANES), F32),
                   jax.ShapeDtypeStruct((8, LANES), F32)),
        grid=(m // tm,),
        in_specs=[pl.BlockSpec((tm, d), lambda i: (i, 0)),
                  pl.BlockSpec((1, d), lambda i: (0, 0)),
                  pl.BlockSpec((d, LANES), lambda i: (0, 0)),
                  pl.BlockSpec((d, LANES), lambda i: (0, 0))],
        out_specs=(pl.BlockSpec((tm, d), lambda i: (i, 0)),
                   pl.BlockSpec((tm, LANES), lambda i: (i, 0)),
                   pl.BlockSpec((8, LANES), lambda i: (0, 0))),
        scratch_shapes=[pltpu.VMEM((8, LANES), F32)],
        compiler_params=_cparams(("arbitrary",)),
        name="moe_router",
    )(h, g.reshape(1, d).astype(F32), wr_hi, wr_lo)


def _row_copy(src_hbm, row, dst, slot, sem):
    return pltpu.make_async_copy(src_hbm.at[pl.ds(row, 1), :], dst.at[pl.ds(slot, 1), :], sem)


def _rows_wait(src_hbm, dst, sem):
    pltpu.make_async_copy(src_hbm.at[pl.ds(0, dst.shape[0]), :], dst, sem).wait()


def _moe_ffn_kernel(src_ref, texp_ref, nact_ref, x_hbm, wg_ref, wu_ref, wd_ref, o_ref,
                    xbuf, xbf, acc_ref, sem, *, tm, nj):
    i = pl.program_id(0)
    j = pl.program_id(1)
    nact = nact_ref[0]
    active = i < nact
    nbuf = xbuf.shape[0]
    ahead = nbuf - 1
    cur = i % nbuf
    rows_per_step = tm // nj

    for t0 in range(ahead):
        @pl.when((i == 0) & (j == 0) & (t0 < nact))
        def _():
            def issue(s, c):
                _row_copy(x_hbm, src_ref[t0 * tm + s], xbuf.at[t0], s, sem.at[t0]).start()
                return c

            lax.fori_loop(0, tm, issue, 0, unroll=8)

    @pl.when(active & (j == 0))
    def _():
        _rows_wait(x_hbm, xbuf.at[cur], sem.at[cur])
        xbf[...] = xbuf[cur].astype(BF16)
        acc_ref[...] = jnp.zeros_like(acc_ref)

    def compute(prefetch):
        if prefetch:
            nxt = (i + ahead) % nbuf
            base = (i + ahead) * tm + j * rows_per_step
            for k in range(rows_per_step):
                _row_copy(x_hbm, src_ref[base + k], xbuf.at[nxt], j * rows_per_step + k, sem.at[nxt]).start()
        _swiglu_accumulate(xbf, wg_ref[0], wu_ref[0], wd_ref[0], acc_ref)

    @pl.when(i + ahead < nact)
    def _():
        compute(True)

    @pl.when(active & (i + ahead >= nact))
    def _():
        compute(False)

    @pl.when(j == nj - 1)
    def _():
        o_ref[...] = jnp.where(active, acc_ref[...], 0.0)


def moe_expert_ffn(xn, src, tile_expert, n_active, wg, wu, wd, *, tm, tf):
    n_slots = src.shape[0]
    d = xn.shape[1]
    ne, _, ff = wg.shape
    nj = ff // tf
    assert nj * tf == ff and tm % nj == 0
    wg, wu, wd = wg.astype(BF16), wu.astype(BF16), wd.astype(BF16)

    def wmap_col(i, j, src, texp, nact):
        return (texp[i], 0, jnp.where(i < nact[0], j, nj - 1))

    def wmap_row(i, j, src, texp, nact):
        return (texp[i], jnp.where(i < nact[0], j, nj - 1), 0)

    return pl.pallas_call(
        functools.partial(_moe_ffn_kernel, tm=tm, nj=nj),
        out_shape=jax.ShapeDtypeStruct((n_slots, d), F32),
        grid_spec=pltpu.PrefetchScalarGridSpec(
            num_scalar_prefetch=3,
            grid=(n_slots // tm, nj),
            in_specs=[pl.BlockSpec(memory_space=pl.ANY),
                      pl.BlockSpec((1, d, tf), wmap_col),
                      pl.BlockSpec((1, d, tf), wmap_col),
                      pl.BlockSpec((1, tf, d), wmap_row)],
            out_specs=pl.BlockSpec((tm, d), lambda i, j, *_: (i, 0)),
            scratch_shapes=[pltpu.VMEM((3, tm, d), F32), pltpu.VMEM((tm, d), BF16),
                            pltpu.VMEM((tm, d), F32), pltpu.SemaphoreType.DMA((3,))]),
        compiler_params=_cparams(("arbitrary", "arbitrary")),
        name="moe_expert_ffn",
    )(src, tile_expert, n_active, xn, wg, wu, wd)


def _moe_combine_kernel(pos_ref, h_ref, info_ref, ys_hbm, g_ref, o_ref, buf, sem, *, tm, final_norm):
    i = pl.program_id(0)
    n = pl.num_programs(0)
    cur = i % 2

    def start_gather(tile, b):
        for s in range(tm):
            for k in range(2):
                _row_copy(ys_hbm, pos_ref[2 * (tile * tm + s) + k], buf.at[b, k], s, sem.at[b]).start()

    @pl.when(i == 0)
    def _():
        start_gather(0, 0)

    @pl.when(i + 1 < n)
    def _():
        start_gather(i + 1, 1 - cur)

    for k in range(2):
        _rows_wait(ys_hbm, buf.at[cur, k], sem.at[cur])
    info = info_ref[...]
    y = h_ref[...] + info[:, 2:3] * buf[cur, 0] + info[:, 3:4] * buf[cur, 1]
    if final_norm:
        y = _rms(y, g_ref[...], y.shape[-1])
    o_ref[...] = y


def moe_combine(h, info, ys, pos_flat, g_final, *, tm, final_norm):
    m, d = h.shape
    return pl.pallas_call(
        functools.partial(_moe_combine_kernel, tm=tm, final_norm=final_norm),
        out_shape=jax.ShapeDtypeStruct((m, d), F32),
        grid_spec=pltpu.PrefetchScalarGridSpec(
            num_scalar_prefetch=1,
            grid=(m // tm,),
            in_specs=[pl.BlockSpec((tm, d), lambda i, *_: (i, 0)),
                      pl.BlockSpec((tm, LANES), lambda i, *_: (i, 0)),
                      pl.BlockSpec(memory_space=pl.ANY),
                      pl.BlockSpec((1, d), lambda i, *_: (0, 0))],
            out_specs=pl.BlockSpec((tm, d), lambda i, *_: (i, 0)),
            scratch_shapes=[pltpu.VMEM((2, 2, tm, d), F32), pltpu.SemaphoreType.DMA((2,))]),
        compiler_params=_cparams(("arbitrary",)),
        name="moe_combine",
    )(pos_flat, h, info, ys, g_final.reshape(1, d).astype(F32))


def moe_layer(h, g, router, wg, wu, wd, g_final, *, final_norm, tm_r=512, tm_g=512, tf=1792, tm_c=256):
    m, d = h.shape
    xn, info, cnt = moe_router(h, g, router, tm=tm_r)
    e_idx = info[:, 0:2].astype(jnp.int32)
    rank = info[:, 4:6].astype(jnp.int32)
    counts = cnt[0, :N_EXPERTS].astype(jnp.int32)
    tiles_per = (counts + tm_g - 1) // tm_g
    tile_end = jnp.cumsum(tiles_per)
    seg_start = (tile_end - tiles_per) * tm_g
    pos = rank
    for e in range(N_EXPERTS):
        pos = pos + jnp.where(e_idx == e, seg_start[e], 0)
    n_tiles = (2 * m) // tm_g + N_EXPERTS
    n_slots = n_tiles * tm_g
    tok = jnp.broadcast_to(jnp.arange(m, dtype=jnp.int32)[:, None], (m, 2))
    src = jnp.zeros((n_slots,), jnp.int32).at[pos.reshape(-1)].set(tok.reshape(-1))
    n_active = tile_end[-1:].astype(jnp.int32)
    tile_ids = jnp.minimum(jnp.arange(n_tiles, dtype=jnp.int32), n_active[0] - 1)
    tile_expert = jnp.sum(tile_ids[:, None] >= tile_end[None, :], axis=1).astype(jnp.int32)
    ys = moe_expert_ffn(xn, src, tile_expert, n_active,
                        wg, wu, wd, tm=tm_g, tf=tf)
    return moe_combine(h, info, ys, pos.reshape(-1).astype(jnp.int32), g_final, tm=tm_c, final_norm=final_norm)


def _final_norm_kernel(h_ref, g_ref, o_ref):
    h = h_ref[...]
    o_ref[...] = _rms(h, g_ref[...], h.shape[-1])


def final_rmsnorm(h, g, *, tm):
    m, d = h.shape
    return pl.pallas_call(
        _final_norm_kernel,
        out_shape=jax.ShapeDtypeStruct((m, d), F32),
        grid=(m // tm,),
        in_specs=[pl.BlockSpec((tm, d), lambda i: (i, 0)), pl.BlockSpec((1, d), lambda i: (0, 0))],
        out_specs=pl.BlockSpec((tm, d), lambda i: (i, 0)),
        compiler_params=_cparams(("parallel",)),
        name="final_rmsnorm",
    )(h, g.reshape(1, d).astype(F32))


def _pack_w_in(w):
    d = w.shape[0]
    w = w.astype(F32)
    o = 0
    u = w[:, o:o + SSM_WIDTH]; o += SSM_WIDTH
    cq = w[:, o:o + MLA_Q_RANK]; o += MLA_Q_RANK
    ckv = w[:, o:o + MLA_KV_RANK]; o += MLA_KV_RANK
    kr = w[:, o:o + MLA_ROPE]; o += MLA_ROPE
    nq = w[:, o:o + NSA_HEADS * NSA_DIM]; o += NSA_HEADS * NSA_DIM
    nkv = w[:, o:o + 6 * NSA_KV_HEADS * NSA_DIM]; o += 6 * NSA_KV_HEADS * NSA_DIM
    gate = w[:, o:o + 3 * NSA_HEADS]
    z = lambda n: jnp.zeros((d, n), F32)
    kr_a = jnp.concatenate([z(MLA_NOPE), kr, z(LANES - MLA_NOPE - MLA_ROPE)], axis=1)
    kr_b = jnp.concatenate([z(MLA_NOPE), _rot_half_cols(kr), z(LANES - MLA_NOPE - MLA_ROPE)], axis=1)
    nq_h = (nq * (NSA_DIM ** -0.5 * LOG2E)).reshape(d, NSA_KV_HEADS, NSA_REP, NSA_DIM)
    zq = jnp.zeros((d, NSA_REP, NSA_DIM), F32)
    nq_p = jnp.concatenate([
        jnp.concatenate([nq_h[:, 0], zq], axis=-1).reshape(d, NSA_REP * LANES),
        jnp.concatenate([zq, nq_h[:, 1]], axis=-1).reshape(d, NSA_REP * LANES)], axis=1)
    packed = jnp.concatenate([u, cq, z(256 - MLA_Q_RANK), kr_a, kr_b, nq_p, nkv, ckv,
                              gate, z(LANES - 3 * NSA_HEADS)], axis=1)
    assert packed.shape[1] == IN_COLS_PACKED
    return packed.astype(BF16)


def _rg_order(a):
    rest = a.shape[1:]
    return a.reshape((NSA_KV_HEADS, NSA_REP, NSA_DIM) + rest).swapaxes(0, 1).reshape((-1,) + rest)


def kernel(x, mem, w_in, w_out, mix_norm, out_norm, ssm_a_re, ssm_a_im, ssm_b_re, ssm_b_im, ssm_c_re, ssm_c_im, ssm_d, ssm_log_dt, ssm_w_glu, mla_q_norm, mla_w_uq, mla_kv_norm, mla_w_ukv, nsa_cmp_pe, nsa_cmp_w1, nsa_cmp_w2, rel_bias, xattn_norm, mem_norm, xattn_wq, xattn_wkv, xattn_wo, ffn_norm, dense_w_gate, dense_w_up, dense_w_down, moe_router, moe_w_gate, moe_w_up, moe_w_down, final_norm):
    bsz, seq, d = x.shape
    depth = w_in.shape[0]
    T = bsz * seq
    nmem = mem.shape[1]
    tq_nsa, tk_nsa = 2 * LANES, 4 * LANES
    rope_tabs = _rope_tables(seq)
    nsa_tabs = _nsa_bias_tables(rel_bias, seq, tq_nsa)
    o1, o2 = SSM_WIDTH, SSM_WIDTH + MLA_HEADS * MLA_V
    mem2 = mem.reshape(bsz * nmem, d)
    h = x.reshape(T, d)
    for l in range(depth):
        proj = norm_matmul(h, mix_norm[l], _pack_w_in(w_in[l]), tm=512, tn=IN_COLS_PACKED, out_dtype=BF16)
        proj3 = proj.reshape(bsz, seq, IN_COLS_PACKED)
        u_tm = proj3[:, :, C_U:C_U + SSM_WIDTH].transpose(1, 0, 2).reshape(seq * bsz, SSM_WIDTH)
        y_ssm = ssm_mixer(u_tm, ssm_a_re[l], ssm_a_im[l], ssm_b_re[l], ssm_b_im[l], ssm_c_re[l], ssm_c_im[l],
                          ssm_d[l], ssm_log_dt[l], ssm_w_glu[l], nb=bsz, tc=64)
        y_ssm = y_ssm.reshape(seq, bsz, SSM_WIDTH).transpose(1, 0, 2)
        y_mla = mla_mixer(proj3, mla_q_norm[l], mla_w_uq[l], mla_kv_norm[l], mla_w_ukv[l], rope_tabs,
                          tm=512, tq=512)
        nch = seq // CMP_STRIDE
        kv_cr = jnp.stack([proj3[:, :, C_NKV:C_NKV + LANES].reshape(bsz, nch, CMP_STRIDE * LANES),
                           proj3[:, :, C_NKV + LANES:C_NKV + 2 * LANES].reshape(bsz, nch, CMP_STRIDE * LANES)],
                          axis=1)
        kvc = nsa_compress(kv_cr, nsa_cmp_pe[l], nsa_cmp_w1[l], nsa_cmp_w2[l])
        y_nsa = nsa_mixer(proj3, kvc, nsa_tabs, tq=tq_nsa, tk=tk_nsa)
        g_out = out_norm[l]
        wo_l = w_out[l]
        kv_mem = norm_matmul(mem2, mem_norm[l], xattn_wkv[l].astype(BF16), tm=256, tn=512, out_dtype=BF16)
        h = mix_out_cross_attention(
            [y_ssm, y_mla, y_nsa], h.reshape(bsz, seq, d),
            [g_out[:o1], g_out[o1:o2], _rg_order(g_out[o2:])],
            [wo_l[:o1].astype(BF16), wo_l[o1:o2].astype(BF16), _rg_order(wo_l[o2:]).astype(BF16)],
            kv_mem.reshape(bsz, nmem, 2 * d), xattn_norm[l], xattn_wq[l], xattn_wo[l], tm=512).reshape(T, d)
        last = l == depth - 1
        if l % 2 == 0:
            h = dense_ffn(h, ffn_norm[l], dense_w_gate[l // 2], dense_w_up[l // 2], dense_w_down[l // 2],
                          tm=1024, tf=1408)
            if last:
                h = final_rmsnorm(h, final_norm, tm=512)
        else:
            h = moe_layer(h, ffn_norm[l], moe_router[l // 2], moe_w_gate[l // 2], moe_w_up[l // 2],
                          moe_w_down[l // 2], final_norm, final_norm=last)
    return h.reshape(bsz, seq, d)
```

```python
import functools
import math

import jax
import jax.numpy as jnp
from jax import lax
from jax.experimental import pallas as pl
from jax.experimental.pallas import tpu as pltpu

F32 = jnp.float32
BF16 = jnp.bfloat16

SSM_WIDTH = 256
SSM_CH = 16
SSM_GROUPS = 16
SSM_STATE = 64
MLA_HEADS = 6
MLA_NOPE = 64
MLA_ROPE = 32
MLA_V = 64
MLA_Q_RANK = 192
MLA_KV_RANK = 128
NSA_HEADS = 6
NSA_KV_HEADS = 2
NSA_REP = 3
NSA_DIM = 64
CMP_BLOCK = 32
CMP_STRIDE = 16
SEL_BLOCK = 64
SEL_TOPN = 8
WINDOW = 256
REL_BUCKETS = 32
REL_MAX_DIST = 128
XATTN_HEADS = 4
N_EXPERTS = 8
ROPE_THETA = 10000.0
EPS = 1e-6
NEG_INF = -1e30
FORCE = 1e9
LOG2E = math.log2(math.e)

LANES = 128
VMEM_LIMIT = 56 * 1024 * 1024

W_CQ = 2 * LANES
W_KR = 2 * LANES
W_NQ = NSA_HEADS * LANES
W_NKV = 6 * LANES
_WIDTHS = (SSM_WIDTH, W_CQ, W_KR, W_NQ, W_NKV, MLA_KV_RANK, LANES)
C_U, C_CQ, C_KR, C_NQ, C_NKV, C_CKV, C_GATE = (sum(_WIDTHS[:k]) for k in range(len(_WIDTHS)))
IN_COLS_PACKED = sum(_WIDTHS)
assert all(c % w == 0 for c, w in zip((C_U, C_CQ, C_KR, C_NQ, C_NKV, C_CKV, C_GATE), _WIDTHS))


def _cparams(sem):
    return pltpu.CompilerParams(dimension_semantics=sem, vmem_limit_bytes=VMEM_LIMIT)


def _dot(a, b):
    return jnp.dot(a, b, preferred_element_type=F32)


def _dot_t(a, b):
    return lax.dot_general(a, b, (((1,), (1,)), ((), ())), preferred_element_type=F32)


def _rms(x, g, n):
    ms = jnp.sum(x * x, axis=-1, keepdims=True) * (1.0 / n)
    return x * lax.rsqrt(ms + EPS) * g


def _sigmoid(x):
    return 1.0 / (1.0 + jnp.exp(-x))


def _silu(x):
    return x * _sigmoid(x)


def _swiglu_accumulate(x_ref, wg, wu, wd, acc_ref, parts=2):
    rows = x_ref.shape[0] // parts
    sl = [slice(k * rows, (k + 1) * rows) for k in range(parts)]
    x = [x_ref[s, :] for s in sl]
    g = [_dot(xk, wg) for xk in x]
    u = [_dot(xk, wu) for xk in x]
    a = [(_silu(gk) * uk).astype(BF16) for gk, uk in zip(g, u)]
    d = [_dot(ak, wd) for ak in a]
    for s, dk in zip(sl, d):
        acc_ref[s, :] += dk


def _norm_mm_kernel(x_ref, g_ref, w_ref, o_ref, xn_ref):
    @pl.when(pl.program_id(1) == 0)
    def _():
        x = x_ref[...].astype(F32)
        xn_ref[...] = _rms(x, g_ref[...], x.shape[-1]).astype(BF16)

    o_ref[...] = _dot(xn_ref[...], w_ref[...]).astype(o_ref.dtype)


def _norm_mm_wide_kernel(x_ref, g_ref, w_ref, o_ref):
    rows = x_ref.shape[0] // 2
    sl = [slice(k * rows, (k + 1) * rows) for k in range(2)]
    xn = [_rms(x_ref[s, :].astype(F32), g_ref[...], x_ref.shape[-1]).astype(BF16) for s in sl]
    for s, xk in zip(sl, xn):
        o_ref[s, :] = _dot(xk, w_ref[...]).astype(o_ref.dtype)


def norm_matmul(x, g, w, *, tm, tn, out_dtype):
    m, k = x.shape
    n = w.shape[1]
    if tn == n:
        return pl.pallas_call(
            _norm_mm_wide_kernel,
            out_shape=jax.ShapeDtypeStruct((m, n), out_dtype),
            grid=(m // tm,),
            in_specs=[pl.BlockSpec((tm, k), lambda i: (i, 0)),
                      pl.BlockSpec((1, k), lambda i: (0, 0)),
                      pl.BlockSpec((k, n), lambda i: (0, 0))],
            out_specs=pl.BlockSpec((tm, n), lambda i: (i, 0)),
            compiler_params=_cparams(("parallel",)),
            name="norm_matmul_wide",
        )(x, g.reshape(1, k).astype(F32), w)
    return pl.pallas_call(
        _norm_mm_kernel,
        out_shape=jax.ShapeDtypeStruct((m, n), out_dtype),
        grid=(m // tm, n // tn),
        in_specs=[pl.BlockSpec((tm, k), lambda i, j: (i, 0)),
                  pl.BlockSpec((1, k), lambda i, j: (0, 0)),
                  pl.BlockSpec((k, tn), lambda i, j: (0, j))],
        out_specs=pl.BlockSpec((tm, tn), lambda i, j: (i, j)),
        scratch_shapes=[pltpu.VMEM((tm, k), BF16)],
        compiler_params=_cparams(("parallel", "arbitrary")),
        name="norm_matmul",
    )(x, g.reshape(1, k).astype(F32), w)


def _ssm_kernel(u_ref, bbr_ref, bbi_ref, ar_ref, ai_ref, ccr_ref, cci_ref, d_ref, wglu_ref,
                o_ref, hr_ref, hi_ref, cr_ref, ci_ref, *, tc, nb):
    @pl.when(pl.program_id(0) == 0)
    def _():
        cr_ref[...] = jnp.zeros_like(cr_ref)
        ci_ref[...] = jnp.zeros_like(ci_ref)

    u = u_ref[...]
    hr_ref[...] = _dot(u, bbr_ref[...])
    hi_ref[...] = _dot(u, bbi_ref[...])
    gp = ar_ref.shape[-1]
    ar = jnp.broadcast_to(ar_ref[...], (nb, gp))
    ai = jnp.broadcast_to(ai_ref[...], (nb, gp))

    def step(t, carry):
        hr, hi = carry
        rows = pl.ds(pl.multiple_of(t * nb, nb), nb)
        nr = ar * hr - ai * hi + hr_ref[rows, :]
        ni = ar * hi + ai * hr + hi_ref[rows, :]
        hr_ref[rows, :] = nr
        hi_ref[rows, :] = ni
        return nr, ni

    hr, hi = lax.fori_loop(0, tc, step, (cr_ref[...], ci_ref[...]))
    cr_ref[...] = hr
    ci_ref[...] = hi
    y = (_dot(hr_ref[...].astype(BF16), ccr_ref[...]) + _dot(hi_ref[...].astype(BF16), cci_ref[...])
         + d_ref[...] * u.astype(F32))
    y = jax.nn.gelu(y)
    z = _dot(y.astype(BF16), wglu_ref[...])
    o_ref[...] = (y * _sigmoid(z)).astype(o_ref.dtype)


def ssm_mixer(u_tm, a_re, a_im, b_re, b_im, c_re, c_im, d, log_dt, w_glu, *, nb, tc):
    rows = u_tm.shape[0]
    G, P, C = SSM_GROUPS, SSM_STATE, SSM_CH
    dt = jnp.exp(log_dt.astype(F32))[:, None]
    lr, li = a_re.astype(F32), a_im.astype(F32)
    mag = jnp.exp(lr * dt)
    ab_r, ab_i = mag * jnp.cos(li * dt), mag * jnp.sin(li * dt)
    den = lr * lr + li * li
    nr = ab_r - 1.0
    f_r = (nr * lr + ab_i * li) / den
    f_i = (ab_i * lr - nr * li) / den
    br, bi = b_re.astype(F32), b_im.astype(F32)
    bb_r = f_r[..., None] * br - f_i[..., None] * bi
    bb_i = f_r[..., None] * bi + f_i[..., None] * br
    eye = jnp.eye(G, dtype=F32)
    bbr = jnp.einsum('gpc,gh->gchp', bb_r, eye).reshape(G * C, G * P).astype(BF16)
    bbi = jnp.einsum('gpc,gh->gchp', bb_i, eye).reshape(G * C, G * P).astype(BF16)
    ccr = jnp.einsum('gcp,gh->gphc', c_re.astype(F32), eye).reshape(G * P, G * C).astype(BF16)
    cci = jnp.einsum('gcp,gh->gphc', -c_im.astype(F32), eye).reshape(G * P, G * C).astype(BF16)
    gp = G * P
    full = lambda shape: pl.BlockSpec(shape, lambda i: (0,) * len(shape))
    return pl.pallas_call(
        functools.partial(_ssm_kernel, tc=tc, nb=nb),
        out_shape=jax.ShapeDtypeStruct((rows, SSM_WIDTH), BF16),
        grid=(rows // (tc * nb),),
        in_specs=[pl.BlockSpec((tc * nb, SSM_WIDTH), lambda i: (i, 0)),
                  full((G * C, gp)), full((G * C, gp)), full((1, gp)), full((1, gp)),
                  full((gp, G * C)), full((gp, G * C)), full((1, SSM_WIDTH)),
                  full((SSM_WIDTH, SSM_WIDTH))],
        out_specs=pl.BlockSpec((tc * nb, SSM_WIDTH), lambda i: (i, 0)),
        scratch_shapes=[pltpu.VMEM((tc * nb, gp), F32), pltpu.VMEM((tc * nb, gp), F32),
                        pltpu.VMEM((nb, gp), F32), pltpu.VMEM((nb, gp), F32)],
        compiler_params=_cparams(("arbitrary",)),
        name="ssm_mixer",
    )(u_tm, bbr, bbi, ab_r.reshape(1, gp), ab_i.reshape(1, gp), ccr, cci,
      d.reshape(1, SSM_WIDTH).astype(F32), w_glu.astype(BF16))


def _mla_prep_kernel(cq_ref, kr_ref, ckv_ref, gq_ref, gkv_ref, wqa_ref, wqb_ref, wk_ref, wv_ref,
                     c1_ref, c0_ref, s0_ref, q_ref, k_ref, v_ref):
    qn = _rms(cq_ref[0].astype(F32), gq_ref[...], MLA_Q_RANK).astype(BF16)
    qa = _dot(qn, wqa_ref[...])
    qb = _dot(qn, wqb_ref[...])
    kn = _rms(ckv_ref[0].astype(F32), gkv_ref[...], MLA_KV_RANK).astype(BF16)
    ka = _dot(kn, wk_ref[...])
    va = _dot(kn, wv_ref[...])
    kr = kr_ref[0].astype(F32)
    c1, c0, s0 = c1_ref[...], c0_ref[...], s0_ref[...]
    krope = kr[:, :LANES] * c0 + kr[:, LANES:] * s0
    low_half = lax.broadcasted_iota(jnp.int32, (kr.shape[0], LANES), 1) < MLA_V
    for h in range(MLA_HEADS):
        sl = slice(h * LANES, (h + 1) * LANES)
        q_ref[0, h] = (qa[:, sl] * c1 + qb[:, sl] * s0).astype(BF16)
        k_ref[0, h] = (ka[:, sl] + krope).astype(BF16)
        ones = jnp.where(low_half if h % 2 == 0 else jnp.logical_not(low_half), 1.0, 0.0)
        v_ref[0, h] = jnp.concatenate([va[:, sl], ones], axis=1).astype(BF16)


def _mla_flash_kernel(q_ref, k_ref, v_ref, o_ref, m_ref, acc_ref, *, tq):
    qi = pl.program_id(1)
    m_ref[...] = jnp.full_like(m_ref, NEG_INF)
    acc_ref[...] = jnp.zeros_like(acc_ref)
    lane = lax.broadcasted_iota(jnp.int32, (tq, LANES), 1)
    rep = tq // LANES

    def tile(kt, masked):
        ks = pl.ds(pl.multiple_of(kt * tq, tq), tq)
        if masked:
            mask = (lax.broadcasted_iota(jnp.int32, (tq, tq), 1)
                    <= lax.broadcasted_iota(jnp.int32, (tq, tq), 0))
        hs = range(MLA_HEADS)
        s = [_dot_t(q_ref[0, h], k_ref[0, h, ks, :]) for h in hs]
        if masked:
            s = [jnp.where(mask, x, NEG_INF) for x in s]
        m_prev = [m_ref[h] for h in hs]
        m_new = [jnp.maximum(mp, jnp.max(x, axis=-1, keepdims=True)) for mp, x in zip(m_prev, s)]
        alpha = [jnp.exp2(mp - mn) for mp, mn in zip(m_prev, m_new)]
        p = [jnp.exp2(x - jnp.tile(mn, (1, rep))).astype(BF16) for x, mn in zip(s, m_new)]
        pv = [_dot(p[h], v_ref[0, h, ks, :]) for h in hs]
        for h in hs:
            m_ref[h] = m_new[h]
        for pr in range(MLA_HEADS // 2):
            a = jnp.tile(jnp.where(lane < MLA_V, alpha[2 * pr], alpha[2 * pr + 1]), (1, 2))
            acc_ref[pr] = acc_ref[pr] * a + pv[2 * pr] + pv[2 * pr + 1]

    def body(kt, c):
        tile(kt, False)
        return c

    lax.fori_loop(0, qi, body, 0)
    tile(qi, True)
    for pr in range(MLA_HEADS // 2):
        o_ref[0, :, pr * LANES:(pr + 1) * LANES] = (acc_ref[pr, :, :LANES] / acc_ref[pr, :, LANES:]).astype(o_ref.dtype)


def _rope_tables(seq):
    pos = jnp.arange(seq, dtype=F32)
    inv = 1.0 / (ROPE_THETA ** (jnp.arange(0, MLA_ROPE, 2, dtype=F32) / MLA_ROPE))
    ang = pos[:, None] * inv[None, :]
    cos, sin = jnp.cos(ang), jnp.sin(ang)
    cos2 = jnp.concatenate([cos, cos], axis=-1)
    sin2 = jnp.concatenate([sin, sin], axis=-1)
    z64 = jnp.zeros((seq, MLA_NOPE), F32)
    z32 = jnp.zeros((seq, LANES - MLA_NOPE - MLA_ROPE), F32)
    c1 = jnp.concatenate([jnp.ones((seq, MLA_NOPE), F32), cos2, z32], axis=-1)
    c0 = jnp.concatenate([z64, cos2, z32], axis=-1)
    s0 = jnp.concatenate([z64, sin2, z32], axis=-1)
    return c1, c0, s0


def _rot_half_cols(w):
    half = MLA_ROPE // 2
    return jnp.concatenate([-w[..., half:], w[..., :half]], axis=-1)


def mla_mixer(proj3, q_norm, w_uq, kv_norm, w_ukv, tabs, *, tm, tq):
    bsz, seq, _ = proj3.shape
    H = MLA_HEADS
    scale = (MLA_NOPE + MLA_ROPE) ** -0.5 * LOG2E
    wq =(w_uq.astype(F32) * scale).reshape(MLA_Q_RANK, H, MLA_NOPE + MLA_ROPE)
    zq = jnp.zeros((MLA_Q_RANK, H, LANES - MLA_NOPE - MLA_ROPE), F32)
    z64 = jnp.zeros((MLA_Q_RANK, H, MLA_NOPE), F32)
    wqa = jnp.concatenate([wq, zq], axis=-1).reshape(MLA_Q_RANK, H * LANES)
    wqb = jnp.concatenate([z64, _rot_half_cols(wq[..., MLA_NOPE:]), zq], axis=-1).reshape(MLA_Q_RANK, H * LANES)
    padq = ((0, W_CQ - MLA_Q_RANK), (0, 0))
    wqa = jnp.pad(wqa, padq).astype(BF16)
    wqb = jnp.pad(wqb, padq).astype(BF16)
    gq = jnp.pad(q_norm.astype(F32), (0, W_CQ - MLA_Q_RANK)).reshape(1, W_CQ)
    wkv = w_ukv.astype(F32).reshape(MLA_KV_RANK, H, MLA_NOPE + MLA_V)
    zk = jnp.zeros((MLA_KV_RANK, H, MLA_NOPE), F32)
    wk = jnp.concatenate([wkv[..., :MLA_NOPE], zk], axis=-1).reshape(MLA_KV_RANK, H * LANES).astype(BF16)
    wv_h = wkv[..., MLA_NOPE:]
    even = (jnp.arange(H) % 2 == 0)[None, :, None]
    wv = jnp.concatenate([jnp.where(even, wv_h, 0.0), jnp.where(even, 0.0, wv_h)], axis=-1)
    wv = wv.reshape(MLA_KV_RANK, H * LANES).astype(BF16)
    c1, c0, s0 = tabs
    full2 = lambda shape: pl.BlockSpec(shape, lambda b, i: (0,) * len(shape))
    tab_spec = pl.BlockSpec((tm, LANES), lambda b, i: (i, 0))
    hd_spec = pl.BlockSpec((1, H, tm, LANES), lambda b, i: (b, 0, i, 0))
    hd_shape = jax.ShapeDtypeStruct((bsz, H, seq, LANES), BF16)
    v_spec = pl.BlockSpec((1, H, tm, 2 * LANES), lambda b, i: (b, 0, i, 0))
    v_shape = jax.ShapeDtypeStruct((bsz, H, seq, 2 * LANES), BF16)
    q, k, v = pl.pallas_call(
        _mla_prep_kernel,
        out_shape=(hd_shape, hd_shape, v_shape),
        grid=(bsz, seq // tm),
        in_specs=[pl.BlockSpec((1, tm, W_CQ), lambda b, i: (b, i, C_CQ // W_CQ)),
                  pl.BlockSpec((1, tm, W_KR), lambda b, i: (b, i, C_KR // W_KR)),
                  pl.BlockSpec((1, tm, MLA_KV_RANK), lambda b, i: (b, i, C_CKV // MLA_KV_RANK)),
                  full2((1, W_CQ)), full2((1, MLA_KV_RANK)),
                  full2((W_CQ, H * LANES)), full2((W_CQ, H * LANES)),
                  full2((MLA_KV_RANK, H * LANES)), full2((MLA_KV_RANK, H * LANES)),
                  tab_spec, tab_spec, tab_spec],
        out_specs=(hd_spec, hd_spec, v_spec),
        compiler_params=_cparams(("parallel", "parallel")),
        name="mla_prep",
    )(proj3, proj3, proj3, gq, kv_norm.astype(F32).reshape(1, MLA_KV_RANK), wqa, wqb, wk, wv, c1, c0, s0)

    return pl.pallas_call(
        functools.partial(_mla_flash_kernel, tq=tq),
        out_shape=jax.ShapeDtypeStruct((bsz, seq, H * MLA_V), BF16),
        grid=(bsz, seq // tq),
        in_specs=[pl.BlockSpec((1, H, tq, LANES), lambda b, i: (b, 0, i, 0)),
                  pl.BlockSpec((1, H, seq, LANES), lambda b, i: (b, 0, 0, 0)),
                  pl.BlockSpec((1, H, seq, 2 * LANES), lambda b, i: (b, 0, 0, 0))],
        out_specs=pl.BlockSpec((1, tq, H * MLA_V), lambda b, i: (b, i, 0)),
        scratch_shapes=[pltpu.VMEM((H, tq, LANES), F32), pltpu.VMEM((H // 2, tq, 2 * LANES), F32)],
        compiler_params=_cparams(("parallel", "arbitrary")),
        name="mla_flash",
    )(q, k, v)


def _nsa_cmp_kernel(x_ref, pea_ref, peb_ref, w1a_ref, w1b_ref, w2_ref, o_ref):
    x = x_ref[0, 0]
    w1a, w1b = w1a_ref[0], w1b_ref[0]
    bias = _dot(pea_ref[0], w1a)[0:1] + _dot(peb_ref[0], w1b)[0:1]
    a = _dot(x, w1a)
    b = _dot(x, w1b)
    n = b.shape[0]
    pre = a + pltpu.roll(b, n - 1, 0) + bias
    o_ref[0, 0] = _dot(jax.nn.gelu(pre).astype(BF16), w2_ref[0]).astype(o_ref.dtype)


def nsa_compress(kv_cr, cmp_pe, cmp_w1, cmp_w2):
    bsz, _, nch, width = kv_cr.shape
    G, dh = NSA_KV_HEADS, NSA_DIM
    half = CMP_BLOCK // 2
    eye = jnp.eye(G, dtype=F32)
    w1r = cmp_w1.astype(F32).reshape(2, CMP_BLOCK, dh, dh)
    w1a = jnp.einsum('kpde,gh->kpgdhe', w1r[:, :half], eye).reshape(2, width, G * dh).astype(BF16)
    w1b = jnp.einsum('kpde,gh->kpgdhe', w1r[:, half:], eye).reshape(2, width, G * dh).astype(BF16)
    w2 = jnp.einsum('kde,gh->kgdhe', cmp_w2.astype(F32), eye).reshape(2, G * dh, G * dh).astype(BF16)
    pe = cmp_pe.astype(F32)
    pe_g = jnp.broadcast_to(pe[:, :, None, :], (2, CMP_BLOCK, G, dh))
    pea = jnp.broadcast_to(pe_g[:, :half].reshape(2, 1, width), (2, 8, width)).astype(BF16)
    peb = jnp.broadcast_to(pe_g[:, half:].reshape(2, 1, width), (2, 8, width)).astype(BF16)
    kvspec = lambda shape: pl.BlockSpec(shape, lambda b, k: (k,) + (0,) * (len(shape) - 1))
    return pl.pallas_call(
        _nsa_cmp_kernel,
        out_shape=jax.ShapeDtypeStruct((bsz, 2, nch, G * dh), BF16),
        grid=(bsz, 2),
        in_specs=[pl.BlockSpec((1, 1, nch, width), lambda b, k: (b, k, 0, 0)),
                  kvspec((1, 8, width)), kvspec((1, 8, width)),
                  kvspec((1, width, G * dh)), kvspec((1, width, G * dh)),
                  kvspec((1, G * dh, G * dh))],
        out_specs=pl.BlockSpec((1, 1, nch, G * dh), lambda b, k: (b, k, 0, 0)),
        compiler_params=_cparams(("parallel", "parallel")),
        name="nsa_compress",
    )(kv_cr, pea, peb, w1a, w1b, w2)


def _nsa_kernel(q_ref, gate_ref, ksl_ref, vsl_ref, kwn_ref, vwn_ref, kvc_ref, bc_ref, bw_ref,
                bs_ref, ov_ref, blk1h_ref, gexp_ref, o_ref, kaug_ref, vaug_ref, m_ref, acc_ref,
                *, tq, tk, nbs, n_sel):
    qi = pl.program_id(1)
    R, G = NSA_REP, NSA_KV_HEADS
    H = R * G
    nsub = tk // tq
    nwin = WINDOW // tq

    @pl.when(qi == 0)
    def _():
        kaug_ref[:, :LANES] = ksl_ref[0]
        kaug_ref[:, LANES:] = blk1h_ref[...]
        vaug_ref[:, :LANES] = vsl_ref[0]
        vaug_ref[:, LANES:] = jnp.ones((vaug_ref.shape[0], LANES), BF16)

    lane = lax.broadcasted_iota(jnp.int32, (tq, LANES), 1)
    t_row = qi * tq + lax.broadcasted_iota(jnp.int32, (H * tq, 1), 0) % tq
    kc = kvc_ref[0, 0]
    vc = kvc_ref[0, 1]
    ov = ov_ref[...]

    def stack(fn):
        return jnp.concatenate([fn(h) for h in range(H)], axis=0)

    q_all = stack(lambda h: q_ref[0, :, h * LANES:(h + 1) * LANES])

    c = jnp.minimum(qi, nwin)
    ws = pl.ds(pl.multiple_of(jnp.maximum(qi - nwin, 0) * tq, tq), (nwin + 1) * tq)
    k_w = kwn_ref[0, ws, :]
    v_w = vwn_ref[0, ws, :]
    o_w = []
    for g in range(G):
        rows = slice(g * R * tq, (g + 1) * R * tq)
        s_w = _dot_t(q_all[rows], k_w) + jnp.concatenate([bw_ref[g * R + r, c] for r in range(R)], axis=0)
        p_w = jnp.exp2(s_w - jnp.max(s_w, axis=-1, keepdims=True))
        o_w.append(_dot(p_w.astype(BF16), v_w) / jnp.sum(p_w, axis=-1, keepdims=True))
    o_w = jnp.concatenate(o_w, axis=0)

    valid = t_row >= (CMP_BLOCK - 1)
    s = _dot_t(q_all, kc) + stack(lambda h: bc_ref[h])
    m = jnp.max(s, axis=-1, keepdims=True)
    p = jnp.where(valid, jnp.exp2(s - m), 0.0)
    l = jnp.where(valid, jnp.sum(p, axis=-1, keepdims=True), 1.0)
    pc = p / l
    o_c = _dot(pc.astype(BF16), vc)
    blk = lax.broadcasted_iota(jnp.int32, (nbs, tq), 0)
    tl = qi * tq + lax.broadcasted_iota(jnp.int32, (nbs, tq), 1)
    cur = tl // SEL_BLOCK
    forced = (blk == 0) | (blk == cur) | (blk == cur - 1)
    future = blk * SEL_BLOCK > tl
    qmask = []
    for g in range(G):
        b0 = g * R * tq
        psum = pc[b0:b0 + tq] + pc[b0 + tq:b0 + 2 * tq] + pc[b0 + 2 * tq:b0 + 3 * tq]
        p_hi = psum.astype(BF16)
        p_lo = (psum - p_hi.astype(F32)).astype(BF16)
        imp = (_dot_t(ov, p_hi) + _dot_t(ov, p_lo))[:nbs]
        imp = jnp.where(forced, FORCE, jnp.where(future, -FORCE, imp))
        rank = jnp.zeros((nbs, tq), F32)
        for i in range(nbs):
            ri = imp[i:i + 1, :]
            beats = (ri > imp) | ((ri == imp) & (blk > i))
            rank = rank + jnp.where(beats, 1.0, 0.0)
        sel = jnp.where(rank < n_sel, 0.0, NEG_INF)
        sel = jnp.concatenate([sel, jnp.zeros((LANES - nbs, tq), F32)], axis=0).T.astype(BF16)
        qmask += [sel] * R
    q_aug = jnp.concatenate([q_all, jnp.concatenate(qmask, axis=0)], axis=1)

    m_ref[...] = jnp.full_like(m_ref, NEG_INF)
    acc_ref[...] = jnp.zeros_like(acc_ref)

    halves = [slice(g * R * tq, (g + 1) * R * tq) for g in range(G)]

    def sel_tile(kt, bias):
        ks = pl.ds(pl.multiple_of(kt * tk, tk), tk)
        k_t = kaug_ref[ks, :]
        v_t = vaug_ref[ks, :]
        s = [_dot_t(q_aug[hs], k_t) for hs in halves]
        if bias is not None:
            s = [x + bias[hs] for x, hs in zip(s, halves)]
        m_prev = [m_ref[hs, :] for hs in halves]
        m_new = [jnp.maximum(mp, jnp.max(x, axis=-1, keepdims=True)) for mp, x in zip(m_prev, s)]
        alpha = [jnp.exp2(mp - mn) for mp, mn in zip(m_prev, m_new)]
        p = [jnp.exp2(x - jnp.tile(mn, (1, tk // LANES))).astype(BF16) for x, mn in zip(s, m_new)]
        pv = [_dot(pp, v_t) for pp in p]
        for hs, mn, a, o in zip(halves, m_new, alpha, pv):
            m_ref[hs, :] = mn
            acc_ref[hs, :] = acc_ref[hs, :] * jnp.tile(a, (1, 2)) + o

    def near_bias(kt):
        cols = []
        for sub in range(nsub):
            d = qi - (kt * nsub + sub)
            cols.append(stack(lambda h: jnp.where(d == 0, bs_ref[h, 0], jnp.where(
                d == 1, bs_ref[h, 1], jnp.where(d < 0, NEG_INF, 0.0)))))
        return jnp.concatenate(cols, axis=1)

    def far_body(kt, c):
        sel_tile(kt, None)
        return c

    kd = (qi * tq) // tk
    lax.fori_loop(0, jnp.maximum(kd - 1, 0), far_body, 0)

    @pl.when(kd >= 1)
    def _():
        sel_tile(kd - 1, near_bias(kd - 1))

    sel_tile(kd, near_bias(kd))
    o_s = acc_ref[:, :LANES] / acc_ref[:, LANES:]

    gates = _sigmoid(_dot(gate_ref[0], gexp_ref[...]))
    for r in range(R):
        res = None
        for b, o_b in enumerate((o_c, o_s, o_w)):
            o_br = jnp.where(lane < NSA_DIM, o_b[r * tq:(r + 1) * tq], o_b[(R + r) * tq:(R + r + 1) * tq])
            term = gates[:, (b * R + r) * LANES:(b * R + r + 1) * LANES] * o_br
            res = term if res is None else res + term
        o_ref[0, :, r * LANES:(r + 1) * LANES] = res.astype(o_ref.dtype)


def _t5_bucket(dist):
    n = jnp.maximum(dist, 0)
    exact = REL_BUCKETS // 2
    nf = jnp.maximum(n, exact).astype(F32)
    large = exact + jnp.floor(jnp.log(nf / exact) / math.log(REL_MAX_DIST / exact)
                              * (REL_BUCKETS - exact)).astype(jnp.int32)
    return jnp.where(n < exact, n, jnp.minimum(large, REL_BUCKETS - 1))


def _nsa_bias_tables(rel_bias, seq, tq):
    rb = rel_bias.astype(F32).T
    far = rb[:, REL_BUCKETS - 1].reshape(NSA_HEADS, 1, 1)

    def by_dist(dist, ok, shift=0.0):
        bucket = _t5_bucket(dist)[None]
        out = jnp.zeros((NSA_HEADS,) + dist.shape, F32)
        for k in range(REL_BUCKETS):
            out = jnp.where(bucket == k, rb[:, k].reshape((NSA_HEADS,) + (1,) * dist.ndim), out)
        return jnp.where(ok[None], (out - shift) * LOG2E, NEG_INF)

    i = jnp.arange(tq)[:, None]
    t = jnp.arange(seq)[:, None]
    dist_c = t - (jnp.arange(LANES)[None, :] * CMP_STRIDE + CMP_BLOCK - 1)
    bc = by_dist(dist_c, dist_c >= 0)
    nwin = WINDOW // tq
    jw = jnp.arange((nwin + 1) * tq)[None, :]
    bw = jnp.stack([by_dist(tq * c + i - jw, (tq * c + i - jw >= 0) & (tq * c + i - jw < WINDOW))
                    for c in range(nwin + 1)], axis=1)
    js = jnp.arange(tq)[None, :]
    bs = jnp.stack([by_dist(tq * c + i - js, tq * c + i - js >= 0, far) for c in range(2)], axis=1)
    ci = jnp.arange(LANES)[:, None]
    sj = jnp.arange(LANES)[None, :]
    nbs = seq // SEL_BLOCK
    ov = ((ci * CMP_STRIDE <= sj * SEL_BLOCK + SEL_BLOCK - 1)
          & (ci * CMP_STRIDE + CMP_BLOCK - 1 >= sj * SEL_BLOCK)
          & (ci < seq // CMP_STRIDE - 1) & (sj < nbs))
    blk1h = (jnp.arange(seq)[:, None] // SEL_BLOCK == sj).astype(BF16)
    col = jnp.arange(3 * NSA_REP * LANES)
    slab, lane_g = col // LANES, (col % LANES) // NSA_DIM
    gate_col = (slab // NSA_REP) * NSA_HEADS + lane_g * NSA_REP + slab % NSA_REP
    gexp = (ci == gate_col[None, :]).astype(BF16)
    return bc, bw, bs, ov.T.astype(BF16), blk1h, gexp


def nsa_mixer(proj3, kvc, tables, *, tq, tk):
    bsz, seq, _ = proj3.shape
    assert tq % LANES == 0 and tq >= REL_MAX_DIST and tk % tq == 0 and seq % tk == 0
    assert seq // CMP_STRIDE == LANES and WINDOW % tq == 0
    nwin = WINDOW // tq
    bc, bw, bs, ov, blk1h, gexp = tables
    nbs = seq // SEL_BLOCK
    H = NSA_HEADS
    slab = lambda j: pl.BlockSpec((1, seq, LANES), lambda b, i: (b, 0, C_NKV // LANES + j))
    const = lambda shape: pl.BlockSpec(shape, lambda b, i: (0,) * len(shape))
    return pl.pallas_call(
        functools.partial(_nsa_kernel, tq=tq, tk=tk, nbs=nbs, n_sel=min(SEL_TOPN, nbs)),
        out_shape=jax.ShapeDtypeStruct((bsz, seq, H * NSA_DIM), BF16),
        grid=(bsz, seq // tq),
        in_specs=[pl.BlockSpec((1, tq, H * LANES), lambda b, i: (b, i, C_NQ // (H * LANES))),
                  pl.BlockSpec((1, tq, LANES), lambda b, i: (b, i, C_GATE // LANES)),
                  slab(2), slab(3), slab(4), slab(5),
                  pl.BlockSpec((1, 2, LANES, LANES), lambda b, i: (b, 0, 0, 0)),
                  pl.BlockSpec((H, tq, LANES), lambda b, i: (0, i, 0)),
                  const((H, nwin + 1, tq, (nwin + 1) * tq)), const((H, 2, tq, tq)),
                  const((LANES, LANES)), const((seq, LANES)), const(gexp.shape)],
        out_specs=pl.BlockSpec((1, tq, H * NSA_DIM), lambda b, i: (b, i, 0)),
        scratch_shapes=[pltpu.VMEM((seq, 2 * LANES), BF16), pltpu.VMEM((seq, 2 * LANES), BF16),
                        pltpu.VMEM((H * tq, LANES), F32), pltpu.VMEM((H * tq, 2 * LANES), F32)],
        compiler_params=_cparams(("parallel", "arbitrary")),
        name="nsa_attention",
    )(proj3, proj3, proj3, proj3, proj3, proj3, kvc, bc, bw, bs, ov, blk1h, gexp)


def _mix_xattn_kernel(ys_ref, ym_ref, yn_ref, h_ref, g1_ref, g2_ref, g3_ref, w1_ref, w2_ref, w3_ref,
                      kv_ref, g_ref, wq_ref, wo_ref, o_ref, *, dh):
    rows = h_ref.shape[1] // 2
    sl = [slice(k * rows, (k + 1) * rows) for k in range(2)]
    hw = XATTN_HEADS * dh
    heads = range(XATTN_HEADS)

    def normed(y_ref, gy_ref):
        return [_rms(y_ref[0, s, :].astype(F32), gy_ref[...], y_ref.shape[-1]).astype(BF16) for s in sl]

    ns, nm, nn = normed(ys_ref, g1_ref), normed(ym_ref, g2_ref), normed(yn_ref, g3_ref)
    h = [h_ref[0, s, :] + _dot(a, w1_ref[...]) + _dot(b, w2_ref[...]) + _dot(c, w3_ref[...])
         for s, a, b, c in zip(sl, ns, nm, nn)]
    xn = [_rms(x, g_ref[...], x.shape[-1]).astype(BF16) for x in h]
    q = [_dot(x, wq_ref[...]).astype(BF16) for x in xn]
    s = [[_dot_t(qk[:, hd * dh:(hd + 1) * dh], kv_ref[0, :, hd * dh:(hd + 1) * dh]) for hd in heads] for qk in q]
    p = [[jnp.exp2(x - jnp.max(x, axis=-1, keepdims=True)) for x in sk] for sk in s]
    p = [[(x / jnp.sum(x, axis=-1, keepdims=True)).astype(BF16) for x in pk] for pk in p]
    o = [jnp.concatenate([_dot(pk[hd], kv_ref[0, :, hw + hd * dh:hw + (hd + 1) * dh]).astype(BF16)
                          for hd in heads], axis=-1) for pk in p]
    for sk, hk, ok in zip(sl, h, o):
        o_ref[0, sk, :] = hk + _dot(ok, wo_ref[...])


def mix_out_cross_attention(ys3, h3, gains, weights, kv3, g_x, wq, wo, *, tm):
    bsz, seq, d = h3.shape
    m = kv3.shape[1]
    dh = d // XATTN_HEADS
    wq_s = (wq.astype(F32) * (dh ** -0.5 * LOG2E)).astype(BF16)
    gains = [g.reshape(1, -1).astype(F32) for g in gains]
    const = lambda shape: pl.BlockSpec(shape, lambda b, i: (0,) * len(shape))
    row = lambda w: pl.BlockSpec((1, tm, w), lambda b, i: (b, i, 0))
    return pl.pallas_call(
        functools.partial(_mix_xattn_kernel, dh=dh),
        out_shape=jax.ShapeDtypeStruct((bsz, seq, d), F32),
        grid=(bsz, seq // tm),
        in_specs=[row(y.shape[2]) for y in ys3] + [row(d)]
                 + [const(g.shape) for g in gains] + [const(w.shape) for w in weights]
                 + [pl.BlockSpec((1, m, 2 * d), lambda b, i: (b, 0, 0)),
                    const((1, d)), const((d, d)), const((d, d))],
        out_specs=row(d),
        compiler_params=_cparams(("parallel", "parallel")),
        name="mix_out_cross_attention",
    )(*ys3, h3, *gains, *weights, kv3, g_x.reshape(1, d).astype(F32), wq_s, wo.astype(BF16))


def _ffn_kernel(h_ref, g_ref, wg_ref, wu_ref, wd_ref, o_ref, xn_ref, acc_ref):
    j = pl.program_id(1)

    @pl.when(j == 0)
    def _():
        h = h_ref[...]
        xn_ref[...] = _rms(h, g_ref[...], h.shape[-1]).astype(BF16)
        acc_ref[...] = h

    _swiglu_accumulate(xn_ref, wg_ref[...], wu_ref[...], wd_ref[...], acc_ref)

    @pl.when(j == pl.num_programs(1) - 1)
    def _():
        o_ref[...] = acc_ref[...]


def dense_ffn(h, g, wg, wu, wd, *, tm, tf):
    m, d = h.shape
    ff = wg.shape[1]
    return pl.pallas_call(
        _ffn_kernel,
        out_shape=jax.ShapeDtypeStruct((m, d), F32),
        grid=(m // tm, ff // tf),
        in_specs=[pl.BlockSpec((tm, d), lambda i, j: (i, 0)),
                  pl.BlockSpec((1, d), lambda i, j: (0, 0)),
                  pl.BlockSpec((d, tf), lambda i, j: (0, j)),
                  pl.BlockSpec((d, tf), lambda i, j: (0, j)),
                  pl.BlockSpec((tf, d), lambda i, j: (j, 0))],
        out_specs=pl.BlockSpec((tm, d), lambda i, j: (i, 0)),
        scratch_shapes=[pltpu.VMEM((tm, d), BF16), pltpu.VMEM((tm, d), F32)],
        compiler_params=_cparams(("parallel", "arbitrary")),
        name="dense_ffn",
    )(h, g.reshape(1, d).astype(F32), wg.astype(BF16), wu.astype(BF16), wd.astype(BF16))


def _router_kernel(h_ref, g_ref, wr_hi_ref, wr_lo_ref, xn_ref, info_ref, cnt_ref, carry_ref, *, tm):
    i = pl.program_id(0)

    @pl.when(i == 0)
    def _():
        carry_ref[...] = jnp.zeros_like(carry_ref)

    h = h_ref[...]
    xn = _rms(h, g_ref[...], h.shape[-1])
    xn_ref[...] = xn
    x_hi = xn.astype(BF16)
    x_lo = (xn - x_hi.astype(F32)).astype(BF16)
    logits = _dot(x_hi, wr_hi_ref[...]) + _dot(x_lo, wr_hi_ref[...]) + _dot(x_hi, wr_lo_ref[...])
    lane = lax.broadcasted_iota(jnp.int32, (tm, LANES), 1)
    lanef = lane.astype(F32)
    logits = jnp.where(lane < N_EXPERTS, logits, NEG_INF)
    m1 = jnp.max(logits, axis=-1, keepdims=True)
    i1 = jnp.min(jnp.where(logits == m1, lanef, float(LANES)), axis=-1, keepdims=True)
    rest = jnp.where(lanef == i1, NEG_INF, logits)
    m2 = jnp.max(rest, axis=-1, keepdims=True)
    i2 = jnp.min(jnp.where(rest == m2, lanef, float(LANES)), axis=-1, keepdims=True)
    e2 = jnp.exp(m2 - m1)
    w1 = 1.0 / (1.0 + e2)
    w2 = e2 / (1.0 + e2)
    oh1 = lanef == i1
    oh2 = lanef == i2
    oh = jnp.where(oh1 | oh2, 1.0, 0.0)
    rr = lax.broadcasted_iota(jnp.int32, (tm, tm), 0)
    cc = lax.broadcasted_iota(jnp.int32, (tm, tm), 1)
    tri = jnp.where(cc < rr, 1.0, 0.0).astype(BF16)
    before = _dot(tri, oh.astype(BF16)) + carry_ref[0:1, :]
    r1 = jnp.sum(jnp.where(oh1, before, 0.0), axis=-1, keepdims=True)
    r2 = jnp.sum(jnp.where(oh2, before, 0.0), axis=-1, keepdims=True)
    carry_ref[...] = carry_ref[...] + jnp.sum(oh, axis=0, keepdims=True)
    info = jnp.where(lane == 0, i1, jnp.where(lane == 1, i2, jnp.where(lane == 2, w1, jnp.where(
        lane == 3, w2, jnp.where(lane == 4, r1, jnp.where(lane == 5, r2, 0.0))))))
    info_ref[...] = info
    cnt_ref[...] = carry_ref[...]


def moe_router(h, g, router, *, tm):
    m, d = h.shape
    wr = jnp.pad(router.astype(F32), ((0, 0), (0, LANES - N_EXPERTS)))
    wr_hi = wr.astype(BF16)
    wr_lo = (wr - wr_hi.astype(F32)).astype(BF16)
    return pl.pallas_call(
        functools.partial(_router_kernel, tm=tm),
        out_shape=(jax.ShapeDtypeStruct((m, d), F32), jax.ShapeDtypeStruct((m, LANES), F32),
                   jax.ShapeDtypeStruct((8, LANES), F32)),
        grid=(m // tm,),
        in_specs=[pl.BlockSpec((tm, d), lambda i: (i, 0)),
                  pl.BlockSpec((1, d), lambda i: (0, 0)),
                  pl.BlockSpec((d, LANES), lambda i: (0, 0)),
                  pl.BlockSpec((d, LANES), lambda i: (0, 0))],
        out_specs=(pl.BlockSpec((tm, d), lambda i: (i, 0)),
                   pl.BlockSpec((tm, LANES), lambda i: (i, 0)),
                   pl.BlockSpec((8, LANES), lambda i: (0, 0))),
        scratch_shapes=[pltpu.VMEM((8, LANES), F32)],
        compiler_params=_cparams(("arbitrary",)),
        name="moe_router",
    )(h, g.reshape(1, d).astype(F32), wr_hi, wr_lo)


def _row_copy(src_hbm, row, dst, slot, sem):
    return pltpu.make_async_copy(src_hbm.at[pl.ds(row, 1), :], dst.at[pl.ds(slot, 1), :], sem)


def _rows_wait(src_hbm, dst, sem):
    pltpu.make_async_copy(src_hbm.at[pl.ds(0, dst.shape[0]), :], dst, sem).wait()


def _moe_ffn_kernel(src_ref, texp_ref, nact_ref, x_hbm, wg_ref, wu_ref, wd_ref, o_ref,
                    xbuf, xbf, acc_ref, sem, *, tm, nj):
    i = pl.program_id(0)
    j = pl.program_id(1)
    nact = nact_ref[0]
    active = i < nact
    nbuf = xbuf.shape[0]
    ahead = nbuf - 1
    cur = i % nbuf
    rows_per_step = tm // nj

    for t0 in range(ahead):
        @pl.when((i == 0) & (j == 0) & (t0 < nact))
        def _():
            def issue(s, c):
                _row_copy(x_hbm, src_ref[t0 * tm + s], xbuf.at[t0], s, sem.at[t0]).start()
                return c

            lax.fori_loop(0, tm, issue, 0, unroll=8)

    @pl.when(active & (j == 0))
    def _():
        _rows_wait(x_hbm, xbuf.at[cur], sem.at[cur])
        xbf[...] = xbuf[cur].astype(BF16)
        acc_ref[...] = jnp.zeros_like(acc_ref)

    def compute(prefetch):
        if prefetch:
            nxt = (i + ahead) % nbuf
            base = (i + ahead) * tm + j * rows_per_step
            for k in range(rows_per_step):
                _row_copy(x_hbm, src_ref[base + k], xbuf.at[nxt], j * rows_per_step + k, sem.at[nxt]).start()
        _swiglu_accumulate(xbf, wg_ref[0], wu_ref[0], wd_ref[0], acc_ref)

    @pl.when(i + ahead < nact)
    def _():
        compute(True)

    @pl.when(active & (i + ahead >= nact))
    def _():
        compute(False)

    @pl.when(j == nj - 1)
    def _():
        o_ref[...] = jnp.where(active, acc_ref[...], 0.0)


def moe_expert_ffn(xn, src, tile_expert, n_active, wg, wu, wd, *, tm, tf):
    n_slots = src.shape[0]
    d = xn.shape[1]
    ne, _, ff = wg.shape
    nj = ff // tf
    assert nj * tf == ff and tm % nj == 0
    wg, wu, wd = wg.astype(BF16), wu.astype(BF16), wd.astype(BF16)

    def wmap_col(i, j, src, texp, nact):
        return (texp[i], 0, jnp.where(i < nact[0], j, nj - 1))

    def wmap_row(i, j, src, texp, nact):
        return (texp[i], jnp.where(i < nact[0], j, nj - 1), 0)

    return pl.pallas_call(
        functools.partial(_moe_ffn_kernel, tm=tm, nj=nj),
        out_shape=jax.ShapeDtypeStruct((n_slots, d), F32),
        grid_spec=pltpu.PrefetchScalarGridSpec(
            num_scalar_prefetch=3,
            grid=(n_slots // tm, nj),
            in_specs=[pl.BlockSpec(memory_space=pl.ANY),
                      pl.BlockSpec((1, d, tf), wmap_col),
                      pl.BlockSpec((1, d, tf), wmap_col),
                      pl.BlockSpec((1, tf, d), wmap_row)],
            out_specs=pl.BlockSpec((tm, d), lambda i, j, *_: (i, 0)),
            scratch_shapes=[pltpu.VMEM((3, tm, d), F32), pltpu.VMEM((tm, d), BF16),
                            pltpu.VMEM((tm, d), F32), pltpu.SemaphoreType.DMA((3,))]),
        compiler_params=_cparams(("arbitrary", "arbitrary")),
        name="moe_expert_ffn",
    )(src, tile_expert, n_active, xn, wg, wu, wd)


def _moe_combine_kernel(pos_ref, h_ref, info_ref, ys_hbm, g_ref, o_ref, buf, sem, *, tm, final_norm):
    i = pl.program_id(0)
    n = pl.num_programs(0)
    cur = i % 2

    def start_gather(tile, b):
        for s in range(tm):
            for k in range(2):
                _row_copy(ys_hbm, pos_ref[2 * (tile * tm + s) + k], buf.at[b, k], s, sem.at[b]).start()

    @pl.when(i == 0)
    def _():
        start_gather(0, 0)

    @pl.when(i + 1 < n)
    def _():
        start_gather(i + 1, 1 - cur)

    for k in range(2):
        _rows_wait(ys_hbm, buf.at[cur, k], sem.at[cur])
    info = info_ref[...]
    y = h_ref[...] + info[:, 2:3] * buf[cur, 0] + info[:, 3:4] * buf[cur, 1]
    if final_norm:
        y = _rms(y, g_ref[...], y.shape[-1])
    o_ref[...] = y


def moe_combine(h, info, ys, pos_flat, g_final, *, tm, final_norm):
    m, d = h.shape
    return pl.pallas_call(
        functools.partial(_moe_combine_kernel, tm=tm, final_norm=final_norm),
        out_shape=jax.ShapeDtypeStruct((m, d), F32),
        grid_spec=pltpu.PrefetchScalarGridSpec(
            num_scalar_prefetch=1,
            grid=(m // tm,),
            in_specs=[pl.BlockSpec((tm, d), lambda i, *_: (i, 0)),
                      pl.BlockSpec((tm, LANES), lambda i, *_: (i, 0)),
                      pl.BlockSpec(memory_space=pl.ANY),
                      pl.BlockSpec((1, d), lambda i, *_: (0, 0))],
            out_specs=pl.BlockSpec((tm, d), lambda i, *_: (i, 0)),
            scratch_shapes=[pltpu.VMEM((2, 2, tm, d), F32), pltpu.SemaphoreType.DMA((2,))]),
        compiler_params=_cparams(("arbitrary",)),
        name="moe_combine",
    )(pos_flat, h, info, ys, g_final.reshape(1, d).astype(F32))


def moe_layer(h, g, router, wg, wu, wd, g_final, *, final_norm, tm_r=512, tm_g=512, tf=1792, tm_c=512):
    m, d = h.shape
    xn, info, cnt = moe_router(h, g, router, tm=tm_r)
    e_idx = info[:, 0:2].astype(jnp.int32)
    rank = info[:, 4:6].astype(jnp.int32)
    counts = cnt[0, :N_EXPERTS].astype(jnp.int32)
    tiles_per = (counts + tm_g - 1) // tm_g
    tile_end = jnp.cumsum(tiles_per)
    seg_start = (tile_end - tiles_per) * tm_g
    pos = rank
    for e in range(N_EXPERTS):
        pos = pos + jnp.where(e_idx == e, seg_start[e], 0)
    n_tiles = (2 * m) // tm_g + N_EXPERTS
    n_slots = n_tiles * tm_g
    tok = jnp.broadcast_to(jnp.arange(m, dtype=jnp.int32)[:, None], (m, 2))
    src = jnp.zeros((n_slots,), jnp.int32).at[pos.reshape(-1)].set(tok.reshape(-1))
    n_active = tile_end[-1:].astype(jnp.int32)
    tile_ids = jnp.minimum(jnp.arange(n_tiles, dtype=jnp.int32), n_active[0] - 1)
    tile_expert = jnp.sum(tile_ids[:, None] >= tile_end[None, :], axis=1).astype(jnp.int32)
    ys = moe_expert_ffn(xn, src, tile_expert, n_active,
                        wg, wu, wd, tm=tm_g, tf=tf)
    return moe_combine(h, info, ys, pos.reshape(-1).astype(jnp.int32), g_final, tm=tm_c, final_norm=final_norm)


def _final_norm_kernel(h_ref, g_ref, o_ref):
    h = h_ref[...]
    o_ref[...] = _rms(h, g_ref[...], h.shape[-1])


def final_rmsnorm(h, g, *, tm):
    m, d = h.shape
    return pl.pallas_call(
        _final_norm_kernel,
        out_shape=jax.ShapeDtypeStruct((m, d), F32),
        grid=(m // tm,),
        in_specs=[pl.BlockSpec((tm, d), lambda i: (i, 0)), pl.BlockSpec((1, d), lambda i: (0, 0))],
        out_specs=pl.BlockSpec((tm, d), lambda i: (i, 0)),
        compiler_params=_cparams(("parallel",)),
        name="final_rmsnorm",
    )(h, g.reshape(1, d).astype(F32))


def _pack_w_in(w):
    d = w.shape[0]
    w = w.astype(F32)
    o = 0
    u = w[:, o:o + SSM_WIDTH]; o += SSM_WIDTH
    cq = w[:, o:o + MLA_Q_RANK]; o += MLA_Q_RANK
    ckv = w[:, o:o + MLA_KV_RANK]; o += MLA_KV_RANK
    kr = w[:, o:o + MLA_ROPE]; o += MLA_ROPE
    nq = w[:, o:o + NSA_HEADS * NSA_DIM]; o += NSA_HEADS * NSA_DIM
    nkv = w[:, o:o + 6 * NSA_KV_HEADS * NSA_DIM]; o += 6 * NSA_KV_HEADS * NSA_DIM
    gate = w[:, o:o + 3 * NSA_HEADS]
    z = lambda n: jnp.zeros((d, n), F32)
    kr_a = jnp.concatenate([z(MLA_NOPE), kr, z(LANES - MLA_NOPE - MLA_ROPE)], axis=1)
    kr_b = jnp.concatenate([z(MLA_NOPE), _rot_half_cols(kr), z(LANES - MLA_NOPE - MLA_ROPE)], axis=1)
    nq_h = (nq * (NSA_DIM ** -0.5 * LOG2E)).reshape(d, NSA_KV_HEADS, NSA_REP, NSA_DIM)
    zq = jnp.zeros((d, NSA_REP, NSA_DIM), F32)
    nq_p = jnp.concatenate([
        jnp.concatenate([nq_h[:, 0], zq], axis=-1).reshape(d, NSA_REP * LANES),
        jnp.concatenate([zq, nq_h[:, 1]], axis=-1).reshape(d, NSA_REP * LANES)], axis=1)
    packed = jnp.concatenate([u, cq, z(W_CQ - MLA_Q_RANK), kr_a, kr_b, nq_p, nkv, ckv,
                              gate, z(LANES - 3 * NSA_HEADS)], axis=1)
    assert packed.shape[1] == IN_COLS_PACKED
    return packed.astype(BF16)


def _rg_order(a):
    rest = a.shape[1:]
    return a.reshape((NSA_KV_HEADS, NSA_REP, NSA_DIM) + rest).swapaxes(0, 1).reshape((-1,) + rest)


def kernel(x, mem, w_in, w_out, mix_norm, out_norm, ssm_a_re, ssm_a_im, ssm_b_re, ssm_b_im, ssm_c_re, ssm_c_im, ssm_d, ssm_log_dt, ssm_w_glu, mla_q_norm, mla_w_uq, mla_kv_norm, mla_w_ukv, nsa_cmp_pe, nsa_cmp_w1, nsa_cmp_w2, rel_bias, xattn_norm, mem_norm, xattn_wq, xattn_wkv, xattn_wo, ffn_norm, dense_w_gate, dense_w_up, dense_w_down, moe_router, moe_w_gate, moe_w_up, moe_w_down, final_norm):
    bsz, seq, d = x.shape
    depth = w_in.shape[0]
    T = bsz * seq
    nmem = mem.shape[1]
    tq_nsa, tk_nsa = 2 * LANES, 4 * LANES
    rope_tabs = _rope_tables(seq)
    nsa_tabs = _nsa_bias_tables(rel_bias, seq, tq_nsa)
    o1, o2 = SSM_WIDTH, SSM_WIDTH + MLA_HEADS * MLA_V
    mem2 = mem.reshape(bsz * nmem, d)
    h = x.reshape(T, d)
    for l in range(depth):
        proj = norm_matmul(h, mix_norm[l], _pack_w_in(w_in[l]), tm=1024, tn=IN_COLS_PACKED, out_dtype=BF16)
        proj3 = proj.reshape(bsz, seq, IN_COLS_PACKED)
        u_tm = proj3[:, :, C_U:C_U + SSM_WIDTH].transpose(1, 0, 2).reshape(seq * bsz, SSM_WIDTH)
        y_ssm = ssm_mixer(u_tm, ssm_a_re[l], ssm_a_im[l], ssm_b_re[l], ssm_b_im[l], ssm_c_re[l], ssm_c_im[l],
                          ssm_d[l], ssm_log_dt[l], ssm_w_glu[l], nb=bsz, tc=64)
        y_ssm = y_ssm.reshape(seq, bsz, SSM_WIDTH).transpose(1, 0, 2)
        y_mla = mla_mixer(proj3, mla_q_norm[l], mla_w_uq[l], mla_kv_norm[l], mla_w_ukv[l], rope_tabs,
                          tm=512, tq=512)
        nch = seq // CMP_STRIDE
        kv_cr = jnp.stack([proj3[:, :, C_NKV:C_NKV + LANES].reshape(bsz, nch, CMP_STRIDE * LANES),
                           proj3[:, :, C_NKV + LANES:C_NKV + 2 * LANES].reshape(bsz, nch, CMP_STRIDE * LANES)],
                          axis=1)
        kvc = nsa_compress(kv_cr, nsa_cmp_pe[l], nsa_cmp_w1[l], nsa_cmp_w2[l])
        y_nsa = nsa_mixer(proj3, kvc, nsa_tabs, tq=tq_nsa, tk=tk_nsa)
        g_out = out_norm[l]
        wo_l = w_out[l]
        kv_mem = norm_matmul(mem2, mem_norm[l], xattn_wkv[l].astype(BF16), tm=256, tn=512, out_dtype=BF16)
        h = mix_out_cross_attention(
            [y_ssm, y_mla, y_nsa], h.reshape(bsz, seq, d),
            [g_out[:o1], g_out[o1:o2], _rg_order(g_out[o2:])],
            [wo_l[:o1].astype(BF16), wo_l[o1:o2].astype(BF16), _rg_order(wo_l[o2:]).astype(BF16)],
            kv_mem.reshape(bsz, nmem, 2 * d), xattn_norm[l], xattn_wq[l], xattn_wo[l], tm=512).reshape(T, d)
        last = l == depth - 1
        if l % 2 == 0:
            h = dense_ffn(h, ffn_norm[l], dense_w_gate[l // 2], dense_w_up[l // 2], dense_w_down[l // 2],
                          tm=1024, tf=1408)
            if last:
                h = final_rmsnorm(h, final_norm, tm=512)
        else:
            h = moe_layer(h, ffn_norm[l], moe_router[l // 2], moe_w_gate[l // 2], moe_w_up[l // 2],
                          moe_w_down[l // 2], final_norm, final_norm=last)
    return h.reshape(bsz, seq, d)
```

```python
import functools
import math

import jax
import jax.numpy as jnp
from jax import lax
from jax.experimental import pallas as pl
from jax.experimental.pallas import tpu as pltpu

F32 = jnp.float32
BF16 = jnp.bfloat16

SSM_WIDTH = 256
SSM_CH = 16
SSM_GROUPS = 16
SSM_STATE = 64
MLA_HEADS = 6
MLA_NOPE = 64
MLA_ROPE = 32
MLA_V = 64
MLA_Q_RANK = 192
MLA_KV_RANK = 128
NSA_HEADS = 6
NSA_KV_HEADS = 2
NSA_REP = 3
NSA_DIM = 64
CMP_BLOCK = 32
CMP_STRIDE = 16
SEL_BLOCK = 64
SEL_TOPN = 8
WINDOW = 256
REL_BUCKETS = 32
REL_MAX_DIST = 128
XATTN_HEADS = 4
N_EXPERTS = 8
ROPE_THETA = 10000.0
EPS = 1e-6
NEG_INF = -1e30
FORCE = 1e9
LOG2E = math.log2(math.e)

LANES = 128
VMEM_LIMIT = 56 * 1024 * 1024

W_CQ = 2 * LANES
W_KR = 2 * LANES
W_NQ = NSA_HEADS * LANES
W_NKV = 6 * LANES
_WIDTHS = (SSM_WIDTH, W_CQ, W_KR, W_NQ, W_NKV, MLA_KV_RANK, LANES)
C_U, C_CQ, C_KR, C_NQ, C_NKV, C_CKV, C_GATE = (sum(_WIDTHS[:k]) for k in range(len(_WIDTHS)))
IN_COLS_PACKED = sum(_WIDTHS)
assert all(c % w == 0 for c, w in zip((C_U, C_CQ, C_KR, C_NQ, C_NKV, C_CKV, C_GATE), _WIDTHS))


def _cparams(sem):
    return pltpu.CompilerParams(dimension_semantics=sem, vmem_limit_bytes=VMEM_LIMIT)


def _dot(a, b):
    return jnp.dot(a, b, preferred_element_type=F32)


def _dot_t(a, b):
    return lax.dot_general(a, b, (((1,), (1,)), ((), ())), preferred_element_type=F32)


def _rms(x, g, n):
    ms = jnp.sum(x * x, axis=-1, keepdims=True) * (1.0 / n)
    return x * lax.rsqrt(ms + EPS) * g


def _sigmoid(x):
    return 1.0 / (1.0 + jnp.exp(-x))


def _silu(x):
    return x * _sigmoid(x)


def _swiglu_accumulate(x_ref, wg, wu, wd, acc_ref, parts=2):
    rows = x_ref.shape[0] // parts
    sl = [slice(k * rows, (k + 1) * rows) for k in range(parts)]
    x = [x_ref[s, :] for s in sl]
    g = [_dot(xk, wg) for xk in x]
    u = [_dot(xk, wu) for xk in x]
    a = [(_silu(gk) * uk).astype(BF16) for gk, uk in zip(g, u)]
    d = [_dot(ak, wd) for ak in a]
    for s, dk in zip(sl, d):
        acc_ref[s, :] += dk


def _norm_mm_kernel(x_ref, g_ref, w_ref, o_ref, xn_ref):
    @pl.when(pl.program_id(1) == 0)
    def _():
        x = x_ref[...].astype(F32)
        xn_ref[...] = _rms(x, g_ref[...], x.shape[-1]).astype(BF16)

    o_ref[...] = _dot(xn_ref[...], w_ref[...]).astype(o_ref.dtype)


def _norm_mm_wide_kernel(x_ref, g_ref, w_ref, o_ref):
    rows = x_ref.shape[0] // 2
    sl = [slice(k * rows, (k + 1) * rows) for k in range(2)]
    xn = [_rms(x_ref[s, :].astype(F32), g_ref[...], x_ref.shape[-1]).astype(BF16) for s in sl]
    for s, xk in zip(sl, xn):
        o_ref[s, :] = _dot(xk, w_ref[...]).astype(o_ref.dtype)


def norm_matmul(x, g, w, *, tm, tn, out_dtype):
    m, k = x.shape
    n = w.shape[1]
    if tn == n:
        return pl.pallas_call(
            _norm_mm_wide_kernel,
            out_shape=jax.ShapeDtypeStruct((m, n), out_dtype),
            grid=(m // tm,),
            in_specs=[pl.BlockSpec((tm, k), lambda i: (i, 0)),
                      pl.BlockSpec((1, k), lambda i: (0, 0)),
                      pl.BlockSpec((k, n), lambda i: (0, 0))],
            out_specs=pl.BlockSpec((tm, n), lambda i: (i, 0)),
            compiler_params=_cparams(("parallel",)),
            name="norm_matmul_wide",
        )(x, g.reshape(1, k).astype(F32), w)
    return pl.pallas_call(
        _norm_mm_kernel,
        out_shape=jax.ShapeDtypeStruct((m, n), out_dtype),
        grid=(m // tm, n // tn),
        in_specs=[pl.BlockSpec((tm, k), lambda i, j: (i, 0)),
                  pl.BlockSpec((1, k), lambda i, j: (0, 0)),
                  pl.BlockSpec((k, tn), lambda i, j: (0, j))],
        out_specs=pl.BlockSpec((tm, tn), lambda i, j: (i, j)),
        scratch_shapes=[pltpu.VMEM((tm, k), BF16)],
        compiler_params=_cparams(("parallel", "arbitrary")),
        name="norm_matmul",
    )(x, g.reshape(1, k).astype(F32), w)


def _ssm_kernel(u_ref, bbr_ref, bbi_ref, ar_ref, ai_ref, ccr_ref, cci_ref, d_ref, wglu_ref,
                o_ref, hr_ref, hi_ref, cr_ref, ci_ref, *, tc, nb):
    @pl.when(pl.program_id(0) == 0)
    def _():
        cr_ref[...] = jnp.zeros_like(cr_ref)
        ci_ref[...] = jnp.zeros_like(ci_ref)

    half = (tc * nb) // 2
    sl = [slice(k * half, (k + 1) * half) for k in range(2)]
    u = [u_ref[s, :] for s in sl]
    bu_r = [_dot(x, bbr_ref[...]) for x in u]
    bu_i = [_dot(x, bbi_ref[...]) for x in u]
    for s, r, i in zip(sl, bu_r, bu_i):
        hr_ref[s, :] = r
        hi_ref[s, :] = i
    gp = ar_ref.shape[-1]
    ar = jnp.broadcast_to(ar_ref[...], (nb, gp))
    ai = jnp.broadcast_to(ai_ref[...], (nb, gp))

    def step(t, carry):
        hr, hi = carry
        rows = pl.ds(pl.multiple_of(t * nb, nb), nb)
        nr = ar * hr - ai * hi + hr_ref[rows, :]
        ni = ar * hi + ai * hr + hi_ref[rows, :]
        hr_ref[rows, :] = nr
        hi_ref[rows, :] = ni
        return nr, ni

    hr, hi = lax.fori_loop(0, tc, step, (cr_ref[...], ci_ref[...]))
    cr_ref[...] = hr
    ci_ref[...] = hi
    y_r = [_dot(hr_ref[s, :].astype(BF16), ccr_ref[...]) for s in sl]
    y_i = [_dot(hi_ref[s, :].astype(BF16), cci_ref[...]) for s in sl]
    y = [jax.nn.gelu(a + b + d_ref[...] * x.astype(F32)) for a, b, x in zip(y_r, y_i, u)]
    z = [_dot(v.astype(BF16), wglu_ref[...]) for v in y]
    for s, v, w in zip(sl, y, z):
        o_ref[s, :] = (v * _sigmoid(w)).astype(o_ref.dtype)


def ssm_mixer(u_tm, a_re, a_im, b_re, b_im, c_re, c_im, d, log_dt, w_glu, *, nb, tc):
    rows = u_tm.shape[0]
    G, P, C = SSM_GROUPS, SSM_STATE, SSM_CH
    dt = jnp.exp(log_dt.astype(F32))[:, None]
    lr, li = a_re.astype(F32), a_im.astype(F32)
    mag = jnp.exp(lr * dt)
    ab_r, ab_i = mag * jnp.cos(li * dt), mag * jnp.sin(li * dt)
    den = lr * lr + li * li
    nr = ab_r - 1.0
    f_r = (nr * lr + ab_i * li) / den
    f_i = (ab_i * lr - nr * li) / den
    br, bi = b_re.astype(F32), b_im.astype(F32)
    bb_r = f_r[..., None] * br - f_i[..., None] * bi
    bb_i = f_r[..., None] * bi + f_i[..., None] * br
    eye = jnp.eye(G, dtype=F32)
    bbr = jnp.einsum('gpc,gh->gchp', bb_r, eye).reshape(G * C, G * P).astype(BF16)
    bbi = jnp.einsum('gpc,gh->gchp', bb_i, eye).reshape(G * C, G * P).astype(BF16)
    ccr = jnp.einsum('gcp,gh->gphc', c_re.astype(F32), eye).reshape(G * P, G * C).astype(BF16)
    cci = jnp.einsum('gcp,gh->gphc', -c_im.astype(F32), eye).reshape(G * P, G * C).astype(BF16)
    gp = G * P
    full = lambda shape: pl.BlockSpec(shape, lambda i: (0,) * len(shape))
    return pl.pallas_call(
        functools.partial(_ssm_kernel, tc=tc, nb=nb),
        out_shape=jax.ShapeDtypeStruct((rows, SSM_WIDTH), BF16),
        grid=(rows // (tc * nb),),
        in_specs=[pl.BlockSpec((tc * nb, SSM_WIDTH), lambda i: (i, 0)),
                  full((G * C, gp)), full((G * C, gp)), full((1, gp)), full((1, gp)),
                  full((gp, G * C)), full((gp, G * C)), full((1, SSM_WIDTH)),
                  full((SSM_WIDTH, SSM_WIDTH))],
        out_specs=pl.BlockSpec((tc * nb, SSM_WIDTH), lambda i: (i, 0)),
        scratch_shapes=[pltpu.VMEM((tc * nb, gp), F32), pltpu.VMEM((tc * nb, gp), F32),
                        pltpu.VMEM((nb, gp), F32), pltpu.VMEM((nb, gp), F32)],
        compiler_params=_cparams(("arbitrary",)),
        name="ssm_mixer",
    )(u_tm, bbr, bbi, ab_r.reshape(1, gp), ab_i.reshape(1, gp), ccr, cci,
      d.reshape(1, SSM_WIDTH).astype(F32), w_glu.astype(BF16))


def _mla_prep_kernel(cq_ref, kr_ref, ckv_ref, gq_ref, gkv_ref, wqa_ref, wqb_ref, wk_ref, wv_ref,
                     c1_ref, c0_ref, s0_ref, q_ref, k_ref, v_ref):
    qn = _rms(cq_ref[0].astype(F32), gq_ref[...], MLA_Q_RANK).astype(BF16)
    qa = _dot(qn, wqa_ref[...])
    qb = _dot(qn, wqb_ref[...])
    kn = _rms(ckv_ref[0].astype(F32), gkv_ref[...], MLA_KV_RANK).astype(BF16)
    ka = _dot(kn, wk_ref[...])
    va = _dot(kn, wv_ref[...])
    kr = kr_ref[0].astype(F32)
    c1, c0, s0 = c1_ref[...], c0_ref[...], s0_ref[...]
    krope = kr[:, :LANES] * c0 + kr[:, LANES:] * s0
    low_half = lax.broadcasted_iota(jnp.int32, (kr.shape[0], LANES), 1) < MLA_V
    for h in range(MLA_HEADS):
        sl = slice(h * LANES, (h + 1) * LANES)
        q_ref[0, h] = (qa[:, sl] * c1 + qb[:, sl] * s0).astype(BF16)
        k_ref[0, h] = (ka[:, sl] + krope).astype(BF16)
        ones = jnp.where(low_half if h % 2 == 0 else jnp.logical_not(low_half), 1.0, 0.0)
        v_ref[0, h] = jnp.concatenate([va[:, sl], ones], axis=1).astype(BF16)


def _mla_flash_kernel(q_ref, k_ref, v_ref, o_ref, m_ref, acc_ref, *, tq):
    qi = pl.program_id(1)
    m_ref[...] = jnp.full_like(m_ref, NEG_INF)
    acc_ref[...] = jnp.zeros_like(acc_ref)
    lane = lax.broadcasted_iota(jnp.int32, (tq, LANES), 1)
    rep = tq // LANES

    def tile(kt, masked):
        ks = pl.ds(pl.multiple_of(kt * tq, tq), tq)
        if masked:
            mask = (lax.broadcasted_iota(jnp.int32, (tq, tq), 1)
                    <= lax.broadcasted_iota(jnp.int32, (tq, tq), 0))
        hs = range(MLA_HEADS)
        s = [_dot_t(q_ref[0, h], k_ref[0, h, ks, :]) for h in hs]
        if masked:
            s = [jnp.where(mask, x, NEG_INF) for x in s]
        m_prev = [m_ref[h] for h in hs]
        m_new = [jnp.maximum(mp, jnp.max(x, axis=-1, keepdims=True)) for mp, x in zip(m_prev, s)]
        alpha = [jnp.exp2(mp - mn) for mp, mn in zip(m_prev, m_new)]
        p = [jnp.exp2(x - jnp.tile(mn, (1, rep))).astype(BF16) for x, mn in zip(s, m_new)]
        pv = [_dot(p[h], v_ref[0, h, ks, :]) for h in hs]
        for h in hs:
            m_ref[h] = m_new[h]
        for pr in range(MLA_HEADS // 2):
            a = jnp.tile(jnp.where(lane < MLA_V, alpha[2 * pr], alpha[2 * pr + 1]), (1, 2))
            acc_ref[pr] = acc_ref[pr] * a + pv[2 * pr] + pv[2 * pr + 1]

    def body(kt, c):
        tile(kt, False)
        return c

    lax.fori_loop(0, qi, body, 0)
    tile(qi, True)
    for pr in range(MLA_HEADS // 2):
        o_ref[0, :, pr * LANES:(pr + 1) * LANES] = (acc_ref[pr, :, :LANES] / acc_ref[pr, :, LANES:]).astype(o_ref.dtype)


def _rope_tables(seq):
    pos = jnp.arange(seq, dtype=F32)
    inv = 1.0 / (ROPE_THETA ** (jnp.arange(0, MLA_ROPE, 2, dtype=F32) / MLA_ROPE))
    ang = pos[:, None] * inv[None, :]
    cos, sin = jnp.cos(ang), jnp.sin(ang)
    cos2 = jnp.concatenate([cos, cos], axis=-1)
    sin2 = jnp.concatenate([sin, sin], axis=-1)
    z64 = jnp.zeros((seq, MLA_NOPE), F32)
    z32 = jnp.zeros((seq, LANES - MLA_NOPE - MLA_ROPE), F32)
    c1 = jnp.concatenate([jnp.ones((seq, MLA_NOPE), F32), cos2, z32], axis=-1)
    c0 = jnp.concatenate([z64, cos2, z32], axis=-1)
    s0 = jnp.concatenate([z64, sin2, z32], axis=-1)
    return c1, c0, s0


def _rot_half_cols(w):
    half = MLA_ROPE // 2
    return jnp.concatenate([-w[..., half:], w[..., :half]], axis=-1)


def mla_mixer(proj3, q_norm, w_uq, kv_norm, w_ukv, tabs, *, tm, tq):
    bsz, seq, _ = proj3.shape
    H = MLA_HEADS
    scale = (MLA_NOPE + MLA_ROPE) ** -0.5 * LOG2E
    wq =(w_uq.astype(F32) * scale).reshape(MLA_Q_RANK, H, MLA_NOPE + MLA_ROPE)
    zq = jnp.zeros((MLA_Q_RANK, H, LANES - MLA_NOPE - MLA_ROPE), F32)
    z64 = jnp.zeros((MLA_Q_RANK, H, MLA_NOPE), F32)
    wqa = jnp.concatenate([wq, zq], axis=-1).reshape(MLA_Q_RANK, H * LANES)
    wqb = jnp.concatenate([z64, _rot_half_cols(wq[..., MLA_NOPE:]), zq], axis=-1).reshape(MLA_Q_RANK, H * LANES)
    padq = ((0, W_CQ - MLA_Q_RANK), (0, 0))
    wqa = jnp.pad(wqa, padq).astype(BF16)
    wqb = jnp.pad(wqb, padq).astype(BF16)
    gq = jnp.pad(q_norm.astype(F32), (0, W_CQ - MLA_Q_RANK)).reshape(1, W_CQ)
    wkv = w_ukv.astype(F32).reshape(MLA_KV_RANK, H, MLA_NOPE + MLA_V)
    zk = jnp.zeros((MLA_KV_RANK, H, MLA_NOPE), F32)
    wk = jnp.concatenate([wkv[..., :MLA_NOPE], zk], axis=-1).reshape(MLA_KV_RANK, H * LANES).astype(BF16)
    wv_h = wkv[..., MLA_NOPE:]
    even = (jnp.arange(H) % 2 == 0)[None, :, None]
    wv = jnp.concatenate([jnp.where(even, wv_h, 0.0), jnp.where(even, 0.0, wv_h)], axis=-1)
    wv = wv.reshape(MLA_KV_RANK, H * LANES).astype(BF16)
    c1, c0, s0 = tabs
    full2 = lambda shape: pl.BlockSpec(shape, lambda b, i: (0,) * len(shape))
    tab_spec = pl.BlockSpec((tm, LANES), lambda b, i: (i, 0))
    hd_spec = pl.BlockSpec((1, H, tm, LANES), lambda b, i: (b, 0, i, 0))
    hd_shape = jax.ShapeDtypeStruct((bsz, H, seq, LANES), BF16)
    v_spec = pl.BlockSpec((1, H, tm, 2 * LANES), lambda b, i: (b, 0, i, 0))
    v_shape = jax.ShapeDtypeStruct((bsz, H, seq, 2 * LANES), BF16)
    q, k, v = pl.pallas_call(
        _mla_prep_kernel,
        out_shape=(hd_shape, hd_shape, v_shape),
        grid=(bsz, seq // tm),
        in_specs=[pl.BlockSpec((1, tm, W_CQ), lambda b, i: (b, i, C_CQ // W_CQ)),
                  pl.BlockSpec((1, tm, W_KR), lambda b, i: (b, i, C_KR // W_KR)),
                  pl.BlockSpec((1, tm, MLA_KV_RANK), lambda b, i: (b, i, C_CKV // MLA_KV_RANK)),
                  full2((1, W_CQ)), full2((1, MLA_KV_RANK)),
                  full2((W_CQ, H * LANES)), full2((W_CQ, H * LANES)),
                  full2((MLA_KV_RANK, H * LANES)), full2((MLA_KV_RANK, H * LANES)),
                  tab_spec, tab_spec, tab_spec],
        out_specs=(hd_spec, hd_spec, v_spec),
        compiler_params=_cparams(("parallel", "parallel")),
        name="mla_prep",
    )(proj3, proj3, proj3, gq, kv_norm.astype(F32).reshape(1, MLA_KV_RANK), wqa, wqb, wk, wv, c1, c0, s0)

    return pl.pallas_call(
        functools.partial(_mla_flash_kernel, tq=tq),
        out_shape=jax.ShapeDtypeStruct((bsz, seq, H * MLA_V), BF16),
        grid=(bsz, seq // tq),
        in_specs=[pl.BlockSpec((1, H, tq, LANES), lambda b, i: (b, 0, i, 0)),
                  pl.BlockSpec((1, H, seq, LANES), lambda b, i: (b, 0, 0, 0)),
                  pl.BlockSpec((1, H, seq, 2 * LANES), lambda b, i: (b, 0, 0, 0))],
        out_specs=pl.BlockSpec((1, tq, H * MLA_V), lambda b, i: (b, i, 0)),
        scratch_shapes=[pltpu.VMEM((H, tq, LANES), F32), pltpu.VMEM((H // 2, tq, 2 * LANES), F32)],
        compiler_params=_cparams(("parallel", "arbitrary")),
        name="mla_flash",
    )(q, k, v)


def _nsa_cmp_kernel(x_ref, pea_ref, peb_ref, w1a_ref, w1b_ref, w2_ref, o_ref):
    x = x_ref[0, 0]
    w1a, w1b = w1a_ref[0], w1b_ref[0]
    bias = _dot(pea_ref[0], w1a)[0:1] + _dot(peb_ref[0], w1b)[0:1]
    a = _dot(x, w1a)
    b = _dot(x, w1b)
    n = b.shape[0]
    pre = a + pltpu.roll(b, n - 1, 0) + bias
    o_ref[0, 0] = _dot(jax.nn.gelu(pre).astype(BF16), w2_ref[0]).astype(o_ref.dtype)


def nsa_compress(kv_cr, cmp_pe, cmp_w1, cmp_w2):
    bsz, _, nch, width = kv_cr.shape
    G, dh = NSA_KV_HEADS, NSA_DIM
    half = CMP_BLOCK // 2
    eye = jnp.eye(G, dtype=F32)
    w1r = cmp_w1.astype(F32).reshape(2, CMP_BLOCK, dh, dh)
    w1a = jnp.einsum('kpde,gh->kpgdhe', w1r[:, :half], eye).reshape(2, width, G * dh).astype(BF16)
    w1b = jnp.einsum('kpde,gh->kpgdhe', w1r[:, half:], eye).reshape(2, width, G * dh).astype(BF16)
    w2 = jnp.einsum('kde,gh->kgdhe', cmp_w2.astype(F32), eye).reshape(2, G * dh, G * dh).astype(BF16)
    pe = cmp_pe.astype(F32)
    pe_g = jnp.broadcast_to(pe[:, :, None, :], (2, CMP_BLOCK, G, dh))
    pea = jnp.broadcast_to(pe_g[:, :half].reshape(2, 1, width), (2, 8, width)).astype(BF16)
    peb = jnp.broadcast_to(pe_g[:, half:].reshape(2, 1, width), (2, 8, width)).astype(BF16)
    kvspec = lambda shape: pl.BlockSpec(shape, lambda b, k: (k,) + (0,) * (len(shape) - 1))
    return pl.pallas_call(
        _nsa_cmp_kernel,
        out_shape=jax.ShapeDtypeStruct((bsz, 2, nch, G * dh), BF16),
        grid=(bsz, 2),
        in_specs=[pl.BlockSpec((1, 1, nch, width), lambda b, k: (b, k, 0, 0)),
                  kvspec((1, 8, width)), kvspec((1, 8, width)),
                  kvspec((1, width, G * dh)), kvspec((1, width, G * dh)),
                  kvspec((1, G * dh, G * dh))],
        out_specs=pl.BlockSpec((1, 1, nch, G * dh), lambda b, k: (b, k, 0, 0)),
        compiler_params=_cparams(("parallel", "parallel")),
        name="nsa_compress",
    )(kv_cr, pea, peb, w1a, w1b, w2)


def _nsa_kernel(q_ref, gate_ref, ksl_ref, vsl_ref, kwn_ref, vwn_ref, kvc_ref, bc_ref, bw_ref,
                bs_ref, ov_ref, blk1h_ref, gexp_ref, o_ref, kaug_ref, vaug_ref, m_ref, acc_ref,
                *, tq, tk, nbs, n_sel):
    qi = pl.program_id(1)
    R, G = NSA_REP, NSA_KV_HEADS
    H = R * G
    nsub = tk // tq
    nwin = WINDOW // tq

    @pl.when(qi == 0)
    def _():
        kaug_ref[:, :LANES] = ksl_ref[0]
        kaug_ref[:, LANES:] = blk1h_ref[...]
        vaug_ref[:, :LANES] = vsl_ref[0]
        vaug_ref[:, LANES:] = jnp.ones((vaug_ref.shape[0], LANES), BF16)

    lane = lax.broadcasted_iota(jnp.int32, (tq, LANES), 1)
    t_row = qi * tq + lax.broadcasted_iota(jnp.int32, (H * tq, 1), 0) % tq
    kc = kvc_ref[0, 0]
    vc = kvc_ref[0, 1]
    ov = ov_ref[...]

    def stack(fn):
        return jnp.concatenate([fn(h) for h in range(H)], axis=0)

    q_all = stack(lambda h: q_ref[0, :, h * LANES:(h + 1) * LANES])

    c = jnp.minimum(qi, nwin)
    ws = pl.ds(pl.multiple_of(jnp.maximum(qi - nwin, 0) * tq, tq), (nwin + 1) * tq)
    k_w = kwn_ref[0, ws, :]
    v_w = vwn_ref[0, ws, :]
    o_w = []
    for g in range(G):
        rows = slice(g * R * tq, (g + 1) * R * tq)
        s_w = _dot_t(q_all[rows], k_w) + jnp.concatenate([bw_ref[g * R + r, c] for r in range(R)], axis=0)
        p_w = jnp.exp2(s_w - jnp.max(s_w, axis=-1, keepdims=True))
        o_w.append(_dot(p_w.astype(BF16), v_w) / jnp.sum(p_w, axis=-1, keepdims=True))
    o_w = jnp.concatenate(o_w, axis=0)

    halves = [slice(g * R * tq, (g + 1) * R * tq) for g in range(G)]
    valid = [t_row[hs] >= (CMP_BLOCK - 1) for hs in halves]
    s = [_dot_t(q_all[hs], kc) + jnp.concatenate([bc_ref[g * R + r] for r in range(R)], axis=0)
         for g, hs in enumerate(halves)]
    p = [jnp.where(v, jnp.exp2(x - jnp.max(x, axis=-1, keepdims=True)), 0.0) for v, x in zip(valid, s)]
    pc = [x / jnp.where(v, jnp.sum(x, axis=-1, keepdims=True), 1.0) for v, x in zip(valid, p)]
    o_c = jnp.concatenate([_dot(x.astype(BF16), vc) for x in pc], axis=0)
    blk = lax.broadcasted_iota(jnp.int32, (nbs, tq), 0)
    tl = qi * tq + lax.broadcasted_iota(jnp.int32, (nbs, tq), 1)
    cur = tl // SEL_BLOCK
    forced = (blk == 0) | (blk == cur) | (blk == cur - 1)
    future = blk * SEL_BLOCK > tl
    qmask = []
    for g in range(G):
        psum = pc[g][0:tq] + pc[g][tq:2 * tq] + pc[g][2 * tq:3 * tq]
        p_hi = psum.astype(BF16)
        p_lo = (psum - p_hi.astype(F32)).astype(BF16)
        imp = (_dot_t(ov, p_hi) + _dot_t(ov, p_lo))[:nbs]
        imp = jnp.where(forced, FORCE, jnp.where(future, -FORCE, imp))
        rank = jnp.zeros((nbs, tq), F32)
        for i in range(nbs):
            ri = imp[i:i + 1, :]
            beats = (ri > imp) | ((ri == imp) & (blk > i))
            rank = rank + jnp.where(beats, 1.0, 0.0)
        sel = jnp.where(rank < n_sel, 0.0, NEG_INF)
        sel = jnp.concatenate([sel, jnp.zeros((LANES - nbs, tq), F32)], axis=0).T.astype(BF16)
        qmask += [sel] * R
    q_aug = jnp.concatenate([q_all, jnp.concatenate(qmask, axis=0)], axis=1)

    m_ref[...] = jnp.full_like(m_ref, NEG_INF)
    acc_ref[...] = jnp.zeros_like(acc_ref)

    def sel_tile(kt, bias):
        ks = pl.ds(pl.multiple_of(kt * tk, tk), tk)
        k_t = kaug_ref[ks, :]
        v_t = vaug_ref[ks, :]
        s = [_dot_t(q_aug[hs], k_t) for hs in halves]
        if bias is not None:
            s = [x + bias[hs] for x, hs in zip(s, halves)]
        m_prev = [m_ref[hs, :] for hs in halves]
        m_new = [jnp.maximum(mp, jnp.max(x, axis=-1, keepdims=True)) for mp, x in zip(m_prev, s)]
        alpha = [jnp.exp2(mp - mn) for mp, mn in zip(m_prev, m_new)]
        p = [jnp.exp2(x - jnp.tile(mn, (1, tk // LANES))).astype(BF16) for x, mn in zip(s, m_new)]
        pv = [_dot(pp, v_t) for pp in p]
        for hs, mn, a, o in zip(halves, m_new, alpha, pv):
            m_ref[hs, :] = mn
            acc_ref[hs, :] = acc_ref[hs, :] * jnp.tile(a, (1, 2)) + o

    def near_bias(kt):
        cols = []
        for sub in range(nsub):
            d = qi - (kt * nsub + sub)
            cols.append(stack(lambda h: jnp.where(d == 0, bs_ref[h, 0], jnp.where(
                d == 1, bs_ref[h, 1], jnp.where(d < 0, NEG_INF, 0.0)))))
        return jnp.concatenate(cols, axis=1)

    def far_body(kt, c):
        sel_tile(kt, None)
        return c

    kd = (qi * tq) // tk
    lax.fori_loop(0, jnp.maximum(kd - 1, 0), far_body, 0)

    @pl.when(kd >= 1)
    def _():
        sel_tile(kd - 1, near_bias(kd - 1))

    sel_tile(kd, near_bias(kd))
    o_s = acc_ref[:, :LANES] / acc_ref[:, LANES:]

    gates = _sigmoid(_dot(gate_ref[0], gexp_ref[...]))
    for r in range(R):
        res = None
        for b, o_b in enumerate((o_c, o_s, o_w)):
            o_br = jnp.where(lane < NSA_DIM, o_b[r * tq:(r + 1) * tq], o_b[(R + r) * tq:(R + r + 1) * tq])
            term = gates[:, (b * R + r) * LANES:(b * R + r + 1) * LANES] * o_br
            res = term if res is None else res + term
        o_ref[0, :, r * LANES:(r + 1) * LANES] = res.astype(o_ref.dtype)


def _t5_bucket(dist):
    n = jnp.maximum(dist, 0)
    exact = REL_BUCKETS // 2
    nf = jnp.maximum(n, exact).astype(F32)
    large = exact + jnp.floor(jnp.log(nf / exact) / math.log(REL_MAX_DIST / exact)
                              * (REL_BUCKETS - exact)).astype(jnp.int32)
    return jnp.where(n < exact, n, jnp.minimum(large, REL_BUCKETS - 1))


def _nsa_bias_tables(rel_bias, seq, tq):
    rb = rel_bias.astype(F32).T
    far = rb[:, REL_BUCKETS - 1].reshape(NSA_HEADS, 1, 1)

    def by_dist(dist, ok, shift=0.0):
        bucket = _t5_bucket(dist)[None]
        out = jnp.zeros((NSA_HEADS,) + dist.shape, F32)
        for k in range(REL_BUCKETS):
            out = jnp.where(bucket == k, rb[:, k].reshape((NSA_HEADS,) + (1,) * dist.ndim), out)
        return jnp.where(ok[None], (out - shift) * LOG2E, NEG_INF)

    i = jnp.arange(tq)[:, None]
    t = jnp.arange(seq)[:, None]
    dist_c = t - (jnp.arange(LANES)[None, :] * CMP_STRIDE + CMP_BLOCK - 1)
    bc = by_dist(dist_c, dist_c >= 0)
    nwin = WINDOW // tq
    jw = jnp.arange((nwin + 1) * tq)[None, :]
    bw = jnp.stack([by_dist(tq * c + i - jw, (tq * c + i - jw >= 0) & (tq * c + i - jw < WINDOW))
                    for c in range(nwin + 1)], axis=1)
    js = jnp.arange(tq)[None, :]
    bs = jnp.stack([by_dist(tq * c + i - js, tq * c + i - js >= 0, far) for c in range(2)], axis=1)
    ci = jnp.arange(LANES)[:, None]
    sj = jnp.arange(LANES)[None, :]
    nbs = seq // SEL_BLOCK
    ov = ((ci * CMP_STRIDE <= sj * SEL_BLOCK + SEL_BLOCK - 1)
          & (ci * CMP_STRIDE + CMP_BLOCK - 1 >= sj * SEL_BLOCK)
          & (ci < seq // CMP_STRIDE - 1) & (sj < nbs))
    blk1h = (jnp.arange(seq)[:, None] // SEL_BLOCK == sj).astype(BF16)
    col = jnp.arange(3 * NSA_REP * LANES)
    slab, lane_g = col // LANES, (col % LANES) // NSA_DIM
    gate_col = (slab // NSA_REP) * NSA_HEADS + lane_g * NSA_REP + slab % NSA_REP
    gexp = (ci == gate_col[None, :]).astype(BF16)
    return bc, bw, bs, ov.T.astype(BF16), blk1h, gexp


def nsa_mixer(proj3, kvc, tables, *, tq, tk):
    bsz, seq, _ = proj3.shape
    assert tq % LANES == 0 and tq >= REL_MAX_DIST and tk % tq == 0 and seq % tk == 0
    assert seq // CMP_STRIDE == LANES and WINDOW % tq == 0
    nwin = WINDOW // tq
    bc, bw, bs, ov, blk1h, gexp = tables
    nbs = seq // SEL_BLOCK
    H = NSA_HEADS
    slab = lambda j: pl.BlockSpec((1, seq, LANES), lambda b, i: (b, 0, C_NKV // LANES + j))
    const = lambda shape: pl.BlockSpec(shape, lambda b, i: (0,) * len(shape))
    return pl.pallas_call(
        functools.partial(_nsa_kernel, tq=tq, tk=tk, nbs=nbs, n_sel=min(SEL_TOPN, nbs)),
        out_shape=jax.ShapeDtypeStruct((bsz, seq, H * NSA_DIM), BF16),
        grid=(bsz, seq // tq),
        in_specs=[pl.BlockSpec((1, tq, H * LANES), lambda b, i: (b, i, C_NQ // (H * LANES))),
                  pl.BlockSpec((1, tq, LANES), lambda b, i: (b, i, C_GATE // LANES)),
                  slab(2), slab(3), slab(4), slab(5),
                  pl.BlockSpec((1, 2, LANES, LANES), lambda b, i: (b, 0, 0, 0)),
                  pl.BlockSpec((H, tq, LANES), lambda b, i: (0, i, 0)),
                  const((H, nwin + 1, tq, (nwin + 1) * tq)), const((H, 2, tq, tq)),
                  const((LANES, LANES)), const((seq, LANES)), const(gexp.shape)],
        out_specs=pl.BlockSpec((1, tq, H * NSA_DIM), lambda b, i: (b, i, 0)),
        scratch_shapes=[pltpu.VMEM((seq, 2 * LANES), BF16), pltpu.VMEM((seq, 2 * LANES), BF16),
                        pltpu.VMEM((H * tq, LANES), F32), pltpu.VMEM((H * tq, 2 * LANES), F32)],
        compiler_params=_cparams(("parallel", "arbitrary")),
        name="nsa_attention",
    )(proj3, proj3, proj3, proj3, proj3, proj3, kvc, bc, bw, bs, ov, blk1h, gexp)


def _mix_xattn_kernel(ys_ref, ym_ref, yn_ref, h_ref, g1_ref, g2_ref, g3_ref, w1_ref, w2_ref, w3_ref,
                      kv_ref, g_ref, wq_ref, wo_ref, o_ref, *, dh):
    rows = h_ref.shape[1] // 2
    sl = [slice(k * rows, (k + 1) * rows) for k in range(2)]
    hw = XATTN_HEADS * dh
    heads = range(XATTN_HEADS)

    def normed(y_ref, gy_ref):
        return [_rms(y_ref[0, s, :].astype(F32), gy_ref[...], y_ref.shape[-1]).astype(BF16) for s in sl]

    ns, nm, nn = normed(ys_ref, g1_ref), normed(ym_ref, g2_ref), normed(yn_ref, g3_ref)
    h = [h_ref[0, s, :] + _dot(a, w1_ref[...]) + _dot(b, w2_ref[...]) + _dot(c, w3_ref[...])
         for s, a, b, c in zip(sl, ns, nm, nn)]
    xn = [_rms(x, g_ref[...], x.shape[-1]).astype(BF16) for x in h]
    q = [_dot(x, wq_ref[...]).astype(BF16) for x in xn]
    s = [[_dot_t(qk[:, hd * dh:(hd + 1) * dh], kv_ref[0, :, hd * dh:(hd + 1) * dh]) for hd in heads] for qk in q]
    p = [[jnp.exp2(x - jnp.max(x, axis=-1, keepdims=True)) for x in sk] for sk in s]
    p = [[(x / jnp.sum(x, axis=-1, keepdims=True)).astype(BF16) for x in pk] for pk in p]
    o = [jnp.concatenate([_dot(pk[hd], kv_ref[0, :, hw + hd * dh:hw + (hd + 1) * dh]).astype(BF16)
                          for hd in heads], axis=-1) for pk in p]
    for sk, hk, ok in zip(sl, h, o):
        o_ref[0, sk, :] = hk + _dot(ok, wo_ref[...])


def mix_out_cross_attention(ys3, h3, gains, weights, kv3, g_x, wq, wo, *, tm):
    bsz, seq, d = h3.shape
    m = kv3.shape[1]
    dh = d // XATTN_HEADS
    wq_s = (wq.astype(F32) * (dh ** -0.5 * LOG2E)).astype(BF16)
    gains = [g.reshape(1, -1).astype(F32) for g in gains]
    const = lambda shape: pl.BlockSpec(shape, lambda b, i: (0,) * len(shape))
    row = lambda w: pl.BlockSpec((1, tm, w), lambda b, i: (b, i, 0))
    return pl.pallas_call(
        functools.partial(_mix_xattn_kernel, dh=dh),
        out_shape=jax.ShapeDtypeStruct((bsz, seq, d), F32),
        grid=(bsz, seq // tm),
        in_specs=[row(y.shape[2]) for y in ys3] + [row(d)]
                 + [const(g.shape) for g in gains] + [const(w.shape) for w in weights]
                 + [pl.BlockSpec((1, m, 2 * d), lambda b, i: (b, 0, 0)),
                    const((1, d)), const((d, d)), const((d, d))],
        out_specs=row(d),
        compiler_params=_cparams(("parallel", "parallel")),
        name="mix_out_cross_attention",
    )(*ys3, h3, *gains, *weights, kv3, g_x.reshape(1, d).astype(F32), wq_s, wo.astype(BF16))


def _ffn_kernel(h_ref, g_ref, wg_ref, wu_ref, wd_ref, o_ref, xn_ref, acc_ref):
    j = pl.program_id(1)

    @pl.when(j == 0)
    def _():
        h = h_ref[...]
        xn_ref[...] = _rms(h, g_ref[...], h.shape[-1]).astype(BF16)
        acc_ref[...] = h

    _swiglu_accumulate(xn_ref, wg_ref[...], wu_ref[...], wd_ref[...], acc_ref)

    @pl.when(j == pl.num_programs(1) - 1)
    def _():
        o_ref[...] = acc_ref[...]


def dense_ffn(h, g, wg, wu, wd, *, tm, tf):
    m, d = h.shape
    ff = wg.shape[1]
    return pl.pallas_call(
        _ffn_kernel,
        out_shape=jax.ShapeDtypeStruct((m, d), F32),
        grid=(m // tm, ff // tf),
        in_specs=[pl.BlockSpec((tm, d), lambda i, j: (i, 0)),
                  pl.BlockSpec((1, d), lambda i, j: (0, 0)),
                  pl.BlockSpec((d, tf), lambda i, j: (0, j)),
                  pl.BlockSpec((d, tf), lambda i, j: (0, j)),
                  pl.BlockSpec((tf, d), lambda i, j: (j, 0))],
        out_specs=pl.BlockSpec((tm, d), lambda i, j: (i, 0)),
        scratch_shapes=[pltpu.VMEM((tm, d), BF16), pltpu.VMEM((tm, d), F32)],
        compiler_params=_cparams(("parallel", "arbitrary")),
        name="dense_ffn",
    )(h, g.reshape(1, d).astype(F32), wg.astype(BF16), wu.astype(BF16), wd.astype(BF16))


def _router_kernel(h_ref, g_ref, wr_hi_ref, wr_lo_ref, xn_ref, info_ref, cnt_ref, carry_ref, *, tm):
    i = pl.program_id(0)

    @pl.when(i == 0)
    def _():
        carry_ref[...] = jnp.zeros_like(carry_ref)

    h = h_ref[...]
    xn = _rms(h, g_ref[...], h.shape[-1])
    xn_ref[...] = xn
    x_hi = xn.astype(BF16)
    x_lo = (xn - x_hi.astype(F32)).astype(BF16)
    logits = _dot(x_hi, wr_hi_ref[...]) + _dot(x_lo, wr_hi_ref[...]) + _dot(x_hi, wr_lo_ref[...])
    lane = lax.broadcasted_iota(jnp.int32, (tm, LANES), 1)
    lanef = lane.astype(F32)
    logits = jnp.where(lane < N_EXPERTS, logits, NEG_INF)
    m1 = jnp.max(logits, axis=-1, keepdims=True)
    i1 = jnp.min(jnp.where(logits == m1, lanef, float(LANES)), axis=-1, keepdims=True)
    rest = jnp.where(lanef == i1, NEG_INF, logits)
    m2 = jnp.max(rest, axis=-1, keepdims=True)
    i2 = jnp.min(jnp.where(rest == m2, lanef, float(LANES)), axis=-1, keepdims=True)
    e2 = jnp.exp(m2 - m1)
    w1 = 1.0 / (1.0 + e2)
    w2 = e2 / (1.0 + e2)
    oh1 = lanef == i1
    oh2 = lanef == i2
    oh = jnp.where(oh1 | oh2, 1.0, 0.0)
    rr = lax.broadcasted_iota(jnp.int32, (tm, tm), 0)
    cc = lax.broadcasted_iota(jnp.int32, (tm, tm), 1)
    tri = jnp.where(cc < rr, 1.0, 0.0).astype(BF16)
    before = _dot(tri, oh.astype(BF16)) + carry_ref[0:1, :]
    r1 = jnp.sum(jnp.where(oh1, before, 0.0), axis=-1, keepdims=True)
    r2 = jnp.sum(jnp.where(oh2, before, 0.0), axis=-1, keepdims=True)
    carry_ref[...] = carry_ref[...] + jnp.sum(oh, axis=0, keepdims=True)
    info = jnp.where(lane == 0, i1, jnp.where(lane == 1, i2, jnp.where(lane == 2, w1, jnp.where(
        lane == 3, w2, jnp.where(lane == 4, r1, jnp.where(lane == 5, r2, 0.0))))))
    info_ref[...] = info
    cnt_ref[...] = carry_ref[...]


def moe_router(h, g, router, *, tm):
    m, d = h.shape
    wr = jnp.pad(router.astype(F32), ((0, 0), (0, LANES - N_EXPERTS)))
    wr_hi = wr.astype(BF16)
    wr_lo = (wr - wr_hi.astype(F32)).astype(BF16)
    return pl.pallas_call(
        functools.partial(_router_kernel, tm=tm),
        out_shape=(jax.ShapeDtypeStruct((m, d), F32), jax.ShapeDtypeStruct((m, LANES), F32),
                   jax.ShapeDtypeStruct((8, LANES), F32)),
        grid=(m // tm,),
        in_specs=[pl.BlockSpec((tm, d), lambda i: (i, 0)),
                  pl.BlockSpec((1, d), lambda i: (0, 0)),
                  pl.BlockSpec((d, LANES), lambda i: (0, 0)),
                  pl.BlockSpec((d, LANES), lambda i: (0, 0))],
        out_specs=(pl.BlockSpec((tm, d), lambda i: (i, 0)),
                   pl.BlockSpec((tm, LANES), lambda i: (i, 0)),
                   pl.BlockSpec((8, LANES), lambda i: (0, 0))),
        scratch_shapes=[pltpu.VMEM((8, LANES), F32)],
        compiler_params=_cparams(("arbitrary",)),
        name="moe_router",
    )(h, g.reshape(1, d).astype(F32), wr_hi, wr_lo)


def _row_copy(src_hbm, row, dst, slot, sem):
    return pltpu.make_async_copy(src_hbm.at[pl.ds(row, 1), :], dst.at[pl.ds(slot, 1), :], sem)


def _rows_wait(src_hbm, dst, sem):
    pltpu.make_async_copy(src_hbm.at[pl.ds(0, dst.shape[0]), :], dst, sem).wait()


def _moe_ffn_kernel(src_ref, texp_ref, nact_ref, x_hbm, wg_ref, wu_ref, wd_ref, o_ref,
                    xbuf, xbf, acc_ref, sem, *, tm, nj):
    i = pl.program_id(0)
    j = pl.program_id(1)
    nact = nact_ref[0]
    active = i < nact
    nbuf = xbuf.shape[0]
    ahead = nbuf - 1
    cur = i % nbuf
    rows_per_step = tm // nj

    for t0 in range(ahead):
        @pl.when((i == 0) & (j == 0) & (t0 < nact))
        def _():
            def issue(s, c):
                _row_copy(x_hbm, src_ref[t0 * tm + s], xbuf.at[t0], s, sem.at[t0]).start()
                return c

            lax.fori_loop(0, tm, issue, 0, unroll=8)

    @pl.when(active & (j == 0))
    def _():
        _rows_wait(x_hbm, xbuf.at[cur], sem.at[cur])
        xbf[...] = xbuf[cur].astype(BF16)
        acc_ref[...] = jnp.zeros_like(acc_ref)

    def compute(prefetch):
        if prefetch:
            nxt = (i + ahead) % nbuf
            base = (i + ahead) * tm + j * rows_per_step
            for k in range(rows_per_step):
                _row_copy(x_hbm, src_ref[base + k], xbuf.at[nxt], j * rows_per_step + k, sem.at[nxt]).start()
        _swiglu_accumulate(xbf, wg_ref[0], wu_ref[0], wd_ref[0], acc_ref)

    @pl.when(i + ahead < nact)
    def _():
        compute(True)

    @pl.when(active & (i + ahead >= nact))
    def _():
        compute(False)

    @pl.when(j == nj - 1)
    def _():
        o_ref[...] = jnp.where(active, acc_ref[...], 0.0)


def moe_expert_ffn(xn, src, tile_expert, n_active, wg, wu, wd, *, tm, tf):
    n_slots = src.shape[0]
    d = xn.shape[1]
    ne, _, ff = wg.shape
    nj = ff // tf
    assert nj * tf == ff and tm % nj == 0
    wg, wu, wd = wg.astype(BF16), wu.astype(BF16), wd.astype(BF16)

    def wmap_col(i, j, src, texp, nact):
        return (texp[i], 0, jnp.where(i < nact[0], j, nj - 1))

    def wmap_row(i, j, src, texp, nact):
        return (texp[i], jnp.where(i < nact[0], j, nj - 1), 0)

    return pl.pallas_call(
        functools.partial(_moe_ffn_kernel, tm=tm, nj=nj),
        out_shape=jax.ShapeDtypeStruct((n_slots, d), F32),
        grid_spec=pltpu.PrefetchScalarGridSpec(
            num_scalar_prefetch=3,
            grid=(n_slots // tm, nj),
            in_specs=[pl.BlockSpec(memory_space=pl.ANY),
                      pl.BlockSpec((1, d, tf), wmap_col),
                      pl.BlockSpec((1, d, tf), wmap_col),
                      pl.BlockSpec((1, tf, d), wmap_row)],
            out_specs=pl.BlockSpec((tm, d), lambda i, j, *_: (i, 0)),
            scratch_shapes=[pltpu.VMEM((3, tm, d), F32), pltpu.VMEM((tm, d), BF16),
                            pltpu.VMEM((tm, d), F32), pltpu.SemaphoreType.DMA((3,))]),
        compiler_params=_cparams(("arbitrary", "arbitrary")),
        name="moe_expert_ffn",
    )(src, tile_expert, n_active, xn, wg, wu, wd)


def _moe_combine_kernel(pos_ref, h_ref, info_ref, ys_hbm, g_ref, o_ref, buf, sem, *, tm, final_norm):
    i = pl.program_id(0)
    n = pl.num_programs(0)
    cur = i % 2

    def start_gather(tile, b):
        for s in range(tm):
            for k in range(2):
                _row_copy(ys_hbm, pos_ref[2 * (tile * tm + s) + k], buf.at[b, k], s, sem.at[b]).start()

    @pl.when(i == 0)
    def _():
        start_gather(0, 0)

    @pl.when(i + 1 < n)
    def _():
        start_gather(i + 1, 1 - cur)

    for k in range(2):
        _rows_wait(ys_hbm, buf.at[cur, k], sem.at[cur])
    info = info_ref[...]
    y = h_ref[...] + info[:, 2:3] * buf[cur, 0] + info[:, 3:4] * buf[cur, 1]
    if final_norm:
        y = _rms(y, g_ref[...], y.shape[-1])
    o_ref[...] = y


def moe_combine(h, info, ys, pos_flat, g_final, *, tm, final_norm):
    m, d = h.shape
    return pl.pallas_call(
        functools.partial(_moe_combine_kernel, tm=tm, final_norm=final_norm),
        out_shape=jax.ShapeDtypeStruct((m, d), F32),
        grid_spec=pltpu.PrefetchScalarGridSpec(
            num_scalar_prefetch=1,
            grid=(m // tm,),
            in_specs=[pl.BlockSpec((tm, d), lambda i, *_: (i, 0)),
                      pl.BlockSpec((tm, LANES), lambda i, *_: (i, 0)),
                      pl.BlockSpec(memory_space=pl.ANY),
                      pl.BlockSpec((1, d), lambda i, *_: (0, 0))],
            out_specs=pl.BlockSpec((tm, d), lambda i, *_: (i, 0)),
            scratch_shapes=[pltpu.VMEM((2, 2, tm, d), F32), pltpu.SemaphoreType.DMA((2,))]),
        compiler_params=_cparams(("arbitrary",)),
        name="moe_combine",
    )(pos_flat, h, info, ys, g_final.reshape(1, d).astype(F32))


def moe_layer(h, g, router, wg, wu, wd, g_final, *, final_norm, tm_r=512, tm_g=512, tf=1792, tm_c=512):
    m, d = h.shape
    xn, info, cnt = moe_router(h, g, router, tm=tm_r)
    e_idx = info[:, 0:2].astype(jnp.int32)
    rank = info[:, 4:6].astype(jnp.int32)
    counts = cnt[0, :N_EXPERTS].astype(jnp.int32)
    tiles_per = (counts + tm_g - 1) // tm_g
    tile_end = jnp.cumsum(tiles_per)
    seg_start = (tile_end - tiles_per) * tm_g
    pos = rank
    for e in range(N_EXPERTS):
        pos = pos + jnp.where(e_idx == e, seg_start[e], 0)
    n_tiles = (2 * m) // tm_g + N_EXPERTS
    n_slots = n_tiles * tm_g
    tok = jnp.broadcast_to(jnp.arange(m, dtype=jnp.int32)[:, None], (m, 2))
    src = jnp.zeros((n_slots,), jnp.int32).at[pos.reshape(-1)].set(tok.reshape(-1))
    n_active = tile_end[-1:].astype(jnp.int32)
    tile_ids = jnp.minimum(jnp.arange(n_tiles, dtype=jnp.int32), n_active[0] - 1)
    tile_expert = jnp.sum(tile_ids[:, None] >= tile_end[None, :], axis=1).astype(jnp.int32)
    ys = moe_expert_ffn(xn, src, tile_expert, n_active,
                        wg, wu, wd, tm=tm_g, tf=tf)
    return moe_combine(h, info, ys, pos.reshape(-1).astype(jnp.int32), g_final, tm=tm_c, final_norm=final_norm)


def _final_norm_kernel(h_ref, g_ref, o_ref):
    h = h_ref[...]
    o_ref[...] = _rms(h, g_ref[...], h.shape[-1])


def final_rmsnorm(h, g, *, tm):
    m, d = h.shape
    return pl.pallas_call(
        _final_norm_kernel,
        out_shape=jax.ShapeDtypeStruct((m, d), F32),
        grid=(m // tm,),
        in_specs=[pl.BlockSpec((tm, d), lambda i: (i, 0)), pl.BlockSpec((1, d), lambda i: (0, 0))],
        out_specs=pl.BlockSpec((tm, d), lambda i: (i, 0)),
        compiler_params=_cparams(("parallel",)),
        name="final_rmsnorm",
    )(h, g.reshape(1, d).astype(F32))


def _pack_w_in(w):
    d = w.shape[0]
    w = w.astype(F32)
    o = 0
    u = w[:, o:o + SSM_WIDTH]; o += SSM_WIDTH
    cq = w[:, o:o + MLA_Q_RANK]; o += MLA_Q_RANK
    ckv = w[:, o:o + MLA_KV_RANK]; o += MLA_KV_RANK
    kr = w[:, o:o + MLA_ROPE]; o += MLA_ROPE
    nq = w[:, o:o + NSA_HEADS * NSA_DIM]; o += NSA_HEADS * NSA_DIM
    nkv = w[:, o:o + 6 * NSA_KV_HEADS * NSA_DIM]; o += 6 * NSA_KV_HEADS * NSA_DIM
    gate = w[:, o:o + 3 * NSA_HEADS]
    z = lambda n: jnp.zeros((d, n), F32)
    kr_a = jnp.concatenate([z(MLA_NOPE), kr, z(LANES - MLA_NOPE - MLA_ROPE)], axis=1)
    kr_b = jnp.concatenate([z(MLA_NOPE), _rot_half_cols(kr), z(LANES - MLA_NOPE - MLA_ROPE)], axis=1)
    nq_h = (nq * (NSA_DIM ** -0.5 * LOG2E)).reshape(d, NSA_KV_HEADS, NSA_REP, NSA_DIM)
    zq = jnp.zeros((d, NSA_REP, NSA_DIM), F32)
    nq_p = jnp.concatenate([
        jnp.concatenate([nq_h[:, 0], zq], axis=-1).reshape(d, NSA_REP * LANES),
        jnp.concatenate([zq, nq_h[:, 1]], axis=-1).reshape(d, NSA_REP * LANES)], axis=1)
    packed = jnp.concatenate([u, cq, z(W_CQ - MLA_Q_RANK), kr_a, kr_b, nq_p, nkv, ckv,
                              gate, z(LANES - 3 * NSA_HEADS)], axis=1)
    assert packed.shape[1] == IN_COLS_PACKED
    return packed.astype(BF16)


def _rg_order(a):
    rest = a.shape[1:]
    return a.reshape((NSA_KV_HEADS, NSA_REP, NSA_DIM) + rest).swapaxes(0, 1).reshape((-1,) + rest)


def kernel(x, mem, w_in, w_out, mix_norm, out_norm, ssm_a_re, ssm_a_im, ssm_b_re, ssm_b_im, ssm_c_re, ssm_c_im, ssm_d, ssm_log_dt, ssm_w_glu, mla_q_norm, mla_w_uq, mla_kv_norm, mla_w_ukv, nsa_cmp_pe, nsa_cmp_w1, nsa_cmp_w2, rel_bias, xattn_norm, mem_norm, xattn_wq, xattn_wkv, xattn_wo, ffn_norm, dense_w_gate, dense_w_up, dense_w_down, moe_router, moe_w_gate, moe_w_up, moe_w_down, final_norm):
    bsz, seq, d = x.shape
    depth = w_in.shape[0]
    T = bsz * seq
    nmem = mem.shape[1]
    tq_nsa, tk_nsa = 2 * LANES, 4 * LANES
    rope_tabs = _rope_tables(seq)
    nsa_tabs = _nsa_bias_tables(rel_bias, seq, tq_nsa)
    o1, o2 = SSM_WIDTH, SSM_WIDTH + MLA_HEADS * MLA_V
    mem2 = mem.reshape(bsz * nmem, d)
    h = x.reshape(T, d)
    for l in range(depth):
        proj = norm_matmul(h, mix_norm[l], _pack_w_in(w_in[l]), tm=1024, tn=IN_COLS_PACKED, out_dtype=BF16)
        proj3 = proj.reshape(bsz, seq, IN_COLS_PACKED)
        u_tm = proj3[:, :, C_U:C_U + SSM_WIDTH].transpose(1, 0, 2).reshape(seq * bsz, SSM_WIDTH)
        y_ssm = ssm_mixer(u_tm, ssm_a_re[l], ssm_a_im[l], ssm_b_re[l], ssm_b_im[l], ssm_c_re[l], ssm_c_im[l],
                          ssm_d[l], ssm_log_dt[l], ssm_w_glu[l], nb=bsz, tc=64)
        y_ssm = y_ssm.reshape(seq, bsz, SSM_WIDTH).transpose(1, 0, 2)
        y_mla = mla_mixer(proj3, mla_q_norm[l], mla_w_uq[l], mla_kv_norm[l], mla_w_ukv[l], rope_tabs,
                          tm=512, tq=512)
        nch = seq // CMP_STRIDE
        kv_cr = jnp.stack([proj3[:, :, C_NKV:C_NKV + LANES].reshape(bsz, nch, CMP_STRIDE * LANES),
                           proj3[:, :, C_NKV + LANES:C_NKV + 2 * LANES].reshape(bsz, nch, CMP_STRIDE * LANES)],
                          axis=1)
        kvc = nsa_compress(kv_cr, nsa_cmp_pe[l], nsa_cmp_w1[l], nsa_cmp_w2[l])
        y_nsa = nsa_mixer(proj3, kvc, nsa_tabs, tq=tq_nsa, tk=tk_nsa)
        g_out = out_norm[l]
        wo_l = w_out[l]
        kv_mem = norm_matmul(mem2, mem_norm[l], xattn_wkv[l].astype(BF16), tm=256, tn=512, out_dtype=BF16)
        h = mix_out_cross_attention(
            [y_ssm, y_mla, y_nsa], h.reshape(bsz, seq, d),
            [g_out[:o1], g_out[o1:o2], _rg_order(g_out[o2:])],
            [wo_l[:o1].astype(BF16), wo_l[o1:o2].astype(BF16), _rg_order(wo_l[o2:]).astype(BF16)],
            kv_mem.reshape(bsz, nmem, 2 * d), xattn_norm[l], xattn_wq[l], xattn_wo[l], tm=512).reshape(T, d)
        last = l == depth - 1
        if l % 2 == 0:
            h = dense_ffn(h, ffn_norm[l], dense_w_gate[l // 2], dense_w_up[l // 2], dense_w_down[l // 2],
                          tm=1024, tf=1408)
            if last:
                h = final_rmsnorm(h, final_norm, tm=512)
        else:
            h = moe_layer(h, ffn_norm[l], moe_router[l // 2], moe_w_gate[l // 2], moe_w_up[l // 2],
                          moe_w_down[l // 2], final_norm, final_norm=last)
    return h.reshape(bsz, seq, d)
```

```python
import functools
import math

import jax
import jax.numpy as jnp
from jax import lax
from jax.experimental import pallas as pl
from jax.experimental.pallas import tpu as pltpu

F32 = jnp.float32
BF16 = jnp.bfloat16

SSM_WIDTH = 256
SSM_CH = 16
SSM_GROUPS = 16
SSM_STATE = 64
MLA_HEADS = 6
MLA_NOPE = 64
MLA_ROPE = 32
MLA_V = 64
MLA_Q_RANK = 192
MLA_KV_RANK = 128
NSA_HEADS = 6
NSA_KV_HEADS = 2
NSA_REP = 3
NSA_DIM = 64
CMP_BLOCK = 32
CMP_STRIDE = 16
SEL_BLOCK = 64
SEL_TOPN = 8
WINDOW = 256
REL_BUCKETS = 32
REL_MAX_DIST = 128
XATTN_HEADS = 4
N_EXPERTS = 8
ROPE_THETA = 10000.0
EPS = 1e-6
NEG_INF = -1e30
FORCE = 1e9
LOG2E = math.log2(math.e)

LANES = 128
VMEM_LIMIT = 56 * 1024 * 1024

W_CQ = 2 * LANES
W_KR = 2 * LANES
W_NQ = NSA_HEADS * LANES
W_NKV = 6 * LANES
_WIDTHS = (SSM_WIDTH, W_CQ, W_KR, W_NQ, W_NKV, MLA_KV_RANK, LANES)
C_U, C_CQ, C_KR, C_NQ, C_NKV, C_CKV, C_GATE = (sum(_WIDTHS[:k]) for k in range(len(_WIDTHS)))
IN_COLS_PACKED = sum(_WIDTHS)
assert all(c % w == 0 for c, w in zip((C_U, C_CQ, C_KR, C_NQ, C_NKV, C_CKV, C_GATE), _WIDTHS))


def _cparams(sem):
    return pltpu.CompilerParams(dimension_semantics=sem, vmem_limit_bytes=VMEM_LIMIT)


def _dot(a, b):
    return jnp.dot(a, b, preferred_element_type=F32)


def _dot_t(a, b):
    return lax.dot_general(a, b, (((1,), (1,)), ((), ())), preferred_element_type=F32)


def _rms(x, g, n):
    ms = jnp.sum(x * x, axis=-1, keepdims=True) * (1.0 / n)
    return x * lax.rsqrt(ms + EPS) * g


def _sigmoid(x):
    return 1.0 / (1.0 + jnp.exp(-x))


def _silu(x):
    return x * _sigmoid(x)


def _swiglu_accumulate(x_ref, wg, wu, wd, acc_ref, parts=2):
    rows = x_ref.shape[0] // parts
    sl = [slice(k * rows, (k + 1) * rows) for k in range(parts)]
    x = [x_ref[s, :] for s in sl]
    g = [_dot(xk, wg) for xk in x]
    u = [_dot(xk, wu) for xk in x]
    a = [(_silu(gk) * uk).astype(BF16) for gk, uk in zip(g, u)]
    d = [_dot(ak, wd) for ak in a]
    for s, dk in zip(sl, d):
        acc_ref[s, :] += dk


def _norm_mm_kernel(x_ref, g_ref, w_ref, o_ref, xn_ref):
    @pl.when(pl.program_id(1) == 0)
    def _():
        x = x_ref[...].astype(F32)
        xn_ref[...] = _rms(x, g_ref[...], x.shape[-1]).astype(BF16)

    o_ref[...] = _dot(xn_ref[...], w_ref[...]).astype(o_ref.dtype)


def _norm_mm_wide_kernel(x_ref, g_ref, w_ref, o_ref):
    rows = x_ref.shape[0] // 2
    sl = [slice(k * rows, (k + 1) * rows) for k in range(2)]
    xn = [_rms(x_ref[s, :].astype(F32), g_ref[...], x_ref.shape[-1]).astype(BF16) for s in sl]
    for s, xk in zip(sl, xn):
        o_ref[s, :] = _dot(xk, w_ref[...]).astype(o_ref.dtype)


def norm_matmul(x, g, w, *, tm, tn, out_dtype):
    m, k = x.shape
    n = w.shape[1]
    if tn == n:
        return pl.pallas_call(
            _norm_mm_wide_kernel,
            out_shape=jax.ShapeDtypeStruct((m, n), out_dtype),
            grid=(m // tm,),
            in_specs=[pl.BlockSpec((tm, k), lambda i: (i, 0)),
                      pl.BlockSpec((1, k), lambda i: (0, 0)),
                      pl.BlockSpec((k, n), lambda i: (0, 0))],
            out_specs=pl.BlockSpec((tm, n), lambda i: (i, 0)),
            compiler_params=_cparams(("parallel",)),
            name="norm_matmul_wide",
        )(x, g.reshape(1, k).astype(F32), w)
    return pl.pallas_call(
        _norm_mm_kernel,
        out_shape=jax.ShapeDtypeStruct((m, n), out_dtype),
        grid=(m // tm, n // tn),
        in_specs=[pl.BlockSpec((tm, k), lambda i, j: (i, 0)),
                  pl.BlockSpec((1, k), lambda i, j: (0, 0)),
                  pl.BlockSpec((k, tn), lambda i, j: (0, j))],
        out_specs=pl.BlockSpec((tm, tn), lambda i, j: (i, j)),
        scratch_shapes=[pltpu.VMEM((tm, k), BF16)],
        compiler_params=_cparams(("parallel", "arbitrary")),
        name="norm_matmul",
    )(x, g.reshape(1, k).astype(F32), w)


def _ssm_kernel(u_ref, bbr_ref, bbi_ref, ar_ref, ai_ref, ccr_ref, cci_ref, d_ref, wglu_ref,
                o_ref, hr_ref, hi_ref, cr_ref, ci_ref, *, tc, nb):
    @pl.when(pl.program_id(0) == 0)
    def _():
        cr_ref[...] = jnp.zeros_like(cr_ref)
        ci_ref[...] = jnp.zeros_like(ci_ref)

    half = (tc * nb) // 2
    sl = [slice(k * half, (k + 1) * half) for k in range(2)]
    u = [u_ref[s, :] for s in sl]
    bu_r = [_dot(x, bbr_ref[...]) for x in u]
    bu_i = [_dot(x, bbi_ref[...]) for x in u]
    for s, r, i in zip(sl, bu_r, bu_i):
        hr_ref[s, :] = r
        hi_ref[s, :] = i
    gp = ar_ref.shape[-1]
    ar = jnp.broadcast_to(ar_ref[...], (nb, gp))
    ai = jnp.broadcast_to(ai_ref[...], (nb, gp))

    def step(t, carry):
        hr, hi = carry
        rows = pl.ds(pl.multiple_of(t * nb, nb), nb)
        nr = ar * hr - ai * hi + hr_ref[rows, :]
        ni = ar * hi + ai * hr + hi_ref[rows, :]
        hr_ref[rows, :] = nr
        hi_ref[rows, :] = ni
        return nr, ni

    hr, hi = lax.fori_loop(0, tc, step, (cr_ref[...], ci_ref[...]))
    cr_ref[...] = hr
    ci_ref[...] = hi
    y_r = [_dot(hr_ref[s, :].astype(BF16), ccr_ref[...]) for s in sl]
    y_i = [_dot(hi_ref[s, :].astype(BF16), cci_ref[...]) for s in sl]
    y = [jax.nn.gelu(a + b + d_ref[...] * x.astype(F32)) for a, b, x in zip(y_r, y_i, u)]
    z = [_dot(v.astype(BF16), wglu_ref[...]) for v in y]
    for s, v, w in zip(sl, y, z):
        o_ref[s, :] = (v * _sigmoid(w)).astype(o_ref.dtype)


def ssm_mixer(u_tm, a_re, a_im, b_re, b_im, c_re, c_im, d, log_dt, w_glu, *, nb, tc):
    rows = u_tm.shape[0]
    G, P, C = SSM_GROUPS, SSM_STATE, SSM_CH
    dt = jnp.exp(log_dt.astype(F32))[:, None]
    lr, li = a_re.astype(F32), a_im.astype(F32)
    mag = jnp.exp(lr * dt)
    ab_r, ab_i = mag * jnp.cos(li * dt), mag * jnp.sin(li * dt)
    den = lr * lr + li * li
    nr = ab_r - 1.0
    f_r = (nr * lr + ab_i * li) / den
    f_i = (ab_i * lr - nr * li) / den
    br, bi = b_re.astype(F32), b_im.astype(F32)
    bb_r = f_r[..., None] * br - f_i[..., None] * bi
    bb_i = f_r[..., None] * bi + f_i[..., None] * br
    eye = jnp.eye(G, dtype=F32)
    bbr = jnp.einsum('gpc,gh->gchp', bb_r, eye).reshape(G * C, G * P).astype(BF16)
    bbi = jnp.einsum('gpc,gh->gchp', bb_i, eye).reshape(G * C, G * P).astype(BF16)
    ccr = jnp.einsum('gcp,gh->gphc', c_re.astype(F32), eye).reshape(G * P, G * C).astype(BF16)
    cci = jnp.einsum('gcp,gh->gphc', -c_im.astype(F32), eye).reshape(G * P, G * C).astype(BF16)
    gp = G * P
    full = lambda shape: pl.BlockSpec(shape, lambda i: (0,) * len(shape))
    return pl.pallas_call(
        functools.partial(_ssm_kernel, tc=tc, nb=nb),
        out_shape=jax.ShapeDtypeStruct((rows, SSM_WIDTH), BF16),
        grid=(rows // (tc * nb),),
        in_specs=[pl.BlockSpec((tc * nb, SSM_WIDTH), lambda i: (i, 0)),
                  full((G * C, gp)), full((G * C, gp)), full((1, gp)), full((1, gp)),
                  full((gp, G * C)), full((gp, G * C)), full((1, SSM_WIDTH)),
                  full((SSM_WIDTH, SSM_WIDTH))],
        out_specs=pl.BlockSpec((tc * nb, SSM_WIDTH), lambda i: (i, 0)),
        scratch_shapes=[pltpu.VMEM((tc * nb, gp), F32), pltpu.VMEM((tc * nb, gp), F32),
                        pltpu.VMEM((nb, gp), F32), pltpu.VMEM((nb, gp), F32)],
        compiler_params=_cparams(("arbitrary",)),
        name="ssm_mixer",
    )(u_tm, bbr, bbi, ab_r.reshape(1, gp), ab_i.reshape(1, gp), ccr, cci,
      d.reshape(1, SSM_WIDTH).astype(F32), w_glu.astype(BF16))


def _mla_prep_kernel(cq_ref, kr_ref, ckv_ref, gq_ref, gkv_ref, wqa_ref, wqb_ref, wk_ref, wv_ref,
                     c1_ref, c0_ref, s0_ref, q_ref, k_ref, v_ref):
    qn = _rms(cq_ref[0].astype(F32), gq_ref[...], MLA_Q_RANK).astype(BF16)
    qa = _dot(qn, wqa_ref[...])
    qb = _dot(qn, wqb_ref[...])
    kn = _rms(ckv_ref[0].astype(F32), gkv_ref[...], MLA_KV_RANK).astype(BF16)
    ka = _dot(kn, wk_ref[...])
    va = _dot(kn, wv_ref[...])
    kr = kr_ref[0].astype(F32)
    c1, c0, s0 = c1_ref[...], c0_ref[...], s0_ref[...]
    krope = kr[:, :LANES] * c0 + kr[:, LANES:] * s0
    for h in range(MLA_HEADS):
        sl = slice(h * LANES, (h + 1) * LANES)
        q_ref[0, h] = (qa[:, sl] * c1 + qb[:, sl] * s0).astype(BF16)
        k_ref[0, h] = (ka[:, sl] + krope).astype(BF16)
        v_ref[0, h] = va[:, sl].astype(BF16)


def _mla_flash_kernel(q_ref, k_ref, v_ref, o_ref, vaug_ref, m_ref, acc_ref, *, tq):
    qi = pl.program_id(1)

    @pl.when(qi == 0)
    def _():
        low_half = lax.broadcasted_iota(jnp.int32, (vaug_ref.shape[1], LANES), 1) < MLA_V
        for h in range(MLA_HEADS):
            own = low_half if h % 2 == 0 else jnp.logical_not(low_half)
            vaug_ref[h, :, :LANES] = v_ref[0, h]
            vaug_ref[h, :, LANES:] = jnp.where(own, 1.0, 0.0).astype(BF16)

    m_ref[...] = jnp.full_like(m_ref, NEG_INF)
    acc_ref[...] = jnp.zeros_like(acc_ref)
    lane = lax.broadcasted_iota(jnp.int32, (tq, LANES), 1)
    rep = tq // LANES

    def tile(kt, masked):
        ks = pl.ds(pl.multiple_of(kt * tq, tq), tq)
        if masked:
            mask = (lax.broadcasted_iota(jnp.int32, (tq, tq), 1)
                    <= lax.broadcasted_iota(jnp.int32, (tq, tq), 0))
        hs = range(MLA_HEADS)
        s = [_dot_t(q_ref[0, h], k_ref[0, h, ks, :]) for h in hs]
        if masked:
            s = [jnp.where(mask, x, NEG_INF) for x in s]
        m_prev = [m_ref[h] for h in hs]
        m_new = [jnp.maximum(mp, jnp.max(x, axis=-1, keepdims=True)) for mp, x in zip(m_prev, s)]
        alpha = [jnp.exp2(mp - mn) for mp, mn in zip(m_prev, m_new)]
        p = [jnp.exp2(x - jnp.tile(mn, (1, rep))).astype(BF16) for x, mn in zip(s, m_new)]
        pv = [_dot(p[h], vaug_ref[h, ks, :]) for h in hs]
        for h in hs:
            m_ref[h] = m_new[h]
        for pr in range(MLA_HEADS // 2):
            a = jnp.tile(jnp.where(lane < MLA_V, alpha[2 * pr], alpha[2 * pr + 1]), (1, 2))
            acc_ref[pr] = acc_ref[pr] * a + pv[2 * pr] + pv[2 * pr + 1]

    def body(kt, c):
        tile(kt, False)
        return c

    lax.fori_loop(0, qi, body, 0)
    tile(qi, True)
    for pr in range(MLA_HEADS // 2):
        o_ref[0, :, pr * LANES:(pr + 1) * LANES] = (acc_ref[pr, :, :LANES] / acc_ref[pr, :, LANES:]).astype(o_ref.dtype)


def _rope_tables(seq):
    pos = jnp.arange(seq, dtype=F32)
    inv = 1.0 / (ROPE_THETA ** (jnp.arange(0, MLA_ROPE, 2, dtype=F32) / MLA_ROPE))
    ang = pos[:, None] * inv[None, :]
    cos, sin = jnp.cos(ang), jnp.sin(ang)
    cos2 = jnp.concatenate([cos, cos], axis=-1)
    sin2 = jnp.concatenate([sin, sin], axis=-1)
    z64 = jnp.zeros((seq, MLA_NOPE), F32)
    z32 = jnp.zeros((seq, LANES - MLA_NOPE - MLA_ROPE), F32)
    c1 = jnp.concatenate([jnp.ones((seq, MLA_NOPE), F32), cos2, z32], axis=-1)
    c0 = jnp.concatenate([z64, cos2, z32], axis=-1)
    s0 = jnp.concatenate([z64, sin2, z32], axis=-1)
    return c1, c0, s0


def _rot_half_cols(w):
    half = MLA_ROPE // 2
    return jnp.concatenate([-w[..., half:], w[..., :half]], axis=-1)


def mla_mixer(proj3, q_norm, w_uq, kv_norm, w_ukv, tabs, *, tm, tq):
    bsz, seq, _ = proj3.shape
    H = MLA_HEADS
    scale = (MLA_NOPE + MLA_ROPE) ** -0.5 * LOG2E
    wq =(w_uq.astype(F32) * scale).reshape(MLA_Q_RANK, H, MLA_NOPE + MLA_ROPE)
    zq = jnp.zeros((MLA_Q_RANK, H, LANES - MLA_NOPE - MLA_ROPE), F32)
    z64 = jnp.zeros((MLA_Q_RANK, H, MLA_NOPE), F32)
    wqa = jnp.concatenate([wq, zq], axis=-1).reshape(MLA_Q_RANK, H * LANES)
    wqb = jnp.concatenate([z64, _rot_half_cols(wq[..., MLA_NOPE:]), zq], axis=-1).reshape(MLA_Q_RANK, H * LANES)
    padq = ((0, W_CQ - MLA_Q_RANK), (0, 0))
    wqa = jnp.pad(wqa, padq).astype(BF16)
    wqb = jnp.pad(wqb, padq).astype(BF16)
    gq = jnp.pad(q_norm.astype(F32), (0, W_CQ - MLA_Q_RANK)).reshape(1, W_CQ)
    wkv = w_ukv.astype(F32).reshape(MLA_KV_RANK, H, MLA_NOPE + MLA_V)
    zk = jnp.zeros((MLA_KV_RANK, H, MLA_NOPE), F32)
    wk = jnp.concatenate([wkv[..., :MLA_NOPE], zk], axis=-1).reshape(MLA_KV_RANK, H * LANES).astype(BF16)
    wv_h = wkv[..., MLA_NOPE:]
    even = (jnp.arange(H) % 2 == 0)[None, :, None]
    wv = jnp.concatenate([jnp.where(even, wv_h, 0.0), jnp.where(even, 0.0, wv_h)], axis=-1)
    wv = wv.reshape(MLA_KV_RANK, H * LANES).astype(BF16)
    c1, c0, s0 = tabs
    full2 = lambda shape: pl.BlockSpec(shape, lambda b, i: (0,) * len(shape))
    tab_spec = pl.BlockSpec((tm, LANES), lambda b, i: (i, 0))
    hd_spec = pl.BlockSpec((1, H, tm, LANES), lambda b, i: (b, 0, i, 0))
    hd_shape = jax.ShapeDtypeStruct((bsz, H, seq, LANES), BF16)
    q, k, v = pl.pallas_call(
        _mla_prep_kernel,
        out_shape=(hd_shape, hd_shape, hd_shape),
        grid=(bsz, seq // tm),
        in_specs=[pl.BlockSpec((1, tm, W_CQ), lambda b, i: (b, i, C_CQ // W_CQ)),
                  pl.BlockSpec((1, tm, W_KR), lambda b, i: (b, i, C_KR // W_KR)),
                  pl.BlockSpec((1, tm, MLA_KV_RANK), lambda b, i: (b, i, C_CKV // MLA_KV_RANK)),
                  full2((1, W_CQ)), full2((1, MLA_KV_RANK)),
                  full2((W_CQ, H * LANES)), full2((W_CQ, H * LANES)),
                  full2((MLA_KV_RANK, H * LANES)), full2((MLA_KV_RANK, H * LANES)),
                  tab_spec, tab_spec, tab_spec],
        out_specs=(hd_spec, hd_spec, hd_spec),
        compiler_params=_cparams(("parallel", "parallel")),
        name="mla_prep",
    )(proj3, proj3, proj3, gq, kv_norm.astype(F32).reshape(1, MLA_KV_RANK), wqa, wqb, wk, wv, c1, c0, s0)

    return pl.pallas_call(
        functools.partial(_mla_flash_kernel, tq=tq),
        out_shape=jax.ShapeDtypeStruct((bsz, seq, H * MLA_V), BF16),
        grid=(bsz, seq // tq),
        in_specs=[pl.BlockSpec((1, H, tq, LANES), lambda b, i: (b, 0, i, 0)),
                  pl.BlockSpec((1, H, seq, LANES), lambda b, i: (b, 0, 0, 0)),
                  pl.BlockSpec((1, H, seq, LANES), lambda b, i: (b, 0, 0, 0))],
        out_specs=pl.BlockSpec((1, tq, H * MLA_V), lambda b, i: (b, i, 0)),
        scratch_shapes=[pltpu.VMEM((H, seq, 2 * LANES), BF16), pltpu.VMEM((H, tq, LANES), F32),
                        pltpu.VMEM((H // 2, tq, 2 * LANES), F32)],
        compiler_params=_cparams(("parallel", "arbitrary")),
        name="mla_flash",
    )(q, k, v)


def _nsa_cmp_kernel(x_ref, pea_ref, peb_ref, w1a_ref, w1b_ref, w2_ref, o_ref):
    x = x_ref[0, 0]
    w1a, w1b = w1a_ref[0], w1b_ref[0]
    bias = _dot(pea_ref[0], w1a)[0:1] + _dot(peb_ref[0], w1b)[0:1]
    a = _dot(x, w1a)
    b = _dot(x, w1b)
    n = b.shape[0]
    pre = a + pltpu.roll(b, n - 1, 0) + bias
    o_ref[0, 0] = _dot(jax.nn.gelu(pre).astype(BF16), w2_ref[0]).astype(o_ref.dtype)


def nsa_compress(kv_cr, cmp_pe, cmp_w1, cmp_w2):
    bsz, _, nch, width = kv_cr.shape
    G, dh = NSA_KV_HEADS, NSA_DIM
    half = CMP_BLOCK // 2
    eye = jnp.eye(G, dtype=F32)
    w1r = cmp_w1.astype(F32).reshape(2, CMP_BLOCK, dh, dh)
    w1a = jnp.einsum('kpde,gh->kpgdhe', w1r[:, :half], eye).reshape(2, width, G * dh).astype(BF16)
    w1b = jnp.einsum('kpde,gh->kpgdhe', w1r[:, half:], eye).reshape(2, width, G * dh).astype(BF16)
    w2 = jnp.einsum('kde,gh->kgdhe', cmp_w2.astype(F32), eye).reshape(2, G * dh, G * dh).astype(BF16)
    pe = cmp_pe.astype(F32)
    pe_g = jnp.broadcast_to(pe[:, :, None, :], (2, CMP_BLOCK, G, dh))
    pea = jnp.broadcast_to(pe_g[:, :half].reshape(2, 1, width), (2, 8, width)).astype(BF16)
    peb = jnp.broadcast_to(pe_g[:, half:].reshape(2, 1, width), (2, 8, width)).astype(BF16)
    kvspec = lambda shape: pl.BlockSpec(shape, lambda b, k: (k,) + (0,) * (len(shape) - 1))
    return pl.pallas_call(
        _nsa_cmp_kernel,
        out_shape=jax.ShapeDtypeStruct((bsz, 2, nch, G * dh), BF16),
        grid=(bsz, 2),
        in_specs=[pl.BlockSpec((1, 1, nch, width), lambda b, k: (b, k, 0, 0)),
                  kvspec((1, 8, width)), kvspec((1, 8, width)),
                  kvspec((1, width, G * dh)), kvspec((1, width, G * dh)),
                  kvspec((1, G * dh, G * dh))],
        out_specs=pl.BlockSpec((1, 1, nch, G * dh), lambda b, k: (b, k, 0, 0)),
        compiler_params=_cparams(("parallel", "parallel")),
        name="nsa_compress",
    )(kv_cr, pea, peb, w1a, w1b, w2)


def _nsa_kernel(q_ref, gate_ref, ksl_ref, vsl_ref, kwn_ref, vwn_ref, kvc_ref, bc_ref, bw_ref,
                bs_ref, ov_ref, blk1h_ref, gexp_ref, o_ref, kaug_ref, vaug_ref, m_ref, acc_ref,
                *, tq, tk, nbs, n_sel):
    qi = pl.program_id(1)
    R, G = NSA_REP, NSA_KV_HEADS
    H = R * G
    nsub = tk // tq
    nwin = WINDOW // tq

    @pl.when(qi == 0)
    def _():
        kaug_ref[:, :LANES] = ksl_ref[0]
        kaug_ref[:, LANES:] = blk1h_ref[...]
        vaug_ref[:, :LANES] = vsl_ref[0]
        vaug_ref[:, LANES:] = jnp.ones((vaug_ref.shape[0], LANES), BF16)

    lane = lax.broadcasted_iota(jnp.int32, (tq, LANES), 1)
    t_row = qi * tq + lax.broadcasted_iota(jnp.int32, (H * tq, 1), 0) % tq
    kc = kvc_ref[0, 0]
    vc = kvc_ref[0, 1]
    ov = ov_ref[...]

    def stack(fn):
        return jnp.concatenate([fn(h) for h in range(H)], axis=0)

    q_all = stack(lambda h: q_ref[0, :, h * LANES:(h + 1) * LANES])

    c = jnp.minimum(qi, nwin)
    ws = pl.ds(pl.multiple_of(jnp.maximum(qi - nwin, 0) * tq, tq), (nwin + 1) * tq)
    k_w = kwn_ref[0, ws, :]
    v_w = vwn_ref[0, ws, :]
    o_w = []
    for g in range(G):
        rows = slice(g * R * tq, (g + 1) * R * tq)
        s_w = _dot_t(q_all[rows], k_w) + jnp.concatenate([bw_ref[g * R + r, c] for r in range(R)], axis=0)
        p_w = jnp.exp2(s_w - jnp.max(s_w, axis=-1, keepdims=True))
        o_w.append(_dot(p_w.astype(BF16), v_w) / jnp.sum(p_w, axis=-1, keepdims=True))
    o_w = jnp.concatenate(o_w, axis=0)

    halves = [slice(g * R * tq, (g + 1) * R * tq) for g in range(G)]
    valid = [t_row[hs] >= (CMP_BLOCK - 1) for hs in halves]
    s = [_dot_t(q_all[hs], kc) + jnp.concatenate([bc_ref[g * R + r] for r in range(R)], axis=0)
         for g, hs in enumerate(halves)]
    p = [jnp.where(v, jnp.exp2(x - jnp.max(x, axis=-1, keepdims=True)), 0.0) for v, x in zip(valid, s)]
    pc = [x / jnp.where(v, jnp.sum(x, axis=-1, keepdims=True), 1.0) for v, x in zip(valid, p)]
    o_c = jnp.concatenate([_dot(x.astype(BF16), vc) for x in pc], axis=0)
    blk = lax.broadcasted_iota(jnp.int32, (nbs, tq), 0)
    tl = qi * tq + lax.broadcasted_iota(jnp.int32, (nbs, tq), 1)
    cur = tl // SEL_BLOCK
    forced = (blk == 0) | (blk == cur) | (blk == cur - 1)
    future = blk * SEL_BLOCK > tl
    qmask = []
    for g in range(G):
        psum = pc[g][0:tq] + pc[g][tq:2 * tq] + pc[g][2 * tq:3 * tq]
        p_hi = psum.astype(BF16)
        p_lo = (psum - p_hi.astype(F32)).astype(BF16)
        imp = (_dot_t(ov, p_hi) + _dot_t(ov, p_lo))[:nbs]
        imp = jnp.where(forced, FORCE, jnp.where(future, -FORCE, imp))
        rank = jnp.zeros((nbs, tq), F32)
        for i in range(nbs):
            ri = imp[i:i + 1, :]
            beats = (ri > imp) | ((ri == imp) & (blk > i))
            rank = rank + jnp.where(beats, 1.0, 0.0)
        sel = jnp.where(rank < n_sel, 0.0, NEG_INF)
        sel = jnp.concatenate([sel, jnp.zeros((LANES - nbs, tq), F32)], axis=0).T.astype(BF16)
        qmask += [sel] * R
    q_aug = jnp.concatenate([q_all, jnp.concatenate(qmask, axis=0)], axis=1)

    m_ref[...] = jnp.full_like(m_ref, NEG_INF)
    acc_ref[...] = jnp.zeros_like(acc_ref)

    def sel_tile(kt, bias):
        ks = pl.ds(pl.multiple_of(kt * tk, tk), tk)
        k_t = kaug_ref[ks, :]
        v_t = vaug_ref[ks, :]
        s = [_dot_t(q_aug[hs], k_t) for hs in halves]
        if bias is not None:
            s = [x + bias[hs] for x, hs in zip(s, halves)]
        m_prev = [m_ref[hs, :] for hs in halves]
        m_new = [jnp.maximum(mp, jnp.max(x, axis=-1, keepdims=True)) for mp, x in zip(m_prev, s)]
        alpha = [jnp.exp2(mp - mn) for mp, mn in zip(m_prev, m_new)]
        p = [jnp.exp2(x - jnp.tile(mn, (1, tk // LANES))).astype(BF16) for x, mn in zip(s, m_new)]
        pv = [_dot(pp, v_t) for pp in p]
        for hs, mn, a, o in zip(halves, m_new, alpha, pv):
            m_ref[hs, :] = mn
            acc_ref[hs, :] = acc_ref[hs, :] * jnp.tile(a, (1, 2)) + o

    def near_bias(kt):
        cols = []
        for sub in range(nsub):
            d = qi - (kt * nsub + sub)
            cols.append(stack(lambda h: jnp.where(d == 0, bs_ref[h, 0], jnp.where(
                d == 1, bs_ref[h, 1], jnp.where(d < 0, NEG_INF, 0.0)))))
        return jnp.concatenate(cols, axis=1)

    def far_body(kt, c):
        sel_tile(kt, None)
        return c

    kd = (qi * tq) // tk
    lax.fori_loop(0, jnp.maximum(kd - 1, 0), far_body, 0)

    @pl.when(kd >= 1)
    def _():
        sel_tile(kd - 1, near_bias(kd - 1))

    sel_tile(kd, near_bias(kd))
    o_s = acc_ref[:, :LANES] / acc_ref[:, LANES:]

    gates = _sigmoid(_dot(gate_ref[0], gexp_ref[...]))
    for r in range(R):
        res = None
        for b, o_b in enumerate((o_c, o_s, o_w)):
            o_br = jnp.where(lane < NSA_DIM, o_b[r * tq:(r + 1) * tq], o_b[(R + r) * tq:(R + r + 1) * tq])
            term = gates[:, (b * R + r) * LANES:(b * R + r + 1) * LANES] * o_br
            res = term if res is None else res + term
        o_ref[0, :, r * LANES:(r + 1) * LANES] = res.astype(o_ref.dtype)


def _t5_bucket(dist):
    n = jnp.maximum(dist, 0)
    exact = REL_BUCKETS // 2
    nf = jnp.maximum(n, exact).astype(F32)
    large = exact + jnp.floor(jnp.log(nf / exact) / math.log(REL_MAX_DIST / exact)
                              * (REL_BUCKETS - exact)).astype(jnp.int32)
    return jnp.where(n < exact, n, jnp.minimum(large, REL_BUCKETS - 1))


def _nsa_bias_tables(rel_bias, seq, tq):
    rb = rel_bias.astype(F32).T
    far = rb[:, REL_BUCKETS - 1].reshape(NSA_HEADS, 1, 1)

    def by_dist(dist, ok, shift=0.0):
        bucket = _t5_bucket(dist)[None]
        out = jnp.zeros((NSA_HEADS,) + dist.shape, F32)
        for k in range(REL_BUCKETS):
            out = jnp.where(bucket == k, rb[:, k].reshape((NSA_HEADS,) + (1,) * dist.ndim), out)
        return jnp.where(ok[None], (out - shift) * LOG2E, NEG_INF)

    i = jnp.arange(tq)[:, None]
    t = jnp.arange(seq)[:, None]
    dist_c = t - (jnp.arange(LANES)[None, :] * CMP_STRIDE + CMP_BLOCK - 1)
    bc = by_dist(dist_c, dist_c >= 0)
    nwin = WINDOW // tq
    jw = jnp.arange((nwin + 1) * tq)[None, :]
    bw = jnp.stack([by_dist(tq * c + i - jw, (tq * c + i - jw >= 0) & (tq * c + i - jw < WINDOW))
                    for c in range(nwin + 1)], axis=1)
    js = jnp.arange(tq)[None, :]
    bs = jnp.stack([by_dist(tq * c + i - js, tq * c + i - js >= 0, far) for c in range(2)], axis=1)
    ci = jnp.arange(LANES)[:, None]
    sj = jnp.arange(LANES)[None, :]
    nbs = seq // SEL_BLOCK
    ov = ((ci * CMP_STRIDE <= sj * SEL_BLOCK + SEL_BLOCK - 1)
          & (ci * CMP_STRIDE + CMP_BLOCK - 1 >= sj * SEL_BLOCK)
          & (ci < seq // CMP_STRIDE - 1) & (sj < nbs))
    blk1h = (jnp.arange(seq)[:, None] // SEL_BLOCK == sj).astype(BF16)
    col = jnp.arange(3 * NSA_REP * LANES)
    slab, lane_g = col // LANES, (col % LANES) // NSA_DIM
    gate_col = (slab // NSA_REP) * NSA_HEADS + lane_g * NSA_REP + slab % NSA_REP
    gexp = (ci == gate_col[None, :]).astype(BF16)
    return bc, bw, bs, ov.T.astype(BF16), blk1h, gexp


def nsa_mixer(proj3, kvc, tables, *, tq, tk):
    bsz, seq, _ = proj3.shape
    assert tq % LANES == 0 and tq >= REL_MAX_DIST and tk % tq == 0 and seq % tk == 0
    assert seq // CMP_STRIDE == LANES and WINDOW % tq == 0
    nwin = WINDOW // tq
    bc, bw, bs, ov, blk1h, gexp = tables
    nbs = seq // SEL_BLOCK
    H = NSA_HEADS
    slab = lambda j: pl.BlockSpec((1, seq, LANES), lambda b, i: (b, 0, C_NKV // LANES + j))
    const = lambda shape: pl.BlockSpec(shape, lambda b, i: (0,) * len(shape))
    return pl.pallas_call(
        functools.partial(_nsa_kernel, tq=tq, tk=tk, nbs=nbs, n_sel=min(SEL_TOPN, nbs)),
        out_shape=jax.ShapeDtypeStruct((bsz, seq, H * NSA_DIM), BF16),
        grid=(bsz, seq // tq),
        in_specs=[pl.BlockSpec((1, tq, H * LANES), lambda b, i: (b, i, C_NQ // (H * LANES))),
                  pl.BlockSpec((1, tq, LANES), lambda b, i: (b, i, C_GATE // LANES)),
                  slab(2), slab(3), slab(4), slab(5),
                  pl.BlockSpec((1, 2, LANES, LANES), lambda b, i: (b, 0, 0, 0)),
                  pl.BlockSpec((H, tq, LANES), lambda b, i: (0, i, 0)),
                  const((H, nwin + 1, tq, (nwin + 1) * tq)), const((H, 2, tq, tq)),
                  const((LANES, LANES)), const((seq, LANES)), const(gexp.shape)],
        out_specs=pl.BlockSpec((1, tq, H * NSA_DIM), lambda b, i: (b, i, 0)),
        scratch_shapes=[pltpu.VMEM((seq, 2 * LANES), BF16), pltpu.VMEM((seq, 2 * LANES), BF16),
                        pltpu.VMEM((H * tq, LANES), F32), pltpu.VMEM((H * tq, 2 * LANES), F32)],
        compiler_params=_cparams(("parallel", "arbitrary")),
        name="nsa_attention",
    )(proj3, proj3, proj3, proj3, proj3, proj3, kvc, bc, bw, bs, ov, blk1h, gexp)


def _mix_xattn_kernel(ys_ref, ym_ref, yn_ref, h_ref, g1_ref, g2_ref, g3_ref, w1_ref, w2_ref, w3_ref,
                      kv_ref, g_ref, wq_ref, wo_ref, o_ref, *, dh):
    rows = h_ref.shape[1] // 2
    sl = [slice(k * rows, (k + 1) * rows) for k in range(2)]
    hw = XATTN_HEADS * dh
    heads = range(XATTN_HEADS)

    def normed(y_ref, gy_ref):
        return [_rms(y_ref[0, s, :].astype(F32), gy_ref[...], y_ref.shape[-1]).astype(BF16) for s in sl]

    ns, nm, nn = normed(ys_ref, g1_ref), normed(ym_ref, g2_ref), normed(yn_ref, g3_ref)
    h = [h_ref[0, s, :] + _dot(a, w1_ref[...]) + _dot(b, w2_ref[...]) + _dot(c, w3_ref[...])
         for s, a, b, c in zip(sl, ns, nm, nn)]
    xn = [_rms(x, g_ref[...], x.shape[-1]).astype(BF16) for x in h]
    q = [_dot(x, wq_ref[...]).astype(BF16) for x in xn]
    s = [[_dot_t(qk[:, hd * dh:(hd + 1) * dh], kv_ref[0, :, hd * dh:(hd + 1) * dh]) for hd in heads] for qk in q]
    p = [[jnp.exp2(x - jnp.max(x, axis=-1, keepdims=True)) for x in sk] for sk in s]
    p = [[(x / jnp.sum(x, axis=-1, keepdims=True)).astype(BF16) for x in pk] for pk in p]
    o = [jnp.concatenate([_dot(pk[hd], kv_ref[0, :, hw + hd * dh:hw + (hd + 1) * dh]).astype(BF16)
                          for hd in heads], axis=-1) for pk in p]
    for sk, hk, ok in zip(sl, h, o):
        o_ref[0, sk, :] = hk + _dot(ok, wo_ref[...])


def mix_out_cross_attention(ys3, h3, gains, weights, kv3, g_x, wq, wo, *, tm):
    bsz, seq, d = h3.shape
    m = kv3.shape[1]
    dh = d // XATTN_HEADS
    wq_s = (wq.astype(F32) * (dh ** -0.5 * LOG2E)).astype(BF16)
    gains = [g.reshape(1, -1).astype(F32) for g in gains]
    const = lambda shape: pl.BlockSpec(shape, lambda b, i: (0,) * len(shape))
    row = lambda w: pl.BlockSpec((1, tm, w), lambda b, i: (b, i, 0))
    return pl.pallas_call(
        functools.partial(_mix_xattn_kernel, dh=dh),
        out_shape=jax.ShapeDtypeStruct((bsz, seq, d), F32),
        grid=(bsz, seq // tm),
        in_specs=[row(y.shape[2]) for y in ys3] + [row(d)]
                 + [const(g.shape) for g in gains] + [const(w.shape) for w in weights]
                 + [pl.BlockSpec((1, m, 2 * d), lambda b, i: (b, 0, 0)),
                    const((1, d)), const((d, d)), const((d, d))],
        out_specs=row(d),
        compiler_params=_cparams(("parallel", "parallel")),
        name="mix_out_cross_attention",
    )(*ys3, h3, *gains, *weights, kv3, g_x.reshape(1, d).astype(F32), wq_s, wo.astype(BF16))


def _ffn_kernel(h_ref, g_ref, wg_ref, wu_ref, wd_ref, o_ref, xn_ref, acc_ref):
    j = pl.program_id(1)

    @pl.when(j == 0)
    def _():
        h = h_ref[...]
        xn_ref[...] = _rms(h, g_ref[...], h.shape[-1]).astype(BF16)
        acc_ref[...] = h

    _swiglu_accumulate(xn_ref, wg_ref[...], wu_ref[...], wd_ref[...], acc_ref)

    @pl.when(j == pl.num_programs(1) - 1)
    def _():
        o_ref[...] = acc_ref[...]


def dense_ffn(h, g, wg, wu, wd, *, tm, tf):
    m, d = h.shape
    ff = wg.shape[1]
    return pl.pallas_call(
        _ffn_kernel,
        out_shape=jax.ShapeDtypeStruct((m, d), F32),
        grid=(m // tm, ff // tf),
        in_specs=[pl.BlockSpec((tm, d), lambda i, j: (i, 0)),
                  pl.BlockSpec((1, d), lambda i, j: (0, 0)),
                  pl.BlockSpec((d, tf), lambda i, j: (0, j)),
                  pl.BlockSpec((d, tf), lambda i, j: (0, j)),
                  pl.BlockSpec((tf, d), lambda i, j: (j, 0))],
        out_specs=pl.BlockSpec((tm, d), lambda i, j: (i, 0)),
        scratch_shapes=[pltpu.VMEM((tm, d), BF16), pltpu.VMEM((tm, d), F32)],
        compiler_params=_cparams(("parallel", "arbitrary")),
        name="dense_ffn",
    )(h, g.reshape(1, d).astype(F32), wg.astype(BF16), wu.astype(BF16), wd.astype(BF16))


def _router_kernel(h_ref, g_ref, wr_hi_ref, wr_lo_ref, xn_ref, info_ref, cnt_ref, carry_ref, *, tm):
    i = pl.program_id(0)

    @pl.when(i == 0)
    def _():
        carry_ref[...] = jnp.zeros_like(carry_ref)

    h = h_ref[...]
    xn = _rms(h, g_ref[...], h.shape[-1])
    xn_ref[...] = xn
    x_hi = xn.astype(BF16)
    x_lo = (xn - x_hi.astype(F32)).astype(BF16)
    logits = _dot(x_hi, wr_hi_ref[...]) + _dot(x_lo, wr_hi_ref[...]) + _dot(x_hi, wr_lo_ref[...])
    lane = lax.broadcasted_iota(jnp.int32, (tm, LANES), 1)
    lanef = lane.astype(F32)
    logits = jnp.where(lane < N_EXPERTS, logits, NEG_INF)
    m1 = jnp.max(logits, axis=-1, keepdims=True)
    i1 = jnp.min(jnp.where(logits == m1, lanef, float(LANES)), axis=-1, keepdims=True)
    rest = jnp.where(lanef == i1, NEG_INF, logits)
    m2 = jnp.max(rest, axis=-1, keepdims=True)
    i2 = jnp.min(jnp.where(rest == m2, lanef, float(LANES)), axis=-1, keepdims=True)
    e2 = jnp.exp(m2 - m1)
    w1 = 1.0 / (1.0 + e2)
    w2 = e2 / (1.0 + e2)
    oh1 = lanef == i1
    oh2 = lanef == i2
    oh = jnp.where(oh1 | oh2, 1.0, 0.0)
    rr = lax.broadcasted_iota(jnp.int32, (tm, tm), 0)
    cc = lax.broadcasted_iota(jnp.int32, (tm, tm), 1)
    tri = jnp.where(cc < rr, 1.0, 0.0).astype(BF16)
    before = _dot(tri, oh.astype(BF16)) + carry_ref[0:1, :]
    r1 = jnp.sum(jnp.where(oh1, before, 0.0), axis=-1, keepdims=True)
    r2 = jnp.sum(jnp.where(oh2, before, 0.0), axis=-1, keepdims=True)
    carry_ref[...] = carry_ref[...] + jnp.sum(oh, axis=0, keepdims=True)
    info = jnp.where(lane == 0, i1, jnp.where(lane == 1, i2, jnp.where(lane == 2, w1, jnp.where(
        lane == 3, w2, jnp.where(lane == 4, r1, jnp.where(lane == 5, r2, 0.0))))))
    info_ref[...] = info
    cnt_ref[...] = carry_ref[...]


def moe_router(h, g, router, *, tm):
    m, d = h.shape
    wr = jnp.pad(router.astype(F32), ((0, 0), (0, LANES - N_EXPERTS)))
    wr_hi = wr.astype(BF16)
    wr_lo = (wr - wr_hi.astype(F32)).astype(BF16)
    return pl.pallas_call(
        functools.partial(_router_kernel, tm=tm),
        out_shape=(jax.ShapeDtypeStruct((m, d), F32), jax.ShapeDtypeStruct((m, LANES), F32),
                   jax.ShapeDtypeStruct((8, LANES), F32)),
        grid=(m // tm,),
        in_specs=[pl.BlockSpec((tm, d), lambda i: (i, 0)),
                  pl.BlockSpec((1, d), lambda i: (0, 0)),
                  pl.BlockSpec((d, LANES), lambda i: (0, 0)),
                  pl.BlockSpec((d, LANES), lambda i: (0, 0))],
        out_specs=(pl.BlockSpec((tm, d), lambda i: (i, 0)),
                   pl.BlockSpec((tm, LANES), lambda i: (i, 0)),
                   pl.BlockSpec((8, LANES), lambda i: (0, 0))),
        scratch_shapes=[pltpu.VMEM((8, LANES), F32)],
        compiler_params=_cparams(("arbitrary",)),
        name="moe_router",
    )(h, g.reshape(1, d).astype(F32), wr_hi, wr_lo)


def _row_copy(src_hbm, row, dst, slot, sem):
    return pltpu.make_async_copy(src_hbm.at[pl.ds(row, 1), :], dst.at[pl.ds(slot, 1), :], sem)


def _rows_wait(src_hbm, dst, sem):
    pltpu.make_async_copy(src_hbm.at[pl.ds(0, dst.shape[0]), :], dst, sem).wait()


def _moe_ffn_kernel(src_ref, texp_ref, nact_ref, x_hbm, wg_ref, wu_ref, wd_ref, o_ref,
                    xbuf, xbf, acc_ref, sem, *, tm, nj):
    i = pl.program_id(0)
    j = pl.program_id(1)
    nact = nact_ref[0]
    active = i < nact
    nbuf = xbuf.shape[0]
    ahead = nbuf - 1
    cur = i % nbuf
    rows_per_step = tm // nj

    for t0 in range(ahead):
        @pl.when((i == 0) & (j == 0) & (t0 < nact))
        def _():
            def issue(s, c):
                _row_copy(x_hbm, src_ref[t0 * tm + s], xbuf.at[t0], s, sem.at[t0]).start()
                return c

            lax.fori_loop(0, tm, issue, 0, unroll=8)

    @pl.when(active & (j == 0))
    def _():
        _rows_wait(x_hbm, xbuf.at[cur], sem.at[cur])
        xbf[...] = xbuf[cur].astype(BF16)
        acc_ref[...] = jnp.zeros_like(acc_ref)

    def compute(prefetch):
        if prefetch:
            nxt = (i + ahead) % nbuf
            base = (i + ahead) * tm + j * rows_per_step
            for k in range(rows_per_step):
                _row_copy(x_hbm, src_ref[base + k], xbuf.at[nxt], j * rows_per_step + k, sem.at[nxt]).start()
        _swiglu_accumulate(xbf, wg_ref[0], wu_ref[0], wd_ref[0], acc_ref)

    @pl.when(i + ahead < nact)
    def _():
        compute(True)

    @pl.when(active & (i + ahead >= nact))
    def _():
        compute(False)

    @pl.when(j == nj - 1)
    def _():
        o_ref[...] = jnp.where(active, acc_ref[...], 0.0)


def moe_expert_ffn(xn, src, tile_expert, n_active, wg, wu, wd, *, tm, tf):
    n_slots = src.shape[0]
    d = xn.shape[1]
    ne, _, ff = wg.shape
    nj = ff // tf
    assert nj * tf == ff and tm % nj == 0
    wg, wu, wd = wg.astype(BF16), wu.astype(BF16), wd.astype(BF16)

    def wmap_col(i, j, src, texp, nact):
        return (texp[i], 0, jnp.where(i < nact[0], j, nj - 1))

    def wmap_row(i, j, src, texp, nact):
        return (texp[i], jnp.where(i < nact[0], j, nj - 1), 0)

    return pl.pallas_call(
        functools.partial(_moe_ffn_kernel, tm=tm, nj=nj),
        out_shape=jax.ShapeDtypeStruct((n_slots, d), F32),
        grid_spec=pltpu.PrefetchScalarGridSpec(
            num_scalar_prefetch=3,
            grid=(n_slots // tm, nj),
            in_specs=[pl.BlockSpec(memory_space=pl.ANY),
                      pl.BlockSpec((1, d, tf), wmap_col),
                      pl.BlockSpec((1, d, tf), wmap_col),
                      pl.BlockSpec((1, tf, d), wmap_row)],
            out_specs=pl.BlockSpec((tm, d), lambda i, j, *_: (i, 0)),
            scratch_shapes=[pltpu.VMEM((3, tm, d), F32), pltpu.VMEM((tm, d), BF16),
                            pltpu.VMEM((tm, d), F32), pltpu.SemaphoreType.DMA((3,))]),
        compiler_params=_cparams(("arbitrary", "arbitrary")),
        name="moe_expert_ffn",
    )(src, tile_expert, n_active, xn, wg, wu, wd)


def _moe_combine_kernel(pos_ref, h_ref, info_ref, ys_hbm, g_ref, o_ref, buf, sem, *, tm, final_norm):
    i = pl.program_id(0)
    n = pl.num_programs(0)
    cur = i % 2

    def start_gather(tile, b):
        for s in range(tm):
            for k in range(2):
                _row_copy(ys_hbm, pos_ref[2 * (tile * tm + s) + k], buf.at[b, k], s, sem.at[b]).start()

    @pl.when(i == 0)
    def _():
        start_gather(0, 0)

    @pl.when(i + 1 < n)
    def _():
        start_gather(i + 1, 1 - cur)

    for k in range(2):
        _rows_wait(ys_hbm, buf.at[cur, k], sem.at[cur])
    info = info_ref[...]
    y = h_ref[...] + info[:, 2:3] * buf[cur, 0] + info[:, 3:4] * buf[cur, 1]
    if final_norm:
        y = _rms(y, g_ref[...], y.shape[-1])
    o_ref[...] = y


def moe_combine(h, info, ys, pos_flat, g_final, *, tm, final_norm):
    m, d = h.shape
    return pl.pallas_call(
        functools.partial(_moe_combine_kernel, tm=tm, final_norm=final_norm),
        out_shape=jax.ShapeDtypeStruct((m, d), F32),
        grid_spec=pltpu.PrefetchScalarGridSpec(
            num_scalar_prefetch=1,
            grid=(m // tm,),
            in_specs=[pl.BlockSpec((tm, d), lambda i, *_: (i, 0)),
                      pl.BlockSpec((tm, LANES), lambda i, *_: (i, 0)),
                      pl.BlockSpec(memory_space=pl.ANY),
                      pl.BlockSpec((1, d), lambda i, *_: (0, 0))],
            out_specs=pl.BlockSpec((tm, d), lambda i, *_: (i, 0)),
            scratch_shapes=[pltpu.VMEM((2, 2, tm, d), F32), pltpu.SemaphoreType.DMA((2,))]),
        compiler_params=_cparams(("arbitrary",)),
        name="moe_combine",
    )(pos_flat, h, info, ys, g_final.reshape(1, d).astype(F32))


def moe_layer(h, g, router, wg, wu, wd, g_final, *, final_norm, tm_r=512, tm_g=512, tf=1792, tm_c=512):
    m, d = h.shape
    xn, info, cnt = moe_router(h, g, router, tm=tm_r)
    e_idx = info[:, 0:2].astype(jnp.int32)
    rank = info[:, 4:6].astype(jnp.int32)
    counts = cnt[0, :N_EXPERTS].astype(jnp.int32)
    tiles_per = (counts + tm_g - 1) // tm_g
    tile_end = jnp.cumsum(tiles_per)
    seg_start = (tile_end - tiles_per) * tm_g
    pos = rank
    for e in range(N_EXPERTS):
        pos = pos + jnp.where(e_idx == e, seg_start[e], 0)
    n_tiles = (2 * m) // tm_g + N_EXPERTS
    n_slots = n_tiles * tm_g
    tok = jnp.broadcast_to(jnp.arange(m, dtype=jnp.int32)[:, None], (m, 2))
    src = jnp.zeros((n_slots,), jnp.int32).at[pos.reshape(-1)].set(
        tok.reshape(-1), unique_indices=True, mode="promise_in_bounds")
    n_active = tile_end[-1:].astype(jnp.int32)
    tile_ids = jnp.minimum(jnp.arange(n_tiles, dtype=jnp.int32), n_active[0] - 1)
    tile_expert = jnp.sum(tile_ids[:, None] >= tile_end[None, :], axis=1).astype(jnp.int32)
    ys = moe_expert_ffn(xn, src, tile_expert, n_active,
                        wg, wu, wd, tm=tm_g, tf=tf)
    return moe_combine(h, info, ys, pos.reshape(-1).astype(jnp.int32), g_final, tm=tm_c, final_norm=final_norm)


def _final_norm_kernel(h_ref, g_ref, o_ref):
    h = h_ref[...]
    o_ref[...] = _rms(h, g_ref[...], h.shape[-1])


def final_rmsnorm(h, g, *, tm):
    m, d = h.shape
    return pl.pallas_call(
        _final_norm_kernel,
        out_shape=jax.ShapeDtypeStruct((m, d), F32),
        grid=(m // tm,),
        in_specs=[pl.BlockSpec((tm, d), lambda i: (i, 0)), pl.BlockSpec((1, d), lambda i: (0, 0))],
        out_specs=pl.BlockSpec((tm, d), lambda i: (i, 0)),
        compiler_params=_cparams(("parallel",)),
        name="final_rmsnorm",
    )(h, g.reshape(1, d).astype(F32))


def _pack_w_in(w):
    d = w.shape[0]
    w = w.astype(F32)
    o = 0
    u = w[:, o:o + SSM_WIDTH]; o += SSM_WIDTH
    cq = w[:, o:o + MLA_Q_RANK]; o += MLA_Q_RANK
    ckv = w[:, o:o + MLA_KV_RANK]; o += MLA_KV_RANK
    kr = w[:, o:o + MLA_ROPE]; o += MLA_ROPE
    nq = w[:, o:o + NSA_HEADS * NSA_DIM]; o += NSA_HEADS * NSA_DIM
    nkv = w[:, o:o + 6 * NSA_KV_HEADS * NSA_DIM]; o += 6 * NSA_KV_HEADS * NSA_DIM
    gate = w[:, o:o + 3 * NSA_HEADS]
    z = lambda n: jnp.zeros((d, n), F32)
    kr_a = jnp.concatenate([z(MLA_NOPE), kr, z(LANES - MLA_NOPE - MLA_ROPE)], axis=1)
    kr_b = jnp.concatenate([z(MLA_NOPE), _rot_half_cols(kr), z(LANES - MLA_NOPE - MLA_ROPE)], axis=1)
    nq_h = (nq * (NSA_DIM ** -0.5 * LOG2E)).reshape(d, NSA_KV_HEADS, NSA_REP, NSA_DIM)
    zq = jnp.zeros((d, NSA_REP, NSA_DIM), F32)
    nq_p = jnp.concatenate([
        jnp.concatenate([nq_h[:, 0], zq], axis=-1).reshape(d, NSA_REP * LANES),
        jnp.concatenate([zq, nq_h[:, 1]], axis=-1).reshape(d, NSA_REP * LANES)], axis=1)
    packed = jnp.concatenate([u, cq, z(W_CQ - MLA_Q_RANK), kr_a, kr_b, nq_p, nkv, ckv,
                              gate, z(LANES - 3 * NSA_HEADS)], axis=1)
    assert packed.shape[1] == IN_COLS_PACKED
    return packed.astype(BF16)


def _rg_order(a):
    rest = a.shape[1:]
    return a.reshape((NSA_KV_HEADS, NSA_REP, NSA_DIM) + rest).swapaxes(0, 1).reshape((-1,) + rest)


def kernel(x, mem, w_in, w_out, mix_norm, out_norm, ssm_a_re, ssm_a_im, ssm_b_re, ssm_b_im, ssm_c_re, ssm_c_im, ssm_d, ssm_log_dt, ssm_w_glu, mla_q_norm, mla_w_uq, mla_kv_norm, mla_w_ukv, nsa_cmp_pe, nsa_cmp_w1, nsa_cmp_w2, rel_bias, xattn_norm, mem_norm, xattn_wq, xattn_wkv, xattn_wo, ffn_norm, dense_w_gate, dense_w_up, dense_w_down, moe_router, moe_w_gate, moe_w_up, moe_w_down, final_norm):
    bsz, seq, d = x.shape
    depth = w_in.shape[0]
    T = bsz * seq
    nmem = mem.shape[1]
    tq_nsa, tk_nsa = 2 * LANES, 4 * LANES
    rope_tabs = _rope_tables(seq)
    nsa_tabs = _nsa_bias_tables(rel_bias, seq, tq_nsa)
    o1, o2 = SSM_WIDTH, SSM_WIDTH + MLA_HEADS * MLA_V
    mem2 = mem.reshape(bsz * nmem, d)
    h = x.reshape(T, d)
    for l in range(depth):
        proj = norm_matmul(h, mix_norm[l], _pack_w_in(w_in[l]), tm=1024, tn=IN_COLS_PACKED, out_dtype=BF16)
        proj3 = proj.reshape(bsz, seq, IN_COLS_PACKED)
        u_tm = proj3[:, :, C_U:C_U + SSM_WIDTH].transpose(1, 0, 2).reshape(seq * bsz, SSM_WIDTH)
        y_ssm = ssm_mixer(u_tm, ssm_a_re[l], ssm_a_im[l], ssm_b_re[l], ssm_b_im[l], ssm_c_re[l], ssm_c_im[l],
                          ssm_d[l], ssm_log_dt[l], ssm_w_glu[l], nb=bsz, tc=64)
        y_ssm = y_ssm.reshape(seq, bsz, SSM_WIDTH).transpose(1, 0, 2)
        y_mla = mla_mixer(proj3, mla_q_norm[l], mla_w_uq[l], mla_kv_norm[l], mla_w_ukv[l], rope_tabs,
                          tm=512, tq=512)
        nch = seq // CMP_STRIDE
        kv_cr = jnp.stack([proj3[:, :, C_NKV:C_NKV + LANES].reshape(bsz, nch, CMP_STRIDE * LANES),
                           proj3[:, :, C_NKV + LANES:C_NKV + 2 * LANES].reshape(bsz, nch, CMP_STRIDE * LANES)],
                          axis=1)
        kvc = nsa_compress(kv_cr, nsa_cmp_pe[l], nsa_cmp_w1[l], nsa_cmp_w2[l])
        y_nsa = nsa_mixer(proj3, kvc, nsa_tabs, tq=tq_nsa, tk=tk_nsa)
        g_out = out_norm[l]
        wo_l = w_out[l]
        kv_mem = norm_matmul(mem2, mem_norm[l], xattn_wkv[l].astype(BF16), tm=256, tn=512, out_dtype=BF16)
        h = mix_out_cross_attention(
            [y_ssm, y_mla, y_nsa], h.reshape(bsz, seq, d),
            [g_out[:o1], g_out[o1:o2], _rg_order(g_out[o2:])],
            [wo_l[:o1].astype(BF16), wo_l[o1:o2].astype(BF16), _rg_order(wo_l[o2:]).astype(BF16)],
            kv_mem.reshape(bsz, nmem, 2 * d), xattn_norm[l], xattn_wq[l], xattn_wo[l], tm=512).reshape(T, d)
        last = l == depth - 1
        if l % 2 == 0:
            h = dense_ffn(h, ffn_norm[l], dense_w_gate[l // 2], dense_w_up[l // 2], dense_w_down[l // 2],
                          tm=1024, tf=1408)
            if last:
                h = final_rmsnorm(h, final_norm, tm=512)
        else:
            h = moe_layer(h, ffn_norm[l], moe_router[l // 2], moe_w_gate[l // 2], moe_w_up[l // 2],
                          moe_w_down[l // 2], final_norm, final_norm=last)
    return h.reshape(bsz, seq, d)
```

```python
import functools
import math

import jax
import jax.numpy as jnp
from jax import lax
from jax.experimental import pallas as pl
from jax.experimental.pallas import tpu as pltpu

F32 = jnp.float32
BF16 = jnp.bfloat16

SSM_WIDTH = 256
SSM_CH = 16
SSM_GROUPS = 16
SSM_STATE = 64
MLA_HEADS = 6
MLA_NOPE = 64
MLA_ROPE = 32
MLA_V = 64
MLA_Q_RANK = 192
MLA_KV_RANK = 128
NSA_HEADS = 6
NSA_KV_HEADS = 2
NSA_REP = 3
NSA_DIM = 64
CMP_BLOCK = 32
CMP_STRIDE = 16
SEL_BLOCK = 64
SEL_TOPN = 8
WINDOW = 256
REL_BUCKETS = 32
REL_MAX_DIST = 128
XATTN_HEADS = 4
N_EXPERTS = 8
ROPE_THETA = 10000.0
EPS = 1e-6
NEG_INF = -1e30
FORCE = 1e9
LOG2E = math.log2(math.e)

LANES = 128
VMEM_LIMIT = 56 * 1024 * 1024

W_CQ = 2 * LANES
W_KR = 2 * LANES
W_NQ = NSA_HEADS * LANES
W_NKV = 6 * LANES
_WIDTHS = (SSM_WIDTH, W_CQ, W_KR, W_NQ, W_NKV, MLA_KV_RANK, LANES)
C_U, C_CQ, C_KR, C_NQ, C_NKV, C_CKV, C_GATE = (sum(_WIDTHS[:k]) for k in range(len(_WIDTHS)))
IN_COLS_PACKED = sum(_WIDTHS)
assert all(c % w == 0 for c, w in zip((C_U, C_CQ, C_KR, C_NQ, C_NKV, C_CKV, C_GATE), _WIDTHS))


def _cparams(sem):
    return pltpu.CompilerParams(dimension_semantics=sem, vmem_limit_bytes=VMEM_LIMIT)


def _dot(a, b):
    return jnp.dot(a, b, preferred_element_type=F32)


def _dot_t(a, b):
    return lax.dot_general(a, b, (((1,), (1,)), ((), ())), preferred_element_type=F32)


def _rms(x, g, n):
    ms = jnp.sum(x * x, axis=-1, keepdims=True) * (1.0 / n)
    return x * lax.rsqrt(ms + EPS) * g


def _sigmoid(x):
    return 1.0 / (1.0 + jnp.exp(-x))


def _silu(x):
    return x * _sigmoid(x)


def _swiglu_accumulate(x_ref, wg, wu, wd, acc_ref, parts=2):
    rows = x_ref.shape[0] // parts
    sl = [slice(k * rows, (k + 1) * rows) for k in range(parts)]
    x = [x_ref[s, :] for s in sl]
    g = [_dot(xk, wg) for xk in x]
    u = [_dot(xk, wu) for xk in x]
    a = [(_silu(gk) * uk).astype(BF16) for gk, uk in zip(g, u)]
    d = [_dot(ak, wd) for ak in a]
    for s, dk in zip(sl, d):
        acc_ref[s, :] += dk


def _norm_mm_kernel(x_ref, g_ref, w_ref, o_ref, xn_ref):
    @pl.when(pl.program_id(1) == 0)
    def _():
        x = x_ref[...].astype(F32)
        xn_ref[...] = _rms(x, g_ref[...], x.shape[-1]).astype(BF16)

    o_ref[...] = _dot(xn_ref[...], w_ref[...]).astype(o_ref.dtype)


def _norm_mm_wide_kernel(x_ref, g_ref, w_ref, o_ref):
    rows = x_ref.shape[0] // 2
    sl = [slice(k * rows, (k + 1) * rows) for k in range(2)]
    xn = [_rms(x_ref[s, :].astype(F32), g_ref[...], x_ref.shape[-1]).astype(BF16) for s in sl]
    for s, xk in zip(sl, xn):
        o_ref[s, :] = _dot(xk, w_ref[...]).astype(o_ref.dtype)


def norm_matmul(x, g, w, *, tm, tn, out_dtype):
    m, k = x.shape
    n = w.shape[1]
    if tn == n:
        return pl.pallas_call(
            _norm_mm_wide_kernel,
            out_shape=jax.ShapeDtypeStruct((m, n), out_dtype),
            grid=(m // tm,),
            in_specs=[pl.BlockSpec((tm, k), lambda i: (i, 0)),
                      pl.BlockSpec((1, k), lambda i: (0, 0)),
                      pl.BlockSpec((k, n), lambda i: (0, 0))],
            out_specs=pl.BlockSpec((tm, n), lambda i: (i, 0)),
            compiler_params=_cparams(("parallel",)),
            name="norm_matmul_wide",
        )(x, g.reshape(1, k).astype(F32), w)
    return pl.pallas_call(
        _norm_mm_kernel,
        out_shape=jax.ShapeDtypeStruct((m, n), out_dtype),
        grid=(m // tm, n // tn),
        in_specs=[pl.BlockSpec((tm, k), lambda i, j: (i, 0)),
                  pl.BlockSpec((1, k), lambda i, j: (0, 0)),
                  pl.BlockSpec((k, tn), lambda i, j: (0, j))],
        out_specs=pl.BlockSpec((tm, tn), lambda i, j: (i, j)),
        scratch_shapes=[pltpu.VMEM((tm, k), BF16)],
        compiler_params=_cparams(("parallel", "arbitrary")),
        name="norm_matmul",
    )(x, g.reshape(1, k).astype(F32), w)


def _ssm_kernel(u_ref, bbr_ref, bbi_ref, ar_ref, ai_ref, ccr_ref, cci_ref, d_ref, wglu_ref,
                o_ref, hr_ref, hi_ref, cr_ref, ci_ref, *, tc, nb):
    @pl.when(pl.program_id(0) == 0)
    def _():
        cr_ref[...] = jnp.zeros_like(cr_ref)
        ci_ref[...] = jnp.zeros_like(ci_ref)

    half = (tc * nb) // 2
    sl = [slice(k * half, (k + 1) * half) for k in range(2)]
    u = [u_ref[s, :] for s in sl]
    bu_r = [_dot(x, bbr_ref[...]) for x in u]
    bu_i = [_dot(x, bbi_ref[...]) for x in u]
    for s, r, i in zip(sl, bu_r, bu_i):
        hr_ref[s, :] = r
        hi_ref[s, :] = i
    gp = ar_ref.shape[-1]
    ar = jnp.broadcast_to(ar_ref[...], (nb, gp))
    ai = jnp.broadcast_to(ai_ref[...], (nb, gp))

    def step(t, carry):
        hr, hi = carry
        rows = pl.ds(pl.multiple_of(t * nb, nb), nb)
        nr = ar * hr - ai * hi + hr_ref[rows, :]
        ni = ar * hi + ai * hr + hi_ref[rows, :]
        hr_ref[rows, :] = nr
        hi_ref[rows, :] = ni
        return nr, ni

    hr, hi = lax.fori_loop(0, tc, step, (cr_ref[...], ci_ref[...]))
    cr_ref[...] = hr
    ci_ref[...] = hi
    y_r = [_dot(hr_ref[s, :].astype(BF16), ccr_ref[...]) for s in sl]
    y_i = [_dot(hi_ref[s, :].astype(BF16), cci_ref[...]) for s in sl]
    y = [jax.nn.gelu(a + b + d_ref[...] * x.astype(F32)) for a, b, x in zip(y_r, y_i, u)]
    z = [_dot(v.astype(BF16), wglu_ref[...]) for v in y]
    for s, v, w in zip(sl, y, z):
        o_ref[s, :] = (v * _sigmoid(w)).astype(o_ref.dtype)


def ssm_mixer(u_tm, a_re, a_im, b_re, b_im, c_re, c_im, d, log_dt, w_glu, *, nb, tc):
    rows = u_tm.shape[0]
    G, P, C = SSM_GROUPS, SSM_STATE, SSM_CH
    dt = jnp.exp(log_dt.astype(F32))[:, None]
    lr, li = a_re.astype(F32), a_im.astype(F32)
    mag = jnp.exp(lr * dt)
    ab_r, ab_i = mag * jnp.cos(li * dt), mag * jnp.sin(li * dt)
    den = lr * lr + li * li
    nr = ab_r - 1.0
    f_r = (nr * lr + ab_i * li) / den
    f_i = (ab_i * lr - nr * li) / den
    br, bi = b_re.astype(F32), b_im.astype(F32)
    bb_r = f_r[..., None] * br - f_i[..., None] * bi
    bb_i = f_r[..., None] * bi + f_i[..., None] * br
    eye = jnp.eye(G, dtype=F32)
    bbr = jnp.einsum('gpc,gh->gchp', bb_r, eye).reshape(G * C, G * P).astype(BF16)
    bbi = jnp.einsum('gpc,gh->gchp', bb_i, eye).reshape(G * C, G * P).astype(BF16)
    ccr = jnp.einsum('gcp,gh->gphc', c_re.astype(F32), eye).reshape(G * P, G * C).astype(BF16)
    cci = jnp.einsum('gcp,gh->gphc', -c_im.astype(F32), eye).reshape(G * P, G * C).astype(BF16)
    gp = G * P
    full = lambda shape: pl.BlockSpec(shape, lambda i: (0,) * len(shape))
    return pl.pallas_call(
        functools.partial(_ssm_kernel, tc=tc, nb=nb),
        out_shape=jax.ShapeDtypeStruct((rows, SSM_WIDTH), BF16),
        grid=(rows // (tc * nb),),
        in_specs=[pl.BlockSpec((tc * nb, SSM_WIDTH), lambda i: (i, 0)),
                  full((G * C, gp)), full((G * C, gp)), full((1, gp)), full((1, gp)),
                  full((gp, G * C)), full((gp, G * C)), full((1, SSM_WIDTH)),
                  full((SSM_WIDTH, SSM_WIDTH))],
        out_specs=pl.BlockSpec((tc * nb, SSM_WIDTH), lambda i: (i, 0)),
        scratch_shapes=[pltpu.VMEM((tc * nb, gp), F32), pltpu.VMEM((tc * nb, gp), F32),
                        pltpu.VMEM((nb, gp), F32), pltpu.VMEM((nb, gp), F32)],
        compiler_params=_cparams(("arbitrary",)),
        name="ssm_mixer",
    )(u_tm, bbr, bbi, ab_r.reshape(1, gp), ab_i.reshape(1, gp), ccr, cci,
      d.reshape(1, SSM_WIDTH).astype(F32), w_glu.astype(BF16))


def _mla_prep_kernel(cq_ref, kr_ref, ckv_ref, gq_ref, gkv_ref, wqa_ref, wqb_ref, wk_ref, wv_ref,
                     c1_ref, c0_ref, s0_ref, q_ref, k_ref, v_ref):
    qn = _rms(cq_ref[0].astype(F32), gq_ref[...], MLA_Q_RANK).astype(BF16)
    qa = _dot(qn, wqa_ref[...])
    qb = _dot(qn, wqb_ref[...])
    kn = _rms(ckv_ref[0].astype(F32), gkv_ref[...], MLA_KV_RANK).astype(BF16)
    ka = _dot(kn, wk_ref[...])
    va = _dot(kn, wv_ref[...])
    kr = kr_ref[0].astype(F32)
    c1, c0, s0 = c1_ref[...], c0_ref[...], s0_ref[...]
    krope = kr[:, :LANES] * c0 + kr[:, LANES:] * s0
    low_half = lax.broadcasted_iota(jnp.int32, (kr.shape[0], LANES), 1) < MLA_V
    for h in range(MLA_HEADS):
        sl = slice(h * LANES, (h + 1) * LANES)
        q_ref[0, h] = (qa[:, sl] * c1 + qb[:, sl] * s0).astype(BF16)
        k_ref[0, h] = (ka[:, sl] + krope).astype(BF16)
        ones = jnp.where(low_half if h % 2 == 0 else jnp.logical_not(low_half), 1.0, 0.0)
        v_ref[0, h] = jnp.concatenate([va[:, sl], ones], axis=1).astype(BF16)


def _mla_flash_kernel(q_ref, k_ref, v_ref, o_ref, m_ref, acc_ref, *, tq):
    qi = pl.program_id(1)
    m_ref[...] = jnp.full_like(m_ref, NEG_INF)
    acc_ref[...] = jnp.zeros_like(acc_ref)
    lane = lax.broadcasted_iota(jnp.int32, (tq, LANES), 1)
    rep = tq // LANES

    def tile(kt, masked):
        ks = pl.ds(pl.multiple_of(kt * tq, tq), tq)
        if masked:
            mask = (lax.broadcasted_iota(jnp.int32, (tq, tq), 1)
                    <= lax.broadcasted_iota(jnp.int32, (tq, tq), 0))
        hs = range(MLA_HEADS)
        s = [_dot_t(q_ref[0, h], k_ref[0, h, ks, :]) for h in hs]
        if masked:
            s = [jnp.where(mask, x, NEG_INF) for x in s]
        m_prev = [m_ref[h] for h in hs]
        m_new = [jnp.maximum(mp, jnp.max(x, axis=-1, keepdims=True)) for mp, x in zip(m_prev, s)]
        alpha = [jnp.exp2(mp - mn) for mp, mn in zip(m_prev, m_new)]
        p = [jnp.exp2(x - jnp.tile(mn, (1, rep))).astype(BF16) for x, mn in zip(s, m_new)]
        pv = [_dot(p[h], v_ref[0, h, ks, :]) for h in hs]
        for h in hs:
            m_ref[h] = m_new[h]
        for pr in range(MLA_HEADS // 2):
            a = jnp.tile(jnp.where(lane < MLA_V, alpha[2 * pr], alpha[2 * pr + 1]), (1, 2))
            acc_ref[pr] = acc_ref[pr] * a + pv[2 * pr] + pv[2 * pr + 1]

    def body(kt, c):
        tile(kt, False)
        return c

    lax.fori_loop(0, qi, body, 0)
    tile(qi, True)
    for pr in range(MLA_HEADS // 2):
        o_ref[0, :, pr * LANES:(pr + 1) * LANES] = (acc_ref[pr, :, :LANES] / acc_ref[pr, :, LANES:]).astype(o_ref.dtype)


def _rope_tables(seq):
    pos = jnp.arange(seq, dtype=F32)
    inv = 1.0 / (ROPE_THETA ** (jnp.arange(0, MLA_ROPE, 2, dtype=F32) / MLA_ROPE))
    ang = pos[:, None] * inv[None, :]
    cos, sin = jnp.cos(ang), jnp.sin(ang)
    cos2 = jnp.concatenate([cos, cos], axis=-1)
    sin2 = jnp.concatenate([sin, sin], axis=-1)
    z64 = jnp.zeros((seq, MLA_NOPE), F32)
    z32 = jnp.zeros((seq, LANES - MLA_NOPE - MLA_ROPE), F32)
    c1 = jnp.concatenate([jnp.ones((seq, MLA_NOPE), F32), cos2, z32], axis=-1)
    c0 = jnp.concatenate([z64, cos2, z32], axis=-1)
    s0 = jnp.concatenate([z64, sin2, z32], axis=-1)
    return c1, c0, s0


def _rot_half_cols(w):
    half = MLA_ROPE // 2
    return jnp.concatenate([-w[..., half:], w[..., :half]], axis=-1)


def mla_mixer(proj3, q_norm, w_uq, kv_norm, w_ukv, tabs, *, tm, tq):
    bsz, seq, _ = proj3.shape
    H = MLA_HEADS
    scale = (MLA_NOPE + MLA_ROPE) ** -0.5 * LOG2E
    wq =(w_uq.astype(F32) * scale).reshape(MLA_Q_RANK, H, MLA_NOPE + MLA_ROPE)
    zq = jnp.zeros((MLA_Q_RANK, H, LANES - MLA_NOPE - MLA_ROPE), F32)
    z64 = jnp.zeros((MLA_Q_RANK, H, MLA_NOPE), F32)
    wqa = jnp.concatenate([wq, zq], axis=-1).reshape(MLA_Q_RANK, H * LANES)
    wqb = jnp.concatenate([z64, _rot_half_cols(wq[..., MLA_NOPE:]), zq], axis=-1).reshape(MLA_Q_RANK, H * LANES)
    padq = ((0, W_CQ - MLA_Q_RANK), (0, 0))
    wqa = jnp.pad(wqa, padq).astype(BF16)
    wqb = jnp.pad(wqb, padq).astype(BF16)
    gq = jnp.pad(q_norm.astype(F32), (0, W_CQ - MLA_Q_RANK)).reshape(1, W_CQ)
    wkv = w_ukv.astype(F32).reshape(MLA_KV_RANK, H, MLA_NOPE + MLA_V)
    zk = jnp.zeros((MLA_KV_RANK, H, MLA_NOPE), F32)
    wk = jnp.concatenate([wkv[..., :MLA_NOPE], zk], axis=-1).reshape(MLA_KV_RANK, H * LANES).astype(BF16)
    wv_h = wkv[..., MLA_NOPE:]
    even = (jnp.arange(H) % 2 == 0)[None, :, None]
    wv = jnp.concatenate([jnp.where(even, wv_h, 0.0), jnp.where(even, 0.0, wv_h)], axis=-1)
    wv = wv.reshape(MLA_KV_RANK, H * LANES).astype(BF16)
    c1, c0, s0 = tabs
    full2 = lambda shape: pl.BlockSpec(shape, lambda b, i: (0,) * len(shape))
    tab_spec = pl.BlockSpec((tm, LANES), lambda b, i: (i, 0))
    hd_spec = pl.BlockSpec((1, H, tm, LANES), lambda b, i: (b, 0, i, 0))
    hd_shape = jax.ShapeDtypeStruct((bsz, H, seq, LANES), BF16)
    v_spec = pl.BlockSpec((1, H, tm, 2 * LANES), lambda b, i: (b, 0, i, 0))
    v_shape = jax.ShapeDtypeStruct((bsz, H, seq, 2 * LANES), BF16)
    q, k, v = pl.pallas_call(
        _mla_prep_kernel,
        out_shape=(hd_shape, hd_shape, v_shape),
        grid=(bsz, seq // tm),
        in_specs=[pl.BlockSpec((1, tm, W_CQ), lambda b, i: (b, i, C_CQ // W_CQ)),
                  pl.BlockSpec((1, tm, W_KR), lambda b, i: (b, i, C_KR // W_KR)),
                  pl.BlockSpec((1, tm, MLA_KV_RANK), lambda b, i: (b, i, C_CKV // MLA_KV_RANK)),
                  full2((1, W_CQ)), full2((1, MLA_KV_RANK)),
                  full2((W_CQ, H * LANES)), full2((W_CQ, H * LANES)),
                  full2((MLA_KV_RANK, H * LANES)), full2((MLA_KV_RANK, H * LANES)),
                  tab_spec, tab_spec, tab_spec],
        out_specs=(hd_spec, hd_spec, v_spec),
        compiler_params=_cparams(("parallel", "parallel")),
        name="mla_prep",
    )(proj3, proj3, proj3, gq, kv_norm.astype(F32).reshape(1, MLA_KV_RANK), wqa, wqb, wk, wv, c1, c0, s0)

    return pl.pallas_call(
        functools.partial(_mla_flash_kernel, tq=tq),
        out_shape=jax.ShapeDtypeStruct((bsz, seq, H * MLA_V), BF16),
        grid=(bsz, seq // tq),
        in_specs=[pl.BlockSpec((1, H, tq, LANES), lambda b, i: (b, 0, i, 0)),
                  pl.BlockSpec((1, H, seq, LANES), lambda b, i: (b, 0, 0, 0)),
                  pl.BlockSpec((1, H, seq, 2 * LANES), lambda b, i: (b, 0, 0, 0))],
        out_specs=pl.BlockSpec((1, tq, H * MLA_V), lambda b, i: (b, i, 0)),
        scratch_shapes=[pltpu.VMEM((H, tq, LANES), F32), pltpu.VMEM((H // 2, tq, 2 * LANES), F32)],
        compiler_params=_cparams(("parallel", "arbitrary")),
        name="mla_flash",
    )(q, k, v)


def _nsa_cmp_kernel(x_ref, pea_ref, peb_ref, w1a_ref, w1b_ref, w2_ref, o_ref):
    x = x_ref[0, 0]
    w1a, w1b = w1a_ref[0], w1b_ref[0]
    bias = _dot(pea_ref[0], w1a)[0:1] + _dot(peb_ref[0], w1b)[0:1]
    a = _dot(x, w1a)
    b = _dot(x, w1b)
    n = b.shape[0]
    pre = a + pltpu.roll(b, n - 1, 0) + bias
    o_ref[0, 0] = _dot(jax.nn.gelu(pre).astype(BF16), w2_ref[0]).astype(o_ref.dtype)


def nsa_compress(kv_cr, cmp_pe, cmp_w1, cmp_w2):
    bsz, _, nch, width = kv_cr.shape
    G, dh = NSA_KV_HEADS, NSA_DIM
    half = CMP_BLOCK // 2
    eye = jnp.eye(G, dtype=F32)
    w1r = cmp_w1.astype(F32).reshape(2, CMP_BLOCK, dh, dh)
    w1a = jnp.einsum('kpde,gh->kpgdhe', w1r[:, :half], eye).reshape(2, width, G * dh).astype(BF16)
    w1b = jnp.einsum('kpde,gh->kpgdhe', w1r[:, half:], eye).reshape(2, width, G * dh).astype(BF16)
    w2 = jnp.einsum('kde,gh->kgdhe', cmp_w2.astype(F32), eye).reshape(2, G * dh, G * dh).astype(BF16)
    pe = cmp_pe.astype(F32)
    pe_g = jnp.broadcast_to(pe[:, :, None, :], (2, CMP_BLOCK, G, dh))
    pea = jnp.broadcast_to(pe_g[:, :half].reshape(2, 1, width), (2, 8, width)).astype(BF16)
    peb = jnp.broadcast_to(pe_g[:, half:].reshape(2, 1, width), (2, 8, width)).astype(BF16)
    kvspec = lambda shape: pl.BlockSpec(shape, lambda b, k: (k,) + (0,) * (len(shape) - 1))
    return pl.pallas_call(
        _nsa_cmp_kernel,
        out_shape=jax.ShapeDtypeStruct((bsz, 2, nch, G * dh), BF16),
        grid=(bsz, 2),
        in_specs=[pl.BlockSpec((1, 1, nch, width), lambda b, k: (b, k, 0, 0)),
                  kvspec((1, 8, width)), kvspec((1, 8, width)),
                  kvspec((1, width, G * dh)), kvspec((1, width, G * dh)),
                  kvspec((1, G * dh, G * dh))],
        out_specs=pl.BlockSpec((1, 1, nch, G * dh), lambda b, k: (b, k, 0, 0)),
        compiler_params=_cparams(("parallel", "parallel")),
        name="nsa_compress",
    )(kv_cr, pea, peb, w1a, w1b, w2)


def _nsa_kernel(q_ref, gate_ref, ksl_ref, vsl_ref, kwn_ref, vwn_ref, kvc_ref, bc_ref, bw_ref,
                bs_ref, ov_ref, blk1h_ref, gexp_ref, o_ref, kaug_ref, vaug_ref, m_ref, acc_ref,
                *, tq, tk, nbs, n_sel):
    qi = pl.program_id(1)
    R, G = NSA_REP, NSA_KV_HEADS
    H = R * G
    nsub = tk // tq
    nwin = WINDOW // tq

    @pl.when(qi == 0)
    def _():
        kaug_ref[:, :LANES] = ksl_ref[0]
        kaug_ref[:, LANES:] = blk1h_ref[...]
        vaug_ref[:, :LANES] = vsl_ref[0]
        vaug_ref[:, LANES:] = jnp.ones((vaug_ref.shape[0], LANES), BF16)

    lane = lax.broadcasted_iota(jnp.int32, (tq, LANES), 1)
    t_row = qi * tq + lax.broadcasted_iota(jnp.int32, (H * tq, 1), 0) % tq
    kc = kvc_ref[0, 0]
    vc = kvc_ref[0, 1]
    ov = ov_ref[...]

    def stack(fn):
        return jnp.concatenate([fn(h) for h in range(H)], axis=0)

    q_all = stack(lambda h: q_ref[0, :, h * LANES:(h + 1) * LANES])

    c = jnp.minimum(qi, nwin)
    ws = pl.ds(pl.multiple_of(jnp.maximum(qi - nwin, 0) * tq, tq), (nwin + 1) * tq)
    k_w = kwn_ref[0, ws, :]
    v_w = vwn_ref[0, ws, :]
    o_w = []
    for g in range(G):
        rows = slice(g * R * tq, (g + 1) * R * tq)
        s_w = _dot_t(q_all[rows], k_w) + jnp.concatenate([bw_ref[g * R + r, c] for r in range(R)], axis=0)
        p_w = jnp.exp2(s_w - jnp.max(s_w, axis=-1, keepdims=True))
        o_w.append(_dot(p_w.astype(BF16), v_w) / jnp.sum(p_w, axis=-1, keepdims=True))
    o_w = jnp.concatenate(o_w, axis=0)

    halves = [slice(g * R * tq, (g + 1) * R * tq) for g in range(G)]
    valid = [t_row[hs] >= (CMP_BLOCK - 1) for hs in halves]
    s = [_dot_t(q_all[hs], kc) + jnp.concatenate([bc_ref[g * R + r] for r in range(R)], axis=0)
         for g, hs in enumerate(halves)]
    p = [jnp.where(v, jnp.exp2(x - jnp.max(x, axis=-1, keepdims=True)), 0.0) for v, x in zip(valid, s)]
    pc = [x / jnp.where(v, jnp.sum(x, axis=-1, keepdims=True), 1.0) for v, x in zip(valid, p)]
    o_c = jnp.concatenate([_dot(x.astype(BF16), vc) for x in pc], axis=0)
    blk = lax.broadcasted_iota(jnp.int32, (nbs, tq), 0)
    tl = qi * tq + lax.broadcasted_iota(jnp.int32, (nbs, tq), 1)
    cur = tl // SEL_BLOCK
    forced = (blk == 0) | (blk == cur) | (blk == cur - 1)
    future = blk * SEL_BLOCK > tl
    qmask = []
    for g in range(G):
        psum = pc[g][0:tq] + pc[g][tq:2 * tq] + pc[g][2 * tq:3 * tq]
        p_hi = psum.astype(BF16)
        p_lo = (psum - p_hi.astype(F32)).astype(BF16)
        imp = (_dot_t(ov, p_hi) + _dot_t(ov, p_lo))[:nbs]
        imp = jnp.where(forced, FORCE, jnp.where(future, -FORCE, imp))
        rank = jnp.zeros((nbs, tq), F32)
        for i in range(nbs):
            ri = imp[i:i + 1, :]
            beats = (ri > imp) | ((ri == imp) & (blk > i))
            rank = rank + jnp.where(beats, 1.0, 0.0)
        sel = jnp.where(rank < n_sel, 0.0, NEG_INF)
        sel = jnp.concatenate([sel, jnp.zeros((LANES - nbs, tq), F32)], axis=0).T.astype(BF16)
        qmask += [sel] * R
    q_aug = jnp.concatenate([q_all, jnp.concatenate(qmask, axis=0)], axis=1)

    m_ref[...] = jnp.full_like(m_ref, NEG_INF)
    acc_ref[...] = jnp.zeros_like(acc_ref)

    def sel_tile(kt, bias):
        ks = pl.ds(pl.multiple_of(kt * tk, tk), tk)
        k_t = kaug_ref[ks, :]
        v_t = vaug_ref[ks, :]
        s = [_dot_t(q_aug[hs], k_t) for hs in halves]
        if bias is not None:
            s = [x + bias[hs] for x, hs in zip(s, halves)]
        m_prev = [m_ref[hs, :] for hs in halves]
        m_new = [jnp.maximum(mp, jnp.max(x, axis=-1, keepdims=True)) for mp, x in zip(m_prev, s)]
        alpha = [jnp.exp2(mp - mn) for mp, mn in zip(m_prev, m_new)]
        p = [jnp.exp2(x - jnp.tile(mn, (1, tk // LANES))).astype(BF16) for x, mn in zip(s, m_new)]
        pv = [_dot(pp, v_t) for pp in p]
        for hs, mn, a, o in zip(halves, m_new, alpha, pv):
            m_ref[hs, :] = mn
            acc_ref[hs, :] = acc_ref[hs, :] * jnp.tile(a, (1, 2)) + o

    def near_bias(kt):
        cols = []
        for sub in range(nsub):
            d = qi - (kt * nsub + sub)
            cols.append(stack(lambda h: jnp.where(d == 0, bs_ref[h, 0], jnp.where(
                d == 1, bs_ref[h, 1], jnp.where(d < 0, NEG_INF, 0.0)))))
        return jnp.concatenate(cols, axis=1)

    def far_body(kt, c):
        sel_tile(kt, None)
        return c

    kd = (qi * tq) // tk
    lax.fori_loop(0, jnp.maximum(kd - 1, 0), far_body, 0)

    @pl.when(kd >= 1)
    def _():
        sel_tile(kd - 1, near_bias(kd - 1))

    sel_tile(kd, near_bias(kd))
    o_s = acc_ref[:, :LANES] / acc_ref[:, LANES:]

    gates = _sigmoid(_dot(gate_ref[0], gexp_ref[...]))
    for r in range(R):
        res = None
        for b, o_b in enumerate((o_c, o_s, o_w)):
            o_br = jnp.where(lane < NSA_DIM, o_b[r * tq:(r + 1) * tq], o_b[(R + r) * tq:(R + r + 1) * tq])
            term = gates[:, (b * R + r) * LANES:(b * R + r + 1) * LANES] * o_br
            res = term if res is None else res + term
        o_ref[0, :, r * LANES:(r + 1) * LANES] = res.astype(o_ref.dtype)


def _t5_bucket(dist):
    n = jnp.maximum(dist, 0)
    exact = REL_BUCKETS // 2
    nf = jnp.maximum(n, exact).astype(F32)
    large = exact + jnp.floor(jnp.log(nf / exact) / math.log(REL_MAX_DIST / exact)
                              * (REL_BUCKETS - exact)).astype(jnp.int32)
    return jnp.where(n < exact, n, jnp.minimum(large, REL_BUCKETS - 1))


def _nsa_bias_tables(rel_bias, seq, tq):
    rb = rel_bias.astype(F32).T
    far = rb[:, REL_BUCKETS - 1].reshape(NSA_HEADS, 1, 1)

    def by_dist(dist, ok, shift=0.0):
        bucket = _t5_bucket(dist)[None]
        out = jnp.zeros((NSA_HEADS,) + dist.shape, F32)
        for k in range(REL_BUCKETS):
            out = jnp.where(bucket == k, rb[:, k].reshape((NSA_HEADS,) + (1,) * dist.ndim), out)
        return jnp.where(ok[None], (out - shift) * LOG2E, NEG_INF)

    i = jnp.arange(tq)[:, None]
    t = jnp.arange(seq)[:, None]
    dist_c = t - (jnp.arange(LANES)[None, :] * CMP_STRIDE + CMP_BLOCK - 1)
    bc = by_dist(dist_c, dist_c >= 0)
    nwin = WINDOW // tq
    jw = jnp.arange((nwin + 1) * tq)[None, :]
    bw = jnp.stack([by_dist(tq * c + i - jw, (tq * c + i - jw >= 0) & (tq * c + i - jw < WINDOW))
                    for c in range(nwin + 1)], axis=1)
    js = jnp.arange(tq)[None, :]
    bs = jnp.stack([by_dist(tq * c + i - js, tq * c + i - js >= 0, far) for c in range(2)], axis=1)
    ci = jnp.arange(LANES)[:, None]
    sj = jnp.arange(LANES)[None, :]
    nbs = seq // SEL_BLOCK
    ov = ((ci * CMP_STRIDE <= sj * SEL_BLOCK + SEL_BLOCK - 1)
          & (ci * CMP_STRIDE + CMP_BLOCK - 1 >= sj * SEL_BLOCK)
          & (ci < seq // CMP_STRIDE - 1) & (sj < nbs))
    blk1h = (jnp.arange(seq)[:, None] // SEL_BLOCK == sj).astype(BF16)
    col = jnp.arange(3 * NSA_REP * LANES)
    slab, lane_g = col // LANES, (col % LANES) // NSA_DIM
    gate_col = (slab // NSA_REP) * NSA_HEADS + lane_g * NSA_REP + slab % NSA_REP
    gexp = (ci == gate_col[None, :]).astype(BF16)
    return bc, bw, bs, ov.T.astype(BF16), blk1h, gexp


def nsa_mixer(proj3, kvc, tables, *, tq, tk):
    bsz, seq, _ = proj3.shape
    assert tq % LANES == 0 and tq >= REL_MAX_DIST and tk % tq == 0 and seq % tk == 0
    assert seq // CMP_STRIDE == LANES and WINDOW % tq == 0
    nwin = WINDOW // tq
    bc, bw, bs, ov, blk1h, gexp = tables
    nbs = seq // SEL_BLOCK
    H = NSA_HEADS
    slab = lambda j: pl.BlockSpec((1, seq, LANES), lambda b, i: (b, 0, C_NKV // LANES + j))
    const = lambda shape: pl.BlockSpec(shape, lambda b, i: (0,) * len(shape))
    return pl.pallas_call(
        functools.partial(_nsa_kernel, tq=tq, tk=tk, nbs=nbs, n_sel=min(SEL_TOPN, nbs)),
        out_shape=jax.ShapeDtypeStruct((bsz, seq, H * NSA_DIM), BF16),
        grid=(bsz, seq // tq),
        in_specs=[pl.BlockSpec((1, tq, H * LANES), lambda b, i: (b, i, C_NQ // (H * LANES))),
                  pl.BlockSpec((1, tq, LANES), lambda b, i: (b, i, C_GATE // LANES)),
                  slab(2), slab(3), slab(4), slab(5),
                  pl.BlockSpec((1, 2, LANES, LANES), lambda b, i: (b, 0, 0, 0)),
                  pl.BlockSpec((H, tq, LANES), lambda b, i: (0, i, 0)),
                  const((H, nwin + 1, tq, (nwin + 1) * tq)), const((H, 2, tq, tq)),
                  const((LANES, LANES)), const((seq, LANES)), const(gexp.shape)],
        out_specs=pl.BlockSpec((1, tq, H * NSA_DIM), lambda b, i: (b, i, 0)),
        scratch_shapes=[pltpu.VMEM((seq, 2 * LANES), BF16), pltpu.VMEM((seq, 2 * LANES), BF16),
                        pltpu.VMEM((H * tq, LANES), F32), pltpu.VMEM((H * tq, 2 * LANES), F32)],
        compiler_params=_cparams(("parallel", "arbitrary")),
        name="nsa_attention",
    )(proj3, proj3, proj3, proj3, proj3, proj3, kvc, bc, bw, bs, ov, blk1h, gexp)


def _mix_xattn_kernel(ys_ref, ym_ref, yn_ref, h_ref, g1_ref, g2_ref, g3_ref, w1_ref, w2_ref, w3_ref,
                      kv_ref, g_ref, wq_ref, wo_ref, o_ref, *, dh):
    rows = h_ref.shape[1] // 2
    sl = [slice(k * rows, (k + 1) * rows) for k in range(2)]
    hw = XATTN_HEADS * dh
    heads = range(XATTN_HEADS)

    def normed(y_ref, gy_ref):
        return [_rms(y_ref[0, s, :].astype(F32), gy_ref[...], y_ref.shape[-1]).astype(BF16) for s in sl]

    ns, nm, nn = normed(ys_ref, g1_ref), normed(ym_ref, g2_ref), normed(yn_ref, g3_ref)
    h = [h_ref[0, s, :] + _dot(a, w1_ref[...]) + _dot(b, w2_ref[...]) + _dot(c, w3_ref[...])
         for s, a, b, c in zip(sl, ns, nm, nn)]
    xn = [_rms(x, g_ref[...], x.shape[-1]).astype(BF16) for x in h]
    q = [_dot(x, wq_ref[...]).astype(BF16) for x in xn]
    s = [[_dot_t(qk[:, hd * dh:(hd + 1) * dh], kv_ref[0, :, hd * dh:(hd + 1) * dh]) for hd in heads] for qk in q]
    p = [[jnp.exp2(x - jnp.max(x, axis=-1, keepdims=True)) for x in sk] for sk in s]
    p = [[(x / jnp.sum(x, axis=-1, keepdims=True)).astype(BF16) for x in pk] for pk in p]
    o = [jnp.concatenate([_dot(pk[hd], kv_ref[0, :, hw + hd * dh:hw + (hd + 1) * dh]).astype(BF16)
                          for hd in heads], axis=-1) for pk in p]
    for sk, hk, ok in zip(sl, h, o):
        o_ref[0, sk, :] = hk + _dot(ok, wo_ref[...])


def mix_out_cross_attention(ys3, h3, gains, weights, kv3, g_x, wq, wo, *, tm):
    bsz, seq, d = h3.shape
    m = kv3.shape[1]
    dh = d // XATTN_HEADS
    wq_s = (wq.astype(F32) * (dh ** -0.5 * LOG2E)).astype(BF16)
    gains = [g.reshape(1, -1).astype(F32) for g in gains]
    const = lambda shape: pl.BlockSpec(shape, lambda b, i: (0,) * len(shape))
    row = lambda w: pl.BlockSpec((1, tm, w), lambda b, i: (b, i, 0))
    return pl.pallas_call(
        functools.partial(_mix_xattn_kernel, dh=dh),
        out_shape=jax.ShapeDtypeStruct((bsz, seq, d), F32),
        grid=(bsz, seq // tm),
        in_specs=[row(y.shape[2]) for y in ys3] + [row(d)]
                 + [const(g.shape) for g in gains] + [const(w.shape) for w in weights]
                 + [pl.BlockSpec((1, m, 2 * d), lambda b, i: (b, 0, 0)),
                    const((1, d)), const((d, d)), const((d, d))],
        out_specs=row(d),
        compiler_params=_cparams(("parallel", "parallel")),
        name="mix_out_cross_attention",
    )(*ys3, h3, *gains, *weights, kv3, g_x.reshape(1, d).astype(F32), wq_s, wo.astype(BF16))


def _ffn_kernel(h_ref, g_ref, wg_ref, wu_ref, wd_ref, o_ref, xn_ref, acc_ref):
    j = pl.program_id(1)

    @pl.when(j == 0)
    def _():
        h = h_ref[...]
        xn_ref[...] = _rms(h, g_ref[...], h.shape[-1]).astype(BF16)
        acc_ref[...] = h

    _swiglu_accumulate(xn_ref, wg_ref[...], wu_ref[...], wd_ref[...], acc_ref)

    @pl.when(j == pl.num_programs(1) - 1)
    def _():
        o_ref[...] = acc_ref[...]


def dense_ffn(h, g, wg, wu, wd, *, tm, tf):
    m, d = h.shape
    ff = wg.shape[1]
    return pl.pallas_call(
        _ffn_kernel,
        out_shape=jax.ShapeDtypeStruct((m, d), F32),
        grid=(m // tm, ff // tf),
        in_specs=[pl.BlockSpec((tm, d), lambda i, j: (i, 0)),
                  pl.BlockSpec((1, d), lambda i, j: (0, 0)),
                  pl.BlockSpec((d, tf), lambda i, j: (0, j)),
                  pl.BlockSpec((d, tf), lambda i, j: (0, j)),
                  pl.BlockSpec((tf, d), lambda i, j: (j, 0))],
        out_specs=pl.BlockSpec((tm, d), lambda i, j: (i, 0)),
        scratch_shapes=[pltpu.VMEM((tm, d), BF16), pltpu.VMEM((tm, d), F32)],
        compiler_params=_cparams(("parallel", "arbitrary")),
        name="dense_ffn",
    )(h, g.reshape(1, d).astype(F32), wg.astype(BF16), wu.astype(BF16), wd.astype(BF16))


def _router_kernel(h_ref, g_ref, wr_hi_ref, wr_lo_ref, xn_ref, info_ref, cnt_ref, carry_ref, *, tm):
    i = pl.program_id(0)

    @pl.when(i == 0)
    def _():
        carry_ref[...] = jnp.zeros_like(carry_ref)

    h = h_ref[...]
    xn = _rms(h, g_ref[...], h.shape[-1])
    xn_ref[...] = xn
    x_hi = xn.astype(BF16)
    x_lo = (xn - x_hi.astype(F32)).astype(BF16)
    logits = _dot(x_hi, wr_hi_ref[...]) + _dot(x_lo, wr_hi_ref[...]) + _dot(x_hi, wr_lo_ref[...])
    lane = lax.broadcasted_iota(jnp.int32, (tm, LANES), 1)
    lanef = lane.astype(F32)
    logits = jnp.where(lane < N_EXPERTS, logits, NEG_INF)
    m1 = jnp.max(logits, axis=-1, keepdims=True)
    i1 = jnp.min(jnp.where(logits == m1, lanef, float(LANES)), axis=-1, keepdims=True)
    rest = jnp.where(lanef == i1, NEG_INF, logits)
    m2 = jnp.max(rest, axis=-1, keepdims=True)
    i2 = jnp.min(jnp.where(rest == m2, lanef, float(LANES)), axis=-1, keepdims=True)
    e2 = jnp.exp(m2 - m1)
    w1 = 1.0 / (1.0 + e2)
    w2 = e2 / (1.0 + e2)
    oh1 = lanef == i1
    oh2 = lanef == i2
    oh = jnp.where(oh1 | oh2, 1.0, 0.0)
    rr = lax.broadcasted_iota(jnp.int32, (tm, tm), 0)
    cc = lax.broadcasted_iota(jnp.int32, (tm, tm), 1)
    tri = jnp.where(cc < rr, 1.0, 0.0).astype(BF16)
    before = _dot(tri, oh.astype(BF16)) + carry_ref[0:1, :]
    r1 = jnp.sum(jnp.where(oh1, before, 0.0), axis=-1, keepdims=True)
    r2 = jnp.sum(jnp.where(oh2, before, 0.0), axis=-1, keepdims=True)
    carry_ref[...] = carry_ref[...] + jnp.sum(oh, axis=0, keepdims=True)
    info = jnp.where(lane == 0, i1, jnp.where(lane == 1, i2, jnp.where(lane == 2, w1, jnp.where(
        lane == 3, w2, jnp.where(lane == 4, r1, jnp.where(lane == 5, r2, 0.0))))))
    info_ref[...] = info
    cnt_ref[...] = carry_ref[...]


def moe_router(h, g, router, *, tm):
    m, d = h.shape
    wr = jnp.pad(router.astype(F32), ((0, 0), (0, LANES - N_EXPERTS)))
    wr_hi = wr.astype(BF16)
    wr_lo = (wr - wr_hi.astype(F32)).astype(BF16)
    return pl.pallas_call(
        functools.partial(_router_kernel, tm=tm),
        out_shape=(jax.ShapeDtypeStruct((m, d), F32), jax.ShapeDtypeStruct((m, LANES), F32),
                   jax.ShapeDtypeStruct((8, LANES), F32)),
        grid=(m // tm,),
        in_specs=[pl.BlockSpec((tm, d), lambda i: (i, 0)),
                  pl.BlockSpec((1, d), lambda i: (0, 0)),
                  pl.BlockSpec((d, LANES), lambda i: (0, 0)),
                  pl.BlockSpec((d, LANES), lambda i: (0, 0))],
        out_specs=(pl.BlockSpec((tm, d), lambda i: (i, 0)),
                   pl.BlockSpec((tm, LANES), lambda i: (i, 0)),
                   pl.BlockSpec((8, LANES), lambda i: (0, 0))),
        scratch_shapes=[pltpu.VMEM((8, LANES), F32)],
        compiler_params=_cparams(("arbitrary",)),
        name="moe_router",
    )(h, g.reshape(1, d).astype(F32), wr_hi, wr_lo)


def _row_copy(src_hbm, row, dst, slot, sem):
    return pltpu.make_async_copy(src_hbm.at[pl.ds(row, 1), :], dst.at[pl.ds(slot, 1), :], sem)


def _rows_wait(src_hbm, dst, sem):
    pltpu.make_async_copy(src_hbm.at[pl.ds(0, dst.shape[0]), :], dst, sem).wait()


def _moe_ffn_kernel(src_ref, texp_ref, nact_ref, x_hbm, wg_ref, wu_ref, wd_ref, o_ref,
                    xbuf, xbf, acc_ref, sem, *, tm, nj):
    i = pl.program_id(0)
    j = pl.program_id(1)
    nact = nact_ref[0]
    active = i < nact
    nbuf = xbuf.shape[0]
    ahead = nbuf - 1
    cur = i % nbuf
    rows_per_step = tm // nj

    for t0 in range(ahead):
        @pl.when((i == 0) & (j == 0) & (t0 < nact))
        def _():
            def issue(s, c):
                _row_copy(x_hbm, src_ref[t0 * tm + s], xbuf.at[t0], s, sem.at[t0]).start()
                return c

            lax.fori_loop(0, tm, issue, 0, unroll=8)

    @pl.when(active & (j == 0))
    def _():
        _rows_wait(x_hbm, xbuf.at[cur], sem.at[cur])
        xbf[...] = xbuf[cur].astype(BF16)
        acc_ref[...] = jnp.zeros_like(acc_ref)

    def compute(prefetch):
        if prefetch:
            nxt = (i + ahead) % nbuf
            base = (i + ahead) * tm + j * rows_per_step
            for k in range(rows_per_step):
                _row_copy(x_hbm, src_ref[base + k], xbuf.at[nxt], j * rows_per_step + k, sem.at[nxt]).start()
        _swiglu_accumulate(xbf, wg_ref[0], wu_ref[0], wd_ref[0], acc_ref)

    @pl.when(i + ahead < nact)
    def _():
        compute(True)

    @pl.when(active & (i + ahead >= nact))
    def _():
        compute(False)

    @pl.when(j == nj - 1)
    def _():
        o_ref[...] = jnp.where(active, acc_ref[...], 0.0)


def moe_expert_ffn(xn, src, tile_expert, n_active, wg, wu, wd, *, tm, tf):
    n_slots = src.shape[0]
    d = xn.shape[1]
    ne, _, ff = wg.shape
    nj = ff // tf
    assert nj * tf == ff and tm % nj == 0
    wg, wu, wd = wg.astype(BF16), wu.astype(BF16), wd.astype(BF16)

    def wmap_col(i, j, src, texp, nact):
        return (texp[i], 0, jnp.where(i < nact[0], j, nj - 1))

    def wmap_row(i, j, src, texp, nact):
        return (texp[i], jnp.where(i < nact[0], j, nj - 1), 0)

    return pl.pallas_call(
        functools.partial(_moe_ffn_kernel, tm=tm, nj=nj),
        out_shape=jax.ShapeDtypeStruct((n_slots, d), F32),
        grid_spec=pltpu.PrefetchScalarGridSpec(
            num_scalar_prefetch=3,
            grid=(n_slots // tm, nj),
            in_specs=[pl.BlockSpec(memory_space=pl.ANY),
                      pl.BlockSpec((1, d, tf), wmap_col),
                      pl.BlockSpec((1, d, tf), wmap_col),
                      pl.BlockSpec((1, tf, d), wmap_row)],
            out_specs=pl.BlockSpec((tm, d), lambda i, j, *_: (i, 0)),
            scratch_shapes=[pltpu.VMEM((3, tm, d), F32), pltpu.VMEM((tm, d), BF16),
                            pltpu.VMEM((tm, d), F32), pltpu.SemaphoreType.DMA((3,))]),
        compiler_params=_cparams(("arbitrary", "arbitrary")),
        name="moe_expert_ffn",
    )(src, tile_expert, n_active, xn, wg, wu, wd)


def _moe_combine_kernel(pos_ref, h_ref, info_ref, ys_hbm, g_ref, o_ref, buf, sem, *, tm, final_norm):
    i = pl.program_id(0)
    n = pl.num_programs(0)
    cur = i % 2

    def start_gather(tile, b):
        for s in range(tm):
            for k in range(2):
                _row_copy(ys_hbm, pos_ref[2 * (tile * tm + s) + k], buf.at[b, k], s, sem.at[b]).start()

    @pl.when(i == 0)
    def _():
        start_gather(0, 0)

    @pl.when(i + 1 < n)
    def _():
        start_gather(i + 1, 1 - cur)

    for k in range(2):
        _rows_wait(ys_hbm, buf.at[cur, k], sem.at[cur])
    info = info_ref[...]
    y = h_ref[...] + info[:, 2:3] * buf[cur, 0] + info[:, 3:4] * buf[cur, 1]
    if final_norm:
        y = _rms(y, g_ref[...], y.shape[-1])
    o_ref[...] = y


def moe_combine(h, info, ys, pos_flat, g_final, *, tm, final_norm):
    m, d = h.shape
    return pl.pallas_call(
        functools.partial(_moe_combine_kernel, tm=tm, final_norm=final_norm),
        out_shape=jax.ShapeDtypeStruct((m, d), F32),
        grid_spec=pltpu.PrefetchScalarGridSpec(
            num_scalar_prefetch=1,
            grid=(m // tm,),
            in_specs=[pl.BlockSpec((tm, d), lambda i, *_: (i, 0)),
                      pl.BlockSpec((tm, LANES), lambda i, *_: (i, 0)),
                      pl.BlockSpec(memory_space=pl.ANY),
                      pl.BlockSpec((1, d), lambda i, *_: (0, 0))],
            out_specs=pl.BlockSpec((tm, d), lambda i, *_: (i, 0)),
            scratch_shapes=[pltpu.VMEM((2, 2, tm, d), F32), pltpu.SemaphoreType.DMA((2,))]),
        compiler_params=_cparams(("arbitrary",)),
        name="moe_combine",
    )(pos_flat, h, info, ys, g_final.reshape(1, d).astype(F32))


def moe_layer(h, g, router, wg, wu, wd, g_final, *, final_norm, tm_r=512, tm_g=768, tf=1792, tm_c=512):
    m, d = h.shape
    xn, info, cnt = moe_router(h, g, router, tm=tm_r)
    e_idx = info[:, 0:2].astype(jnp.int32)
    rank = info[:, 4:6].astype(jnp.int32)
    counts = cnt[0, :N_EXPERTS].astype(jnp.int32)
    tiles_per = (counts + tm_g - 1) // tm_g
    tile_end = jnp.cumsum(tiles_per)
    seg_start = (tile_end - tiles_per) * tm_g
    pos = rank
    for e in range(N_EXPERTS):
        pos = pos + jnp.where(e_idx == e, seg_start[e], 0)
    n_tiles = (2 * m) // tm_g + N_EXPERTS
    n_slots = n_tiles * tm_g
    tok = jnp.broadcast_to(jnp.arange(m, dtype=jnp.int32)[:, None], (m, 2))
    src = jnp.zeros((n_slots,), jnp.int32).at[pos.reshape(-1)].set(tok.reshape(-1))
    n_active = tile_end[-1:].astype(jnp.int32)
    tile_ids = jnp.minimum(jnp.arange(n_tiles, dtype=jnp.int32), n_active[0] - 1)
    tile_expert = jnp.sum(tile_ids[:, None] >= tile_end[None, :], axis=1).astype(jnp.int32)
    ys = moe_expert_ffn(xn, src, tile_expert, n_active,
                        wg, wu, wd, tm=tm_g, tf=tf)
    return moe_combine(h, info, ys, pos.reshape(-1).astype(jnp.int32), g_final, tm=tm_c, final_norm=final_norm)


def _final_norm_kernel(h_ref, g_ref, o_ref):
    h = h_ref[...]
    o_ref[...] = _rms(h, g_ref[...], h.shape[-1])


def final_rmsnorm(h, g, *, tm):
    m, d = h.shape
    return pl.pallas_call(
        _final_norm_kernel,
        out_shape=jax.ShapeDtypeStruct((m, d), F32),
        grid=(m // tm,),
        in_specs=[pl.BlockSpec((tm, d), lambda i: (i, 0)), pl.BlockSpec((1, d), lambda i: (0, 0))],
        out_specs=pl.BlockSpec((tm, d), lambda i: (i, 0)),
        compiler_params=_cparams(("parallel",)),
        name="final_rmsnorm",
    )(h, g.reshape(1, d).astype(F32))


def _pack_w_in(w):
    d = w.shape[0]
    w = w.astype(F32)
    o = 0
    u = w[:, o:o + SSM_WIDTH]; o += SSM_WIDTH
    cq = w[:, o:o + MLA_Q_RANK]; o += MLA_Q_RANK
    ckv = w[:, o:o + MLA_KV_RANK]; o += MLA_KV_RANK
    kr = w[:, o:o + MLA_ROPE]; o += MLA_ROPE
    nq = w[:, o:o + NSA_HEADS * NSA_DIM]; o += NSA_HEADS * NSA_DIM
    nkv = w[:, o:o + 6 * NSA_KV_HEADS * NSA_DIM]; o += 6 * NSA_KV_HEADS * NSA_DIM
    gate = w[:, o:o + 3 * NSA_HEADS]
    z = lambda n: jnp.zeros((d, n), F32)
    kr_a = jnp.concatenate([z(MLA_NOPE), kr, z(LANES - MLA_NOPE - MLA_ROPE)], axis=1)
    kr_b = jnp.concatenate([z(MLA_NOPE), _rot_half_cols(kr), z(LANES - MLA_NOPE - MLA_ROPE)], axis=1)
    nq_h = (nq * (NSA_DIM ** -0.5 * LOG2E)).reshape(d, NSA_KV_HEADS, NSA_REP, NSA_DIM)
    zq = jnp.zeros((d, NSA_REP, NSA_DIM), F32)
    nq_p = jnp.concatenate([
        jnp.concatenate([nq_h[:, 0], zq], axis=-1).reshape(d, NSA_REP * LANES),
        jnp.concatenate([zq, nq_h[:, 1]], axis=-1).reshape(d, NSA_REP * LANES)], axis=1)
    packed = jnp.concatenate([u, cq, z(W_CQ - MLA_Q_RANK), kr_a, kr_b, nq_p, nkv, ckv,
                              gate, z(LANES - 3 * NSA_HEADS)], axis=1)
    assert packed.shape[1] == IN_COLS_PACKED
    return packed.astype(BF16)


def _rg_order(a):
    rest = a.shape[1:]
    return a.reshape((NSA_KV_HEADS, NSA_REP, NSA_DIM) + rest).swapaxes(0, 1).reshape((-1,) + rest)


def kernel(x, mem, w_in, w_out, mix_norm, out_norm, ssm_a_re, ssm_a_im, ssm_b_re, ssm_b_im, ssm_c_re, ssm_c_im, ssm_d, ssm_log_dt, ssm_w_glu, mla_q_norm, mla_w_uq, mla_kv_norm, mla_w_ukv, nsa_cmp_pe, nsa_cmp_w1, nsa_cmp_w2, rel_bias, xattn_norm, mem_norm, xattn_wq, xattn_wkv, xattn_wo, ffn_norm, dense_w_gate, dense_w_up, dense_w_down, moe_router, moe_w_gate, moe_w_up, moe_w_down, final_norm):
    bsz, seq, d = x.shape
    depth = w_in.shape[0]
    T = bsz * seq
    nmem = mem.shape[1]
    tq_nsa, tk_nsa = 2 * LANES, 4 * LANES
    rope_tabs = _rope_tables(seq)
    nsa_tabs = _nsa_bias_tables(rel_bias, seq, tq_nsa)
    o1, o2 = SSM_WIDTH, SSM_WIDTH + MLA_HEADS * MLA_V
    mem2 = mem.reshape(bsz * nmem, d)
    h = x.reshape(T, d)
    for l in range(depth):
        proj = norm_matmul(h, mix_norm[l], _pack_w_in(w_in[l]), tm=1024, tn=IN_COLS_PACKED, out_dtype=BF16)
        proj3 = proj.reshape(bsz, seq, IN_COLS_PACKED)
        u_tm = proj3[:, :, C_U:C_U + SSM_WIDTH].transpose(1, 0, 2).reshape(seq * bsz, SSM_WIDTH)
        y_ssm = ssm_mixer(u_tm, ssm_a_re[l], ssm_a_im[l], ssm_b_re[l], ssm_b_im[l], ssm_c_re[l], ssm_c_im[l],
                          ssm_d[l], ssm_log_dt[l], ssm_w_glu[l], nb=bsz, tc=64)
        y_ssm = y_ssm.reshape(seq, bsz, SSM_WIDTH).transpose(1, 0, 2)
        y_mla = mla_mixer(proj3, mla_q_norm[l], mla_w_uq[l], mla_kv_norm[l], mla_w_ukv[l], rope_tabs,
                          tm=512, tq=512)
        nch = seq // CMP_STRIDE
        kv_cr = jnp.stack([proj3[:, :, C_NKV:C_NKV + LANES].reshape(bsz, nch, CMP_STRIDE * LANES),
                           proj3[:, :, C_NKV + LANES:C_NKV + 2 * LANES].reshape(bsz, nch, CMP_STRIDE * LANES)],
                          axis=1)
        kvc = nsa_compress(kv_cr, nsa_cmp_pe[l], nsa_cmp_w1[l], nsa_cmp_w2[l])
        y_nsa = nsa_mixer(proj3, kvc, nsa_tabs, tq=tq_nsa, tk=tk_nsa)
        g_out = out_norm[l]
        wo_l = w_out[l]
        kv_mem = norm_matmul(mem2, mem_norm[l], xattn_wkv[l].astype(BF16), tm=256, tn=512, out_dtype=BF16)
        h = mix_out_cross_attention(
            [y_ssm, y_mla, y_nsa], h.reshape(bsz, seq, d),
            [g_out[:o1], g_out[o1:o2], _rg_order(g_out[o2:])],
            [wo_l[:o1].astype(BF16), wo_l[o1:o2].astype(BF16), _rg_order(wo_l[o2:]).astype(BF16)],
            kv_mem.reshape(bsz, nmem, 2 * d), xattn_norm[l], xattn_wq[l], xattn_wo[l], tm=512).reshape(T, d)
        last = l == depth - 1
        if l % 2 == 0:
            h = dense_ffn(h, ffn_norm[l], dense_w_gate[l // 2], dense_w_up[l // 2], dense_w_down[l // 2],
                          tm=1024, tf=1408)
            if last:
                h = final_rmsnorm(h, final_norm, tm=512)
        else:
            h = moe_layer(h, ffn_norm[l], moe_router[l // 2], moe_w_gate[l // 2], moe_w_up[l // 2],
                          moe_w_down[l // 2], final_norm, final_norm=last)
    return h.reshape(bsz, seq, d)
```

```python
import functools
import math

import jax
import jax.numpy as jnp
from jax import lax
from jax.experimental import pallas as pl
from jax.experimental.pallas import tpu as pltpu

F32 = jnp.float32
BF16 = jnp.bfloat16

SSM_WIDTH = 256
SSM_CH = 16
SSM_GROUPS = 16
SSM_STATE = 64
MLA_HEADS = 6
MLA_NOPE = 64
MLA_ROPE = 32
MLA_V = 64
MLA_Q_RANK = 192
MLA_KV_RANK = 128
NSA_HEADS = 6
NSA_KV_HEADS = 2
NSA_REP = 3
NSA_DIM = 64
CMP_BLOCK = 32
CMP_STRIDE = 16
SEL_BLOCK = 64
SEL_TOPN = 8
WINDOW = 256
REL_BUCKETS = 32
REL_MAX_DIST = 128
XATTN_HEADS = 4
N_EXPERTS = 8
ROPE_THETA = 10000.0
EPS = 1e-6
NEG_INF = -1e30
FORCE = 1e9
LOG2E = math.log2(math.e)

LANES = 128
VMEM_LIMIT = 56 * 1024 * 1024

W_CQ = 2 * LANES
W_KR = 2 * LANES
W_NQ = NSA_HEADS * LANES
W_NKV = 6 * LANES
_WIDTHS = (SSM_WIDTH, W_CQ, W_KR, W_NQ, W_NKV, MLA_KV_RANK, LANES)
C_U, C_CQ, C_KR, C_NQ, C_NKV, C_CKV, C_GATE = (sum(_WIDTHS[:k]) for k in range(len(_WIDTHS)))
IN_COLS_PACKED = sum(_WIDTHS)
assert all(c % w == 0 for c, w in zip((C_U, C_CQ, C_KR, C_NQ, C_NKV, C_CKV, C_GATE), _WIDTHS))


def _cparams(sem):
    return pltpu.CompilerParams(dimension_semantics=sem, vmem_limit_bytes=VMEM_LIMIT)


def _dot(a, b):
    return jnp.dot(a, b, preferred_element_type=F32)


def _dot_t(a, b):
    return lax.dot_general(a, b, (((1,), (1,)), ((), ())), preferred_element_type=F32)


def _rms(x, g, n):
    ms = jnp.sum(x * x, axis=-1, keepdims=True) * (1.0 / n)
    return x * lax.rsqrt(ms + EPS) * g


def _sigmoid(x):
    return 1.0 / (1.0 + jnp.exp(-x))


def _silu(x):
    return x * _sigmoid(x)


def _swiglu_accumulate(x_ref, wg, wu, wd, acc_ref, parts=2):
    rows = x_ref.shape[0] // parts
    sl = [slice(k * rows, (k + 1) * rows) for k in range(parts)]
    x = [x_ref[s, :] for s in sl]
    g = [_dot(xk, wg) for xk in x]
    u = [_dot(xk, wu) for xk in x]
    a = [(_silu(gk) * uk).astype(BF16) for gk, uk in zip(g, u)]
    d = [_dot(ak, wd) for ak in a]
    for s, dk in zip(sl, d):
        acc_ref[s, :] += dk


def _norm_mm_kernel(x_ref, g_ref, w_ref, o_ref):
    rows = x_ref.shape[0] // 2
    sl = [slice(k * rows, (k + 1) * rows) for k in range(2)]
    xn = [_rms(x_ref[s, :].astype(F32), g_ref[...], x_ref.shape[-1]).astype(BF16) for s in sl]
    for s, xk in zip(sl, xn):
        o_ref[s, :] = _dot(xk, w_ref[...]).astype(o_ref.dtype)


def norm_matmul(x, g, w, *, tm, out_dtype):
    m, k = x.shape
    n = w.shape[1]
    return pl.pallas_call(
        _norm_mm_kernel,
        out_shape=jax.ShapeDtypeStruct((m, n), out_dtype),
        grid=(m // tm,),
        in_specs=[pl.BlockSpec((tm, k), lambda i: (i, 0)),
                  pl.BlockSpec((1, k), lambda i: (0, 0)),
                  pl.BlockSpec((k, n), lambda i: (0, 0))],
        out_specs=pl.BlockSpec((tm, n), lambda i: (i, 0)),
        compiler_params=_cparams(("parallel",)),
        name="norm_matmul",
    )(x, g.reshape(1, k).astype(F32), w)


def _ssm_kernel(u_ref, bbr_ref, bbi_ref, ar_ref, ai_ref, ccr_ref, cci_ref, d_ref, wglu_ref,
                o_ref, hr_ref, hi_ref, cr_ref, ci_ref, *, tc, nb):
    @pl.when(pl.program_id(0) == 0)
    def _():
        cr_ref[...] = jnp.zeros_like(cr_ref)
        ci_ref[...] = jnp.zeros_like(ci_ref)

    half = (tc * nb) // 2
    sl = [slice(k * half, (k + 1) * half) for k in range(2)]
    u = [u_ref[s, :] for s in sl]
    bu_r = [_dot(x, bbr_ref[...]) for x in u]
    bu_i = [_dot(x, bbi_ref[...]) for x in u]
    for s, r, i in zip(sl, bu_r, bu_i):
        hr_ref[s, :] = r
        hi_ref[s, :] = i
    gp = ar_ref.shape[-1]
    ar = jnp.broadcast_to(ar_ref[...], (nb, gp))
    ai = jnp.broadcast_to(ai_ref[...], (nb, gp))

    def step(t, carry):
        hr, hi = carry
        rows = pl.ds(pl.multiple_of(t * nb, nb), nb)
        nr = ar * hr - ai * hi + hr_ref[rows, :]
        ni = ar * hi + ai * hr + hi_ref[rows, :]
        hr_ref[rows, :] = nr
        hi_ref[rows, :] = ni
        return nr, ni

    hr, hi = lax.fori_loop(0, tc, step, (cr_ref[...], ci_ref[...]))
    cr_ref[...] = hr
    ci_ref[...] = hi
    y_r = [_dot(hr_ref[s, :].astype(BF16), ccr_ref[...]) for s in sl]
    y_i = [_dot(hi_ref[s, :].astype(BF16), cci_ref[...]) for s in sl]
    y = [jax.nn.gelu(a + b + d_ref[...] * x.astype(F32)) for a, b, x in zip(y_r, y_i, u)]
    z = [_dot(v.astype(BF16), wglu_ref[...]) for v in y]
    for s, v, w in zip(sl, y, z):
        o_ref[s, :] = (v * _sigmoid(w)).astype(o_ref.dtype)


def ssm_mixer(u_tm, a_re, a_im, b_re, b_im, c_re, c_im, d, log_dt, w_glu, *, nb, tc):
    rows = u_tm.shape[0]
    G, P, C = SSM_GROUPS, SSM_STATE, SSM_CH
    dt = jnp.exp(log_dt.astype(F32))[:, None]
    lr, li = a_re.astype(F32), a_im.astype(F32)
    mag = jnp.exp(lr * dt)
    ab_r, ab_i = mag * jnp.cos(li * dt), mag * jnp.sin(li * dt)
    den = lr * lr + li * li
    nr = ab_r - 1.0
    f_r = (nr * lr + ab_i * li) / den
    f_i = (ab_i * lr - nr * li) / den
    br, bi = b_re.astype(F32), b_im.astype(F32)
    bb_r = f_r[..., None] * br - f_i[..., None] * bi
    bb_i = f_r[..., None] * bi + f_i[..., None] * br
    eye = jnp.eye(G, dtype=F32)
    bbr = jnp.einsum('gpc,gh->gchp', bb_r, eye).reshape(G * C, G * P).astype(BF16)
    bbi = jnp.einsum('gpc,gh->gchp', bb_i, eye).reshape(G * C, G * P).astype(BF16)
    ccr = jnp.einsum('gcp,gh->gphc', c_re.astype(F32), eye).reshape(G * P, G * C).astype(BF16)
    cci = jnp.einsum('gcp,gh->gphc', -c_im.astype(F32), eye).reshape(G * P, G * C).astype(BF16)
    gp = G * P
    full = lambda shape: pl.BlockSpec(shape, lambda i: (0,) * len(shape))
    return pl.pallas_call(
        functools.partial(_ssm_kernel, tc=tc, nb=nb),
        out_shape=jax.ShapeDtypeStruct((rows, SSM_WIDTH), BF16),
        grid=(rows // (tc * nb),),
        in_specs=[pl.BlockSpec((tc * nb, SSM_WIDTH), lambda i: (i, 0)),
                  full((G * C, gp)), full((G * C, gp)), full((1, gp)), full((1, gp)),
                  full((gp, G * C)), full((gp, G * C)), full((1, SSM_WIDTH)),
                  full((SSM_WIDTH, SSM_WIDTH))],
        out_specs=pl.BlockSpec((tc * nb, SSM_WIDTH), lambda i: (i, 0)),
        scratch_shapes=[pltpu.VMEM((tc * nb, gp), F32), pltpu.VMEM((tc * nb, gp), F32),
                        pltpu.VMEM((nb, gp), F32), pltpu.VMEM((nb, gp), F32)],
        compiler_params=_cparams(("arbitrary",)),
        name="ssm_mixer",
    )(u_tm, bbr, bbi, ab_r.reshape(1, gp), ab_i.reshape(1, gp), ccr, cci,
      d.reshape(1, SSM_WIDTH).astype(F32), w_glu.astype(BF16))


def _mla_prep_kernel(cq_ref, kr_ref, ckv_ref, gq_ref, gkv_ref, wqa_ref, wqb_ref, wk_ref, wv_ref,
                     c1_ref, c0_ref, s0_ref, q_ref, k_ref, v_ref):
    qn = _rms(cq_ref[0].astype(F32), gq_ref[...], MLA_Q_RANK).astype(BF16)
    qa = _dot(qn, wqa_ref[...])
    qb = _dot(qn, wqb_ref[...])
    kn = _rms(ckv_ref[0].astype(F32), gkv_ref[...], MLA_KV_RANK).astype(BF16)
    ka = _dot(kn, wk_ref[...])
    va = _dot(kn, wv_ref[...])
    kr = kr_ref[0].astype(F32)
    c1, c0, s0 = c1_ref[...], c0_ref[...], s0_ref[...]
    krope = kr[:, :LANES] * c0 + kr[:, LANES:] * s0
    low_half = lax.broadcasted_iota(jnp.int32, (kr.shape[0], LANES), 1) < MLA_V
    for h in range(MLA_HEADS):
        sl = slice(h * LANES, (h + 1) * LANES)
        q_ref[0, h] = (qa[:, sl] * c1 + qb[:, sl] * s0).astype(BF16)
        k_ref[0, h] = (ka[:, sl] + krope).astype(BF16)
        ones = jnp.where(low_half if h % 2 == 0 else jnp.logical_not(low_half), 1.0, 0.0)
        v_ref[0, h] = jnp.concatenate([va[:, sl], ones], axis=1).astype(BF16)


def _mla_flash_kernel(q_ref, k_ref, v_ref, o_ref, m_ref, acc_ref, *, tq):
    qi = pl.program_id(1)
    m_ref[...] = jnp.full_like(m_ref, NEG_INF)
    acc_ref[...] = jnp.zeros_like(acc_ref)
    lane = lax.broadcasted_iota(jnp.int32, (tq, LANES), 1)
    rep = tq // LANES

    def tile(kt, masked):
        ks = pl.ds(pl.multiple_of(kt * tq, tq), tq)
        if masked:
            mask = (lax.broadcasted_iota(jnp.int32, (tq, tq), 1)
                    <= lax.broadcasted_iota(jnp.int32, (tq, tq), 0))
        hs = range(MLA_HEADS)
        s = [_dot_t(q_ref[0, h], k_ref[0, h, ks, :]) for h in hs]
        if masked:
            s = [jnp.where(mask, x, NEG_INF) for x in s]
        m_prev = [m_ref[h] for h in hs]
        m_new = [jnp.maximum(mp, jnp.max(x, axis=-1, keepdims=True)) for mp, x in zip(m_prev, s)]
        alpha = [jnp.exp2(mp - mn) for mp, mn in zip(m_prev, m_new)]
        p = [jnp.exp2(x - jnp.tile(mn, (1, rep))).astype(BF16) for x, mn in zip(s, m_new)]
        pv = [_dot(p[h], v_ref[0, h, ks, :]) for h in hs]
        for h in hs:
            m_ref[h] = m_new[h]
        for pr in range(MLA_HEADS // 2):
            a = jnp.tile(jnp.where(lane < MLA_V, alpha[2 * pr], alpha[2 * pr + 1]), (1, 2))
            acc_ref[pr] = acc_ref[pr] * a + pv[2 * pr] + pv[2 * pr + 1]

    def body(kt, c):
        tile(kt, False)
        return c

    lax.fori_loop(0, qi, body, 0)
    tile(qi, True)
    for pr in range(MLA_HEADS // 2):
        o_ref[0, :, pr * LANES:(pr + 1) * LANES] = (acc_ref[pr, :, :LANES] / acc_ref[pr, :, LANES:]).astype(o_ref.dtype)


def _rope_tables(seq):
    pos = jnp.arange(seq, dtype=F32)
    inv = 1.0 / (ROPE_THETA ** (jnp.arange(0, MLA_ROPE, 2, dtype=F32) / MLA_ROPE))
    ang = pos[:, None] * inv[None, :]
    cos, sin = jnp.cos(ang), jnp.sin(ang)
    cos2 = jnp.concatenate([cos, cos], axis=-1)
    sin2 = jnp.concatenate([sin, sin], axis=-1)
    z64 = jnp.zeros((seq, MLA_NOPE), F32)
    z32 = jnp.zeros((seq, LANES - MLA_NOPE - MLA_ROPE), F32)
    c1 = jnp.concatenate([jnp.ones((seq, MLA_NOPE), F32), cos2, z32], axis=-1)
    c0 = jnp.concatenate([z64, cos2, z32], axis=-1)
    s0 = jnp.concatenate([z64, sin2, z32], axis=-1)
    return c1, c0, s0


def _rot_half_cols(w):
    half = MLA_ROPE // 2
    return jnp.concatenate([-w[..., half:], w[..., :half]], axis=-1)


def mla_mixer(proj3, q_norm, w_uq, kv_norm, w_ukv, tabs, *, tm, tq):
    bsz, seq, _ = proj3.shape
    H = MLA_HEADS
    scale = (MLA_NOPE + MLA_ROPE) ** -0.5 * LOG2E
    wq =(w_uq.astype(F32) * scale).reshape(MLA_Q_RANK, H, MLA_NOPE + MLA_ROPE)
    zq = jnp.zeros((MLA_Q_RANK, H, LANES - MLA_NOPE - MLA_ROPE), F32)
    z64 = jnp.zeros((MLA_Q_RANK, H, MLA_NOPE), F32)
    wqa = jnp.concatenate([wq, zq], axis=-1).reshape(MLA_Q_RANK, H * LANES)
    wqb = jnp.concatenate([z64, _rot_half_cols(wq[..., MLA_NOPE:]), zq], axis=-1).reshape(MLA_Q_RANK, H * LANES)
    padq = ((0, W_CQ - MLA_Q_RANK), (0, 0))
    wqa = jnp.pad(wqa, padq).astype(BF16)
    wqb = jnp.pad(wqb, padq).astype(BF16)
    gq = jnp.pad(q_norm.astype(F32), (0, W_CQ - MLA_Q_RANK)).reshape(1, W_CQ)
    wkv = w_ukv.astype(F32).reshape(MLA_KV_RANK, H, MLA_NOPE + MLA_V)
    zk = jnp.zeros((MLA_KV_RANK, H, MLA_NOPE), F32)
    wk = jnp.concatenate([wkv[..., :MLA_NOPE], zk], axis=-1).reshape(MLA_KV_RANK, H * LANES).astype(BF16)
    wv_h = wkv[..., MLA_NOPE:]
    even = (jnp.arange(H) % 2 == 0)[None, :, None]
    wv = jnp.concatenate([jnp.where(even, wv_h, 0.0), jnp.where(even, 0.0, wv_h)], axis=-1)
    wv = wv.reshape(MLA_KV_RANK, H * LANES).astype(BF16)
    c1, c0, s0 = tabs
    full2 = lambda shape: pl.BlockSpec(shape, lambda b, i: (0,) * len(shape))
    tab_spec = pl.BlockSpec((tm, LANES), lambda b, i: (i, 0))
    hd_spec = pl.BlockSpec((1, H, tm, LANES), lambda b, i: (b, 0, i, 0))
    hd_shape = jax.ShapeDtypeStruct((bsz, H, seq, LANES), BF16)
    v_spec = pl.BlockSpec((1, H, tm, 2 * LANES), lambda b, i: (b, 0, i, 0))
    v_shape = jax.ShapeDtypeStruct((bsz, H, seq, 2 * LANES), BF16)
    q, k, v = pl.pallas_call(
        _mla_prep_kernel,
        out_shape=(hd_shape, hd_shape, v_shape),
        grid=(bsz, seq // tm),
        in_specs=[pl.BlockSpec((1, tm, W_CQ), lambda b, i: (b, i, C_CQ // W_CQ)),
                  pl.BlockSpec((1, tm, W_KR), lambda b, i: (b, i, C_KR // W_KR)),
                  pl.BlockSpec((1, tm, MLA_KV_RANK), lambda b, i: (b, i, C_CKV // MLA_KV_RANK)),
                  full2((1, W_CQ)), full2((1, MLA_KV_RANK)),
                  full2((W_CQ, H * LANES)), full2((W_CQ, H * LANES)),
                  full2((MLA_KV_RANK, H * LANES)), full2((MLA_KV_RANK, H * LANES)),
                  tab_spec, tab_spec, tab_spec],
        out_specs=(hd_spec, hd_spec, v_spec),
        compiler_params=_cparams(("parallel", "parallel")),
        name="mla_prep",
    )(proj3, proj3, proj3, gq, kv_norm.astype(F32).reshape(1, MLA_KV_RANK), wqa, wqb, wk, wv, c1, c0, s0)

    return pl.pallas_call(
        functools.partial(_mla_flash_kernel, tq=tq),
        out_shape=jax.ShapeDtypeStruct((bsz, seq, H * MLA_V), BF16),
        grid=(bsz, seq // tq),
        in_specs=[pl.BlockSpec((1, H, tq, LANES), lambda b, i: (b, 0, i, 0)),
                  pl.BlockSpec((1, H, seq, LANES), lambda b, i: (b, 0, 0, 0)),
                  pl.BlockSpec((1, H, seq, 2 * LANES), lambda b, i: (b, 0, 0, 0))],
        out_specs=pl.BlockSpec((1, tq, H * MLA_V), lambda b, i: (b, i, 0)),
        scratch_shapes=[pltpu.VMEM((H, tq, LANES), F32), pltpu.VMEM((H // 2, tq, 2 * LANES), F32)],
        compiler_params=_cparams(("parallel", "arbitrary")),
        name="mla_flash",
    )(q, k, v)


def _nsa_cmp_kernel(x_ref, pea_ref, peb_ref, w1a_ref, w1b_ref, w2_ref, o_ref):
    x = x_ref[0, 0]
    w1a, w1b = w1a_ref[0], w1b_ref[0]
    bias = _dot(pea_ref[0], w1a)[0:1] + _dot(peb_ref[0], w1b)[0:1]
    a = _dot(x, w1a)
    b = _dot(x, w1b)
    n = b.shape[0]
    pre = a + pltpu.roll(b, n - 1, 0) + bias
    o_ref[0, 0] = _dot(jax.nn.gelu(pre).astype(BF16), w2_ref[0]).astype(o_ref.dtype)


def nsa_compress(kv_cr, cmp_pe, cmp_w1, cmp_w2):
    bsz, _, nch, width = kv_cr.shape
    G, dh = NSA_KV_HEADS, NSA_DIM
    half = CMP_BLOCK // 2
    eye = jnp.eye(G, dtype=F32)
    w1r = cmp_w1.astype(F32).reshape(2, CMP_BLOCK, dh, dh)
    w1a = jnp.einsum('kpde,gh->kpgdhe', w1r[:, :half], eye).reshape(2, width, G * dh).astype(BF16)
    w1b = jnp.einsum('kpde,gh->kpgdhe', w1r[:, half:], eye).reshape(2, width, G * dh).astype(BF16)
    w2 = jnp.einsum('kde,gh->kgdhe', cmp_w2.astype(F32), eye).reshape(2, G * dh, G * dh).astype(BF16)
    pe = cmp_pe.astype(F32)
    pe_g = jnp.broadcast_to(pe[:, :, None, :], (2, CMP_BLOCK, G, dh))
    pea = jnp.broadcast_to(pe_g[:, :half].reshape(2, 1, width), (2, 8, width)).astype(BF16)
    peb = jnp.broadcast_to(pe_g[:, half:].reshape(2, 1, width), (2, 8, width)).astype(BF16)
    kvspec = lambda shape: pl.BlockSpec(shape, lambda b, k: (k,) + (0,) * (len(shape) - 1))
    return pl.pallas_call(
        _nsa_cmp_kernel,
        out_shape=jax.ShapeDtypeStruct((bsz, 2, nch, G * dh), BF16),
        grid=(bsz, 2),
        in_specs=[pl.BlockSpec((1, 1, nch, width), lambda b, k: (b, k, 0, 0)),
                  kvspec((1, 8, width)), kvspec((1, 8, width)),
                  kvspec((1, width, G * dh)), kvspec((1, width, G * dh)),
                  kvspec((1, G * dh, G * dh))],
        out_specs=pl.BlockSpec((1, 1, nch, G * dh), lambda b, k: (b, k, 0, 0)),
        compiler_params=_cparams(("parallel", "parallel")),
        name="nsa_compress",
    )(kv_cr, pea, peb, w1a, w1b, w2)


def _nsa_kernel(q_ref, gate_ref, ksl_ref, vsl_ref, kwn_ref, vwn_ref, kvc_ref, bc_ref, bw_ref,
                bs_ref, ov_ref, blk1h_ref, gexp_ref, o_ref, kaug_ref, vaug_ref, m_ref, acc_ref,
                *, tq, tk, nbs, n_sel):
    qi = pl.program_id(1)
    R, G = NSA_REP, NSA_KV_HEADS
    H = R * G
    nsub = tk // tq
    nwin = WINDOW // tq

    @pl.when(qi == 0)
    def _():
        kaug_ref[:, :LANES] = ksl_ref[0]
        kaug_ref[:, LANES:] = blk1h_ref[...]
        vaug_ref[:, :LANES] = vsl_ref[0]
        vaug_ref[:, LANES:] = jnp.ones((vaug_ref.shape[0], LANES), BF16)

    lane = lax.broadcasted_iota(jnp.int32, (tq, LANES), 1)
    t_row = qi * tq + lax.broadcasted_iota(jnp.int32, (H * tq, 1), 0) % tq
    kc = kvc_ref[0, 0]
    vc = kvc_ref[0, 1]
    ov = ov_ref[...]

    def stack(fn):
        return jnp.concatenate([fn(h) for h in range(H)], axis=0)

    q_all = stack(lambda h: q_ref[0, :, h * LANES:(h + 1) * LANES])

    c = jnp.minimum(qi, nwin)
    ws = pl.ds(pl.multiple_of(jnp.maximum(qi - nwin, 0) * tq, tq), (nwin + 1) * tq)
    k_w = kwn_ref[0, ws, :]
    v_w = vwn_ref[0, ws, :]
    o_w = []
    for g in range(G):
        rows = slice(g * R * tq, (g + 1) * R * tq)
        s_w = _dot_t(q_all[rows], k_w) + jnp.concatenate([bw_ref[g * R + r, c] for r in range(R)], axis=0)
        p_w = jnp.exp2(s_w - jnp.max(s_w, axis=-1, keepdims=True))
        o_w.append(_dot(p_w.astype(BF16), v_w) / jnp.sum(p_w, axis=-1, keepdims=True))
    o_w = jnp.concatenate(o_w, axis=0)

    halves = [slice(g * R * tq, (g + 1) * R * tq) for g in range(G)]
    valid = [t_row[hs] >= (CMP_BLOCK - 1) for hs in halves]
    s = [_dot_t(q_all[hs], kc) + jnp.concatenate([bc_ref[g * R + r] for r in range(R)], axis=0)
         for g, hs in enumerate(halves)]
    p = [jnp.where(v, jnp.exp2(x - jnp.max(x, axis=-1, keepdims=True)), 0.0) for v, x in zip(valid, s)]
    pc = [x / jnp.where(v, jnp.sum(x, axis=-1, keepdims=True), 1.0) for v, x in zip(valid, p)]
    o_c = jnp.concatenate([_dot(x.astype(BF16), vc) for x in pc], axis=0)
    blk = lax.broadcasted_iota(jnp.int32, (nbs, tq), 0)
    tl = qi * tq + lax.broadcasted_iota(jnp.int32, (nbs, tq), 1)
    cur = tl // SEL_BLOCK
    forced = (blk == 0) | (blk == cur) | (blk == cur - 1)
    future = blk * SEL_BLOCK > tl
    qmask = []
    for g in range(G):
        psum = pc[g][0:tq] + pc[g][tq:2 * tq] + pc[g][2 * tq:3 * tq]
        p_hi = psum.astype(BF16)
        p_lo = (psum - p_hi.astype(F32)).astype(BF16)
        imp = (_dot_t(ov, p_hi) + _dot_t(ov, p_lo))[:nbs]
        imp = jnp.where(forced, FORCE, jnp.where(future, -FORCE, imp))
        rank = jnp.zeros((nbs, tq), F32)
        for i in range(nbs):
            ri = imp[i:i + 1, :]
            beats = (ri > imp) | ((ri == imp) & (blk > i))
            rank = rank + jnp.where(beats, 1.0, 0.0)
        sel = jnp.where(rank < n_sel, 0.0, NEG_INF)
        sel = jnp.concatenate([sel, jnp.zeros((LANES - nbs, tq), F32)], axis=0).T.astype(BF16)
        qmask += [sel] * R
    q_aug = jnp.concatenate([q_all, jnp.concatenate(qmask, axis=0)], axis=1)

    m_ref[...] = jnp.full_like(m_ref, NEG_INF)
    acc_ref[...] = jnp.zeros_like(acc_ref)

    def sel_tile(kt, bias):
        ks = pl.ds(pl.multiple_of(kt * tk, tk), tk)
        k_t = kaug_ref[ks, :]
        v_t = vaug_ref[ks, :]
        s = [_dot_t(q_aug[hs], k_t) for hs in halves]
        if bias is not None:
            s = [x + bias[hs] for x, hs in zip(s, halves)]
        m_prev = [m_ref[hs, :] for hs in halves]
        m_new = [jnp.maximum(mp, jnp.max(x, axis=-1, keepdims=True)) for mp, x in zip(m_prev, s)]
        alpha = [jnp.exp2(mp - mn) for mp, mn in zip(m_prev, m_new)]
        p = [jnp.exp2(x - jnp.tile(mn, (1, tk // LANES))).astype(BF16) for x, mn in zip(s, m_new)]
        pv = [_dot(pp, v_t) for pp in p]
        for hs, mn, a, o in zip(halves, m_new, alpha, pv):
            m_ref[hs, :] = mn
            acc_ref[hs, :] = acc_ref[hs, :] * jnp.tile(a, (1, 2)) + o

    def near_bias(kt):
        cols = []
        for sub in range(nsub):
            d = qi - (kt * nsub + sub)
            cols.append(stack(lambda h: jnp.where(d == 0, bs_ref[h, 0], jnp.where(
                d == 1, bs_ref[h, 1], jnp.where(d < 0, NEG_INF, 0.0)))))
        return jnp.concatenate(cols, axis=1)

    def far_body(kt, c):
        sel_tile(kt, None)
        return c

    kd = (qi * tq) // tk
    lax.fori_loop(0, jnp.maximum(kd - 1, 0), far_body, 0)

    @pl.when(kd >= 1)
    def _():
        sel_tile(kd - 1, near_bias(kd - 1))

    sel_tile(kd, near_bias(kd))
    o_s = acc_ref[:, :LANES] / acc_ref[:, LANES:]

    gates = _sigmoid(_dot(gate_ref[0], gexp_ref[...]))
    for r in range(R):
        res = None
        for b, o_b in enumerate((o_c, o_s, o_w)):
            o_br = jnp.where(lane < NSA_DIM, o_b[r * tq:(r + 1) * tq], o_b[(R + r) * tq:(R + r + 1) * tq])
            term = gates[:, (b * R + r) * LANES:(b * R + r + 1) * LANES] * o_br
            res = term if res is None else res + term
        o_ref[0, :, r * LANES:(r + 1) * LANES] = res.astype(o_ref.dtype)


def _t5_bucket(dist):
    n = jnp.maximum(dist, 0)
    exact = REL_BUCKETS // 2
    nf = jnp.maximum(n, exact).astype(F32)
    large = exact + jnp.floor(jnp.log(nf / exact) / math.log(REL_MAX_DIST / exact)
                              * (REL_BUCKETS - exact)).astype(jnp.int32)
    return jnp.where(n < exact, n, jnp.minimum(large, REL_BUCKETS - 1))


def _nsa_bias_tables(rel_bias, seq, tq):
    rb = rel_bias.astype(F32).T
    far = rb[:, REL_BUCKETS - 1].reshape(NSA_HEADS, 1, 1)

    def by_dist(dist, ok, shift=0.0):
        bucket = _t5_bucket(dist)[None]
        out = jnp.zeros((NSA_HEADS,) + dist.shape, F32)
        for k in range(REL_BUCKETS):
            out = jnp.where(bucket == k, rb[:, k].reshape((NSA_HEADS,) + (1,) * dist.ndim), out)
        return jnp.where(ok[None], (out - shift) * LOG2E, NEG_INF)

    i = jnp.arange(tq)[:, None]
    t = jnp.arange(seq)[:, None]
    dist_c = t - (jnp.arange(LANES)[None, :] * CMP_STRIDE + CMP_BLOCK - 1)
    bc = by_dist(dist_c, dist_c >= 0)
    nwin = WINDOW // tq
    jw = jnp.arange((nwin + 1) * tq)[None, :]
    bw = jnp.stack([by_dist(tq * c + i - jw, (tq * c + i - jw >= 0) & (tq * c + i - jw < WINDOW))
                    for c in range(nwin + 1)], axis=1)
    js = jnp.arange(tq)[None, :]
    bs = jnp.stack([by_dist(tq * c + i - js, tq * c + i - js >= 0, far) for c in range(2)], axis=1)
    ci = jnp.arange(LANES)[:, None]
    sj = jnp.arange(LANES)[None, :]
    nbs = seq // SEL_BLOCK
    ov = ((ci * CMP_STRIDE <= sj * SEL_BLOCK + SEL_BLOCK - 1)
          & (ci * CMP_STRIDE + CMP_BLOCK - 1 >= sj * SEL_BLOCK)
          & (ci < seq // CMP_STRIDE - 1) & (sj < nbs))
    blk1h = (jnp.arange(seq)[:, None] // SEL_BLOCK == sj).astype(BF16)
    col = jnp.arange(3 * NSA_REP * LANES)
    slab, lane_g = col // LANES, (col % LANES) // NSA_DIM
    gate_col = (slab // NSA_REP) * NSA_HEADS + lane_g * NSA_REP + slab % NSA_REP
    gexp = (ci == gate_col[None, :]).astype(BF16)
    return bc, bw, bs, ov.T.astype(BF16), blk1h, gexp


def nsa_mixer(proj3, kvc, tables, *, tq, tk):
    bsz, seq, _ = proj3.shape
    assert tq % LANES == 0 and tq >= REL_MAX_DIST and tk % tq == 0 and seq % tk == 0
    assert seq // CMP_STRIDE == LANES and WINDOW % tq == 0
    nwin = WINDOW // tq
    bc, bw, bs, ov, blk1h, gexp = tables
    nbs = seq // SEL_BLOCK
    H = NSA_HEADS
    slab = lambda j: pl.BlockSpec((1, seq, LANES), lambda b, i: (b, 0, C_NKV // LANES + j))
    const = lambda shape: pl.BlockSpec(shape, lambda b, i: (0,) * len(shape))
    return pl.pallas_call(
        functools.partial(_nsa_kernel, tq=tq, tk=tk, nbs=nbs, n_sel=min(SEL_TOPN, nbs)),
        out_shape=jax.ShapeDtypeStruct((bsz, seq, H * NSA_DIM), BF16),
        grid=(bsz, seq // tq),
        in_specs=[pl.BlockSpec((1, tq, H * LANES), lambda b, i: (b, i, C_NQ // (H * LANES))),
                  pl.BlockSpec((1, tq, LANES), lambda b, i: (b, i, C_GATE // LANES)),
                  slab(2), slab(3), slab(4), slab(5),
                  pl.BlockSpec((1, 2, LANES, LANES), lambda b, i: (b, 0, 0, 0)),
                  pl.BlockSpec((H, tq, LANES), lambda b, i: (0, i, 0)),
                  const((H, nwin + 1, tq, (nwin + 1) * tq)), const((H, 2, tq, tq)),
                  const((LANES, LANES)), const((seq, LANES)), const(gexp.shape)],
        out_specs=pl.BlockSpec((1, tq, H * NSA_DIM), lambda b, i: (b, i, 0)),
        scratch_shapes=[pltpu.VMEM((seq, 2 * LANES), BF16), pltpu.VMEM((seq, 2 * LANES), BF16),
                        pltpu.VMEM((H * tq, LANES), F32), pltpu.VMEM((H * tq, 2 * LANES), F32)],
        compiler_params=_cparams(("parallel", "arbitrary")),
        name="nsa_attention",
    )(proj3, proj3, proj3, proj3, proj3, proj3, kvc, bc, bw, bs, ov, blk1h, gexp)


def _mix_xattn_kernel(ys_ref, ym_ref, yn_ref, h_ref, g1_ref, g2_ref, g3_ref, w1_ref, w2_ref, w3_ref,
                      kv_ref, g_ref, wq_ref, wo_ref, o_ref, *, dh):
    rows = h_ref.shape[1] // 2
    sl = [slice(k * rows, (k + 1) * rows) for k in range(2)]
    hw = XATTN_HEADS * dh
    heads = range(XATTN_HEADS)

    def normed(y_ref, gy_ref):
        return [_rms(y_ref[0, s, :].astype(F32), gy_ref[...], y_ref.shape[-1]).astype(BF16) for s in sl]

    ns, nm, nn = normed(ys_ref, g1_ref), normed(ym_ref, g2_ref), normed(yn_ref, g3_ref)
    h = [h_ref[0, s, :] + _dot(a, w1_ref[...]) + _dot(b, w2_ref[...]) + _dot(c, w3_ref[...])
         for s, a, b, c in zip(sl, ns, nm, nn)]
    xn = [_rms(x, g_ref[...], x.shape[-1]).astype(BF16) for x in h]
    q = [_dot(x, wq_ref[...]).astype(BF16) for x in xn]
    s = [[_dot_t(qk[:, hd * dh:(hd + 1) * dh], kv_ref[0, :, hd * dh:(hd + 1) * dh]) for hd in heads] for qk in q]
    p = [[jnp.exp2(x - jnp.max(x, axis=-1, keepdims=True)) for x in sk] for sk in s]
    p = [[(x / jnp.sum(x, axis=-1, keepdims=True)).astype(BF16) for x in pk] for pk in p]
    o = [jnp.concatenate([_dot(pk[hd], kv_ref[0, :, hw + hd * dh:hw + (hd + 1) * dh]).astype(BF16)
                          for hd in heads], axis=-1) for pk in p]
    for sk, hk, ok in zip(sl, h, o):
        o_ref[0, sk, :] = hk + _dot(ok, wo_ref[...])


def mix_out_cross_attention(ys3, h3, gains, weights, kv3, g_x, wq, wo, *, tm):
    bsz, seq, d = h3.shape
    m = kv3.shape[1]
    dh = d // XATTN_HEADS
    wq_s = (wq.astype(F32) * (dh ** -0.5 * LOG2E)).astype(BF16)
    gains = [g.reshape(1, -1).astype(F32) for g in gains]
    const = lambda shape: pl.BlockSpec(shape, lambda b, i: (0,) * len(shape))
    row = lambda w: pl.BlockSpec((1, tm, w), lambda b, i: (b, i, 0))
    return pl.pallas_call(
        functools.partial(_mix_xattn_kernel, dh=dh),
        out_shape=jax.ShapeDtypeStruct((bsz, seq, d), F32),
        grid=(bsz, seq // tm),
        in_specs=[row(y.shape[2]) for y in ys3] + [row(d)]
                 + [const(g.shape) for g in gains] + [const(w.shape) for w in weights]
                 + [pl.BlockSpec((1, m, 2 * d), lambda b, i: (b, 0, 0)),
                    const((1, d)), const((d, d)), const((d, d))],
        out_specs=row(d),
        compiler_params=_cparams(("parallel", "parallel")),
        name="mix_out_cross_attention",
    )(*ys3, h3, *gains, *weights, kv3, g_x.reshape(1, d).astype(F32), wq_s, wo.astype(BF16))


def _ffn_kernel(h_ref, g_ref, wg_ref, wu_ref, wd_ref, o_ref, xn_ref, acc_ref):
    j = pl.program_id(1)

    @pl.when(j == 0)
    def _():
        h = h_ref[...]
        xn_ref[...] = _rms(h, g_ref[...], h.shape[-1]).astype(BF16)
        acc_ref[...] = h

    _swiglu_accumulate(xn_ref, wg_ref[...], wu_ref[...], wd_ref[...], acc_ref)

    @pl.when(j == pl.num_programs(1) - 1)
    def _():
        o_ref[...] = acc_ref[...]


def dense_ffn(h, g, wg, wu, wd, *, tm, tf):
    m, d = h.shape
    ff = wg.shape[1]
    return pl.pallas_call(
        _ffn_kernel,
        out_shape=jax.ShapeDtypeStruct((m, d), F32),
        grid=(m // tm, ff // tf),
        in_specs=[pl.BlockSpec((tm, d), lambda i, j: (i, 0)),
                  pl.BlockSpec((1, d), lambda i, j: (0, 0)),
                  pl.BlockSpec((d, tf), lambda i, j: (0, j)),
                  pl.BlockSpec((d, tf), lambda i, j: (0, j)),
                  pl.BlockSpec((tf, d), lambda i, j: (j, 0))],
        out_specs=pl.BlockSpec((tm, d), lambda i, j: (i, 0)),
        scratch_shapes=[pltpu.VMEM((tm, d), BF16), pltpu.VMEM((tm, d), F32)],
        compiler_params=_cparams(("parallel", "arbitrary")),
        name="dense_ffn",
    )(h, g.reshape(1, d).astype(F32), wg.astype(BF16), wu.astype(BF16), wd.astype(BF16))


def _router_kernel(h_ref, g_ref, wr_hi_ref, wr_lo_ref, xn_ref, info_ref, cnt_ref, carry_ref, *, tm):
    i = pl.program_id(0)

    @pl.when(i == 0)
    def _():
        carry_ref[...] = jnp.zeros_like(carry_ref)

    h = h_ref[...]
    xn = _rms(h, g_ref[...], h.shape[-1])
    xn_ref[...] = xn
    x_hi = xn.astype(BF16)
    x_lo = (xn - x_hi.astype(F32)).astype(BF16)
    logits = _dot(x_hi, wr_hi_ref[...]) + _dot(x_lo, wr_hi_ref[...]) + _dot(x_hi, wr_lo_ref[...])
    lane = lax.broadcasted_iota(jnp.int32, (tm, LANES), 1)
    lanef = lane.astype(F32)
    logits = jnp.where(lane < N_EXPERTS, logits, NEG_INF)
    m1 = jnp.max(logits, axis=-1, keepdims=True)
    i1 = jnp.min(jnp.where(logits == m1, lanef, float(LANES)), axis=-1, keepdims=True)
    rest = jnp.where(lanef == i1, NEG_INF, logits)
    m2 = jnp.max(rest, axis=-1, keepdims=True)
    i2 = jnp.min(jnp.where(rest == m2, lanef, float(LANES)), axis=-1, keepdims=True)
    e2 = jnp.exp(m2 - m1)
    w1 = 1.0 / (1.0 + e2)
    w2 = e2 / (1.0 + e2)
    oh1 = lanef == i1
    oh2 = lanef == i2
    oh = jnp.where(oh1 | oh2, 1.0, 0.0)
    rr = lax.broadcasted_iota(jnp.int32, (tm, tm), 0)
    cc = lax.broadcasted_iota(jnp.int32, (tm, tm), 1)
    tri = jnp.where(cc < rr, 1.0, 0.0).astype(BF16)
    before = _dot(tri, oh.astype(BF16)) + carry_ref[0:1, :]
    r1 = jnp.sum(jnp.where(oh1, before, 0.0), axis=-1, keepdims=True)
    r2 = jnp.sum(jnp.where(oh2, before, 0.0), axis=-1, keepdims=True)
    carry_ref[...] = carry_ref[...] + jnp.sum(oh, axis=0, keepdims=True)
    info = jnp.where(lane == 0, i1, jnp.where(lane == 1, i2, jnp.where(lane == 2, w1, jnp.where(
        lane == 3, w2, jnp.where(lane == 4, r1, jnp.where(lane == 5, r2, 0.0))))))
    info_ref[...] = info
    cnt_ref[...] = carry_ref[...]


def moe_router(h, g, router, *, tm):
    m, d = h.shape
    wr = jnp.pad(router.astype(F32), ((0, 0), (0, LANES - N_EXPERTS)))
    wr_hi = wr.astype(BF16)
    wr_lo = (wr - wr_hi.astype(F32)).astype(BF16)
    return pl.pallas_call(
        functools.partial(_router_kernel, tm=tm),
        out_shape=(jax.ShapeDtypeStruct((m, d), F32), jax.ShapeDtypeStruct((m, LANES), F32),
                   jax.ShapeDtypeStruct((8, LANES), F32)),
        grid=(m // tm,),
        in_specs=[pl.BlockSpec((tm, d), lambda i: (i, 0)),
                  pl.BlockSpec((1, d), lambda i: (0, 0)),
                  pl.BlockSpec((d, LANES), lambda i: (0, 0)),
                  pl.BlockSpec((d, LANES), lambda i: (0, 0))],
        out_specs=(pl.BlockSpec((tm, d), lambda i: (i, 0)),
                   pl.BlockSpec((tm, LANES), lambda i: (i, 0)),
                   pl.BlockSpec((8, LANES), lambda i: (0, 0))),
        scratch_shapes=[pltpu.VMEM((8, LANES), F32)],
        compiler_params=_cparams(("arbitrary",)),
        name="moe_router",
    )(h, g.reshape(1, d).astype(F32), wr_hi, wr_lo)


def _row_copy(src_hbm, row, dst, slot, sem):
    return pltpu.make_async_copy(src_hbm.at[pl.ds(row, 1), :], dst.at[pl.ds(slot, 1), :], sem)


def _rows_wait(src_hbm, dst, sem):
    pltpu.make_async_copy(src_hbm.at[pl.ds(0, dst.shape[0]), :], dst, sem).wait()


def _moe_ffn_kernel(src_ref, texp_ref, nact_ref, x_hbm, wg_ref, wu_ref, wd_ref, o_ref,
                    xbuf, xbf, acc_ref, sem, *, tm, nj):
    i = pl.program_id(0)
    j = pl.program_id(1)
    nact = nact_ref[0]
    active = i < nact
    nbuf = xbuf.shape[0]
    ahead = nbuf - 1
    cur = i % nbuf
    rows_per_step = tm // nj

    for t0 in range(ahead):
        @pl.when((i == 0) & (j == 0) & (t0 < nact))
        def _():
            def issue(s, c):
                _row_copy(x_hbm, src_ref[t0 * tm + s], xbuf.at[t0], s, sem.at[t0]).start()
                return c

            lax.fori_loop(0, tm, issue, 0, unroll=8)

    @pl.when(active & (j == 0))
    def _():
        _rows_wait(x_hbm, xbuf.at[cur], sem.at[cur])
        xbf[...] = xbuf[cur].astype(BF16)
        acc_ref[...] = jnp.zeros_like(acc_ref)

    def compute(prefetch):
        if prefetch:
            nxt = (i + ahead) % nbuf
            base = (i + ahead) * tm + j * rows_per_step
            for k in range(rows_per_step):
                _row_copy(x_hbm, src_ref[base + k], xbuf.at[nxt], j * rows_per_step + k, sem.at[nxt]).start()
        _swiglu_accumulate(xbf, wg_ref[0], wu_ref[0], wd_ref[0], acc_ref)

    @pl.when(i + ahead < nact)
    def _():
        compute(True)

    @pl.when(active & (i + ahead >= nact))
    def _():
        compute(False)

    @pl.when(j == nj - 1)
    def _():
        o_ref[...] = jnp.where(active, acc_ref[...], 0.0)


def moe_expert_ffn(xn, src, tile_expert, n_active, wg, wu, wd, *, tm, tf):
    n_slots = src.shape[0]
    d = xn.shape[1]
    ne, _, ff = wg.shape
    nj = ff // tf
    assert nj * tf == ff and tm % nj == 0
    wg, wu, wd = wg.astype(BF16), wu.astype(BF16), wd.astype(BF16)

    def wmap_col(i, j, src, texp, nact):
        return (texp[i], 0, jnp.where(i < nact[0], j, nj - 1))

    def wmap_row(i, j, src, texp, nact):
        return (texp[i], jnp.where(i < nact[0], j, nj - 1), 0)

    return pl.pallas_call(
        functools.partial(_moe_ffn_kernel, tm=tm, nj=nj),
        out_shape=jax.ShapeDtypeStruct((n_slots, d), F32),
        grid_spec=pltpu.PrefetchScalarGridSpec(
            num_scalar_prefetch=3,
            grid=(n_slots // tm, nj),
            in_specs=[pl.BlockSpec(memory_space=pl.ANY),
                      pl.BlockSpec((1, d, tf), wmap_col),
                      pl.BlockSpec((1, d, tf), wmap_col),
                      pl.BlockSpec((1, tf, d), wmap_row)],
            out_specs=pl.BlockSpec((tm, d), lambda i, j, *_: (i, 0)),
            scratch_shapes=[pltpu.VMEM((3, tm, d), F32), pltpu.VMEM((tm, d), BF16),
                            pltpu.VMEM((tm, d), F32), pltpu.SemaphoreType.DMA((3,))]),
        compiler_params=_cparams(("arbitrary", "arbitrary")),
        name="moe_expert_ffn",
    )(src, tile_expert, n_active, xn, wg, wu, wd)


def _moe_combine_kernel(pos_ref, h_ref, info_ref, ys_hbm, g_ref, o_ref, buf, sem, *, tm, final_norm):
    i = pl.program_id(0)
    n = pl.num_programs(0)
    cur = i % 2

    def start_gather(tile, b):
        for s in range(tm):
            for k in range(2):
                _row_copy(ys_hbm, pos_ref[2 * (tile * tm + s) + k], buf.at[b, k], s, sem.at[b]).start()

    @pl.when(i == 0)
    def _():
        start_gather(0, 0)

    @pl.when(i + 1 < n)
    def _():
        start_gather(i + 1, 1 - cur)

    for k in range(2):
        _rows_wait(ys_hbm, buf.at[cur, k], sem.at[cur])
    info = info_ref[...]
    y = h_ref[...] + info[:, 2:3] * buf[cur, 0] + info[:, 3:4] * buf[cur, 1]
    if final_norm:
        y = _rms(y, g_ref[...], y.shape[-1])
    o_ref[...] = y


def moe_combine(h, info, ys, pos_flat, g_final, *, tm, final_norm):
    m, d = h.shape
    return pl.pallas_call(
        functools.partial(_moe_combine_kernel, tm=tm, final_norm=final_norm),
        out_shape=jax.ShapeDtypeStruct((m, d), F32),
        grid_spec=pltpu.PrefetchScalarGridSpec(
            num_scalar_prefetch=1,
            grid=(m // tm,),
            in_specs=[pl.BlockSpec((tm, d), lambda i, *_: (i, 0)),
                      pl.BlockSpec((tm, LANES), lambda i, *_: (i, 0)),
                      pl.BlockSpec(memory_space=pl.ANY),
                      pl.BlockSpec((1, d), lambda i, *_: (0, 0))],
            out_specs=pl.BlockSpec((tm, d), lambda i, *_: (i, 0)),
            scratch_shapes=[pltpu.VMEM((2, 2, tm, d), F32), pltpu.SemaphoreType.DMA((2,))]),
        compiler_params=_cparams(("arbitrary",)),
        name="moe_combine",
    )(pos_flat, h, info, ys, g_final.reshape(1, d).astype(F32))


def moe_layer(h, g, router, wg, wu, wd, g_final, *, final_norm, tm_r=512, tm_g=512, tf=1792, tm_c=512):
    m, d = h.shape
    xn, info, cnt = moe_router(h, g, router, tm=tm_r)
    e_idx = info[:, 0:2].astype(jnp.int32)
    rank = info[:, 4:6].astype(jnp.int32)
    counts = cnt[0, :N_EXPERTS].astype(jnp.int32)
    tiles_per = (counts + tm_g - 1) // tm_g
    tile_end = jnp.cumsum(tiles_per)
    seg_start = (tile_end - tiles_per) * tm_g
    pos = rank
    for e in range(N_EXPERTS):
        pos = pos + jnp.where(e_idx == e, seg_start[e], 0)
    n_tiles = (2 * m) // tm_g + N_EXPERTS
    n_slots = n_tiles * tm_g
    tok = jnp.broadcast_to(jnp.arange(m, dtype=jnp.int32)[:, None], (m, 2))
    src = jnp.zeros((n_slots,), jnp.int32).at[pos.reshape(-1)].set(tok.reshape(-1))
    n_active = tile_end[-1:].astype(jnp.int32)
    tile_ids = jnp.minimum(jnp.arange(n_tiles, dtype=jnp.int32), n_active[0] - 1)
    tile_expert = jnp.sum(tile_ids[:, None] >= tile_end[None, :], axis=1).astype(jnp.int32)
    ys = moe_expert_ffn(xn, src, tile_expert, n_active,
                        wg, wu, wd, tm=tm_g, tf=tf)
    return moe_combine(h, info, ys, pos.reshape(-1).astype(jnp.int32), g_final, tm=tm_c, final_norm=final_norm)


def _final_norm_kernel(h_ref, g_ref, o_ref):
    h = h_ref[...]
    o_ref[...] = _rms(h, g_ref[...], h.shape[-1])


def final_rmsnorm(h, g, *, tm):
    m, d = h.shape
    return pl.pallas_call(
        _final_norm_kernel,
        out_shape=jax.ShapeDtypeStruct((m, d), F32),
        grid=(m // tm,),
        in_specs=[pl.BlockSpec((tm, d), lambda i: (i, 0)), pl.BlockSpec((1, d), lambda i: (0, 0))],
        out_specs=pl.BlockSpec((tm, d), lambda i: (i, 0)),
        compiler_params=_cparams(("parallel",)),
        name="final_rmsnorm",
    )(h, g.reshape(1, d).astype(F32))


def _pack_w_in(w):
    d = w.shape[0]
    w = w.astype(F32)
    o = 0
    u = w[:, o:o + SSM_WIDTH]; o += SSM_WIDTH
    cq = w[:, o:o + MLA_Q_RANK]; o += MLA_Q_RANK
    ckv = w[:, o:o + MLA_KV_RANK]; o += MLA_KV_RANK
    kr = w[:, o:o + MLA_ROPE]; o += MLA_ROPE
    nq = w[:, o:o + NSA_HEADS * NSA_DIM]; o += NSA_HEADS * NSA_DIM
    nkv = w[:, o:o + 6 * NSA_KV_HEADS * NSA_DIM]; o += 6 * NSA_KV_HEADS * NSA_DIM
    gate = w[:, o:o + 3 * NSA_HEADS]
    z = lambda n: jnp.zeros((d, n), F32)
    kr_a = jnp.concatenate([z(MLA_NOPE), kr, z(LANES - MLA_NOPE - MLA_ROPE)], axis=1)
    kr_b = jnp.concatenate([z(MLA_NOPE), _rot_half_cols(kr), z(LANES - MLA_NOPE - MLA_ROPE)], axis=1)
    nq_h = (nq * (NSA_DIM ** -0.5 * LOG2E)).reshape(d, NSA_KV_HEADS, NSA_REP, NSA_DIM)
    zq = jnp.zeros((d, NSA_REP, NSA_DIM), F32)
    nq_p = jnp.concatenate([
        jnp.concatenate([nq_h[:, 0], zq], axis=-1).reshape(d, NSA_REP * LANES),
        jnp.concatenate([zq, nq_h[:, 1]], axis=-1).reshape(d, NSA_REP * LANES)], axis=1)
    packed = jnp.concatenate([u, cq, z(W_CQ - MLA_Q_RANK), kr_a, kr_b, nq_p, nkv, ckv,
                              gate, z(LANES - 3 * NSA_HEADS)], axis=1)
    assert packed.shape[1] == IN_COLS_PACKED
    return packed.astype(BF16)


def _rg_order(a):
    rest = a.shape[1:]
    return a.reshape((NSA_KV_HEADS, NSA_REP, NSA_DIM) + rest).swapaxes(0, 1).reshape((-1,) + rest)


def kernel(x, mem, w_in, w_out, mix_norm, out_norm, ssm_a_re, ssm_a_im, ssm_b_re, ssm_b_im, ssm_c_re, ssm_c_im, ssm_d, ssm_log_dt, ssm_w_glu, mla_q_norm, mla_w_uq, mla_kv_norm, mla_w_ukv, nsa_cmp_pe, nsa_cmp_w1, nsa_cmp_w2, rel_bias, xattn_norm, mem_norm, xattn_wq, xattn_wkv, xattn_wo, ffn_norm, dense_w_gate, dense_w_up, dense_w_down, moe_router, moe_w_gate, moe_w_up, moe_w_down, final_norm):
    bsz, seq, d = x.shape
    depth = w_in.shape[0]
    T = bsz * seq
    nmem = mem.shape[1]
    tq_nsa, tk_nsa = 2 * LANES, 4 * LANES
    rope_tabs = _rope_tables(seq)
    nsa_tabs = _nsa_bias_tables(rel_bias, seq, tq_nsa)
    o1, o2 = SSM_WIDTH, SSM_WIDTH + MLA_HEADS * MLA_V
    mem2 = mem.reshape(bsz * nmem, d)
    h = x.reshape(T, d)
    for l in range(depth):
        proj = norm_matmul(h, mix_norm[l], _pack_w_in(w_in[l]), tm=1024, out_dtype=BF16)
        proj3 = proj.reshape(bsz, seq, IN_COLS_PACKED)
        u_tm = proj3[:, :, C_U:C_U + SSM_WIDTH].transpose(1, 0, 2).reshape(seq * bsz, SSM_WIDTH)
        y_ssm = ssm_mixer(u_tm, ssm_a_re[l], ssm_a_im[l], ssm_b_re[l], ssm_b_im[l], ssm_c_re[l], ssm_c_im[l],
                          ssm_d[l], ssm_log_dt[l], ssm_w_glu[l], nb=bsz, tc=64)
        y_ssm = y_ssm.reshape(seq, bsz, SSM_WIDTH).transpose(1, 0, 2)
        y_mla = mla_mixer(proj3, mla_q_norm[l], mla_w_uq[l], mla_kv_norm[l], mla_w_ukv[l], rope_tabs,
                          tm=512, tq=512)
        nch = seq // CMP_STRIDE
        kv_cr = jnp.stack([proj3[:, :, C_NKV:C_NKV + LANES].reshape(bsz, nch, CMP_STRIDE * LANES),
                           proj3[:, :, C_NKV + LANES:C_NKV + 2 * LANES].reshape(bsz, nch, CMP_STRIDE * LANES)],
                          axis=1)
        kvc = nsa_compress(kv_cr, nsa_cmp_pe[l], nsa_cmp_w1[l], nsa_cmp_w2[l])
        y_nsa = nsa_mixer(proj3, kvc, nsa_tabs, tq=tq_nsa, tk=tk_nsa)
        g_out = out_norm[l]
        wo_l = w_out[l]
        kv_mem = norm_matmul(mem2, mem_norm[l], xattn_wkv[l].astype(BF16), tm=256, out_dtype=BF16)
        h = mix_out_cross_attention(
            [y_ssm, y_mla, y_nsa], h.reshape(bsz, seq, d),
            [g_out[:o1], g_out[o1:o2], _rg_order(g_out[o2:])],
            [wo_l[:o1].astype(BF16), wo_l[o1:o2].astype(BF16), _rg_order(wo_l[o2:]).astype(BF16)],
            kv_mem.reshape(bsz, nmem, 2 * d), xattn_norm[l], xattn_wq[l], xattn_wo[l], tm=512).reshape(T, d)
        last = l == depth - 1
        if l % 2 == 0:
            h = dense_ffn(h, ffn_norm[l], dense_w_gate[l // 2], dense_w_up[l // 2], dense_w_down[l // 2],
                          tm=1024, tf=1408)
            if last:
                h = final_rmsnorm(h, final_norm, tm=512)
        else:
            h = moe_layer(h, ffn_norm[l], moe_router[l // 2], moe_w_gate[l // 2], moe_w_up[l // 2],
                          moe_w_down[l // 2], final_norm, final_norm=last)
    return h.reshape(bsz, seq, d)
```

```python
import functools
import math

import jax
import jax.numpy as jnp
from jax import lax
from jax.experimental import pallas as pl
from jax.experimental.pallas import tpu as pltpu

F32 = jnp.float32
BF16 = jnp.bfloat16

SSM_WIDTH = 256
SSM_CH = 16
SSM_GROUPS = 16
SSM_STATE = 64
MLA_HEADS = 6
MLA_NOPE = 64
MLA_ROPE = 32
MLA_V = 64
MLA_Q_RANK = 192
MLA_KV_RANK = 128
NSA_HEADS = 6
NSA_KV_HEADS = 2
NSA_REP = 3
NSA_DIM = 64
CMP_BLOCK = 32
CMP_STRIDE = 16
SEL_BLOCK = 64
SEL_TOPN = 8
WINDOW = 256
REL_BUCKETS = 32
REL_MAX_DIST = 128
XATTN_HEADS = 4
N_EXPERTS = 8
ROPE_THETA = 10000.0
EPS = 1e-6
NEG_INF = -1e30
FORCE = 1e9
LOG2E = math.log2(math.e)

LANES = 128
VMEM_LIMIT = 56 * 1024 * 1024

W_CQ = 2 * LANES
W_KR = 2 * LANES
W_NQ = NSA_HEADS * LANES
W_NKV = 6 * LANES
_WIDTHS = (SSM_WIDTH, W_CQ, W_KR, W_NQ, W_NKV, MLA_KV_RANK, LANES)
C_U, C_CQ, C_KR, C_NQ, C_NKV, C_CKV, C_GATE = (sum(_WIDTHS[:k]) for k in range(len(_WIDTHS)))
IN_COLS_PACKED = sum(_WIDTHS)
assert all(c % w == 0 for c, w in zip((C_U, C_CQ, C_KR, C_NQ, C_NKV, C_CKV, C_GATE), _WIDTHS))


def _cparams(sem):
    return pltpu.CompilerParams(dimension_semantics=sem, vmem_limit_bytes=VMEM_LIMIT)


def _dot(a, b):
    return jnp.dot(a, b, preferred_element_type=F32)


def _dot_t(a, b):
    return lax.dot_general(a, b, (((1,), (1,)), ((), ())), preferred_element_type=F32)


def _rms(x, g, n):
    ms = jnp.sum(x * x, axis=-1, keepdims=True) * (1.0 / n)
    return x * lax.rsqrt(ms + EPS) * g


def _sigmoid(x):
    return 1.0 / (1.0 + jnp.exp(-x))


def _silu(x):
    return x * _sigmoid(x)


def _swiglu_accumulate(x_ref, wg, wu, wd, acc_ref, parts=2):
    rows = x_ref.shape[0] // parts
    sl = [slice(k * rows, (k + 1) * rows) for k in range(parts)]
    x = [x_ref[s, :] for s in sl]
    g = [_dot(xk, wg) for xk in x]
    u = [_dot(xk, wu) for xk in x]
    a = [(_silu(gk) * uk).astype(BF16) for gk, uk in zip(g, u)]
    d = [_dot(ak, wd) for ak in a]
    for s, dk in zip(sl, d):
        acc_ref[s, :] += dk


def _norm_mm_kernel(x_ref, g_ref, w_ref, o_ref, *f32_ref, f32_cols):
    rows = x_ref.shape[0] // 2
    sl = [slice(k * rows, (k + 1) * rows) for k in range(2)]
    xn = [_rms(x_ref[s, :].astype(F32), g_ref[...], x_ref.shape[-1]).astype(BF16) for s in sl]
    for s, xk in zip(sl, xn):
        r = _dot(xk, w_ref[...])
        o_ref[s, :] = r.astype(o_ref.dtype)
        if f32_cols is not None:
            f32_ref[0][s, :] = r[:, f32_cols[0]:f32_cols[0] + f32_cols[1]]


def norm_matmul(x, g, w, *, tm, out_dtype, f32_cols=None):
    m, k = x.shape
    n = w.shape[1]
    out_shape = [jax.ShapeDtypeStruct((m, n), out_dtype)]
    out_specs = [pl.BlockSpec((tm, n), lambda i: (i, 0))]
    if f32_cols is not None:
        out_shape.append(jax.ShapeDtypeStruct((m, f32_cols[1]), F32))
        out_specs.append(pl.BlockSpec((tm, f32_cols[1]), lambda i: (i, 0)))
    out = pl.pallas_call(
        functools.partial(_norm_mm_kernel, f32_cols=f32_cols),
        out_shape=out_shape,
        grid=(m // tm,),
        in_specs=[pl.BlockSpec((tm, k), lambda i: (i, 0)),
                  pl.BlockSpec((1, k), lambda i: (0, 0)),
                  pl.BlockSpec((k, n), lambda i: (0, 0))],
        out_specs=out_specs,
        compiler_params=_cparams(("parallel",)),
        name="norm_matmul",
    )(x, g.reshape(1, k).astype(F32), w)
    return out if f32_cols is not None else out[0]


def _ssm_kernel(u_ref, bbr_ref, bbi_ref, ar_ref, ai_ref, ccr_ref, cci_ref, d_ref, wglu_ref,
                o_ref, hr_ref, hi_ref, cr_ref, ci_ref, *, tc, nb):
    @pl.when(pl.program_id(0) == 0)
    def _():
        cr_ref[...] = jnp.zeros_like(cr_ref)
        ci_ref[...] = jnp.zeros_like(ci_ref)

    half = (tc * nb) // 2
    sl = [slice(k * half, (k + 1) * half) for k in range(2)]
    u = [u_ref[s, :] for s in sl]
    bu_r = [_dot(x, bbr_ref[...]) for x in u]
    bu_i = [_dot(x, bbi_ref[...]) for x in u]
    for s, r, i in zip(sl, bu_r, bu_i):
        hr_ref[s, :] = r
        hi_ref[s, :] = i
    gp = ar_ref.shape[-1]
    ar = jnp.broadcast_to(ar_ref[...], (nb, gp))
    ai = jnp.broadcast_to(ai_ref[...], (nb, gp))

    def step(t, carry):
        hr, hi = carry
        rows = pl.ds(pl.multiple_of(t * nb, nb), nb)
        nr = ar * hr - ai * hi + hr_ref[rows, :]
        ni = ar * hi + ai * hr + hi_ref[rows, :]
        hr_ref[rows, :] = nr
        hi_ref[rows, :] = ni
        return nr, ni

    hr, hi = lax.fori_loop(0, tc, step, (cr_ref[...], ci_ref[...]))
    cr_ref[...] = hr
    ci_ref[...] = hi
    y_r = [_dot(hr_ref[s, :].astype(BF16), ccr_ref[...]) for s in sl]
    y_i = [_dot(hi_ref[s, :].astype(BF16), cci_ref[...]) for s in sl]
    y = [jax.nn.gelu(a + b + d_ref[...] * x.astype(F32)) for a, b, x in zip(y_r, y_i, u)]
    z = [_dot(v.astype(BF16), wglu_ref[...]) for v in y]
    for s, v, w in zip(sl, y, z):
        o_ref[s, :] = (v * _sigmoid(w)).astype(o_ref.dtype)


def ssm_mixer(u_tm, a_re, a_im, b_re, b_im, c_re, c_im, d, log_dt, w_glu, *, nb, tc):
    rows = u_tm.shape[0]
    G, P, C = SSM_GROUPS, SSM_STATE, SSM_CH
    dt = jnp.exp(log_dt.astype(F32))[:, None]
    lr, li = a_re.astype(F32), a_im.astype(F32)
    mag = jnp.exp(lr * dt)
    ab_r, ab_i = mag * jnp.cos(li * dt), mag * jnp.sin(li * dt)
    den = lr * lr + li * li
    nr = ab_r - 1.0
    f_r = (nr * lr + ab_i * li) / den
    f_i = (ab_i * lr - nr * li) / den
    br, bi = b_re.astype(F32), b_im.astype(F32)
    bb_r = f_r[..., None] * br - f_i[..., None] * bi
    bb_i = f_r[..., None] * bi + f_i[..., None] * br
    eye = jnp.eye(G, dtype=F32)
    bbr = jnp.einsum('gpc,gh->gchp', bb_r, eye).reshape(G * C, G * P).astype(BF16)
    bbi = jnp.einsum('gpc,gh->gchp', bb_i, eye).reshape(G * C, G * P).astype(BF16)
    ccr = jnp.einsum('gcp,gh->gphc', c_re.astype(F32), eye).reshape(G * P, G * C).astype(BF16)
    cci = jnp.einsum('gcp,gh->gphc', -c_im.astype(F32), eye).reshape(G * P, G * C).astype(BF16)
    gp = G * P
    full = lambda shape: pl.BlockSpec(shape, lambda i: (0,) * len(shape))
    return pl.pallas_call(
        functools.partial(_ssm_kernel, tc=tc, nb=nb),
        out_shape=jax.ShapeDtypeStruct((rows, SSM_WIDTH), BF16),
        grid=(rows // (tc * nb),),
        in_specs=[pl.BlockSpec((tc * nb, SSM_WIDTH), lambda i: (i, 0)),
                  full((G * C, gp)), full((G * C, gp)), full((1, gp)), full((1, gp)),
                  full((gp, G * C)), full((gp, G * C)), full((1, SSM_WIDTH)),
                  full((SSM_WIDTH, SSM_WIDTH))],
        out_specs=pl.BlockSpec((tc * nb, SSM_WIDTH), lambda i: (i, 0)),
        scratch_shapes=[pltpu.VMEM((tc * nb, gp), F32), pltpu.VMEM((tc * nb, gp), F32),
                        pltpu.VMEM((nb, gp), F32), pltpu.VMEM((nb, gp), F32)],
        compiler_params=_cparams(("arbitrary",)),
        name="ssm_mixer",
    )(u_tm, bbr, bbi, ab_r.reshape(1, gp), ab_i.reshape(1, gp), ccr, cci,
      d.reshape(1, SSM_WIDTH).astype(F32), w_glu.astype(BF16))


def _mla_prep_kernel(cq_ref, kr_ref, ckv_ref, gq_ref, gkv_ref, wqa_ref, wqb_ref, wk_ref, wv_ref,
                     c1_ref, c0_ref, s0_ref, q_ref, k_ref, v_ref):
    qn = _rms(cq_ref[0].astype(F32), gq_ref[...], MLA_Q_RANK).astype(BF16)
    qa = _dot(qn, wqa_ref[...])
    qb = _dot(qn, wqb_ref[...])
    kn = _rms(ckv_ref[0].astype(F32), gkv_ref[...], MLA_KV_RANK).astype(BF16)
    ka = _dot(kn, wk_ref[...])
    va = _dot(kn, wv_ref[...])
    kr = kr_ref[0].astype(F32)
    c1, c0, s0 = c1_ref[...], c0_ref[...], s0_ref[...]
    krope = kr[:, :LANES] * c0 + kr[:, LANES:] * s0
    low_half = lax.broadcasted_iota(jnp.int32, (kr.shape[0], LANES), 1) < MLA_V
    for h in range(MLA_HEADS):
        sl = slice(h * LANES, (h + 1) * LANES)
        q_ref[0, h] = (qa[:, sl] * c1 + qb[:, sl] * s0).astype(BF16)
        k_ref[0, h] = (ka[:, sl] + krope).astype(BF16)
        ones = jnp.where(low_half if h % 2 == 0 else jnp.logical_not(low_half), 1.0, 0.0)
        v_ref[0, h] = jnp.concatenate([va[:, sl], ones], axis=1).astype(BF16)


def _mla_flash_kernel(q_ref, k_ref, v_ref, o_ref, m_ref, acc_ref, *, tq):
    qi = pl.program_id(1)
    m_ref[...] = jnp.full_like(m_ref, NEG_INF)
    acc_ref[...] = jnp.zeros_like(acc_ref)
    lane = lax.broadcasted_iota(jnp.int32, (tq, LANES), 1)
    rep = tq // LANES

    def tile(kt, masked):
        ks = pl.ds(pl.multiple_of(kt * tq, tq), tq)
        if masked:
            mask = (lax.broadcasted_iota(jnp.int32, (tq, tq), 1)
                    <= lax.broadcasted_iota(jnp.int32, (tq, tq), 0))
        hs = range(MLA_HEADS)
        s = [_dot_t(q_ref[0, h], k_ref[0, h, ks, :]) for h in hs]
        if masked:
            s = [jnp.where(mask, x, NEG_INF) for x in s]
        m_prev = [m_ref[h] for h in hs]
        m_new = [jnp.maximum(mp, jnp.max(x, axis=-1, keepdims=True)) for mp, x in zip(m_prev, s)]
        alpha = [jnp.exp2(mp - mn) for mp, mn in zip(m_prev, m_new)]
        p = [jnp.exp2(x - jnp.tile(mn, (1, rep))).astype(BF16) for x, mn in zip(s, m_new)]
        pv = [_dot(p[h], v_ref[0, h, ks, :]) for h in hs]
        for h in hs:
            m_ref[h] = m_new[h]
        for pr in range(MLA_HEADS // 2):
            a = jnp.tile(jnp.where(lane < MLA_V, alpha[2 * pr], alpha[2 * pr + 1]), (1, 2))
            acc_ref[pr] = acc_ref[pr] * a + pv[2 * pr] + pv[2 * pr + 1]

    def body(kt, c):
        tile(kt, False)
        return c

    lax.fori_loop(0, qi, body, 0)
    tile(qi, True)
    for pr in range(MLA_HEADS // 2):
        o_ref[0, :, pr * LANES:(pr + 1) * LANES] = (acc_ref[pr, :, :LANES] / acc_ref[pr, :, LANES:]).astype(o_ref.dtype)


def _rope_tables(seq):
    pos = jnp.arange(seq, dtype=F32)
    inv = 1.0 / (ROPE_THETA ** (jnp.arange(0, MLA_ROPE, 2, dtype=F32) / MLA_ROPE))
    ang = pos[:, None] * inv[None, :]
    cos, sin = jnp.cos(ang), jnp.sin(ang)
    cos2 = jnp.concatenate([cos, cos], axis=-1)
    sin2 = jnp.concatenate([sin, sin], axis=-1)
    z64 = jnp.zeros((seq, MLA_NOPE), F32)
    z32 = jnp.zeros((seq, LANES - MLA_NOPE - MLA_ROPE), F32)
    c1 = jnp.concatenate([jnp.ones((seq, MLA_NOPE), F32), cos2, z32], axis=-1)
    c0 = jnp.concatenate([z64, cos2, z32], axis=-1)
    s0 = jnp.concatenate([z64, sin2, z32], axis=-1)
    return c1, c0, s0


def _rot_half_cols(w):
    half = MLA_ROPE // 2
    return jnp.concatenate([-w[..., half:], w[..., :half]], axis=-1)


def mla_mixer(proj3, q_norm, w_uq, kv_norm, w_ukv, tabs, *, tm, tq):
    bsz, seq, _ = proj3.shape
    H = MLA_HEADS
    scale = (MLA_NOPE + MLA_ROPE) ** -0.5 * LOG2E
    wq =(w_uq.astype(F32) * scale).reshape(MLA_Q_RANK, H, MLA_NOPE + MLA_ROPE)
    zq = jnp.zeros((MLA_Q_RANK, H, LANES - MLA_NOPE - MLA_ROPE), F32)
    z64 = jnp.zeros((MLA_Q_RANK, H, MLA_NOPE), F32)
    wqa = jnp.concatenate([wq, zq], axis=-1).reshape(MLA_Q_RANK, H * LANES)
    wqb = jnp.concatenate([z64, _rot_half_cols(wq[..., MLA_NOPE:]), zq], axis=-1).reshape(MLA_Q_RANK, H * LANES)
    padq = ((0, W_CQ - MLA_Q_RANK), (0, 0))
    wqa = jnp.pad(wqa, padq).astype(BF16)
    wqb = jnp.pad(wqb, padq).astype(BF16)
    gq = jnp.pad(q_norm.astype(F32), (0, W_CQ - MLA_Q_RANK)).reshape(1, W_CQ)
    wkv = w_ukv.astype(F32).reshape(MLA_KV_RANK, H, MLA_NOPE + MLA_V)
    zk = jnp.zeros((MLA_KV_RANK, H, MLA_NOPE), F32)
    wk = jnp.concatenate([wkv[..., :MLA_NOPE], zk], axis=-1).reshape(MLA_KV_RANK, H * LANES).astype(BF16)
    wv_h = wkv[..., MLA_NOPE:]
    even = (jnp.arange(H) % 2 == 0)[None, :, None]
    wv = jnp.concatenate([jnp.where(even, wv_h, 0.0), jnp.where(even, 0.0, wv_h)], axis=-1)
    wv = wv.reshape(MLA_KV_RANK, H * LANES).astype(BF16)
    c1, c0, s0 = tabs
    full2 = lambda shape: pl.BlockSpec(shape, lambda b, i: (0,) * len(shape))
    tab_spec = pl.BlockSpec((tm, LANES), lambda b, i: (i, 0))
    hd_spec = pl.BlockSpec((1, H, tm, LANES), lambda b, i: (b, 0, i, 0))
    hd_shape = jax.ShapeDtypeStruct((bsz, H, seq, LANES), BF16)
    v_spec = pl.BlockSpec((1, H, tm, 2 * LANES), lambda b, i: (b, 0, i, 0))
    v_shape = jax.ShapeDtypeStruct((bsz, H, seq, 2 * LANES), BF16)
    q, k, v = pl.pallas_call(
        _mla_prep_kernel,
        out_shape=(hd_shape, hd_shape, v_shape),
        grid=(bsz, seq // tm),
        in_specs=[pl.BlockSpec((1, tm, W_CQ), lambda b, i: (b, i, C_CQ // W_CQ)),
                  pl.BlockSpec((1, tm, W_KR), lambda b, i: (b, i, C_KR // W_KR)),
                  pl.BlockSpec((1, tm, MLA_KV_RANK), lambda b, i: (b, i, C_CKV // MLA_KV_RANK)),
                  full2((1, W_CQ)), full2((1, MLA_KV_RANK)),
                  full2((W_CQ, H * LANES)), full2((W_CQ, H * LANES)),
                  full2((MLA_KV_RANK, H * LANES)), full2((MLA_KV_RANK, H * LANES)),
                  tab_spec, tab_spec, tab_spec],
        out_specs=(hd_spec, hd_spec, v_spec),
        compiler_params=_cparams(("parallel", "parallel")),
        name="mla_prep",
    )(proj3, proj3, proj3, gq, kv_norm.astype(F32).reshape(1, MLA_KV_RANK), wqa, wqb, wk, wv, c1, c0, s0)

    return pl.pallas_call(
        functools.partial(_mla_flash_kernel, tq=tq),
        out_shape=jax.ShapeDtypeStruct((bsz, seq, H * MLA_V), BF16),
        grid=(bsz, seq // tq),
        in_specs=[pl.BlockSpec((1, H, tq, LANES), lambda b, i: (b, 0, i, 0)),
                  pl.BlockSpec((1, H, seq, LANES), lambda b, i: (b, 0, 0, 0)),
                  pl.BlockSpec((1, H, seq, 2 * LANES), lambda b, i: (b, 0, 0, 0))],
        out_specs=pl.BlockSpec((1, tq, H * MLA_V), lambda b, i: (b, i, 0)),
        scratch_shapes=[pltpu.VMEM((H, tq, LANES), F32), pltpu.VMEM((H // 2, tq, 2 * LANES), F32)],
        compiler_params=_cparams(("parallel", "arbitrary")),
        name="mla_flash",
    )(q, k, v)


def _nsa_cmp_kernel(x_ref, pea_ref, peb_ref, w1a_ref, w1b_ref, w2_ref, o_ref):
    nch = o_ref.shape[2]
    bias = _dot(pea_ref[0], w1a_ref[0])[0:1] + _dot(peb_ref[0], w1b_ref[0])[0:1]
    a = b = None
    for pos in range(CMP_STRIDE):
        xs = x_ref[0, pl.ds(pos, nch, stride=CMP_STRIDE), :].astype(BF16)
        rows = slice(pos * LANES, (pos + 1) * LANES)
        da, db = _dot(xs, w1a_ref[0, rows, :]), _dot(xs, w1b_ref[0, rows, :])
        a, b = (da, db) if a is None else (a + da, b + db)
    pre = a + pltpu.roll(b, nch - 1, 0) + bias
    o_ref[0, 0] = _dot(jax.nn.gelu(pre).astype(BF16), w2_ref[0]).astype(o_ref.dtype)


def nsa_compress(kv_cr, cmp_pe, cmp_w1, cmp_w2):
    bsz, seq, _ = kv_cr.shape
    nch, width = seq // CMP_STRIDE, CMP_STRIDE * LANES
    G, dh = NSA_KV_HEADS, NSA_DIM
    half = CMP_BLOCK // 2
    eye = jnp.eye(G, dtype=F32)
    w1r = cmp_w1.astype(F32).reshape(2, CMP_BLOCK, dh, dh)
    w1a = jnp.einsum('kpde,gh->kpgdhe', w1r[:, :half], eye).reshape(2, width, G * dh).astype(BF16)
    w1b = jnp.einsum('kpde,gh->kpgdhe', w1r[:, half:], eye).reshape(2, width, G * dh).astype(BF16)
    w2 = jnp.einsum('kde,gh->kgdhe', cmp_w2.astype(F32), eye).reshape(2, G * dh, G * dh).astype(BF16)
    pe = cmp_pe.astype(F32)
    pe_g = jnp.broadcast_to(pe[:, :, None, :], (2, CMP_BLOCK, G, dh))
    pea = jnp.broadcast_to(pe_g[:, :half].reshape(2, 1, width), (2, 8, width)).astype(BF16)
    peb = jnp.broadcast_to(pe_g[:, half:].reshape(2, 1, width), (2, 8, width)).astype(BF16)
    kvspec = lambda shape: pl.BlockSpec(shape, lambda b, k: (k,) + (0,) * (len(shape) - 1))
    return pl.pallas_call(
        _nsa_cmp_kernel,
        out_shape=jax.ShapeDtypeStruct((bsz, 2, nch, G * dh), BF16),
        grid=(bsz, 2),
        in_specs=[pl.BlockSpec((1, seq, LANES), lambda b, k: (b, 0, k)),
                  kvspec((1, 8, width)), kvspec((1, 8, width)),
                  kvspec((1, width, G * dh)), kvspec((1, width, G * dh)),
                  kvspec((1, G * dh, G * dh))],
        out_specs=pl.BlockSpec((1, 1, nch, G * dh), lambda b, k: (b, k, 0, 0)),
        compiler_params=_cparams(("parallel", "parallel")),
        name="nsa_compress",
    )(kv_cr, pea, peb, w1a, w1b, w2)


def _nsa_kernel(q_ref, gate_ref, ksl_ref, vsl_ref, kwn_ref, vwn_ref, kvc_ref, bc_ref, bw_ref,
                bs_ref, ov_ref, blk1h_ref, gexp_ref, o_ref, kaug_ref, vaug_ref, m_ref, acc_ref,
                *, tq, tk, nbs, n_sel):
    qi = pl.program_id(1)
    R, G = NSA_REP, NSA_KV_HEADS
    H = R * G
    nsub = tk // tq
    nwin = WINDOW // tq

    @pl.when(qi == 0)
    def _():
        kaug_ref[:, :LANES] = ksl_ref[0]
        kaug_ref[:, LANES:] = blk1h_ref[...]
        vaug_ref[:, :LANES] = vsl_ref[0]
        vaug_ref[:, LANES:] = jnp.ones((vaug_ref.shape[0], LANES), BF16)

    lane = lax.broadcasted_iota(jnp.int32, (tq, LANES), 1)
    t_row = qi * tq + lax.broadcasted_iota(jnp.int32, (H * tq, 1), 0) % tq
    kc = kvc_ref[0, 0]
    vc = kvc_ref[0, 1]
    ov = ov_ref[...]

    def stack(fn):
        return jnp.concatenate([fn(h) for h in range(H)], axis=0)

    q_all = stack(lambda h: q_ref[0, :, h * LANES:(h + 1) * LANES])

    c = jnp.minimum(qi, nwin)
    ws = pl.ds(pl.multiple_of(jnp.maximum(qi - nwin, 0) * tq, tq), (nwin + 1) * tq)
    k_w = kwn_ref[0, ws, :]
    v_w = vwn_ref[0, ws, :]
    o_w = []
    for g in range(G):
        rows = slice(g * R * tq, (g + 1) * R * tq)
        s_w = _dot_t(q_all[rows], k_w) + jnp.concatenate([bw_ref[g * R + r, c] for r in range(R)], axis=0)
        p_w = jnp.exp2(s_w - jnp.max(s_w, axis=-1, keepdims=True))
        o_w.append(_dot(p_w.astype(BF16), v_w) / jnp.sum(p_w, axis=-1, keepdims=True))
    o_w = jnp.concatenate(o_w, axis=0)

    halves = [slice(g * R * tq, (g + 1) * R * tq) for g in range(G)]
    valid = [t_row[hs] >= (CMP_BLOCK - 1) for hs in halves]
    s = [_dot_t(q_all[hs], kc) + jnp.concatenate([bc_ref[g * R + r] for r in range(R)], axis=0)
         for g, hs in enumerate(halves)]
    p = [jnp.where(v, jnp.exp2(x - jnp.max(x, axis=-1, keepdims=True)), 0.0) for v, x in zip(valid, s)]
    pc = [x / jnp.where(v, jnp.sum(x, axis=-1, keepdims=True), 1.0) for v, x in zip(valid, p)]
    o_c = jnp.concatenate([_dot(x.astype(BF16), vc) for x in pc], axis=0)
    blk = lax.broadcasted_iota(jnp.int32, (nbs, tq), 0)
    tl = qi * tq + lax.broadcasted_iota(jnp.int32, (nbs, tq), 1)
    cur = tl // SEL_BLOCK
    forced = (blk == 0) | (blk == cur) | (blk == cur - 1)
    future = blk * SEL_BLOCK > tl
    qmask = []
    for g in range(G):
        psum = pc[g][0:tq] + pc[g][tq:2 * tq] + pc[g][2 * tq:3 * tq]
        p_hi = psum.astype(BF16)
        p_lo = (psum - p_hi.astype(F32)).astype(BF16)
        imp = (_dot_t(ov, p_hi) + _dot_t(ov, p_lo))[:nbs]
        imp = jnp.where(forced, FORCE, jnp.where(future, -FORCE, imp))
        rank = jnp.zeros((nbs, tq), F32)
        for i in range(nbs):
            ri = imp[i:i + 1, :]
            beats = (ri > imp) | ((ri == imp) & (blk > i))
            rank = rank + jnp.where(beats, 1.0, 0.0)
        sel = jnp.where(rank < n_sel, 0.0, NEG_INF)
        sel = jnp.concatenate([sel, jnp.zeros((LANES - nbs, tq), F32)], axis=0).T.astype(BF16)
        qmask += [sel] * R
    q_aug = jnp.concatenate([q_all, jnp.concatenate(qmask, axis=0)], axis=1)

    m_ref[...] = jnp.full_like(m_ref, NEG_INF)
    acc_ref[...] = jnp.zeros_like(acc_ref)

    def sel_tile(kt, bias):
        ks = pl.ds(pl.multiple_of(kt * tk, tk), tk)
        k_t = kaug_ref[ks, :]
        v_t = vaug_ref[ks, :]
        s = [_dot_t(q_aug[hs], k_t) for hs in halves]
        if bias is not None:
            s = [x + bias[hs] for x, hs in zip(s, halves)]
        m_prev = [m_ref[hs, :] for hs in halves]
        m_new = [jnp.maximum(mp, jnp.max(x, axis=-1, keepdims=True)) for mp, x in zip(m_prev, s)]
        alpha = [jnp.exp2(mp - mn) for mp, mn in zip(m_prev, m_new)]
        p = [jnp.exp2(x - jnp.tile(mn, (1, tk // LANES))).astype(BF16) for x, mn in zip(s, m_new)]
        pv = [_dot(pp, v_t) for pp in p]
        for hs, mn, a, o in zip(halves, m_new, alpha, pv):
            m_ref[hs, :] = mn
            acc_ref[hs, :] = acc_ref[hs, :] * jnp.tile(a, (1, 2)) + o

    def near_bias(kt):
        cols = []
        for sub in range(nsub):
            d = qi - (kt * nsub + sub)
            cols.append(stack(lambda h: jnp.where(d == 0, bs_ref[h, 0], jnp.where(
                d == 1, bs_ref[h, 1], jnp.where(d < 0, NEG_INF, 0.0)))))
        return jnp.concatenate(cols, axis=1)

    def far_body(kt, c):
        sel_tile(kt, None)
        return c

    kd = (qi * tq) // tk
    lax.fori_loop(0, jnp.maximum(kd - 1, 0), far_body, 0)

    @pl.when(kd >= 1)
    def _():
        sel_tile(kd - 1, near_bias(kd - 1))

    sel_tile(kd, near_bias(kd))
    o_s = acc_ref[:, :LANES] / acc_ref[:, LANES:]

    gates = _sigmoid(_dot(gate_ref[0], gexp_ref[...]))
    for r in range(R):
        res = None
        for b, o_b in enumerate((o_c, o_s, o_w)):
            o_br = jnp.where(lane < NSA_DIM, o_b[r * tq:(r + 1) * tq], o_b[(R + r) * tq:(R + r + 1) * tq])
            term = gates[:, (b * R + r) * LANES:(b * R + r + 1) * LANES] * o_br
            res = term if res is None else res + term
        o_ref[0, :, r * LANES:(r + 1) * LANES] = res.astype(o_ref.dtype)


def _t5_bucket(dist):
    n = jnp.maximum(dist, 0)
    exact = REL_BUCKETS // 2
    nf = jnp.maximum(n, exact).astype(F32)
    large = exact + jnp.floor(jnp.log(nf / exact) / math.log(REL_MAX_DIST / exact)
                              * (REL_BUCKETS - exact)).astype(jnp.int32)
    return jnp.where(n < exact, n, jnp.minimum(large, REL_BUCKETS - 1))


def _nsa_bias_tables(rel_bias, seq, tq):
    rb = rel_bias.astype(F32).T
    far = rb[:, REL_BUCKETS - 1].reshape(NSA_HEADS, 1, 1)

    def by_dist(dist, ok, shift=0.0):
        bucket = _t5_bucket(dist)[None]
        out = jnp.zeros((NSA_HEADS,) + dist.shape, F32)
        for k in range(REL_BUCKETS):
            out = jnp.where(bucket == k, rb[:, k].reshape((NSA_HEADS,) + (1,) * dist.ndim), out)
        return jnp.where(ok[None], (out - shift) * LOG2E, NEG_INF)

    i = jnp.arange(tq)[:, None]
    t = jnp.arange(seq)[:, None]
    dist_c = t - (jnp.arange(LANES)[None, :] * CMP_STRIDE + CMP_BLOCK - 1)
    bc = by_dist(dist_c, dist_c >= 0)
    nwin = WINDOW // tq
    jw = jnp.arange((nwin + 1) * tq)[None, :]
    bw = jnp.stack([by_dist(tq * c + i - jw, (tq * c + i - jw >= 0) & (tq * c + i - jw < WINDOW))
                    for c in range(nwin + 1)], axis=1)
    js = jnp.arange(tq)[None, :]
    bs = jnp.stack([by_dist(tq * c + i - js, tq * c + i - js >= 0, far) for c in range(2)], axis=1)
    ci = jnp.arange(LANES)[:, None]
    sj = jnp.arange(LANES)[None, :]
    nbs = seq // SEL_BLOCK
    ov = ((ci * CMP_STRIDE <= sj * SEL_BLOCK + SEL_BLOCK - 1)
          & (ci * CMP_STRIDE + CMP_BLOCK - 1 >= sj * SEL_BLOCK)
          & (ci < seq // CMP_STRIDE - 1) & (sj < nbs))
    blk1h = (jnp.arange(seq)[:, None] // SEL_BLOCK == sj).astype(BF16)
    col = jnp.arange(3 * NSA_REP * LANES)
    slab, lane_g = col // LANES, (col % LANES) // NSA_DIM
    gate_col = (slab // NSA_REP) * NSA_HEADS + lane_g * NSA_REP + slab % NSA_REP
    gexp = (ci == gate_col[None, :]).astype(BF16)
    return bc, bw, bs, ov.T.astype(BF16), blk1h, gexp


def nsa_mixer(proj3, kvc, tables, *, tq, tk):
    bsz, seq, _ = proj3.shape
    assert tq % LANES == 0 and tq >= REL_MAX_DIST and tk % tq == 0 and seq % tk == 0
    assert seq // CMP_STRIDE == LANES and WINDOW % tq == 0
    nwin = WINDOW // tq
    bc, bw, bs, ov, blk1h, gexp = tables
    nbs = seq // SEL_BLOCK
    H = NSA_HEADS
    slab = lambda j: pl.BlockSpec((1, seq, LANES), lambda b, i: (b, 0, C_NKV // LANES + j))
    const = lambda shape: pl.BlockSpec(shape, lambda b, i: (0,) * len(shape))
    return pl.pallas_call(
        functools.partial(_nsa_kernel, tq=tq, tk=tk, nbs=nbs, n_sel=min(SEL_TOPN, nbs)),
        out_shape=jax.ShapeDtypeStruct((bsz, seq, H * NSA_DIM), BF16),
        grid=(bsz, seq // tq),
        in_specs=[pl.BlockSpec((1, tq, H * LANES), lambda b, i: (b, i, C_NQ // (H * LANES))),
                  pl.BlockSpec((1, tq, LANES), lambda b, i: (b, i, C_GATE // LANES)),
                  slab(2), slab(3), slab(4), slab(5),
                  pl.BlockSpec((1, 2, LANES, LANES), lambda b, i: (b, 0, 0, 0)),
                  pl.BlockSpec((H, tq, LANES), lambda b, i: (0, i, 0)),
                  const((H, nwin + 1, tq, (nwin + 1) * tq)), const((H, 2, tq, tq)),
                  const((LANES, LANES)), const((seq, LANES)), const(gexp.shape)],
        out_specs=pl.BlockSpec((1, tq, H * NSA_DIM), lambda b, i: (b, i, 0)),
        scratch_shapes=[pltpu.VMEM((seq, 2 * LANES), BF16), pltpu.VMEM((seq, 2 * LANES), BF16),
                        pltpu.VMEM((H * tq, LANES), F32), pltpu.VMEM((H * tq, 2 * LANES), F32)],
        compiler_params=_cparams(("parallel", "arbitrary")),
        name="nsa_attention",
    )(proj3, proj3, proj3, proj3, proj3, proj3, kvc, bc, bw, bs, ov, blk1h, gexp)


def _mix_xattn_kernel(ys_ref, ym_ref, yn_ref, h_ref, g1_ref, g2_ref, g3_ref, w1_ref, w2_ref, w3_ref,
                      kv_ref, g_ref, wq_ref, wo_ref, o_ref, *, dh):
    rows = h_ref.shape[1] // 2
    sl = [slice(k * rows, (k + 1) * rows) for k in range(2)]
    hw = XATTN_HEADS * dh
    heads = range(XATTN_HEADS)

    def normed(y_ref, gy_ref):
        return [_rms(y_ref[0, s, :].astype(F32), gy_ref[...], y_ref.shape[-1]).astype(BF16) for s in sl]

    ns, nm, nn = normed(ys_ref, g1_ref), normed(ym_ref, g2_ref), normed(yn_ref, g3_ref)
    h = [h_ref[0, s, :] + _dot(a, w1_ref[...]) + _dot(b, w2_ref[...]) + _dot(c, w3_ref[...])
         for s, a, b, c in zip(sl, ns, nm, nn)]
    xn = [_rms(x, g_ref[...], x.shape[-1]).astype(BF16) for x in h]
    q = [_dot(x, wq_ref[...]).astype(BF16) for x in xn]
    s = [[_dot_t(qk[:, hd * dh:(hd + 1) * dh], kv_ref[0, :, hd * dh:(hd + 1) * dh]) for hd in heads] for qk in q]
    p = [[jnp.exp2(x - jnp.max(x, axis=-1, keepdims=True)) for x in sk] for sk in s]
    p = [[(x / jnp.sum(x, axis=-1, keepdims=True)).astype(BF16) for x in pk] for pk in p]
    o = [jnp.concatenate([_dot(pk[hd], kv_ref[0, :, hw + hd * dh:hw + (hd + 1) * dh]).astype(BF16)
                          for hd in heads], axis=-1) for pk in p]
    for sk, hk, ok in zip(sl, h, o):
        o_ref[0, sk, :] = hk + _dot(ok, wo_ref[...])


def mix_out_cross_attention(ys3, h3, gains, weights, kv3, g_x, wq, wo, *, tm):
    bsz, seq, d = h3.shape
    m = kv3.shape[1]
    dh = d // XATTN_HEADS
    wq_s = (wq.astype(F32) * (dh ** -0.5 * LOG2E)).astype(BF16)
    gains = [g.reshape(1, -1).astype(F32) for g in gains]
    const = lambda shape: pl.BlockSpec(shape, lambda b, i: (0,) * len(shape))
    row = lambda w: pl.BlockSpec((1, tm, w), lambda b, i: (b, i, 0))
    return pl.pallas_call(
        functools.partial(_mix_xattn_kernel, dh=dh),
        out_shape=jax.ShapeDtypeStruct((bsz, seq, d), F32),
        grid=(bsz, seq // tm),
        in_specs=[row(y.shape[2]) for y in ys3] + [row(d)]
                 + [const(g.shape) for g in gains] + [const(w.shape) for w in weights]
                 + [pl.BlockSpec((1, m, 2 * d), lambda b, i: (b, 0, 0)),
                    const((1, d)), const((d, d)), const((d, d))],
        out_specs=row(d),
        compiler_params=_cparams(("parallel", "parallel")),
        name="mix_out_cross_attention",
    )(*ys3, h3, *gains, *weights, kv3, g_x.reshape(1, d).astype(F32), wq_s, wo.astype(BF16))


def _ffn_kernel(h_ref, g_ref, wg_ref, wu_ref, wd_ref, o_ref, xn_ref, acc_ref):
    j = pl.program_id(1)

    @pl.when(j == 0)
    def _():
        h = h_ref[...]
        xn_ref[...] = _rms(h, g_ref[...], h.shape[-1]).astype(BF16)
        acc_ref[...] = h

    _swiglu_accumulate(xn_ref, wg_ref[...], wu_ref[...], wd_ref[...], acc_ref)

    @pl.when(j == pl.num_programs(1) - 1)
    def _():
        o_ref[...] = acc_ref[...]


def dense_ffn(h, g, wg, wu, wd, *, tm, tf):
    m, d = h.shape
    ff = wg.shape[1]
    return pl.pallas_call(
        _ffn_kernel,
        out_shape=jax.ShapeDtypeStruct((m, d), F32),
        grid=(m // tm, ff // tf),
        in_specs=[pl.BlockSpec((tm, d), lambda i, j: (i, 0)),
                  pl.BlockSpec((1, d), lambda i, j: (0, 0)),
                  pl.BlockSpec((d, tf), lambda i, j: (0, j)),
                  pl.BlockSpec((d, tf), lambda i, j: (0, j)),
                  pl.BlockSpec((tf, d), lambda i, j: (j, 0))],
        out_specs=pl.BlockSpec((tm, d), lambda i, j: (i, 0)),
        scratch_shapes=[pltpu.VMEM((tm, d), BF16), pltpu.VMEM((tm, d), F32)],
        compiler_params=_cparams(("parallel", "arbitrary")),
        name="dense_ffn",
    )(h, g.reshape(1, d).astype(F32), wg.astype(BF16), wu.astype(BF16), wd.astype(BF16))


def _router_kernel(h_ref, g_ref, wr_hi_ref, wr_lo_ref, xn_ref, info_ref, cnt_ref, carry_ref, *, tm):
    i = pl.program_id(0)

    @pl.when(i == 0)
    def _():
        carry_ref[...] = jnp.zeros_like(carry_ref)

    h = h_ref[...]
    xn = _rms(h, g_ref[...], h.shape[-1])
    xn_ref[...] = xn
    x_hi = xn.astype(BF16)
    x_lo = (xn - x_hi.astype(F32)).astype(BF16)
    logits = _dot(x_hi, wr_hi_ref[...]) + _dot(x_lo, wr_hi_ref[...]) + _dot(x_hi, wr_lo_ref[...])
    lane = lax.broadcasted_iota(jnp.int32, (tm, LANES), 1)
    lanef = lane.astype(F32)
    logits = jnp.where(lane < N_EXPERTS, logits, NEG_INF)
    m1 = jnp.max(logits, axis=-1, keepdims=True)
    i1 = jnp.min(jnp.where(logits == m1, lanef, float(LANES)), axis=-1, keepdims=True)
    rest = jnp.where(lanef == i1, NEG_INF, logits)
    m2 = jnp.max(rest, axis=-1, keepdims=True)
    i2 = jnp.min(jnp.where(rest == m2, lanef, float(LANES)), axis=-1, keepdims=True)
    e2 = jnp.exp(m2 - m1)
    w1 = 1.0 / (1.0 + e2)
    w2 = e2 / (1.0 + e2)
    oh1 = lanef == i1
    oh2 = lanef == i2
    oh = jnp.where(oh1 | oh2, 1.0, 0.0)
    rr = lax.broadcasted_iota(jnp.int32, (tm, tm), 0)
    cc = lax.broadcasted_iota(jnp.int32, (tm, tm), 1)
    tri = jnp.where(cc < rr, 1.0, 0.0).astype(BF16)
    before = _dot(tri, oh.astype(BF16)) + carry_ref[0:1, :]
    r1 = jnp.sum(jnp.where(oh1, before, 0.0), axis=-1, keepdims=True)
    r2 = jnp.sum(jnp.where(oh2, before, 0.0), axis=-1, keepdims=True)
    carry_ref[...] = carry_ref[...] + jnp.sum(oh, axis=0, keepdims=True)
    info = jnp.where(lane == 0, i1, jnp.where(lane == 1, i2, jnp.where(lane == 2, w1, jnp.where(
        lane == 3, w2, jnp.where(lane == 4, r1, jnp.where(lane == 5, r2, 0.0))))))
    info_ref[...] = info
    cnt_ref[...] = carry_ref[...]


def moe_router(h, g, router, *, tm):
    m, d = h.shape
    wr = jnp.pad(router.astype(F32), ((0, 0), (0, LANES - N_EXPERTS)))
    wr_hi = wr.astype(BF16)
    wr_lo = (wr - wr_hi.astype(F32)).astype(BF16)
    return pl.pallas_call(
        functools.partial(_router_kernel, tm=tm),
        out_shape=(jax.ShapeDtypeStruct((m, d), F32), jax.ShapeDtypeStruct((m, LANES), F32),
                   jax.ShapeDtypeStruct((8, LANES), F32)),
        grid=(m // tm,),
        in_specs=[pl.BlockSpec((tm, d), lambda i: (i, 0)),
                  pl.BlockSpec((1, d), lambda i: (0, 0)),
                  pl.BlockSpec((d, LANES), lambda i: (0, 0)),
                  pl.BlockSpec((d, LANES), lambda i: (0, 0))],
        out_specs=(pl.BlockSpec((tm, d), lambda i: (i, 0)),
                   pl.BlockSpec((tm, LANES), lambda i: (i, 0)),
                   pl.BlockSpec((8, LANES), lambda i: (0, 0))),
        scratch_shapes=[pltpu.VMEM((8, LANES), F32)],
        compiler_params=_cparams(("arbitrary",)),
        name="moe_router",
    )(h, g.reshape(1, d).astype(F32), wr_hi, wr_lo)


def _row_copy(src_hbm, row, dst, slot, sem):
    return pltpu.make_async_copy(src_hbm.at[pl.ds(row, 1), :], dst.at[pl.ds(slot, 1), :], sem)


def _rows_wait(src_hbm, dst, sem):
    pltpu.make_async_copy(src_hbm.at[pl.ds(0, dst.shape[0]), :], dst, sem).wait()


def _moe_ffn_kernel(src_ref, texp_ref, nact_ref, x_hbm, wg_ref, wu_ref, wd_ref, o_ref,
                    xbuf, xbf, acc_ref, sem, *, tm, nj):
    i = pl.program_id(0)
    j = pl.program_id(1)
    nact = nact_ref[0]
    active = i < nact
    nbuf = xbuf.shape[0]
    ahead = nbuf - 1
    cur = i % nbuf
    rows_per_step = tm // nj

    for t0 in range(ahead):
        @pl.when((i == 0) & (j == 0) & (t0 < nact))
        def _():
            def issue(s, c):
                _row_copy(x_hbm, src_ref[t0 * tm + s], xbuf.at[t0], s, sem.at[t0]).start()
                return c

            lax.fori_loop(0, tm, issue, 0, unroll=8)

    @pl.when(active & (j == 0))
    def _():
        _rows_wait(x_hbm, xbuf.at[cur], sem.at[cur])
        xbf[...] = xbuf[cur].astype(BF16)
        acc_ref[...] = jnp.zeros_like(acc_ref)

    def compute(prefetch):
        if prefetch:
            nxt = (i + ahead) % nbuf
            base = (i + ahead) * tm + j * rows_per_step
            for k in range(rows_per_step):
                _row_copy(x_hbm, src_ref[base + k], xbuf.at[nxt], j * rows_per_step + k, sem.at[nxt]).start()
        _swiglu_accumulate(xbf, wg_ref[0], wu_ref[0], wd_ref[0], acc_ref)

    @pl.when(i + ahead < nact)
    def _():
        compute(True)

    @pl.when(active & (i + ahead >= nact))
    def _():
        compute(False)

    @pl.when(j == nj - 1)
    def _():
        o_ref[...] = jnp.where(active, acc_ref[...], 0.0)


def moe_expert_ffn(xn, src, tile_expert, n_active, wg, wu, wd, *, tm, tf):
    n_slots = src.shape[0]
    d = xn.shape[1]
    ne, _, ff = wg.shape
    nj = ff // tf
    assert nj * tf == ff and tm % nj == 0
    wg, wu, wd = wg.astype(BF16), wu.astype(BF16), wd.astype(BF16)

    def wmap_col(i, j, src, texp, nact):
        return (texp[i], 0, jnp.where(i < nact[0], j, nj - 1))

    def wmap_row(i, j, src, texp, nact):
        return (texp[i], jnp.where(i < nact[0], j, nj - 1), 0)

    return pl.pallas_call(
        functools.partial(_moe_ffn_kernel, tm=tm, nj=nj),
        out_shape=jax.ShapeDtypeStruct((n_slots, d), F32),
        grid_spec=pltpu.PrefetchScalarGridSpec(
            num_scalar_prefetch=3,
            grid=(n_slots // tm, nj),
            in_specs=[pl.BlockSpec(memory_space=pl.ANY),
                      pl.BlockSpec((1, d, tf), wmap_col),
                      pl.BlockSpec((1, d, tf), wmap_col),
                      pl.BlockSpec((1, tf, d), wmap_row)],
            out_specs=pl.BlockSpec((tm, d), lambda i, j, *_: (i, 0)),
            scratch_shapes=[pltpu.VMEM((3, tm, d), F32), pltpu.VMEM((tm, d), BF16),
                            pltpu.VMEM((tm, d), F32), pltpu.SemaphoreType.DMA((3,))]),
        compiler_params=_cparams(("arbitrary", "arbitrary")),
        name="moe_expert_ffn",
    )(src, tile_expert, n_active, xn, wg, wu, wd)


def _moe_combine_kernel(pos_ref, h_ref, info_ref, ys_hbm, g_ref, o_ref, buf, sem, *, tm, final_norm):
    i = pl.program_id(0)
    n = pl.num_programs(0)
    cur = i % 2

    def start_gather(tile, b):
        for s in range(tm):
            for k in range(2):
                _row_copy(ys_hbm, pos_ref[2 * (tile * tm + s) + k], buf.at[b, k], s, sem.at[b]).start()

    @pl.when(i == 0)
    def _():
        start_gather(0, 0)

    @pl.when(i + 1 < n)
    def _():
        start_gather(i + 1, 1 - cur)

    for k in range(2):
        _rows_wait(ys_hbm, buf.at[cur, k], sem.at[cur])
    info = info_ref[...]
    y = h_ref[...] + info[:, 2:3] * buf[cur, 0] + info[:, 3:4] * buf[cur, 1]
    if final_norm:
        y = _rms(y, g_ref[...], y.shape[-1])
    o_ref[...] = y


def moe_combine(h, info, ys, pos_flat, g_final, *, tm, final_norm):
    m, d = h.shape
    return pl.pallas_call(
        functools.partial(_moe_combine_kernel, tm=tm, final_norm=final_norm),
        out_shape=jax.ShapeDtypeStruct((m, d), F32),
        grid_spec=pltpu.PrefetchScalarGridSpec(
            num_scalar_prefetch=1,
            grid=(m // tm,),
            in_specs=[pl.BlockSpec((tm, d), lambda i, *_: (i, 0)),
                      pl.BlockSpec((tm, LANES), lambda i, *_: (i, 0)),
                      pl.BlockSpec(memory_space=pl.ANY),
                      pl.BlockSpec((1, d), lambda i, *_: (0, 0))],
            out_specs=pl.BlockSpec((tm, d), lambda i, *_: (i, 0)),
            scratch_shapes=[pltpu.VMEM((2, 2, tm, d), F32), pltpu.SemaphoreType.DMA((2,))]),
        compiler_params=_cparams(("arbitrary",)),
        name="moe_combine",
    )(pos_flat, h, info, ys, g_final.reshape(1, d).astype(F32))


def moe_layer(h, g, router, wg, wu, wd, g_final, *, final_norm, tm_r=512, tm_g=512, tf=1792, tm_c=512):
    m, d = h.shape
    xn, info, cnt = moe_router(h, g, router, tm=tm_r)
    e_idx = info[:, 0:2].astype(jnp.int32)
    rank = info[:, 4:6].astype(jnp.int32)
    counts = cnt[0, :N_EXPERTS].astype(jnp.int32)
    tiles_per = (counts + tm_g - 1) // tm_g
    tile_end = jnp.cumsum(tiles_per)
    seg_start = (tile_end - tiles_per) * tm_g
    pos = rank
    for e in range(N_EXPERTS):
        pos = pos + jnp.where(e_idx == e, seg_start[e], 0)
    n_tiles = (2 * m) // tm_g + N_EXPERTS
    n_slots = n_tiles * tm_g
    tok = jnp.broadcast_to(jnp.arange(m, dtype=jnp.int32)[:, None], (m, 2))
    src = jnp.zeros((n_slots,), jnp.int32).at[pos.reshape(-1)].set(tok.reshape(-1))
    n_active = tile_end[-1:].astype(jnp.int32)
    tile_ids = jnp.minimum(jnp.arange(n_tiles, dtype=jnp.int32), n_active[0] - 1)
    tile_expert = jnp.sum(tile_ids[:, None] >= tile_end[None, :], axis=1).astype(jnp.int32)
    ys = moe_expert_ffn(xn, src, tile_expert, n_active,
                        wg, wu, wd, tm=tm_g, tf=tf)
    return moe_combine(h, info, ys, pos.reshape(-1).astype(jnp.int32), g_final, tm=tm_c, final_norm=final_norm)


def _final_norm_kernel(h_ref, g_ref, o_ref):
    h = h_ref[...]
    o_ref[...] = _rms(h, g_ref[...], h.shape[-1])


def final_rmsnorm(h, g, *, tm):
    m, d = h.shape
    return pl.pallas_call(
        _final_norm_kernel,
        out_shape=jax.ShapeDtypeStruct((m, d), F32),
        grid=(m // tm,),
        in_specs=[pl.BlockSpec((tm, d), lambda i: (i, 0)), pl.BlockSpec((1, d), lambda i: (0, 0))],
        out_specs=pl.BlockSpec((tm, d), lambda i: (i, 0)),
        compiler_params=_cparams(("parallel",)),
        name="final_rmsnorm",
    )(h, g.reshape(1, d).astype(F32))


def _pack_w_in(w):
    d = w.shape[0]
    w = w.astype(F32)
    o = 0
    u = w[:, o:o + SSM_WIDTH]; o += SSM_WIDTH
    cq = w[:, o:o + MLA_Q_RANK]; o += MLA_Q_RANK
    ckv = w[:, o:o + MLA_KV_RANK]; o += MLA_KV_RANK
    kr = w[:, o:o + MLA_ROPE]; o += MLA_ROPE
    nq = w[:, o:o + NSA_HEADS * NSA_DIM]; o += NSA_HEADS * NSA_DIM
    nkv = w[:, o:o + 6 * NSA_KV_HEADS * NSA_DIM]; o += 6 * NSA_KV_HEADS * NSA_DIM
    gate = w[:, o:o + 3 * NSA_HEADS]
    z = lambda n: jnp.zeros((d, n), F32)
    kr_a = jnp.concatenate([z(MLA_NOPE), kr, z(LANES - MLA_NOPE - MLA_ROPE)], axis=1)
    kr_b = jnp.concatenate([z(MLA_NOPE), _rot_half_cols(kr), z(LANES - MLA_NOPE - MLA_ROPE)], axis=1)
    nq_h = (nq * (NSA_DIM ** -0.5 * LOG2E)).reshape(d, NSA_KV_HEADS, NSA_REP, NSA_DIM)
    zq = jnp.zeros((d, NSA_REP, NSA_DIM), F32)
    nq_p = jnp.concatenate([
        jnp.concatenate([nq_h[:, 0], zq], axis=-1).reshape(d, NSA_REP * LANES),
        jnp.concatenate([zq, nq_h[:, 1]], axis=-1).reshape(d, NSA_REP * LANES)], axis=1)
    packed = jnp.concatenate([u, cq, z(W_CQ - MLA_Q_RANK), kr_a, kr_b, nq_p, nkv, ckv,
                              gate, z(LANES - 3 * NSA_HEADS)], axis=1)
    assert packed.shape[1] == IN_COLS_PACKED
    return packed.astype(BF16)


def _rg_order(a):
    rest = a.shape[1:]
    return a.reshape((NSA_KV_HEADS, NSA_REP, NSA_DIM) + rest).swapaxes(0, 1).reshape((-1,) + rest)


def kernel(x, mem, w_in, w_out, mix_norm, out_norm, ssm_a_re, ssm_a_im, ssm_b_re, ssm_b_im, ssm_c_re, ssm_c_im, ssm_d, ssm_log_dt, ssm_w_glu, mla_q_norm, mla_w_uq, mla_kv_norm, mla_w_ukv, nsa_cmp_pe, nsa_cmp_w1, nsa_cmp_w2, rel_bias, xattn_norm, mem_norm, xattn_wq, xattn_wkv, xattn_wo, ffn_norm, dense_w_gate, dense_w_up, dense_w_down, moe_router, moe_w_gate, moe_w_up, moe_w_down, final_norm):
    bsz, seq, d = x.shape
    depth = w_in.shape[0]
    T = bsz * seq
    nmem = mem.shape[1]
    tq_nsa, tk_nsa = 2 * LANES, 4 * LANES
    rope_tabs = _rope_tables(seq)
    nsa_tabs = _nsa_bias_tables(rel_bias, seq, tq_nsa)
    o1, o2 = SSM_WIDTH, SSM_WIDTH + MLA_HEADS * MLA_V
    mem2 = mem.reshape(bsz * nmem, d)
    h = x.reshape(T, d)
    for l in range(depth):
        proj, kv_cr = norm_matmul(h, mix_norm[l], _pack_w_in(w_in[l]), tm=1024, out_dtype=BF16,
                                  f32_cols=(C_NKV, 2 * LANES))
        proj3 = proj.reshape(bsz, seq, IN_COLS_PACKED)
        u_tm = proj3[:, :, C_U:C_U + SSM_WIDTH].transpose(1, 0, 2).reshape(seq * bsz, SSM_WIDTH)
        y_ssm = ssm_mixer(u_tm, ssm_a_re[l], ssm_a_im[l], ssm_b_re[l], ssm_b_im[l], ssm_c_re[l], ssm_c_im[l],
                          ssm_d[l], ssm_log_dt[l], ssm_w_glu[l], nb=bsz, tc=64)
        y_ssm = y_ssm.reshape(seq, bsz, SSM_WIDTH).transpose(1, 0, 2)
        y_mla = mla_mixer(proj3, mla_q_norm[l], mla_w_uq[l], mla_kv_norm[l], mla_w_ukv[l], rope_tabs,
                          tm=512, tq=512)
        kvc = nsa_compress(kv_cr.reshape(bsz, seq, 2 * LANES), nsa_cmp_pe[l], nsa_cmp_w1[l], nsa_cmp_w2[l])
        y_nsa = nsa_mixer(proj3, kvc, nsa_tabs, tq=tq_nsa, tk=tk_nsa)
        g_out = out_norm[l]
        wo_l = w_out[l]
        kv_mem = norm_matmul(mem2, mem_norm[l], xattn_wkv[l].astype(BF16), tm=256, out_dtype=BF16)
        h = mix_out_cross_attention(
            [y_ssm, y_mla, y_nsa], h.reshape(bsz, seq, d),
            [g_out[:o1], g_out[o1:o2], _rg_order(g_out[o2:])],
            [wo_l[:o1].astype(BF16), wo_l[o1:o2].astype(BF16), _rg_order(wo_l[o2:]).astype(BF16)],
            kv_mem.reshape(bsz, nmem, 2 * d), xattn_norm[l], xattn_wq[l], xattn_wo[l], tm=512).reshape(T, d)
        last = l == depth - 1
        if l % 2 == 0:
            h = dense_ffn(h, ffn_norm[l], dense_w_gate[l // 2], dense_w_up[l // 2], dense_w_down[l // 2],
                          tm=1024, tf=1408)
            if last:
                h = final_rmsnorm(h, final_norm, tm=512)
        else:
            h = moe_layer(h, ffn_norm[l], moe_router[l // 2], moe_w_gate[l // 2], moe_w_up[l // 2],
                          moe_w_down[l // 2], final_norm, final_norm=last)
    return h.reshape(bsz, seq, d)
```

```python
import functools
import math

import jax
import jax.numpy as jnp
from jax import lax
from jax.experimental import pallas as pl
from jax.experimental.pallas import tpu as pltpu

F32 = jnp.float32
BF16 = jnp.bfloat16

SSM_WIDTH = 256
SSM_CH = 16
SSM_GROUPS = 16
SSM_STATE = 64
MLA_HEADS = 6
MLA_NOPE = 64
MLA_ROPE = 32
MLA_V = 64
MLA_Q_RANK = 192
MLA_KV_RANK = 128
NSA_HEADS = 6
NSA_KV_HEADS = 2
NSA_REP = 3
NSA_DIM = 64
CMP_BLOCK = 32
CMP_STRIDE = 16
SEL_BLOCK = 64
SEL_TOPN = 8
WINDOW = 256
REL_BUCKETS = 32
REL_MAX_DIST = 128
XATTN_HEADS = 4
N_EXPERTS = 8
ROPE_THETA = 10000.0
EPS = 1e-6
NEG_INF = -1e30
FORCE = 1e9
LOG2E = math.log2(math.e)

LANES = 128
VMEM_LIMIT = 56 * 1024 * 1024

W_CQ = 2 * LANES
W_KR = 2 * LANES
W_NQ = NSA_HEADS * LANES
W_NKV = 6 * LANES
_WIDTHS = (SSM_WIDTH, W_CQ, W_KR, W_NQ, W_NKV, MLA_KV_RANK, LANES)
C_U, C_CQ, C_KR, C_NQ, C_NKV, C_CKV, C_GATE = (sum(_WIDTHS[:k]) for k in range(len(_WIDTHS)))
IN_COLS_PACKED = sum(_WIDTHS)
assert all(c % w == 0 for c, w in zip((C_U, C_CQ, C_KR, C_NQ, C_NKV, C_CKV, C_GATE), _WIDTHS))


def _cparams(sem):
    return pltpu.CompilerParams(dimension_semantics=sem, vmem_limit_bytes=VMEM_LIMIT)


def _dot(a, b):
    return jnp.dot(a, b, preferred_element_type=F32)


def _dot_t(a, b):
    return lax.dot_general(a, b, (((1,), (1,)), ((), ())), preferred_element_type=F32)


def _rms(x, g, n):
    ms = jnp.sum(x * x, axis=-1, keepdims=True) * (1.0 / n)
    return x * lax.rsqrt(ms + EPS) * g


def _sigmoid(x):
    return 1.0 / (1.0 + jnp.exp(-x))


def _silu(x):
    return x * _sigmoid(x)


def _swiglu_accumulate(x_ref, wg, wu, wd, acc_ref, parts=2):
    rows = x_ref.shape[0] // parts
    sl = [slice(k * rows, (k + 1) * rows) for k in range(parts)]
    x = [x_ref[s, :] for s in sl]
    g = [_dot(xk, wg) for xk in x]
    u = [_dot(xk, wu) for xk in x]
    a = [(_silu(gk) * uk).astype(BF16) for gk, uk in zip(g, u)]
    d = [_dot(ak, wd) for ak in a]
    for s, dk in zip(sl, d):
        acc_ref[s, :] += dk


def _norm_mm_kernel(x_ref, g_ref, w_ref, o_ref, *f32_ref, f32_cols):
    rows = x_ref.shape[0] // 2
    sl = [slice(k * rows, (k + 1) * rows) for k in range(2)]
    xn = [_rms(x_ref[s, :].astype(F32), g_ref[...], x_ref.shape[-1]).astype(BF16) for s in sl]
    for s, xk in zip(sl, xn):
        r = _dot(xk, w_ref[...])
        o_ref[s, :] = r.astype(o_ref.dtype)
        if f32_cols is not None:
            f32_ref[0][s, :] = r[:, f32_cols[0]:f32_cols[0] + f32_cols[1]]


def norm_matmul(x, g, w, *, tm, out_dtype, f32_cols=None):
    m, k = x.shape
    n = w.shape[1]
    out_shape = [jax.ShapeDtypeStruct((m, n), out_dtype)]
    out_specs = [pl.BlockSpec((tm, n), lambda i: (i, 0))]
    if f32_cols is not None:
        out_shape.append(jax.ShapeDtypeStruct((m, f32_cols[1]), F32))
        out_specs.append(pl.BlockSpec((tm, f32_cols[1]), lambda i: (i, 0)))
    out = pl.pallas_call(
        functools.partial(_norm_mm_kernel, f32_cols=f32_cols),
        out_shape=out_shape,
        grid=(m // tm,),
        in_specs=[pl.BlockSpec((tm, k), lambda i: (i, 0)),
                  pl.BlockSpec((1, k), lambda i: (0, 0)),
                  pl.BlockSpec((k, n), lambda i: (0, 0))],
        out_specs=out_specs,
        compiler_params=_cparams(("parallel",)),
        name="norm_matmul",
    )(x, g.reshape(1, k).astype(F32), w)
    return out if f32_cols is not None else out[0]


def _ssm_kernel(u_ref, bbr_ref, bbi_ref, ar_ref, ai_ref, ccr_ref, cci_ref, d_ref, wglu_ref,
                o_ref, hr_ref, hi_ref, cr_ref, ci_ref, *, tc, nb):
    @pl.when(pl.program_id(0) == 0)
    def _():
        cr_ref[...] = jnp.zeros_like(cr_ref)
        ci_ref[...] = jnp.zeros_like(ci_ref)

    half = (tc * nb) // 2
    sl = [slice(k * half, (k + 1) * half) for k in range(2)]
    u = [u_ref[s, :] for s in sl]
    bu_r = [_dot(x, bbr_ref[...]) for x in u]
    bu_i = [_dot(x, bbi_ref[...]) for x in u]
    for s, r, i in zip(sl, bu_r, bu_i):
        hr_ref[s, :] = r
        hi_ref[s, :] = i
    gp = ar_ref.shape[-1]
    ar = jnp.broadcast_to(ar_ref[...], (nb, gp))
    ai = jnp.broadcast_to(ai_ref[...], (nb, gp))

    def step(t, carry):
        hr, hi = carry
        rows = pl.ds(pl.multiple_of(t * nb, nb), nb)
        nr = ar * hr - ai * hi + hr_ref[rows, :]
        ni = ar * hi + ai * hr + hi_ref[rows, :]
        hr_ref[rows, :] = nr
        hi_ref[rows, :] = ni
        return nr, ni

    hr, hi = lax.fori_loop(0, tc, step, (cr_ref[...], ci_ref[...]))
    cr_ref[...] = hr
    ci_ref[...] = hi
    y_r = [_dot(hr_ref[s, :].astype(BF16), ccr_ref[...]) for s in sl]
    y_i = [_dot(hi_ref[s, :].astype(BF16), cci_ref[...]) for s in sl]
    y = [jax.nn.gelu(a + b + d_ref[...] * x.astype(F32)) for a, b, x in zip(y_r, y_i, u)]
    z = [_dot(v.astype(BF16), wglu_ref[...]) for v in y]
    for s, v, w in zip(sl, y, z):
        o_ref[s, :] = (v * _sigmoid(w)).astype(o_ref.dtype)


def ssm_mixer(u_tm, a_re, a_im, b_re, b_im, c_re, c_im, d, log_dt, w_glu, *, nb, tc):
    rows = u_tm.shape[0]
    G, P, C = SSM_GROUPS, SSM_STATE, SSM_CH
    dt = jnp.exp(log_dt.astype(F32))[:, None]
    lr, li = a_re.astype(F32), a_im.astype(F32)
    mag = jnp.exp(lr * dt)
    ab_r, ab_i = mag * jnp.cos(li * dt), mag * jnp.sin(li * dt)
    den = lr * lr + li * li
    nr = ab_r - 1.0
    f_r = (nr * lr + ab_i * li) / den
    f_i = (ab_i * lr - nr * li) / den
    br, bi = b_re.astype(F32), b_im.astype(F32)
    bb_r = f_r[..., None] * br - f_i[..., None] * bi
    bb_i = f_r[..., None] * bi + f_i[..., None] * br
    eye = jnp.eye(G, dtype=F32)
    bbr = jnp.einsum('gpc,gh->gchp', bb_r, eye).reshape(G * C, G * P).astype(BF16)
    bbi = jnp.einsum('gpc,gh->gchp', bb_i, eye).reshape(G * C, G * P).astype(BF16)
    ccr = jnp.einsum('gcp,gh->gphc', c_re.astype(F32), eye).reshape(G * P, G * C).astype(BF16)
    cci = jnp.einsum('gcp,gh->gphc', -c_im.astype(F32), eye).reshape(G * P, G * C).astype(BF16)
    gp = G * P
    full = lambda shape: pl.BlockSpec(shape, lambda i: (0,) * len(shape))
    return pl.pallas_call(
        functools.partial(_ssm_kernel, tc=tc, nb=nb),
        out_shape=jax.ShapeDtypeStruct((rows, SSM_WIDTH), BF16),
        grid=(rows // (tc * nb),),
        in_specs=[pl.BlockSpec((tc * nb, SSM_WIDTH), lambda i: (i, 0)),
                  full((G * C, gp)), full((G * C, gp)), full((1, gp)), full((1, gp)),
                  full((gp, G * C)), full((gp, G * C)), full((1, SSM_WIDTH)),
                  full((SSM_WIDTH, SSM_WIDTH))],
        out_specs=pl.BlockSpec((tc * nb, SSM_WIDTH), lambda i: (i, 0)),
        scratch_shapes=[pltpu.VMEM((tc * nb, gp), F32), pltpu.VMEM((tc * nb, gp), F32),
                        pltpu.VMEM((nb, gp), F32), pltpu.VMEM((nb, gp), F32)],
        compiler_params=_cparams(("arbitrary",)),
        name="ssm_mixer",
    )(u_tm, bbr, bbi, ab_r.reshape(1, gp), ab_i.reshape(1, gp), ccr, cci,
      d.reshape(1, SSM_WIDTH).astype(F32), w_glu.astype(BF16))


def _mla_prep_kernel(cq_ref, kr_ref, ckv_ref, gq_ref, gkv_ref, wqa_ref, wqb_ref, wk_ref, wv_ref,
                     c1_ref, c0_ref, s0_ref, q_ref, k_ref, v_ref):
    qn = _rms(cq_ref[0].astype(F32), gq_ref[...], MLA_Q_RANK).astype(BF16)
    qa = _dot(qn, wqa_ref[...])
    qb = _dot(qn, wqb_ref[...])
    kn = _rms(ckv_ref[0].astype(F32), gkv_ref[...], MLA_KV_RANK).astype(BF16)
    ka = _dot(kn, wk_ref[...])
    va = _dot(kn, wv_ref[...])
    kr = kr_ref[0].astype(F32)
    c1, c0, s0 = c1_ref[...], c0_ref[...], s0_ref[...]
    krope = kr[:, :LANES] * c0 + kr[:, LANES:] * s0
    low_half = lax.broadcasted_iota(jnp.int32, (kr.shape[0], LANES), 1) < MLA_V
    for h in range(MLA_HEADS):
        sl = slice(h * LANES, (h + 1) * LANES)
        q_ref[0, h] = (qa[:, sl] * c1 + qb[:, sl] * s0).astype(BF16)
        k_ref[0, h] = (ka[:, sl] + krope).astype(BF16)
        ones = jnp.where(low_half if h % 2 == 0 else jnp.logical_not(low_half), 1.0, 0.0)
        v_ref[0, h] = jnp.concatenate([va[:, sl], ones], axis=1).astype(BF16)


def _mla_flash_kernel(q_ref, k_ref, v_ref, o_ref, m_ref, acc_ref, *, tq):
    qi = pl.program_id(1)
    m_ref[...] = jnp.full_like(m_ref, NEG_INF)
    acc_ref[...] = jnp.zeros_like(acc_ref)
    lane = lax.broadcasted_iota(jnp.int32, (tq, LANES), 1)
    rep = tq // LANES

    def tile(kt, masked):
        ks = pl.ds(pl.multiple_of(kt * tq, tq), tq)
        if masked:
            mask = (lax.broadcasted_iota(jnp.int32, (tq, tq), 1)
                    <= lax.broadcasted_iota(jnp.int32, (tq, tq), 0))
        hs = range(MLA_HEADS)
        s = [_dot_t(q_ref[0, h], k_ref[0, h, ks, :]) for h in hs]
        if masked:
            s = [jnp.where(mask, x, NEG_INF) for x in s]
        m_prev = [m_ref[h] for h in hs]
        m_new = [jnp.maximum(mp, jnp.max(x, axis=-1, keepdims=True)) for mp, x in zip(m_prev, s)]
        alpha = [jnp.exp2(mp - mn) for mp, mn in zip(m_prev, m_new)]
        p = [jnp.exp2(x - jnp.tile(mn, (1, rep))).astype(BF16) for x, mn in zip(s, m_new)]
        pv = [_dot(p[h], v_ref[0, h, ks, :]) for h in hs]
        for h in hs:
            m_ref[h] = m_new[h]
        for pr in range(MLA_HEADS // 2):
            a = jnp.tile(jnp.where(lane < MLA_V, alpha[2 * pr], alpha[2 * pr + 1]), (1, 2))
            acc_ref[pr] = acc_ref[pr] * a + pv[2 * pr] + pv[2 * pr + 1]

    def body(kt, c):
        tile(kt, False)
        return c

    lax.fori_loop(0, qi, body, 0)
    tile(qi, True)
    for pr in range(MLA_HEADS // 2):
        o_ref[0, :, pr * LANES:(pr + 1) * LANES] = (acc_ref[pr, :, :LANES] / acc_ref[pr, :, LANES:]).astype(o_ref.dtype)


def _rope_tables(seq):
    pos = jnp.arange(seq, dtype=F32)
    inv = 1.0 / (ROPE_THETA ** (jnp.arange(0, MLA_ROPE, 2, dtype=F32) / MLA_ROPE))
    ang = pos[:, None] * inv[None, :]
    cos, sin = jnp.cos(ang), jnp.sin(ang)
    cos2 = jnp.concatenate([cos, cos], axis=-1)
    sin2 = jnp.concatenate([sin, sin], axis=-1)
    z64 = jnp.zeros((seq, MLA_NOPE), F32)
    z32 = jnp.zeros((seq, LANES - MLA_NOPE - MLA_ROPE), F32)
    c1 = jnp.concatenate([jnp.ones((seq, MLA_NOPE), F32), cos2, z32], axis=-1)
    c0 = jnp.concatenate([z64, cos2, z32], axis=-1)
    s0 = jnp.concatenate([z64, sin2, z32], axis=-1)
    return c1, c0, s0


def _rot_half_cols(w):
    half = MLA_ROPE // 2
    return jnp.concatenate([-w[..., half:], w[..., :half]], axis=-1)


def mla_mixer(proj3, q_norm, w_uq, kv_norm, w_ukv, tabs, *, tm, tq):
    bsz, seq, _ = proj3.shape
    H = MLA_HEADS
    scale = (MLA_NOPE + MLA_ROPE) ** -0.5 * LOG2E
    wq =(w_uq.astype(F32) * scale).reshape(MLA_Q_RANK, H, MLA_NOPE + MLA_ROPE)
    zq = jnp.zeros((MLA_Q_RANK, H, LANES - MLA_NOPE - MLA_ROPE), F32)
    z64 = jnp.zeros((MLA_Q_RANK, H, MLA_NOPE), F32)
    wqa = jnp.concatenate([wq, zq], axis=-1).reshape(MLA_Q_RANK, H * LANES)
    wqb = jnp.concatenate([z64, _rot_half_cols(wq[..., MLA_NOPE:]), zq], axis=-1).reshape(MLA_Q_RANK, H * LANES)
    padq = ((0, W_CQ - MLA_Q_RANK), (0, 0))
    wqa = jnp.pad(wqa, padq).astype(BF16)
    wqb = jnp.pad(wqb, padq).astype(BF16)
    gq = jnp.pad(q_norm.astype(F32), (0, W_CQ - MLA_Q_RANK)).reshape(1, W_CQ)
    wkv = w_ukv.astype(F32).reshape(MLA_KV_RANK, H, MLA_NOPE + MLA_V)
    zk = jnp.zeros((MLA_KV_RANK, H, MLA_NOPE), F32)
    wk = jnp.concatenate([wkv[..., :MLA_NOPE], zk], axis=-1).reshape(MLA_KV_RANK, H * LANES).astype(BF16)
    wv_h = wkv[..., MLA_NOPE:]
    even = (jnp.arange(H) % 2 == 0)[None, :, None]
    wv = jnp.concatenate([jnp.where(even, wv_h, 0.0), jnp.where(even, 0.0, wv_h)], axis=-1)
    wv = wv.reshape(MLA_KV_RANK, H * LANES).astype(BF16)
    c1, c0, s0 = tabs
    full2 = lambda shape: pl.BlockSpec(shape, lambda b, i: (0,) * len(shape))
    tab_spec = pl.BlockSpec((tm, LANES), lambda b, i: (i, 0))
    hd_spec = pl.BlockSpec((1, H, tm, LANES), lambda b, i: (b, 0, i, 0))
    hd_shape = jax.ShapeDtypeStruct((bsz, H, seq, LANES), BF16)
    v_spec = pl.BlockSpec((1, H, tm, 2 * LANES), lambda b, i: (b, 0, i, 0))
    v_shape = jax.ShapeDtypeStruct((bsz, H, seq, 2 * LANES), BF16)
    q, k, v = pl.pallas_call(
        _mla_prep_kernel,
        out_shape=(hd_shape, hd_shape, v_shape),
        grid=(bsz, seq // tm),
        in_specs=[pl.BlockSpec((1, tm, W_CQ), lambda b, i: (b, i, C_CQ // W_CQ)),
                  pl.BlockSpec((1, tm, W_KR), lambda b, i: (b, i, C_KR // W_KR)),
                  pl.BlockSpec((1, tm, MLA_KV_RANK), lambda b, i: (b, i, C_CKV // MLA_KV_RANK)),
                  full2((1, W_CQ)), full2((1, MLA_KV_RANK)),
                  full2((W_CQ, H * LANES)), full2((W_CQ, H * LANES)),
                  full2((MLA_KV_RANK, H * LANES)), full2((MLA_KV_RANK, H * LANES)),
                  tab_spec, tab_spec, tab_spec],
        out_specs=(hd_spec, hd_spec, v_spec),
        compiler_params=_cparams(("parallel", "parallel")),
        name="mla_prep",
    )(proj3, proj3, proj3, gq, kv_norm.astype(F32).reshape(1, MLA_KV_RANK), wqa, wqb, wk, wv, c1, c0, s0)

    return pl.pallas_call(
        functools.partial(_mla_flash_kernel, tq=tq),
        out_shape=jax.ShapeDtypeStruct((bsz, seq, H * MLA_V), BF16),
        grid=(bsz, seq // tq),
        in_specs=[pl.BlockSpec((1, H, tq, LANES), lambda b, i: (b, 0, i, 0)),
                  pl.BlockSpec((1, H, seq, LANES), lambda b, i: (b, 0, 0, 0)),
                  pl.BlockSpec((1, H, seq, 2 * LANES), lambda b, i: (b, 0, 0, 0))],
        out_specs=pl.BlockSpec((1, tq, H * MLA_V), lambda b, i: (b, i, 0)),
        scratch_shapes=[pltpu.VMEM((H, tq, LANES), F32), pltpu.VMEM((H // 2, tq, 2 * LANES), F32)],
        compiler_params=_cparams(("parallel", "arbitrary")),
        name="mla_flash",
    )(q, k, v)


def _nsa_cmp_kernel(x_ref, pea_ref, peb_ref, w1a_ref, w1b_ref, w2_ref, o_ref):
    nch = o_ref.shape[2]
    bias = _dot(pea_ref[0], w1a_ref[0])[0:1] + _dot(peb_ref[0], w1b_ref[0])[0:1]
    a = b = None
    for pos in range(CMP_STRIDE):
        xs = x_ref[0, pl.ds(pos, nch, stride=CMP_STRIDE), :].astype(BF16)
        rows = slice(pos * LANES, (pos + 1) * LANES)
        da, db = _dot(xs, w1a_ref[0, rows, :]), _dot(xs, w1b_ref[0, rows, :])
        a, b = (da, db) if a is None else (a + da, b + db)
    pre = a + pltpu.roll(b, nch - 1, 0) + bias
    o_ref[0, 0] = _dot(jax.nn.gelu(pre).astype(BF16), w2_ref[0]).astype(o_ref.dtype)


def nsa_compress(kv_cr, cmp_pe, cmp_w1, cmp_w2):
    bsz, seq, _ = kv_cr.shape
    nch, width = seq // CMP_STRIDE, CMP_STRIDE * LANES
    G, dh = NSA_KV_HEADS, NSA_DIM
    half = CMP_BLOCK // 2
    eye = jnp.eye(G, dtype=F32)
    w1r = cmp_w1.astype(F32).reshape(2, CMP_BLOCK, dh, dh)
    w1a = jnp.einsum('kpde,gh->kpgdhe', w1r[:, :half], eye).reshape(2, width, G * dh).astype(BF16)
    w1b = jnp.einsum('kpde,gh->kpgdhe', w1r[:, half:], eye).reshape(2, width, G * dh).astype(BF16)
    w2 = jnp.einsum('kde,gh->kgdhe', cmp_w2.astype(F32), eye).reshape(2, G * dh, G * dh).astype(BF16)
    pe = cmp_pe.astype(F32)
    pe_g = jnp.broadcast_to(pe[:, :, None, :], (2, CMP_BLOCK, G, dh))
    pea = jnp.broadcast_to(pe_g[:, :half].reshape(2, 1, width), (2, 8, width)).astype(BF16)
    peb = jnp.broadcast_to(pe_g[:, half:].reshape(2, 1, width), (2, 8, width)).astype(BF16)
    kvspec = lambda shape: pl.BlockSpec(shape, lambda b, k: (k,) + (0,) * (len(shape) - 1))
    return pl.pallas_call(
        _nsa_cmp_kernel,
        out_shape=jax.ShapeDtypeStruct((bsz, 2, nch, G * dh), BF16),
        grid=(bsz, 2),
        in_specs=[pl.BlockSpec((1, seq, LANES), lambda b, k: (b, 0, k)),
                  kvspec((1, 8, width)), kvspec((1, 8, width)),
                  kvspec((1, width, G * dh)), kvspec((1, width, G * dh)),
                  kvspec((1, G * dh, G * dh))],
        out_specs=pl.BlockSpec((1, 1, nch, G * dh), lambda b, k: (b, k, 0, 0)),
        compiler_params=_cparams(("parallel", "parallel")),
        name="nsa_compress",
    )(kv_cr, pea, peb, w1a, w1b, w2)


def _nsa_kernel(q_ref, gate_ref, ksl_ref, vsl_ref, kwn_ref, vwn_ref, kvc_ref, bc_ref, bw_ref,
                bs_ref, ov_ref, blk1h_ref, gexp_ref, o_ref, kaug_ref, vaug_ref, m_ref, acc_ref,
                *, tq, tk, nbs, n_sel):
    qi = pl.program_id(1)
    R, G = NSA_REP, NSA_KV_HEADS
    H = R * G
    nsub = tk // tq
    nwin = WINDOW // tq

    @pl.when(qi == 0)
    def _():
        kaug_ref[:, :LANES] = ksl_ref[0]
        kaug_ref[:, LANES:] = blk1h_ref[...]
        vaug_ref[:, :LANES] = vsl_ref[0]
        vaug_ref[:, LANES:] = jnp.ones((vaug_ref.shape[0], LANES), BF16)

    lane = lax.broadcasted_iota(jnp.int32, (tq, LANES), 1)
    t_row = qi * tq + lax.broadcasted_iota(jnp.int32, (H * tq, 1), 0) % tq
    kc = kvc_ref[0, 0]
    vc = kvc_ref[0, 1]
    ov = ov_ref[...]

    def stack(fn):
        return jnp.concatenate([fn(h) for h in range(H)], axis=0)

    q_all = stack(lambda h: q_ref[0, :, h * LANES:(h + 1) * LANES])

    c = jnp.minimum(qi, nwin)
    ws = pl.ds(pl.multiple_of(jnp.maximum(qi - nwin, 0) * tq, tq), (nwin + 1) * tq)
    k_w = kwn_ref[0, ws, :]
    v_w = vwn_ref[0, ws, :]
    o_w = []
    for g in range(G):
        rows = slice(g * R * tq, (g + 1) * R * tq)
        s_w = _dot_t(q_all[rows], k_w) + jnp.concatenate([bw_ref[g * R + r, c] for r in range(R)], axis=0)
        p_w = jnp.exp2(s_w - jnp.max(s_w, axis=-1, keepdims=True))
        o_w.append(_dot(p_w.astype(BF16), v_w) / jnp.sum(p_w, axis=-1, keepdims=True))
    o_w = jnp.concatenate(o_w, axis=0)

    halves = [slice(g * R * tq, (g + 1) * R * tq) for g in range(G)]
    valid = [t_row[hs] >= (CMP_BLOCK - 1) for hs in halves]
    s = [_dot_t(q_all[hs], kc) + jnp.concatenate([bc_ref[g * R + r] for r in range(R)], axis=0)
         for g, hs in enumerate(halves)]
    p = [jnp.where(v, jnp.exp2(x - jnp.max(x, axis=-1, keepdims=True)), 0.0) for v, x in zip(valid, s)]
    pc = [x / jnp.where(v, jnp.sum(x, axis=-1, keepdims=True), 1.0) for v, x in zip(valid, p)]
    o_c = jnp.concatenate([_dot(x.astype(BF16), vc) for x in pc], axis=0)
    blk = lax.broadcasted_iota(jnp.int32, (nbs, tq), 0)
    tl = qi * tq + lax.broadcasted_iota(jnp.int32, (nbs, tq), 1)
    cur = tl // SEL_BLOCK
    forced = (blk == 0) | (blk == cur) | (blk == cur - 1)
    future = blk * SEL_BLOCK > tl
    qmask = []
    for g in range(G):
        psum = pc[g][0:tq] + pc[g][tq:2 * tq] + pc[g][2 * tq:3 * tq]
        p_hi = psum.astype(BF16)
        p_lo = (psum - p_hi.astype(F32)).astype(BF16)
        imp = (_dot_t(ov, p_hi) + _dot_t(ov, p_lo))[:nbs]
        imp = jnp.where(forced, FORCE, jnp.where(future, -FORCE, imp))
        rank = jnp.zeros((nbs, tq), F32)
        for i in range(nbs):
            ri = imp[i:i + 1, :]
            beats = (ri > imp) | ((ri == imp) & (blk > i))
            rank = rank + jnp.where(beats, 1.0, 0.0)
        sel = jnp.where(rank < n_sel, 0.0, NEG_INF)
        sel = jnp.concatenate([sel, jnp.zeros((LANES - nbs, tq), F32)], axis=0).T.astype(BF16)
        qmask += [sel] * R
    q_aug = jnp.concatenate([q_all, jnp.concatenate(qmask, axis=0)], axis=1)

    m_ref[...] = jnp.full_like(m_ref, NEG_INF)
    acc_ref[...] = jnp.zeros_like(acc_ref)

    def sel_tile(kt, bias):
        ks = pl.ds(pl.multiple_of(kt * tk, tk), tk)
        k_t = kaug_ref[ks, :]
        v_t = vaug_ref[ks, :]
        s = [_dot_t(q_aug[hs], k_t) for hs in halves]
        if bias is not None:
            s = [x + bias[hs] for x, hs in zip(s, halves)]
        m_prev = [m_ref[hs, :] for hs in halves]
        m_new = [jnp.maximum(mp, jnp.max(x, axis=-1, keepdims=True)) for mp, x in zip(m_prev, s)]
        alpha = [jnp.exp2(mp - mn) for mp, mn in zip(m_prev, m_new)]
        p = [jnp.exp2(x - jnp.tile(mn, (1, tk // LANES))).astype(BF16) for x, mn in zip(s, m_new)]
        pv = [_dot(pp, v_t) for pp in p]
        for hs, mn, a, o in zip(halves, m_new, alpha, pv):
            m_ref[hs, :] = mn
            acc_ref[hs, :] = acc_ref[hs, :] * jnp.tile(a, (1, 2)) + o

    def near_bias(kt):
        cols = []
        for sub in range(nsub):
            d = qi - (kt * nsub + sub)
            cols.append(stack(lambda h: jnp.where(d == 0, bs_ref[h, 0], jnp.where(
                d == 1, bs_ref[h, 1], jnp.where(d < 0, NEG_INF, 0.0)))))
        return jnp.concatenate(cols, axis=1)

    def far_body(kt, c):
        sel_tile(kt, None)
        return c

    kd = (qi * tq) // tk
    lax.fori_loop(0, jnp.maximum(kd - 1, 0), far_body, 0)

    @pl.when(kd >= 1)
    def _():
        sel_tile(kd - 1, near_bias(kd - 1))

    sel_tile(kd, near_bias(kd))
    o_s = acc_ref[:, :LANES] / acc_ref[:, LANES:]

    gates = _sigmoid(_dot(gate_ref[0], gexp_ref[...]))
    for r in range(R):
        res = None
        for b, o_b in enumerate((o_c, o_s, o_w)):
            o_br = jnp.where(lane < NSA_DIM, o_b[r * tq:(r + 1) * tq], o_b[(R + r) * tq:(R + r + 1) * tq])
            term = gates[:, (b * R + r) * LANES:(b * R + r + 1) * LANES] * o_br
            res = term if res is None else res + term
        o_ref[0, :, r * LANES:(r + 1) * LANES] = res.astype(o_ref.dtype)


def _t5_bucket(dist):
    n = jnp.maximum(dist, 0)
    exact = REL_BUCKETS // 2
    nf = jnp.maximum(n, exact).astype(F32)
    large = exact + jnp.floor(jnp.log(nf / exact) / math.log(REL_MAX_DIST / exact)
                              * (REL_BUCKETS - exact)).astype(jnp.int32)
    return jnp.where(n < exact, n, jnp.minimum(large, REL_BUCKETS - 1))


def _nsa_bias_tables(rel_bias, seq, tq):
    rb = rel_bias.astype(F32).T
    far = rb[:, REL_BUCKETS - 1].reshape(NSA_HEADS, 1, 1)

    def by_dist(dist, ok, shift=0.0):
        bucket = _t5_bucket(dist)[None]
        out = jnp.zeros((NSA_HEADS,) + dist.shape, F32)
        for k in range(REL_BUCKETS):
            out = jnp.where(bucket == k, rb[:, k].reshape((NSA_HEADS,) + (1,) * dist.ndim), out)
        return jnp.where(ok[None], (out - shift) * LOG2E, NEG_INF)

    i = jnp.arange(tq)[:, None]
    t = jnp.arange(seq)[:, None]
    dist_c = t - (jnp.arange(LANES)[None, :] * CMP_STRIDE + CMP_BLOCK - 1)
    bc = by_dist(dist_c, dist_c >= 0)
    nwin = WINDOW // tq
    jw = jnp.arange((nwin + 1) * tq)[None, :]
    bw = jnp.stack([by_dist(tq * c + i - jw, (tq * c + i - jw >= 0) & (tq * c + i - jw < WINDOW))
                    for c in range(nwin + 1)], axis=1)
    js = jnp.arange(tq)[None, :]
    bs = jnp.stack([by_dist(tq * c + i - js, tq * c + i - js >= 0, far) for c in range(2)], axis=1)
    ci = jnp.arange(LANES)[:, None]
    sj = jnp.arange(LANES)[None, :]
    nbs = seq // SEL_BLOCK
    ov = ((ci * CMP_STRIDE <= sj * SEL_BLOCK + SEL_BLOCK - 1)
          & (ci * CMP_STRIDE + CMP_BLOCK - 1 >= sj * SEL_BLOCK)
          & (ci < seq // CMP_STRIDE - 1) & (sj < nbs))
    blk1h = (jnp.arange(seq)[:, None] // SEL_BLOCK == sj).astype(BF16)
    col = jnp.arange(3 * NSA_REP * LANES)
    slab, lane_g = col // LANES, (col % LANES) // NSA_DIM
    gate_col = (slab // NSA_REP) * NSA_HEADS + lane_g * NSA_REP + slab % NSA_REP
    gexp = (ci == gate_col[None, :]).astype(BF16)
    return bc, bw, bs, ov.T.astype(BF16), blk1h, gexp


def nsa_mixer(proj3, kvc, tables, *, tq, tk):
    bsz, seq, _ = proj3.shape
    assert tq % LANES == 0 and tq >= REL_MAX_DIST and tk % tq == 0 and seq % tk == 0
    assert seq // CMP_STRIDE == LANES and WINDOW % tq == 0
    nwin = WINDOW // tq
    bc, bw, bs, ov, blk1h, gexp = tables
    nbs = seq // SEL_BLOCK
    H = NSA_HEADS
    slab = lambda j: pl.BlockSpec((1, seq, LANES), lambda b, i: (b, 0, C_NKV // LANES + j))
    const = lambda shape: pl.BlockSpec(shape, lambda b, i: (0,) * len(shape))
    return pl.pallas_call(
        functools.partial(_nsa_kernel, tq=tq, tk=tk, nbs=nbs, n_sel=min(SEL_TOPN, nbs)),
        out_shape=jax.ShapeDtypeStruct((bsz, seq, H * NSA_DIM), BF16),
        grid=(bsz, seq // tq),
        in_specs=[pl.BlockSpec((1, tq, H * LANES), lambda b, i: (b, i, C_NQ // (H * LANES))),
                  pl.BlockSpec((1, tq, LANES), lambda b, i: (b, i, C_GATE // LANES)),
                  slab(2), slab(3), slab(4), slab(5),
                  pl.BlockSpec((1, 2, LANES, LANES), lambda b, i: (b, 0, 0, 0)),
                  pl.BlockSpec((H, tq, LANES), lambda b, i: (0, i, 0)),
                  const((H, nwin + 1, tq, (nwin + 1) * tq)), const((H, 2, tq, tq)),
                  const((LANES, LANES)), const((seq, LANES)), const(gexp.shape)],
        out_specs=pl.BlockSpec((1, tq, H * NSA_DIM), lambda b, i: (b, i, 0)),
        scratch_shapes=[pltpu.VMEM((seq, 2 * LANES), BF16), pltpu.VMEM((seq, 2 * LANES), BF16),
                        pltpu.VMEM((H * tq, LANES), F32), pltpu.VMEM((H * tq, 2 * LANES), F32)],
        compiler_params=_cparams(("parallel", "arbitrary")),
        name="nsa_attention",
    )(proj3, proj3, proj3, proj3, proj3, proj3, kvc, bc, bw, bs, ov, blk1h, gexp)


def _mix_xattn_kernel(ys_ref, ym_ref, yn_ref, h_ref, g1_ref, g2_ref, g3_ref, w1_ref, w2_ref, w3_ref,
                      kv_ref, g_ref, wq_ref, wo_ref, o_ref, *, dh):
    rows = h_ref.shape[1] // 2
    sl = [slice(k * rows, (k + 1) * rows) for k in range(2)]
    hw = XATTN_HEADS * dh
    heads = range(XATTN_HEADS)

    def normed(y_ref, gy_ref):
        return [_rms(y_ref[0, s, :].astype(F32), gy_ref[...], y_ref.shape[-1]).astype(BF16) for s in sl]

    ns, nm, nn = normed(ys_ref, g1_ref), normed(ym_ref, g2_ref), normed(yn_ref, g3_ref)
    h = [h_ref[0, s, :] + _dot(a, w1_ref[...]) + _dot(b, w2_ref[...]) + _dot(c, w3_ref[...])
         for s, a, b, c in zip(sl, ns, nm, nn)]
    xn = [_rms(x, g_ref[...], x.shape[-1]).astype(BF16) for x in h]
    q = [_dot(x, wq_ref[...]).astype(BF16) for x in xn]
    s = [[_dot_t(qk[:, hd * dh:(hd + 1) * dh], kv_ref[0, :, hd * dh:(hd + 1) * dh]) for hd in heads] for qk in q]
    p = [[jnp.exp2(x - jnp.max(x, axis=-1, keepdims=True)) for x in sk] for sk in s]
    p = [[(x / jnp.sum(x, axis=-1, keepdims=True)).astype(BF16) for x in pk] for pk in p]
    o = [jnp.concatenate([_dot(pk[hd], kv_ref[0, :, hw + hd * dh:hw + (hd + 1) * dh]).astype(BF16)
                          for hd in heads], axis=-1) for pk in p]
    for sk, hk, ok in zip(sl, h, o):
        o_ref[0, sk, :] = hk + _dot(ok, wo_ref[...])


def mix_out_cross_attention(ys3, h3, gains, weights, kv3, g_x, wq, wo, *, tm):
    bsz, seq, d = h3.shape
    m = kv3.shape[1]
    dh = d // XATTN_HEADS
    wq_s = (wq.astype(F32) * (dh ** -0.5 * LOG2E)).astype(BF16)
    gains = [g.reshape(1, -1).astype(F32) for g in gains]
    const = lambda shape: pl.BlockSpec(shape, lambda b, i: (0,) * len(shape))
    row = lambda w: pl.BlockSpec((1, tm, w), lambda b, i: (b, i, 0))
    return pl.pallas_call(
        functools.partial(_mix_xattn_kernel, dh=dh),
        out_shape=jax.ShapeDtypeStruct((bsz, seq, d), F32),
        grid=(bsz, seq // tm),
        in_specs=[row(y.shape[2]) for y in ys3] + [row(d)]
                 + [const(g.shape) for g in gains] + [const(w.shape) for w in weights]
                 + [pl.BlockSpec((1, m, 2 * d), lambda b, i: (b, 0, 0)),
                    const((1, d)), const((d, d)), const((d, d))],
        out_specs=row(d),
        compiler_params=_cparams(("parallel", "parallel")),
        name="mix_out_cross_attention",
    )(*ys3, h3, *gains, *weights, kv3, g_x.reshape(1, d).astype(F32), wq_s, wo.astype(BF16))


def _ffn_kernel(h_ref, g_ref, wg_ref, wu_ref, wd_ref, o_ref, xn_ref, acc_ref):
    j = pl.program_id(1)

    @pl.when(j == 0)
    def _():
        h = h_ref[...]
        xn_ref[...] = _rms(h, g_ref[...], h.shape[-1]).astype(BF16)
        acc_ref[...] = h

    _swiglu_accumulate(xn_ref, wg_ref[...], wu_ref[...], wd_ref[...], acc_ref)

    @pl.when(j == pl.num_programs(1) - 1)
    def _():
        o_ref[...] = acc_ref[...]


def dense_ffn(h, g, wg, wu, wd, *, tm, tf):
    m, d = h.shape
    ff = wg.shape[1]
    return pl.pallas_call(
        _ffn_kernel,
        out_shape=jax.ShapeDtypeStruct((m, d), F32),
        grid=(m // tm, ff // tf),
        in_specs=[pl.BlockSpec((tm, d), lambda i, j: (i, 0)),
                  pl.BlockSpec((1, d), lambda i, j: (0, 0)),
                  pl.BlockSpec((d, tf), lambda i, j: (0, j)),
                  pl.BlockSpec((d, tf), lambda i, j: (0, j)),
                  pl.BlockSpec((tf, d), lambda i, j: (j, 0))],
        out_specs=pl.BlockSpec((tm, d), lambda i, j: (i, 0)),
        scratch_shapes=[pltpu.VMEM((tm, d), BF16), pltpu.VMEM((tm, d), F32)],
        compiler_params=_cparams(("parallel", "arbitrary")),
        name="dense_ffn",
    )(h, g.reshape(1, d).astype(F32), wg.astype(BF16), wu.astype(BF16), wd.astype(BF16))


def _router_kernel(h_ref, g_ref, wr_hi_ref, wr_lo_ref, xn_ref, info_ref, cnt_ref, carry_ref, *, tm):
    i = pl.program_id(0)

    @pl.when(i == 0)
    def _():
        carry_ref[...] = jnp.zeros_like(carry_ref)

    h = h_ref[...]
    xn = _rms(h, g_ref[...], h.shape[-1])
    xn_ref[...] = xn
    x_hi = xn.astype(BF16)
    x_lo = (xn - x_hi.astype(F32)).astype(BF16)
    logits = _dot(x_hi, wr_hi_ref[...]) + _dot(x_lo, wr_hi_ref[...]) + _dot(x_hi, wr_lo_ref[...])
    lane = lax.broadcasted_iota(jnp.int32, (tm, LANES), 1)
    lanef = lane.astype(F32)
    logits = jnp.where(lane < N_EXPERTS, logits, NEG_INF)
    m1 = jnp.max(logits, axis=-1, keepdims=True)
    i1 = jnp.min(jnp.where(logits == m1, lanef, float(LANES)), axis=-1, keepdims=True)
    rest = jnp.where(lanef == i1, NEG_INF, logits)
    m2 = jnp.max(rest, axis=-1, keepdims=True)
    i2 = jnp.min(jnp.where(rest == m2, lanef, float(LANES)), axis=-1, keepdims=True)
    e2 = jnp.exp(m2 - m1)
    w1 = 1.0 / (1.0 + e2)
    w2 = e2 / (1.0 + e2)
    oh1 = lanef == i1
    oh2 = lanef == i2
    oh = jnp.where(oh1 | oh2, 1.0, 0.0)
    rr = lax.broadcasted_iota(jnp.int32, (tm, tm), 0)
    cc = lax.broadcasted_iota(jnp.int32, (tm, tm), 1)
    tri = jnp.where(cc < rr, 1.0, 0.0).astype(BF16)
    before = _dot(tri, oh.astype(BF16)) + carry_ref[0:1, :]
    r1 = jnp.sum(jnp.where(oh1, before, 0.0), axis=-1, keepdims=True)
    r2 = jnp.sum(jnp.where(oh2, before, 0.0), axis=-1, keepdims=True)
    carry_ref[...] = carry_ref[...] + jnp.sum(oh, axis=0, keepdims=True)
    info = jnp.where(lane == 0, i1, jnp.where(lane == 1, i2, jnp.where(lane == 2, w1, jnp.where(
        lane == 3, w2, jnp.where(lane == 4, r1, jnp.where(lane == 5, r2, 0.0))))))
    info_ref[...] = info
    cnt_ref[...] = carry_ref[...]


def moe_router(h, g, router, *, tm):
    m, d = h.shape
    wr = jnp.pad(router.astype(F32), ((0, 0), (0, LANES - N_EXPERTS)))
    wr_hi = wr.astype(BF16)
    wr_lo = (wr - wr_hi.astype(F32)).astype(BF16)
    return pl.pallas_call(
        functools.partial(_router_kernel, tm=tm),
        out_shape=(jax.ShapeDtypeStruct((m, d), F32), jax.ShapeDtypeStruct((m, LANES), F32),
                   jax.ShapeDtypeStruct((8, LANES), F32)),
        grid=(m // tm,),
        in_specs=[pl.BlockSpec((tm, d), lambda i: (i, 0)),
                  pl.BlockSpec((1, d), lambda i: (0, 0)),
                  pl.BlockSpec((d, LANES), lambda i: (0, 0)),
                  pl.BlockSpec((d, LANES), lambda i: (0, 0))],
        out_specs=(pl.BlockSpec((tm, d), lambda i: (i, 0)),
                   pl.BlockSpec((tm, LANES), lambda i: (i, 0)),
                   pl.BlockSpec((8, LANES), lambda i: (0, 0))),
        scratch_shapes=[pltpu.VMEM((8, LANES), F32)],
        compiler_params=_cparams(("arbitrary",)),
        name="moe_router",
    )(h, g.reshape(1, d).astype(F32), wr_hi, wr_lo)


def _row_copy(src_hbm, row, dst, slot, sem):
    return pltpu.make_async_copy(src_hbm.at[pl.ds(row, 1), :], dst.at[pl.ds(slot, 1), :], sem)


def _rows_wait(src_hbm, dst, sem):
    pltpu.make_async_copy(src_hbm.at[pl.ds(0, dst.shape[0]), :], dst, sem).wait()


def _moe_ffn_kernel(src_ref, texp_ref, nact_ref, x_hbm, wg_ref, wu_ref, wd_ref, o_ref,
                    xbuf, xbf, acc_ref, sem, *, tm, nj):
    i = pl.program_id(0)
    j = pl.program_id(1)
    nact = nact_ref[0]
    active = i < nact
    nbuf = xbuf.shape[0]
    ahead = nbuf - 1
    cur = i % nbuf
    rows_per_step = tm // nj

    for t0 in range(ahead):
        @pl.when((i == 0) & (j == 0) & (t0 < nact))
        def _():
            def issue(s, c):
                _row_copy(x_hbm, src_ref[t0 * tm + s], xbuf.at[t0], s, sem.at[t0]).start()
                return c

            lax.fori_loop(0, tm, issue, 0, unroll=8)

    @pl.when(active & (j == 0))
    def _():
        _rows_wait(x_hbm, xbuf.at[cur], sem.at[cur])
        xbf[...] = xbuf[cur].astype(BF16)
        acc_ref[...] = jnp.zeros_like(acc_ref)

    def compute(prefetch):
        if prefetch:
            nxt = (i + ahead) % nbuf
            base = (i + ahead) * tm + j * rows_per_step
            for k in range(rows_per_step):
                _row_copy(x_hbm, src_ref[base + k], xbuf.at[nxt], j * rows_per_step + k, sem.at[nxt]).start()
        _swiglu_accumulate(xbf, wg_ref[0], wu_ref[0], wd_ref[0], acc_ref)

    @pl.when(i + ahead < nact)
    def _():
        compute(True)

    @pl.when(active & (i + ahead >= nact))
    def _():
        compute(False)

    @pl.when(j == nj - 1)
    def _():
        o_ref[...] = jnp.where(active, acc_ref[...], 0.0)


def moe_expert_ffn(xn, src, tile_expert, n_active, wg, wu, wd, *, tm, tf):
    n_slots = src.shape[0]
    d = xn.shape[1]
    ne, _, ff = wg.shape
    nj = ff // tf
    assert nj * tf == ff and tm % nj == 0
    wg, wu, wd = wg.astype(BF16), wu.astype(BF16), wd.astype(BF16)

    def wmap_col(i, j, src, texp, nact):
        return (texp[i], 0, jnp.where(i < nact[0], j, nj - 1))

    def wmap_row(i, j, src, texp, nact):
        return (texp[i], jnp.where(i < nact[0], j, nj - 1), 0)

    return pl.pallas_call(
        functools.partial(_moe_ffn_kernel, tm=tm, nj=nj),
        out_shape=jax.ShapeDtypeStruct((n_slots, d), F32),
        grid_spec=pltpu.PrefetchScalarGridSpec(
            num_scalar_prefetch=3,
            grid=(n_slots // tm, nj),
            in_specs=[pl.BlockSpec(memory_space=pl.ANY),
                      pl.BlockSpec((1, d, tf), wmap_col),
                      pl.BlockSpec((1, d, tf), wmap_col),
                      pl.BlockSpec((1, tf, d), wmap_row)],
            out_specs=pl.BlockSpec((tm, d), lambda i, j, *_: (i, 0)),
            scratch_shapes=[pltpu.VMEM((3, tm, d), F32), pltpu.VMEM((tm, d), BF16),
                            pltpu.VMEM((tm, d), F32), pltpu.SemaphoreType.DMA((3,))]),
        compiler_params=_cparams(("arbitrary", "arbitrary")),
        name="moe_expert_ffn",
    )(src, tile_expert, n_active, xn, wg, wu, wd)


def _moe_combine_kernel(pos_ref, h_ref, info_ref, ys_hbm, g_ref, o_ref, buf, sem, *, tm, final_norm):
    i = pl.program_id(0)
    n = pl.num_programs(0)
    cur = i % 2

    def start_gather(tile, b):
        for s in range(tm):
            for k in range(2):
                _row_copy(ys_hbm, pos_ref[2 * (tile * tm + s) + k], buf.at[b, k], s, sem.at[b]).start()

    @pl.when(i == 0)
    def _():
        start_gather(0, 0)

    @pl.when(i + 1 < n)
    def _():
        start_gather(i + 1, 1 - cur)

    for k in range(2):
        _rows_wait(ys_hbm, buf.at[cur, k], sem.at[cur])
    info = info_ref[...]
    y = h_ref[...] + info[:, 2:3] * buf[cur, 0] + info[:, 3:4] * buf[cur, 1]
    if final_norm:
        y = _rms(y, g_ref[...], y.shape[-1])
    o_ref[...] = y


def moe_combine(h, info, ys, pos_flat, g_final, *, tm, final_norm):
    m, d = h.shape
    return pl.pallas_call(
        functools.partial(_moe_combine_kernel, tm=tm, final_norm=final_norm),
        out_shape=jax.ShapeDtypeStruct((m, d), F32),
        grid_spec=pltpu.PrefetchScalarGridSpec(
            num_scalar_prefetch=1,
            grid=(m // tm,),
            in_specs=[pl.BlockSpec((tm, d), lambda i, *_: (i, 0)),
                      pl.BlockSpec((tm, LANES), lambda i, *_: (i, 0)),
                      pl.BlockSpec(memory_space=pl.ANY),
                      pl.BlockSpec((1, d), lambda i, *_: (0, 0))],
            out_specs=pl.BlockSpec((tm, d), lambda i, *_: (i, 0)),
            scratch_shapes=[pltpu.VMEM((2, 2, tm, d), F32), pltpu.SemaphoreType.DMA((2,))]),
        compiler_params=_cparams(("arbitrary",)),
        name="moe_combine",
    )(pos_flat, h, info, ys, g_final.reshape(1, d).astype(F32))


def moe_layer(h, g, router, wg, wu, wd, g_final, *, final_norm, tm_r=512, tm_g=512, tf=1792, tm_c=512):
    m, d = h.shape
    xn, info, cnt = moe_router(h, g, router, tm=tm_r)
    e_idx = info[:, 0:2].astype(jnp.int32)
    rank = info[:, 4:6].astype(jnp.int32)
    counts = cnt[0, :N_EXPERTS].astype(jnp.int32)
    tiles_per = (counts + tm_g - 1) // tm_g
    tile_end = jnp.cumsum(tiles_per)
    seg_start = (tile_end - tiles_per) * tm_g
    pos = rank
    for e in range(N_EXPERTS):
        pos = pos + jnp.where(e_idx == e, seg_start[e], 0)
    n_tiles = (2 * m) // tm_g + N_EXPERTS
    n_slots = n_tiles * tm_g
    tok = jnp.broadcast_to(jnp.arange(m, dtype=jnp.int32)[:, None], (m, 2))
    src = jnp.zeros((n_slots,), jnp.int32).at[pos.reshape(-1)].set(tok.reshape(-1))
    n_active = tile_end[-1:].astype(jnp.int32)
    tile_ids = jnp.minimum(jnp.arange(n_tiles, dtype=jnp.int32), n_active[0] - 1)
    tile_expert = jnp.sum(tile_ids[:, None] >= tile_end[None, :], axis=1).astype(jnp.int32)
    ys = moe_expert_ffn(xn, src, tile_expert, n_active,
                        wg, wu, wd, tm=tm_g, tf=tf)
    return moe_combine(h, info, ys, pos.reshape(-1).astype(jnp.int32), g_final, tm=tm_c, final_norm=final_norm)


def _final_norm_kernel(h_ref, g_ref, o_ref):
    h = h_ref[...]
    o_ref[...] = _rms(h, g_ref[...], h.shape[-1])


def final_rmsnorm(h, g, *, tm):
    m, d = h.shape
    return pl.pallas_call(
        _final_norm_kernel,
        out_shape=jax.ShapeDtypeStruct((m, d), F32),
        grid=(m // tm,),
        in_specs=[pl.BlockSpec((tm, d), lambda i: (i, 0)), pl.BlockSpec((1, d), lambda i: (0, 0))],
        out_specs=pl.BlockSpec((tm, d), lambda i: (i, 0)),
        compiler_params=_cparams(("parallel",)),
        name="final_rmsnorm",
    )(h, g.reshape(1, d).astype(F32))


def _pack_w_in(w):
    d = w.shape[0]
    w = w.astype(F32)
    o = 0
    u = w[:, o:o + SSM_WIDTH]; o += SSM_WIDTH
    cq = w[:, o:o + MLA_Q_RANK]; o += MLA_Q_RANK
    ckv = w[:, o:o + MLA_KV_RANK]; o += MLA_KV_RANK
    kr = w[:, o:o + MLA_ROPE]; o += MLA_ROPE
    nq = w[:, o:o + NSA_HEADS * NSA_DIM]; o += NSA_HEADS * NSA_DIM
    nkv = w[:, o:o + 6 * NSA_KV_HEADS * NSA_DIM]; o += 6 * NSA_KV_HEADS * NSA_DIM
    gate = w[:, o:o + 3 * NSA_HEADS]
    z = lambda n: jnp.zeros((d, n), F32)
    kr_a = jnp.concatenate([z(MLA_NOPE), kr, z(LANES - MLA_NOPE - MLA_ROPE)], axis=1)
    kr_b = jnp.concatenate([z(MLA_NOPE), _rot_half_cols(kr), z(LANES - MLA_NOPE - MLA_ROPE)], axis=1)
    nq_h = (nq * (NSA_DIM ** -0.5 * LOG2E)).reshape(d, NSA_KV_HEADS, NSA_REP, NSA_DIM)
    zq = jnp.zeros((d, NSA_REP, NSA_DIM), F32)
    nq_p = jnp.concatenate([
        jnp.concatenate([nq_h[:, 0], zq], axis=-1).reshape(d, NSA_REP * LANES),
        jnp.concatenate([zq, nq_h[:, 1]], axis=-1).reshape(d, NSA_REP * LANES)], axis=1)
    packed = jnp.concatenate([u, cq, z(W_CQ - MLA_Q_RANK), kr_a, kr_b, nq_p, nkv, ckv,
                              gate, z(LANES - 3 * NSA_HEADS)], axis=1)
    assert packed.shape[1] == IN_COLS_PACKED
    return packed.astype(BF16)


def _rg_order(a):
    rest = a.shape[1:]
    return a.reshape((NSA_KV_HEADS, NSA_REP, NSA_DIM) + rest).swapaxes(0, 1).reshape((-1,) + rest)


def kernel(x, mem, w_in, w_out, mix_norm, out_norm, ssm_a_re, ssm_a_im, ssm_b_re, ssm_b_im, ssm_c_re, ssm_c_im, ssm_d, ssm_log_dt, ssm_w_glu, mla_q_norm, mla_w_uq, mla_kv_norm, mla_w_ukv, nsa_cmp_pe, nsa_cmp_w1, nsa_cmp_w2, rel_bias, xattn_norm, mem_norm, xattn_wq, xattn_wkv, xattn_wo, ffn_norm, dense_w_gate, dense_w_up, dense_w_down, moe_router, moe_w_gate, moe_w_up, moe_w_down, final_norm):
    bsz, seq, d = x.shape
    depth = w_in.shape[0]
    T = bsz * seq
    nmem = mem.shape[1]
    tq_nsa, tk_nsa = 2 * LANES, 4 * LANES
    rope_tabs = _rope_tables(seq)
    nsa_tabs = _nsa_bias_tables(rel_bias, seq, tq_nsa)
    o1, o2 = SSM_WIDTH, SSM_WIDTH + MLA_HEADS * MLA_V
    mem2 = mem.reshape(bsz * nmem, d)
    h = x.reshape(T, d)
    for l in range(depth):
        proj, kv_cr = norm_matmul(h, mix_norm[l], _pack_w_in(w_in[l]), tm=1024, out_dtype=BF16,
                                  f32_cols=(C_NKV, 2 * LANES))
        proj3 = proj.reshape(bsz, seq, IN_COLS_PACKED)
        u_tm = proj3[:, :, C_U:C_U + SSM_WIDTH].transpose(1, 0, 2).reshape(seq * bsz, SSM_WIDTH)
        y_ssm = ssm_mixer(u_tm, ssm_a_re[l], ssm_a_im[l], ssm_b_re[l], ssm_b_im[l], ssm_c_re[l], ssm_c_im[l],
                          ssm_d[l], ssm_log_dt[l], ssm_w_glu[l], nb=bsz, tc=128)
        y_ssm = y_ssm.reshape(seq, bsz, SSM_WIDTH).transpose(1, 0, 2)
        y_mla = mla_mixer(proj3, mla_q_norm[l], mla_w_uq[l], mla_kv_norm[l], mla_w_ukv[l], rope_tabs,
                          tm=512, tq=512)
        kvc = nsa_compress(kv_cr.reshape(bsz, seq, 2 * LANES), nsa_cmp_pe[l], nsa_cmp_w1[l], nsa_cmp_w2[l])
        y_nsa = nsa_mixer(proj3, kvc, nsa_tabs, tq=tq_nsa, tk=tk_nsa)
        g_out = out_norm[l]
        wo_l = w_out[l]
        kv_mem = norm_matmul(mem2, mem_norm[l], xattn_wkv[l].astype(BF16), tm=256, out_dtype=BF16)
        h = mix_out_cross_attention(
            [y_ssm, y_mla, y_nsa], h.reshape(bsz, seq, d),
            [g_out[:o1], g_out[o1:o2], _rg_order(g_out[o2:])],
            [wo_l[:o1].astype(BF16), wo_l[o1:o2].astype(BF16), _rg_order(wo_l[o2:]).astype(BF16)],
            kv_mem.reshape(bsz, nmem, 2 * d), xattn_norm[l], xattn_wq[l], xattn_wo[l], tm=1024).reshape(T, d)
        last = l == depth - 1
        if l % 2 == 0:
            h = dense_ffn(h, ffn_norm[l], dense_w_gate[l // 2], dense_w_up[l // 2], dense_w_down[l // 2],
                          tm=1024, tf=1408)
            if last:
                h = final_rmsnorm(h, final_norm, tm=512)
        else:
            h = moe_layer(h, ffn_norm[l], moe_router[l // 2], moe_w_gate[l // 2], moe_w_up[l // 2],
                          moe_w_down[l // 2], final_norm, final_norm=last)
    return h.reshape(bsz, seq, d)
```
